```python
import math
import jax, jax.numpy as jnp
from jax import lax
import numpy as np

D_MODEL = 1024
BATCH = 8
SEQ = 4096
DEPTH = 1

HEAD_DIM = 64
N_ATTN_HEADS = 8
N_GMLP_GROUPS = 8
D_ATTN = N_ATTN_HEADS * HEAD_DIM
D_GMLP = N_GMLP_GROUPS * HEAD_DIM
D_MIX = D_ATTN + D_GMLP
D_IN = 3 * D_ATTN + 2 * D_GMLP
DILATIONS = ((128, 1), (512, 4), (2048, 16))
ROPE_THETA = 500000.0
ROPE_DIM = HEAD_DIM // 4
CHUNK = 128
D_FF = 2816
CONV_WIDTH = 3
D_PLE = 256
LN_EPS = 1e-5
ALPHA = (2.0 * DEPTH) ** 0.25
BETA = (8.0 * DEPTH) ** -0.25
NEG_INF = -1e30

kernel_name = "hybrid_dilated_attn_gmlp_deepnorm_layer"


def _layernorm(x, g, b):
    xf = x.astype(jnp.float32)
    mu = jnp.mean(xf, axis=-1, keepdims=True)
    var = jnp.mean(jnp.square(xf - mu), axis=-1, keepdims=True)
    y = (xf - mu) * lax.rsqrt(var + LN_EPS)
    return (y * g.astype(jnp.float32) + b.astype(jnp.float32)).astype(x.dtype)


def _partial_rope(t, positions):
    inv = ROPE_THETA ** (-jnp.arange(0, ROPE_DIM, 2, dtype=jnp.float32) / ROPE_DIM)
    ang = positions.astype(jnp.float32)[..., None] * inv
    cos = jnp.cos(ang)[:, :, None, :]
    sin = jnp.sin(ang)[:, :, None, :]
    half = ROPE_DIM // 2
    x1 = t[..., :half]
    x2 = t[..., half:ROPE_DIM]
    rot = jnp.concatenate([x1 * cos - x2 * sin, x2 * cos + x1 * sin], axis=-1)
    return jnp.concatenate([rot, t[..., ROPE_DIM:]], axis=-1)


def _dilated_branch(q, k, v, window, dilation):
    b, s, h, dh = q.shape
    d = dilation
    w = window // d
    L = s // d
    nb = -(-L // w)
    Lp = nb * w

    def sub(t):
        t = t.reshape(b, L, d, h, dh)
        return jnp.pad(t, ((0, 0), (0, Lp - L), (0, 0), (0, 0), (0, 0)))

    def band(t):
        tp = jnp.pad(t, ((0, 0), (w, 0), (0, 0), (0, 0), (0, 0)))
        prev = tp[:, :Lp].reshape(b, nb, w, d, h, dh)
        cur = t.reshape(b, nb, w, d, h, dh)
        return jnp.concatenate([prev, cur], axis=2)

    qb = sub(q).reshape(b, nb, w, d, h, dh)
    kb = band(sub(k))
    vb = band(sub(v))
    scores = jnp.einsum('bnqrhd,bnkrhd->bnrhqk', qb, kb) * (1.0 / math.sqrt(dh))
    qi = jnp.arange(nb)[:, None, None] * w + jnp.arange(w)[None, :, None]
    kj = jnp.arange(nb)[:, None, None] * w - w + jnp.arange(2 * w)[None, None, :]
    dist = qi - kj
    mask = (dist >= 0) & (dist <= w) & (kj >= 0)
    scores = jnp.where(mask[None, :, None, None], scores, NEG_INF)
    m = jnp.max(scores, axis=-1, keepdims=True)
    e = jnp.exp(scores - m)
    den = jnp.sum(e, axis=-1, keepdims=True)
    out = jnp.einsum('bnrhqk,bnkrhd->bnqrhd', e / den, vb)
    lse = jnp.transpose((m + jnp.log(den))[..., 0], (0, 1, 4, 2, 3))
    out = out.reshape(b, Lp, d, h, dh)[:, :L].reshape(b, s, h, dh)
    lse = lse.reshape(b, Lp, d, h)[:, :L].reshape(b, s, h)
    return out, lse


def _dilated_attention(q, k, v):
    outs, lses = [], []
    for window, dilation in DILATIONS:
        o, l = _dilated_branch(q, k, v, window, dilation)
        outs.append(o)
        lses.append(l)
    wts = jax.nn.softmax(jnp.stack(lses, axis=0), axis=0)
    return jnp.einsum('cbsh,cbshd->bshd', wts, jnp.stack(outs, axis=0))


def _chunked_gmlp(u, z, ln_z_g, ln_z_b, w_s, b_s):
    b, s, _ = z.shape
    zn = _layernorm(z, ln_z_g, ln_z_b)
    zc = zn.reshape(b, s // CHUNK, CHUNK, N_GMLP_GROUPS, HEAD_DIM)
    causal = jnp.tril(jnp.ones((CHUNK, CHUNK), dtype=w_s.dtype))
    mixed = jnp.einsum('gij,bcjgd->bcigd', w_s * causal, zc) + jnp.transpose(b_s)[None, None, :, :, None]
    return u * mixed.reshape(b, s, D_GMLP)


def _causal_dwconv(a, w, bias):
    s = a.shape[1]
    ap = jnp.pad(a, ((0, 0), (CONV_WIDTH - 1, 0), (0, 0)))
    out = bias
    for kk in range(CONV_WIDTH):
        out = out + w[kk] * ap[:, kk:kk + s]
    return out


def _fwd_setup_inputs(seed: int = 0) -> dict:
    key = jax.random.key(seed)
    ks = jax.random.split(key, 24)
    nrm = lambda k, shape, scale: jax.random.normal(k, shape, dtype=jnp.float32) * scale
    gain = lambda k, n: 1.0 + nrm(k, (DEPTH, n), 0.01)
    bias = lambda k, n: nrm(k, (DEPTH, n), 0.01)
    x = nrm(ks[0], (BATCH, SEQ, D_MODEL), 1.0)
    p = nrm(ks[1], (DEPTH, BATCH, SEQ, D_PLE), 1.0)
    offs = jax.random.randint(ks[2], (BATCH, 1), 0, 1024, dtype=jnp.int32)
    positions = offs + jnp.arange(SEQ, dtype=jnp.int32)[None, :]
    w_in = nrm(ks[3], (DEPTH, D_MODEL, D_IN), D_MODEL ** -0.5)
    col_scale = jnp.concatenate([jnp.ones((2 * D_ATTN,), jnp.float32),
                                 jnp.full((D_ATTN,), BETA, jnp.float32),
                                 jnp.ones((2 * D_GMLP,), jnp.float32)])
    w_in = w_in * col_scale
    return {
        "x": x,
        "p": p,
        "positions": positions,
        "w_in": w_in,
        "ln_z_g": gain(ks[4], D_GMLP),
        "ln_z_b": bias(ks[5], D_GMLP),
        "w_s": nrm(ks[6], (DEPTH, N_GMLP_GROUPS, CHUNK, CHUNK), CHUNK ** -0.5),
        "b_s": 1.0 + nrm(ks[7], (DEPTH, N_GMLP_GROUPS, CHUNK), 0.01),
        "w_o": nrm(ks[8], (DEPTH, D_MIX, D_MODEL), BETA * D_MIX ** -0.5),
        "ln1_g": gain(ks[9], D_MODEL),
        "ln1_b": bias(ks[10], D_MODEL),
        "w_ff_a": nrm(ks[11], (DEPTH, D_MODEL, D_FF), BETA * D_MODEL ** -0.5),
        "w_ff_b": nrm(ks[12], (DEPTH, D_MODEL, D_FF), BETA * D_MODEL ** -0.5),
        "conv_w": nrm(ks[13], (DEPTH, CONV_WIDTH, D_FF), CONV_WIDTH ** -0.5),
        "conv_b": bias(ks[14], D_FF),
        "w_ff_down": nrm(ks[15], (DEPTH, D_FF, D_MODEL), BETA * D_FF ** -0.5),
        "ln2_g": gain(ks[16], D_MODEL),
        "ln2_b": bias(ks[17], D_MODEL),
        "w_ple_gate": nrm(ks[18], (DEPTH, D_MODEL, D_MODEL), D_MODEL ** -0.5),
        "b_ple_gate": bias(ks[19], D_MODEL),
        "w_ple_in": nrm(ks[20], (DEPTH, D_PLE, D_MODEL), BETA * D_PLE ** -0.5),
        "ln3_g": gain(ks[21], D_MODEL),
        "ln3_b": bias(ks[22], D_MODEL),
    }


def _fwd_reference(x, p, positions, w_in, ln_z_g, ln_z_b, w_s, b_s, w_o, ln1_g, ln1_b,
              w_ff_a, w_ff_b, conv_w, conv_b, w_ff_down, ln2_g, ln2_b,
              w_ple_gate, b_ple_gate, w_ple_in, ln3_g, ln3_b):
    b, s, _ = x.shape
    for i in range(DEPTH):
        h = x @ w_in[i]
        q = h[..., :D_ATTN].reshape(b, s, N_ATTN_HEADS, HEAD_DIM)
        k = h[..., D_ATTN:2 * D_ATTN].reshape(b, s, N_ATTN_HEADS, HEAD_DIM)
        v = h[..., 2 * D_ATTN:3 * D_ATTN].reshape(b, s, N_ATTN_HEADS, HEAD_DIM)
        u = jax.nn.gelu(h[..., 3 * D_ATTN:3 * D_ATTN + D_GMLP], approximate=False)
        z = jax.nn.gelu(h[..., 3 * D_ATTN + D_GMLP:], approximate=False)
        q = _partial_rope(q, positions).astype(jnp.float32)
        k = _partial_rope(k, positions).astype(jnp.float32)
        attn = _dilated_attention(q, k, v.astype(jnp.float32)).astype(x.dtype).reshape(b, s, D_ATTN)
        gm = _chunked_gmlp(u, z, ln_z_g[i], ln_z_b[i], w_s[i], b_s[i])
        mix = jnp.concatenate([attn, gm], axis=-1) @ w_o[i]
        x = _layernorm(ALPHA * x + mix, ln1_g[i], ln1_b[i])
        a = _causal_dwconv(x @ w_ff_a[i], conv_w[i], conv_b[i])
        ff = (jax.nn.gelu(a, approximate=False) * (x @ w_ff_b[i])) @ w_ff_down[i]
        x = _layernorm(ALPHA * x + ff, ln2_g[i], ln2_b[i])
        gate = jax.nn.sigmoid(x @ w_ple_gate[i] + b_ple_gate[i])
        ple = gate * (p[i] @ w_ple_in[i])
        x = _layernorm(ALPHA * x + ple, ln3_g[i], ln3_b[i])
    return x


import jax as _jax
import jax.numpy as _jnp

TWIN_FORMAT = 'train_step'
FWD_PARAMS = ['x', 'p', 'positions', 'w_in', 'ln_z_g', 'ln_z_b', 'w_s', 'b_s', 'w_o', 'ln1_g', 'ln1_b', 'w_ff_a', 'w_ff_b', 'conv_w', 'conv_b', 'w_ff_down', 'ln2_g', 'ln2_b', 'w_ple_gate', 'b_ple_gate', 'w_ple_in', 'ln3_g', 'ln3_b']
TWIN_WEIGHTS = ['w_in', 'ln_z_g', 'ln_z_b', 'w_s', 'b_s', 'w_o', 'ln1_g', 'ln1_b', 'w_ff_a', 'w_ff_b', 'conv_w', 'conv_b', 'w_ff_down', 'ln2_g', 'ln2_b', 'w_ple_gate', 'b_ple_gate', 'w_ple_in', 'ln3_g', 'ln3_b']
TWIN_DIFF_INPUT = 'x'
TWIN_INPUTS = ['x', 'p', 'positions', 'w_in', 'ln_z_g', 'ln_z_b', 'w_s', 'b_s', 'w_o', 'ln1_g', 'ln1_b', 'w_ff_a', 'w_ff_b', 'conv_w', 'conv_b', 'w_ff_down', 'ln2_g', 'ln2_b', 'w_ple_gate', 'b_ple_gate', 'w_ple_in', 'ln3_g', 'ln3_b', 'loss_target', 'm_w_in', 'm_ln_z_g', 'm_ln_z_b', 'm_w_s', 'm_b_s', 'm_w_o', 'm_ln1_g', 'm_ln1_b', 'm_w_ff_a', 'm_w_ff_b', 'm_conv_w', 'm_conv_b', 'm_w_ff_down', 'm_ln2_g', 'm_ln2_b', 'm_w_ple_gate', 'm_b_ple_gate', 'm_w_ple_in', 'm_ln3_g', 'm_ln3_b', 'v_w_in', 'v_ln_z_g', 'v_ln_z_b', 'v_w_s', 'v_b_s', 'v_w_o', 'v_ln1_g', 'v_ln1_b', 'v_w_ff_a', 'v_w_ff_b', 'v_conv_w', 'v_conv_b', 'v_w_ff_down', 'v_ln2_g', 'v_ln2_b', 'v_w_ple_gate', 'v_b_ple_gate', 'v_w_ple_in', 'v_ln3_g', 'v_ln3_b']
TWIN_OUTPUTS = ['loss', 'grad_x', 'grad_w_in', 'grad_ln_z_g', 'grad_ln_z_b', 'grad_w_s', 'grad_b_s', 'grad_w_o', 'grad_ln1_g', 'grad_ln1_b', 'grad_w_ff_a', 'grad_w_ff_b', 'grad_conv_w', 'grad_conv_b', 'grad_w_ff_down', 'grad_ln2_g', 'grad_ln2_b', 'grad_w_ple_gate', 'grad_b_ple_gate', 'grad_w_ple_in', 'grad_ln3_g', 'grad_ln3_b', 'delta_w_in', 'delta_ln_z_g', 'delta_ln_z_b', 'delta_w_s', 'delta_b_s', 'delta_w_o', 'delta_ln1_g', 'delta_ln1_b', 'delta_w_ff_a', 'delta_w_ff_b', 'delta_conv_w', 'delta_conv_b', 'delta_w_ff_down', 'delta_ln2_g', 'delta_ln2_b', 'delta_w_ple_gate', 'delta_b_ple_gate', 'delta_w_ple_in', 'delta_ln3_g', 'delta_ln3_b', 'new_m_w_in', 'new_m_ln_z_g', 'new_m_ln_z_b', 'new_m_w_s', 'new_m_b_s', 'new_m_w_o', 'new_m_ln1_g', 'new_m_ln1_b', 'new_m_w_ff_a', 'new_m_w_ff_b', 'new_m_conv_w', 'new_m_conv_b', 'new_m_w_ff_down', 'new_m_ln2_g', 'new_m_ln2_b', 'new_m_w_ple_gate', 'new_m_b_ple_gate', 'new_m_w_ple_in', 'new_m_ln3_g', 'new_m_ln3_b', 'new_v_w_in', 'new_v_ln_z_g', 'new_v_ln_z_b', 'new_v_w_s', 'new_v_b_s', 'new_v_w_o', 'new_v_ln1_g', 'new_v_ln1_b', 'new_v_w_ff_a', 'new_v_w_ff_b', 'new_v_conv_w', 'new_v_conv_b', 'new_v_w_ff_down', 'new_v_ln2_g', 'new_v_ln2_b', 'new_v_w_ple_gate', 'new_v_b_ple_gate', 'new_v_w_ple_in', 'new_v_ln3_g', 'new_v_ln3_b']
TWIN_LEAF_KINDS = {'loss': 'loss', 'grad_x': 'grad_x', 'grad_w_in': 'grad_w', 'grad_ln_z_g': 'grad_w', 'grad_ln_z_b': 'grad_w', 'grad_w_s': 'grad_w', 'grad_b_s': 'grad_w', 'grad_w_o': 'grad_w', 'grad_ln1_g': 'grad_w', 'grad_ln1_b': 'grad_w', 'grad_w_ff_a': 'grad_w', 'grad_w_ff_b': 'grad_w', 'grad_conv_w': 'grad_w', 'grad_conv_b': 'grad_w', 'grad_w_ff_down': 'grad_w', 'grad_ln2_g': 'grad_w', 'grad_ln2_b': 'grad_w', 'grad_w_ple_gate': 'grad_w', 'grad_b_ple_gate': 'grad_w', 'grad_w_ple_in': 'grad_w', 'grad_ln3_g': 'grad_w', 'grad_ln3_b': 'grad_w', 'delta_w_in': 'delta_w', 'delta_ln_z_g': 'delta_w', 'delta_ln_z_b': 'delta_w', 'delta_w_s': 'delta_w', 'delta_b_s': 'delta_w', 'delta_w_o': 'delta_w', 'delta_ln1_g': 'delta_w', 'delta_ln1_b': 'delta_w', 'delta_w_ff_a': 'delta_w', 'delta_w_ff_b': 'delta_w', 'delta_conv_w': 'delta_w', 'delta_conv_b': 'delta_w', 'delta_w_ff_down': 'delta_w', 'delta_ln2_g': 'delta_w', 'delta_ln2_b': 'delta_w', 'delta_w_ple_gate': 'delta_w', 'delta_b_ple_gate': 'delta_w', 'delta_w_ple_in': 'delta_w', 'delta_ln3_g': 'delta_w', 'delta_ln3_b': 'delta_w', 'new_m_w_in': 'new_m', 'new_m_ln_z_g': 'new_m', 'new_m_ln_z_b': 'new_m', 'new_m_w_s': 'new_m', 'new_m_b_s': 'new_m', 'new_m_w_o': 'new_m', 'new_m_ln1_g': 'new_m', 'new_m_ln1_b': 'new_m', 'new_m_w_ff_a': 'new_m', 'new_m_w_ff_b': 'new_m', 'new_m_conv_w': 'new_m', 'new_m_conv_b': 'new_m', 'new_m_w_ff_down': 'new_m', 'new_m_ln2_g': 'new_m', 'new_m_ln2_b': 'new_m', 'new_m_w_ple_gate': 'new_m', 'new_m_b_ple_gate': 'new_m', 'new_m_w_ple_in': 'new_m', 'new_m_ln3_g': 'new_m', 'new_m_ln3_b': 'new_m', 'new_v_w_in': 'new_v', 'new_v_ln_z_g': 'new_v', 'new_v_ln_z_b': 'new_v', 'new_v_w_s': 'new_v', 'new_v_b_s': 'new_v', 'new_v_w_o': 'new_v', 'new_v_ln1_g': 'new_v', 'new_v_ln1_b': 'new_v', 'new_v_w_ff_a': 'new_v', 'new_v_w_ff_b': 'new_v', 'new_v_conv_w': 'new_v', 'new_v_conv_b': 'new_v', 'new_v_w_ff_down': 'new_v', 'new_v_ln2_g': 'new_v', 'new_v_ln2_b': 'new_v', 'new_v_w_ple_gate': 'new_v', 'new_v_b_ple_gate': 'new_v', 'new_v_w_ple_in': 'new_v', 'new_v_ln3_g': 'new_v', 'new_v_ln3_b': 'new_v'}


def _forward(args):
    return _fwd_reference(*[args[k] for k in FWD_PARAMS])


def _output_shape():
    out = _jax.eval_shape(lambda: _forward(_fwd_setup_inputs(0)))
    return out.shape, out.dtype

N_MICROBATCH = 1
ADAM_LR = 0.001
ADAM_B1 = 0.9
ADAM_B2 = 0.999
ADAM_EPS = 1e-08
ADAM_WD = 0.01
ADAM_STEP = 10
PER_EXAMPLE_BATCH_AXIS = {'x': 0, 'p': 1, 'positions': 0, 'loss_target': 0}
SHARED_INPUTS = []
_WEIGHT_DTYPES = {'w_in': _jnp.float32, 'ln_z_g': _jnp.float32, 'ln_z_b': _jnp.float32, 'w_s': _jnp.float32, 'b_s': _jnp.float32, 'w_o': _jnp.float32, 'ln1_g': _jnp.float32, 'ln1_b': _jnp.float32, 'w_ff_a': _jnp.float32, 'w_ff_b': _jnp.float32, 'conv_w': _jnp.float32, 'conv_b': _jnp.float32, 'w_ff_down': _jnp.float32, 'ln2_g': _jnp.float32, 'ln2_b': _jnp.float32, 'w_ple_gate': _jnp.float32, 'b_ple_gate': _jnp.float32, 'w_ple_in': _jnp.float32, 'ln3_g': _jnp.float32, 'ln3_b': _jnp.float32}
MOMENT_SCALE = {'w_in': 3.791800e-02, 'ln_z_g': 3.837266e-02, 'ln_z_b': 4.023049e-02, 'w_s': 2.704147e-02, 'b_s': 3.743282e-02, 'w_o': 8.216836e-02, 'ln1_g': 4.231663e-01, 'ln1_b': 2.446389e-01, 'w_ff_a': 1.957377e-02, 'w_ff_b': 1.898625e-02, 'conv_w': 1.157224e-02, 'conv_b': 1.903016e-02, 'w_ff_down': 3.145627e-02, 'ln2_g': 4.319318e-01, 'ln2_b': 2.426528e-01, 'w_ple_gate': 1.802825e-02, 'b_ple_gate': 2.112731e-02, 'w_ple_in': 7.813938e-02, 'ln3_g': 3.197895e+01, 'ln3_b': 2.696722e+00}


def _to_microbatches(a, axis):
    t = _jnp.moveaxis(a, axis, 0)
    t = t.reshape((N_MICROBATCH, t.shape[0] // N_MICROBATCH) + t.shape[1:])
    return _jnp.moveaxis(t, 1, axis + 1)


def setup_inputs(seed: int = 0) -> dict:
    inp = _fwd_setup_inputs(seed)
    key = _jax.random.fold_in(_jax.random.key(seed), 7919)
    shape, _ = _output_shape()
    out = dict(inp)
    out["loss_target"] = _jax.random.normal(_jax.random.fold_in(key, 0), shape, _jnp.float32)
    for i, name in enumerate(TWIN_WEIGHTS):
        w = inp[name].astype(_jnp.float32)
        if MOMENT_SCALE is None:
            s = _jnp.sqrt(_jnp.mean(_jnp.square(w)) + 1e-30)
        else:
            s = MOMENT_SCALE[name]
        km, kv = _jax.random.split(_jax.random.fold_in(key, i + 1))
        out[name] = w
        out["m_" + name] = s * _jax.random.normal(km, w.shape, _jnp.float32)
        out["v_" + name] = (s * s) * _jax.random.uniform(kv, w.shape, _jnp.float32, 0.5, 1.5)
    if N_MICROBATCH > 1:
        for name, axis in PER_EXAMPLE_BATCH_AXIS.items():
            out[name] = _to_microbatches(out[name], axis)
    return {'x': out['x'], 'p': out['p'], 'positions': out['positions'], 'w_in': out['w_in'], 'ln_z_g': out['ln_z_g'], 'ln_z_b': out['ln_z_b'], 'w_s': out['w_s'], 'b_s': out['b_s'], 'w_o': out['w_o'], 'ln1_g': out['ln1_g'], 'ln1_b': out['ln1_b'], 'w_ff_a': out['w_ff_a'], 'w_ff_b': out['w_ff_b'], 'conv_w': out['conv_w'], 'conv_b': out['conv_b'], 'w_ff_down': out['w_ff_down'], 'ln2_g': out['ln2_g'], 'ln2_b': out['ln2_b'], 'w_ple_gate': out['w_ple_gate'], 'b_ple_gate': out['b_ple_gate'], 'w_ple_in': out['w_ple_in'], 'ln3_g': out['ln3_g'], 'ln3_b': out['ln3_b'], 'loss_target': out['loss_target'], 'm_w_in': out['m_w_in'], 'm_ln_z_g': out['m_ln_z_g'], 'm_ln_z_b': out['m_ln_z_b'], 'm_w_s': out['m_w_s'], 'm_b_s': out['m_b_s'], 'm_w_o': out['m_w_o'], 'm_ln1_g': out['m_ln1_g'], 'm_ln1_b': out['m_ln1_b'], 'm_w_ff_a': out['m_w_ff_a'], 'm_w_ff_b': out['m_w_ff_b'], 'm_conv_w': out['m_conv_w'], 'm_conv_b': out['m_conv_b'], 'm_w_ff_down': out['m_w_ff_down'], 'm_ln2_g': out['m_ln2_g'], 'm_ln2_b': out['m_ln2_b'], 'm_w_ple_gate': out['m_w_ple_gate'], 'm_b_ple_gate': out['m_b_ple_gate'], 'm_w_ple_in': out['m_w_ple_in'], 'm_ln3_g': out['m_ln3_g'], 'm_ln3_b': out['m_ln3_b'], 'v_w_in': out['v_w_in'], 'v_ln_z_g': out['v_ln_z_g'], 'v_ln_z_b': out['v_ln_z_b'], 'v_w_s': out['v_w_s'], 'v_b_s': out['v_b_s'], 'v_w_o': out['v_w_o'], 'v_ln1_g': out['v_ln1_g'], 'v_ln1_b': out['v_ln1_b'], 'v_w_ff_a': out['v_w_ff_a'], 'v_w_ff_b': out['v_w_ff_b'], 'v_conv_w': out['v_conv_w'], 'v_conv_b': out['v_conv_b'], 'v_w_ff_down': out['v_w_ff_down'], 'v_ln2_g': out['v_ln2_g'], 'v_ln2_b': out['v_ln2_b'], 'v_w_ple_gate': out['v_w_ple_gate'], 'v_b_ple_gate': out['v_b_ple_gate'], 'v_w_ple_in': out['v_w_ple_in'], 'v_ln3_g': out['v_ln3_g'], 'v_ln3_b': out['v_ln3_b']}


def _loss(weights, diff, rest, loss_target):
    with _jax.named_scope("forward"):
        args = {**rest, TWIN_DIFF_INPUT: diff, **{k: w.astype(_WEIGHT_DTYPES[k]) for k, w in weights.items()}}
        y = _forward(args)
    with _jax.named_scope("loss_head"):
        err = _jnp.square(y.astype(_jnp.float32) - loss_target)
        return 0.5 * _jnp.sum(_jnp.mean(err, axis=-1)) if err.ndim else 0.5 * err


def _adamw(w, g, m, v):
    m = ADAM_B1 * m + (1.0 - ADAM_B1) * g
    v = ADAM_B2 * v + (1.0 - ADAM_B2) * _jnp.square(g)
    m_hat = m / (1.0 - ADAM_B1 ** ADAM_STEP)
    v_hat = v / (1.0 - ADAM_B2 ** ADAM_STEP)
    delta = -ADAM_LR * (m_hat / (_jnp.sqrt(v_hat) + ADAM_EPS) + ADAM_WD * w)
    return delta, m, v


def reference(x, p, positions, w_in, ln_z_g, ln_z_b, w_s, b_s, w_o, ln1_g, ln1_b, w_ff_a, w_ff_b, conv_w, conv_b, w_ff_down, ln2_g, ln2_b, w_ple_gate, b_ple_gate, w_ple_in, ln3_g, ln3_b, loss_target, m_w_in, m_ln_z_g, m_ln_z_b, m_w_s, m_b_s, m_w_o, m_ln1_g, m_ln1_b, m_w_ff_a, m_w_ff_b, m_conv_w, m_conv_b, m_w_ff_down, m_ln2_g, m_ln2_b, m_w_ple_gate, m_b_ple_gate, m_w_ple_in, m_ln3_g, m_ln3_b, v_w_in, v_ln_z_g, v_ln_z_b, v_w_s, v_b_s, v_w_o, v_ln1_g, v_ln1_b, v_w_ff_a, v_w_ff_b, v_conv_w, v_conv_b, v_w_ff_down, v_ln2_g, v_ln2_b, v_w_ple_gate, v_b_ple_gate, v_w_ple_in, v_ln3_g, v_ln3_b):
    given = dict(x=x, p=p, positions=positions, w_in=w_in, ln_z_g=ln_z_g, ln_z_b=ln_z_b, w_s=w_s, b_s=b_s, w_o=w_o, ln1_g=ln1_g, ln1_b=ln1_b, w_ff_a=w_ff_a, w_ff_b=w_ff_b, conv_w=conv_w, conv_b=conv_b, w_ff_down=w_ff_down, ln2_g=ln2_g, ln2_b=ln2_b, w_ple_gate=w_ple_gate, b_ple_gate=b_ple_gate, w_ple_in=w_ple_in, ln3_g=ln3_g, ln3_b=ln3_b, loss_target=loss_target, m_w_in=m_w_in, m_ln_z_g=m_ln_z_g, m_ln_z_b=m_ln_z_b, m_w_s=m_w_s, m_b_s=m_b_s, m_w_o=m_w_o, m_ln1_g=m_ln1_g, m_ln1_b=m_ln1_b, m_w_ff_a=m_w_ff_a, m_w_ff_b=m_w_ff_b, m_conv_w=m_conv_w, m_conv_b=m_conv_b, m_w_ff_down=m_w_ff_down, m_ln2_g=m_ln2_g, m_ln2_b=m_ln2_b, m_w_ple_gate=m_w_ple_gate, m_b_ple_gate=m_b_ple_gate, m_w_ple_in=m_w_ple_in, m_ln3_g=m_ln3_g, m_ln3_b=m_ln3_b, v_w_in=v_w_in, v_ln_z_g=v_ln_z_g, v_ln_z_b=v_ln_z_b, v_w_s=v_w_s, v_b_s=v_b_s, v_w_o=v_w_o, v_ln1_g=v_ln1_g, v_ln1_b=v_ln1_b, v_w_ff_a=v_w_ff_a, v_w_ff_b=v_w_ff_b, v_conv_w=v_conv_w, v_conv_b=v_conv_b, v_w_ff_down=v_w_ff_down, v_ln2_g=v_ln2_g, v_ln2_b=v_ln2_b, v_w_ple_gate=v_w_ple_gate, v_b_ple_gate=v_b_ple_gate, v_w_ple_in=v_w_ple_in, v_ln3_g=v_ln3_g, v_ln3_b=v_ln3_b)
    weights = {n: given[n] for n in TWIN_WEIGHTS}
    shared = {n: given[n] for n in SHARED_INPUTS}
    per_example = {n: given[n] for n in ['x', 'p', 'positions']}
    grad_fn = _jax.value_and_grad(_loss, argnums=(0, 1))

    def one_microbatch(ex, loss_target):
        ex = dict(ex)
        diff = ex.pop(TWIN_DIFF_INPUT)
        return grad_fn(weights, diff, {**shared, **ex}, loss_target)

    if N_MICROBATCH == 1:
        loss, (grad_w, grad_x) = one_microbatch(per_example, given["loss_target"])
    else:
        def body(carry, xs):
            loss_sum, grad_sum = carry
            l_k, (gw_k, gx_k) = one_microbatch(xs[0], xs[1])
            with _jax.named_scope("update"):
                return (loss_sum + l_k, _jax.tree.map(_jnp.add, grad_sum, gw_k)), gx_k

        init = (_jnp.zeros((), _jnp.float32), _jax.tree.map(_jnp.zeros_like, weights))
        (loss, grad_w), grad_x = _jax.lax.scan(body, init, (per_example, given["loss_target"]))
    with _jax.named_scope("update"):
        delta_w, new_m, new_v = {}, {}, {}
        for n in TWIN_WEIGHTS:
            delta_w[n], new_m[n], new_v[n] = _adamw(weights[n], grad_w[n], given["m_" + n], given["v_" + n])
    return (loss, grad_x, *[grad_w[n] for n in TWIN_WEIGHTS], *[delta_w[n] for n in TWIN_WEIGHTS],
            *[new_m[n] for n in TWIN_WEIGHTS], *[new_v[n] for n in TWIN_WEIGHTS])
```

```python
import math

import numpy as np
import jax
import jax.numpy as jnp
from jax import lax
from jax.experimental import pallas as pl
from jax.experimental.pallas import tpu as pltpu

F32 = jnp.float32
BF16 = jnp.bfloat16
MESH = pl.DeviceIdType.MESH

N_DEV = 8
S = 4096
D = 1024
D_HALF = 512
D_IN = 2560
D_FF = 2816
D_PLE = 256
CHUNK = 128
DILATIONS = ((1, 32), (4, 8), (16, 2))
ROPE_THETA = 500000.0
LN_EPS = 1e-5
ALPHA = 2.0 ** 0.25
NEG_INF = -1e30
INV_SQRT2 = 1.0 / math.sqrt(2.0)
INV_SQRT_2PI = 1.0 / math.sqrt(2.0 * math.pi)

ADAM_LR, ADAM_B1, ADAM_B2, ADAM_EPS, ADAM_WD, ADAM_STEP = 0.001, 0.9, 0.999, 1e-08, 0.01, 10

TM = 512
NT = S // TM
TN = 256
NJ = D_FF // TN
LANES = 128
VMEM_MIB = 1024 * 1024

BIG = (("w_in", (1024, 320), 1), ("w_o", (128, 1024), 0), ("w_ff_a", (1024, 352), 1),
       ("w_ff_b", (1024, 352), 1), ("w_ff_down", (352, 1024), 0), ("w_ple_gate", (128, 1024), 0),
       ("w_ple_in", (256, 128), 1), ("conv_w", (3, 352), 1))
BIG_N = sum(r * c for _, (r, c), _ in BIG)
CONV_W_N = 3 * 352
BIG_ROWS = -(-(BIG_N + 2 * CONV_W_N) // (LANES * 16 * 7)) * 16 * 7
BIG_TILE = BIG_ROWS // 7
SMALL = (("ln_z_g", (1, 512)), ("ln_z_b", (1, 512)), ("w_s", (1, 8, 128, 128)), ("b_s", (1, 8, 128)),
         ("ln1_g", (1, 1024)), ("ln1_b", (1, 1024)), ("conv_b", (1, 2816)), ("ln2_g", (1, 1024)),
         ("ln2_b", (1, 1024)), ("b_ple_gate", (1, 1024)), ("ln3_g", (1, 1024)), ("ln3_b", (1, 1024)))
SMALL_N = sum(int(np.prod(s)) for _, s in SMALL)
SMALL_ROWS = -(-SMALL_N // (LANES * 8)) * 8
WEIGHT_ORDER = ("w_in", "ln_z_g", "ln_z_b", "w_s", "b_s", "w_o", "ln1_g", "ln1_b", "w_ff_a", "w_ff_b",
                "conv_w", "conv_b", "w_ff_down", "ln2_g", "ln2_b", "w_ple_gate", "b_ple_gate",
                "w_ple_in", "ln3_g", "ln3_b")


def _params(semantics=None, vmem_mib=48):
    return pltpu.CompilerParams(dimension_semantics=semantics, vmem_limit_bytes=vmem_mib * VMEM_MIB)


def _dot(a, b):
    return jnp.dot(a, b, preferred_element_type=F32)


def _dot_nt(a, b):
    return lax.dot_general(a, b, (((1,), (1,)), ((), ())), preferred_element_type=F32)


def _dot_tn(a, b):
    return lax.dot_general(a, b, (((0,), (0,)), ((), ())), preferred_element_type=F32)


def _gelu(x):
    return 0.5 * x * (1.0 + lax.erf(x * INV_SQRT2))


def _gelu_grad(x):
    return 0.5 * (1.0 + lax.erf(x * INV_SQRT2)) + x * (jnp.exp(-0.5 * x * x) * INV_SQRT_2PI)


def _ln_stats(y):
    mu = jnp.mean(y, axis=-1, keepdims=True)
    yc = y - mu
    var = jnp.mean(yc * yc, axis=-1, keepdims=True)
    rstd = lax.rsqrt(var + LN_EPS)
    return yc * rstd, rstd


def _ln_bwd(dxhat, xhat, rstd):
    m1 = jnp.mean(dxhat, axis=-1, keepdims=True)
    m2 = jnp.mean(dxhat * xhat, axis=-1, keepdims=True)
    return rstd * (dxhat - m1 - xhat * m2)


def _colsum(x):
    return jnp.sum(x, axis=0, keepdims=True)


def _rows(i):
    return (i, 0)


def _fixed(*_):
    return (0, 0)


def _row_spec(width):
    return pl.BlockSpec((TM, width), _rows)


def _full_spec(shape):
    return pl.BlockSpec(shape, lambda *_: (0,) * len(shape))


def _lane_lo():
    return lax.broadcasted_iota(jnp.int32, (CHUNK, LANES), 1) < 64


def _tril():
    r = lax.broadcasted_iota(jnp.int32, (CHUNK, CHUNK), 0)
    c = lax.broadcasted_iota(jnp.int32, (CHUNK, CHUNK), 1)
    return c <= r


def _rope_consts():
    lane = np.arange(LANES) % 64
    j = lane % 8
    inv = np.where(lane < 16, np.float32(ROPE_THETA) ** (-(2.0 * j).astype(np.float32) / np.float32(16.0)), 0.0)
    m_lo = (lane < 8).astype(np.float32)
    m_hi = ((lane >= 8) & (lane < 16)).astype(np.float32)
    return (jnp.asarray(inv, F32).reshape(1, LANES), jnp.asarray(m_lo).reshape(1, LANES),
            jnp.asarray(m_hi).reshape(1, LANES))


def _rope_tables(pos_col):
    inv, m_lo, m_hi = _rope_consts()

    def body(pos_ref, inv_ref, lo_ref, hi_ref, c_ref, sa_ref, sb_ref):
        ang = pos_ref[...] * inv_ref[...]
        c = jnp.cos(ang)
        s = jnp.sin(ang)
        lo = lo_ref[...]
        hi = hi_ref[...]
        c_ref[...] = jnp.where(lo + hi > 0.0, c, 1.0)
        sa_ref[...] = s * hi
        sb_ref[...] = -s * lo

    vec = _full_spec((1, LANES))
    out = jax.ShapeDtypeStruct((S, LANES), F32)
    return pl.pallas_call(
        body, name="rope_tables", grid=(NT,), out_shape=(out, out, out),
        in_specs=[pl.BlockSpec((TM, 1), _rows), vec, vec, vec],
        out_specs=(_row_spec(LANES),) * 3, compiler_params=_params(("parallel",)),
    )(pos_col, inv, m_lo, m_hi)


def _rope(t, c, sa, sb):
    return t * c + pltpu.roll(t, 8, 1) * sa + pltpu.roll(t, LANES - 8, 1) * sb


def _rope_t(dy, c, sa, sb):
    return dy * c + pltpu.roll(dy * sa, LANES - 8, 1) + pltpu.roll(dy * sb, 8, 1)


def _masked_ws(ws_ref):
    tril = _tril()
    return [jnp.where(tril, ws_ref[g], 0.0).astype(BF16) for g in range(8)]


def _spatial_mix(zn, wm, bs):
    lo = _lane_lo()
    rows = []
    for ch in range(TM // CHUNK):
        slabs = []
        for pr in range(4):
            zp = zn[ch * CHUNK:(ch + 1) * CHUNK, pr * LANES:(pr + 1) * LANES].astype(BF16)
            slabs.append(jnp.where(lo, _dot(wm[2 * pr], zp), _dot(wm[2 * pr + 1], zp)))
        rows.append(jnp.concatenate(slabs, axis=1) + bs)
    return jnp.concatenate(rows, axis=0)


def _proj_in_fwd(x, w_in, tabs, ln_z_g, ln_z_b, w_s, bs_exp):
    def body(x_ref, w_ref, c_ref, sa_ref, sb_ref, g_ref, b_ref, ws_ref, bs_ref,
             q_ref, k_ref, v_ref, u_ref, z_ref, gm_ref, xb_ref):
        xb = x_ref[...].astype(BF16)
        xb_ref[...] = xb
        c, sa, sb = c_ref[...], sa_ref[...], sb_ref[...]
        hq = _dot(xb, w_ref[:, 0:512])
        hk = _dot(xb, w_ref[:, 512:1024])
        for s in range(4):
            sl = slice(s * LANES, (s + 1) * LANES)
            q_ref[:, sl] = _rope(hq[:, sl], c, sa, sb)
            k_ref[:, sl] = _rope(hk[:, sl], c, sa, sb)
        v_ref[...] = _dot(xb, w_ref[:, 1024:1536])
        u_pre = _dot(xb, w_ref[:, 1536:2048])
        z_pre = _dot(xb, w_ref[:, 2048:2560])
        u_ref[...] = u_pre
        z_ref[...] = z_pre
        zhat, _ = _ln_stats(_gelu(z_pre))
        zn = zhat * g_ref[...] + b_ref[...]
        mixed = _spatial_mix(zn, _masked_ws(ws_ref), bs_ref[...])
        gm_ref[...] = (_gelu(u_pre) * mixed).astype(BF16)

    half = jax.ShapeDtypeStruct((S, D_HALF), F32)
    tab = _row_spec(LANES)
    return pl.pallas_call(
        body, name="proj_in_fwd", grid=(NT,),
        out_shape=(half, half, half, half, half, jax.ShapeDtypeStruct((S, D_HALF), BF16),
                   jax.ShapeDtypeStruct((S, D), BF16)),
        in_specs=[_row_spec(D), _full_spec((D, D_IN)), tab, tab, tab, _full_spec((1, D_HALF)),
                  _full_spec((1, D_HALF)), _full_spec((8, CHUNK, CHUNK)), _full_spec((CHUNK, D_HALF))],
        out_specs=(_row_spec(D_HALF),) * 6 + (_row_spec(D),),
        compiler_params=_params(("parallel",)),
    )(x, w_in, *tabs, ln_z_g, ln_z_b, w_s, bs_exp)


def _band_mask(first):
    qi = lax.broadcasted_iota(jnp.int32, (CHUNK, 2 * CHUNK), 0)
    kj = lax.broadcasted_iota(jnp.int32, (CHUNK, 2 * CHUNK), 1)
    band = (kj >= qi) & (kj <= qi + CHUNK)
    return band & (jnp.logical_not(first) | (kj >= CHUNK))


def _permuted_rows(ref, d, r):
    return ref[...] if d == 1 else ref[pl.ds(r, S // d, stride=d), :]


def _attention_fwd(q, k, v):
    def body(q_ref, k_ref, v_ref, o_ref, lse_ref, q0b, q1b, kb, vb, op, lp, ob0, lb0, ob1, lb1, ob2, lb2):
        lo = _lane_lo()
        lo_f = lo.astype(F32)[0:1, :]
        hi_f = 1.0 - lo_f
        zero_pad = jnp.zeros((CHUNK, LANES), BF16)
        for buf in (q0b, q1b, kb, vb):
            buf[0:CHUNK, :] = zero_pad
        outs = ((ob0, lb0), (ob1, lb1), (ob2, lb2))
        for (d, nb), (ob, lb) in zip(DILATIONS, outs):
            length = S // d
            for r in range(d):
                dst = slice(CHUNK + r * length, CHUNK + (r + 1) * length)
                qs = _permuted_rows(q_ref, d, r) * 0.125
                q0b[dst, :] = (qs * lo_f).astype(BF16)
                q1b[dst, :] = (qs * hi_f).astype(BF16)
                kb[dst, :] = _permuted_rows(k_ref, d, r).astype(BF16)
                vb[dst, :] = _permuted_rows(v_ref, d, r).astype(BF16)

            def block(b, carry, nb=nb):
                base = pl.multiple_of(b * CHUNK, CHUNK)
                mask = _band_mask(b % nb == 0)
                kblk = kb[pl.ds(base, 2 * CHUNK), :]
                vblk = vb[pl.ds(base, 2 * CHUNK), :]
                res = []
                for qh in (q0b, q1b):
                    s = _dot_nt(qh[pl.ds(pl.multiple_of(base + CHUNK, CHUNK), CHUNK), :], kblk)
                    s = jnp.where(mask, s, NEG_INF)
                    m = jnp.max(s, axis=-1, keepdims=True)
                    p = jnp.exp(s - m)
                    den = jnp.sum(p, axis=-1, keepdims=True)
                    res.append((_dot(p.astype(BF16), vblk) / den, m + jnp.log(den)))
                op[pl.ds(base, CHUNK), :] = jnp.where(lo, res[0][0], res[1][0])
                lp[pl.ds(base, CHUNK), :] = jnp.where(lo, res[0][1], res[1][1])
                return carry

            lax.fori_loop(0, S // CHUNK, block, 0)
            for r in range(d):
                src = slice(r * length, (r + 1) * length)
                if d == 1:
                    ob[...] = op[...]
                    lb[...] = lp[...]
                else:
                    ob[pl.ds(r, length, stride=d), :] = op[src, :]
                    lb[pl.ds(r, length, stride=d), :] = lp[src, :]
        for t in range(NT):
            rows = slice(t * TM, (t + 1) * TM)
            l0, l1, l2 = lb0[rows, :], lb1[rows, :], lb2[rows, :]
            mx = jnp.maximum(jnp.maximum(l0, l1), l2)
            e0, e1, e2 = jnp.exp(l0 - mx), jnp.exp(l1 - mx), jnp.exp(l2 - mx)
            den = e0 + e1 + e2
            o_ref[rows, :] = (e0 * ob0[rows, :] + e1 * ob1[rows, :] + e2 * ob2[rows, :]) / den
            lse_ref[rows, :] = mx + jnp.log(den)

    slab = pl.BlockSpec((S, LANES), lambda h: (0, h))
    out = jax.ShapeDtypeStruct((S, D_HALF), F32)
    padded = pltpu.VMEM((CHUNK + S, LANES), BF16)
    whole = pltpu.VMEM((S, LANES), F32)
    return pl.pallas_call(
        body, name="attention_fwd", grid=(4,), out_shape=(out, out),
        in_specs=[slab, slab, slab], out_specs=(slab, slab),
        scratch_shapes=[padded] * 4 + [whole] * 8,
        compiler_params=_params(("parallel",), 56),
    )(q, k, v)


def _mix_ln1_fwd(attn, gm, w_o, x, g1, b1):
    def body(a_ref, gm_ref, w_ref, x_ref, g_ref, b_ref, xhat_ref, rstd_ref, x1b_ref):
        mix = _dot(a_ref[...].astype(BF16), w_ref[0:D_HALF, :]) + _dot(gm_ref[...], w_ref[D_HALF:D, :])
        xhat, rstd = _ln_stats(ALPHA * x_ref[...] + mix)
        xhat_ref[...] = xhat
        rstd_ref[...] = rstd
        x1b_ref[...] = (xhat * g_ref[...] + b_ref[...]).astype(BF16)

    vec = _full_spec((1, D))
    return pl.pallas_call(
        body, name="mix_ln1_fwd", grid=(NT,),
        out_shape=(jax.ShapeDtypeStruct((S, D), F32), jax.ShapeDtypeStruct((S, 1), F32),
                   jax.ShapeDtypeStruct((S, D), BF16)),
        in_specs=[_row_spec(D_HALF), _row_spec(D_HALF), _full_spec((D, D)), _row_spec(D), vec, vec],
        out_specs=(_row_spec(D), pl.BlockSpec((TM, 1), _rows), _row_spec(D)),
        compiler_params=_params(("parallel",)),
    )(attn, gm, w_o, x, g1, b1)


def _ffn_up_fwd(x1b, w_a, w_b, conv_w8, conv_b):
    def body(x_ref, wa_ref, wb_ref, cw_ref, cb_ref, ap_ref, a_ref, bl_ref, h_ref, carry):
        @pl.when(pl.program_id(1) == 0)
        def _():
            carry[...] = jnp.zeros_like(carry)

        xb = x_ref[...]
        ap = _dot(xb, wa_ref[...])
        bl = _dot(xb, wb_ref[...])
        row = lax.broadcasted_iota(jnp.int32, (TM, TN), 0)
        c6, c7 = carry[6:7, :], carry[7:8, :]
        m1 = jnp.where(row == 0, c7, pltpu.roll(ap, 1, 0))
        m2 = jnp.where(row == 0, c6, jnp.where(row == 1, c7, pltpu.roll(ap, 2, 0)))
        a = cb_ref[...] + cw_ref[0:1, :] * m2 + cw_ref[1:2, :] * m1 + cw_ref[2:3, :] * ap
        carry[...] = ap[TM - 8:TM, :]
        ap_ref[...] = ap
        a_ref[...] = a
        bl_ref[...] = bl
        h_ref[...] = (_gelu(a) * bl).astype(BF16)

    tile = pl.BlockSpec((TM, TN), lambda j, i: (i, j))
    wcol = pl.BlockSpec((D, TN), lambda j, i: (0, j))
    ff = jax.ShapeDtypeStruct((S, D_FF), F32)
    return pl.pallas_call(
        body, name="ffn_up_fwd", grid=(NJ, NT),
        out_shape=(ff, ff, ff, jax.ShapeDtypeStruct((S, D_FF), BF16)),
        in_specs=[pl.BlockSpec((TM, D), lambda j, i: (i, 0)), wcol, wcol,
                  pl.BlockSpec((8, TN), lambda j, i: (0, j)), pl.BlockSpec((1, TN), lambda j, i: (0, j))],
        out_specs=(tile, tile, tile, tile),
        scratch_shapes=[pltpu.VMEM((8, TN), F32)],
        compiler_params=_params(("parallel", "arbitrary")),
    )(x1b, w_a, w_b, conv_w8, conv_b)


def _ffn_down_ln2_fwd(hff, w_down, xhat1, g1, b1):
    def body(h_ref, w_ref, xh_ref, g_ref, b_ref, xhat_ref, rstd_ref):
        x1 = xh_ref[...] * g_ref[...] + b_ref[...]
        xhat, rstd = _ln_stats(ALPHA * x1 + _dot(h_ref[...], w_ref[...]))
        xhat_ref[...] = xhat
        rstd_ref[...] = rstd

    vec = _full_spec((1, D))
    return pl.pallas_call(
        body, name="ffn_down_ln2_fwd", grid=(NT,),
        out_shape=(jax.ShapeDtypeStruct((S, D), F32), jax.ShapeDtypeStruct((S, 1), F32)),
        in_specs=[_row_spec(D_FF), _full_spec((D_FF, D)), _row_spec(D), vec, vec],
        out_specs=(_row_spec(D), pl.BlockSpec((TM, 1), _rows)),
        compiler_params=_params(("parallel",)),
    )(hff, w_down, xhat1, g1, b1)


def _tail_fwd_bwd(xhat2, rstd2, p, target, w_g, w_p, g2, b2, bg, g3, b3):
    def body(xh_ref, rs_ref, p_ref, t_ref, wg_ref, wp_ref, g2_ref, b2_ref, bg_ref, g3_ref, b3_ref,
             loss_ref, dy2_ref, dy2b_ref, dwg_ref, dwp_ref, vec_ref):
        @pl.when(pl.program_id(0) == 0)
        def _():
            loss_ref[...] = jnp.zeros_like(loss_ref)
            dwg_ref[...] = jnp.zeros_like(dwg_ref)
            dwp_ref[...] = jnp.zeros_like(dwp_ref)
            vec_ref[...] = jnp.zeros_like(vec_ref)

        xhat2_t = xh_ref[...]
        x2 = xhat2_t * g2_ref[...] + b2_ref[...]
        x2b = x2.astype(BF16)
        pb = p_ref[...].astype(BF16)
        gate = jax.nn.sigmoid(_dot(x2b, wg_ref[...]) + bg_ref[...])
        pin = _dot(pb, wp_ref[...])
        xhat3, rstd3 = _ln_stats(ALPHA * x2 + gate * pin)
        err = xhat3 * g3_ref[...] + b3_ref[...] - t_ref[...]
        loss_ref[...] += jnp.sum(jnp.mean(err * err, axis=-1, keepdims=True), axis=0, keepdims=True) * 0.5
        dout = err * (1.0 / D)
        dy3 = _ln_bwd(dout * g3_ref[...], xhat3, rstd3)
        dgp = dy3 * pin * gate * (1.0 - gate)
        dgpb = dgp.astype(BF16)
        dwg_ref[...] += _dot_tn(x2b, dgpb)
        dwp_ref[...] += _dot_tn(pb, (dy3 * gate).astype(BF16))
        dx2 = ALPHA * dy3 + _dot_nt(dgpb, wg_ref[...])
        dy2 = _ln_bwd(dx2 * g2_ref[...], xhat2_t, rs_ref[...])
        dy2_ref[...] = dy2
        dy2b_ref[...] = dy2.astype(BF16)
        vec_ref[0:1, :] += _colsum(dgp)
        vec_ref[1:2, :] += _colsum(dout * xhat3)
        vec_ref[2:3, :] += _colsum(dout)
        vec_ref[3:4, :] += _colsum(dx2 * xhat2_t)
        vec_ref[4:5, :] += _colsum(dx2)

    vec = _full_spec((1, D))
    return pl.pallas_call(
        body, name="tail_fwd_bwd", grid=(NT,),
        out_shape=(jax.ShapeDtypeStruct((1, LANES), F32), jax.ShapeDtypeStruct((S, D), F32),
                   jax.ShapeDtypeStruct((S, D), BF16), jax.ShapeDtypeStruct((D, D), F32),
                   jax.ShapeDtypeStruct((D_PLE, D), F32), jax.ShapeDtypeStruct((8, D), F32)),
        in_specs=[_row_spec(D), pl.BlockSpec((TM, 1), _rows), _row_spec(D_PLE), _row_spec(D),
                  _full_spec((D, D)), _full_spec((D_PLE, D)), vec, vec, vec, vec, vec],
        out_specs=(_full_spec((1, LANES)), _row_spec(D), _row_spec(D), _full_spec((D, D)),
                   _full_spec((D_PLE, D)), _full_spec((8, D))),
        compiler_params=_params(("arbitrary",)),
    )(xhat2, rstd2, p, target, w_g, w_p, g2, b2, bg, g3, b3)


def _ffn_bwd(dy2b, x1b, w_down, w_a, w_b, a_pre, a, b_lin, hff, conv_w8):
    def body(dy_ref, x_ref, wd_ref, wa_ref, wb_ref, ap_ref, a_ref, bl_ref, h_ref, cw_ref,
             dwd_ref, dwa_ref, dwb_ref, dcw_ref, dcb_ref, dx_hbm, dx_acc, carry, sem):
        j, i = pl.program_id(0), pl.program_id(1)

        @pl.when(i == 0)
        def _():
            carry[...] = jnp.zeros_like(carry)
            dwd_ref[...] = jnp.zeros_like(dwd_ref)
            dwa_ref[...] = jnp.zeros_like(dwa_ref)
            dwb_ref[...] = jnp.zeros_like(dwb_ref)
            dcw_ref[...] = jnp.zeros_like(dcw_ref)
            dcb_ref[...] = jnp.zeros_like(dcb_ref)

        dyb = dy_ref[...]
        xb = x_ref[...]
        dh = _dot_nt(dyb, wd_ref[...])
        av = a_ref[...]
        dbl = dh * _gelu(av)
        da = dh * bl_ref[...] * _gelu_grad(av)
        row = lax.broadcasted_iota(jnp.int32, (TM, TN), 0)
        c0, c1 = carry[0:1, :], carry[1:2, :]
        p1 = jnp.where(row == TM - 1, c0, pltpu.roll(da, TM - 1, 0))
        p2 = jnp.where(row == TM - 2, c0, jnp.where(row == TM - 1, c1, pltpu.roll(da, TM - 2, 0)))
        carry[...] = da[0:8, :]
        ap = ap_ref[...]
        dcb_ref[...] += _colsum(da)
        dcw_ref[0:1, :] += _colsum(ap * p2)
        dcw_ref[1:2, :] += _colsum(ap * p1)
        dcw_ref[2:3, :] += _colsum(ap * da)
        dap = (cw_ref[2:3, :] * da + cw_ref[1:2, :] * p1 + cw_ref[0:1, :] * p2).astype(BF16)
        dblb = dbl.astype(BF16)
        dwa_ref[...] += _dot_tn(xb, dap)
        dwb_ref[...] += _dot_tn(xb, dblb)
        dwd_ref[...] += _dot_tn(h_ref[...], dyb)
        dx = _dot_nt(dap, wa_ref[...]) + _dot_nt(dblb, wb_ref[...])
        rows = pl.ds(pl.multiple_of((NT - 1 - i) * TM, TM), TM)

        @pl.when(j == 0)
        def _():
            dx_acc[rows, :] = dx

        @pl.when(j > 0)
        def _():
            dx_acc[rows, :] += dx

        @pl.when((j == NJ - 1) & (i == NT - 1))
        def _():
            cp = pltpu.make_async_copy(dx_acc, dx_hbm, sem)
            cp.start()
            cp.wait()

    rev_rows = lambda j, i: (NT - 1 - i, 0)
    rev_tile = pl.BlockSpec((TM, TN), lambda j, i: (NT - 1 - i, j))
    wcol = pl.BlockSpec((D, TN), lambda j, i: (0, j))
    small = pl.BlockSpec((8, TN), lambda j, i: (0, j))
    return pl.pallas_call(
        body, name="ffn_bwd", grid=(NJ, NT),
        out_shape=(jax.ShapeDtypeStruct((D_FF, D), F32), jax.ShapeDtypeStruct((D, D_FF), F32),
                   jax.ShapeDtypeStruct((D, D_FF), F32), jax.ShapeDtypeStruct((8, D_FF), F32),
                   jax.ShapeDtypeStruct((1, D_FF), F32), jax.ShapeDtypeStruct((S, D), F32)),
        in_specs=[pl.BlockSpec((TM, D), rev_rows), pl.BlockSpec((TM, D), rev_rows),
                  pl.BlockSpec((TN, D), lambda j, i: (j, 0)), wcol, wcol, rev_tile, rev_tile, rev_tile,
                  rev_tile, small],
        out_specs=(pl.BlockSpec((TN, D), lambda j, i: (j, 0)), wcol, wcol, small,
                   pl.BlockSpec((1, TN), lambda j, i: (0, j)), pl.BlockSpec(memory_space=pl.ANY)),
        scratch_shapes=[pltpu.VMEM((S, D), F32), pltpu.VMEM((8, TN), F32), pltpu.SemaphoreType.DMA],
        compiler_params=_params(("arbitrary", "arbitrary"), 56),
    )(dy2b, x1b, w_down, w_a, w_b, a_pre, a, b_lin, hff, conv_w8)


def _ln1_mix_bwd(dy2, dx1_ffn, xhat1, rstd1, g1, attn, gm, w_o):
    def body(dy2_ref, dxf_ref, xh_ref, rs_ref, g_ref, a_ref, gm_ref, w_ref,
             dy1_ref, da_ref, dlt_ref, dgm_ref, dwo_ref, vec_ref):
        @pl.when(pl.program_id(0) == 0)
        def _():
            dwo_ref[...] = jnp.zeros_like(dwo_ref)
            vec_ref[...] = jnp.zeros_like(vec_ref)

        xhat = xh_ref[...]
        dx1 = ALPHA * dy2_ref[...] + dxf_ref[...]
        vec_ref[0:1, :] += _colsum(dx1 * xhat)
        vec_ref[1:2, :] += _colsum(dx1)
        dy1 = _ln_bwd(dx1 * g_ref[...], xhat, rs_ref[...])
        dy1_ref[...] = dy1
        dy1b = dy1.astype(BF16)
        dmix = _dot_nt(dy1b, w_ref[...])
        attn_t = a_ref[...]
        d_attn = dmix[:, 0:D_HALF]
        da_ref[...] = d_attn
        dgm_ref[...] = dmix[:, D_HALF:D]
        lo = (lax.broadcasted_iota(jnp.int32, (TM, LANES), 1) < 64)
        for s in range(4):
            sl = slice(s * LANES, (s + 1) * LANES)
            prod = d_attn[:, sl] * attn_t[:, sl]
            s0 = jnp.sum(jnp.where(lo, prod, 0.0), axis=-1, keepdims=True)
            s1 = jnp.sum(jnp.where(lo, 0.0, prod), axis=-1, keepdims=True)
            dlt_ref[:, sl] = jnp.where(lo, s0, s1)
        dwo_ref[0:D_HALF, :] += _dot_tn(attn_t.astype(BF16), dy1b)
        dwo_ref[D_HALF:D, :] += _dot_tn(gm_ref[...], dy1b)

    half = jax.ShapeDtypeStruct((S, D_HALF), F32)
    return pl.pallas_call(
        body, name="ln1_mix_bwd", grid=(NT,),
        out_shape=(jax.ShapeDtypeStruct((S, D), F32), half, half, half,
                   jax.ShapeDtypeStruct((D, D), F32), jax.ShapeDtypeStruct((8, D), F32)),
        in_specs=[_row_spec(D), _row_spec(D), _row_spec(D), pl.BlockSpec((TM, 1), _rows), _full_spec((1, D)),
                  _row_spec(D_HALF), _row_spec(D_HALF), _full_spec((D, D))],
        out_specs=(_row_spec(D), _row_spec(D_HALF), _row_spec(D_HALF), _row_spec(D_HALF),
                   _full_spec((D, D)), _full_spec((8, D))),
        compiler_params=_params(("arbitrary",)),
    )(dy2, dx1_ffn, xhat1, rstd1, g1, attn, gm, w_o)


def _gmlp_bwd(d_gm, u_pre, z_pre, ln_z_g, ln_z_b, w_s, bs_exp):
    def body(dg_ref, u_ref, z_ref, g_ref, b_ref, ws_ref, bs_ref, du_ref, dz_ref, dws_ref, dbs_ref, vec_ref):
        @pl.when(pl.program_id(0) == 0)
        def _():
            dws_ref[...] = jnp.zeros_like(dws_ref)
            dbs_ref[...] = jnp.zeros_like(dbs_ref)
            vec_ref[...] = jnp.zeros_like(vec_ref)

        u_pre_t, z_pre_t, dgm = u_ref[...], z_ref[...], dg_ref[...]
        zhat, rstd = _ln_stats(_gelu(z_pre_t))
        zn = zhat * g_ref[...] + b_ref[...]
        wm = _masked_ws(ws_ref)
        mixed = _spatial_mix(zn, wm, bs_ref[...])
        du_ref[...] = (dgm * mixed * _gelu_grad(u_pre_t)).astype(BF16)
        dmixed = dgm * _gelu(u_pre_t)
        lo = _lane_lo()
        tril = _tril()
        dzn_rows = []
        for ch in range(TM // CHUNK):
            rows = slice(ch * CHUNK, (ch + 1) * CHUNK)
            dbs_ref[...] += dmixed[rows, :]
            slabs = []
            for pr in range(4):
                sl = slice(pr * LANES, (pr + 1) * LANES)
                dm = dmixed[rows, sl]
                zp = zn[rows, sl].astype(BF16)
                dm_lo = jnp.where(lo, dm, 0.0).astype(BF16)
                dm_hi = jnp.where(lo, 0.0, dm).astype(BF16)
                dws_ref[2 * pr] += jnp.where(tril, _dot_nt(dm_lo, zp), 0.0)
                dws_ref[2 * pr + 1] += jnp.where(tril, _dot_nt(dm_hi, zp), 0.0)
                dmb = dm.astype(BF16)
                slabs.append(jnp.where(lo, _dot_tn(wm[2 * pr], dmb), _dot_tn(wm[2 * pr + 1], dmb)))
            dzn_rows.append(jnp.concatenate(slabs, axis=1))
        dzn = jnp.concatenate(dzn_rows, axis=0)
        vec_ref[0:1, :] += _colsum(dzn * zhat)
        vec_ref[1:2, :] += _colsum(dzn)
        dz = _ln_bwd(dzn * g_ref[...], zhat, rstd)
        dz_ref[...] = (dz * _gelu_grad(z_pre_t)).astype(BF16)

    halfb = jax.ShapeDtypeStruct((S, D_HALF), BF16)
    vec = _full_spec((1, D_HALF))
    return pl.pallas_call(
        body, name="gmlp_bwd", grid=(NT,),
        out_shape=(halfb, halfb, jax.ShapeDtypeStruct((8, CHUNK, CHUNK), F32),
                   jax.ShapeDtypeStruct((CHUNK, D_HALF), F32), jax.ShapeDtypeStruct((8, D_HALF), F32)),
        in_specs=[_row_spec(D_HALF), _row_spec(D_HALF), _row_spec(D_HALF), vec, vec,
                  _full_spec((8, CHUNK, CHUNK)), _full_spec((CHUNK, D_HALF))],
        out_specs=(_row_spec(D_HALF), _row_spec(D_HALF), _full_spec((8, CHUNK, CHUNK)),
                   _full_spec((CHUNK, D_HALF)), _full_spec((8, D_HALF))),
        compiler_params=_params(("arbitrary",)),
    )(d_gm, u_pre, z_pre, ln_z_g, ln_z_b, w_s, bs_exp)


def _attention_bwd(q, k, v, lse, d_attn, delta, tabs):
    def body(q_ref, k_ref, v_ref, l_ref, do_ref, dl_ref, c_ref, sa_ref, sb_ref, dq_ref, dk_ref, dv_ref,
             q0b, q1b, kb, vb, g0b, g1b, lsp, dlp, dqp, dkp, dvp, dqa, dka, dva):
        lo = _lane_lo()
        lo_f = lo.astype(F32)[0:1, :]
        hi_f = 1.0 - lo_f
        zero_pad = jnp.zeros((CHUNK, LANES), BF16)
        for buf in (q0b, q1b, kb, vb, g0b, g1b):
            buf[0:CHUNK, :] = zero_pad
        for d, nb in DILATIONS:
            length = S // d
            for r in range(d):
                dst = slice(CHUNK + r * length, CHUNK + (r + 1) * length)
                src = slice(r * length, (r + 1) * length)
                qs = _permuted_rows(q_ref, d, r) * 0.125
                q0b[dst, :] = (qs * lo_f).astype(BF16)
                q1b[dst, :] = (qs * hi_f).astype(BF16)
                kb[dst, :] = _permuted_rows(k_ref, d, r).astype(BF16)
                vb[dst, :] = _permuted_rows(v_ref, d, r).astype(BF16)
                go = _permuted_rows(do_ref, d, r)
                g0b[dst, :] = (go * lo_f).astype(BF16)
                g1b[dst, :] = (go * hi_f).astype(BF16)
                lsp[src, :] = _permuted_rows(l_ref, d, r)
                dlp[src, :] = _permuted_rows(dl_ref, d, r)
            dkp[...] = jnp.zeros_like(dkp)
            dvp[...] = jnp.zeros_like(dvp)

            def block(b, carry, nb=nb):
                base = pl.multiple_of(b * CHUNK, CHUNK)
                mask = _band_mask(b % nb == 0)
                kblk = kb[pl.ds(base, 2 * CHUNK), :]
                vblk = vb[pl.ds(base, 2 * CHUNK), :]
                lse_t = lsp[pl.ds(base, CHUNK), :]
                dlt_t = dlp[pl.ds(base, CHUNK), :]
                dk_blk = jnp.zeros((2 * CHUNK, LANES), F32)
                dv_blk = jnp.zeros((2 * CHUNK, LANES), F32)
                dq_parts = []
                for qh, gh, col in ((q0b, g0b, 0), (q1b, g1b, 64)):
                    qt = qh[pl.ds(pl.multiple_of(base + CHUNK, CHUNK), CHUNK), :]
                    gt = gh[pl.ds(pl.multiple_of(base + CHUNK, CHUNK), CHUNK), :]
                    s = jnp.where(mask, _dot_nt(qt, kblk), NEG_INF)
                    p = jnp.exp(s - lse_t[:, col:col + 1])
                    ds = (p * (_dot_nt(gt, vblk) - dlt_t[:, col:col + 1])).astype(BF16)
                    dv_blk += _dot_tn(p.astype(BF16), gt)
                    dk_blk += _dot_tn(ds, qt)
                    dq_parts.append(_dot(ds, kblk) * 0.125)
                dqp[pl.ds(base, CHUNK), :] = jnp.where(lo, dq_parts[0], dq_parts[1])
                dkp[pl.ds(base, 2 * CHUNK), :] += dk_blk
                dvp[pl.ds(base, 2 * CHUNK), :] += dv_blk
                return carry

            lax.fori_loop(0, S // CHUNK, block, 0)
            for r in range(d):
                src = slice(r * length, (r + 1) * length)
                pad = slice(CHUNK + r * length, CHUNK + (r + 1) * length)
                if d == 1:
                    dqa[...] = dqp[...]
                    dka[...] = dkp[pad, :]
                    dva[...] = dvp[pad, :]
                else:
                    dst = pl.ds(r, length, stride=d)
                    dqa[dst, :] = dqa[dst, :] + dqp[src, :]
                    dka[dst, :] = dka[dst, :] + dkp[pad, :]
                    dva[dst, :] = dva[dst, :] + dvp[pad, :]
        for t in range(NT):
            rows = slice(t * TM, (t + 1) * TM)
            c, sa, sb = c_ref[rows, :], sa_ref[rows, :], sb_ref[rows, :]
            dq_ref[rows, :] = _rope_t(dqa[rows, :], c, sa, sb).astype(BF16)
            dk_ref[rows, :] = _rope_t(dka[rows, :], c, sa, sb).astype(BF16)
            dv_ref[rows, :] = dva[rows, :].astype(BF16)

    slab = pl.BlockSpec((S, LANES), lambda h: (0, h), pipeline_mode=pl.Buffered(1))
    tab = pl.BlockSpec((S, LANES), lambda h: (0, 0), pipeline_mode=pl.Buffered(1))
    out_slab = pl.BlockSpec((S, LANES), lambda h: (0, h))
    out = jax.ShapeDtypeStruct((S, D_HALF), BF16)
    padded_b = pltpu.VMEM((CHUNK + S, LANES), BF16)
    padded_f = pltpu.VMEM((CHUNK + S, LANES), F32)
    whole = pltpu.VMEM((S, LANES), F32)
    return pl.pallas_call(
        body, name="attention_bwd", grid=(4,), out_shape=(out, out, out),
        in_specs=[slab] * 6 + [tab] * 3, out_specs=(out_slab,) * 3,
        scratch_shapes=[padded_b] * 6 + [whole, whole, whole, padded_f, padded_f, whole, whole, whole],
        compiler_params=_params(("parallel",), 60),
    )(q, k, v, lse, d_attn, delta, *tabs)


def _proj_in_bwd_w(xb, parts):
    def body(x_ref, p0, p1, p2, p3, p4, dw_ref):
        @pl.when(pl.program_id(0) == 0)
        def _():
            dw_ref[...] = jnp.zeros_like(dw_ref)

        xt = x_ref[...]
        for n, part in enumerate((p0, p1, p2, p3, p4)):
            dw_ref[:, n * D_HALF:(n + 1) * D_HALF] += _dot_tn(xt, part[...])

    return pl.pallas_call(
        body, name="proj_in_bwd_w", grid=(NT,), out_shape=jax.ShapeDtypeStruct((D, D_IN), F32),
        in_specs=[_row_spec(D)] + [_row_spec(D_HALF)] * 5, out_specs=_full_spec((D, D_IN)),
        compiler_params=_params(("arbitrary",)),
    )(xb, *parts)


def _proj_in_bwd_x(dy1, parts, w_in):
    def body(dy_ref, p0, p1, p2, p3, p4, w_ref, gx_ref):
        acc = ALPHA * dy_ref[...]
        for n, part in enumerate((p0, p1, p2, p3, p4)):
            acc += _dot_nt(part[...], w_ref[:, n * D_HALF:(n + 1) * D_HALF])
        gx_ref[...] = acc

    return pl.pallas_call(
        body, name="proj_in_bwd_x", grid=(NT,), out_shape=jax.ShapeDtypeStruct((S, D), F32),
        in_specs=[_row_spec(D)] + [_row_spec(D_HALF)] * 5 + [_full_spec((D, D_IN))], out_specs=_row_spec(D),
        compiler_params=_params(("parallel",)),
    )(dy1, *parts, w_in)


def _local_step(x, p, pos_col, target, wb, sm):
    bs_exp = jnp.repeat(sm["b_s"].T, 64, axis=1)
    conv_w8 = jnp.concatenate([wb["conv_w"], jnp.zeros((5, D_FF), F32)], axis=0)
    tabs = _rope_tables(pos_col)
    q, k, v, u_pre, z_pre, gm, xb = _proj_in_fwd(x, wb["w_in"], tabs, sm["ln_z_g"], sm["ln_z_b"],
                                                 sm["w_s"], bs_exp)
    attn, lse = _attention_fwd(q, k, v)
    xhat1, rstd1, x1b = _mix_ln1_fwd(attn, gm, wb["w_o"], x, sm["ln1_g"], sm["ln1_b"])
    a_pre, a, b_lin, hff = _ffn_up_fwd(x1b, wb["w_ff_a"], wb["w_ff_b"], conv_w8, sm["conv_b"])
    xhat2, rstd2 = _ffn_down_ln2_fwd(hff, wb["w_ff_down"], xhat1, sm["ln1_g"], sm["ln1_b"])
    loss, dy2, dy2b, dw_g, dw_p, vec_tail = _tail_fwd_bwd(
        xhat2, rstd2, p, target, wb["w_ple_gate"], wb["w_ple_in"], sm["ln2_g"], sm["ln2_b"],
        sm["b_ple_gate"], sm["ln3_g"], sm["ln3_b"])
    dw_down, dw_a, dw_b, dcw, dcb, dx1_ffn = _ffn_bwd(dy2b, x1b, wb["w_ff_down"], wb["w_ff_a"], wb["w_ff_b"],
                                                       a_pre, a, b_lin, hff, conv_w8)
    dy1, d_attn, delta, d_gm, dw_o, vec_ln1 = _ln1_mix_bwd(dy2, dx1_ffn, xhat1, rstd1, sm["ln1_g"], attn, gm,
                                                            wb["w_o"])
    du, dz, dws, dbs_exp, vec_z = _gmlp_bwd(d_gm, u_pre, z_pre, sm["ln_z_g"], sm["ln_z_b"], sm["w_s"], bs_exp)
    dq, dk, dv = _attention_bwd(q, k, v, lse, d_attn, delta, tabs)
    parts = (dq, dk, dv, du, dz)
    dw_in = _proj_in_bwd_w(xb, parts)
    grad_x = _proj_in_bwd_x(dy1, parts, wb["w_in"])
    grads = {
        "w_in": dw_in, "ln_z_g": vec_z[0:1], "ln_z_b": vec_z[1:2], "w_s": dws,
        "b_s": dbs_exp.reshape(CHUNK, 8, 64).sum(axis=2).T, "w_o": dw_o, "ln1_g": vec_ln1[0:1],
        "ln1_b": vec_ln1[1:2], "w_ff_a": dw_a, "w_ff_b": dw_b, "conv_w": dcw[0:3], "conv_b": dcb,
        "w_ff_down": dw_down, "ln2_g": vec_tail[3:4], "ln2_b": vec_tail[4:5], "w_ple_gate": dw_g,
        "b_ple_gate": vec_tail[0:1], "w_ple_in": dw_p, "ln3_g": vec_tail[1:2], "ln3_b": vec_tail[2:3],
    }
    return loss[0, 0], grad_x, grads


def _mesh_pos():
    return lax.axis_index("x"), lax.axis_index("y"), lax.axis_index("c")


def _all_gather_weights(shard):
    def body(x_ref, out_ref, send_sems, recv_sems, local_sem):
        x, y, c = _mesh_pos()
        me, sibling = (x, y, c), (x, y, 1 - c)
        chips = [(1 - x, y), (x, 1 - y), (1 - x, 1 - y)]

        def rows(px, py, pc):
            return out_ref.at[4 * px + 2 * py + pc]

        def copy(n, block, to, src=None):
            return pltpu.make_async_remote_copy(
                src_ref=rows(*block) if src is None else src, dst_ref=rows(*block),
                send_sem=send_sems.at[n], recv_sem=recv_sems.at[n], device_id=to, device_id_type=MESH)

        mine = pltpu.make_async_copy(x_ref, rows(*me), local_sem)
        mine.start()
        first = [copy(0, me, sibling, src=x_ref)]
        first += [copy(1 + n, me, (*chip, c), src=x_ref) for n, chip in enumerate(chips)]
        for cp in first:
            cp.start()
        passed = [copy(4 + n, (*chip, c), sibling) for n, chip in enumerate(chips)]
        for n, chip in enumerate(chips):
            copy(1 + n, (*chip, c), me).wait_recv()
            passed[n].start()
        copy(0, sibling, me).wait_recv()
        for n, chip in enumerate(chips):
            copy(4 + n, (*chip, 1 - c), me).wait_recv()
        for cp in first + passed:
            cp.wait_send()
        mine.wait()

    anywhere = pl.BlockSpec(memory_space=pl.ANY)
    return pl.pallas_call(
        body, name="all_gather_weights", out_shape=jax.ShapeDtypeStruct((N_DEV,) + shard.shape, shard.dtype),
        in_specs=[anywhere], out_specs=anywhere,
        scratch_shapes=[pltpu.SemaphoreType.DMA((7,)), pltpu.SemaphoreType.DMA((7,)), pltpu.SemaphoreType.DMA],
    )(shard)


def _exchange_grads(g_big, g_small):
    def body(gb_ref, gs_ref, rb_ref, rs_ref, send_b, recv_b, send_s, recv_s, local_sems):
        x, y, c = _mesh_pos()
        me = 4 * x + 2 * y + c
        copies = [pltpu.make_async_copy(gb_ref.at[me], rb_ref.at[me], local_sems.at[0]),
                  pltpu.make_async_copy(gs_ref, rs_ref.at[me], local_sems.at[1])]
        for flip in range(1, N_DEV):
            px = 1 - x if flip & 4 else x
            py = 1 - y if flip & 2 else y
            pc = 1 - c if flip & 1 else c
            peer = 4 * px + 2 * py + pc
            copies.append(pltpu.make_async_remote_copy(
                src_ref=gb_ref.at[peer], dst_ref=rb_ref.at[me], send_sem=send_b.at[flip - 1],
                recv_sem=recv_b.at[flip - 1], device_id=(px, py, pc), device_id_type=MESH))
            copies.append(pltpu.make_async_remote_copy(
                src_ref=gs_ref, dst_ref=rs_ref.at[me], send_sem=send_s.at[flip - 1],
                recv_sem=recv_s.at[flip - 1], device_id=(px, py, pc), device_id_type=MESH))
        for cp in copies:
            cp.start()
        for cp in copies:
            cp.wait()

    anywhere = pl.BlockSpec(memory_space=pl.ANY)
    sems = pltpu.SemaphoreType.DMA((N_DEV - 1,))
    return pl.pallas_call(
        body, name="exchange_grads",
        out_shape=(jax.ShapeDtypeStruct(g_big.shape, g_big.dtype),
                   jax.ShapeDtypeStruct((N_DEV,) + g_small.shape, g_small.dtype)),
        in_specs=[anywhere, anywhere], out_specs=(anywhere, anywhere),
        scratch_shapes=[sems, sems, sems, sems, pltpu.SemaphoreType.DMA((2,))],
    )(g_big, g_small)


def _sum_adamw(parts, w, m, v, tile, name):
    def body(p_ref, w_ref, m_ref, v_ref, g_ref, d_ref, nm_ref, nv_ref):
        g = p_ref[0].astype(F32)
        for n in range(1, N_DEV):
            g = g + p_ref[n].astype(F32)
        wv = w_ref[...]
        nm = ADAM_B1 * m_ref[...] + (1.0 - ADAM_B1) * g
        nv = ADAM_B2 * v_ref[...] + (1.0 - ADAM_B2) * (g * g)
        m_hat = nm / (1.0 - ADAM_B1 ** ADAM_STEP)
        v_hat = nv / (1.0 - ADAM_B2 ** ADAM_STEP)
        g_ref[...] = g
        d_ref[...] = -ADAM_LR * (m_hat / (jnp.sqrt(v_hat) + ADAM_EPS) + ADAM_WD * wv)
        nm_ref[...] = nm
        nv_ref[...] = nv

    rows = w.shape[0]
    flat = pl.BlockSpec((tile, LANES), _rows)
    out = jax.ShapeDtypeStruct((rows, LANES), F32)
    return pl.pallas_call(
        body, name=name, grid=(rows // tile,), out_shape=(out,) * 4,
        in_specs=[pl.BlockSpec((N_DEV, tile, LANES), lambda i: (0, i, 0)), flat, flat, flat],
        out_specs=(flat,) * 4, compiler_params=_params(("parallel",)),
    )(parts, w, m, v)


def _pack_rows(flat_parts, rows, dtype):
    flat = jnp.concatenate([a.reshape(-1).astype(dtype) for a in flat_parts])
    return jnp.pad(flat, (0, rows * LANES - flat.shape[0])).reshape(rows, LANES)


def _unpack(packed, layout):
    flat = packed.reshape(-1)
    out, off = {}, 0
    for name, shape in layout:
        n = int(np.prod(shape))
        out[name] = flat[off:off + n].reshape(shape)
        off += n
    return out


def _pack_weight_shards(w):
    parts = [w[name].astype(BF16) for name, _, _ in BIG[:-1]]
    rest = w["conv_w"]
    for _ in range(3):
        term = rest.astype(BF16)
        parts.append(term)
        rest = rest - term.astype(F32)
    return _pack_rows(parts, BIG_ROWS, BF16)


def _unpack_gathered(gathered):
    flat = gathered.reshape(N_DEV, -1)
    out, off = {}, 0
    for name, (r, c), axis in BIG[:-1]:
        blk = flat[:, off:off + r * c].reshape(N_DEV, r, c)
        off += r * c
        out[name] = blk.reshape(N_DEV * r, c) if axis == 0 else blk.transpose(1, 0, 2).reshape(r, N_DEV * c)
    terms = flat[:, off:off + 3 * CONV_W_N].reshape(N_DEV, 3, 3, 352).astype(F32)
    conv_w = terms[:, 0] + terms[:, 1] + terms[:, 2]
    out["conv_w"] = conv_w.transpose(1, 0, 2).reshape(3, D_FF)
    return out


def _pack_grads_by_owner(grads):
    cols = []
    for name, (r, c), axis in BIG:
        g = grads[name]
        blk = g.reshape(N_DEV, r * c) if axis == 0 else g.reshape(r, N_DEV, c).transpose(1, 0, 2).reshape(N_DEV, r * c)
        cols.append(blk.astype(BF16))
    flat = jnp.concatenate(cols, axis=1)
    return jnp.pad(flat, ((0, 0), (0, BIG_ROWS * LANES - flat.shape[1]))).reshape(N_DEV, BIG_ROWS, LANES)


def kernel(x, p, positions, w_in, ln_z_g, ln_z_b, w_s, b_s, w_o, ln1_g, ln1_b, w_ff_a, w_ff_b, conv_w, conv_b, w_ff_down, ln2_g, ln2_b, w_ple_gate, b_ple_gate, w_ple_in, ln3_g, ln3_b, loss_target, m_w_in, m_ln_z_g, m_ln_z_b, m_w_s, m_b_s, m_w_o, m_ln1_g, m_ln1_b, m_w_ff_a, m_w_ff_b, m_conv_w, m_conv_b, m_w_ff_down, m_ln2_g, m_ln2_b, m_w_ple_gate, m_b_ple_gate, m_w_ple_in, m_ln3_g, m_ln3_b, v_w_in, v_ln_z_g, v_ln_z_b, v_w_s, v_b_s, v_w_o, v_ln1_g, v_ln1_b, v_w_ff_a, v_w_ff_b, v_conv_w, v_conv_b, v_w_ff_down, v_ln2_g, v_ln2_b, v_w_ple_gate, v_b_ple_gate, v_w_ple_in, v_ln3_g, v_ln3_b):
    w = dict(w_in=w_in, ln_z_g=ln_z_g, ln_z_b=ln_z_b, w_s=w_s, b_s=b_s, w_o=w_o, ln1_g=ln1_g, ln1_b=ln1_b,
             w_ff_a=w_ff_a, w_ff_b=w_ff_b, conv_w=conv_w, conv_b=conv_b, w_ff_down=w_ff_down, ln2_g=ln2_g,
             ln2_b=ln2_b, w_ple_gate=w_ple_gate, b_ple_gate=b_ple_gate, w_ple_in=w_ple_in, ln3_g=ln3_g,
             ln3_b=ln3_b)
    m = dict(w_in=m_w_in, ln_z_g=m_ln_z_g, ln_z_b=m_ln_z_b, w_s=m_w_s, b_s=m_b_s, w_o=m_w_o, ln1_g=m_ln1_g,
             ln1_b=m_ln1_b, w_ff_a=m_w_ff_a, w_ff_b=m_w_ff_b, conv_w=m_conv_w, conv_b=m_conv_b,
             w_ff_down=m_w_ff_down, ln2_g=m_ln2_g, ln2_b=m_ln2_b, w_ple_gate=m_w_ple_gate,
             b_ple_gate=m_b_ple_gate, w_ple_in=m_w_ple_in, ln3_g=m_ln3_g, ln3_b=m_ln3_b)
    v = dict(w_in=v_w_in, ln_z_g=v_ln_z_g, ln_z_b=v_ln_z_b, w_s=v_w_s, b_s=v_b_s, w_o=v_w_o, ln1_g=v_ln1_g,
             ln1_b=v_ln1_b, w_ff_a=v_w_ff_a, w_ff_b=v_w_ff_b, conv_w=v_conv_w, conv_b=v_conv_b,
             w_ff_down=v_w_ff_down, ln2_g=v_ln2_g, ln2_b=v_ln2_b, w_ple_gate=v_w_ple_gate,
             b_ple_gate=v_b_ple_gate, w_ple_in=v_w_ple_in, ln3_g=v_ln3_g, ln3_b=v_ln3_b)
    big_names = [name for name, _, _ in BIG]
    small_names = [name for name, _ in SMALL]

    gathered = _all_gather_weights(_pack_weight_shards(w))
    wb = _unpack_gathered(gathered)
    sm = {n: w[n][0] if w[n].ndim > 2 else w[n] for n in small_names}
    pos_col = positions.reshape(S, 1).astype(F32)
    loss_part, grad_x, grads = _local_step(x[0], p[0, 0], pos_col, loss_target[0], wb, sm)

    g_small = _pack_rows([grads[n] for n in small_names], SMALL_ROWS, F32)
    recv_big, recv_small = _exchange_grads(_pack_grads_by_owner(grads), g_small)

    big = _sum_adamw(recv_big, _pack_rows([w[n] for n in big_names], BIG_ROWS, F32),
                     _pack_rows([m[n] for n in big_names], BIG_ROWS, F32),
                     _pack_rows([v[n] for n in big_names], BIG_ROWS, F32), BIG_TILE, "adamw_sharded")
    small = _sum_adamw(recv_small, _pack_rows([w[n] for n in small_names], SMALL_ROWS, F32),
                       _pack_rows([m[n] for n in small_names], SMALL_ROWS, F32),
                       _pack_rows([v[n] for n in small_names], SMALL_ROWS, F32), SMALL_ROWS, "adamw_replicated")
    big_layout = [(name, (1,) + shape) for name, shape, _ in BIG]
    results = []
    for kind in range(4):
        leaves = {**_unpack(big[kind], big_layout), **_unpack(small[kind], SMALL)}
        results.append([leaves[n] for n in WEIGHT_ORDER])
    loss = lax.psum(loss_part, ("x", "y", "c"))
    return (loss, grad_x[None], *results[0], *results[1], *results[2], *results[3])
```

```python
import math

import numpy as np
import jax
import jax.numpy as jnp
from jax import lax
from jax.experimental import pallas as pl
from jax.experimental.pallas import tpu as pltpu

F32 = jnp.float32
BF16 = jnp.bfloat16
MESH = pl.DeviceIdType.MESH

N_DEV = 8
S = 4096
D = 1024
D_HALF = 512
D_IN = 2560
D_FF = 2816
D_PLE = 256
CHUNK = 128
DILATIONS = ((1, 32), (4, 8), (16, 2))
ROPE_THETA = 500000.0
LN_EPS = 1e-5
ALPHA = 2.0 ** 0.25
NEG_INF = -1e30
INV_SQRT2 = 1.0 / math.sqrt(2.0)
INV_SQRT_2PI = 1.0 / math.sqrt(2.0 * math.pi)

ADAM_LR, ADAM_B1, ADAM_B2, ADAM_EPS, ADAM_WD, ADAM_STEP = 0.001, 0.9, 0.999, 1e-08, 0.01, 10

TM = 512
NT = S // TM
ATTN_UNROLL = 4
TN = 256
NJ = D_FF // TN
LANES = 128
VMEM_MIB = 1024 * 1024

BIG = (("w_in", (1024, 320), 1), ("w_o", (128, 1024), 0), ("w_ff_a", (1024, 352), 1),
       ("w_ff_b", (1024, 352), 1), ("w_ff_down", (352, 1024), 0), ("w_ple_gate", (128, 1024), 0),
       ("w_ple_in", (256, 128), 1), ("conv_w", (3, 352), 1))
BIG_N = sum(r * c for _, (r, c), _ in BIG)
CONV_W_N = 3 * 352
BIG_ROWS = -(-(BIG_N + 2 * CONV_W_N) // (LANES * 16 * 7)) * 16 * 7
BIG_TILE = BIG_ROWS // 7
SMALL = (("ln_z_g", (1, 512)), ("ln_z_b", (1, 512)), ("w_s", (1, 8, 128, 128)), ("b_s", (1, 8, 128)),
         ("ln1_g", (1, 1024)), ("ln1_b", (1, 1024)), ("conv_b", (1, 2816)), ("ln2_g", (1, 1024)),
         ("ln2_b", (1, 1024)), ("b_ple_gate", (1, 1024)), ("ln3_g", (1, 1024)), ("ln3_b", (1, 1024)))
SMALL_N = sum(int(np.prod(s)) for _, s in SMALL)
SMALL_ROWS = -(-SMALL_N // (LANES * 8)) * 8
WEIGHT_ORDER = ("w_in", "ln_z_g", "ln_z_b", "w_s", "b_s", "w_o", "ln1_g", "ln1_b", "w_ff_a", "w_ff_b",
                "conv_w", "conv_b", "w_ff_down", "ln2_g", "ln2_b", "w_ple_gate", "b_ple_gate",
                "w_ple_in", "ln3_g", "ln3_b")


def _params(semantics=None, vmem_mib=48):
    return pltpu.CompilerParams(dimension_semantics=semantics, vmem_limit_bytes=vmem_mib * VMEM_MIB)


def _dot(a, b):
    return jnp.dot(a, b, preferred_element_type=F32)


def _dot_nt(a, b):
    return lax.dot_general(a, b, (((1,), (1,)), ((), ())), preferred_element_type=F32)


def _dot_tn(a, b):
    return lax.dot_general(a, b, (((0,), (0,)), ((), ())), preferred_element_type=F32)


def _gelu(x):
    return 0.5 * x * (1.0 + lax.erf(x * INV_SQRT2))


def _gelu_grad(x):
    return 0.5 * (1.0 + lax.erf(x * INV_SQRT2)) + x * (jnp.exp(-0.5 * x * x) * INV_SQRT_2PI)


def _ln_stats(y):
    mu = jnp.mean(y, axis=-1, keepdims=True)
    yc = y - mu
    var = jnp.mean(yc * yc, axis=-1, keepdims=True)
    rstd = lax.rsqrt(var + LN_EPS)
    return yc * rstd, rstd


def _ln_bwd(dxhat, xhat, rstd):
    m1 = jnp.mean(dxhat, axis=-1, keepdims=True)
    m2 = jnp.mean(dxhat * xhat, axis=-1, keepdims=True)
    return rstd * (dxhat - m1 - xhat * m2)


def _colsum(x):
    return jnp.sum(x, axis=0, keepdims=True)


def _rows(i):
    return (i, 0)


def _fixed(*_):
    return (0, 0)


def _row_spec(width):
    return pl.BlockSpec((TM, width), _rows)


def _full_spec(shape):
    return pl.BlockSpec(shape, lambda *_: (0,) * len(shape))


def _lane_lo():
    return lax.broadcasted_iota(jnp.int32, (CHUNK, LANES), 1) < 64


def _tril():
    r = lax.broadcasted_iota(jnp.int32, (CHUNK, CHUNK), 0)
    c = lax.broadcasted_iota(jnp.int32, (CHUNK, CHUNK), 1)
    return c <= r


def _rope_consts():
    lane = np.arange(LANES) % 64
    j = lane % 8
    inv = np.where(lane < 16, np.float32(ROPE_THETA) ** (-(2.0 * j).astype(np.float32) / np.float32(16.0)), 0.0)
    m_lo = (lane < 8).astype(np.float32)
    m_hi = ((lane >= 8) & (lane < 16)).astype(np.float32)
    return (jnp.asarray(inv, F32).reshape(1, LANES), jnp.asarray(m_lo).reshape(1, LANES),
            jnp.asarray(m_hi).reshape(1, LANES))


def _rope_tables(pos_col):
    inv, m_lo, m_hi = _rope_consts()

    def body(pos_ref, inv_ref, lo_ref, hi_ref, c_ref, sa_ref, sb_ref):
        ang = pos_ref[...] * inv_ref[...]
        c = jnp.cos(ang)
        s = jnp.sin(ang)
        lo = lo_ref[...]
        hi = hi_ref[...]
        c_ref[...] = jnp.where(lo + hi > 0.0, c, 1.0)
        sa_ref[...] = s * hi
        sb_ref[...] = -s * lo

    vec = _full_spec((1, LANES))
    out = jax.ShapeDtypeStruct((S, LANES), F32)
    return pl.pallas_call(
        body, name="rope_tables", grid=(NT,), out_shape=(out, out, out),
        in_specs=[pl.BlockSpec((TM, 1), _rows), vec, vec, vec],
        out_specs=(_row_spec(LANES),) * 3, compiler_params=_params(("parallel",)),
    )(pos_col, inv, m_lo, m_hi)


def _rope(t, c, sa, sb):
    return t * c + pltpu.roll(t, 8, 1) * sa + pltpu.roll(t, LANES - 8, 1) * sb


def _rope_t(dy, c, sa, sb):
    return dy * c + pltpu.roll(dy * sa, LANES - 8, 1) + pltpu.roll(dy * sb, 8, 1)


def _masked_ws(ws_ref):
    tril = _tril()
    return [jnp.where(tril, ws_ref[g], 0.0).astype(BF16) for g in range(8)]


def _spatial_mix(zn, wm, bs):
    lo = _lane_lo()
    rows = []
    for ch in range(TM // CHUNK):
        slabs = []
        for pr in range(4):
            zp = zn[ch * CHUNK:(ch + 1) * CHUNK, pr * LANES:(pr + 1) * LANES].astype(BF16)
            slabs.append(jnp.where(lo, _dot(wm[2 * pr], zp), _dot(wm[2 * pr + 1], zp)))
        rows.append(jnp.concatenate(slabs, axis=1) + bs)
    return jnp.concatenate(rows, axis=0)


def _proj_in_fwd(x, w_in, tabs, ln_z_g, ln_z_b, w_s, bs_exp):
    def body(x_ref, w_ref, c_ref, sa_ref, sb_ref, g_ref, b_ref, ws_ref, bs_ref,
             q_ref, k_ref, v_ref, u_ref, z_ref, gm_ref, xb_ref):
        xb = x_ref[...].astype(BF16)
        xb_ref[...] = xb
        c, sa, sb = c_ref[...], sa_ref[...], sb_ref[...]
        hq = _dot(xb, w_ref[:, 0:512])
        hk = _dot(xb, w_ref[:, 512:1024])
        for s in range(4):
            sl = slice(s * LANES, (s + 1) * LANES)
            q_ref[:, sl] = _rope(hq[:, sl], c, sa, sb)
            k_ref[:, sl] = _rope(hk[:, sl], c, sa, sb)
        v_ref[...] = _dot(xb, w_ref[:, 1024:1536])
        u_pre = _dot(xb, w_ref[:, 1536:2048])
        z_pre = _dot(xb, w_ref[:, 2048:2560])
        u_ref[...] = u_pre
        z_ref[...] = z_pre
        zhat, _ = _ln_stats(_gelu(z_pre))
        zn = zhat * g_ref[...] + b_ref[...]
        mixed = _spatial_mix(zn, _masked_ws(ws_ref), bs_ref[...])
        gm_ref[...] = (_gelu(u_pre) * mixed).astype(BF16)

    half = jax.ShapeDtypeStruct((S, D_HALF), F32)
    tab = _row_spec(LANES)
    return pl.pallas_call(
        body, name="proj_in_fwd", grid=(NT,),
        out_shape=(half, half, half, half, half, jax.ShapeDtypeStruct((S, D_HALF), BF16),
                   jax.ShapeDtypeStruct((S, D), BF16)),
        in_specs=[_row_spec(D), _full_spec((D, D_IN)), tab, tab, tab, _full_spec((1, D_HALF)),
                  _full_spec((1, D_HALF)), _full_spec((8, CHUNK, CHUNK)), _full_spec((CHUNK, D_HALF))],
        out_specs=(_row_spec(D_HALF),) * 6 + (_row_spec(D),),
        compiler_params=_params(("parallel",)),
    )(x, w_in, *tabs, ln_z_g, ln_z_b, w_s, bs_exp)


def _store_band_bias(bias_ref):
    qi = lax.broadcasted_iota(jnp.int32, (CHUNK, 2 * CHUNK), 0)
    kj = lax.broadcasted_iota(jnp.int32, (CHUNK, 2 * CHUNK), 1)
    band = (kj >= qi) & (kj <= qi + CHUNK)
    bias_ref[0] = jnp.where(band, 0.0, NEG_INF)
    bias_ref[1] = jnp.where(band & (kj >= CHUNK), 0.0, NEG_INF)


def _permuted_rows(ref, d, r):
    return ref[...] if d == 1 else ref[pl.ds(r, S // d, stride=d), :]


def _attention_fwd(q, k, v):
    def body(q_ref, k_ref, v_ref, o_ref, lse_ref, qb, kb, v0b, v1b, bias, op, lp, ob0, lb0, ob1, lb1, ob2, lb2):
        lo = _lane_lo()
        lo_f = lo.astype(F32)[0:1, :]
        hi_f = 1.0 - lo_f
        zero_pad = jnp.zeros((CHUNK, LANES), BF16)
        for buf in (qb, kb, v0b, v1b):
            buf[0:CHUNK, :] = zero_pad
        _store_band_bias(bias)
        outs = ((ob0, lb0), (ob1, lb1), (ob2, lb2))
        for (d, nb), (ob, lb) in zip(DILATIONS, outs):
            length = S // d
            for r in range(d):
                dst = slice(CHUNK + r * length, CHUNK + (r + 1) * length)
                qb[dst, :] = (_permuted_rows(q_ref, d, r) * 0.125).astype(BF16)
                kb[dst, :] = _permuted_rows(k_ref, d, r).astype(BF16)
                vs = _permuted_rows(v_ref, d, r)
                v0b[dst, :] = (vs * lo_f + hi_f).astype(BF16)
                v1b[dst, :] = (vs * hi_f + lo_f).astype(BF16)

            def block(b, carry, nb=nb):
                base = pl.multiple_of(b * CHUNK, CHUNK)
                add = bias[jnp.where(b % nb == 0, 1, 0)]
                qblk = qb[pl.ds(pl.multiple_of(base + CHUNK, CHUNK), CHUNK), :]
                kblk = kb[pl.ds(base, 2 * CHUNK), :]
                pv, mx = [], []
                for head, vh in enumerate((v0b, v1b)):
                    qh = jnp.where(lo, qblk, 0) if head == 0 else jnp.where(lo, 0, qblk)
                    s = _dot_nt(qh, kblk) + add
                    m = jnp.max(s, axis=-1, keepdims=True)
                    p = jnp.exp(s - m).astype(BF16)
                    pv.append(_dot(p, vh[pl.ds(base, 2 * CHUNK), :]))
                    mx.append(m)
                den = pltpu.roll(jnp.where(lo, pv[1], pv[0]), 64, 1)
                op[pl.ds(base, CHUNK), :] = jnp.where(lo, pv[0], pv[1]) / den
                lp[pl.ds(base, CHUNK), :] = jnp.where(lo, mx[0], mx[1]) + jnp.log(den)
                return carry

            lax.fori_loop(0, S // CHUNK, block, 0, unroll=ATTN_UNROLL)
            for r in range(d):
                src = slice(r * length, (r + 1) * length)
                if d == 1:
                    ob[...] = op[...]
                    lb[...] = lp[...]
                else:
                    ob[pl.ds(r, length, stride=d), :] = op[src, :]
                    lb[pl.ds(r, length, stride=d), :] = lp[src, :]
        for t in range(NT):
            rows = slice(t * TM, (t + 1) * TM)
            l0, l1, l2 = lb0[rows, :], lb1[rows, :], lb2[rows, :]
            mx = jnp.maximum(jnp.maximum(l0, l1), l2)
            e0, e1, e2 = jnp.exp(l0 - mx), jnp.exp(l1 - mx), jnp.exp(l2 - mx)
            den = e0 + e1 + e2
            o_ref[rows, :] = (e0 * ob0[rows, :] + e1 * ob1[rows, :] + e2 * ob2[rows, :]) / den
            lse_ref[rows, :] = mx + jnp.log(den)

    slab = pl.BlockSpec((S, LANES), lambda h: (0, h))
    out = jax.ShapeDtypeStruct((S, D_HALF), F32)
    padded = pltpu.VMEM((CHUNK + S, LANES), BF16)
    whole = pltpu.VMEM((S, LANES), F32)
    return pl.pallas_call(
        body, name="attention_fwd", grid=(4,), out_shape=(out, out),
        in_specs=[slab, slab, slab], out_specs=(slab, slab),
        scratch_shapes=[padded] * 4 + [pltpu.VMEM((2, CHUNK, 2 * CHUNK), F32)] + [whole] * 8,
        compiler_params=_params(("parallel",), 56),
    )(q, k, v)


def _mix_ln1_fwd(attn, gm, w_o, x, g1, b1):
    def body(a_ref, gm_ref, w_ref, x_ref, g_ref, b_ref, xhat_ref, rstd_ref, x1b_ref):
        mix = _dot(a_ref[...].astype(BF16), w_ref[0:D_HALF, :]) + _dot(gm_ref[...], w_ref[D_HALF:D, :])
        xhat, rstd = _ln_stats(ALPHA * x_ref[...] + mix)
        xhat_ref[...] = xhat
        rstd_ref[...] = rstd
        x1b_ref[...] = (xhat * g_ref[...] + b_ref[...]).astype(BF16)

    vec = _full_spec((1, D))
    return pl.pallas_call(
        body, name="mix_ln1_fwd", grid=(NT,),
        out_shape=(jax.ShapeDtypeStruct((S, D), F32), jax.ShapeDtypeStruct((S, 1), F32),
                   jax.ShapeDtypeStruct((S, D), BF16)),
        in_specs=[_row_spec(D_HALF), _row_spec(D_HALF), _full_spec((D, D)), _row_spec(D), vec, vec],
        out_specs=(_row_spec(D), pl.BlockSpec((TM, 1), _rows), _row_spec(D)),
        compiler_params=_params(("parallel",)),
    )(attn, gm, w_o, x, g1, b1)


def _ffn_up_fwd(x1b, w_a, w_b, conv_w8, conv_b):
    def body(x_ref, wa_ref, wb_ref, cw_ref, cb_ref, ap_ref, a_ref, bl_ref, h_ref, carry):
        @pl.when(pl.program_id(1) == 0)
        def _():
            carry[...] = jnp.zeros_like(carry)

        xb = x_ref[...]
        ap = _dot(xb, wa_ref[...])
        bl = _dot(xb, wb_ref[...])
        row = lax.broadcasted_iota(jnp.int32, (TM, TN), 0)
        c6, c7 = carry[6:7, :], carry[7:8, :]
        m1 = jnp.where(row == 0, c7, pltpu.roll(ap, 1, 0))
        m2 = jnp.where(row == 0, c6, jnp.where(row == 1, c7, pltpu.roll(ap, 2, 0)))
        a = cb_ref[...] + cw_ref[0:1, :] * m2 + cw_ref[1:2, :] * m1 + cw_ref[2:3, :] * ap
        carry[...] = ap[TM - 8:TM, :]
        ap_ref[...] = ap
        a_ref[...] = a
        bl_ref[...] = bl
        h_ref[...] = (_gelu(a) * bl).astype(BF16)

    tile = pl.BlockSpec((TM, TN), lambda j, i: (i, j))
    wcol = pl.BlockSpec((D, TN), lambda j, i: (0, j))
    ff = jax.ShapeDtypeStruct((S, D_FF), F32)
    return pl.pallas_call(
        body, name="ffn_up_fwd", grid=(NJ, NT),
        out_shape=(ff, ff, ff, jax.ShapeDtypeStruct((S, D_FF), BF16)),
        in_specs=[pl.BlockSpec((TM, D), lambda j, i: (i, 0)), wcol, wcol,
                  pl.BlockSpec((8, TN), lambda j, i: (0, j)), pl.BlockSpec((1, TN), lambda j, i: (0, j))],
        out_specs=(tile, tile, tile, tile),
        scratch_shapes=[pltpu.VMEM((8, TN), F32)],
        compiler_params=_params(("parallel", "arbitrary")),
    )(x1b, w_a, w_b, conv_w8, conv_b)


def _ffn_down_ln2_fwd(hff, w_down, xhat1, g1, b1):
    def body(h_ref, w_ref, xh_ref, g_ref, b_ref, xhat_ref, rstd_ref):
        x1 = xh_ref[...] * g_ref[...] + b_ref[...]
        xhat, rstd = _ln_stats(ALPHA * x1 + _dot(h_ref[...], w_ref[...]))
        xhat_ref[...] = xhat
        rstd_ref[...] = rstd

    vec = _full_spec((1, D))
    return pl.pallas_call(
        body, name="ffn_down_ln2_fwd", grid=(NT,),
        out_shape=(jax.ShapeDtypeStruct((S, D), F32), jax.ShapeDtypeStruct((S, 1), F32)),
        in_specs=[_row_spec(D_FF), _full_spec((D_FF, D)), _row_spec(D), vec, vec],
        out_specs=(_row_spec(D), pl.BlockSpec((TM, 1), _rows)),
        compiler_params=_params(("parallel",)),
    )(hff, w_down, xhat1, g1, b1)


def _tail_fwd_bwd(xhat2, rstd2, p, target, w_g, w_p, g2, b2, bg, g3, b3):
    def body(xh_ref, rs_ref, p_ref, t_ref, wg_ref, wp_ref, g2_ref, b2_ref, bg_ref, g3_ref, b3_ref,
             loss_ref, dy2_ref, dy2b_ref, dwg_ref, dwp_ref, vec_ref):
        @pl.when(pl.program_id(0) == 0)
        def _():
            loss_ref[...] = jnp.zeros_like(loss_ref)
            dwg_ref[...] = jnp.zeros_like(dwg_ref)
            dwp_ref[...] = jnp.zeros_like(dwp_ref)
            vec_ref[...] = jnp.zeros_like(vec_ref)

        xhat2_t = xh_ref[...]
        x2 = xhat2_t * g2_ref[...] + b2_ref[...]
        x2b = x2.astype(BF16)
        pb = p_ref[...].astype(BF16)
        gate = jax.nn.sigmoid(_dot(x2b, wg_ref[...]) + bg_ref[...])
        pin = _dot(pb, wp_ref[...])
        xhat3, rstd3 = _ln_stats(ALPHA * x2 + gate * pin)
        err = xhat3 * g3_ref[...] + b3_ref[...] - t_ref[...]
        loss_ref[...] += jnp.sum(jnp.mean(err * err, axis=-1, keepdims=True), axis=0, keepdims=True) * 0.5
        dout = err * (1.0 / D)
        dy3 = _ln_bwd(dout * g3_ref[...], xhat3, rstd3)
        dgp = dy3 * pin * gate * (1.0 - gate)
        dgpb = dgp.astype(BF16)
        dwg_ref[...] += _dot_tn(x2b, dgpb)
        dwp_ref[...] += _dot_tn(pb, (dy3 * gate).astype(BF16))
        dx2 = ALPHA * dy3 + _dot_nt(dgpb, wg_ref[...])
        dy2 = _ln_bwd(dx2 * g2_ref[...], xhat2_t, rs_ref[...])
        dy2_ref[...] = dy2
        dy2b_ref[...] = dy2.astype(BF16)
        vec_ref[0:1, :] += _colsum(dgp)
        vec_ref[1:2, :] += _colsum(dout * xhat3)
        vec_ref[2:3, :] += _colsum(dout)
        vec_ref[3:4, :] += _colsum(dx2 * xhat2_t)
        vec_ref[4:5, :] += _colsum(dx2)

    vec = _full_spec((1, D))
    return pl.pallas_call(
        body, name="tail_fwd_bwd", grid=(NT,),
        out_shape=(jax.ShapeDtypeStruct((1, LANES), F32), jax.ShapeDtypeStruct((S, D), F32),
                   jax.ShapeDtypeStruct((S, D), BF16), jax.ShapeDtypeStruct((D, D), F32),
                   jax.ShapeDtypeStruct((D_PLE, D), F32), jax.ShapeDtypeStruct((8, D), F32)),
        in_specs=[_row_spec(D), pl.BlockSpec((TM, 1), _rows), _row_spec(D_PLE), _row_spec(D),
                  _full_spec((D, D)), _full_spec((D_PLE, D)), vec, vec, vec, vec, vec],
        out_specs=(_full_spec((1, LANES)), _row_spec(D), _row_spec(D), _full_spec((D, D)),
                   _full_spec((D_PLE, D)), _full_spec((8, D))),
        compiler_params=_params(("arbitrary",)),
    )(xhat2, rstd2, p, target, w_g, w_p, g2, b2, bg, g3, b3)


def _ffn_bwd(dy2b, x1b, w_down, w_a, w_b, a_pre, a, b_lin, hff, conv_w8):
    def body(dy_ref, x_ref, wd_ref, wa_ref, wb_ref, ap_ref, a_ref, bl_ref, h_ref, cw_ref,
             dwd_ref, dwa_ref, dwb_ref, dcw_ref, dcb_ref, dx_hbm, dx_acc, carry, sem):
        j, i = pl.program_id(0), pl.program_id(1)

        @pl.when(i == 0)
        def _():
            carry[...] = jnp.zeros_like(carry)
            dwd_ref[...] = jnp.zeros_like(dwd_ref)
            dwa_ref[...] = jnp.zeros_like(dwa_ref)
            dwb_ref[...] = jnp.zeros_like(dwb_ref)
            dcw_ref[...] = jnp.zeros_like(dcw_ref)
            dcb_ref[...] = jnp.zeros_like(dcb_ref)

        dyb = dy_ref[...]
        xb = x_ref[...]
        dh = _dot_nt(dyb, wd_ref[...])
        av = a_ref[...]
        dbl = dh * _gelu(av)
        da = dh * bl_ref[...] * _gelu_grad(av)
        row = lax.broadcasted_iota(jnp.int32, (TM, TN), 0)
        c0, c1 = carry[0:1, :], carry[1:2, :]
        p1 = jnp.where(row == TM - 1, c0, pltpu.roll(da, TM - 1, 0))
        p2 = jnp.where(row == TM - 2, c0, jnp.where(row == TM - 1, c1, pltpu.roll(da, TM - 2, 0)))
        carry[...] = da[0:8, :]
        ap = ap_ref[...]
        dcb_ref[...] += _colsum(da)
        dcw_ref[0:1, :] += _colsum(ap * p2)
        dcw_ref[1:2, :] += _colsum(ap * p1)
        dcw_ref[2:3, :] += _colsum(ap * da)
        dap = (cw_ref[2:3, :] * da + cw_ref[1:2, :] * p1 + cw_ref[0:1, :] * p2).astype(BF16)
        dblb = dbl.astype(BF16)
        dwa_ref[...] += _dot_tn(xb, dap)
        dwb_ref[...] += _dot_tn(xb, dblb)
        dwd_ref[...] += _dot_tn(h_ref[...], dyb)
        dx = _dot_nt(dap, wa_ref[...]) + _dot_nt(dblb, wb_ref[...])
        rows = pl.ds(pl.multiple_of((NT - 1 - i) * TM, TM), TM)

        @pl.when(j == 0)
        def _():
            dx_acc[rows, :] = dx

        @pl.when(j > 0)
        def _():
            dx_acc[rows, :] += dx

        @pl.when((j == NJ - 1) & (i == NT - 1))
        def _():
            cp = pltpu.make_async_copy(dx_acc, dx_hbm, sem)
            cp.start()
            cp.wait()

    rev_rows = lambda j, i: (NT - 1 - i, 0)
    rev_tile = pl.BlockSpec((TM, TN), lambda j, i: (NT - 1 - i, j))
    wcol = pl.BlockSpec((D, TN), lambda j, i: (0, j))
    small = pl.BlockSpec((8, TN), lambda j, i: (0, j))
    return pl.pallas_call(
        body, name="ffn_bwd", grid=(NJ, NT),
        out_shape=(jax.ShapeDtypeStruct((D_FF, D), F32), jax.ShapeDtypeStruct((D, D_FF), F32),
                   jax.ShapeDtypeStruct((D, D_FF), F32), jax.ShapeDtypeStruct((8, D_FF), F32),
                   jax.ShapeDtypeStruct((1, D_FF), F32), jax.ShapeDtypeStruct((S, D), F32)),
        in_specs=[pl.BlockSpec((TM, D), rev_rows), pl.BlockSpec((TM, D), rev_rows),
                  pl.BlockSpec((TN, D), lambda j, i: (j, 0)), wcol, wcol, rev_tile, rev_tile, rev_tile,
                  rev_tile, small],
        out_specs=(pl.BlockSpec((TN, D), lambda j, i: (j, 0)), wcol, wcol, small,
                   pl.BlockSpec((1, TN), lambda j, i: (0, j)), pl.BlockSpec(memory_space=pl.ANY)),
        scratch_shapes=[pltpu.VMEM((S, D), F32), pltpu.VMEM((8, TN), F32), pltpu.SemaphoreType.DMA],
        compiler_params=_params(("arbitrary", "arbitrary"), 56),
    )(dy2b, x1b, w_down, w_a, w_b, a_pre, a, b_lin, hff, conv_w8)


def _ln1_mix_bwd(dy2, dx1_ffn, xhat1, rstd1, g1, attn, gm, w_o):
    def body(dy2_ref, dxf_ref, xh_ref, rs_ref, g_ref, a_ref, gm_ref, w_ref,
             dy1_ref, da_ref, dlt_ref, dgm_ref, dwo_ref, vec_ref):
        @pl.when(pl.program_id(0) == 0)
        def _():
            dwo_ref[...] = jnp.zeros_like(dwo_ref)
            vec_ref[...] = jnp.zeros_like(vec_ref)

        xhat = xh_ref[...]
        dx1 = ALPHA * dy2_ref[...] + dxf_ref[...]
        vec_ref[0:1, :] += _colsum(dx1 * xhat)
        vec_ref[1:2, :] += _colsum(dx1)
        dy1 = _ln_bwd(dx1 * g_ref[...], xhat, rs_ref[...])
        dy1_ref[...] = dy1
        dy1b = dy1.astype(BF16)
        dmix = _dot_nt(dy1b, w_ref[...])
        attn_t = a_ref[...]
        d_attn = dmix[:, 0:D_HALF]
        da_ref[...] = d_attn
        dgm_ref[...] = dmix[:, D_HALF:D]
        lo = (lax.broadcasted_iota(jnp.int32, (TM, LANES), 1) < 64)
        for s in range(4):
            sl = slice(s * LANES, (s + 1) * LANES)
            prod = d_attn[:, sl] * attn_t[:, sl]
            s0 = jnp.sum(jnp.where(lo, prod, 0.0), axis=-1, keepdims=True)
            s1 = jnp.sum(jnp.where(lo, 0.0, prod), axis=-1, keepdims=True)
            dlt_ref[:, sl] = jnp.where(lo, s0, s1)
        dwo_ref[0:D_HALF, :] += _dot_tn(attn_t.astype(BF16), dy1b)
        dwo_ref[D_HALF:D, :] += _dot_tn(gm_ref[...], dy1b)

    half = jax.ShapeDtypeStruct((S, D_HALF), F32)
    return pl.pallas_call(
        body, name="ln1_mix_bwd", grid=(NT,),
        out_shape=(jax.ShapeDtypeStruct((S, D), F32), half, half, half,
                   jax.ShapeDtypeStruct((D, D), F32), jax.ShapeDtypeStruct((8, D), F32)),
        in_specs=[_row_spec(D), _row_spec(D), _row_spec(D), pl.BlockSpec((TM, 1), _rows), _full_spec((1, D)),
                  _row_spec(D_HALF), _row_spec(D_HALF), _full_spec((D, D))],
        out_specs=(_row_spec(D), _row_spec(D_HALF), _row_spec(D_HALF), _row_spec(D_HALF),
                   _full_spec((D, D)), _full_spec((8, D))),
        compiler_params=_params(("arbitrary",)),
    )(dy2, dx1_ffn, xhat1, rstd1, g1, attn, gm, w_o)


def _gmlp_bwd(d_gm, u_pre, z_pre, ln_z_g, ln_z_b, w_s, bs_exp):
    def body(dg_ref, u_ref, z_ref, g_ref, b_ref, ws_ref, bs_ref, du_ref, dz_ref, dws_ref, dbs_ref, vec_ref):
        @pl.when(pl.program_id(0) == 0)
        def _():
            dws_ref[...] = jnp.zeros_like(dws_ref)
            dbs_ref[...] = jnp.zeros_like(dbs_ref)
            vec_ref[...] = jnp.zeros_like(vec_ref)

        u_pre_t, z_pre_t, dgm = u_ref[...], z_ref[...], dg_ref[...]
        zhat, rstd = _ln_stats(_gelu(z_pre_t))
        zn = zhat * g_ref[...] + b_ref[...]
        wm = _masked_ws(ws_ref)
        mixed = _spatial_mix(zn, wm, bs_ref[...])
        du_ref[...] = (dgm * mixed * _gelu_grad(u_pre_t)).astype(BF16)
        dmixed = dgm * _gelu(u_pre_t)
        lo = _lane_lo()
        tril = _tril()
        dzn_rows = []
        for ch in range(TM // CHUNK):
            rows = slice(ch * CHUNK, (ch + 1) * CHUNK)
            dbs_ref[...] += dmixed[rows, :]
            slabs = []
            for pr in range(4):
                sl = slice(pr * LANES, (pr + 1) * LANES)
                dm = dmixed[rows, sl]
                zp = zn[rows, sl].astype(BF16)
                dm_lo = jnp.where(lo, dm, 0.0).astype(BF16)
                dm_hi = jnp.where(lo, 0.0, dm).astype(BF16)
                dws_ref[2 * pr] += jnp.where(tril, _dot_nt(dm_lo, zp), 0.0)
                dws_ref[2 * pr + 1] += jnp.where(tril, _dot_nt(dm_hi, zp), 0.0)
                dmb = dm.astype(BF16)
                slabs.append(jnp.where(lo, _dot_tn(wm[2 * pr], dmb), _dot_tn(wm[2 * pr + 1], dmb)))
            dzn_rows.append(jnp.concatenate(slabs, axis=1))
        dzn = jnp.concatenate(dzn_rows, axis=0)
        vec_ref[0:1, :] += _colsum(dzn * zhat)
        vec_ref[1:2, :] += _colsum(dzn)
        dz = _ln_bwd(dzn * g_ref[...], zhat, rstd)
        dz_ref[...] = (dz * _gelu_grad(z_pre_t)).astype(BF16)

    halfb = jax.ShapeDtypeStruct((S, D_HALF), BF16)
    vec = _full_spec((1, D_HALF))
    return pl.pallas_call(
        body, name="gmlp_bwd", grid=(NT,),
        out_shape=(halfb, halfb, jax.ShapeDtypeStruct((8, CHUNK, CHUNK), F32),
                   jax.ShapeDtypeStruct((CHUNK, D_HALF), F32), jax.ShapeDtypeStruct((8, D_HALF), F32)),
        in_specs=[_row_spec(D_HALF), _row_spec(D_HALF), _row_spec(D_HALF), vec, vec,
                  _full_spec((8, CHUNK, CHUNK)), _full_spec((CHUNK, D_HALF))],
        out_specs=(_row_spec(D_HALF), _row_spec(D_HALF), _full_spec((8, CHUNK, CHUNK)),
                   _full_spec((CHUNK, D_HALF)), _full_spec((8, D_HALF))),
        compiler_params=_params(("arbitrary",)),
    )(d_gm, u_pre, z_pre, ln_z_g, ln_z_b, w_s, bs_exp)


def _attention_bwd(q, k, v, lse, d_attn, delta, tabs):
    def body(q_ref, k_ref, v_ref, l_ref, do_ref, dl_ref, c_ref, sa_ref, sb_ref, dq_ref, dk_ref, dv_ref,
             qb, kb, vb, gb, bias, lsp, dlp, dqp, dk_own, dk_prev, dv_own, dv_prev, dqa, dka, dva):
        lo = _lane_lo()
        zero_pad = jnp.zeros((CHUNK, LANES), BF16)
        for buf in (qb, kb, vb, gb):
            buf[0:CHUNK, :] = zero_pad
        dk_prev[S:S + CHUNK, :] = jnp.zeros((CHUNK, LANES), F32)
        dv_prev[S:S + CHUNK, :] = jnp.zeros((CHUNK, LANES), F32)
        _store_band_bias(bias)
        for d, nb in DILATIONS:
            length = S // d
            for r in range(d):
                dst = slice(CHUNK + r * length, CHUNK + (r + 1) * length)
                src = slice(r * length, (r + 1) * length)
                qb[dst, :] = (_permuted_rows(q_ref, d, r) * 0.125).astype(BF16)
                kb[dst, :] = _permuted_rows(k_ref, d, r).astype(BF16)
                vb[dst, :] = _permuted_rows(v_ref, d, r).astype(BF16)
                gb[dst, :] = _permuted_rows(do_ref, d, r).astype(BF16)
                lsp[src, :] = _permuted_rows(l_ref, d, r)
                dlp[src, :] = _permuted_rows(dl_ref, d, r)

            def block(b, carry, nb=nb):
                base = pl.multiple_of(b * CHUNK, CHUNK)
                own = pl.multiple_of(base + CHUNK, CHUNK)
                add = bias[jnp.where(b % nb == 0, 1, 0)]
                qblk = qb[pl.ds(own, CHUNK), :]
                gblk = gb[pl.ds(own, CHUNK), :]
                kblk = kb[pl.ds(base, 2 * CHUNK), :]
                vblk = vb[pl.ds(base, 2 * CHUNK), :]
                lse_t = lsp[pl.ds(base, CHUNK), :]
                dlt_t = dlp[pl.ds(base, CHUNK), :]
                dq, dk, dv = [], [], []
                for head in range(2):
                    qh = jnp.where(lo, qblk, 0) if head == 0 else jnp.where(lo, 0, qblk)
                    gh = jnp.where(lo, gblk, 0) if head == 0 else jnp.where(lo, 0, gblk)
                    col = 64 * head
                    p = jnp.exp(_dot_nt(qh, kblk) + add - lse_t[:, col:col + 1])
                    ds = (p * (_dot_nt(gh, vblk) - dlt_t[:, col:col + 1])).astype(BF16)
                    dv.append(_dot_tn(p.astype(BF16), gh))
                    dk.append(_dot_tn(ds, qh))
                    dq.append(_dot(ds, kblk))
                dqp[pl.ds(base, CHUNK), :] = jnp.where(lo, dq[0], dq[1]) * 0.125
                dk_blk = dk[0] + dk[1]
                dv_blk = dv[0] + dv[1]
                dk_prev[pl.ds(base, CHUNK), :] = dk_blk[0:CHUNK, :]
                dk_own[pl.ds(own, CHUNK), :] = dk_blk[CHUNK:2 * CHUNK, :]
                dv_prev[pl.ds(base, CHUNK), :] = dv_blk[0:CHUNK, :]
                dv_own[pl.ds(own, CHUNK), :] = dv_blk[CHUNK:2 * CHUNK, :]
                return carry

            lax.fori_loop(0, S // CHUNK, block, 0, unroll=ATTN_UNROLL)
            for r in range(d):
                src = slice(r * length, (r + 1) * length)
                pad = slice(CHUNK + r * length, CHUNK + (r + 1) * length)
                if d == 1:
                    dqa[...] = dqp[...]
                    dka[...] = dk_own[pad, :] + dk_prev[pad, :]
                    dva[...] = dv_own[pad, :] + dv_prev[pad, :]
                else:
                    dst = pl.ds(r, length, stride=d)
                    dqa[dst, :] = dqa[dst, :] + dqp[src, :]
                    dka[dst, :] = dka[dst, :] + (dk_own[pad, :] + dk_prev[pad, :])
                    dva[dst, :] = dva[dst, :] + (dv_own[pad, :] + dv_prev[pad, :])
        for t in range(NT):
            rows = slice(t * TM, (t + 1) * TM)
            c, sa, sb = c_ref[rows, :], sa_ref[rows, :], sb_ref[rows, :]
            dq_ref[rows, :] = _rope_t(dqa[rows, :], c, sa, sb).astype(BF16)
            dk_ref[rows, :] = _rope_t(dka[rows, :], c, sa, sb).astype(BF16)
            dv_ref[rows, :] = dva[rows, :].astype(BF16)

    slab = pl.BlockSpec((S, LANES), lambda h: (0, h), pipeline_mode=pl.Buffered(1))
    tab = pl.BlockSpec((S, LANES), lambda h: (0, 0), pipeline_mode=pl.Buffered(1))
    out_slab = pl.BlockSpec((S, LANES), lambda h: (0, h))
    out = jax.ShapeDtypeStruct((S, D_HALF), BF16)
    padded_b = pltpu.VMEM((CHUNK + S, LANES), BF16)
    padded_f = pltpu.VMEM((CHUNK + S, LANES), F32)
    whole = pltpu.VMEM((S, LANES), F32)
    return pl.pallas_call(
        body, name="attention_bwd", grid=(4,), out_shape=(out, out, out),
        in_specs=[slab] * 6 + [tab] * 3, out_specs=(out_slab,) * 3,
        scratch_shapes=[padded_b] * 4 + [pltpu.VMEM((2, CHUNK, 2 * CHUNK), F32)] + [whole] * 3
        + [padded_f] * 4 + [whole] * 3,
        compiler_params=_params(("parallel",), 60),
    )(q, k, v, lse, d_attn, delta, *tabs)


def _proj_in_bwd_w(xb, parts):
    def body(x_ref, p0, p1, p2, p3, p4, dw_ref):
        @pl.when(pl.program_id(0) == 0)
        def _():
            dw_ref[...] = jnp.zeros_like(dw_ref)

        xt = x_ref[...]
        for n, part in enumerate((p0, p1, p2, p3, p4)):
            dw_ref[:, n * D_HALF:(n + 1) * D_HALF] += _dot_tn(xt, part[...])

    return pl.pallas_call(
        body, name="proj_in_bwd_w", grid=(NT,), out_shape=jax.ShapeDtypeStruct((D, D_IN), F32),
        in_specs=[_row_spec(D)] + [_row_spec(D_HALF)] * 5, out_specs=_full_spec((D, D_IN)),
        compiler_params=_params(("arbitrary",)),
    )(xb, *parts)


def _proj_in_bwd_x(dy1, parts, w_in):
    def body(dy_ref, p0, p1, p2, p3, p4, w_ref, gx_ref):
        acc = ALPHA * dy_ref[...]
        for n, part in enumerate((p0, p1, p2, p3, p4)):
            acc += _dot_nt(part[...], w_ref[:, n * D_HALF:(n + 1) * D_HALF])
        gx_ref[...] = acc

    return pl.pallas_call(
        body, name="proj_in_bwd_x", grid=(NT,), out_shape=jax.ShapeDtypeStruct((S, D), F32),
        in_specs=[_row_spec(D)] + [_row_spec(D_HALF)] * 5 + [_full_spec((D, D_IN))], out_specs=_row_spec(D),
        compiler_params=_params(("parallel",)),
    )(dy1, *parts, w_in)


def _local_step(x, p, pos_col, target, wb, sm):
    bs_exp = jnp.repeat(sm["b_s"].T, 64, axis=1)
    conv_w8 = jnp.concatenate([wb["conv_w"], jnp.zeros((5, D_FF), F32)], axis=0)
    tabs = _rope_tables(pos_col)
    q, k, v, u_pre, z_pre, gm, xb = _proj_in_fwd(x, wb["w_in"], tabs, sm["ln_z_g"], sm["ln_z_b"],
                                                 sm["w_s"], bs_exp)
    attn, lse = _attention_fwd(q, k, v)
    xhat1, rstd1, x1b = _mix_ln1_fwd(attn, gm, wb["w_o"], x, sm["ln1_g"], sm["ln1_b"])
    a_pre, a, b_lin, hff = _ffn_up_fwd(x1b, wb["w_ff_a"], wb["w_ff_b"], conv_w8, sm["conv_b"])
    xhat2, rstd2 = _ffn_down_ln2_fwd(hff, wb["w_ff_down"], xhat1, sm["ln1_g"], sm["ln1_b"])
    loss, dy2, dy2b, dw_g, dw_p, vec_tail = _tail_fwd_bwd(
        xhat2, rstd2, p, target, wb["w_ple_gate"], wb["w_ple_in"], sm["ln2_g"], sm["ln2_b"],
        sm["b_ple_gate"], sm["ln3_g"], sm["ln3_b"])
    dw_down, dw_a, dw_b, dcw, dcb, dx1_ffn = _ffn_bwd(dy2b, x1b, wb["w_ff_down"], wb["w_ff_a"], wb["w_ff_b"],
                                                       a_pre, a, b_lin, hff, conv_w8)
    dy1, d_attn, delta, d_gm, dw_o, vec_ln1 = _ln1_mix_bwd(dy2, dx1_ffn, xhat1, rstd1, sm["ln1_g"], attn, gm,
                                                            wb["w_o"])
    du, dz, dws, dbs_exp, vec_z = _gmlp_bwd(d_gm, u_pre, z_pre, sm["ln_z_g"], sm["ln_z_b"], sm["w_s"], bs_exp)
    dq, dk, dv = _attention_bwd(q, k, v, lse, d_attn, delta, tabs)
    parts = (dq, dk, dv, du, dz)
    dw_in = _proj_in_bwd_w(xb, parts)
    grad_x = _proj_in_bwd_x(dy1, parts, wb["w_in"])
    grads = {
        "w_in": dw_in, "ln_z_g": vec_z[0:1], "ln_z_b": vec_z[1:2], "w_s": dws,
        "b_s": dbs_exp.reshape(CHUNK, 8, 64).sum(axis=2).T, "w_o": dw_o, "ln1_g": vec_ln1[0:1],
        "ln1_b": vec_ln1[1:2], "w_ff_a": dw_a, "w_ff_b": dw_b, "conv_w": dcw[0:3], "conv_b": dcb,
        "w_ff_down": dw_down, "ln2_g": vec_tail[3:4], "ln2_b": vec_tail[4:5], "w_ple_gate": dw_g,
        "b_ple_gate": vec_tail[0:1], "w_ple_in": dw_p, "ln3_g": vec_tail[1:2], "ln3_b": vec_tail[2:3],
    }
    return loss[0, 0], grad_x, grads


def _mesh_pos():
    return lax.axis_index("x"), lax.axis_index("y"), lax.axis_index("c")


def _all_gather_weights(shard):
    def body(x_ref, out_ref, send_sems, recv_sems, local_sem):
        x, y, c = _mesh_pos()
        me, sibling = (x, y, c), (x, y, 1 - c)
        chips = [(1 - x, y), (x, 1 - y), (1 - x, 1 - y)]

        def rows(px, py, pc):
            return out_ref.at[4 * px + 2 * py + pc]

        def copy(n, block, to, src=None):
            return pltpu.make_async_remote_copy(
                src_ref=rows(*block) if src is None else src, dst_ref=rows(*block),
                send_sem=send_sems.at[n], recv_sem=recv_sems.at[n], device_id=to, device_id_type=MESH)

        mine = pltpu.make_async_copy(x_ref, rows(*me), local_sem)
        mine.start()
        first = [copy(0, me, sibling, src=x_ref)]
        first += [copy(1 + n, me, (*chip, c), src=x_ref) for n, chip in enumerate(chips)]
        for cp in first:
            cp.start()
        passed = [copy(4 + n, (*chip, c), sibling) for n, chip in enumerate(chips)]
        for n, chip in enumerate(chips):
            copy(1 + n, (*chip, c), me).wait_recv()
            passed[n].start()
        copy(0, sibling, me).wait_recv()
        for n, chip in enumerate(chips):
            copy(4 + n, (*chip, 1 - c), me).wait_recv()
        for cp in first + passed:
            cp.wait_send()
        mine.wait()

    anywhere = pl.BlockSpec(memory_space=pl.ANY)
    return pl.pallas_call(
        body, name="all_gather_weights", out_shape=jax.ShapeDtypeStruct((N_DEV,) + shard.shape, shard.dtype),
        in_specs=[anywhere], out_specs=anywhere,
        scratch_shapes=[pltpu.SemaphoreType.DMA((7,)), pltpu.SemaphoreType.DMA((7,)), pltpu.SemaphoreType.DMA],
    )(shard)


def _exchange_grads(g_big, g_small):
    def body(gb_ref, gs_ref, rb_ref, rs_ref, send_b, recv_b, send_s, recv_s, local_sems):
        x, y, c = _mesh_pos()
        me = 4 * x + 2 * y + c
        copies = [pltpu.make_async_copy(gb_ref.at[me], rb_ref.at[me], local_sems.at[0]),
                  pltpu.make_async_copy(gs_ref, rs_ref.at[me], local_sems.at[1])]
        for flip in range(1, N_DEV):
            px = 1 - x if flip & 4 else x
            py = 1 - y if flip & 2 else y
            pc = 1 - c if flip & 1 else c
            peer = 4 * px + 2 * py + pc
            copies.append(pltpu.make_async_remote_copy(
                src_ref=gb_ref.at[peer], dst_ref=rb_ref.at[me], send_sem=send_b.at[flip - 1],
                recv_sem=recv_b.at[flip - 1], device_id=(px, py, pc), device_id_type=MESH))
            copies.append(pltpu.make_async_remote_copy(
                src_ref=gs_ref, dst_ref=rs_ref.at[me], send_sem=send_s.at[flip - 1],
                recv_sem=recv_s.at[flip - 1], device_id=(px, py, pc), device_id_type=MESH))
        for cp in copies:
            cp.start()
        for cp in copies:
            cp.wait()

    anywhere = pl.BlockSpec(memory_space=pl.ANY)
    sems = pltpu.SemaphoreType.DMA((N_DEV - 1,))
    return pl.pallas_call(
        body, name="exchange_grads",
        out_shape=(jax.ShapeDtypeStruct(g_big.shape, g_big.dtype),
                   jax.ShapeDtypeStruct((N_DEV,) + g_small.shape, g_small.dtype)),
        in_specs=[anywhere, anywhere], out_specs=(anywhere, anywhere),
        scratch_shapes=[sems, sems, sems, sems, pltpu.SemaphoreType.DMA((2,))],
    )(g_big, g_small)


def _sum_adamw(parts, w, m, v, tile, name):
    def body(p_ref, w_ref, m_ref, v_ref, g_ref, d_ref, nm_ref, nv_ref):
        g = p_ref[0].astype(F32)
        for n in range(1, N_DEV):
            g = g + p_ref[n].astype(F32)
        wv = w_ref[...]
        nm = ADAM_B1 * m_ref[...] + (1.0 - ADAM_B1) * g
        nv = ADAM_B2 * v_ref[...] + (1.0 - ADAM_B2) * (g * g)
        m_hat = nm / (1.0 - ADAM_B1 ** ADAM_STEP)
        v_hat = nv / (1.0 - ADAM_B2 ** ADAM_STEP)
        g_ref[...] = g
        d_ref[...] = -ADAM_LR * (m_hat / (jnp.sqrt(v_hat) + ADAM_EPS) + ADAM_WD * wv)
        nm_ref[...] = nm
        nv_ref[...] = nv

    rows = w.shape[0]
    flat = pl.BlockSpec((tile, LANES), _rows)
    out = jax.ShapeDtypeStruct((rows, LANES), F32)
    return pl.pallas_call(
        body, name=name, grid=(rows // tile,), out_shape=(out,) * 4,
        in_specs=[pl.BlockSpec((N_DEV, tile, LANES), lambda i: (0, i, 0)), flat, flat, flat],
        out_specs=(flat,) * 4, compiler_params=_params(("parallel",)),
    )(parts, w, m, v)


def _pack_rows(flat_parts, rows, dtype):
    flat = jnp.concatenate([a.reshape(-1).astype(dtype) for a in flat_parts])
    return jnp.pad(flat, (0, rows * LANES - flat.shape[0])).reshape(rows, LANES)


def _unpack(packed, layout):
    flat = packed.reshape(-1)
    out, off = {}, 0
    for name, shape in layout:
        n = int(np.prod(shape))
        out[name] = flat[off:off + n].reshape(shape)
        off += n
    return out


def _pack_weight_shards(w):
    parts = [w[name].astype(BF16) for name, _, _ in BIG[:-1]]
    rest = w["conv_w"]
    for _ in range(3):
        term = rest.astype(BF16)
        parts.append(term)
        rest = rest - term.astype(F32)
    return _pack_rows(parts, BIG_ROWS, BF16)


def _unpack_gathered(gathered):
    flat = gathered.reshape(N_DEV, -1)
    out, off = {}, 0
    for name, (r, c), axis in BIG[:-1]:
        blk = flat[:, off:off + r * c].reshape(N_DEV, r, c)
        off += r * c
        out[name] = blk.reshape(N_DEV * r, c) if axis == 0 else blk.transpose(1, 0, 2).reshape(r, N_DEV * c)
    terms = flat[:, off:off + 3 * CONV_W_N].reshape(N_DEV, 3, 3, 352).astype(F32)
    conv_w = terms[:, 0] + terms[:, 1] + terms[:, 2]
    out["conv_w"] = conv_w.transpose(1, 0, 2).reshape(3, D_FF)
    return out


def _pack_grads_by_owner(grads):
    cols = []
    for name, (r, c), axis in BIG:
        g = grads[name]
        blk = g.reshape(N_DEV, r * c) if axis == 0 else g.reshape(r, N_DEV, c).transpose(1, 0, 2).reshape(N_DEV, r * c)
        cols.append(blk.astype(BF16))
    flat = jnp.concatenate(cols, axis=1)
    return jnp.pad(flat, ((0, 0), (0, BIG_ROWS * LANES - flat.shape[1]))).reshape(N_DEV, BIG_ROWS, LANES)


def kernel(x, p, positions, w_in, ln_z_g, ln_z_b, w_s, b_s, w_o, ln1_g, ln1_b, w_ff_a, w_ff_b, conv_w, conv_b, w_ff_down, ln2_g, ln2_b, w_ple_gate, b_ple_gate, w_ple_in, ln3_g, ln3_b, loss_target, m_w_in, m_ln_z_g, m_ln_z_b, m_w_s, m_b_s, m_w_o, m_ln1_g, m_ln1_b, m_w_ff_a, m_w_ff_b, m_conv_w, m_conv_b, m_w_ff_down, m_ln2_g, m_ln2_b, m_w_ple_gate, m_b_ple_gate, m_w_ple_in, m_ln3_g, m_ln3_b, v_w_in, v_ln_z_g, v_ln_z_b, v_w_s, v_b_s, v_w_o, v_ln1_g, v_ln1_b, v_w_ff_a, v_w_ff_b, v_conv_w, v_conv_b, v_w_ff_down, v_ln2_g, v_ln2_b, v_w_ple_gate, v_b_ple_gate, v_w_ple_in, v_ln3_g, v_ln3_b):
    w = dict(w_in=w_in, ln_z_g=ln_z_g, ln_z_b=ln_z_b, w_s=w_s, b_s=b_s, w_o=w_o, ln1_g=ln1_g, ln1_b=ln1_b,
             w_ff_a=w_ff_a, w_ff_b=w_ff_b, conv_w=conv_w, conv_b=conv_b, w_ff_down=w_ff_down, ln2_g=ln2_g,
             ln2_b=ln2_b, w_ple_gate=w_ple_gate, b_ple_gate=b_ple_gate, w_ple_in=w_ple_in, ln3_g=ln3_g,
             ln3_b=ln3_b)
    m = dict(w_in=m_w_in, ln_z_g=m_ln_z_g, ln_z_b=m_ln_z_b, w_s=m_w_s, b_s=m_b_s, w_o=m_w_o, ln1_g=m_ln1_g,
             ln1_b=m_ln1_b, w_ff_a=m_w_ff_a, w_ff_b=m_w_ff_b, conv_w=m_conv_w, conv_b=m_conv_b,
             w_ff_down=m_w_ff_down, ln2_g=m_ln2_g, ln2_b=m_ln2_b, w_ple_gate=m_w_ple_gate,
             b_ple_gate=m_b_ple_gate, w_ple_in=m_w_ple_in, ln3_g=m_ln3_g, ln3_b=m_ln3_b)
    v = dict(w_in=v_w_in, ln_z_g=v_ln_z_g, ln_z_b=v_ln_z_b, w_s=v_w_s, b_s=v_b_s, w_o=v_w_o, ln1_g=v_ln1_g,
             ln1_b=v_ln1_b, w_ff_a=v_w_ff_a, w_ff_b=v_w_ff_b, conv_w=v_conv_w, conv_b=v_conv_b,
             w_ff_down=v_w_ff_down, ln2_g=v_ln2_g, ln2_b=v_ln2_b, w_ple_gate=v_w_ple_gate,
             b_ple_gate=v_b_ple_gate, w_ple_in=v_w_ple_in, ln3_g=v_ln3_g, ln3_b=v_ln3_b)
    big_names = [name for name, _, _ in BIG]
    small_names = [name for name, _ in SMALL]

    gathered = _all_gather_weights(_pack_weight_shards(w))
    wb = _unpack_gathered(gathered)
    sm = {n: w[n][0] if w[n].ndim > 2 else w[n] for n in small_names}
    pos_col = positions.reshape(S, 1).astype(F32)
    loss_part, grad_x, grads = _local_step(x[0], p[0, 0], pos_col, loss_target[0], wb, sm)

    g_small = _pack_rows([grads[n] for n in small_names], SMALL_ROWS, F32)
    recv_big, recv_small = _exchange_grads(_pack_grads_by_owner(grads), g_small)

    big = _sum_adamw(recv_big, _pack_rows([w[n] for n in big_names], BIG_ROWS, F32),
                     _pack_rows([m[n] for n in big_names], BIG_ROWS, F32),
                     _pack_rows([v[n] for n in big_names], BIG_ROWS, F32), BIG_TILE, "adamw_sharded")
    small = _sum_adamw(recv_small, _pack_rows([w[n] for n in small_names], SMALL_ROWS, F32),
                       _pack_rows([m[n] for n in small_names], SMALL_ROWS, F32),
                       _pack_rows([v[n] for n in small_names], SMALL_ROWS, F32), SMALL_ROWS, "adamw_replicated")
    big_layout = [(name, (1,) + shape) for name, shape, _ in BIG]
    results = []
    for kind in range(4):
        leaves = {**_unpack(big[kind], big_layout), **_unpack(small[kind], SMALL)}
        results.append([leaves[n] for n in WEIGHT_ORDER])
    loss = lax.psum(loss_part, ("x", "y", "c"))
    return (loss, grad_x[None], *results[0], *results[1], *results[2], *results[3])
```

```python
import math

import numpy as np
import jax
import jax.numpy as jnp
from jax import lax
from jax.experimental import pallas as pl
from jax.experimental.pallas import tpu as pltpu

F32 = jnp.float32
BF16 = jnp.bfloat16
MESH = pl.DeviceIdType.MESH

N_DEV = 8
S = 4096
D = 1024
D_HALF = 512
D_IN = 2560
D_FF = 2816
D_PLE = 256
CHUNK = 128
DILATIONS = ((1, 32), (4, 8), (16, 2))
ROPE_THETA = 500000.0
LN_EPS = 1e-5
ALPHA = 2.0 ** 0.25
NEG_INF = -1e30
INV_SQRT2 = 1.0 / math.sqrt(2.0)
INV_SQRT_2PI = 1.0 / math.sqrt(2.0 * math.pi)

ADAM_LR, ADAM_B1, ADAM_B2, ADAM_EPS, ADAM_WD, ADAM_STEP = 0.001, 0.9, 0.999, 1e-08, 0.01, 10

TM = 512
NT = S // TM
ATTN_UNROLL = 4
TN = 256
NJ = D_FF // TN
LANES = 128
VMEM_MIB = 1024 * 1024

BIG = (("w_in", (1024, 320), 1), ("w_o", (128, 1024), 0), ("w_ff_a", (1024, 352), 1),
       ("w_ff_b", (1024, 352), 1), ("w_ff_down", (352, 1024), 0), ("w_ple_gate", (128, 1024), 0),
       ("w_ple_in", (256, 128), 1))
WEIGHT_ORDER = ("w_in", "ln_z_g", "ln_z_b", "w_s", "b_s", "w_o", "ln1_g", "ln1_b", "w_ff_a", "w_ff_b",
                "conv_w", "conv_b", "w_ff_down", "ln2_g", "ln2_b", "w_ple_gate", "b_ple_gate",
                "w_ple_in", "ln3_g", "ln3_b")


def _params(semantics=None, vmem_mib=48):
    return pltpu.CompilerParams(dimension_semantics=semantics, vmem_limit_bytes=vmem_mib * VMEM_MIB)


def _dot(a, b):
    return jnp.dot(a, b, preferred_element_type=F32)


def _dot_nt(a, b):
    return lax.dot_general(a, b, (((1,), (1,)), ((), ())), preferred_element_type=F32)


def _dot_tn(a, b):
    return lax.dot_general(a, b, (((0,), (0,)), ((), ())), preferred_element_type=F32)


def _gelu(x):
    return 0.5 * x * (1.0 + lax.erf(x * INV_SQRT2))


def _gelu_grad(x):
    return 0.5 * (1.0 + lax.erf(x * INV_SQRT2)) + x * (jnp.exp(-0.5 * x * x) * INV_SQRT_2PI)


def _ln_stats(y):
    mu = jnp.mean(y, axis=-1, keepdims=True)
    yc = y - mu
    var = jnp.mean(yc * yc, axis=-1, keepdims=True)
    rstd = lax.rsqrt(var + LN_EPS)
    return yc * rstd, rstd


def _ln_bwd(dxhat, xhat, rstd):
    m1 = jnp.mean(dxhat, axis=-1, keepdims=True)
    m2 = jnp.mean(dxhat * xhat, axis=-1, keepdims=True)
    return rstd * (dxhat - m1 - xhat * m2)


def _colsum(x):
    return jnp.sum(x, axis=0, keepdims=True)


def _rows(i):
    return (i, 0)


def _fixed(*_):
    return (0, 0)


def _row_spec(width):
    return pl.BlockSpec((TM, width), _rows)


def _full_spec(shape):
    return pl.BlockSpec(shape, lambda *_: (0,) * len(shape))


def _owner_slot(j):
    return (j % 2) * 4 + j // 2


def _lane_lo():
    return lax.broadcasted_iota(jnp.int32, (CHUNK, LANES), 1) < 64


def _tril():
    r = lax.broadcasted_iota(jnp.int32, (CHUNK, CHUNK), 0)
    c = lax.broadcasted_iota(jnp.int32, (CHUNK, CHUNK), 1)
    return c <= r


def _rope_consts():
    lane = np.arange(LANES) % 64
    j = lane % 8
    inv = np.where(lane < 16, np.float32(ROPE_THETA) ** (-(2.0 * j).astype(np.float32) / np.float32(16.0)), 0.0)
    m_lo = (lane < 8).astype(np.float32)
    m_hi = ((lane >= 8) & (lane < 16)).astype(np.float32)
    return (jnp.asarray(inv, F32).reshape(1, LANES), jnp.asarray(m_lo).reshape(1, LANES),
            jnp.asarray(m_hi).reshape(1, LANES))


def _rope_tables(pos_col):
    inv, m_lo, m_hi = _rope_consts()

    def body(pos_ref, inv_ref, lo_ref, hi_ref, c_ref, sa_ref, sb_ref):
        ang = pos_ref[...] * inv_ref[...]
        c = jnp.cos(ang)
        s = jnp.sin(ang)
        lo = lo_ref[...]
        hi = hi_ref[...]
        c_ref[...] = jnp.where(lo + hi > 0.0, c, 1.0)
        sa_ref[...] = s * hi
        sb_ref[...] = -s * lo

    vec = _full_spec((1, LANES))
    out = jax.ShapeDtypeStruct((S, LANES), F32)
    return pl.pallas_call(
        body, name="rope_tables", grid=(NT,), out_shape=(out, out, out),
        in_specs=[pl.BlockSpec((TM, 1), _rows), vec, vec, vec],
        out_specs=(_row_spec(LANES),) * 3, compiler_params=_params(("parallel",)),
    )(pos_col, inv, m_lo, m_hi)


def _rope(t, c, sa, sb):
    return t * c + pltpu.roll(t, 8, 1) * sa + pltpu.roll(t, LANES - 8, 1) * sb


def _rope_t(dy, c, sa, sb):
    return dy * c + pltpu.roll(dy * sa, LANES - 8, 1) + pltpu.roll(dy * sb, 8, 1)


def _masked_ws(ws_ref):
    tril = _tril()
    return [jnp.where(tril, ws_ref[g], 0.0).astype(BF16) for g in range(8)]


def _spatial_mix(zn, wm, bs):
    lo = _lane_lo()
    rows = []
    for ch in range(TM // CHUNK):
        slabs = []
        for pr in range(4):
            zp = zn[ch * CHUNK:(ch + 1) * CHUNK, pr * LANES:(pr + 1) * LANES].astype(BF16)
            slabs.append(jnp.where(lo, _dot(wm[2 * pr], zp), _dot(wm[2 * pr + 1], zp)))
        rows.append(jnp.concatenate(slabs, axis=1) + bs)
    return jnp.concatenate(rows, axis=0)


def _proj_in_fwd(x, w_in, tabs, ln_z_g, ln_z_b, w_s, bs_exp):
    def body(x_ref, w_ref, c_ref, sa_ref, sb_ref, g_ref, b_ref, ws_ref, bs_ref,
             q_ref, k_ref, v_ref, u_ref, z_ref, gm_ref, xb_ref):
        xb = x_ref[...].astype(BF16)
        xb_ref[...] = xb
        c, sa, sb = c_ref[...], sa_ref[...], sb_ref[...]
        hq = _dot(xb, w_ref[:, 0:512])
        hk = _dot(xb, w_ref[:, 512:1024])
        for s in range(4):
            sl = slice(s * LANES, (s + 1) * LANES)
            q_ref[:, sl] = _rope(hq[:, sl], c, sa, sb)
            k_ref[:, sl] = _rope(hk[:, sl], c, sa, sb)
        v_ref[...] = _dot(xb, w_ref[:, 1024:1536])
        u_pre = _dot(xb, w_ref[:, 1536:2048])
        z_pre = _dot(xb, w_ref[:, 2048:2560])
        u_ref[...] = u_pre
        z_ref[...] = z_pre
        zhat, _ = _ln_stats(_gelu(z_pre))
        zn = zhat * g_ref[...] + b_ref[...]
        mixed = _spatial_mix(zn, _masked_ws(ws_ref), bs_ref[...])
        gm_ref[...] = (_gelu(u_pre) * mixed).astype(BF16)

    half = jax.ShapeDtypeStruct((S, D_HALF), F32)
    tab = _row_spec(LANES)
    return pl.pallas_call(
        body, name="proj_in_fwd", grid=(NT,),
        out_shape=(half, half, half, half, half, jax.ShapeDtypeStruct((S, D_HALF), BF16),
                   jax.ShapeDtypeStruct((S, D), BF16)),
        in_specs=[_row_spec(D), _full_spec((D, D_IN)), tab, tab, tab, _full_spec((1, D_HALF)),
                  _full_spec((1, D_HALF)), _full_spec((8, CHUNK, CHUNK)), _full_spec((CHUNK, D_HALF))],
        out_specs=(_row_spec(D_HALF),) * 6 + (_row_spec(D),),
        compiler_params=_params(("parallel",)),
    )(x, w_in, *tabs, ln_z_g, ln_z_b, w_s, bs_exp)


def _store_band_bias(bias_ref):
    qi = lax.broadcasted_iota(jnp.int32, (CHUNK, 2 * CHUNK), 0)
    kj = lax.broadcasted_iota(jnp.int32, (CHUNK, 2 * CHUNK), 1)
    band = (kj >= qi) & (kj <= qi + CHUNK)
    bias_ref[0] = jnp.where(band, 0.0, NEG_INF)
    bias_ref[1] = jnp.where(band & (kj >= CHUNK), 0.0, NEG_INF)


def _permuted_rows(ref, d, r):
    return ref[...] if d == 1 else ref[pl.ds(r, S // d, stride=d), :]


def _attention_fwd(q, k, v):
    def body(q_ref, k_ref, v_ref, o_ref, lse_ref, qb, kb, v0b, v1b, bias, op, lp, ob0, lb0, ob1, lb1, ob2, lb2):
        lo = _lane_lo()
        lo_f = lo.astype(F32)[0:1, :]
        hi_f = 1.0 - lo_f
        zero_pad = jnp.zeros((CHUNK, LANES), BF16)
        for buf in (qb, kb, v0b, v1b):
            buf[0:CHUNK, :] = zero_pad
        _store_band_bias(bias)
        outs = ((ob0, lb0), (ob1, lb1), (ob2, lb2))
        for (d, nb), (ob, lb) in zip(DILATIONS, outs):
            length = S // d
            for r in range(d):
                dst = slice(CHUNK + r * length, CHUNK + (r + 1) * length)
                qb[dst, :] = (_permuted_rows(q_ref, d, r) * 0.125).astype(BF16)
                kb[dst, :] = _permuted_rows(k_ref, d, r).astype(BF16)
                vs = _permuted_rows(v_ref, d, r)
                v0b[dst, :] = (vs * lo_f + hi_f).astype(BF16)
                v1b[dst, :] = (vs * hi_f + lo_f).astype(BF16)

            def block(b, carry, nb=nb):
                base = pl.multiple_of(b * CHUNK, CHUNK)
                add = bias[jnp.where(b % nb == 0, 1, 0)]
                qblk = qb[pl.ds(pl.multiple_of(base + CHUNK, CHUNK), CHUNK), :]
                kblk = kb[pl.ds(base, 2 * CHUNK), :]
                pv, mx = [], []
                for head, vh in enumerate((v0b, v1b)):
                    qh = jnp.where(lo, qblk, 0) if head == 0 else jnp.where(lo, 0, qblk)
                    s = _dot_nt(qh, kblk) + add
                    m = jnp.max(s, axis=-1, keepdims=True)
                    p = jnp.exp(s - m).astype(BF16)
                    pv.append(_dot(p, vh[pl.ds(base, 2 * CHUNK), :]))
                    mx.append(m)
                den = pltpu.roll(jnp.where(lo, pv[1], pv[0]), 64, 1)
                op[pl.ds(base, CHUNK), :] = jnp.where(lo, pv[0], pv[1]) / den
                lp[pl.ds(base, CHUNK), :] = jnp.where(lo, mx[0], mx[1]) + jnp.log(den)
                return carry

            lax.fori_loop(0, S // CHUNK, block, 0, unroll=ATTN_UNROLL)
            for r in range(d):
                src = slice(r * length, (r + 1) * length)
                if d == 1:
                    ob[...] = op[...]
                    lb[...] = lp[...]
                else:
                    ob[pl.ds(r, length, stride=d), :] = op[src, :]
                    lb[pl.ds(r, length, stride=d), :] = lp[src, :]
        for t in range(NT):
            rows = slice(t * TM, (t + 1) * TM)
            l0, l1, l2 = lb0[rows, :], lb1[rows, :], lb2[rows, :]
            mx = jnp.maximum(jnp.maximum(l0, l1), l2)
            e0, e1, e2 = jnp.exp(l0 - mx), jnp.exp(l1 - mx), jnp.exp(l2 - mx)
            den = e0 + e1 + e2
            o_ref[rows, :] = (e0 * ob0[rows, :] + e1 * ob1[rows, :] + e2 * ob2[rows, :]) / den
            lse_ref[rows, :] = mx + jnp.log(den)

    slab = pl.BlockSpec((S, LANES), lambda h: (0, h))
    out = jax.ShapeDtypeStruct((S, D_HALF), F32)
    padded = pltpu.VMEM((CHUNK + S, LANES), BF16)
    whole = pltpu.VMEM((S, LANES), F32)
    return pl.pallas_call(
        body, name="attention_fwd", grid=(4,), out_shape=(out, out),
        in_specs=[slab, slab, slab], out_specs=(slab, slab),
        scratch_shapes=[padded] * 4 + [pltpu.VMEM((2, CHUNK, 2 * CHUNK), F32)] + [whole] * 8,
        compiler_params=_params(("parallel",), 56),
    )(q, k, v)


def _mix_ln1_fwd(attn, gm, w_o, x, g1, b1):
    def body(a_ref, gm_ref, w_ref, x_ref, g_ref, b_ref, xhat_ref, rstd_ref, x1b_ref):
        mix = _dot(a_ref[...].astype(BF16), w_ref[0:D_HALF, :]) + _dot(gm_ref[...], w_ref[D_HALF:D, :])
        xhat, rstd = _ln_stats(ALPHA * x_ref[...] + mix)
        xhat_ref[...] = xhat
        rstd_ref[...] = rstd
        x1b_ref[...] = (xhat * g_ref[...] + b_ref[...]).astype(BF16)

    vec = _full_spec((1, D))
    return pl.pallas_call(
        body, name="mix_ln1_fwd", grid=(NT,),
        out_shape=(jax.ShapeDtypeStruct((S, D), F32), jax.ShapeDtypeStruct((S, 1), F32),
                   jax.ShapeDtypeStruct((S, D), BF16)),
        in_specs=[_row_spec(D_HALF), _row_spec(D_HALF), _full_spec((D, D)), _row_spec(D), vec, vec],
        out_specs=(_row_spec(D), pl.BlockSpec((TM, 1), _rows), _row_spec(D)),
        compiler_params=_params(("parallel",)),
    )(attn, gm, w_o, x, g1, b1)


def _ffn_up_fwd(x1b, w_a, w_b, conv_w8, conv_b):
    def body(x_ref, wa_ref, wb_ref, cw_ref, cb_ref, ap_ref, a_ref, bl_ref, h_ref, carry):
        @pl.when(pl.program_id(1) == 0)
        def _():
            carry[...] = jnp.zeros_like(carry)

        xb = x_ref[...]
        ap = _dot(xb, wa_ref[...])
        bl = _dot(xb, wb_ref[...])
        row = lax.broadcasted_iota(jnp.int32, (TM, TN), 0)
        c6, c7 = carry[6:7, :], carry[7:8, :]
        m1 = jnp.where(row == 0, c7, pltpu.roll(ap, 1, 0))
        m2 = jnp.where(row == 0, c6, jnp.where(row == 1, c7, pltpu.roll(ap, 2, 0)))
        a = cb_ref[...] + cw_ref[0:1, :] * m2 + cw_ref[1:2, :] * m1 + cw_ref[2:3, :] * ap
        carry[...] = ap[TM - 8:TM, :]
        ap_ref[...] = ap
        a_ref[...] = a
        bl_ref[...] = bl
        h_ref[...] = (_gelu(a) * bl).astype(BF16)

    tile = pl.BlockSpec((TM, TN), lambda j, i: (i, j))
    wcol = pl.BlockSpec((D, TN), lambda j, i: (0, j))
    ff = jax.ShapeDtypeStruct((S, D_FF), F32)
    return pl.pallas_call(
        body, name="ffn_up_fwd", grid=(NJ, NT),
        out_shape=(ff, ff, ff, jax.ShapeDtypeStruct((S, D_FF), BF16)),
        in_specs=[pl.BlockSpec((TM, D), lambda j, i: (i, 0)), wcol, wcol,
                  pl.BlockSpec((8, TN), lambda j, i: (0, j)), pl.BlockSpec((1, TN), lambda j, i: (0, j))],
        out_specs=(tile, tile, tile, tile),
        scratch_shapes=[pltpu.VMEM((8, TN), F32)],
        compiler_params=_params(("parallel", "arbitrary")),
    )(x1b, w_a, w_b, conv_w8, conv_b)


def _ffn_down_ln2_fwd(hff, w_down, xhat1, g1, b1):
    def body(h_ref, w_ref, xh_ref, g_ref, b_ref, xhat_ref, rstd_ref):
        x1 = xh_ref[...] * g_ref[...] + b_ref[...]
        xhat, rstd = _ln_stats(ALPHA * x1 + _dot(h_ref[...], w_ref[...]))
        xhat_ref[...] = xhat
        rstd_ref[...] = rstd

    vec = _full_spec((1, D))
    return pl.pallas_call(
        body, name="ffn_down_ln2_fwd", grid=(NT,),
        out_shape=(jax.ShapeDtypeStruct((S, D), F32), jax.ShapeDtypeStruct((S, 1), F32)),
        in_specs=[_row_spec(D_FF), _full_spec((D_FF, D)), _row_spec(D), vec, vec],
        out_specs=(_row_spec(D), pl.BlockSpec((TM, 1), _rows)),
        compiler_params=_params(("parallel",)),
    )(hff, w_down, xhat1, g1, b1)


def _tail_fwd_bwd(xhat2, rstd2, p, target, w_g, w_p, g2, b2, bg, g3, b3):
    def body(xh_ref, rs_ref, p_ref, t_ref, wg_ref, wp_ref, g2_ref, b2_ref, bg_ref, g3_ref, b3_ref,
             loss_ref, dy2_ref, dy2b_ref, gwg_ref, gwp_ref, vec_ref, dwg_ref, dwp_ref):
        @pl.when(pl.program_id(0) == 0)
        def _():
            loss_ref[...] = jnp.zeros_like(loss_ref)
            dwg_ref[...] = jnp.zeros_like(dwg_ref)
            dwp_ref[...] = jnp.zeros_like(dwp_ref)
            vec_ref[...] = jnp.zeros_like(vec_ref)

        xhat2_t = xh_ref[...]
        x2 = xhat2_t * g2_ref[...] + b2_ref[...]
        x2b = x2.astype(BF16)
        pb = p_ref[...].astype(BF16)
        gate = jax.nn.sigmoid(_dot(x2b, wg_ref[...]) + bg_ref[...])
        pin = _dot(pb, wp_ref[...])
        xhat3, rstd3 = _ln_stats(ALPHA * x2 + gate * pin)
        err = xhat3 * g3_ref[...] + b3_ref[...] - t_ref[...]
        loss_ref[...] += jnp.sum(jnp.mean(err * err, axis=-1, keepdims=True), axis=0, keepdims=True) * 0.5
        dout = err * (1.0 / D)
        dy3 = _ln_bwd(dout * g3_ref[...], xhat3, rstd3)
        dgp = dy3 * pin * gate * (1.0 - gate)
        dgpb = dgp.astype(BF16)
        dwg_ref[...] += _dot_tn(x2b, dgpb)
        dwp_ref[...] += _dot_tn(pb, (dy3 * gate).astype(BF16))
        dx2 = ALPHA * dy3 + _dot_nt(dgpb, wg_ref[...])
        dy2 = _ln_bwd(dx2 * g2_ref[...], xhat2_t, rs_ref[...])
        dy2_ref[...] = dy2
        dy2b_ref[...] = dy2.astype(BF16)
        vec_ref[0:1, :] += _colsum(dgp)
        vec_ref[1:2, :] += _colsum(dout * xhat3)
        vec_ref[2:3, :] += _colsum(dout)
        vec_ref[3:4, :] += _colsum(dx2 * xhat2_t)
        vec_ref[4:5, :] += _colsum(dx2)

        @pl.when(pl.program_id(0) == NT - 1)
        def _():
            for j in range(N_DEV):
                gwg_ref[_owner_slot(j)] = dwg_ref[LANES * j:LANES * (j + 1), :].astype(BF16)
                gwp_ref[_owner_slot(j)] = dwp_ref[:, LANES * j:LANES * (j + 1)].astype(BF16)

    vec = _full_spec((1, D))
    return pl.pallas_call(
        body, name="tail_fwd_bwd", grid=(NT,),
        out_shape=(jax.ShapeDtypeStruct((8, LANES), F32), jax.ShapeDtypeStruct((S, D), F32),
                   jax.ShapeDtypeStruct((S, D), BF16), jax.ShapeDtypeStruct((N_DEV, D // N_DEV, D), BF16),
                   jax.ShapeDtypeStruct((N_DEV, D_PLE, D // N_DEV), BF16), jax.ShapeDtypeStruct((8, D), F32)),
        in_specs=[_row_spec(D), pl.BlockSpec((TM, 1), _rows), _row_spec(D_PLE), _row_spec(D),
                  _full_spec((D, D)), _full_spec((D_PLE, D)), vec, vec, vec, vec, vec],
        out_specs=(_full_spec((8, LANES)), _row_spec(D), _row_spec(D), _full_spec((N_DEV, D // N_DEV, D)),
                   _full_spec((N_DEV, D_PLE, D // N_DEV)), _full_spec((8, D))),
        scratch_shapes=[pltpu.VMEM((D, D), F32), pltpu.VMEM((D_PLE, D), F32)],
        compiler_params=_params(("arbitrary",)),
    )(xhat2, rstd2, p, target, w_g, w_p, g2, b2, bg, g3, b3)


def _ffn_bwd(dy2b, x1b, w_down, w_a, w_b, a_pre, a, b_lin, hff, conv_w8):
    def body(dy_ref, x_ref, wd_ref, wa_ref, wb_ref, ap_ref, a_ref, bl_ref, h_ref, cw_ref,
             dwd_ref, dwa_ref, dwb_ref, dcw_ref, dx_hbm, dx_acc, carry, sem):
        j, i = pl.program_id(0), pl.program_id(1)

        @pl.when(i == 0)
        def _():
            carry[...] = jnp.zeros_like(carry)
            dwd_ref[...] = jnp.zeros_like(dwd_ref)
            dwa_ref[...] = jnp.zeros_like(dwa_ref)
            dwb_ref[...] = jnp.zeros_like(dwb_ref)
            dcw_ref[...] = jnp.zeros_like(dcw_ref)

        dyb = dy_ref[...]
        xb = x_ref[...]
        dh = _dot_nt(dyb, wd_ref[...])
        av = a_ref[...]
        dbl = dh * _gelu(av)
        da = dh * bl_ref[...] * _gelu_grad(av)
        row = lax.broadcasted_iota(jnp.int32, (TM, TN), 0)
        c0, c1 = carry[0:1, :], carry[1:2, :]
        p1 = jnp.where(row == TM - 1, c0, pltpu.roll(da, TM - 1, 0))
        p2 = jnp.where(row == TM - 2, c0, jnp.where(row == TM - 1, c1, pltpu.roll(da, TM - 2, 0)))
        carry[...] = da[0:8, :]
        ap = ap_ref[...]
        dcw_ref[3:4, :] += _colsum(da)
        dcw_ref[0:1, :] += _colsum(ap * p2)
        dcw_ref[1:2, :] += _colsum(ap * p1)
        dcw_ref[2:3, :] += _colsum(ap * da)
        dap = (cw_ref[2:3, :] * da + cw_ref[1:2, :] * p1 + cw_ref[0:1, :] * p2).astype(BF16)
        dblb = dbl.astype(BF16)
        dwa_ref[...] += _dot_tn(xb, dap)
        dwb_ref[...] += _dot_tn(xb, dblb)
        dwd_ref[...] += _dot_tn(h_ref[...], dyb)
        dx = _dot_nt(dap, wa_ref[...]) + _dot_nt(dblb, wb_ref[...])
        rows = pl.ds(pl.multiple_of((NT - 1 - i) * TM, TM), TM)

        @pl.when(j == 0)
        def _():
            dx_acc[rows, :] = dx

        @pl.when(j > 0)
        def _():
            dx_acc[rows, :] += dx

        @pl.when((j == NJ - 1) & (i == NT - 1))
        def _():
            cp = pltpu.make_async_copy(dx_acc, dx_hbm, sem)
            cp.start()
            cp.wait()

    rev_rows = lambda j, i: (NT - 1 - i, 0)
    rev_tile = pl.BlockSpec((TM, TN), lambda j, i: (NT - 1 - i, j))
    wcol = pl.BlockSpec((D, TN), lambda j, i: (0, j))
    small = pl.BlockSpec((8, TN), lambda j, i: (0, j))
    return pl.pallas_call(
        body, name="ffn_bwd", grid=(NJ, NT),
        out_shape=(jax.ShapeDtypeStruct((D_FF, D), F32), jax.ShapeDtypeStruct((D, D_FF), F32),
                   jax.ShapeDtypeStruct((D, D_FF), F32), jax.ShapeDtypeStruct((8, D_FF), F32),
                   jax.ShapeDtypeStruct((S, D), F32)),
        in_specs=[pl.BlockSpec((TM, D), rev_rows), pl.BlockSpec((TM, D), rev_rows),
                  pl.BlockSpec((TN, D), lambda j, i: (j, 0)), wcol, wcol, rev_tile, rev_tile, rev_tile,
                  rev_tile, small],
        out_specs=(pl.BlockSpec((TN, D), lambda j, i: (j, 0)), wcol, wcol, small,
                   pl.BlockSpec(memory_space=pl.ANY)),
        scratch_shapes=[pltpu.VMEM((S, D), F32), pltpu.VMEM((8, TN), F32), pltpu.SemaphoreType.DMA],
        compiler_params=_params(("arbitrary", "arbitrary"), 56),
    )(dy2b, x1b, w_down, w_a, w_b, a_pre, a, b_lin, hff, conv_w8)


def _ln1_mix_bwd(dy2, dx1_ffn, xhat1, rstd1, g1, attn, gm, w_o):
    def body(dy2_ref, dxf_ref, xh_ref, rs_ref, g_ref, a_ref, gm_ref, w_ref,
             dy1_ref, da_ref, dlt_ref, dgm_ref, gwo_ref, vec_ref, dwo_ref):
        @pl.when(pl.program_id(0) == 0)
        def _():
            dwo_ref[...] = jnp.zeros_like(dwo_ref)
            vec_ref[...] = jnp.zeros_like(vec_ref)

        xhat = xh_ref[...]
        dx1 = ALPHA * dy2_ref[...] + dxf_ref[...]
        vec_ref[0:1, :] += _colsum(dx1 * xhat)
        vec_ref[1:2, :] += _colsum(dx1)
        dy1 = _ln_bwd(dx1 * g_ref[...], xhat, rs_ref[...])
        dy1_ref[...] = dy1
        dy1b = dy1.astype(BF16)
        dmix = _dot_nt(dy1b, w_ref[...])
        attn_t = a_ref[...]
        d_attn = dmix[:, 0:D_HALF]
        da_ref[...] = d_attn
        dgm_ref[...] = dmix[:, D_HALF:D]
        lo = (lax.broadcasted_iota(jnp.int32, (TM, LANES), 1) < 64)
        for s in range(4):
            sl = slice(s * LANES, (s + 1) * LANES)
            prod = d_attn[:, sl] * attn_t[:, sl]
            s0 = jnp.sum(jnp.where(lo, prod, 0.0), axis=-1, keepdims=True)
            s1 = jnp.sum(jnp.where(lo, 0.0, prod), axis=-1, keepdims=True)
            dlt_ref[:, sl] = jnp.where(lo, s0, s1)
        dwo_ref[0:D_HALF, :] += _dot_tn(attn_t.astype(BF16), dy1b)
        dwo_ref[D_HALF:D, :] += _dot_tn(gm_ref[...], dy1b)

        @pl.when(pl.program_id(0) == NT - 1)
        def _():
            for j in range(N_DEV):
                gwo_ref[_owner_slot(j)] = dwo_ref[LANES * j:LANES * (j + 1), :].astype(BF16)

    half = jax.ShapeDtypeStruct((S, D_HALF), F32)
    return pl.pallas_call(
        body, name="ln1_mix_bwd", grid=(NT,),
        out_shape=(jax.ShapeDtypeStruct((S, D), F32), half, half, half,
                   jax.ShapeDtypeStruct((N_DEV, D // N_DEV, D), BF16), jax.ShapeDtypeStruct((8, D), F32)),
        in_specs=[_row_spec(D), _row_spec(D), _row_spec(D), pl.BlockSpec((TM, 1), _rows), _full_spec((1, D)),
                  _row_spec(D_HALF), _row_spec(D_HALF), _full_spec((D, D))],
        out_specs=(_row_spec(D), _row_spec(D_HALF), _row_spec(D_HALF), _row_spec(D_HALF),
                   _full_spec((N_DEV, D // N_DEV, D)), _full_spec((8, D))),
        scratch_shapes=[pltpu.VMEM((D, D), F32)],
        compiler_params=_params(("arbitrary",)),
    )(dy2, dx1_ffn, xhat1, rstd1, g1, attn, gm, w_o)


def _gmlp_bwd(d_gm, u_pre, z_pre, ln_z_g, ln_z_b, w_s, bs_exp):
    def body(dg_ref, u_ref, z_ref, g_ref, b_ref, ws_ref, bs_ref, du_ref, dz_ref, dws_ref, dbs_ref, vec_ref):
        @pl.when(pl.program_id(0) == 0)
        def _():
            dws_ref[...] = jnp.zeros_like(dws_ref)
            dbs_ref[...] = jnp.zeros_like(dbs_ref)
            vec_ref[...] = jnp.zeros_like(vec_ref)

        u_pre_t, z_pre_t, dgm = u_ref[...], z_ref[...], dg_ref[...]
        zhat, rstd = _ln_stats(_gelu(z_pre_t))
        zn = zhat * g_ref[...] + b_ref[...]
        wm = _masked_ws(ws_ref)
        mixed = _spatial_mix(zn, wm, bs_ref[...])
        du_ref[...] = (dgm * mixed * _gelu_grad(u_pre_t)).astype(BF16)
        dmixed = dgm * _gelu(u_pre_t)
        lo = _lane_lo()
        tril = _tril()
        group_of_lane = lax.broadcasted_iota(jnp.int32, (8, D_HALF), 1) // 64
        pick = (group_of_lane == lax.broadcasted_iota(jnp.int32, (8, D_HALF), 0)).astype(F32)
        dzn_rows = []
        for ch in range(TM // CHUNK):
            rows = slice(ch * CHUNK, (ch + 1) * CHUNK)
            dbs_ref[...] += lax.dot_general(pick, dmixed[rows, :], (((1,), (1,)), ((), ())),
                                            precision=lax.Precision.HIGHEST, preferred_element_type=F32)
            slabs = []
            for pr in range(4):
                sl = slice(pr * LANES, (pr + 1) * LANES)
                dm = dmixed[rows, sl]
                zp = zn[rows, sl].astype(BF16)
                dm_lo = jnp.where(lo, dm, 0.0).astype(BF16)
                dm_hi = jnp.where(lo, 0.0, dm).astype(BF16)
                dws_ref[2 * pr] += jnp.where(tril, _dot_nt(dm_lo, zp), 0.0)
                dws_ref[2 * pr + 1] += jnp.where(tril, _dot_nt(dm_hi, zp), 0.0)
                dmb = dm.astype(BF16)
                slabs.append(jnp.where(lo, _dot_tn(wm[2 * pr], dmb), _dot_tn(wm[2 * pr + 1], dmb)))
            dzn_rows.append(jnp.concatenate(slabs, axis=1))
        dzn = jnp.concatenate(dzn_rows, axis=0)
        vec_ref[0:1, :] += _colsum(dzn * zhat)
        vec_ref[1:2, :] += _colsum(dzn)
        dz = _ln_bwd(dzn * g_ref[...], zhat, rstd)
        dz_ref[...] = (dz * _gelu_grad(z_pre_t)).astype(BF16)

    halfb = jax.ShapeDtypeStruct((S, D_HALF), BF16)
    vec = _full_spec((1, D_HALF))
    return pl.pallas_call(
        body, name="gmlp_bwd", grid=(NT,),
        out_shape=(halfb, halfb, jax.ShapeDtypeStruct((8, CHUNK, CHUNK), F32),
                   jax.ShapeDtypeStruct((8, CHUNK), F32), jax.ShapeDtypeStruct((8, D_HALF), F32)),
        in_specs=[_row_spec(D_HALF), _row_spec(D_HALF), _row_spec(D_HALF), vec, vec,
                  _full_spec((8, CHUNK, CHUNK)), _full_spec((CHUNK, D_HALF))],
        out_specs=(_row_spec(D_HALF), _row_spec(D_HALF), _full_spec((8, CHUNK, CHUNK)),
                   _full_spec((8, CHUNK)), _full_spec((8, D_HALF))),
        compiler_params=_params(("arbitrary",)),
    )(d_gm, u_pre, z_pre, ln_z_g, ln_z_b, w_s, bs_exp)


def _attention_bwd(q, k, v, lse, d_attn, delta, tabs):
    def body(q_ref, k_ref, v_ref, l_ref, do_ref, dl_ref, c_ref, sa_ref, sb_ref, dq_ref, dk_ref, dv_ref,
             qb, kb, vb, gb, bias, lsp, dlp, dqp, dk_own, dk_prev, dv_own, dv_prev, dqa, dka, dva):
        lo = _lane_lo()
        zero_pad = jnp.zeros((CHUNK, LANES), BF16)
        for buf in (qb, kb, vb, gb):
            buf[0:CHUNK, :] = zero_pad
        dk_prev[S:S + CHUNK, :] = jnp.zeros((CHUNK, LANES), F32)
        dv_prev[S:S + CHUNK, :] = jnp.zeros((CHUNK, LANES), F32)
        _store_band_bias(bias)
        for d, nb in DILATIONS:
            length = S // d
            for r in range(d):
                dst = slice(CHUNK + r * length, CHUNK + (r + 1) * length)
                src = slice(r * length, (r + 1) * length)
                qb[dst, :] = (_permuted_rows(q_ref, d, r) * 0.125).astype(BF16)
                kb[dst, :] = _permuted_rows(k_ref, d, r).astype(BF16)
                vb[dst, :] = _permuted_rows(v_ref, d, r).astype(BF16)
                gb[dst, :] = _permuted_rows(do_ref, d, r).astype(BF16)
                lsp[src, :] = _permuted_rows(l_ref, d, r)
                dlp[src, :] = _permuted_rows(dl_ref, d, r)

            def block(b, carry, nb=nb):
                base = pl.multiple_of(b * CHUNK, CHUNK)
                own = pl.multiple_of(base + CHUNK, CHUNK)
                add = bias[jnp.where(b % nb == 0, 1, 0)]
                qblk = qb[pl.ds(own, CHUNK), :]
                gblk = gb[pl.ds(own, CHUNK), :]
                kblk = kb[pl.ds(base, 2 * CHUNK), :]
                vblk = vb[pl.ds(base, 2 * CHUNK), :]
                lse_t = lsp[pl.ds(base, CHUNK), :]
                dlt_t = dlp[pl.ds(base, CHUNK), :]
                dq, dk, dv = [], [], []
                for head in range(2):
                    qh = jnp.where(lo, qblk, 0) if head == 0 else jnp.where(lo, 0, qblk)
                    gh = jnp.where(lo, gblk, 0) if head == 0 else jnp.where(lo, 0, gblk)
                    col = 64 * head
                    p = jnp.exp(_dot_nt(qh, kblk) + add - lse_t[:, col:col + 1])
                    ds = (p * (_dot_nt(gh, vblk) - dlt_t[:, col:col + 1])).astype(BF16)
                    dv.append(_dot_tn(p.astype(BF16), gh))
                    dk.append(_dot_tn(ds, qh))
                    dq.append(_dot(ds, kblk))
                dqp[pl.ds(base, CHUNK), :] = jnp.where(lo, dq[0], dq[1]) * 0.125
                dk_blk = dk[0] + dk[1]
                dv_blk = dv[0] + dv[1]
                dk_prev[pl.ds(base, CHUNK), :] = dk_blk[0:CHUNK, :]
                dk_own[pl.ds(own, CHUNK), :] = dk_blk[CHUNK:2 * CHUNK, :]
                dv_prev[pl.ds(base, CHUNK), :] = dv_blk[0:CHUNK, :]
                dv_own[pl.ds(own, CHUNK), :] = dv_blk[CHUNK:2 * CHUNK, :]
                return carry

            lax.fori_loop(0, S // CHUNK, block, 0, unroll=ATTN_UNROLL)
            for r in range(d):
                src = slice(r * length, (r + 1) * length)
                pad = slice(CHUNK + r * length, CHUNK + (r + 1) * length)
                if d == 1:
                    dqa[...] = dqp[...]
                    dka[...] = dk_own[pad, :] + dk_prev[pad, :]
                    dva[...] = dv_own[pad, :] + dv_prev[pad, :]
                else:
                    dst = pl.ds(r, length, stride=d)
                    dqa[dst, :] = dqa[dst, :] + dqp[src, :]
                    dka[dst, :] = dka[dst, :] + (dk_own[pad, :] + dk_prev[pad, :])
                    dva[dst, :] = dva[dst, :] + (dv_own[pad, :] + dv_prev[pad, :])
        for t in range(NT):
            rows = slice(t * TM, (t + 1) * TM)
            c, sa, sb = c_ref[rows, :], sa_ref[rows, :], sb_ref[rows, :]
            dq_ref[rows, :] = _rope_t(dqa[rows, :], c, sa, sb).astype(BF16)
            dk_ref[rows, :] = _rope_t(dka[rows, :], c, sa, sb).astype(BF16)
            dv_ref[rows, :] = dva[rows, :].astype(BF16)

    slab = pl.BlockSpec((S, LANES), lambda h: (0, h), pipeline_mode=pl.Buffered(1))
    tab = pl.BlockSpec((S, LANES), lambda h: (0, 0), pipeline_mode=pl.Buffered(1))
    out_slab = pl.BlockSpec((S, LANES), lambda h: (0, h))
    out = jax.ShapeDtypeStruct((S, D_HALF), BF16)
    padded_b = pltpu.VMEM((CHUNK + S, LANES), BF16)
    padded_f = pltpu.VMEM((CHUNK + S, LANES), F32)
    whole = pltpu.VMEM((S, LANES), F32)
    return pl.pallas_call(
        body, name="attention_bwd", grid=(4,), out_shape=(out, out, out),
        in_specs=[slab] * 6 + [tab] * 3, out_specs=(out_slab,) * 3,
        scratch_shapes=[padded_b] * 4 + [pltpu.VMEM((2, CHUNK, 2 * CHUNK), F32)] + [whole] * 3
        + [padded_f] * 4 + [whole] * 3,
        compiler_params=_params(("parallel",), 60),
    )(q, k, v, lse, d_attn, delta, *tabs)


def _proj_in_bwd_w(xb, parts):
    def body(x_ref, p0, p1, p2, p3, p4, gw_ref, dw_ref):
        @pl.when(pl.program_id(0) == 0)
        def _():
            dw_ref[...] = jnp.zeros_like(dw_ref)

        xt = x_ref[...]
        for n, part in enumerate((p0, p1, p2, p3, p4)):
            dw_ref[:, n * D_HALF:(n + 1) * D_HALF] += _dot_tn(xt, part[...])

        @pl.when(pl.program_id(0) == NT - 1)
        def _():
            width = D_IN // N_DEV
            for j in range(N_DEV):
                gw_ref[_owner_slot(j)] = dw_ref[:, width * j:width * (j + 1)].astype(BF16)

    return pl.pallas_call(
        body, name="proj_in_bwd_w", grid=(NT,), out_shape=jax.ShapeDtypeStruct((N_DEV, D, D_IN // N_DEV), BF16),
        in_specs=[_row_spec(D)] + [_row_spec(D_HALF)] * 5, out_specs=_full_spec((N_DEV, D, D_IN // N_DEV)),
        scratch_shapes=[pltpu.VMEM((D, D_IN), F32)],
        compiler_params=_params(("arbitrary",)),
    )(xb, *parts)


def _proj_in_bwd_x(dy1, parts, w_in):
    def body(dy_ref, p0, p1, p2, p3, p4, w_ref, gx_ref):
        acc = ALPHA * dy_ref[...]
        for n, part in enumerate((p0, p1, p2, p3, p4)):
            acc += _dot_nt(part[...], w_ref[:, n * D_HALF:(n + 1) * D_HALF])
        gx_ref[...] = acc

    return pl.pallas_call(
        body, name="proj_in_bwd_x", grid=(NT,), out_shape=jax.ShapeDtypeStruct((S, D), F32),
        in_specs=[_row_spec(D)] + [_row_spec(D_HALF)] * 5 + [_full_spec((D, D_IN))], out_specs=_row_spec(D),
        compiler_params=_params(("parallel",)),
    )(dy1, *parts, w_in)


def _to_natural(blocks, name):
    n, rows, w = blocks.shape
    tile = min(rows, 256)

    def body(i_ref, o_ref):
        o_ref[...] = jnp.concatenate([i_ref[j] for j in range(n)], axis=1)

    return pl.pallas_call(
        body, name=name, grid=(rows // tile,), out_shape=jax.ShapeDtypeStruct((rows, n * w), blocks.dtype),
        in_specs=[pl.BlockSpec((n, tile, w), lambda i: (0, i, 0))],
        out_specs=pl.BlockSpec((tile, n * w), lambda i: (i, 0)), compiler_params=_params(("parallel",)),
    )(blocks)


def _column_blocks(full, name):
    rows, cols = full.shape
    w = cols // N_DEV
    tile = 256

    def body(i_ref, o_ref):
        for j in range(N_DEV):
            o_ref[_owner_slot(j)] = i_ref[:, j * w:(j + 1) * w].astype(BF16)

    return pl.pallas_call(
        body, name=name, grid=(rows // tile,), out_shape=jax.ShapeDtypeStruct((N_DEV, rows, w), BF16),
        in_specs=[pl.BlockSpec((tile, cols), lambda i: (i, 0))],
        out_specs=pl.BlockSpec((N_DEV, tile, w), lambda i: (0, i, 0)), compiler_params=_params(("parallel",)),
    )(full)


def _row_blocks(full):
    rows, cols = full.shape
    r = rows // N_DEV

    def body(i_ref, o_ref):
        o_ref[0] = i_ref[...].astype(BF16)

    return pl.pallas_call(
        body, name="ff_down_grad_blocks", grid=(N_DEV,), out_shape=jax.ShapeDtypeStruct((N_DEV, r, cols), BF16),
        in_specs=[pl.BlockSpec((r, cols), lambda s: ((s % 4) * 2 + s // 4, 0))],
        out_specs=pl.BlockSpec((1, r, cols), lambda s: (s, 0, 0)), compiler_params=_params(("parallel",)),
    )(full)


def _local_step(x, p, pos_col, target, wb, sm):
    bs_exp = jnp.repeat(sm["b_s"].T, 64, axis=1)
    conv_w8 = wb["conv_w8"]
    tabs = _rope_tables(pos_col)
    q, k, v, u_pre, z_pre, gm, xb = _proj_in_fwd(x, wb["w_in"], tabs, sm["ln_z_g"], sm["ln_z_b"],
                                                 sm["w_s"], bs_exp)
    attn, lse = _attention_fwd(q, k, v)
    xhat1, rstd1, x1b = _mix_ln1_fwd(attn, gm, wb["w_o"], x, sm["ln1_g"], sm["ln1_b"])
    a_pre, a, b_lin, hff = _ffn_up_fwd(x1b, wb["w_ff_a"], wb["w_ff_b"], conv_w8, sm["conv_b"])
    xhat2, rstd2 = _ffn_down_ln2_fwd(hff, wb["w_ff_down"], xhat1, sm["ln1_g"], sm["ln1_b"])
    loss, dy2, dy2b, dw_g, dw_p, vec_tail = _tail_fwd_bwd(
        xhat2, rstd2, p, target, wb["w_ple_gate"], wb["w_ple_in"], sm["ln2_g"], sm["ln2_b"],
        sm["b_ple_gate"], sm["ln3_g"], sm["ln3_b"])
    dw_down, dw_a, dw_b, dconv, dx1_ffn = _ffn_bwd(dy2b, x1b, wb["w_ff_down"], wb["w_ff_a"], wb["w_ff_b"],
                                                    a_pre, a, b_lin, hff, conv_w8)
    dy1, d_attn, delta, d_gm, dw_o, vec_ln1 = _ln1_mix_bwd(dy2, dx1_ffn, xhat1, rstd1, sm["ln1_g"], attn, gm,
                                                            wb["w_o"])
    du, dz, dws, dbs, vec_z = _gmlp_bwd(d_gm, u_pre, z_pre, sm["ln_z_g"], sm["ln_z_b"], sm["w_s"], bs_exp)
    dq, dk, dv = _attention_bwd(q, k, v, lse, d_attn, delta, tabs)
    parts = (dq, dk, dv, du, dz)
    dw_in = _proj_in_bwd_w(xb, parts)
    grad_x = _proj_in_bwd_x(dy1, parts, wb["w_in"])
    big = {"w_in": dw_in, "w_o": dw_o, "w_ff_a": _column_blocks(dw_a, "ff_a_grad_blocks"),
           "w_ff_b": _column_blocks(dw_b, "ff_b_grad_blocks"), "w_ff_down": _row_blocks(dw_down),
           "w_ple_gate": dw_g, "w_ple_in": dw_p}
    small = {"tail": vec_tail, "ln1": vec_ln1, "ln_z": vec_z, "conv": dconv, "w_s": dws, "b_s": dbs, "loss": loss}
    return grad_x, big, small


def _mesh_pos():
    return lax.axis_index("x"), lax.axis_index("y"), lax.axis_index("c")


def _cast_shards(shards):
    n = len(shards)

    def body(*refs):
        for a in range(n):
            refs[n + a][...] = refs[a][...].astype(BF16)

    return pl.pallas_call(
        body, name="cast_shards", out_shape=tuple(jax.ShapeDtypeStruct(s.shape, BF16) for s in shards),
        compiler_params=_params(None),
    )(*shards)


def _all_gather(shards):
    n_arr = len(shards)

    def body(*refs):
        x_refs, out_refs = refs[:n_arr], refs[n_arr:2 * n_arr]
        send_sems, recv_sems, local_sems = refs[2 * n_arr:]
        x, y, c = _mesh_pos()
        me, sibling = (x, y, c), (x, y, 1 - c)
        chips = [(1 - x, y), (x, 1 - y), (1 - x, 1 - y)]

        def copy(a, n, block, to, from_shard=False):
            dst = out_refs[a].at[4 * block[0] + 2 * block[1] + block[2]]
            return pltpu.make_async_remote_copy(
                src_ref=x_refs[a] if from_shard else dst, dst_ref=dst, send_sem=send_sems.at[7 * a + n],
                recv_sem=recv_sems.at[7 * a + n], device_id=to, device_id_type=MESH)

        mine = [pltpu.make_async_copy(x_refs[a], out_refs[a].at[4 * x + 2 * y + c], local_sems.at[a])
                for a in range(n_arr)]
        first = []
        for a in range(n_arr):
            mine[a].start()
            first.append(copy(a, 0, me, sibling, from_shard=True))
            first += [copy(a, 1 + n, me, (*chip, c), from_shard=True) for n, chip in enumerate(chips)]
        for cp in first:
            cp.start()
        passed = []
        for n, chip in enumerate(chips):
            for a in range(n_arr):
                copy(a, 1 + n, (*chip, c), me).wait_recv()
                passed.append(copy(a, 4 + n, (*chip, c), sibling))
                passed[-1].start()
        for a in range(n_arr):
            copy(a, 0, sibling, me).wait_recv()
        for n, chip in enumerate(chips):
            for a in range(n_arr):
                copy(a, 4 + n, (*chip, 1 - c), me).wait_recv()
        for cp in first + passed:
            cp.wait_send()
        for cp in mine:
            cp.wait()

    anywhere = pl.BlockSpec(memory_space=pl.ANY)
    return pl.pallas_call(
        body, name="all_gather_weights",
        out_shape=tuple(jax.ShapeDtypeStruct((N_DEV,) + s.shape, s.dtype) for s in shards),
        in_specs=[anywhere] * n_arr, out_specs=(anywhere,) * n_arr,
        scratch_shapes=[pltpu.SemaphoreType.DMA((7 * n_arr,)), pltpu.SemaphoreType.DMA((7 * n_arr,)),
                        pltpu.SemaphoreType.DMA((n_arr,))],
    )(*shards)


def _sibling_exchange(big, small):
    n_big, n_small = len(big), len(small)

    def body(*refs):
        big_refs, small_refs = refs[:n_big], refs[n_big:n_big + n_small]
        outs = refs[n_big + n_small:2 * (n_big + n_small)]
        from_sib, gathered = outs[:n_big], outs[n_big:]
        send_b, recv_b, send_s, recv_s, local_sems = refs[2 * (n_big + n_small):]
        x, y, c = _mesh_pos()
        me = 4 * x + 2 * y + c
        copies = [pltpu.make_async_copy(small_refs[k], gathered[k].at[me], local_sems.at[k])
                  for k in range(n_small)]
        for a in range(n_big):
            copies.append(pltpu.make_async_remote_copy(
                src_ref=big_refs[a].at[1 - c], dst_ref=from_sib[a], send_sem=send_b.at[a], recv_sem=recv_b.at[a],
                device_id=(x, y, 1 - c), device_id_type=MESH))
        for flip in range(1, N_DEV):
            px = 1 - x if flip & 4 else x
            py = 1 - y if flip & 2 else y
            pc = 1 - c if flip & 1 else c
            for k in range(n_small):
                n = (flip - 1) * n_small + k
                copies.append(pltpu.make_async_remote_copy(
                    src_ref=small_refs[k], dst_ref=gathered[k].at[me], send_sem=send_s.at[n],
                    recv_sem=recv_s.at[n], device_id=(px, py, pc), device_id_type=MESH))
        for cp in copies:
            cp.start()
        for cp in copies:
            cp.wait()

    anywhere = pl.BlockSpec(memory_space=pl.ANY)
    n_all = n_big + n_small
    return pl.pallas_call(
        body, name="sibling_exchange",
        out_shape=tuple(jax.ShapeDtypeStruct(b.shape[1:], b.dtype) for b in big)
        + tuple(jax.ShapeDtypeStruct((N_DEV,) + s.shape, s.dtype) for s in small),
        in_specs=[anywhere] * n_all, out_specs=(anywhere,) * n_all,
        scratch_shapes=[pltpu.SemaphoreType.DMA((n_big,)), pltpu.SemaphoreType.DMA((n_big,)),
                        pltpu.SemaphoreType.DMA((7 * n_small,)), pltpu.SemaphoreType.DMA((7 * n_small,)),
                        pltpu.SemaphoreType.DMA((n_small,))],
    )(*big, *small)


def _chip_reduce(big, from_sibling, core):
    n = len(big)

    def body(core_ref, *refs):
        for a in range(n):
            mine, theirs, out = refs[a], refs[n + a], refs[2 * n + a]
            out[0] = (mine[0, 0].astype(F32) + theirs[0].astype(F32)).astype(BF16)

    def block(shape):
        return pl.BlockSpec((1,) + shape, lambda ch, core_ref: (ch, 0, 0))

    grid_spec = pltpu.PrefetchScalarGridSpec(
        num_scalar_prefetch=1, grid=(4,),
        in_specs=[pl.BlockSpec((1, 1) + b.shape[2:], lambda ch, core_ref: (core_ref[0], ch, 0, 0)) for b in big]
        + [block(b.shape[2:]) for b in big],
        out_specs=[block(b.shape[2:]) for b in big])
    return pl.pallas_call(
        body, name="chip_reduce", grid_spec=grid_spec,
        out_shape=tuple(jax.ShapeDtypeStruct(b.shape[1:], BF16) for b in big),
        compiler_params=_params(("parallel",)),
    )(core, *big, *from_sibling)


def _chip_exchange(sums):
    n_arr = len(sums)

    def body(*refs):
        src, dst = refs[:n_arr], refs[n_arr:2 * n_arr]
        send_sems, recv_sems, local_sems = refs[2 * n_arr:]
        x, y, c = _mesh_pos()
        my_chip = 2 * x + y
        copies = [pltpu.make_async_copy(src[a].at[my_chip], dst[a].at[my_chip], local_sems.at[a])
                  for a in range(n_arr)]
        for n, (px, py) in enumerate([(1 - x, y), (x, 1 - y), (1 - x, 1 - y)]):
            for a in range(n_arr):
                copies.append(pltpu.make_async_remote_copy(
                    src_ref=src[a].at[2 * px + py], dst_ref=dst[a].at[my_chip], send_sem=send_sems.at[3 * a + n],
                    recv_sem=recv_sems.at[3 * a + n], device_id=(px, py, c), device_id_type=MESH))
        for cp in copies:
            cp.start()
        for cp in copies:
            cp.wait()

    anywhere = pl.BlockSpec(memory_space=pl.ANY)
    return pl.pallas_call(
        body, name="chip_exchange", out_shape=tuple(jax.ShapeDtypeStruct(s.shape, s.dtype) for s in sums),
        in_specs=[anywhere] * n_arr, out_specs=(anywhere,) * n_arr,
        scratch_shapes=[pltpu.SemaphoreType.DMA((3 * n_arr,)), pltpu.SemaphoreType.DMA((3 * n_arr,)),
                        pltpu.SemaphoreType.DMA((n_arr,))],
    )(*sums)


def _adamw(g, w, m, v):
    nm = ADAM_B1 * m + (1.0 - ADAM_B1) * g
    nv = ADAM_B2 * v + (1.0 - ADAM_B2) * (g * g)
    m_hat = nm / (1.0 - ADAM_B1 ** ADAM_STEP)
    v_hat = nv / (1.0 - ADAM_B2 ** ADAM_STEP)
    return -ADAM_LR * (m_hat / (jnp.sqrt(v_hat) + ADAM_EPS) + ADAM_WD * w), nm, nv


def _adamw_sharded(parts, w, m, v, name):
    def body(p_ref, w_ref, m_ref, v_ref, g_ref, d_ref, nm_ref, nv_ref):
        g = p_ref[0].astype(F32)
        for s in range(1, 4):
            g = g + p_ref[s].astype(F32)
        delta, nm, nv = _adamw(g, w_ref[0], m_ref[0], v_ref[0])
        g_ref[0] = g
        d_ref[0] = delta
        nm_ref[0] = nm
        nv_ref[0] = nv

    return pl.pallas_call(
        body, name=name, out_shape=(jax.ShapeDtypeStruct(w.shape, F32),) * 4, compiler_params=_params(None),
    )(parts, w, m, v)


REPLICATED = (("ln_z_g", "ln_z", 0), ("ln_z_b", "ln_z", 1), ("w_s", "w_s", None), ("b_s", "b_s", None),
              ("ln1_g", "ln1", 0), ("ln1_b", "ln1", 1), ("conv_w", "conv_mine", None), ("conv_b", "conv", 3),
              ("ln2_g", "tail", 3), ("ln2_b", "tail", 4), ("b_ple_gate", "tail", 0), ("ln3_g", "tail", 1),
              ("ln3_b", "tail", 2))
GATHERED = ("tail", "ln1", "ln_z", "conv", "w_s", "b_s", "loss", "conv_mine")


def _adamw_replicated(gathered, w, m, v):
    n_par = len(REPLICATED)

    def body(*refs):
        srcs = dict(zip(GATHERED, refs[:len(GATHERED)]))
        rest = refs[len(GATHERED):]
        w_refs, m_refs, v_refs = rest[:n_par], rest[n_par:2 * n_par], rest[2 * n_par:3 * n_par]
        outs = rest[3 * n_par:]
        loss_ref = outs[4 * n_par]
        sums = {}
        for key, ref in srcs.items():
            total = ref[0]
            for dev in range(1, N_DEV):
                total = total + ref[dev]
            sums[key] = total
        loss_ref[...] = sums["loss"]
        for n, (name, key, row) in enumerate(REPLICATED):
            if name == "conv_w":
                g = sums[key][0:3, :]
            elif row is None:
                g = sums[key]
            else:
                g = sums[key][row:row + 1, :]
            lead = len(w_refs[n].shape) - g.ndim
            idx = (0,) * lead + (Ellipsis,)
            delta, nm, nv = _adamw(g, w_refs[n][idx], m_refs[n][idx], v_refs[n][idx])
            for kind, val in enumerate((g, delta, nm, nv)):
                outs[kind * n_par + n][idx] = val

    names = [name for name, _, _ in REPLICATED]
    shapes = [jax.ShapeDtypeStruct(w[name].shape, F32) for name in names]
    return pl.pallas_call(
        body, name="adamw_replicated", out_shape=tuple(shapes * 4) + (jax.ShapeDtypeStruct((8, LANES), F32),),
        compiler_params=_params(None),
    )(*[gathered[k] for k in GATHERED], *[w[n] for n in names], *[m[n] for n in names], *[v[n] for n in names])


def kernel(x, p, positions, w_in, ln_z_g, ln_z_b, w_s, b_s, w_o, ln1_g, ln1_b, w_ff_a, w_ff_b, conv_w, conv_b, w_ff_down, ln2_g, ln2_b, w_ple_gate, b_ple_gate, w_ple_in, ln3_g, ln3_b, loss_target, m_w_in, m_ln_z_g, m_ln_z_b, m_w_s, m_b_s, m_w_o, m_ln1_g, m_ln1_b, m_w_ff_a, m_w_ff_b, m_conv_w, m_conv_b, m_w_ff_down, m_ln2_g, m_ln2_b, m_w_ple_gate, m_b_ple_gate, m_w_ple_in, m_ln3_g, m_ln3_b, v_w_in, v_ln_z_g, v_ln_z_b, v_w_s, v_b_s, v_w_o, v_ln1_g, v_ln1_b, v_w_ff_a, v_w_ff_b, v_conv_w, v_conv_b, v_w_ff_down, v_ln2_g, v_ln2_b, v_w_ple_gate, v_b_ple_gate, v_w_ple_in, v_ln3_g, v_ln3_b):
    w = dict(w_in=w_in, ln_z_g=ln_z_g, ln_z_b=ln_z_b, w_s=w_s, b_s=b_s, w_o=w_o, ln1_g=ln1_g, ln1_b=ln1_b,
             w_ff_a=w_ff_a, w_ff_b=w_ff_b, conv_w=conv_w, conv_b=conv_b, w_ff_down=w_ff_down, ln2_g=ln2_g,
             ln2_b=ln2_b, w_ple_gate=w_ple_gate, b_ple_gate=b_ple_gate, w_ple_in=w_ple_in, ln3_g=ln3_g,
             ln3_b=ln3_b)
    m = dict(w_in=m_w_in, ln_z_g=m_ln_z_g, ln_z_b=m_ln_z_b, w_s=m_w_s, b_s=m_b_s, w_o=m_w_o, ln1_g=m_ln1_g,
             ln1_b=m_ln1_b, w_ff_a=m_w_ff_a, w_ff_b=m_w_ff_b, conv_w=m_conv_w, conv_b=m_conv_b,
             w_ff_down=m_w_ff_down, ln2_g=m_ln2_g, ln2_b=m_ln2_b, w_ple_gate=m_w_ple_gate,
             b_ple_gate=m_b_ple_gate, w_ple_in=m_w_ple_in, ln3_g=m_ln3_g, ln3_b=m_ln3_b)
    v = dict(w_in=v_w_in, ln_z_g=v_ln_z_g, ln_z_b=v_ln_z_b, w_s=v_w_s, b_s=v_b_s, w_o=v_w_o, ln1_g=v_ln1_g,
             ln1_b=v_ln1_b, w_ff_a=v_w_ff_a, w_ff_b=v_w_ff_b, conv_w=v_conv_w, conv_b=v_conv_b,
             w_ff_down=v_w_ff_down, ln2_g=v_ln2_g, ln2_b=v_ln2_b, w_ple_gate=v_w_ple_gate,
             b_ple_gate=v_b_ple_gate, w_ple_in=v_w_ple_in, ln3_g=v_ln3_g, ln3_b=v_ln3_b)
    big_names = [name for name, _, _ in BIG]
    small_names = ("ln_z_g", "ln_z_b", "w_s", "b_s", "ln1_g", "ln1_b", "conv_b", "ln2_g", "ln2_b", "b_ple_gate",
                   "ln3_g", "ln3_b")

    shards = _cast_shards([w[n][0] for n in big_names])
    conv_rows = jnp.pad(w["conv_w"][0], ((0, 5), (0, 0)))
    gathered = _all_gather([*shards, conv_rows])
    wb = {}
    for (name, (r, c), axis), blocks in zip(BIG, gathered):
        wb[name] = blocks.reshape(N_DEV * r, c) if axis == 0 else _to_natural(blocks, name + "_natural")
    wb["conv_w8"] = _to_natural(gathered[-1], "conv_w_natural")
    sm = {n: w[n][0] if w[n].ndim > 2 else w[n] for n in small_names}
    pos_col = positions.reshape(S, 1).astype(F32)
    grad_x, g_big, g_small = _local_step(x[0], p[0, 0], pos_col, loss_target[0], wb, sm)

    by_core = [g_big[n].reshape((2, 4) + g_big[n].shape[1:]) for n in big_names]
    small_keys = GATHERED[:-1]
    exchanged = _sibling_exchange(by_core, [g_small[k] for k in small_keys])
    from_sibling, small_all = exchanged[:len(big_names)], dict(zip(small_keys, exchanged[len(big_names):]))
    core = lax.axis_index("c").astype(jnp.int32).reshape(1)
    chip_sums = _chip_exchange(_chip_reduce(by_core, from_sibling, core))
    me = 4 * lax.axis_index("x") + 2 * lax.axis_index("y") + lax.axis_index("c")
    conv_cols = small_all["conv"].reshape(N_DEV, 8, N_DEV, D_FF // N_DEV)
    small_all["conv_mine"] = lax.dynamic_index_in_dim(conv_cols, me, axis=2, keepdims=False)

    leaves = {}
    for name, parts in zip(big_names, chip_sums):
        leaves[name] = _adamw_sharded(parts, w[name], m[name], v[name], "adamw_" + name)
    rep = _adamw_replicated(small_all, w, m, v)
    n_rep = len(REPLICATED)
    for n, (name, _, _) in enumerate(REPLICATED):
        leaves[name] = tuple(rep[kind * n_rep + n] for kind in range(4))
    loss = rep[4 * n_rep][0, 0]
    return (loss, grad_x[None], *[leaves[n][kind] for kind in range(4) for n in WEIGHT_ORDER])
```

```python
import math

import numpy as np
import jax
import jax.numpy as jnp
from jax import lax
from jax.experimental import pallas as pl
from jax.experimental.pallas import tpu as pltpu

F32 = jnp.float32
BF16 = jnp.bfloat16
MESH = pl.DeviceIdType.MESH

N_DEV = 8
S = 4096
D = 1024
D_HALF = 512
D_IN = 2560
D_FF = 2816
D_PLE = 256
CHUNK = 128
DILATIONS = ((1, 32), (4, 8), (16, 2))
ROPE_THETA = 500000.0
LN_EPS = 1e-5
ALPHA = 2.0 ** 0.25
NEG_INF = -1e30
INV_SQRT2 = 1.0 / math.sqrt(2.0)
INV_SQRT_2PI = 1.0 / math.sqrt(2.0 * math.pi)

ADAM_LR, ADAM_B1, ADAM_B2, ADAM_EPS, ADAM_WD, ADAM_STEP = 0.001, 0.9, 0.999, 1e-08, 0.01, 10

TM = 512
NT = S // TM
ATTN_UNROLL = 4
TN = 256
NJ = D_FF // TN
LANES = 128
VMEM_MIB = 1024 * 1024

BIG = (("w_in", (1024, 320), 1), ("w_o", (128, 1024), 0), ("w_ff_a", (1024, 352), 1),
       ("w_ff_b", (1024, 352), 1), ("w_ff_down", (352, 1024), 0), ("w_ple_gate", (128, 1024), 0),
       ("w_ple_in", (256, 128), 1))
WEIGHT_ORDER = ("w_in", "ln_z_g", "ln_z_b", "w_s", "b_s", "w_o", "ln1_g", "ln1_b", "w_ff_a", "w_ff_b",
                "conv_w", "conv_b", "w_ff_down", "ln2_g", "ln2_b", "w_ple_gate", "b_ple_gate",
                "w_ple_in", "ln3_g", "ln3_b")


def _params(semantics=None, vmem_mib=48):
    return pltpu.CompilerParams(dimension_semantics=semantics, vmem_limit_bytes=vmem_mib * VMEM_MIB)


def _dot(a, b):
    return jnp.dot(a, b, preferred_element_type=F32)


def _dot_nt(a, b):
    return lax.dot_general(a, b, (((1,), (1,)), ((), ())), preferred_element_type=F32)


def _dot_tn(a, b):
    return lax.dot_general(a, b, (((0,), (0,)), ((), ())), preferred_element_type=F32)


def _gelu(x):
    return 0.5 * x * (1.0 + lax.erf(x * INV_SQRT2))


def _gelu_grad(x):
    return 0.5 * (1.0 + lax.erf(x * INV_SQRT2)) + x * (jnp.exp(-0.5 * x * x) * INV_SQRT_2PI)


def _ln_stats(y):
    mu = jnp.mean(y, axis=-1, keepdims=True)
    yc = y - mu
    var = jnp.mean(yc * yc, axis=-1, keepdims=True)
    rstd = lax.rsqrt(var + LN_EPS)
    return yc * rstd, rstd


def _ln_bwd(dxhat, xhat, rstd):
    m1 = jnp.mean(dxhat, axis=-1, keepdims=True)
    m2 = jnp.mean(dxhat * xhat, axis=-1, keepdims=True)
    return rstd * (dxhat - m1 - xhat * m2)


def _colsum(x):
    return jnp.sum(x, axis=0, keepdims=True)


def _rows(i):
    return (i, 0)


def _fixed(*_):
    return (0, 0)


def _row_spec(width):
    return pl.BlockSpec((TM, width), _rows)


def _full_spec(shape):
    return pl.BlockSpec(shape, lambda *_: (0,) * len(shape))


def _owner_slot(j):
    return (j % 2) * 4 + j // 2


def _lane_lo():
    return lax.broadcasted_iota(jnp.int32, (CHUNK, LANES), 1) < 64


def _tril():
    r = lax.broadcasted_iota(jnp.int32, (CHUNK, CHUNK), 0)
    c = lax.broadcasted_iota(jnp.int32, (CHUNK, CHUNK), 1)
    return c <= r


def _rope_consts():
    lane = np.arange(LANES) % 64
    j = lane % 8
    inv = np.where(lane < 16, np.float32(ROPE_THETA) ** (-(2.0 * j).astype(np.float32) / np.float32(16.0)), 0.0)
    m_lo = (lane < 8).astype(np.float32)
    m_hi = ((lane >= 8) & (lane < 16)).astype(np.float32)
    return (jnp.asarray(inv, F32).reshape(1, LANES), jnp.asarray(m_lo).reshape(1, LANES),
            jnp.asarray(m_hi).reshape(1, LANES))


def _rope_tables(pos_col):
    inv, m_lo, m_hi = _rope_consts()

    def body(pos_ref, inv_ref, lo_ref, hi_ref, c_ref, sa_ref, sb_ref):
        ang = pos_ref[...] * inv_ref[...]
        c = jnp.cos(ang)
        s = jnp.sin(ang)
        lo = lo_ref[...]
        hi = hi_ref[...]
        c_ref[...] = jnp.where(lo + hi > 0.0, c, 1.0)
        sa_ref[...] = s * hi
        sb_ref[...] = -s * lo

    vec = _full_spec((1, LANES))
    out = jax.ShapeDtypeStruct((S, LANES), F32)
    return pl.pallas_call(
        body, name="rope_tables", grid=(NT,), out_shape=(out, out, out),
        in_specs=[pl.BlockSpec((TM, 1), _rows), vec, vec, vec],
        out_specs=(_row_spec(LANES),) * 3, compiler_params=_params(("parallel",)),
    )(pos_col, inv, m_lo, m_hi)


def _rope(t, c, sa, sb):
    return t * c + pltpu.roll(t, 8, 1) * sa + pltpu.roll(t, LANES - 8, 1) * sb


def _rope_t(dy, c, sa, sb):
    return dy * c + pltpu.roll(dy * sa, LANES - 8, 1) + pltpu.roll(dy * sb, 8, 1)


def _masked_ws(ws_ref):
    tril = _tril()
    return [jnp.where(tril, ws_ref[g], 0.0).astype(BF16) for g in range(8)]


def _spatial_mix(zn, wm, bs):
    lo = _lane_lo()
    rows = []
    for ch in range(TM // CHUNK):
        slabs = []
        for pr in range(4):
            zp = zn[ch * CHUNK:(ch + 1) * CHUNK, pr * LANES:(pr + 1) * LANES].astype(BF16)
            slabs.append(jnp.where(lo, _dot(wm[2 * pr], zp), _dot(wm[2 * pr + 1], zp)))
        rows.append(jnp.concatenate(slabs, axis=1) + bs)
    return jnp.concatenate(rows, axis=0)


def _proj_in_fwd(x, w_in, tabs, ln_z_g, ln_z_b, w_s, bs_exp):
    def body(x_ref, w_ref, c_ref, sa_ref, sb_ref, g_ref, b_ref, ws_ref, bs_ref,
             q_ref, k_ref, v_ref, u_ref, z_ref, gm_ref, xb_ref):
        xb = x_ref[...].astype(BF16)
        xb_ref[...] = xb
        c, sa, sb = c_ref[...], sa_ref[...], sb_ref[...]
        hq = _dot(xb, w_ref[:, 0:512])
        hk = _dot(xb, w_ref[:, 512:1024])
        for s in range(4):
            sl = slice(s * LANES, (s + 1) * LANES)
            q_ref[:, sl] = _rope(hq[:, sl], c, sa, sb)
            k_ref[:, sl] = _rope(hk[:, sl], c, sa, sb)
        v_ref[...] = _dot(xb, w_ref[:, 1024:1536])
        u_pre = _dot(xb, w_ref[:, 1536:2048])
        z_pre = _dot(xb, w_ref[:, 2048:2560])
        u_ref[...] = u_pre
        z_ref[...] = z_pre
        zhat, _ = _ln_stats(_gelu(z_pre))
        zn = zhat * g_ref[...] + b_ref[...]
        mixed = _spatial_mix(zn, _masked_ws(ws_ref), bs_ref[...])
        gm_ref[...] = (_gelu(u_pre) * mixed).astype(BF16)

    half = jax.ShapeDtypeStruct((S, D_HALF), F32)
    tab = _row_spec(LANES)
    return pl.pallas_call(
        body, name="proj_in_fwd", grid=(NT,),
        out_shape=(half, half, half, half, half, jax.ShapeDtypeStruct((S, D_HALF), BF16),
                   jax.ShapeDtypeStruct((S, D), BF16)),
        in_specs=[_row_spec(D), _full_spec((D, D_IN)), tab, tab, tab, _full_spec((1, D_HALF)),
                  _full_spec((1, D_HALF)), _full_spec((8, CHUNK, CHUNK)), _full_spec((CHUNK, D_HALF))],
        out_specs=(_row_spec(D_HALF),) * 6 + (_row_spec(D),),
        compiler_params=_params(("parallel",)),
    )(x, w_in, *tabs, ln_z_g, ln_z_b, w_s, bs_exp)


def _store_band_bias(bias_ref):
    qi = lax.broadcasted_iota(jnp.int32, (CHUNK, 2 * CHUNK), 0)
    kj = lax.broadcasted_iota(jnp.int32, (CHUNK, 2 * CHUNK), 1)
    band = (kj >= qi) & (kj <= qi + CHUNK)
    bias_ref[0] = jnp.where(band, 0.0, NEG_INF)
    bias_ref[1] = jnp.where(band & (kj >= CHUNK), 0.0, NEG_INF)


def _permuted_rows(ref, d, r):
    return ref[...] if d == 1 else ref[pl.ds(r, S // d, stride=d), :]


def _attention_fwd(q, k, v, carried=None):
    def body(q_ref, k_ref, v_ref, o_ref, lse_ref, qb, kb, v0b, v1b, bias, op, lp, ob0, lb0, ob1, lb1, ob2, lb2):
        lo = _lane_lo()
        lo_f = lo.astype(F32)[0:1, :]
        hi_f = 1.0 - lo_f
        zero_pad = jnp.zeros((CHUNK, LANES), BF16)
        for buf in (qb, kb, v0b, v1b):
            buf[0:CHUNK, :] = zero_pad
        _store_band_bias(bias)
        outs = ((ob0, lb0), (ob1, lb1), (ob2, lb2))
        for (d, nb), (ob, lb) in zip(DILATIONS, outs):
            length = S // d
            for r in range(d):
                dst = slice(CHUNK + r * length, CHUNK + (r + 1) * length)
                qb[dst, :] = (_permuted_rows(q_ref, d, r) * 0.125).astype(BF16)
                kb[dst, :] = _permuted_rows(k_ref, d, r).astype(BF16)
                vs = _permuted_rows(v_ref, d, r)
                v0b[dst, :] = (vs * lo_f + hi_f).astype(BF16)
                v1b[dst, :] = (vs * hi_f + lo_f).astype(BF16)

            def block(b, carry, nb=nb):
                base = pl.multiple_of(b * CHUNK, CHUNK)
                add = bias[jnp.where(b % nb == 0, 1, 0)]
                qblk = qb[pl.ds(pl.multiple_of(base + CHUNK, CHUNK), CHUNK), :]
                kblk = kb[pl.ds(base, 2 * CHUNK), :]
                pv, mx = [], []
                for head, vh in enumerate((v0b, v1b)):
                    qh = jnp.where(lo, qblk, 0) if head == 0 else jnp.where(lo, 0, qblk)
                    s = _dot_nt(qh, kblk) + add
                    m = jnp.max(s, axis=-1, keepdims=True)
                    p = jnp.exp(s - m).astype(BF16)
                    pv.append(_dot(p, vh[pl.ds(base, 2 * CHUNK), :]))
                    mx.append(m)
                den = pltpu.roll(jnp.where(lo, pv[1], pv[0]), 64, 1)
                op[pl.ds(base, CHUNK), :] = jnp.where(lo, pv[0], pv[1]) / den
                lp[pl.ds(base, CHUNK), :] = jnp.where(lo, mx[0], mx[1]) + jnp.log(den)
                return carry

            lax.fori_loop(0, S // CHUNK, block, 0, unroll=ATTN_UNROLL)
            for r in range(d):
                src = slice(r * length, (r + 1) * length)
                if d == 1:
                    ob[...] = op[...]
                    lb[...] = lp[...]
                else:
                    ob[pl.ds(r, length, stride=d), :] = op[src, :]
                    lb[pl.ds(r, length, stride=d), :] = lp[src, :]
        for t in range(NT):
            rows = slice(t * TM, (t + 1) * TM)
            l0, l1, l2 = lb0[rows, :], lb1[rows, :], lb2[rows, :]
            mx = jnp.maximum(jnp.maximum(l0, l1), l2)
            e0, e1, e2 = jnp.exp(l0 - mx), jnp.exp(l1 - mx), jnp.exp(l2 - mx)
            den = e0 + e1 + e2
            o_ref[rows, :] = (e0 * ob0[rows, :] + e1 * ob1[rows, :] + e2 * ob2[rows, :]) / den
            lse_ref[rows, :] = mx + jnp.log(den)

    slab = pl.BlockSpec((S, LANES), lambda h: (0, h))
    out = jax.ShapeDtypeStruct((S, D_HALF), F32)
    padded = pltpu.VMEM((CHUNK + S, LANES), BF16)
    whole = pltpu.VMEM((S, LANES), F32)
    return _host_call(
        body, carried, name="attention_fwd", grid=(4,), out_shape=(out, out),
        in_specs=[slab, slab, slab], out_specs=(slab, slab),
        scratch_shapes=[padded] * 4 + [pltpu.VMEM((2, CHUNK, 2 * CHUNK), F32)] + [whole] * 8,
        params=_params(("arbitrary",), 56), args=(q, k, v))


def _mix_ln1_fwd(attn, gm, w_o, x, g1, b1):
    def body(a_ref, gm_ref, w_ref, x_ref, g_ref, b_ref, xhat_ref, rstd_ref, x1b_ref):
        mix = _dot(a_ref[...].astype(BF16), w_ref[0:D_HALF, :]) + _dot(gm_ref[...], w_ref[D_HALF:D, :])
        xhat, rstd = _ln_stats(ALPHA * x_ref[...] + mix)
        xhat_ref[...] = xhat
        rstd_ref[...] = rstd
        x1b_ref[...] = (xhat * g_ref[...] + b_ref[...]).astype(BF16)

    vec = _full_spec((1, D))
    return pl.pallas_call(
        body, name="mix_ln1_fwd", grid=(NT,),
        out_shape=(jax.ShapeDtypeStruct((S, D), F32), jax.ShapeDtypeStruct((S, 1), F32),
                   jax.ShapeDtypeStruct((S, D), BF16)),
        in_specs=[_row_spec(D_HALF), _row_spec(D_HALF), _full_spec((D, D)), _row_spec(D), vec, vec],
        out_specs=(_row_spec(D), pl.BlockSpec((TM, 1), _rows), _row_spec(D)),
        compiler_params=_params(("parallel",)),
    )(attn, gm, w_o, x, g1, b1)


def _ffn_up_fwd(x1b, w_a, w_b, conv_w8, conv_b, carried=None):
    def body(x_ref, wa_ref, wb_ref, cw_ref, cb_ref, ap_ref, a_ref, bl_ref, h_ref, carry):
        @pl.when(pl.program_id(1) == 0)
        def _():
            carry[...] = jnp.zeros_like(carry)

        xb = x_ref[...]
        ap = _dot(xb, wa_ref[...])
        bl = _dot(xb, wb_ref[...])
        row = lax.broadcasted_iota(jnp.int32, (TM, TN), 0)
        c6, c7 = carry[6:7, :], carry[7:8, :]
        m1 = jnp.where(row == 0, c7, pltpu.roll(ap, 1, 0))
        m2 = jnp.where(row == 0, c6, jnp.where(row == 1, c7, pltpu.roll(ap, 2, 0)))
        a = cb_ref[...] + cw_ref[0:1, :] * m2 + cw_ref[1:2, :] * m1 + cw_ref[2:3, :] * ap
        carry[...] = ap[TM - 8:TM, :]
        ap_ref[...] = ap
        a_ref[...] = a
        bl_ref[...] = bl
        h_ref[...] = (_gelu(a) * bl).astype(BF16)

    tile = pl.BlockSpec((TM, TN), lambda j, i: (i, j))
    wcol = pl.BlockSpec((D, TN), lambda j, i: (0, j))
    ff = jax.ShapeDtypeStruct((S, D_FF), F32)
    return _host_call(
        body, carried, name="ffn_up_fwd", grid=(NJ, NT),
        out_shape=(ff, ff, ff, jax.ShapeDtypeStruct((S, D_FF), BF16)),
        in_specs=[pl.BlockSpec((TM, D), lambda j, i: (i, 0)), wcol, wcol,
                  pl.BlockSpec((8, TN), lambda j, i: (0, j)), pl.BlockSpec((1, TN), lambda j, i: (0, j))],
        out_specs=(tile, tile, tile, tile),
        scratch_shapes=[pltpu.VMEM((8, TN), F32)],
        params=_params(("arbitrary", "arbitrary")), args=(x1b, w_a, w_b, conv_w8, conv_b))


def _ffn_down_ln2_fwd(hff, w_down, xhat1, g1, b1):
    def body(h_ref, w_ref, xh_ref, g_ref, b_ref, xhat_ref, rstd_ref):
        x1 = xh_ref[...] * g_ref[...] + b_ref[...]
        xhat, rstd = _ln_stats(ALPHA * x1 + _dot(h_ref[...], w_ref[...]))
        xhat_ref[...] = xhat
        rstd_ref[...] = rstd

    vec = _full_spec((1, D))
    return pl.pallas_call(
        body, name="ffn_down_ln2_fwd", grid=(NT,),
        out_shape=(jax.ShapeDtypeStruct((S, D), F32), jax.ShapeDtypeStruct((S, 1), F32)),
        in_specs=[_row_spec(D_FF), _full_spec((D_FF, D)), _row_spec(D), vec, vec],
        out_specs=(_row_spec(D), pl.BlockSpec((TM, 1), _rows)),
        compiler_params=_params(("parallel",)),
    )(hff, w_down, xhat1, g1, b1)


def _tail_fwd_bwd(xhat2, rstd2, p, target, w_g, w_p, g2, b2, bg, g3, b3):
    def body(xh_ref, rs_ref, p_ref, t_ref, wg_ref, wp_ref, g2_ref, b2_ref, bg_ref, g3_ref, b3_ref,
             loss_ref, dy2_ref, dy2b_ref, gwg_ref, gwp_ref, vec_ref, dwg_ref, dwp_ref):
        @pl.when(pl.program_id(0) == 0)
        def _():
            loss_ref[...] = jnp.zeros_like(loss_ref)
            dwg_ref[...] = jnp.zeros_like(dwg_ref)
            dwp_ref[...] = jnp.zeros_like(dwp_ref)
            vec_ref[...] = jnp.zeros_like(vec_ref)

        xhat2_t = xh_ref[...]
        x2 = xhat2_t * g2_ref[...] + b2_ref[...]
        x2b = x2.astype(BF16)
        pb = p_ref[...].astype(BF16)
        gate = jax.nn.sigmoid(_dot(x2b, wg_ref[...]) + bg_ref[...])
        pin = _dot(pb, wp_ref[...])
        xhat3, rstd3 = _ln_stats(ALPHA * x2 + gate * pin)
        err = xhat3 * g3_ref[...] + b3_ref[...] - t_ref[...]
        loss_ref[...] += jnp.sum(jnp.mean(err * err, axis=-1, keepdims=True), axis=0, keepdims=True) * 0.5
        dout = err * (1.0 / D)
        dy3 = _ln_bwd(dout * g3_ref[...], xhat3, rstd3)
        dgp = dy3 * pin * gate * (1.0 - gate)
        dgpb = dgp.astype(BF16)
        dwg_ref[...] += _dot_tn(x2b, dgpb)
        dwp_ref[...] += _dot_tn(pb, (dy3 * gate).astype(BF16))
        dx2 = ALPHA * dy3 + _dot_nt(dgpb, wg_ref[...])
        dy2 = _ln_bwd(dx2 * g2_ref[...], xhat2_t, rs_ref[...])
        dy2_ref[...] = dy2
        dy2b_ref[...] = dy2.astype(BF16)
        vec_ref[0:1, :] += _colsum(dgp)
        vec_ref[1:2, :] += _colsum(dout * xhat3)
        vec_ref[2:3, :] += _colsum(dout)
        vec_ref[3:4, :] += _colsum(dx2 * xhat2_t)
        vec_ref[4:5, :] += _colsum(dx2)

        @pl.when(pl.program_id(0) == NT - 1)
        def _():
            for j in range(N_DEV):
                gwg_ref[_owner_slot(j)] = dwg_ref[LANES * j:LANES * (j + 1), :].astype(BF16)
                gwp_ref[_owner_slot(j)] = dwp_ref[:, LANES * j:LANES * (j + 1)].astype(BF16)

    vec = _full_spec((1, D))
    return pl.pallas_call(
        body, name="tail_fwd_bwd", grid=(NT,),
        out_shape=(jax.ShapeDtypeStruct((8, LANES), F32), jax.ShapeDtypeStruct((S, D), F32),
                   jax.ShapeDtypeStruct((S, D), BF16), jax.ShapeDtypeStruct((N_DEV, D // N_DEV, D), BF16),
                   jax.ShapeDtypeStruct((N_DEV, D_PLE, D // N_DEV), BF16), jax.ShapeDtypeStruct((8, D), F32)),
        in_specs=[_row_spec(D), pl.BlockSpec((TM, 1), _rows), _row_spec(D_PLE), _row_spec(D),
                  _full_spec((D, D)), _full_spec((D_PLE, D)), vec, vec, vec, vec, vec],
        out_specs=(_full_spec((8, LANES)), _row_spec(D), _row_spec(D), _full_spec((N_DEV, D // N_DEV, D)),
                   _full_spec((N_DEV, D_PLE, D // N_DEV)), _full_spec((8, D))),
        scratch_shapes=[pltpu.VMEM((D, D), F32), pltpu.VMEM((D_PLE, D), F32)],
        compiler_params=_params(("arbitrary",)),
    )(xhat2, rstd2, p, target, w_g, w_p, g2, b2, bg, g3, b3)


def _ffn_bwd(dy2b, x1b, w_down, w_a, w_b, a_pre, a, b_lin, hff, conv_w8):
    def body(dy_ref, x_ref, wd_ref, wa_ref, wb_ref, ap_ref, a_ref, bl_ref, h_ref, cw_ref,
             dwd_ref, dwa_ref, dwb_ref, dcw_ref, dx_hbm, dx_acc, carry, sem):
        j, i = pl.program_id(0), pl.program_id(1)

        @pl.when(i == 0)
        def _():
            carry[...] = jnp.zeros_like(carry)
            dwd_ref[...] = jnp.zeros_like(dwd_ref)
            dwa_ref[...] = jnp.zeros_like(dwa_ref)
            dwb_ref[...] = jnp.zeros_like(dwb_ref)
            dcw_ref[...] = jnp.zeros_like(dcw_ref)

        dyb = dy_ref[...]
        xb = x_ref[...]
        dh = _dot_nt(dyb, wd_ref[...])
        av = a_ref[...]
        dbl = dh * _gelu(av)
        da = dh * bl_ref[...] * _gelu_grad(av)
        row = lax.broadcasted_iota(jnp.int32, (TM, TN), 0)
        c0, c1 = carry[0:1, :], carry[1:2, :]
        p1 = jnp.where(row == TM - 1, c0, pltpu.roll(da, TM - 1, 0))
        p2 = jnp.where(row == TM - 2, c0, jnp.where(row == TM - 1, c1, pltpu.roll(da, TM - 2, 0)))
        carry[...] = da[0:8, :]
        ap = ap_ref[...]
        dcw_ref[3:4, :] += _colsum(da)
        dcw_ref[0:1, :] += _colsum(ap * p2)
        dcw_ref[1:2, :] += _colsum(ap * p1)
        dcw_ref[2:3, :] += _colsum(ap * da)
        dap = (cw_ref[2:3, :] * da + cw_ref[1:2, :] * p1 + cw_ref[0:1, :] * p2).astype(BF16)
        dblb = dbl.astype(BF16)
        dwa_ref[...] += _dot_tn(xb, dap)
        dwb_ref[...] += _dot_tn(xb, dblb)
        dwd_ref[...] += _dot_tn(h_ref[...], dyb)
        dx = _dot_nt(dap, wa_ref[...]) + _dot_nt(dblb, wb_ref[...])
        rows = pl.ds(pl.multiple_of((NT - 1 - i) * TM, TM), TM)

        @pl.when(j == 0)
        def _():
            dx_acc[rows, :] = dx

        @pl.when(j > 0)
        def _():
            dx_acc[rows, :] += dx

        @pl.when((j == NJ - 1) & (i == NT - 1))
        def _():
            cp = pltpu.make_async_copy(dx_acc, dx_hbm, sem)
            cp.start()
            cp.wait()

    rev_rows = lambda j, i: (NT - 1 - i, 0)
    rev_tile = pl.BlockSpec((TM, TN), lambda j, i: (NT - 1 - i, j))
    wcol = pl.BlockSpec((D, TN), lambda j, i: (0, j))
    small = pl.BlockSpec((8, TN), lambda j, i: (0, j))
    return pl.pallas_call(
        body, name="ffn_bwd", grid=(NJ, NT),
        out_shape=(jax.ShapeDtypeStruct((D_FF, D), F32), jax.ShapeDtypeStruct((D, D_FF), F32),
                   jax.ShapeDtypeStruct((D, D_FF), F32), jax.ShapeDtypeStruct((8, D_FF), F32),
                   jax.ShapeDtypeStruct((S, D), F32)),
        in_specs=[pl.BlockSpec((TM, D), rev_rows), pl.BlockSpec((TM, D), rev_rows),
                  pl.BlockSpec((TN, D), lambda j, i: (j, 0)), wcol, wcol, rev_tile, rev_tile, rev_tile,
                  rev_tile, small],
        out_specs=(pl.BlockSpec((TN, D), lambda j, i: (j, 0)), wcol, wcol, small,
                   pl.BlockSpec(memory_space=pl.ANY)),
        scratch_shapes=[pltpu.VMEM((S, D), F32), pltpu.VMEM((8, TN), F32), pltpu.SemaphoreType.DMA],
        compiler_params=_params(("arbitrary", "arbitrary"), 56),
    )(dy2b, x1b, w_down, w_a, w_b, a_pre, a, b_lin, hff, conv_w8)


def _ln1_mix_bwd(dy2, dx1_ffn, xhat1, rstd1, g1, attn, gm, w_o, carried=None):
    def body(dy2_ref, dxf_ref, xh_ref, rs_ref, g_ref, a_ref, gm_ref, w_ref,
             dy1_ref, da_ref, dlt_ref, dgm_ref, gwo_ref, vec_ref, dwo_ref):
        @pl.when(pl.program_id(0) == 0)
        def _():
            dwo_ref[...] = jnp.zeros_like(dwo_ref)
            vec_ref[...] = jnp.zeros_like(vec_ref)

        xhat = xh_ref[...]
        dx1 = ALPHA * dy2_ref[...] + dxf_ref[...]
        vec_ref[0:1, :] += _colsum(dx1 * xhat)
        vec_ref[1:2, :] += _colsum(dx1)
        dy1 = _ln_bwd(dx1 * g_ref[...], xhat, rs_ref[...])
        dy1_ref[...] = dy1
        dy1b = dy1.astype(BF16)
        dmix = _dot_nt(dy1b, w_ref[...])
        attn_t = a_ref[...]
        d_attn = dmix[:, 0:D_HALF]
        da_ref[...] = d_attn
        dgm_ref[...] = dmix[:, D_HALF:D]
        lo = (lax.broadcasted_iota(jnp.int32, (TM, LANES), 1) < 64)
        for s in range(4):
            sl = slice(s * LANES, (s + 1) * LANES)
            prod = d_attn[:, sl] * attn_t[:, sl]
            s0 = jnp.sum(jnp.where(lo, prod, 0.0), axis=-1, keepdims=True)
            s1 = jnp.sum(jnp.where(lo, 0.0, prod), axis=-1, keepdims=True)
            dlt_ref[:, sl] = jnp.where(lo, s0, s1)
        dwo_ref[0:D_HALF, :] += _dot_tn(attn_t.astype(BF16), dy1b)
        dwo_ref[D_HALF:D, :] += _dot_tn(gm_ref[...], dy1b)

        @pl.when(pl.program_id(0) == NT - 1)
        def _():
            for j in range(N_DEV):
                gwo_ref[_owner_slot(j)] = dwo_ref[LANES * j:LANES * (j + 1), :].astype(BF16)

    half = jax.ShapeDtypeStruct((S, D_HALF), F32)
    return _host_call(
        body, carried, name="ln1_mix_bwd", grid=(NT,),
        out_shape=(jax.ShapeDtypeStruct((S, D), F32), half, half, half,
                   jax.ShapeDtypeStruct((N_DEV, D // N_DEV, D), BF16), jax.ShapeDtypeStruct((8, D), F32)),
        in_specs=[_row_spec(D), _row_spec(D), _row_spec(D), pl.BlockSpec((TM, 1), _rows), _full_spec((1, D)),
                  _row_spec(D_HALF), _row_spec(D_HALF), _full_spec((D, D))],
        out_specs=(_row_spec(D), _row_spec(D_HALF), _row_spec(D_HALF), _row_spec(D_HALF),
                   _full_spec((N_DEV, D // N_DEV, D)), _full_spec((8, D))),
        scratch_shapes=[pltpu.VMEM((D, D), F32)],
        params=_params(("arbitrary",)), args=(dy2, dx1_ffn, xhat1, rstd1, g1, attn, gm, w_o))


def _gmlp_bwd(d_gm, u_pre, z_pre, ln_z_g, ln_z_b, w_s, bs_exp):
    def body(dg_ref, u_ref, z_ref, g_ref, b_ref, ws_ref, bs_ref, du_ref, dz_ref, dws_ref, dbs_ref, vec_ref):
        @pl.when(pl.program_id(0) == 0)
        def _():
            dws_ref[...] = jnp.zeros_like(dws_ref)
            dbs_ref[...] = jnp.zeros_like(dbs_ref)
            vec_ref[...] = jnp.zeros_like(vec_ref)

        u_pre_t, z_pre_t, dgm = u_ref[...], z_ref[...], dg_ref[...]
        zhat, rstd = _ln_stats(_gelu(z_pre_t))
        zn = zhat * g_ref[...] + b_ref[...]
        wm = _masked_ws(ws_ref)
        mixed = _spatial_mix(zn, wm, bs_ref[...])
        du_ref[...] = (dgm * mixed * _gelu_grad(u_pre_t)).astype(BF16)
        dmixed = dgm * _gelu(u_pre_t)
        lo = _lane_lo()
        tril = _tril()
        group_of_lane = lax.broadcasted_iota(jnp.int32, (8, D_HALF), 1) // 64
        pick = (group_of_lane == lax.broadcasted_iota(jnp.int32, (8, D_HALF), 0)).astype(F32)
        dzn_rows = []
        for ch in range(TM // CHUNK):
            rows = slice(ch * CHUNK, (ch + 1) * CHUNK)
            dbs_ref[...] += lax.dot_general(pick, dmixed[rows, :], (((1,), (1,)), ((), ())),
                                            precision=lax.Precision.HIGHEST, preferred_element_type=F32)
            slabs = []
            for pr in range(4):
                sl = slice(pr * LANES, (pr + 1) * LANES)
                dm = dmixed[rows, sl]
                zp = zn[rows, sl].astype(BF16)
                dm_lo = jnp.where(lo, dm, 0.0).astype(BF16)
                dm_hi = jnp.where(lo, 0.0, dm).astype(BF16)
                dws_ref[2 * pr] += jnp.where(tril, _dot_nt(dm_lo, zp), 0.0)
                dws_ref[2 * pr + 1] += jnp.where(tril, _dot_nt(dm_hi, zp), 0.0)
                dmb = dm.astype(BF16)
                slabs.append(jnp.where(lo, _dot_tn(wm[2 * pr], dmb), _dot_tn(wm[2 * pr + 1], dmb)))
            dzn_rows.append(jnp.concatenate(slabs, axis=1))
        dzn = jnp.concatenate(dzn_rows, axis=0)
        vec_ref[0:1, :] += _colsum(dzn * zhat)
        vec_ref[1:2, :] += _colsum(dzn)
        dz = _ln_bwd(dzn * g_ref[...], zhat, rstd)
        dz_ref[...] = (dz * _gelu_grad(z_pre_t)).astype(BF16)

    halfb = jax.ShapeDtypeStruct((S, D_HALF), BF16)
    vec = _full_spec((1, D_HALF))
    return pl.pallas_call(
        body, name="gmlp_bwd", grid=(NT,),
        out_shape=(halfb, halfb, jax.ShapeDtypeStruct((8, CHUNK, CHUNK), F32),
                   jax.ShapeDtypeStruct((8, CHUNK), F32), jax.ShapeDtypeStruct((8, D_HALF), F32)),
        in_specs=[_row_spec(D_HALF), _row_spec(D_HALF), _row_spec(D_HALF), vec, vec,
                  _full_spec((8, CHUNK, CHUNK)), _full_spec((CHUNK, D_HALF))],
        out_specs=(_row_spec(D_HALF), _row_spec(D_HALF), _full_spec((8, CHUNK, CHUNK)),
                   _full_spec((8, CHUNK)), _full_spec((8, D_HALF))),
        compiler_params=_params(("arbitrary",)),
    )(d_gm, u_pre, z_pre, ln_z_g, ln_z_b, w_s, bs_exp)


def _attention_bwd(q, k, v, lse, d_attn, delta, tabs, carried=None):
    def body(q_ref, k_ref, v_ref, l_ref, do_ref, dl_ref, c_ref, sa_ref, sb_ref, dq_ref, dk_ref, dv_ref,
             qb, kb, vb, gb, bias, lsp, dlp, dqp, dk_own, dk_prev, dv_own, dv_prev, dqa, dka, dva):
        lo = _lane_lo()
        zero_pad = jnp.zeros((CHUNK, LANES), BF16)
        for buf in (qb, kb, vb, gb):
            buf[0:CHUNK, :] = zero_pad
        dk_prev[S:S + CHUNK, :] = jnp.zeros((CHUNK, LANES), F32)
        dv_prev[S:S + CHUNK, :] = jnp.zeros((CHUNK, LANES), F32)
        _store_band_bias(bias)
        for d, nb in DILATIONS:
            length = S // d
            for r in range(d):
                dst = slice(CHUNK + r * length, CHUNK + (r + 1) * length)
                src = slice(r * length, (r + 1) * length)
                qb[dst, :] = (_permuted_rows(q_ref, d, r) * 0.125).astype(BF16)
                kb[dst, :] = _permuted_rows(k_ref, d, r).astype(BF16)
                vb[dst, :] = _permuted_rows(v_ref, d, r).astype(BF16)
                gb[dst, :] = _permuted_rows(do_ref, d, r).astype(BF16)
                lsp[src, :] = _permuted_rows(l_ref, d, r)
                dlp[src, :] = _permuted_rows(dl_ref, d, r)

            def block(b, carry, nb=nb):
                base = pl.multiple_of(b * CHUNK, CHUNK)
                own = pl.multiple_of(base + CHUNK, CHUNK)
                add = bias[jnp.where(b % nb == 0, 1, 0)]
                qblk = qb[pl.ds(own, CHUNK), :]
                gblk = gb[pl.ds(own, CHUNK), :]
                kblk = kb[pl.ds(base, 2 * CHUNK), :]
                vblk = vb[pl.ds(base, 2 * CHUNK), :]
                lse_t = lsp[pl.ds(base, CHUNK), :]
                dlt_t = dlp[pl.ds(base, CHUNK), :]
                dq, dk, dv = [], [], []
                for head in range(2):
                    qh = jnp.where(lo, qblk, 0) if head == 0 else jnp.where(lo, 0, qblk)
                    gh = jnp.where(lo, gblk, 0) if head == 0 else jnp.where(lo, 0, gblk)
                    col = 64 * head
                    p = jnp.exp(_dot_nt(qh, kblk) + add - lse_t[:, col:col + 1])
                    ds = (p * (_dot_nt(gh, vblk) - dlt_t[:, col:col + 1])).astype(BF16)
                    dv.append(_dot_tn(p.astype(BF16), gh))
                    dk.append(_dot_tn(ds, qh))
                    dq.append(_dot(ds, kblk))
                dqp[pl.ds(base, CHUNK), :] = jnp.where(lo, dq[0], dq[1]) * 0.125
                dk_blk = dk[0] + dk[1]
                dv_blk = dv[0] + dv[1]
                dk_prev[pl.ds(base, CHUNK), :] = dk_blk[0:CHUNK, :]
                dk_own[pl.ds(own, CHUNK), :] = dk_blk[CHUNK:2 * CHUNK, :]
                dv_prev[pl.ds(base, CHUNK), :] = dv_blk[0:CHUNK, :]
                dv_own[pl.ds(own, CHUNK), :] = dv_blk[CHUNK:2 * CHUNK, :]
                return carry

            lax.fori_loop(0, S // CHUNK, block, 0, unroll=ATTN_UNROLL)
            for r in range(d):
                src = slice(r * length, (r + 1) * length)
                pad = slice(CHUNK + r * length, CHUNK + (r + 1) * length)
                if d == 1:
                    dqa[...] = dqp[...]
                    dka[...] = dk_own[pad, :] + dk_prev[pad, :]
                    dva[...] = dv_own[pad, :] + dv_prev[pad, :]
                else:
                    dst = pl.ds(r, length, stride=d)
                    dqa[dst, :] = dqa[dst, :] + dqp[src, :]
                    dka[dst, :] = dka[dst, :] + (dk_own[pad, :] + dk_prev[pad, :])
                    dva[dst, :] = dva[dst, :] + (dv_own[pad, :] + dv_prev[pad, :])
        for t in range(NT):
            rows = slice(t * TM, (t + 1) * TM)
            c, sa, sb = c_ref[rows, :], sa_ref[rows, :], sb_ref[rows, :]
            dq_ref[rows, :] = _rope_t(dqa[rows, :], c, sa, sb).astype(BF16)
            dk_ref[rows, :] = _rope_t(dka[rows, :], c, sa, sb).astype(BF16)
            dv_ref[rows, :] = dva[rows, :].astype(BF16)

    slab = pl.BlockSpec((S, LANES), lambda h: (0, h), pipeline_mode=pl.Buffered(1))
    tab = pl.BlockSpec((S, LANES), lambda h: (0, 0), pipeline_mode=pl.Buffered(1))
    out_slab = pl.BlockSpec((S, LANES), lambda h: (0, h))
    out = jax.ShapeDtypeStruct((S, D_HALF), BF16)
    padded_b = pltpu.VMEM((CHUNK + S, LANES), BF16)
    padded_f = pltpu.VMEM((CHUNK + S, LANES), F32)
    whole = pltpu.VMEM((S, LANES), F32)
    return _host_call(
        body, carried, name="attention_bwd", grid=(4,), out_shape=(out, out, out),
        in_specs=[slab] * 6 + [tab] * 3, out_specs=(out_slab,) * 3,
        scratch_shapes=[padded_b] * 4 + [pltpu.VMEM((2, CHUNK, 2 * CHUNK), F32)] + [whole] * 3
        + [padded_f] * 4 + [whole] * 3,
        params=_params(("arbitrary",), 60), args=(q, k, v, lse, d_attn, delta, *tabs))


def _proj_in_bwd_w(xb, parts):
    def body(x_ref, p0, p1, p2, p3, p4, gw_ref, dw_ref):
        @pl.when(pl.program_id(0) == 0)
        def _():
            dw_ref[...] = jnp.zeros_like(dw_ref)

        xt = x_ref[...]
        for n, part in enumerate((p0, p1, p2, p3, p4)):
            dw_ref[:, n * D_HALF:(n + 1) * D_HALF] += _dot_tn(xt, part[...])

        @pl.when(pl.program_id(0) == NT - 1)
        def _():
            width = D_IN // N_DEV
            for j in range(N_DEV):
                gw_ref[_owner_slot(j)] = dw_ref[:, width * j:width * (j + 1)].astype(BF16)

    return pl.pallas_call(
        body, name="proj_in_bwd_w", grid=(NT,), out_shape=jax.ShapeDtypeStruct((N_DEV, D, D_IN // N_DEV), BF16),
        in_specs=[_row_spec(D)] + [_row_spec(D_HALF)] * 5, out_specs=_full_spec((N_DEV, D, D_IN // N_DEV)),
        scratch_shapes=[pltpu.VMEM((D, D_IN), F32)],
        compiler_params=_params(("arbitrary",)),
    )(xb, *parts)


def _proj_in_bwd_x(dy1, parts, w_in, carried=None):
    def body(dy_ref, p0, p1, p2, p3, p4, w_ref, gx_ref):
        acc = ALPHA * dy_ref[...]
        for n, part in enumerate((p0, p1, p2, p3, p4)):
            acc += _dot_nt(part[...], w_ref[:, n * D_HALF:(n + 1) * D_HALF])
        gx_ref[...] = acc

    return _host_call(
        body, carried, name="proj_in_bwd_x", grid=(NT,), out_shape=(jax.ShapeDtypeStruct((S, D), F32),),
        in_specs=[_row_spec(D)] + [_row_spec(D_HALF)] * 5 + [_full_spec((D, D_IN))], out_specs=(_row_spec(D),),
        scratch_shapes=[], params=_params(("arbitrary",)), args=(dy1, *parts, w_in))


def _to_natural(blocks, name):
    n, rows, w = blocks.shape
    tile = min(rows, 256)

    def body(i_ref, o_ref):
        o_ref[...] = jnp.concatenate([i_ref[j] for j in range(n)], axis=1)

    return pl.pallas_call(
        body, name=name, grid=(rows // tile,), out_shape=jax.ShapeDtypeStruct((rows, n * w), blocks.dtype),
        in_specs=[pl.BlockSpec((n, tile, w), lambda i: (0, i, 0))],
        out_specs=pl.BlockSpec((tile, n * w), lambda i: (i, 0)), compiler_params=_params(("parallel",)),
    )(blocks)


def _column_blocks(full, name):
    rows, cols = full.shape
    w = cols // N_DEV
    tile = 256

    def body(i_ref, o_ref):
        for j in range(N_DEV):
            o_ref[_owner_slot(j)] = i_ref[:, j * w:(j + 1) * w].astype(BF16)

    return pl.pallas_call(
        body, name=name, grid=(rows // tile,), out_shape=jax.ShapeDtypeStruct((N_DEV, rows, w), BF16),
        in_specs=[pl.BlockSpec((tile, cols), lambda i: (i, 0))],
        out_specs=pl.BlockSpec((N_DEV, tile, w), lambda i: (0, i, 0)), compiler_params=_params(("parallel",)),
    )(full)


def _row_blocks(full):
    rows, cols = full.shape
    r = rows // N_DEV

    def body(i_ref, o_ref):
        o_ref[0] = i_ref[...].astype(BF16)

    return pl.pallas_call(
        body, name="ff_down_grad_blocks", grid=(N_DEV,), out_shape=jax.ShapeDtypeStruct((N_DEV, r, cols), BF16),
        in_specs=[pl.BlockSpec((r, cols), lambda s: ((s % 4) * 2 + s // 4, 0))],
        out_specs=pl.BlockSpec((1, r, cols), lambda s: (s, 0, 0)), compiler_params=_params(("parallel",)),
    )(full)


def _local_step(x, p, pos_col, target, w_in, sm, ex):
    bs_exp = jnp.repeat(sm["b_s"].T, 64, axis=1)
    tabs = _rope_tables(pos_col)
    q, k, v, u_pre, z_pre, gm, xb = _proj_in_fwd(x, w_in, tabs, sm["ln_z_g"], sm["ln_z_b"], sm["w_s"], bs_exp)
    (attn, lse), got = _attention_fwd(q, k, v, ex.gather_first())
    wa = ex.weights_first(got)
    xhat1, rstd1, x1b = _mix_ln1_fwd(attn, gm, wa["w_o"], x, sm["ln1_g"], sm["ln1_b"])
    (a_pre, a, b_lin, hff), got = _ffn_up_fwd(x1b, wa["w_ff_a"], wa["w_ff_b"], wa["conv_w8"], sm["conv_b"],
                                              ex.gather_second())
    wc = ex.weights_second(got)
    xhat2, rstd2 = _ffn_down_ln2_fwd(hff, wc["w_ff_down"], xhat1, sm["ln1_g"], sm["ln1_b"])
    loss, dy2, dy2b, dw_g, dw_p, vec_tail = _tail_fwd_bwd(
        xhat2, rstd2, p, target, wc["w_ple_gate"], wc["w_ple_in"], sm["ln2_g"], sm["ln2_b"],
        sm["b_ple_gate"], sm["ln3_g"], sm["ln3_b"])
    dw_down, dw_a, dw_b, dconv, dx1_ffn = _ffn_bwd(dy2b, x1b, wc["w_ff_down"], wa["w_ff_a"], wa["w_ff_b"],
                                                    a_pre, a, b_lin, hff, wa["conv_w8"])
    early = {"w_ff_a": _column_blocks(dw_a, "ff_a_grad_blocks"), "w_ff_b": _column_blocks(dw_b, "ff_b_grad_blocks"),
             "w_ff_down": _row_blocks(dw_down), "w_ple_gate": dw_g, "w_ple_in": dw_p}
    (dy1, d_attn, delta, d_gm, dw_o, vec_ln1), got = _ln1_mix_bwd(
        dy2, dx1_ffn, xhat1, rstd1, sm["ln1_g"], attn, gm, wa["w_o"], ex.to_sibling(early))
    chip_sums = ex.reduce_on_chip(got)
    du, dz, dws, dbs, vec_z = _gmlp_bwd(d_gm, u_pre, z_pre, sm["ln_z_g"], sm["ln_z_b"], sm["w_s"], bs_exp)
    (dq, dk, dv), got_early = _attention_bwd(q, k, v, lse, d_attn, delta, tabs, ex.between_chips(chip_sums))
    parts = (dq, dk, dv, du, dz)
    late = {"w_in": _proj_in_bwd_w(xb, parts), "w_o": dw_o}
    small = {"tail": vec_tail, "ln1": vec_ln1, "ln_z": vec_z, "conv": dconv, "w_s": dws, "b_s": dbs, "loss": loss}
    (grad_x,), got_late = _proj_in_bwd_x(dy1, parts, w_in, ex.to_owners(late, small))
    return grad_x, ex.collect(got_early, got_late)


def _mesh_pos():
    return lax.axis_index("x"), lax.axis_index("y"), lax.axis_index("c")


def _cast_shards(shards):
    n = len(shards)

    def body(*refs):
        for a in range(n):
            refs[n + a][...] = refs[a][...].astype(BF16)

    return pl.pallas_call(
        body, name="cast_shards", out_shape=tuple(jax.ShapeDtypeStruct(s.shape, BF16) for s in shards),
        compiler_params=_params(None),
    )(*shards)


class _GatherComm:
    def __init__(self, shards):
        n = len(shards)
        self.inputs = list(shards)
        self.out_shapes = [jax.ShapeDtypeStruct((N_DEV,) + s.shape, s.dtype) for s in shards]
        self.scratch = [pltpu.SemaphoreType.DMA((7 * n,)), pltpu.SemaphoreType.DMA((7 * n,)),
                        pltpu.SemaphoreType.DMA((n,))]

    def phases(self, x_refs, out_refs, sems):
        send_sems, recv_sems, local_sems = sems
        n_arr = len(x_refs)

        def where():
            x, y, c = _mesh_pos()
            return (x, y, c), (x, y, 1 - c), [(1 - x, y), (x, 1 - y), (1 - x, 1 - y)]

        def copy(a, n, block, to, from_shard=False):
            dst = out_refs[a].at[4 * block[0] + 2 * block[1] + block[2]]
            return pltpu.make_async_remote_copy(
                src_ref=x_refs[a] if from_shard else dst, dst_ref=dst, send_sem=send_sems.at[7 * a + n],
                recv_sem=recv_sems.at[7 * a + n], device_id=to, device_id_type=MESH)

        def local(a):
            x, y, c = _mesh_pos()
            return pltpu.make_async_copy(x_refs[a], out_refs[a].at[4 * x + 2 * y + c], local_sems.at[a])

        def start():
            me, sibling, chips = where()
            for a in range(n_arr):
                local(a).start()
                copy(a, 0, me, sibling, from_shard=True).start()
                for n, chip in enumerate(chips):
                    copy(a, 1 + n, me, (*chip, me[2]), from_shard=True).start()

        def forward():
            me, sibling, chips = where()
            for n, chip in enumerate(chips):
                for a in range(n_arr):
                    copy(a, 1 + n, (*chip, me[2]), me).wait_recv()
                    copy(a, 4 + n, (*chip, me[2]), sibling).start()

        def finish():
            me, sibling, chips = where()
            for a in range(n_arr):
                copy(a, 0, sibling, me).wait_recv()
                copy(a, 0, me, sibling, from_shard=True).wait_send()
                for n, chip in enumerate(chips):
                    copy(a, 4 + n, (*chip, 1 - me[2]), me).wait_recv()
                    copy(a, 1 + n, me, (*chip, me[2]), from_shard=True).wait_send()
                    copy(a, 4 + n, (*chip, me[2]), sibling).wait_send()
                local(a).wait()

        return {"start": start, "forward": forward, "finish": finish}


class _SiblingComm:
    def __init__(self, big):
        n = len(big)
        self.inputs = list(big)
        self.out_shapes = [jax.ShapeDtypeStruct(b.shape[1:], b.dtype) for b in big]
        self.scratch = [pltpu.SemaphoreType.DMA((n,)), pltpu.SemaphoreType.DMA((n,))]

    def phases(self, src, dst, sems):
        send_sems, recv_sems = sems

        def copies():
            x, y, c = _mesh_pos()
            return [pltpu.make_async_remote_copy(
                src_ref=src[a].at[1 - c], dst_ref=dst[a], send_sem=send_sems.at[a], recv_sem=recv_sems.at[a],
                device_id=(x, y, 1 - c), device_id_type=MESH) for a in range(len(src))]

        def start():
            for cp in copies():
                cp.start()

        def finish():
            for cp in copies():
                cp.wait()

        return {"start": start, "finish": finish}


class _ChipComm:
    def __init__(self, sums):
        n = len(sums)
        self.inputs = list(sums)
        self.out_shapes = [jax.ShapeDtypeStruct(s.shape, s.dtype) for s in sums]
        self.scratch = [pltpu.SemaphoreType.DMA((3 * n,)), pltpu.SemaphoreType.DMA((3 * n,)),
                        pltpu.SemaphoreType.DMA((n,))]

    def phases(self, src, dst, sems):
        send_sems, recv_sems, local_sems = sems

        def copies():
            x, y, c = _mesh_pos()
            my_chip = 2 * x + y
            out = [pltpu.make_async_copy(src[a].at[my_chip], dst[a].at[my_chip], local_sems.at[a])
                   for a in range(len(src))]
            for n, (px, py) in enumerate([(1 - x, y), (x, 1 - y), (1 - x, 1 - y)]):
                for a in range(len(src)):
                    out.append(pltpu.make_async_remote_copy(
                        src_ref=src[a].at[2 * px + py], dst_ref=dst[a].at[my_chip],
                        send_sem=send_sems.at[3 * a + n], recv_sem=recv_sems.at[3 * a + n],
                        device_id=(px, py, c), device_id_type=MESH))
            return out

        def start():
            for cp in copies():
                cp.start()

        def finish():
            for cp in copies():
                cp.wait()

        return {"start": start, "finish": finish}


class _ScatterComm:
    def __init__(self, blocks, small):
        self.n_big, self.n_small = len(blocks), len(small)
        n = self.n_big + self.n_small
        self.inputs = list(blocks) + list(small)
        self.out_shapes = ([jax.ShapeDtypeStruct(b.shape, b.dtype) for b in blocks]
                           + [jax.ShapeDtypeStruct((N_DEV,) + s.shape, s.dtype) for s in small])
        self.scratch = [pltpu.SemaphoreType.DMA((7 * n,)), pltpu.SemaphoreType.DMA((7 * n,)),
                        pltpu.SemaphoreType.DMA((n,))]

    def phases(self, src, dst, sems):
        send_sems, recv_sems, local_sems = sems
        n_big, n_all = self.n_big, self.n_big + self.n_small

        def source(a, core, chip):
            return src[a].at[core * 4 + chip] if a < n_big else src[a]

        def copies():
            x, y, c = _mesh_pos()
            me = 4 * x + 2 * y + c
            out = [pltpu.make_async_copy(source(a, c, 2 * x + y), dst[a].at[me], local_sems.at[a])
                   for a in range(n_all)]
            for flip in range(1, N_DEV):
                px = 1 - x if flip & 4 else x
                py = 1 - y if flip & 2 else y
                pc = 1 - c if flip & 1 else c
                for a in range(n_all):
                    n = 7 * a + flip - 1
                    out.append(pltpu.make_async_remote_copy(
                        src_ref=source(a, pc, 2 * px + py), dst_ref=dst[a].at[me], send_sem=send_sems.at[n],
                        recv_sem=recv_sems.at[n], device_id=(px, py, pc), device_id_type=MESH))
            return out

        def start():
            for cp in copies():
                cp.start()

        def finish():
            for cp in copies():
                cp.wait()

        return {"start": start, "finish": finish}


def _host_call(body, carried, *, name, grid, out_shape, in_specs, out_specs, scratch_shapes, params, args):
    if carried is None:
        return pl.pallas_call(body, name=name, grid=grid, out_shape=tuple(out_shape), in_specs=list(in_specs),
                              out_specs=tuple(out_specs), scratch_shapes=list(scratch_shapes),
                              compiler_params=params)(*args), ()
    comm, when = carried
    n_in, n_out, n_scratch = len(in_specs), len(out_shape), len(scratch_shapes)
    k_in, k_out = len(comm.inputs), len(comm.out_shapes)

    def wrapped(*refs):
        bounds = np.cumsum([0, n_in, k_in, n_out, k_out, n_scratch])
        ins, c_in, outs, c_out, scr = (refs[bounds[i]:bounds[i + 1]] for i in range(5))
        phases = comm.phases(c_in, c_out, refs[bounds[5]:])
        for phase, cond in when("before"):
            pl.when(cond)(phases[phase])
        body(*ins, *outs, *scr)
        for phase, cond in when("after"):
            pl.when(cond)(phases[phase])

    anywhere = pl.BlockSpec(memory_space=pl.ANY)
    results = pl.pallas_call(
        wrapped, name=name, grid=grid, out_shape=tuple(out_shape) + tuple(comm.out_shapes),
        in_specs=list(in_specs) + [anywhere] * k_in, out_specs=tuple(out_specs) + (anywhere,) * k_out,
        scratch_shapes=list(scratch_shapes) + comm.scratch, compiler_params=params,
    )(*args, *comm.inputs)
    return results[:n_out], results[n_out:]


class _Exchanges:
    FIRST = ("w_o", "w_ff_a", "w_ff_b")
    SECOND = ("w_ff_down", "w_ple_gate", "w_ple_in")
    EARLY = ("w_ff_a", "w_ff_b", "w_ff_down", "w_ple_gate", "w_ple_in")
    LATE = ("w_in", "w_o")

    def __init__(self, shards, conv_rows):
        self.shards, self.conv_rows = shards, conv_rows
        self.axis = {name: axis for name, _, axis in BIG}

    def _natural(self, name, blocks):
        n, r, c = blocks.shape
        return blocks.reshape(n * r, c) if self.axis[name] == 0 else _to_natural(blocks, name + "_natural")

    def gather_first(self):
        comm = _GatherComm([self.shards[n] for n in self.FIRST] + [self.conv_rows])

        def when(position):
            step = pl.program_id(0)
            if position == "before":
                return [("start", step == 0), ("forward", step == 3)]
            return [("finish", step == 3)]
        return comm, when

    def weights_first(self, got):
        out = {name: self._natural(name, blocks) for name, blocks in zip(self.FIRST, got)}
        out["conv_w8"] = _to_natural(got[-1], "conv_w_natural")
        return out

    def gather_second(self):
        comm = _GatherComm([self.shards[n] for n in self.SECOND])

        def when(position):
            j, i = pl.program_id(0), pl.program_id(1)
            if position == "before":
                return [("start", (j == 0) & (i == 0)), ("forward", (j == NJ - 3) & (i == 0))]
            return [("finish", (j == NJ - 1) & (i == NT - 1))]
        return comm, when

    def weights_second(self, got):
        return {name: self._natural(name, blocks) for name, blocks in zip(self.SECOND, got)}

    def to_sibling(self, early):
        self.by_core = [early[n].reshape((2, 4) + early[n].shape[1:]) for n in self.EARLY]
        return _SiblingComm(self.by_core), _first_and_last(NT)

    def reduce_on_chip(self, from_sibling):
        core = lax.axis_index("c").astype(jnp.int32).reshape(1)
        return _chip_reduce(self.by_core, from_sibling, core)

    def between_chips(self, chip_sums):
        return _ChipComm(chip_sums), _first_and_last(4)

    def to_owners(self, late, small):
        self.small_keys = tuple(small)
        return _ScatterComm([late[n] for n in self.LATE], [small[k] for k in self.small_keys]), _first_and_last(NT)

    def collect(self, got_early, got_late):
        parts = dict(zip(self.EARLY, got_early))
        parts.update(zip(self.LATE, got_late[:len(self.LATE)]))
        return parts, dict(zip(self.small_keys, got_late[len(self.LATE):]))


def _first_and_last(n_steps):
    def when(position):
        step = pl.program_id(0)
        return [("start", step == 0)] if position == "before" else [("finish", step == n_steps - 1)]
    return when


def _all_gather(shards, name):
    comm = _GatherComm(shards)

    def body(*refs):
        n = len(shards)
        phases = comm.phases(refs[:n], refs[n:2 * n], refs[2 * n:])
        phases["start"]()
        phases["forward"]()
        phases["finish"]()

    anywhere = pl.BlockSpec(memory_space=pl.ANY)
    return pl.pallas_call(
        body, name=name, out_shape=tuple(comm.out_shapes), in_specs=[anywhere] * len(shards),
        out_specs=(anywhere,) * len(shards), scratch_shapes=comm.scratch,
    )(*shards)


def _chip_reduce(big, from_sibling, core):
    n = len(big)

    def body(core_ref, *refs):
        for a in range(n):
            mine, theirs, out = refs[a], refs[n + a], refs[2 * n + a]
            out[0] = (mine[0, 0].astype(F32) + theirs[0].astype(F32)).astype(BF16)

    def block(shape):
        return pl.BlockSpec((1,) + shape, lambda ch, core_ref: (ch, 0, 0))

    grid_spec = pltpu.PrefetchScalarGridSpec(
        num_scalar_prefetch=1, grid=(4,),
        in_specs=[pl.BlockSpec((1, 1) + b.shape[2:], lambda ch, core_ref: (core_ref[0], ch, 0, 0)) for b in big]
        + [block(b.shape[2:]) for b in big],
        out_specs=[block(b.shape[2:]) for b in big])
    return pl.pallas_call(
        body, name="chip_reduce", grid_spec=grid_spec,
        out_shape=tuple(jax.ShapeDtypeStruct(b.shape[1:], BF16) for b in big),
        compiler_params=_params(("parallel",)),
    )(core, *big, *from_sibling)


def _adamw(g, w, m, v):
    nm = ADAM_B1 * m + (1.0 - ADAM_B1) * g
    nv = ADAM_B2 * v + (1.0 - ADAM_B2) * (g * g)
    m_hat = nm / (1.0 - ADAM_B1 ** ADAM_STEP)
    v_hat = nv / (1.0 - ADAM_B2 ** ADAM_STEP)
    return -ADAM_LR * (m_hat / (jnp.sqrt(v_hat) + ADAM_EPS) + ADAM_WD * w), nm, nv


def _adamw_sharded(parts, w, m, v, name):
    def body(p_ref, w_ref, m_ref, v_ref, g_ref, d_ref, nm_ref, nv_ref):
        g = p_ref[0].astype(F32)
        for s in range(1, parts.shape[0]):
            g = g + p_ref[s].astype(F32)
        delta, nm, nv = _adamw(g, w_ref[0], m_ref[0], v_ref[0])
        g_ref[0] = g
        d_ref[0] = delta
        nm_ref[0] = nm
        nv_ref[0] = nv

    return pl.pallas_call(
        body, name=name, out_shape=(jax.ShapeDtypeStruct(w.shape, F32),) * 4, compiler_params=_params(None),
    )(parts, w, m, v)


REPLICATED = (("ln_z_g", "ln_z", 0), ("ln_z_b", "ln_z", 1), ("w_s", "w_s", None), ("b_s", "b_s", None),
              ("ln1_g", "ln1", 0), ("ln1_b", "ln1", 1), ("conv_w", "conv_mine", None), ("conv_b", "conv", 3),
              ("ln2_g", "tail", 3), ("ln2_b", "tail", 4), ("b_ple_gate", "tail", 0), ("ln3_g", "tail", 1),
              ("ln3_b", "tail", 2))
GATHERED = ("tail", "ln1", "ln_z", "conv", "w_s", "b_s", "loss", "conv_mine")


def _adamw_replicated(gathered, w, m, v):
    n_par = len(REPLICATED)

    def body(*refs):
        srcs = dict(zip(GATHERED, refs[:len(GATHERED)]))
        rest = refs[len(GATHERED):]
        w_refs, m_refs, v_refs = rest[:n_par], rest[n_par:2 * n_par], rest[2 * n_par:3 * n_par]
        outs = rest[3 * n_par:]
        loss_ref = outs[4 * n_par]
        sums = {}
        for key, ref in srcs.items():
            total = ref[0]
            for dev in range(1, N_DEV):
                total = total + ref[dev]
            sums[key] = total
        loss_ref[...] = sums["loss"]
        for n, (name, key, row) in enumerate(REPLICATED):
            if name == "conv_w":
                g = sums[key][0:3, :]
            elif row is None:
                g = sums[key]
            else:
                g = sums[key][row:row + 1, :]
            lead = len(w_refs[n].shape) - g.ndim
            idx = (0,) * lead + (Ellipsis,)
            delta, nm, nv = _adamw(g, w_refs[n][idx], m_refs[n][idx], v_refs[n][idx])
            for kind, val in enumerate((g, delta, nm, nv)):
                outs[kind * n_par + n][idx] = val

    names = [name for name, _, _ in REPLICATED]
    shapes = [jax.ShapeDtypeStruct(w[name].shape, F32) for name in names]
    return pl.pallas_call(
        body, name="adamw_replicated", out_shape=tuple(shapes * 4) + (jax.ShapeDtypeStruct((8, LANES), F32),),
        compiler_params=_params(None),
    )(*[gathered[k] for k in GATHERED], *[w[n] for n in names], *[m[n] for n in names], *[v[n] for n in names])


def kernel(x, p, positions, w_in, ln_z_g, ln_z_b, w_s, b_s, w_o, ln1_g, ln1_b, w_ff_a, w_ff_b, conv_w, conv_b, w_ff_down, ln2_g, ln2_b, w_ple_gate, b_ple_gate, w_ple_in, ln3_g, ln3_b, loss_target, m_w_in, m_ln_z_g, m_ln_z_b, m_w_s, m_b_s, m_w_o, m_ln1_g, m_ln1_b, m_w_ff_a, m_w_ff_b, m_conv_w, m_conv_b, m_w_ff_down, m_ln2_g, m_ln2_b, m_w_ple_gate, m_b_ple_gate, m_w_ple_in, m_ln3_g, m_ln3_b, v_w_in, v_ln_z_g, v_ln_z_b, v_w_s, v_b_s, v_w_o, v_ln1_g, v_ln1_b, v_w_ff_a, v_w_ff_b, v_conv_w, v_conv_b, v_w_ff_down, v_ln2_g, v_ln2_b, v_w_ple_gate, v_b_ple_gate, v_w_ple_in, v_ln3_g, v_ln3_b):
    w = dict(w_in=w_in, ln_z_g=ln_z_g, ln_z_b=ln_z_b, w_s=w_s, b_s=b_s, w_o=w_o, ln1_g=ln1_g, ln1_b=ln1_b,
             w_ff_a=w_ff_a, w_ff_b=w_ff_b, conv_w=conv_w, conv_b=conv_b, w_ff_down=w_ff_down, ln2_g=ln2_g,
             ln2_b=ln2_b, w_ple_gate=w_ple_gate, b_ple_gate=b_ple_gate, w_ple_in=w_ple_in, ln3_g=ln3_g,
             ln3_b=ln3_b)
    m = dict(w_in=m_w_in, ln_z_g=m_ln_z_g, ln_z_b=m_ln_z_b, w_s=m_w_s, b_s=m_b_s, w_o=m_w_o, ln1_g=m_ln1_g,
             ln1_b=m_ln1_b, w_ff_a=m_w_ff_a, w_ff_b=m_w_ff_b, conv_w=m_conv_w, conv_b=m_conv_b,
             w_ff_down=m_w_ff_down, ln2_g=m_ln2_g, ln2_b=m_ln2_b, w_ple_gate=m_w_ple_gate,
             b_ple_gate=m_b_ple_gate, w_ple_in=m_w_ple_in, ln3_g=m_ln3_g, ln3_b=m_ln3_b)
    v = dict(w_in=v_w_in, ln_z_g=v_ln_z_g, ln_z_b=v_ln_z_b, w_s=v_w_s, b_s=v_b_s, w_o=v_w_o, ln1_g=v_ln1_g,
             ln1_b=v_ln1_b, w_ff_a=v_w_ff_a, w_ff_b=v_w_ff_b, conv_w=v_conv_w, conv_b=v_conv_b,
             w_ff_down=v_w_ff_down, ln2_g=v_ln2_g, ln2_b=v_ln2_b, w_ple_gate=v_w_ple_gate,
             b_ple_gate=v_b_ple_gate, w_ple_in=v_w_ple_in, ln3_g=v_ln3_g, ln3_b=v_ln3_b)
    big_names = [name for name, _, _ in BIG]
    small_names = ("ln_z_g", "ln_z_b", "w_s", "b_s", "ln1_g", "ln1_b", "conv_b", "ln2_g", "ln2_b", "b_ple_gate",
                   "ln3_g", "ln3_b")

    shards = dict(zip(big_names, _cast_shards([w[n][0] for n in big_names])))
    conv_rows = jnp.pad(w["conv_w"][0], ((0, 5), (0, 0)))
    w_in_full = _to_natural(_all_gather([shards["w_in"]], "all_gather_w_in")[0], "w_in_natural")
    sm = {n: w[n][0] if w[n].ndim > 2 else w[n] for n in small_names}
    pos_col = positions.reshape(S, 1).astype(F32)
    grad_x, (parts, small_all) = _local_step(x[0], p[0, 0], pos_col, loss_target[0], w_in_full, sm,
                                             _Exchanges(shards, conv_rows))
    me = 4 * lax.axis_index("x") + 2 * lax.axis_index("y") + lax.axis_index("c")
    conv_cols = small_all["conv"].reshape(N_DEV, 8, N_DEV, D_FF // N_DEV)
    small_all["conv_mine"] = lax.dynamic_index_in_dim(conv_cols, me, axis=2, keepdims=False)

    leaves = {}
    for name in big_names:
        leaves[name] = _adamw_sharded(parts[name], w[name], m[name], v[name], "adamw_" + name)
    rep = _adamw_replicated(small_all, w, m, v)
    n_rep = len(REPLICATED)
    for n, (name, _, _) in enumerate(REPLICATED):
        leaves[name] = tuple(rep[kind * n_rep + n] for kind in range(4))
    loss = rep[4 * n_rep][0, 0]
    return (loss, grad_x[None], *[leaves[n][kind] for kind in range(4) for n in WEIGHT_ORDER])
```

```python
import math

import numpy as np
import jax
import jax.numpy as jnp
from jax import lax
from jax.experimental import pallas as pl
from jax.experimental.pallas import tpu as pltpu

F32 = jnp.float32
BF16 = jnp.bfloat16
MESH = pl.DeviceIdType.MESH

N_DEV = 8
S = 4096
D = 1024
D_HALF = 512
D_IN = 2560
D_FF = 2816
D_PLE = 256
CHUNK = 128
DILATIONS = ((1, 32), (4, 8), (16, 2))
ROPE_THETA = 500000.0
LN_EPS = 1e-5
ALPHA = 2.0 ** 0.25
NEG_INF = -1e30
INV_SQRT2 = 1.0 / math.sqrt(2.0)
INV_SQRT_2PI = 1.0 / math.sqrt(2.0 * math.pi)

ADAM_LR, ADAM_B1, ADAM_B2, ADAM_EPS, ADAM_WD, ADAM_STEP = 0.001, 0.9, 0.999, 1e-08, 0.01, 10

TM = 512
NT = S // TM
ATTN_UNROLL = 4
TN = 256
NJ = D_FF // TN
LANES = 128
VMEM_MIB = 1024 * 1024

BIG = (("w_in", (1024, 320), 1), ("w_o", (128, 1024), 0), ("w_ff_a", (1024, 352), 1),
       ("w_ff_b", (1024, 352), 1), ("w_ff_down", (352, 1024), 0), ("w_ple_gate", (128, 1024), 0),
       ("w_ple_in", (256, 128), 1))
WEIGHT_ORDER = ("w_in", "ln_z_g", "ln_z_b", "w_s", "b_s", "w_o", "ln1_g", "ln1_b", "w_ff_a", "w_ff_b",
                "conv_w", "conv_b", "w_ff_down", "ln2_g", "ln2_b", "w_ple_gate", "b_ple_gate",
                "w_ple_in", "ln3_g", "ln3_b")


def _params(semantics=None, vmem_mib=48):
    return pltpu.CompilerParams(dimension_semantics=semantics, vmem_limit_bytes=vmem_mib * VMEM_MIB)


def _dot(a, b):
    return jnp.dot(a, b, preferred_element_type=F32)


def _dot_nt(a, b):
    return lax.dot_general(a, b, (((1,), (1,)), ((), ())), preferred_element_type=F32)


def _dot_tn(a, b):
    return lax.dot_general(a, b, (((0,), (0,)), ((), ())), preferred_element_type=F32)


def _gelu(x):
    return 0.5 * x * (1.0 + lax.erf(x * INV_SQRT2))


def _gelu_grad(x):
    return 0.5 * (1.0 + lax.erf(x * INV_SQRT2)) + x * (jnp.exp(-0.5 * x * x) * INV_SQRT_2PI)


def _ln_stats(y):
    mu = jnp.mean(y, axis=-1, keepdims=True)
    yc = y - mu
    var = jnp.mean(yc * yc, axis=-1, keepdims=True)
    rstd = lax.rsqrt(var + LN_EPS)
    return yc * rstd, rstd


def _ln_bwd(dxhat, xhat, rstd):
    m1 = jnp.mean(dxhat, axis=-1, keepdims=True)
    m2 = jnp.mean(dxhat * xhat, axis=-1, keepdims=True)
    return rstd * (dxhat - m1 - xhat * m2)


def _colsum(x):
    return jnp.sum(x, axis=0, keepdims=True)


def _rows(i):
    return (i, 0)


def _fixed(*_):
    return (0, 0)


def _row_spec(width):
    return pl.BlockSpec((TM, width), _rows)


def _full_spec(shape):
    return pl.BlockSpec(shape, lambda *_: (0,) * len(shape))


def _owner_slot(j):
    return (j % 2) * 4 + j // 2


def _lane_lo():
    return lax.broadcasted_iota(jnp.int32, (CHUNK, LANES), 1) < 64


def _tril():
    r = lax.broadcasted_iota(jnp.int32, (CHUNK, CHUNK), 0)
    c = lax.broadcasted_iota(jnp.int32, (CHUNK, CHUNK), 1)
    return c <= r


def _rope_consts():
    lane = np.arange(LANES) % 64
    j = lane % 8
    inv = np.where(lane < 16, np.float32(ROPE_THETA) ** (-(2.0 * j).astype(np.float32) / np.float32(16.0)), 0.0)
    m_lo = (lane < 8).astype(np.float32)
    m_hi = ((lane >= 8) & (lane < 16)).astype(np.float32)
    return (jnp.asarray(inv, F32).reshape(1, LANES), jnp.asarray(m_lo).reshape(1, LANES),
            jnp.asarray(m_hi).reshape(1, LANES))


def _rope_tables(pos_col):
    inv, m_lo, m_hi = _rope_consts()

    def body(pos_ref, inv_ref, lo_ref, hi_ref, c_ref, sa_ref, sb_ref):
        ang = pos_ref[...] * inv_ref[...]
        c = jnp.cos(ang)
        s = jnp.sin(ang)
        lo = lo_ref[...]
        hi = hi_ref[...]
        c_ref[...] = jnp.where(lo + hi > 0.0, c, 1.0)
        sa_ref[...] = s * hi
        sb_ref[...] = -s * lo

    vec = _full_spec((1, LANES))
    out = jax.ShapeDtypeStruct((S, LANES), F32)
    return pl.pallas_call(
        body, name="rope_tables", grid=(NT,), out_shape=(out, out, out),
        in_specs=[pl.BlockSpec((TM, 1), _rows), vec, vec, vec],
        out_specs=(_row_spec(LANES),) * 3, compiler_params=_params(("parallel",)),
    )(pos_col, inv, m_lo, m_hi)


def _rope(t, c, sa, sb):
    return t * c + pltpu.roll(t, 8, 1) * sa + pltpu.roll(t, LANES - 8, 1) * sb


def _rope_t(dy, c, sa, sb):
    return dy * c + pltpu.roll(dy * sa, LANES - 8, 1) + pltpu.roll(dy * sb, 8, 1)


def _masked_ws(ws_ref):
    tril = _tril()
    return [jnp.where(tril, ws_ref[g], 0.0).astype(BF16) for g in range(8)]


def _spatial_mix(zn, wm, bs):
    lo = _lane_lo()
    rows = []
    for ch in range(TM // CHUNK):
        slabs = []
        for pr in range(4):
            zp = zn[ch * CHUNK:(ch + 1) * CHUNK, pr * LANES:(pr + 1) * LANES].astype(BF16)
            slabs.append(jnp.where(lo, _dot(wm[2 * pr], zp), _dot(wm[2 * pr + 1], zp)))
        rows.append(jnp.concatenate(slabs, axis=1) + bs)
    return jnp.concatenate(rows, axis=0)


def _proj_in_fwd(x, w_in, tabs, ln_z_g, ln_z_b, w_s, bs_exp):
    def body(x_ref, w_ref, c_ref, sa_ref, sb_ref, g_ref, b_ref, ws_ref, bs_ref,
             q_ref, k_ref, v_ref, u_ref, z_ref, gm_ref, xb_ref):
        xb = x_ref[...].astype(BF16)
        xb_ref[...] = xb
        c, sa, sb = c_ref[...], sa_ref[...], sb_ref[...]
        hq = _dot(xb, w_ref[:, 0:512])
        hk = _dot(xb, w_ref[:, 512:1024])
        for s in range(4):
            sl = slice(s * LANES, (s + 1) * LANES)
            q_ref[:, sl] = _rope(hq[:, sl], c, sa, sb)
            k_ref[:, sl] = _rope(hk[:, sl], c, sa, sb)
        v_ref[...] = _dot(xb, w_ref[:, 1024:1536])
        u_pre = _dot(xb, w_ref[:, 1536:2048])
        z_pre = _dot(xb, w_ref[:, 2048:2560])
        u_ref[...] = u_pre
        z_ref[...] = z_pre
        zhat, _ = _ln_stats(_gelu(z_pre))
        zn = zhat * g_ref[...] + b_ref[...]
        mixed = _spatial_mix(zn, _masked_ws(ws_ref), bs_ref[...])
        gm_ref[...] = (_gelu(u_pre) * mixed).astype(BF16)

    half = jax.ShapeDtypeStruct((S, D_HALF), F32)
    tab = _row_spec(LANES)
    return pl.pallas_call(
        body, name="proj_in_fwd", grid=(NT,),
        out_shape=(half, half, half, half, half, jax.ShapeDtypeStruct((S, D_HALF), BF16),
                   jax.ShapeDtypeStruct((S, D), BF16)),
        in_specs=[_row_spec(D), _full_spec((D, D_IN)), tab, tab, tab, _full_spec((1, D_HALF)),
                  _full_spec((1, D_HALF)), _full_spec((8, CHUNK, CHUNK)), _full_spec((CHUNK, D_HALF))],
        out_specs=(_row_spec(D_HALF),) * 6 + (_row_spec(D),),
        compiler_params=_params(("parallel",)),
    )(x, w_in, *tabs, ln_z_g, ln_z_b, w_s, bs_exp)


def _store_band_bias(bias_ref):
    qi = lax.broadcasted_iota(jnp.int32, (CHUNK, 2 * CHUNK), 0)
    kj = lax.broadcasted_iota(jnp.int32, (CHUNK, 2 * CHUNK), 1)
    band = (kj >= qi) & (kj <= qi + CHUNK)
    bias_ref[0] = jnp.where(band, 0.0, NEG_INF)
    bias_ref[1] = jnp.where(band & (kj >= CHUNK), 0.0, NEG_INF)


def _permuted_rows(ref, d, r):
    return ref[...] if d == 1 else ref[pl.ds(r, S // d, stride=d), :]


def _attention_fwd(q, k, v, carried=None):
    def body(q_ref, k_ref, v_ref, o_ref, lse_ref, qb, kb, v0b, v1b, bias, op, lp, ob0, lb0, ob1, lb1, ob2, lb2):
        lo = _lane_lo()
        lo_f = lo.astype(F32)[0:1, :]
        hi_f = 1.0 - lo_f
        zero_pad = jnp.zeros((CHUNK, LANES), BF16)
        for buf in (qb, kb, v0b, v1b):
            buf[0:CHUNK, :] = zero_pad
        _store_band_bias(bias)
        outs = ((ob0, lb0), (ob1, lb1), (ob2, lb2))
        for (d, nb), (ob, lb) in zip(DILATIONS, outs):
            length = S // d
            for r in range(d):
                dst = slice(CHUNK + r * length, CHUNK + (r + 1) * length)
                qb[dst, :] = (_permuted_rows(q_ref, d, r) * 0.125).astype(BF16)
                kb[dst, :] = _permuted_rows(k_ref, d, r).astype(BF16)
                vs = _permuted_rows(v_ref, d, r)
                v0b[dst, :] = (vs * lo_f + hi_f).astype(BF16)
                v1b[dst, :] = (vs * hi_f + lo_f).astype(BF16)

            def block(b, carry, nb=nb):
                base = pl.multiple_of(b * CHUNK, CHUNK)
                add = bias[jnp.where(b % nb == 0, 1, 0)]
                qblk = qb[pl.ds(pl.multiple_of(base + CHUNK, CHUNK), CHUNK), :]
                kblk = kb[pl.ds(base, 2 * CHUNK), :]
                pv, mx = [], []
                for head, vh in enumerate((v0b, v1b)):
                    qh = jnp.where(lo, qblk, 0) if head == 0 else jnp.where(lo, 0, qblk)
                    s = _dot_nt(qh, kblk) + add
                    m = jnp.max(s, axis=-1, keepdims=True)
                    p = jnp.exp(s - m).astype(BF16)
                    pv.append(_dot(p, vh[pl.ds(base, 2 * CHUNK), :]))
                    mx.append(m)
                den = pltpu.roll(jnp.where(lo, pv[1], pv[0]), 64, 1)
                op[pl.ds(base, CHUNK), :] = jnp.where(lo, pv[0], pv[1]) / den
                lp[pl.ds(base, CHUNK), :] = jnp.where(lo, mx[0], mx[1]) + jnp.log(den)
                return carry

            lax.fori_loop(0, S // CHUNK, block, 0, unroll=ATTN_UNROLL)
            for r in range(d):
                src = slice(r * length, (r + 1) * length)
                if d == 1:
                    ob[...] = op[...]
                    lb[...] = lp[...]
                else:
                    ob[pl.ds(r, length, stride=d), :] = op[src, :]
                    lb[pl.ds(r, length, stride=d), :] = lp[src, :]
        for t in range(NT):
            rows = slice(t * TM, (t + 1) * TM)
            l0, l1, l2 = lb0[rows, :], lb1[rows, :], lb2[rows, :]
            mx = jnp.maximum(jnp.maximum(l0, l1), l2)
            e0, e1, e2 = jnp.exp(l0 - mx), jnp.exp(l1 - mx), jnp.exp(l2 - mx)
            den = e0 + e1 + e2
            o_ref[rows, :] = (e0 * ob0[rows, :] + e1 * ob1[rows, :] + e2 * ob2[rows, :]) / den
            lse_ref[rows, :] = mx + jnp.log(den)

    slab = pl.BlockSpec((S, LANES), lambda h: (0, h))
    out = jax.ShapeDtypeStruct((S, D_HALF), F32)
    padded = pltpu.VMEM((CHUNK + S, LANES), BF16)
    whole = pltpu.VMEM((S, LANES), F32)
    return _host_call(
        body, carried, name="attention_fwd", grid=(4,), out_shape=(out, out),
        in_specs=[slab, slab, slab], out_specs=(slab, slab),
        scratch_shapes=[padded] * 4 + [pltpu.VMEM((2, CHUNK, 2 * CHUNK), F32)] + [whole] * 8,
        params=_params(("arbitrary",), 56), args=(q, k, v))


def _mix_ln1_fwd(attn, gm, w_o, x, g1, b1):
    def body(a_ref, gm_ref, w_ref, x_ref, g_ref, b_ref, xhat_ref, rstd_ref, x1b_ref):
        mix = _dot(a_ref[...].astype(BF16), w_ref[0:D_HALF, :]) + _dot(gm_ref[...], w_ref[D_HALF:D, :])
        xhat, rstd = _ln_stats(ALPHA * x_ref[...] + mix)
        xhat_ref[...] = xhat
        rstd_ref[...] = rstd
        x1b_ref[...] = (xhat * g_ref[...] + b_ref[...]).astype(BF16)

    vec = _full_spec((1, D))
    return pl.pallas_call(
        body, name="mix_ln1_fwd", grid=(NT,),
        out_shape=(jax.ShapeDtypeStruct((S, D), F32), jax.ShapeDtypeStruct((S, 1), F32),
                   jax.ShapeDtypeStruct((S, D), BF16)),
        in_specs=[_row_spec(D_HALF), _row_spec(D_HALF), _full_spec((D, D)), _row_spec(D), vec, vec],
        out_specs=(_row_spec(D), pl.BlockSpec((TM, 1), _rows), _row_spec(D)),
        compiler_params=_params(("parallel",)),
    )(attn, gm, w_o, x, g1, b1)


def _ffn_up_fwd(x1b, w_a, w_b, conv_w8, conv_b, carried=None):
    def body(x_ref, wa_ref, wb_ref, cw_ref, cb_ref, ap_ref, a_ref, bl_ref, h_ref, carry):
        @pl.when(pl.program_id(1) == 0)
        def _():
            carry[...] = jnp.zeros_like(carry)

        xb = x_ref[...]
        ap = _dot(xb, wa_ref[...])
        bl = _dot(xb, wb_ref[...])
        row = lax.broadcasted_iota(jnp.int32, (TM, TN), 0)
        c6, c7 = carry[6:7, :], carry[7:8, :]
        m1 = jnp.where(row == 0, c7, pltpu.roll(ap, 1, 0))
        m2 = jnp.where(row == 0, c6, jnp.where(row == 1, c7, pltpu.roll(ap, 2, 0)))
        a = cb_ref[...] + cw_ref[0:1, :] * m2 + cw_ref[1:2, :] * m1 + cw_ref[2:3, :] * ap
        carry[...] = ap[TM - 8:TM, :]
        ap_ref[...] = ap
        a_ref[...] = a
        bl_ref[...] = bl
        h_ref[...] = (_gelu(a) * bl).astype(BF16)

    tile = pl.BlockSpec((TM, TN), lambda j, i: (i, j))
    wcol = pl.BlockSpec((D, TN), lambda j, i: (0, j))
    ff = jax.ShapeDtypeStruct((S, D_FF), F32)
    return _host_call(
        body, carried, name="ffn_up_fwd", grid=(NJ, NT),
        out_shape=(ff, ff, ff, jax.ShapeDtypeStruct((S, D_FF), BF16)),
        in_specs=[pl.BlockSpec((TM, D), lambda j, i: (i, 0)), wcol, wcol,
                  pl.BlockSpec((8, TN), lambda j, i: (0, j)), pl.BlockSpec((1, TN), lambda j, i: (0, j))],
        out_specs=(tile, tile, tile, tile),
        scratch_shapes=[pltpu.VMEM((8, TN), F32)],
        params=_params(("arbitrary", "arbitrary")), args=(x1b, w_a, w_b, conv_w8, conv_b))


def _ffn_down_ln2_fwd(hff, w_down, xhat1, g1, b1):
    def body(h_ref, w_ref, xh_ref, g_ref, b_ref, xhat_ref, rstd_ref):
        x1 = xh_ref[...] * g_ref[...] + b_ref[...]
        xhat, rstd = _ln_stats(ALPHA * x1 + _dot(h_ref[...], w_ref[...]))
        xhat_ref[...] = xhat
        rstd_ref[...] = rstd

    vec = _full_spec((1, D))
    return pl.pallas_call(
        body, name="ffn_down_ln2_fwd", grid=(NT,),
        out_shape=(jax.ShapeDtypeStruct((S, D), F32), jax.ShapeDtypeStruct((S, 1), F32)),
        in_specs=[_row_spec(D_FF), _full_spec((D_FF, D)), _row_spec(D), vec, vec],
        out_specs=(_row_spec(D), pl.BlockSpec((TM, 1), _rows)),
        compiler_params=_params(("parallel",)),
    )(hff, w_down, xhat1, g1, b1)


def _tail_fwd_bwd(xhat2, rstd2, p, target, w_g, w_p, g2, b2, bg, g3, b3):
    def body(xh_ref, rs_ref, p_ref, t_ref, wg_ref, wp_ref, g2_ref, b2_ref, bg_ref, g3_ref, b3_ref,
             loss_ref, dy2_ref, dy2b_ref, gwg_ref, gwp_ref, vec_ref, dwg_ref, dwp_ref):
        @pl.when(pl.program_id(0) == 0)
        def _():
            loss_ref[...] = jnp.zeros_like(loss_ref)
            dwg_ref[...] = jnp.zeros_like(dwg_ref)
            dwp_ref[...] = jnp.zeros_like(dwp_ref)
            vec_ref[...] = jnp.zeros_like(vec_ref)

        xhat2_t = xh_ref[...]
        x2 = xhat2_t * g2_ref[...] + b2_ref[...]
        x2b = x2.astype(BF16)
        pb = p_ref[...].astype(BF16)
        gate = jax.nn.sigmoid(_dot(x2b, wg_ref[...]) + bg_ref[...])
        pin = _dot(pb, wp_ref[...])
        xhat3, rstd3 = _ln_stats(ALPHA * x2 + gate * pin)
        err = xhat3 * g3_ref[...] + b3_ref[...] - t_ref[...]
        loss_ref[...] += jnp.sum(jnp.mean(err * err, axis=-1, keepdims=True), axis=0, keepdims=True) * 0.5
        dout = err * (1.0 / D)
        dy3 = _ln_bwd(dout * g3_ref[...], xhat3, rstd3)
        dgp = dy3 * pin * gate * (1.0 - gate)
        dgpb = dgp.astype(BF16)
        dwg_ref[...] += _dot_tn(x2b, dgpb)
        dwp_ref[...] += _dot_tn(pb, (dy3 * gate).astype(BF16))
        dx2 = ALPHA * dy3 + _dot_nt(dgpb, wg_ref[...])
        dy2 = _ln_bwd(dx2 * g2_ref[...], xhat2_t, rs_ref[...])
        dy2_ref[...] = dy2
        dy2b_ref[...] = dy2.astype(BF16)
        vec_ref[0:1, :] += _colsum(dgp)
        vec_ref[1:2, :] += _colsum(dout * xhat3)
        vec_ref[2:3, :] += _colsum(dout)
        vec_ref[3:4, :] += _colsum(dx2 * xhat2_t)
        vec_ref[4:5, :] += _colsum(dx2)

        @pl.when(pl.program_id(0) == NT - 1)
        def _():
            for j in range(N_DEV):
                gwg_ref[_owner_slot(j)] = dwg_ref[LANES * j:LANES * (j + 1), :].astype(BF16)
                gwp_ref[_owner_slot(j)] = dwp_ref[:, LANES * j:LANES * (j + 1)].astype(BF16)

    vec = _full_spec((1, D))
    return pl.pallas_call(
        body, name="tail_fwd_bwd", grid=(NT,),
        out_shape=(jax.ShapeDtypeStruct((8, LANES), F32), jax.ShapeDtypeStruct((S, D), F32),
                   jax.ShapeDtypeStruct((S, D), BF16), jax.ShapeDtypeStruct((N_DEV, D // N_DEV, D), BF16),
                   jax.ShapeDtypeStruct((N_DEV, D_PLE, D // N_DEV), BF16), jax.ShapeDtypeStruct((8, D), F32)),
        in_specs=[_row_spec(D), pl.BlockSpec((TM, 1), _rows), _row_spec(D_PLE), _row_spec(D),
                  _full_spec((D, D)), _full_spec((D_PLE, D)), vec, vec, vec, vec, vec],
        out_specs=(_full_spec((8, LANES)), _row_spec(D), _row_spec(D), _full_spec((N_DEV, D // N_DEV, D)),
                   _full_spec((N_DEV, D_PLE, D // N_DEV)), _full_spec((8, D))),
        scratch_shapes=[pltpu.VMEM((D, D), F32), pltpu.VMEM((D_PLE, D), F32)],
        compiler_params=_params(("arbitrary",)),
    )(xhat2, rstd2, p, target, w_g, w_p, g2, b2, bg, g3, b3)


def _ffn_bwd(dy2b, x1b, w_down, w_a, w_b, a_pre, a, b_lin, hff, conv_w8):
    def body(dy_ref, x_ref, wd_ref, wa_ref, wb_ref, ap_ref, a_ref, bl_ref, h_ref, cw_ref,
             dwd_ref, dwa_ref, dwb_ref, dcw_ref, dx_hbm, dx_acc, carry, sem):
        j, i = pl.program_id(0), pl.program_id(1)

        @pl.when(i == 0)
        def _():
            carry[...] = jnp.zeros_like(carry)
            dwd_ref[...] = jnp.zeros_like(dwd_ref)
            dwa_ref[...] = jnp.zeros_like(dwa_ref)
            dwb_ref[...] = jnp.zeros_like(dwb_ref)
            dcw_ref[...] = jnp.zeros_like(dcw_ref)

        dyb = dy_ref[...]
        xb = x_ref[...]
        dh = _dot_nt(dyb, wd_ref[...])
        av = a_ref[...]
        dbl = dh * _gelu(av)
        da = dh * bl_ref[...] * _gelu_grad(av)
        row = lax.broadcasted_iota(jnp.int32, (TM, TN), 0)
        c0, c1 = carry[0:1, :], carry[1:2, :]
        p1 = jnp.where(row == TM - 1, c0, pltpu.roll(da, TM - 1, 0))
        p2 = jnp.where(row == TM - 2, c0, jnp.where(row == TM - 1, c1, pltpu.roll(da, TM - 2, 0)))
        carry[...] = da[0:8, :]
        ap = ap_ref[...]
        dcw_ref[3:4, :] += _colsum(da)
        dcw_ref[0:1, :] += _colsum(ap * p2)
        dcw_ref[1:2, :] += _colsum(ap * p1)
        dcw_ref[2:3, :] += _colsum(ap * da)
        dap = (cw_ref[2:3, :] * da + cw_ref[1:2, :] * p1 + cw_ref[0:1, :] * p2).astype(BF16)
        dblb = dbl.astype(BF16)
        dwa_ref[...] += _dot_tn(xb, dap)
        dwb_ref[...] += _dot_tn(xb, dblb)
        dwd_ref[...] += _dot_tn(h_ref[...], dyb)
        dx = _dot_nt(dap, wa_ref[...]) + _dot_nt(dblb, wb_ref[...])
        rows = pl.ds(pl.multiple_of((NT - 1 - i) * TM, TM), TM)

        @pl.when(j == 0)
        def _():
            dx_acc[rows, :] = dx

        @pl.when(j > 0)
        def _():
            dx_acc[rows, :] += dx

        @pl.when((j == NJ - 1) & (i == NT - 1))
        def _():
            cp = pltpu.make_async_copy(dx_acc, dx_hbm, sem)
            cp.start()
            cp.wait()

    rev_rows = lambda j, i: (NT - 1 - i, 0)
    rev_tile = pl.BlockSpec((TM, TN), lambda j, i: (NT - 1 - i, j))
    wcol = pl.BlockSpec((D, TN), lambda j, i: (0, j))
    small = pl.BlockSpec((8, TN), lambda j, i: (0, j))
    return pl.pallas_call(
        body, name="ffn_bwd", grid=(NJ, NT),
        out_shape=(jax.ShapeDtypeStruct((D_FF, D), F32), jax.ShapeDtypeStruct((D, D_FF), F32),
                   jax.ShapeDtypeStruct((D, D_FF), F32), jax.ShapeDtypeStruct((8, D_FF), F32),
                   jax.ShapeDtypeStruct((S, D), F32)),
        in_specs=[pl.BlockSpec((TM, D), rev_rows), pl.BlockSpec((TM, D), rev_rows),
                  pl.BlockSpec((TN, D), lambda j, i: (j, 0)), wcol, wcol, rev_tile, rev_tile, rev_tile,
                  rev_tile, small],
        out_specs=(pl.BlockSpec((TN, D), lambda j, i: (j, 0)), wcol, wcol, small,
                   pl.BlockSpec(memory_space=pl.ANY)),
        scratch_shapes=[pltpu.VMEM((S, D), F32), pltpu.VMEM((8, TN), F32), pltpu.SemaphoreType.DMA],
        compiler_params=_params(("arbitrary", "arbitrary"), 56),
    )(dy2b, x1b, w_down, w_a, w_b, a_pre, a, b_lin, hff, conv_w8)


def _ln1_mix_bwd(dy2, dx1_ffn, xhat1, rstd1, g1, attn, gm, w_o, carried=None):
    def body(dy2_ref, dxf_ref, xh_ref, rs_ref, g_ref, a_ref, gm_ref, w_ref,
             dy1_ref, da_ref, dlt_ref, dgm_ref, gwo_ref, vec_ref, dwo_ref):
        @pl.when(pl.program_id(0) == 0)
        def _():
            dwo_ref[...] = jnp.zeros_like(dwo_ref)
            vec_ref[...] = jnp.zeros_like(vec_ref)

        xhat = xh_ref[...]
        dx1 = ALPHA * dy2_ref[...] + dxf_ref[...]
        vec_ref[0:1, :] += _colsum(dx1 * xhat)
        vec_ref[1:2, :] += _colsum(dx1)
        dy1 = _ln_bwd(dx1 * g_ref[...], xhat, rs_ref[...])
        dy1_ref[...] = dy1
        dy1b = dy1.astype(BF16)
        dmix = _dot_nt(dy1b, w_ref[...])
        attn_t = a_ref[...]
        d_attn = dmix[:, 0:D_HALF]
        da_ref[...] = d_attn
        dgm_ref[...] = dmix[:, D_HALF:D]
        lo = (lax.broadcasted_iota(jnp.int32, (TM, LANES), 1) < 64)
        for s in range(4):
            sl = slice(s * LANES, (s + 1) * LANES)
            prod = d_attn[:, sl] * attn_t[:, sl]
            s0 = jnp.sum(jnp.where(lo, prod, 0.0), axis=-1, keepdims=True)
            s1 = jnp.sum(jnp.where(lo, 0.0, prod), axis=-1, keepdims=True)
            dlt_ref[:, sl] = jnp.where(lo, s0, s1)
        dwo_ref[0:D_HALF, :] += _dot_tn(attn_t.astype(BF16), dy1b)
        dwo_ref[D_HALF:D, :] += _dot_tn(gm_ref[...], dy1b)

        @pl.when(pl.program_id(0) == NT - 1)
        def _():
            for j in range(N_DEV):
                gwo_ref[_owner_slot(j)] = dwo_ref[LANES * j:LANES * (j + 1), :].astype(BF16)

    half = jax.ShapeDtypeStruct((S, D_HALF), F32)
    return _host_call(
        body, carried, name="ln1_mix_bwd", grid=(NT,),
        out_shape=(jax.ShapeDtypeStruct((S, D), F32), half, half, half,
                   jax.ShapeDtypeStruct((N_DEV, D // N_DEV, D), BF16), jax.ShapeDtypeStruct((8, D), F32)),
        in_specs=[_row_spec(D), _row_spec(D), _row_spec(D), pl.BlockSpec((TM, 1), _rows), _full_spec((1, D)),
                  _row_spec(D_HALF), _row_spec(D_HALF), _full_spec((D, D))],
        out_specs=(_row_spec(D), _row_spec(D_HALF), _row_spec(D_HALF), _row_spec(D_HALF),
                   _full_spec((N_DEV, D // N_DEV, D)), _full_spec((8, D))),
        scratch_shapes=[pltpu.VMEM((D, D), F32)],
        params=_params(("arbitrary",)), args=(dy2, dx1_ffn, xhat1, rstd1, g1, attn, gm, w_o))


def _gmlp_bwd(d_gm, u_pre, z_pre, ln_z_g, ln_z_b, w_s, bs_exp, carried=None):
    def body(dg_ref, u_ref, z_ref, g_ref, b_ref, ws_ref, bs_ref, du_ref, dz_ref, dws_ref, dbs_ref, vec_ref):
        @pl.when(pl.program_id(0) == 0)
        def _():
            dws_ref[...] = jnp.zeros_like(dws_ref)
            dbs_ref[...] = jnp.zeros_like(dbs_ref)
            vec_ref[...] = jnp.zeros_like(vec_ref)

        u_pre_t, z_pre_t, dgm = u_ref[...], z_ref[...], dg_ref[...]
        zhat, rstd = _ln_stats(_gelu(z_pre_t))
        zn = zhat * g_ref[...] + b_ref[...]
        wm = _masked_ws(ws_ref)
        mixed = _spatial_mix(zn, wm, bs_ref[...])
        du_ref[...] = (dgm * mixed * _gelu_grad(u_pre_t)).astype(BF16)
        dmixed = dgm * _gelu(u_pre_t)
        lo = _lane_lo()
        tril = _tril()
        group_of_lane = lax.broadcasted_iota(jnp.int32, (8, D_HALF), 1) // 64
        pick = (group_of_lane == lax.broadcasted_iota(jnp.int32, (8, D_HALF), 0)).astype(F32)
        dzn_rows = []
        for ch in range(TM // CHUNK):
            rows = slice(ch * CHUNK, (ch + 1) * CHUNK)
            dbs_ref[...] += lax.dot_general(pick, dmixed[rows, :], (((1,), (1,)), ((), ())),
                                            precision=lax.Precision.HIGHEST, preferred_element_type=F32)
            slabs = []
            for pr in range(4):
                sl = slice(pr * LANES, (pr + 1) * LANES)
                dm = dmixed[rows, sl]
                zp = zn[rows, sl].astype(BF16)
                dm_lo = jnp.where(lo, dm, 0.0).astype(BF16)
                dm_hi = jnp.where(lo, 0.0, dm).astype(BF16)
                dws_ref[2 * pr] += jnp.where(tril, _dot_nt(dm_lo, zp), 0.0)
                dws_ref[2 * pr + 1] += jnp.where(tril, _dot_nt(dm_hi, zp), 0.0)
                dmb = dm.astype(BF16)
                slabs.append(jnp.where(lo, _dot_tn(wm[2 * pr], dmb), _dot_tn(wm[2 * pr + 1], dmb)))
            dzn_rows.append(jnp.concatenate(slabs, axis=1))
        dzn = jnp.concatenate(dzn_rows, axis=0)
        vec_ref[0:1, :] += _colsum(dzn * zhat)
        vec_ref[1:2, :] += _colsum(dzn)
        dz = _ln_bwd(dzn * g_ref[...], zhat, rstd)
        dz_ref[...] = (dz * _gelu_grad(z_pre_t)).astype(BF16)

    halfb = jax.ShapeDtypeStruct((S, D_HALF), BF16)
    vec = _full_spec((1, D_HALF))
    return _host_call(
        body, carried, name="gmlp_bwd", grid=(NT,),
        out_shape=(halfb, halfb, jax.ShapeDtypeStruct((8, CHUNK, CHUNK), F32),
                   jax.ShapeDtypeStruct((8, CHUNK), F32), jax.ShapeDtypeStruct((8, D_HALF), F32)),
        in_specs=[_row_spec(D_HALF), _row_spec(D_HALF), _row_spec(D_HALF), vec, vec,
                  _full_spec((8, CHUNK, CHUNK)), _full_spec((CHUNK, D_HALF))],
        out_specs=(_row_spec(D_HALF), _row_spec(D_HALF), _full_spec((8, CHUNK, CHUNK)),
                   _full_spec((8, CHUNK)), _full_spec((8, D_HALF))),
        scratch_shapes=[], params=_params(("arbitrary",)), args=(d_gm, u_pre, z_pre, ln_z_g, ln_z_b, w_s, bs_exp))


def _attention_bwd(q, k, v, lse, d_attn, delta, tabs, carried=None):
    def body(q_ref, k_ref, v_ref, l_ref, do_ref, dl_ref, c_ref, sa_ref, sb_ref, dq_ref, dk_ref, dv_ref,
             qb, kb, vb, gb, bias, lsp, dlp, dqp, dk_own, dk_prev, dv_own, dv_prev, dqa, dka, dva):
        lo = _lane_lo()
        zero_pad = jnp.zeros((CHUNK, LANES), BF16)
        for buf in (qb, kb, vb, gb):
            buf[0:CHUNK, :] = zero_pad
        dk_prev[S:S + CHUNK, :] = jnp.zeros((CHUNK, LANES), F32)
        dv_prev[S:S + CHUNK, :] = jnp.zeros((CHUNK, LANES), F32)
        _store_band_bias(bias)
        for d, nb in DILATIONS:
            length = S // d
            for r in range(d):
                dst = slice(CHUNK + r * length, CHUNK + (r + 1) * length)
                src = slice(r * length, (r + 1) * length)
                qb[dst, :] = (_permuted_rows(q_ref, d, r) * 0.125).astype(BF16)
                kb[dst, :] = _permuted_rows(k_ref, d, r).astype(BF16)
                vb[dst, :] = _permuted_rows(v_ref, d, r).astype(BF16)
                gb[dst, :] = _permuted_rows(do_ref, d, r).astype(BF16)
                lsp[src, :] = _permuted_rows(l_ref, d, r)
                dlp[src, :] = _permuted_rows(dl_ref, d, r)

            def block(b, carry, nb=nb):
                base = pl.multiple_of(b * CHUNK, CHUNK)
                own = pl.multiple_of(base + CHUNK, CHUNK)
                add = bias[jnp.where(b % nb == 0, 1, 0)]
                qblk = qb[pl.ds(own, CHUNK), :]
                gblk = gb[pl.ds(own, CHUNK), :]
                kblk = kb[pl.ds(base, 2 * CHUNK), :]
                vblk = vb[pl.ds(base, 2 * CHUNK), :]
                lse_t = lsp[pl.ds(base, CHUNK), :]
                dlt_t = dlp[pl.ds(base, CHUNK), :]
                dq, dk, dv = [], [], []
                for head in range(2):
                    qh = jnp.where(lo, qblk, 0) if head == 0 else jnp.where(lo, 0, qblk)
                    gh = jnp.where(lo, gblk, 0) if head == 0 else jnp.where(lo, 0, gblk)
                    col = 64 * head
                    p = jnp.exp(_dot_nt(qh, kblk) + add - lse_t[:, col:col + 1])
                    ds = (p * (_dot_nt(gh, vblk) - dlt_t[:, col:col + 1])).astype(BF16)
                    dv.append(_dot_tn(p.astype(BF16), gh))
                    dk.append(_dot_tn(ds, qh))
                    dq.append(_dot(ds, kblk))
                dqp[pl.ds(base, CHUNK), :] = jnp.where(lo, dq[0], dq[1]) * 0.125
                dk_blk = dk[0] + dk[1]
                dv_blk = dv[0] + dv[1]
                dk_prev[pl.ds(base, CHUNK), :] = dk_blk[0:CHUNK, :]
                dk_own[pl.ds(own, CHUNK), :] = dk_blk[CHUNK:2 * CHUNK, :]
                dv_prev[pl.ds(base, CHUNK), :] = dv_blk[0:CHUNK, :]
                dv_own[pl.ds(own, CHUNK), :] = dv_blk[CHUNK:2 * CHUNK, :]
                return carry

            lax.fori_loop(0, S // CHUNK, block, 0, unroll=ATTN_UNROLL)
            for r in range(d):
                src = slice(r * length, (r + 1) * length)
                pad = slice(CHUNK + r * length, CHUNK + (r + 1) * length)
                if d == 1:
                    dqa[...] = dqp[...]
                    dka[...] = dk_own[pad, :] + dk_prev[pad, :]
                    dva[...] = dv_own[pad, :] + dv_prev[pad, :]
                else:
                    dst = pl.ds(r, length, stride=d)
                    dqa[dst, :] = dqa[dst, :] + dqp[src, :]
                    dka[dst, :] = dka[dst, :] + (dk_own[pad, :] + dk_prev[pad, :])
                    dva[dst, :] = dva[dst, :] + (dv_own[pad, :] + dv_prev[pad, :])
        for t in range(NT):
            rows = slice(t * TM, (t + 1) * TM)
            c, sa, sb = c_ref[rows, :], sa_ref[rows, :], sb_ref[rows, :]
            dq_ref[rows, :] = _rope_t(dqa[rows, :], c, sa, sb).astype(BF16)
            dk_ref[rows, :] = _rope_t(dka[rows, :], c, sa, sb).astype(BF16)
            dv_ref[rows, :] = dva[rows, :].astype(BF16)

    slab = pl.BlockSpec((S, LANES), lambda h: (0, h), pipeline_mode=pl.Buffered(1))
    tab = pl.BlockSpec((S, LANES), lambda h: (0, 0), pipeline_mode=pl.Buffered(1))
    out_slab = pl.BlockSpec((S, LANES), lambda h: (0, h))
    out = jax.ShapeDtypeStruct((S, D_HALF), BF16)
    padded_b = pltpu.VMEM((CHUNK + S, LANES), BF16)
    padded_f = pltpu.VMEM((CHUNK + S, LANES), F32)
    whole = pltpu.VMEM((S, LANES), F32)
    return _host_call(
        body, carried, name="attention_bwd", grid=(4,), out_shape=(out, out, out),
        in_specs=[slab] * 6 + [tab] * 3, out_specs=(out_slab,) * 3,
        scratch_shapes=[padded_b] * 4 + [pltpu.VMEM((2, CHUNK, 2 * CHUNK), F32)] + [whole] * 3
        + [padded_f] * 4 + [whole] * 3,
        params=_params(("arbitrary",), 60), args=(q, k, v, lse, d_attn, delta, *tabs))


def _proj_in_bwd_w(xb, parts):
    def body(x_ref, p0, p1, p2, p3, p4, gw_ref, dw_ref):
        @pl.when(pl.program_id(0) == 0)
        def _():
            dw_ref[...] = jnp.zeros_like(dw_ref)

        xt = x_ref[...]
        for n, part in enumerate((p0, p1, p2, p3, p4)):
            dw_ref[:, n * D_HALF:(n + 1) * D_HALF] += _dot_tn(xt, part[...])

        @pl.when(pl.program_id(0) == NT - 1)
        def _():
            width = D_IN // N_DEV
            for j in range(N_DEV):
                gw_ref[_owner_slot(j)] = dw_ref[:, width * j:width * (j + 1)].astype(BF16)

    return pl.pallas_call(
        body, name="proj_in_bwd_w", grid=(NT,), out_shape=jax.ShapeDtypeStruct((N_DEV, D, D_IN // N_DEV), BF16),
        in_specs=[_row_spec(D)] + [_row_spec(D_HALF)] * 5, out_specs=_full_spec((N_DEV, D, D_IN // N_DEV)),
        scratch_shapes=[pltpu.VMEM((D, D_IN), F32)],
        compiler_params=_params(("arbitrary",)),
    )(xb, *parts)


def _proj_in_bwd_x(dy1, parts, w_in, carried=None):
    def body(dy_ref, p0, p1, p2, p3, p4, w_ref, gx_ref):
        acc = ALPHA * dy_ref[...]
        for n, part in enumerate((p0, p1, p2, p3, p4)):
            acc += _dot_nt(part[...], w_ref[:, n * D_HALF:(n + 1) * D_HALF])
        gx_ref[...] = acc

    return _host_call(
        body, carried, name="proj_in_bwd_x", grid=(NT,), out_shape=(jax.ShapeDtypeStruct((S, D), F32),),
        in_specs=[_row_spec(D)] + [_row_spec(D_HALF)] * 5 + [_full_spec((D, D_IN))], out_specs=(_row_spec(D),),
        scratch_shapes=[], params=_params(("arbitrary",)), args=(dy1, *parts, w_in))


def _to_natural(blocks, name):
    n, rows, w = blocks.shape
    tile = min(rows, 256)

    def body(i_ref, o_ref):
        o_ref[...] = jnp.concatenate([i_ref[j] for j in range(n)], axis=1)

    return pl.pallas_call(
        body, name=name, grid=(rows // tile,), out_shape=jax.ShapeDtypeStruct((rows, n * w), blocks.dtype),
        in_specs=[pl.BlockSpec((n, tile, w), lambda i: (0, i, 0))],
        out_specs=pl.BlockSpec((tile, n * w), lambda i: (i, 0)), compiler_params=_params(("parallel",)),
    )(blocks)


def _column_blocks(full, name):
    rows, cols = full.shape
    w = cols // N_DEV
    tile = 256

    def body(i_ref, o_ref):
        for j in range(N_DEV):
            o_ref[_owner_slot(j)] = i_ref[:, j * w:(j + 1) * w].astype(BF16)

    return pl.pallas_call(
        body, name=name, grid=(rows // tile,), out_shape=jax.ShapeDtypeStruct((N_DEV, rows, w), BF16),
        in_specs=[pl.BlockSpec((tile, cols), lambda i: (i, 0))],
        out_specs=pl.BlockSpec((N_DEV, tile, w), lambda i: (0, i, 0)), compiler_params=_params(("parallel",)),
    )(full)


def _row_blocks(full):
    rows, cols = full.shape
    r = rows // N_DEV

    def body(i_ref, o_ref):
        o_ref[0] = i_ref[...].astype(BF16)

    return pl.pallas_call(
        body, name="ff_down_grad_blocks", grid=(N_DEV,), out_shape=jax.ShapeDtypeStruct((N_DEV, r, cols), BF16),
        in_specs=[pl.BlockSpec((r, cols), lambda s: ((s % 4) * 2 + s // 4, 0))],
        out_specs=pl.BlockSpec((1, r, cols), lambda s: (s, 0, 0)), compiler_params=_params(("parallel",)),
    )(full)


def _local_step(x, p, pos_col, target, w_in, sm, ex):
    bs_exp = jnp.repeat(sm["b_s"].T, 64, axis=1)
    tabs = _rope_tables(pos_col)
    q, k, v, u_pre, z_pre, gm, xb = _proj_in_fwd(x, w_in, tabs, sm["ln_z_g"], sm["ln_z_b"], sm["w_s"], bs_exp)
    (attn, lse), got = _attention_fwd(q, k, v, ex.gather_first())
    wa = ex.weights_first(got)
    xhat1, rstd1, x1b = _mix_ln1_fwd(attn, gm, wa["w_o"], x, sm["ln1_g"], sm["ln1_b"])
    (a_pre, a, b_lin, hff), got = _ffn_up_fwd(x1b, wa["w_ff_a"], wa["w_ff_b"], wa["conv_w8"], sm["conv_b"],
                                              ex.gather_second())
    wc = ex.weights_second(got)
    xhat2, rstd2 = _ffn_down_ln2_fwd(hff, wc["w_ff_down"], xhat1, sm["ln1_g"], sm["ln1_b"])
    loss, dy2, dy2b, dw_g, dw_p, vec_tail = _tail_fwd_bwd(
        xhat2, rstd2, p, target, wc["w_ple_gate"], wc["w_ple_in"], sm["ln2_g"], sm["ln2_b"],
        sm["b_ple_gate"], sm["ln3_g"], sm["ln3_b"])
    dw_down, dw_a, dw_b, dconv, dx1_ffn = _ffn_bwd(dy2b, x1b, wc["w_ff_down"], wa["w_ff_a"], wa["w_ff_b"],
                                                    a_pre, a, b_lin, hff, wa["conv_w8"])
    (dy1, d_attn, delta, d_gm, dw_o, vec_ln1), _ = _ln1_mix_bwd(
        dy2, dx1_ffn, xhat1, rstd1, sm["ln1_g"], attn, gm, wa["w_o"])
    early = {"w_ff_a": _column_blocks(dw_a, "ff_a_grad_blocks"), "w_ff_b": _column_blocks(dw_b, "ff_b_grad_blocks"),
             "w_ff_down": _row_blocks(dw_down), "w_ple_gate": dw_g, "w_ple_in": dw_p, "w_o": dw_o}
    (du, dz, dws, dbs, vec_z), got = _gmlp_bwd(d_gm, u_pre, z_pre, sm["ln_z_g"], sm["ln_z_b"], sm["w_s"], bs_exp,
                                               ex.to_sibling(early))
    chip_sums = ex.reduce_on_chip(got)
    (dq, dk, dv), got_early = _attention_bwd(q, k, v, lse, d_attn, delta, tabs, ex.between_chips(chip_sums))
    parts = (dq, dk, dv, du, dz)
    small = {"tail": vec_tail, "ln1": vec_ln1, "ln_z": vec_z, "conv": dconv, "w_s": dws, "b_s": dbs, "loss": loss}
    (grad_x,), got_late = _proj_in_bwd_x(dy1, parts, w_in, ex.last(_proj_in_bwd_w(xb, parts), small))
    return grad_x, ex.collect(got_early, got_late)


def _mesh_pos():
    return lax.axis_index("x"), lax.axis_index("y"), lax.axis_index("c")


def _cast_shards(shards):
    n = len(shards)

    def body(*refs):
        for a in range(n):
            refs[n + a][...] = refs[a][...].astype(BF16)

    return pl.pallas_call(
        body, name="cast_shards", out_shape=tuple(jax.ShapeDtypeStruct(s.shape, BF16) for s in shards),
        compiler_params=_params(None),
    )(*shards)


class _GatherComm:
    def __init__(self, shards):
        n = len(shards)
        self.inputs = list(shards)
        self.out_shapes = [jax.ShapeDtypeStruct((N_DEV,) + s.shape, s.dtype) for s in shards]
        self.scratch = [pltpu.SemaphoreType.DMA((7 * n,)), pltpu.SemaphoreType.DMA((7 * n,)),
                        pltpu.SemaphoreType.DMA((n,))]

    def phases(self, x_refs, out_refs, sems):
        send_sems, recv_sems, local_sems = sems
        n_arr = len(x_refs)

        def where():
            x, y, c = _mesh_pos()
            return (x, y, c), (x, y, 1 - c), [(1 - x, y), (x, 1 - y), (1 - x, 1 - y)]

        def copy(a, n, block, to, from_shard=False):
            dst = out_refs[a].at[4 * block[0] + 2 * block[1] + block[2]]
            return pltpu.make_async_remote_copy(
                src_ref=x_refs[a] if from_shard else dst, dst_ref=dst, send_sem=send_sems.at[7 * a + n],
                recv_sem=recv_sems.at[7 * a + n], device_id=to, device_id_type=MESH)

        def local(a):
            x, y, c = _mesh_pos()
            return pltpu.make_async_copy(x_refs[a], out_refs[a].at[4 * x + 2 * y + c], local_sems.at[a])

        def start():
            me, sibling, chips = where()
            for a in range(n_arr):
                local(a).start()
                copy(a, 0, me, sibling, from_shard=True).start()
                for n, chip in enumerate(chips):
                    copy(a, 1 + n, me, (*chip, me[2]), from_shard=True).start()

        def forward():
            me, sibling, chips = where()
            for n, chip in enumerate(chips):
                for a in range(n_arr):
                    copy(a, 1 + n, (*chip, me[2]), me).wait_recv()
                    copy(a, 4 + n, (*chip, me[2]), sibling).start()

        def finish():
            me, sibling, chips = where()
            for a in range(n_arr):
                copy(a, 0, sibling, me).wait_recv()
                copy(a, 0, me, sibling, from_shard=True).wait_send()
                for n, chip in enumerate(chips):
                    copy(a, 4 + n, (*chip, 1 - me[2]), me).wait_recv()
                    copy(a, 1 + n, me, (*chip, me[2]), from_shard=True).wait_send()
                    copy(a, 4 + n, (*chip, me[2]), sibling).wait_send()
                local(a).wait()

        return {"start": start, "forward": forward, "finish": finish}


class _SiblingComm:
    def __init__(self, big):
        n = len(big)
        self.inputs = list(big)
        self.out_shapes = [jax.ShapeDtypeStruct(b.shape[1:], b.dtype) for b in big]
        self.scratch = [pltpu.SemaphoreType.DMA((n,)), pltpu.SemaphoreType.DMA((n,))]

    def phases(self, src, dst, sems):
        send_sems, recv_sems = sems

        def copies():
            x, y, c = _mesh_pos()
            return [pltpu.make_async_remote_copy(
                src_ref=src[a].at[1 - c], dst_ref=dst[a], send_sem=send_sems.at[a], recv_sem=recv_sems.at[a],
                device_id=(x, y, 1 - c), device_id_type=MESH) for a in range(len(src))]

        def start():
            for cp in copies():
                cp.start()

        def finish():
            for cp in copies():
                cp.wait()

        return {"start": start, "finish": finish}


class _ChipComm:
    def __init__(self, sums):
        n = len(sums)
        self.inputs = list(sums)
        self.out_shapes = [jax.ShapeDtypeStruct(s.shape, s.dtype) for s in sums]
        self.scratch = [pltpu.SemaphoreType.DMA((3 * n,)), pltpu.SemaphoreType.DMA((3 * n,)),
                        pltpu.SemaphoreType.DMA((n,))]

    def phases(self, src, dst, sems):
        send_sems, recv_sems, local_sems = sems

        def copies():
            x, y, c = _mesh_pos()
            my_chip = 2 * x + y
            out = [pltpu.make_async_copy(src[a].at[my_chip], dst[a].at[my_chip], local_sems.at[a])
                   for a in range(len(src))]
            for n, (px, py) in enumerate([(1 - x, y), (x, 1 - y), (1 - x, 1 - y)]):
                for a in range(len(src)):
                    out.append(pltpu.make_async_remote_copy(
                        src_ref=src[a].at[2 * px + py], dst_ref=dst[a].at[my_chip],
                        send_sem=send_sems.at[3 * a + n], recv_sem=recv_sems.at[3 * a + n],
                        device_id=(px, py, c), device_id_type=MESH))
            return out

        def start():
            for cp in copies():
                cp.start()

        def finish():
            for cp in copies():
                cp.wait()

        return {"start": start, "finish": finish}


class _ScatterComm:
    def __init__(self, blocks, small):
        self.n_big, self.n_small = len(blocks), len(small)
        n = self.n_big + self.n_small
        self.inputs = list(blocks) + list(small)
        self.out_shapes = ([jax.ShapeDtypeStruct(b.shape, b.dtype) for b in blocks]
                           + [jax.ShapeDtypeStruct((N_DEV,) + s.shape, s.dtype) for s in small])
        self.scratch = [pltpu.SemaphoreType.DMA((7 * n,)), pltpu.SemaphoreType.DMA((7 * n,)),
                        pltpu.SemaphoreType.DMA((n,))]

    def phases(self, src, dst, sems):
        send_sems, recv_sems, local_sems = sems
        n_big, n_all = self.n_big, self.n_big + self.n_small

        def source(a, core, chip):
            return src[a].at[core * 4 + chip] if a < n_big else src[a]

        def copies():
            x, y, c = _mesh_pos()
            me = 4 * x + 2 * y + c
            out = [pltpu.make_async_copy(source(a, c, 2 * x + y), dst[a].at[me], local_sems.at[a])
                   for a in range(n_all)]
            for flip in range(1, N_DEV):
                px = 1 - x if flip & 4 else x
                py = 1 - y if flip & 2 else y
                pc = 1 - c if flip & 1 else c
                for a in range(n_all):
                    n = 7 * a + flip - 1
                    out.append(pltpu.make_async_remote_copy(
                        src_ref=source(a, pc, 2 * px + py), dst_ref=dst[a].at[me], send_sem=send_sems.at[n],
                        recv_sem=recv_sems.at[n], device_id=(px, py, pc), device_id_type=MESH))
            return out

        def start():
            for cp in copies():
                cp.start()

        def finish():
            for cp in copies():
                cp.wait()

        return {"start": start, "finish": finish}


class _Both:
    def __init__(self, first, second):
        self.parts = (first, second)
        self.inputs = first.inputs + second.inputs
        self.out_shapes = first.out_shapes + second.out_shapes
        self.scratch = first.scratch + second.scratch

    def phases(self, src, dst, sems):
        a, b = self.parts
        pa = a.phases(src[:len(a.inputs)], dst[:len(a.out_shapes)], sems[:len(a.scratch)])
        pb = b.phases(src[len(a.inputs):], dst[len(a.out_shapes):], sems[len(a.scratch):])

        def both(name):
            def run():
                pa[name]()
                pb[name]()
            return run

        return {name: both(name) for name in pa}


def _host_call(body, carried, *, name, grid, out_shape, in_specs, out_specs, scratch_shapes, params, args):
    if carried is None:
        return pl.pallas_call(body, name=name, grid=grid, out_shape=tuple(out_shape), in_specs=list(in_specs),
                              out_specs=tuple(out_specs), scratch_shapes=list(scratch_shapes),
                              compiler_params=params)(*args), ()
    comm, when = carried
    n_in, n_out, n_scratch = len(in_specs), len(out_shape), len(scratch_shapes)
    k_in, k_out = len(comm.inputs), len(comm.out_shapes)

    def wrapped(*refs):
        bounds = np.cumsum([0, n_in, k_in, n_out, k_out, n_scratch])
        ins, c_in, outs, c_out, scr = (refs[bounds[i]:bounds[i + 1]] for i in range(5))
        phases = comm.phases(c_in, c_out, refs[bounds[5]:])
        for phase, cond in when("before"):
            pl.when(cond)(phases[phase])
        body(*ins, *outs, *scr)
        for phase, cond in when("after"):
            pl.when(cond)(phases[phase])

    anywhere = pl.BlockSpec(memory_space=pl.ANY)
    results = pl.pallas_call(
        wrapped, name=name, grid=grid, out_shape=tuple(out_shape) + tuple(comm.out_shapes),
        in_specs=list(in_specs) + [anywhere] * k_in, out_specs=tuple(out_specs) + (anywhere,) * k_out,
        scratch_shapes=list(scratch_shapes) + comm.scratch, compiler_params=params,
    )(*args, *comm.inputs)
    return results[:n_out], results[n_out:]


class _Exchanges:
    FIRST = ("w_o", "w_ff_a", "w_ff_b")
    SECOND = ("w_ff_down", "w_ple_gate", "w_ple_in")
    EARLY = ("w_ff_a", "w_ff_b", "w_ff_down", "w_ple_gate", "w_ple_in", "w_o")
    LATE = ("w_in",)

    def __init__(self, shards, conv_rows):
        self.shards, self.conv_rows = shards, conv_rows
        self.axis = {name: axis for name, _, axis in BIG}

    def _natural(self, name, blocks):
        n, r, c = blocks.shape
        return blocks.reshape(n * r, c) if self.axis[name] == 0 else _to_natural(blocks, name + "_natural")

    def gather_first(self):
        comm = _GatherComm([self.shards[n] for n in self.FIRST] + [self.conv_rows])

        def when(position):
            step = pl.program_id(0)
            if position == "before":
                return [("start", step == 0), ("forward", step == 3)]
            return [("finish", step == 3)]
        return comm, when

    def weights_first(self, got):
        out = {name: self._natural(name, blocks) for name, blocks in zip(self.FIRST, got)}
        out["conv_w8"] = _to_natural(got[-1], "conv_w_natural")
        return out

    def gather_second(self):
        comm = _GatherComm([self.shards[n] for n in self.SECOND])

        def when(position):
            j, i = pl.program_id(0), pl.program_id(1)
            if position == "before":
                return [("start", (j == 0) & (i == 0)), ("forward", (j == NJ - 3) & (i == 0))]
            return [("finish", (j == NJ - 1) & (i == NT - 1))]
        return comm, when

    def weights_second(self, got):
        return {name: self._natural(name, blocks) for name, blocks in zip(self.SECOND, got)}

    def to_sibling(self, early):
        self.by_core = [early[n].reshape((2, 4) + early[n].shape[1:]) for n in self.EARLY]
        return _SiblingComm(self.by_core), _first_and_last(NT)

    def reduce_on_chip(self, from_sibling):
        core = lax.axis_index("c").astype(jnp.int32).reshape(1)
        return _chip_reduce(self.by_core, from_sibling, core, "chip_reduce")

    def between_chips(self, chip_sums):
        return _ChipComm(chip_sums), _first_and_last(4)

    def last(self, dw_in, small):
        by_core = [dw_in.reshape((2, 4) + dw_in.shape[1:])]
        core = lax.axis_index("c").astype(jnp.int32).reshape(1)
        sums = _chip_reduce(by_core, _standalone(_SiblingComm(by_core), "w_in_grad_to_sibling"), core, "w_in_chip_reduce")
        self.small_keys = tuple(small)
        return _Both(_ChipComm(sums), _ScatterComm([], [small[k] for k in self.small_keys])), _first_and_last(NT)

    def collect(self, got_early, got_late):
        parts = dict(zip(self.EARLY, got_early))
        parts.update(zip(self.LATE, got_late[:len(self.LATE)]))
        return parts, dict(zip(self.small_keys, got_late[len(self.LATE):]))


def _first_and_last(n_steps):
    def when(position):
        step = pl.program_id(0)
        return [("start", step == 0)] if position == "before" else [("finish", step == n_steps - 1)]
    return when


def _all_gather(shards, name):
    comm = _GatherComm(shards)

    def body(*refs):
        n = len(shards)
        phases = comm.phases(refs[:n], refs[n:2 * n], refs[2 * n:])
        phases["start"]()
        phases["forward"]()
        phases["finish"]()

    anywhere = pl.BlockSpec(memory_space=pl.ANY)
    return pl.pallas_call(
        body, name=name, out_shape=tuple(comm.out_shapes), in_specs=[anywhere] * len(shards),
        out_specs=(anywhere,) * len(shards), scratch_shapes=comm.scratch,
    )(*shards)


def _standalone(comm, name):
    n_in = len(comm.inputs)

    def body(*refs):
        phases = comm.phases(refs[:n_in], refs[n_in:n_in + len(comm.out_shapes)], refs[n_in + len(comm.out_shapes):])
        phases["start"]()
        phases["finish"]()

    anywhere = pl.BlockSpec(memory_space=pl.ANY)
    return pl.pallas_call(
        body, name=name, out_shape=tuple(comm.out_shapes), in_specs=[anywhere] * n_in,
        out_specs=(anywhere,) * len(comm.out_shapes), scratch_shapes=comm.scratch,
    )(*comm.inputs)


def _chip_reduce(big, from_sibling, core, name):
    n = len(big)

    def body(core_ref, *refs):
        for a in range(n):
            mine, theirs, out = refs[a], refs[n + a], refs[2 * n + a]
            out[0] = (mine[0, 0].astype(F32) + theirs[0].astype(F32)).astype(BF16)

    def block(shape):
        return pl.BlockSpec((1,) + shape, lambda ch, core_ref: (ch, 0, 0))

    grid_spec = pltpu.PrefetchScalarGridSpec(
        num_scalar_prefetch=1, grid=(4,),
        in_specs=[pl.BlockSpec((1, 1) + b.shape[2:], lambda ch, core_ref: (core_ref[0], ch, 0, 0)) for b in big]
        + [block(b.shape[2:]) for b in big],
        out_specs=[block(b.shape[2:]) for b in big])
    return pl.pallas_call(
        body, name=name, grid_spec=grid_spec,
        out_shape=tuple(jax.ShapeDtypeStruct(b.shape[1:], BF16) for b in big),
        compiler_params=_params(("parallel",)),
    )(core, *big, *from_sibling)


def _adamw(g, w, m, v):
    nm = ADAM_B1 * m + (1.0 - ADAM_B1) * g
    nv = ADAM_B2 * v + (1.0 - ADAM_B2) * (g * g)
    m_hat = nm / (1.0 - ADAM_B1 ** ADAM_STEP)
    v_hat = nv / (1.0 - ADAM_B2 ** ADAM_STEP)
    return -ADAM_LR * (m_hat / (jnp.sqrt(v_hat) + ADAM_EPS) + ADAM_WD * w), nm, nv


def _adamw_sharded(parts, w, m, v, name):
    def body(p_ref, w_ref, m_ref, v_ref, g_ref, d_ref, nm_ref, nv_ref):
        g = p_ref[0].astype(F32)
        for s in range(1, parts.shape[0]):
            g = g + p_ref[s].astype(F32)
        delta, nm, nv = _adamw(g, w_ref[0], m_ref[0], v_ref[0])
        g_ref[0] = g
        d_ref[0] = delta
        nm_ref[0] = nm
        nv_ref[0] = nv

    return pl.pallas_call(
        body, name=name, out_shape=(jax.ShapeDtypeStruct(w.shape, F32),) * 4, compiler_params=_params(None),
    )(parts, w, m, v)


REPLICATED = (("ln_z_g", "ln_z", 0), ("ln_z_b", "ln_z", 1), ("w_s", "w_s", None), ("b_s", "b_s", None),
              ("ln1_g", "ln1", 0), ("ln1_b", "ln1", 1), ("conv_w", "conv_mine", None), ("conv_b", "conv", 3),
              ("ln2_g", "tail", 3), ("ln2_b", "tail", 4), ("b_ple_gate", "tail", 0), ("ln3_g", "tail", 1),
              ("ln3_b", "tail", 2))
GATHERED = ("tail", "ln1", "ln_z", "conv", "w_s", "b_s", "loss", "conv_mine")


def _adamw_replicated(gathered, w, m, v):
    n_par = len(REPLICATED)

    def body(*refs):
        srcs = dict(zip(GATHERED, refs[:len(GATHERED)]))
        rest = refs[len(GATHERED):]
        w_refs, m_refs, v_refs = rest[:n_par], rest[n_par:2 * n_par], rest[2 * n_par:3 * n_par]
        outs = rest[3 * n_par:]
        loss_ref = outs[4 * n_par]
        sums = {}
        for key, ref in srcs.items():
            total = ref[0]
            for dev in range(1, N_DEV):
                total = total + ref[dev]
            sums[key] = total
        loss_ref[...] = sums["loss"]
        for n, (name, key, row) in enumerate(REPLICATED):
            if name == "conv_w":
                g = sums[key][0:3, :]
            elif row is None:
                g = sums[key]
            else:
                g = sums[key][row:row + 1, :]
            lead = len(w_refs[n].shape) - g.ndim
            idx = (0,) * lead + (Ellipsis,)
            delta, nm, nv = _adamw(g, w_refs[n][idx], m_refs[n][idx], v_refs[n][idx])
            for kind, val in enumerate((g, delta, nm, nv)):
                outs[kind * n_par + n][idx] = val

    names = [name for name, _, _ in REPLICATED]
    shapes = [jax.ShapeDtypeStruct(w[name].shape, F32) for name in names]
    return pl.pallas_call(
        body, name="adamw_replicated", out_shape=tuple(shapes * 4) + (jax.ShapeDtypeStruct((8, LANES), F32),),
        compiler_params=_params(None),
    )(*[gathered[k] for k in GATHERED], *[w[n] for n in names], *[m[n] for n in names], *[v[n] for n in names])


def kernel(x, p, positions, w_in, ln_z_g, ln_z_b, w_s, b_s, w_o, ln1_g, ln1_b, w_ff_a, w_ff_b, conv_w, conv_b, w_ff_down, ln2_g, ln2_b, w_ple_gate, b_ple_gate, w_ple_in, ln3_g, ln3_b, loss_target, m_w_in, m_ln_z_g, m_ln_z_b, m_w_s, m_b_s, m_w_o, m_ln1_g, m_ln1_b, m_w_ff_a, m_w_ff_b, m_conv_w, m_conv_b, m_w_ff_down, m_ln2_g, m_ln2_b, m_w_ple_gate, m_b_ple_gate, m_w_ple_in, m_ln3_g, m_ln3_b, v_w_in, v_ln_z_g, v_ln_z_b, v_w_s, v_b_s, v_w_o, v_ln1_g, v_ln1_b, v_w_ff_a, v_w_ff_b, v_conv_w, v_conv_b, v_w_ff_down, v_ln2_g, v_ln2_b, v_w_ple_gate, v_b_ple_gate, v_w_ple_in, v_ln3_g, v_ln3_b):
    w = dict(w_in=w_in, ln_z_g=ln_z_g, ln_z_b=ln_z_b, w_s=w_s, b_s=b_s, w_o=w_o, ln1_g=ln1_g, ln1_b=ln1_b,
             w_ff_a=w_ff_a, w_ff_b=w_ff_b, conv_w=conv_w, conv_b=conv_b, w_ff_down=w_ff_down, ln2_g=ln2_g,
             ln2_b=ln2_b, w_ple_gate=w_ple_gate, b_ple_gate=b_ple_gate, w_ple_in=w_ple_in, ln3_g=ln3_g,
             ln3_b=ln3_b)
    m = dict(w_in=m_w_in, ln_z_g=m_ln_z_g, ln_z_b=m_ln_z_b, w_s=m_w_s, b_s=m_b_s, w_o=m_w_o, ln1_g=m_ln1_g,
             ln1_b=m_ln1_b, w_ff_a=m_w_ff_a, w_ff_b=m_w_ff_b, conv_w=m_conv_w, conv_b=m_conv_b,
             w_ff_down=m_w_ff_down, ln2_g=m_ln2_g, ln2_b=m_ln2_b, w_ple_gate=m_w_ple_gate,
             b_ple_gate=m_b_ple_gate, w_ple_in=m_w_ple_in, ln3_g=m_ln3_g, ln3_b=m_ln3_b)
    v = dict(w_in=v_w_in, ln_z_g=v_ln_z_g, ln_z_b=v_ln_z_b, w_s=v_w_s, b_s=v_b_s, w_o=v_w_o, ln1_g=v_ln1_g,
             ln1_b=v_ln1_b, w_ff_a=v_w_ff_a, w_ff_b=v_w_ff_b, conv_w=v_conv_w, conv_b=v_conv_b,
             w_ff_down=v_w_ff_down, ln2_g=v_ln2_g, ln2_b=v_ln2_b, w_ple_gate=v_w_ple_gate,
             b_ple_gate=v_b_ple_gate, w_ple_in=v_w_ple_in, ln3_g=v_ln3_g, ln3_b=v_ln3_b)
    big_names = [name for name, _, _ in BIG]
    small_names = ("ln_z_g", "ln_z_b", "w_s", "b_s", "ln1_g", "ln1_b", "conv_b", "ln2_g", "ln2_b", "b_ple_gate",
                   "ln3_g", "ln3_b")

    shards = dict(zip(big_names, _cast_shards([w[n][0] for n in big_names])))
    conv_rows = jnp.pad(w["conv_w"][0], ((0, 5), (0, 0)))
    w_in_full = _to_natural(_all_gather([shards["w_in"]], "all_gather_w_in")[0], "w_in_natural")
    sm = {n: w[n][0] if w[n].ndim > 2 else w[n] for n in small_names}
    pos_col = positions.reshape(S, 1).astype(F32)
    grad_x, (parts, small_all) = _local_step(x[0], p[0, 0], pos_col, loss_target[0], w_in_full, sm,
                                             _Exchanges(shards, conv_rows))
    me = 4 * lax.axis_index("x") + 2 * lax.axis_index("y") + lax.axis_index("c")
    conv_cols = small_all["conv"].reshape(N_DEV, 8, N_DEV, D_FF // N_DEV)
    small_all["conv_mine"] = lax.dynamic_index_in_dim(conv_cols, me, axis=2, keepdims=False)

    leaves = {}
    for name in big_names:
        leaves[name] = _adamw_sharded(parts[name], w[name], m[name], v[name], "adamw_" + name)
    rep = _adamw_replicated(small_all, w, m, v)
    n_rep = len(REPLICATED)
    for n, (name, _, _) in enumerate(REPLICATED):
        leaves[name] = tuple(rep[kind * n_rep + n] for kind in range(4))
    loss = rep[4 * n_rep][0, 0]
    return (loss, grad_x[None], *[leaves[n][kind] for kind in range(4) for n in WEIGHT_ORDER])
```

```python
import math

import numpy as np
import jax
import jax.numpy as jnp
from jax import lax
from jax.experimental import pallas as pl
from jax.experimental.pallas import tpu as pltpu

F32 = jnp.float32
BF16 = jnp.bfloat16
MESH = pl.DeviceIdType.MESH

N_DEV = 8
S = 4096
D = 1024
D_HALF = 512
D_IN = 2560
D_FF = 2816
D_PLE = 256
CHUNK = 128
DILATIONS = ((1, 32), (4, 8), (16, 2))
ROPE_THETA = 500000.0
LN_EPS = 1e-5
ALPHA = 2.0 ** 0.25
NEG_INF = -1e30
INV_SQRT2 = 1.0 / math.sqrt(2.0)
INV_SQRT_2PI = 1.0 / math.sqrt(2.0 * math.pi)

ADAM_LR, ADAM_B1, ADAM_B2, ADAM_EPS, ADAM_WD, ADAM_STEP = 0.001, 0.9, 0.999, 1e-08, 0.01, 10

TM = 512
NT = S // TM
ATTN_UNROLL = 8
TN = 256
NJ = D_FF // TN
LANES = 128
VMEM_MIB = 1024 * 1024

BIG = (("w_in", (1024, 320), 1), ("w_o", (128, 1024), 0), ("w_ff_a", (1024, 352), 1),
       ("w_ff_b", (1024, 352), 1), ("w_ff_down", (352, 1024), 0), ("w_ple_gate", (128, 1024), 0),
       ("w_ple_in", (256, 128), 1))
WEIGHT_ORDER = ("w_in", "ln_z_g", "ln_z_b", "w_s", "b_s", "w_o", "ln1_g", "ln1_b", "w_ff_a", "w_ff_b",
                "conv_w", "conv_b", "w_ff_down", "ln2_g", "ln2_b", "w_ple_gate", "b_ple_gate",
                "w_ple_in", "ln3_g", "ln3_b")


def _params(semantics=None, vmem_mib=48):
    return pltpu.CompilerParams(dimension_semantics=semantics, vmem_limit_bytes=vmem_mib * VMEM_MIB)


def _dot(a, b):
    return jnp.dot(a, b, preferred_element_type=F32)


def _dot_nt(a, b):
    return lax.dot_general(a, b, (((1,), (1,)), ((), ())), preferred_element_type=F32)


def _dot_tn(a, b):
    return lax.dot_general(a, b, (((0,), (0,)), ((), ())), preferred_element_type=F32)


def _gelu(x):
    return 0.5 * x * (1.0 + lax.erf(x * INV_SQRT2))


def _gelu_grad(x):
    return 0.5 * (1.0 + lax.erf(x * INV_SQRT2)) + x * (jnp.exp(-0.5 * x * x) * INV_SQRT_2PI)


def _ln_stats(y):
    mu = jnp.mean(y, axis=-1, keepdims=True)
    yc = y - mu
    var = jnp.mean(yc * yc, axis=-1, keepdims=True)
    rstd = lax.rsqrt(var + LN_EPS)
    return yc * rstd, rstd


def _ln_bwd(dxhat, xhat, rstd):
    m1 = jnp.mean(dxhat, axis=-1, keepdims=True)
    m2 = jnp.mean(dxhat * xhat, axis=-1, keepdims=True)
    return rstd * (dxhat - m1 - xhat * m2)


def _colsum(x):
    return jnp.sum(x, axis=0, keepdims=True)


def _rows(i):
    return (i, 0)


def _fixed(*_):
    return (0, 0)


def _row_spec(width):
    return pl.BlockSpec((TM, width), _rows)


def _full_spec(shape):
    return pl.BlockSpec(shape, lambda *_: (0,) * len(shape))


def _owner_slot(j):
    return (j % 2) * 4 + j // 2


def _lane_lo():
    return lax.broadcasted_iota(jnp.int32, (CHUNK, LANES), 1) < 64


def _tril():
    r = lax.broadcasted_iota(jnp.int32, (CHUNK, CHUNK), 0)
    c = lax.broadcasted_iota(jnp.int32, (CHUNK, CHUNK), 1)
    return c <= r


def _rope_consts():
    lane = np.arange(LANES) % 64
    j = lane % 8
    inv = np.where(lane < 16, np.float32(ROPE_THETA) ** (-(2.0 * j).astype(np.float32) / np.float32(16.0)), 0.0)
    m_lo = (lane < 8).astype(np.float32)
    m_hi = ((lane >= 8) & (lane < 16)).astype(np.float32)
    return (jnp.asarray(inv, F32).reshape(1, LANES), jnp.asarray(m_lo).reshape(1, LANES),
            jnp.asarray(m_hi).reshape(1, LANES))


def _rope_tables(pos_col):
    inv, m_lo, m_hi = _rope_consts()

    def body(pos_ref, inv_ref, lo_ref, hi_ref, c_ref, sa_ref, sb_ref):
        ang = pos_ref[...] * inv_ref[...]
        c = jnp.cos(ang)
        s = jnp.sin(ang)
        lo = lo_ref[...]
        hi = hi_ref[...]
        c_ref[...] = jnp.where(lo + hi > 0.0, c, 1.0)
        sa_ref[...] = s * hi
        sb_ref[...] = -s * lo

    vec = _full_spec((1, LANES))
    out = jax.ShapeDtypeStruct((S, LANES), F32)
    return pl.pallas_call(
        body, name="rope_tables", grid=(NT,), out_shape=(out, out, out),
        in_specs=[pl.BlockSpec((TM, 1), _rows), vec, vec, vec],
        out_specs=(_row_spec(LANES),) * 3, compiler_params=_params(("parallel",)),
    )(pos_col, inv, m_lo, m_hi)


def _rope(t, c, sa, sb):
    return t * c + pltpu.roll(t, 8, 1) * sa + pltpu.roll(t, LANES - 8, 1) * sb


def _rope_t(dy, c, sa, sb):
    return dy * c + pltpu.roll(dy * sa, LANES - 8, 1) + pltpu.roll(dy * sb, 8, 1)


def _masked_ws(ws_ref):
    tril = _tril()
    return [jnp.where(tril, ws_ref[g], 0.0).astype(BF16) for g in range(8)]


def _spatial_mix(zn, wm, bs):
    lo = _lane_lo()
    rows = []
    for ch in range(TM // CHUNK):
        slabs = []
        for pr in range(4):
            zp = zn[ch * CHUNK:(ch + 1) * CHUNK, pr * LANES:(pr + 1) * LANES].astype(BF16)
            slabs.append(jnp.where(lo, _dot(wm[2 * pr], zp), _dot(wm[2 * pr + 1], zp)))
        rows.append(jnp.concatenate(slabs, axis=1) + bs)
    return jnp.concatenate(rows, axis=0)


def _proj_in_fwd(x, w_in, tabs, ln_z_g, ln_z_b, w_s, bs_exp):
    def body(x_ref, w_ref, c_ref, sa_ref, sb_ref, g_ref, b_ref, ws_ref, bs_ref,
             q_ref, k_ref, v_ref, u_ref, z_ref, gm_ref, xb_ref):
        xb = x_ref[...].astype(BF16)
        xb_ref[...] = xb
        c, sa, sb = c_ref[...], sa_ref[...], sb_ref[...]
        hq = _dot(xb, w_ref[:, 0:512])
        hk = _dot(xb, w_ref[:, 512:1024])
        for s in range(4):
            sl = slice(s * LANES, (s + 1) * LANES)
            q_ref[:, sl] = _rope(hq[:, sl], c, sa, sb)
            k_ref[:, sl] = _rope(hk[:, sl], c, sa, sb)
        v_ref[...] = _dot(xb, w_ref[:, 1024:1536])
        u_pre = _dot(xb, w_ref[:, 1536:2048])
        z_pre = _dot(xb, w_ref[:, 2048:2560])
        u_ref[...] = u_pre
        z_ref[...] = z_pre
        zhat, _ = _ln_stats(_gelu(z_pre))
        zn = zhat * g_ref[...] + b_ref[...]
        mixed = _spatial_mix(zn, _masked_ws(ws_ref), bs_ref[...])
        gm_ref[...] = (_gelu(u_pre) * mixed).astype(BF16)

    half = jax.ShapeDtypeStruct((S, D_HALF), F32)
    tab = _row_spec(LANES)
    return pl.pallas_call(
        body, name="proj_in_fwd", grid=(NT,),
        out_shape=(half, half, half, half, half, jax.ShapeDtypeStruct((S, D_HALF), BF16),
                   jax.ShapeDtypeStruct((S, D), BF16)),
        in_specs=[_row_spec(D), _full_spec((D, D_IN)), tab, tab, tab, _full_spec((1, D_HALF)),
                  _full_spec((1, D_HALF)), _full_spec((8, CHUNK, CHUNK)), _full_spec((CHUNK, D_HALF))],
        out_specs=(_row_spec(D_HALF),) * 6 + (_row_spec(D),),
        compiler_params=_params(("parallel",)),
    )(x, w_in, *tabs, ln_z_g, ln_z_b, w_s, bs_exp)


def _store_band_bias(bias_ref):
    qi = lax.broadcasted_iota(jnp.int32, (CHUNK, 2 * CHUNK), 0)
    kj = lax.broadcasted_iota(jnp.int32, (CHUNK, 2 * CHUNK), 1)
    band = (kj >= qi) & (kj <= qi + CHUNK)
    bias_ref[0] = jnp.where(band, 0.0, NEG_INF)
    bias_ref[1] = jnp.where(band & (kj >= CHUNK), 0.0, NEG_INF)


def _permuted_rows(ref, d, r):
    return ref[...] if d == 1 else ref[pl.ds(r, S // d, stride=d), :]


def _attention_fwd(q, k, v, carried=None):
    def body(q_ref, k_ref, v_ref, o_ref, lse_ref, qb, kb, v0b, v1b, bias, op, lp, ob0, lb0, ob1, lb1, ob2, lb2):
        lo = _lane_lo()
        lo_f = lo.astype(F32)[0:1, :]
        hi_f = 1.0 - lo_f
        zero_pad = jnp.zeros((CHUNK, LANES), BF16)
        for buf in (qb, kb, v0b, v1b):
            buf[0:CHUNK, :] = zero_pad
        _store_band_bias(bias)
        outs = ((ob0, lb0), (ob1, lb1), (ob2, lb2))
        for (d, nb), (ob, lb) in zip(DILATIONS, outs):
            length = S // d
            for r in range(d):
                dst = slice(CHUNK + r * length, CHUNK + (r + 1) * length)
                qb[dst, :] = (_permuted_rows(q_ref, d, r) * 0.125).astype(BF16)
                kb[dst, :] = _permuted_rows(k_ref, d, r).astype(BF16)
                vs = _permuted_rows(v_ref, d, r)
                v0b[dst, :] = (vs * lo_f + hi_f).astype(BF16)
                v1b[dst, :] = (vs * hi_f + lo_f).astype(BF16)

            def block(b, carry, nb=nb):
                base = pl.multiple_of(b * CHUNK, CHUNK)
                add = bias[jnp.where(b % nb == 0, 1, 0)]
                qblk = qb[pl.ds(pl.multiple_of(base + CHUNK, CHUNK), CHUNK), :]
                kblk = kb[pl.ds(base, 2 * CHUNK), :]
                q2 = jnp.concatenate([jnp.where(lo, qblk, 0), jnp.where(lo, 0, qblk)], axis=0)
                s2 = _dot_nt(q2, kblk) + jnp.concatenate([add, add], axis=0)
                m2 = jnp.max(s2, axis=-1, keepdims=True)
                p2 = jnp.exp(s2 - m2).astype(BF16)
                pv, mx = [], []
                for head, vh in enumerate((v0b, v1b)):
                    rows = slice(head * CHUNK, (head + 1) * CHUNK)
                    pv.append(_dot(p2[rows, :], vh[pl.ds(base, 2 * CHUNK), :]))
                    mx.append(m2[rows, :])
                den = pltpu.roll(jnp.where(lo, pv[1], pv[0]), 64, 1)
                op[pl.ds(base, CHUNK), :] = jnp.where(lo, pv[0], pv[1]) / den
                lp[pl.ds(base, CHUNK), :] = jnp.where(lo, mx[0], mx[1]) + jnp.log(den)
                return carry

            lax.fori_loop(0, S // CHUNK, block, 0, unroll=ATTN_UNROLL)
            for r in range(d):
                src = slice(r * length, (r + 1) * length)
                if d == 1:
                    ob[...] = op[...]
                    lb[...] = lp[...]
                else:
                    ob[pl.ds(r, length, stride=d), :] = op[src, :]
                    lb[pl.ds(r, length, stride=d), :] = lp[src, :]
        for t in range(NT):
            rows = slice(t * TM, (t + 1) * TM)
            l0, l1, l2 = lb0[rows, :], lb1[rows, :], lb2[rows, :]
            mx = jnp.maximum(jnp.maximum(l0, l1), l2)
            e0, e1, e2 = jnp.exp(l0 - mx), jnp.exp(l1 - mx), jnp.exp(l2 - mx)
            den = e0 + e1 + e2
            o_ref[rows, :] = (e0 * ob0[rows, :] + e1 * ob1[rows, :] + e2 * ob2[rows, :]) / den
            lse_ref[rows, :] = mx + jnp.log(den)

    slab = pl.BlockSpec((S, LANES), lambda h: (0, h))
    out = jax.ShapeDtypeStruct((S, D_HALF), F32)
    padded = pltpu.VMEM((CHUNK + S, LANES), BF16)
    whole = pltpu.VMEM((S, LANES), F32)
    return _host_call(
        body, carried, name="attention_fwd", grid=(4,), out_shape=(out, out),
        in_specs=[slab, slab, slab], out_specs=(slab, slab),
        scratch_shapes=[padded] * 4 + [pltpu.VMEM((2, CHUNK, 2 * CHUNK), F32)] + [whole] * 8,
        params=_params(("arbitrary",), 56), args=(q, k, v))


def _mix_ln1_fwd(attn, gm, w_o, x, g1, b1):
    def body(a_ref, gm_ref, w_ref, x_ref, g_ref, b_ref, xhat_ref, rstd_ref, x1b_ref):
        mix = _dot(a_ref[...].astype(BF16), w_ref[0:D_HALF, :]) + _dot(gm_ref[...], w_ref[D_HALF:D, :])
        xhat, rstd = _ln_stats(ALPHA * x_ref[...] + mix)
        xhat_ref[...] = xhat
        rstd_ref[...] = rstd
        x1b_ref[...] = (xhat * g_ref[...] + b_ref[...]).astype(BF16)

    vec = _full_spec((1, D))
    return pl.pallas_call(
        body, name="mix_ln1_fwd", grid=(NT,),
        out_shape=(jax.ShapeDtypeStruct((S, D), F32), jax.ShapeDtypeStruct((S, 1), F32),
                   jax.ShapeDtypeStruct((S, D), BF16)),
        in_specs=[_row_spec(D_HALF), _row_spec(D_HALF), _full_spec((D, D)), _row_spec(D), vec, vec],
        out_specs=(_row_spec(D), pl.BlockSpec((TM, 1), _rows), _row_spec(D)),
        compiler_params=_params(("parallel",)),
    )(attn, gm, w_o, x, g1, b1)


def _ffn_up_fwd(x1b, w_a, w_b, conv_w8, conv_b, carried=None):
    def body(x_ref, wa_ref, wb_ref, cw_ref, cb_ref, ap_ref, a_ref, bl_ref, h_ref, carry):
        @pl.when(pl.program_id(1) == 0)
        def _():
            carry[...] = jnp.zeros_like(carry)

        xb = x_ref[...]
        ap = _dot(xb, wa_ref[...])
        bl = _dot(xb, wb_ref[...])
        row = lax.broadcasted_iota(jnp.int32, (TM, TN), 0)
        c6, c7 = carry[6:7, :], carry[7:8, :]
        m1 = jnp.where(row == 0, c7, pltpu.roll(ap, 1, 0))
        m2 = jnp.where(row == 0, c6, jnp.where(row == 1, c7, pltpu.roll(ap, 2, 0)))
        a = cb_ref[...] + cw_ref[0:1, :] * m2 + cw_ref[1:2, :] * m1 + cw_ref[2:3, :] * ap
        carry[...] = ap[TM - 8:TM, :]
        ap_ref[...] = ap
        a_ref[...] = a
        bl_ref[...] = bl
        h_ref[...] = (_gelu(a) * bl).astype(BF16)

    tile = pl.BlockSpec((TM, TN), lambda j, i: (i, j))
    wcol = pl.BlockSpec((D, TN), lambda j, i: (0, j))
    ff = jax.ShapeDtypeStruct((S, D_FF), F32)
    return _host_call(
        body, carried, name="ffn_up_fwd", grid=(NJ, NT),
        out_shape=(ff, ff, ff, jax.ShapeDtypeStruct((S, D_FF), BF16)),
        in_specs=[pl.BlockSpec((TM, D), lambda j, i: (i, 0)), wcol, wcol,
                  pl.BlockSpec((8, TN), lambda j, i: (0, j)), pl.BlockSpec((1, TN), lambda j, i: (0, j))],
        out_specs=(tile, tile, tile, tile),
        scratch_shapes=[pltpu.VMEM((8, TN), F32)],
        params=_params(("arbitrary", "arbitrary")), args=(x1b, w_a, w_b, conv_w8, conv_b))


def _ffn_down_ln2_fwd(hff, w_down, xhat1, g1, b1):
    def body(h_ref, w_ref, xh_ref, g_ref, b_ref, xhat_ref, rstd_ref):
        x1 = xh_ref[...] * g_ref[...] + b_ref[...]
        xhat, rstd = _ln_stats(ALPHA * x1 + _dot(h_ref[...], w_ref[...]))
        xhat_ref[...] = xhat
        rstd_ref[...] = rstd

    vec = _full_spec((1, D))
    return pl.pallas_call(
        body, name="ffn_down_ln2_fwd", grid=(NT,),
        out_shape=(jax.ShapeDtypeStruct((S, D), F32), jax.ShapeDtypeStruct((S, 1), F32)),
        in_specs=[_row_spec(D_FF), _full_spec((D_FF, D)), _row_spec(D), vec, vec],
        out_specs=(_row_spec(D), pl.BlockSpec((TM, 1), _rows)),
        compiler_params=_params(("parallel",)),
    )(hff, w_down, xhat1, g1, b1)


def _tail_fwd_bwd(xhat2, rstd2, p, target, w_g, w_p, g2, b2, bg, g3, b3):
    def body(xh_ref, rs_ref, p_ref, t_ref, wg_ref, wp_ref, g2_ref, b2_ref, bg_ref, g3_ref, b3_ref,
             loss_ref, dy2_ref, dy2b_ref, gwg_ref, gwp_ref, vec_ref, dwg_ref, dwp_ref):
        @pl.when(pl.program_id(0) == 0)
        def _():
            loss_ref[...] = jnp.zeros_like(loss_ref)
            dwg_ref[...] = jnp.zeros_like(dwg_ref)
            dwp_ref[...] = jnp.zeros_like(dwp_ref)
            vec_ref[...] = jnp.zeros_like(vec_ref)

        xhat2_t = xh_ref[...]
        x2 = xhat2_t * g2_ref[...] + b2_ref[...]
        x2b = x2.astype(BF16)
        pb = p_ref[...].astype(BF16)
        gate = jax.nn.sigmoid(_dot(x2b, wg_ref[...]) + bg_ref[...])
        pin = _dot(pb, wp_ref[...])
        xhat3, rstd3 = _ln_stats(ALPHA * x2 + gate * pin)
        err = xhat3 * g3_ref[...] + b3_ref[...] - t_ref[...]
        loss_ref[...] += jnp.sum(jnp.mean(err * err, axis=-1, keepdims=True), axis=0, keepdims=True) * 0.5
        dout = err * (1.0 / D)
        dy3 = _ln_bwd(dout * g3_ref[...], xhat3, rstd3)
        dgp = dy3 * pin * gate * (1.0 - gate)
        dgpb = dgp.astype(BF16)
        dwg_ref[...] += _dot_tn(x2b, dgpb)
        dwp_ref[...] += _dot_tn(pb, (dy3 * gate).astype(BF16))
        dx2 = ALPHA * dy3 + _dot_nt(dgpb, wg_ref[...])
        dy2 = _ln_bwd(dx2 * g2_ref[...], xhat2_t, rs_ref[...])
        dy2_ref[...] = dy2
        dy2b_ref[...] = dy2.astype(BF16)
        vec_ref[0:1, :] += _colsum(dgp)
        vec_ref[1:2, :] += _colsum(dout * xhat3)
        vec_ref[2:3, :] += _colsum(dout)
        vec_ref[3:4, :] += _colsum(dx2 * xhat2_t)
        vec_ref[4:5, :] += _colsum(dx2)

        @pl.when(pl.program_id(0) == NT - 1)
        def _():
            for j in range(N_DEV):
                gwg_ref[_owner_slot(j)] = dwg_ref[LANES * j:LANES * (j + 1), :].astype(BF16)
                gwp_ref[_owner_slot(j)] = dwp_ref[:, LANES * j:LANES * (j + 1)].astype(BF16)

    vec = _full_spec((1, D))
    return pl.pallas_call(
        body, name="tail_fwd_bwd", grid=(NT,),
        out_shape=(jax.ShapeDtypeStruct((8, LANES), F32), jax.ShapeDtypeStruct((S, D), F32),
                   jax.ShapeDtypeStruct((S, D), BF16), jax.ShapeDtypeStruct((N_DEV, D // N_DEV, D), BF16),
                   jax.ShapeDtypeStruct((N_DEV, D_PLE, D // N_DEV), BF16), jax.ShapeDtypeStruct((8, D), F32)),
        in_specs=[_row_spec(D), pl.BlockSpec((TM, 1), _rows), _row_spec(D_PLE), _row_spec(D),
                  _full_spec((D, D)), _full_spec((D_PLE, D)), vec, vec, vec, vec, vec],
        out_specs=(_full_spec((8, LANES)), _row_spec(D), _row_spec(D), _full_spec((N_DEV, D // N_DEV, D)),
                   _full_spec((N_DEV, D_PLE, D // N_DEV)), _full_spec((8, D))),
        scratch_shapes=[pltpu.VMEM((D, D), F32), pltpu.VMEM((D_PLE, D), F32)],
        compiler_params=_params(("arbitrary",)),
    )(xhat2, rstd2, p, target, w_g, w_p, g2, b2, bg, g3, b3)


def _ffn_bwd(dy2b, x1b, w_down, w_a, w_b, a_pre, a, b_lin, hff, conv_w8):
    def body(dy_ref, x_ref, wd_ref, wa_ref, wb_ref, ap_ref, a_ref, bl_ref, h_ref, cw_ref,
             dwd_ref, dwa_ref, dwb_ref, dcw_ref, dx_hbm, dx_acc, carry, sem):
        j, i = pl.program_id(0), pl.program_id(1)

        @pl.when(i == 0)
        def _():
            carry[...] = jnp.zeros_like(carry)
            dwd_ref[...] = jnp.zeros_like(dwd_ref)
            dwa_ref[...] = jnp.zeros_like(dwa_ref)
            dwb_ref[...] = jnp.zeros_like(dwb_ref)
            dcw_ref[...] = jnp.zeros_like(dcw_ref)

        dyb = dy_ref[...]
        xb = x_ref[...]
        dh = _dot_nt(dyb, wd_ref[...])
        av = a_ref[...]
        dbl = dh * _gelu(av)
        da = dh * bl_ref[...] * _gelu_grad(av)
        row = lax.broadcasted_iota(jnp.int32, (TM, TN), 0)
        c0, c1 = carry[0:1, :], carry[1:2, :]
        p1 = jnp.where(row == TM - 1, c0, pltpu.roll(da, TM - 1, 0))
        p2 = jnp.where(row == TM - 2, c0, jnp.where(row == TM - 1, c1, pltpu.roll(da, TM - 2, 0)))
        carry[...] = da[0:8, :]
        ap = ap_ref[...]
        dcw_ref[3:4, :] += _colsum(da)
        dcw_ref[0:1, :] += _colsum(ap * p2)
        dcw_ref[1:2, :] += _colsum(ap * p1)
        dcw_ref[2:3, :] += _colsum(ap * da)
        dap = (cw_ref[2:3, :] * da + cw_ref[1:2, :] * p1 + cw_ref[0:1, :] * p2).astype(BF16)
        dblb = dbl.astype(BF16)
        dwa_ref[...] += _dot_tn(xb, dap)
        dwb_ref[...] += _dot_tn(xb, dblb)
        dwd_ref[...] += _dot_tn(h_ref[...], dyb)
        dx = _dot_nt(dap, wa_ref[...]) + _dot_nt(dblb, wb_ref[...])
        rows = pl.ds(pl.multiple_of((NT - 1 - i) * TM, TM), TM)

        @pl.when(j == 0)
        def _():
            dx_acc[rows, :] = dx

        @pl.when(j > 0)
        def _():
            dx_acc[rows, :] += dx

        @pl.when((j == NJ - 1) & (i == NT - 1))
        def _():
            cp = pltpu.make_async_copy(dx_acc, dx_hbm, sem)
            cp.start()
            cp.wait()

    rev_rows = lambda j, i: (NT - 1 - i, 0)
    rev_tile = pl.BlockSpec((TM, TN), lambda j, i: (NT - 1 - i, j))
    wcol = pl.BlockSpec((D, TN), lambda j, i: (0, j))
    small = pl.BlockSpec((8, TN), lambda j, i: (0, j))
    return pl.pallas_call(
        body, name="ffn_bwd", grid=(NJ, NT),
        out_shape=(jax.ShapeDtypeStruct((D_FF, D), F32), jax.ShapeDtypeStruct((D, D_FF), F32),
                   jax.ShapeDtypeStruct((D, D_FF), F32), jax.ShapeDtypeStruct((8, D_FF), F32),
                   jax.ShapeDtypeStruct((S, D), F32)),
        in_specs=[pl.BlockSpec((TM, D), rev_rows), pl.BlockSpec((TM, D), rev_rows),
                  pl.BlockSpec((TN, D), lambda j, i: (j, 0)), wcol, wcol, rev_tile, rev_tile, rev_tile,
                  rev_tile, small],
        out_specs=(pl.BlockSpec((TN, D), lambda j, i: (j, 0)), wcol, wcol, small,
                   pl.BlockSpec(memory_space=pl.ANY)),
        scratch_shapes=[pltpu.VMEM((S, D), F32), pltpu.VMEM((8, TN), F32), pltpu.SemaphoreType.DMA],
        compiler_params=_params(("arbitrary", "arbitrary"), 56),
    )(dy2b, x1b, w_down, w_a, w_b, a_pre, a, b_lin, hff, conv_w8)


def _ln1_mix_bwd(dy2, dx1_ffn, xhat1, rstd1, g1, attn, gm, w_o, carried=None):
    def body(dy2_ref, dxf_ref, xh_ref, rs_ref, g_ref, a_ref, gm_ref, w_ref,
             dy1_ref, da_ref, dlt_ref, dgm_ref, gwo_ref, vec_ref, dwo_ref):
        @pl.when(pl.program_id(0) == 0)
        def _():
            dwo_ref[...] = jnp.zeros_like(dwo_ref)
            vec_ref[...] = jnp.zeros_like(vec_ref)

        xhat = xh_ref[...]
        dx1 = ALPHA * dy2_ref[...] + dxf_ref[...]
        vec_ref[0:1, :] += _colsum(dx1 * xhat)
        vec_ref[1:2, :] += _colsum(dx1)
        dy1 = _ln_bwd(dx1 * g_ref[...], xhat, rs_ref[...])
        dy1_ref[...] = dy1
        dy1b = dy1.astype(BF16)
        dmix = _dot_nt(dy1b, w_ref[...])
        attn_t = a_ref[...]
        d_attn = dmix[:, 0:D_HALF]
        da_ref[...] = d_attn
        dgm_ref[...] = dmix[:, D_HALF:D]
        lo = (lax.broadcasted_iota(jnp.int32, (TM, LANES), 1) < 64)
        for s in range(4):
            sl = slice(s * LANES, (s + 1) * LANES)
            prod = d_attn[:, sl] * attn_t[:, sl]
            s0 = jnp.sum(jnp.where(lo, prod, 0.0), axis=-1, keepdims=True)
            s1 = jnp.sum(jnp.where(lo, 0.0, prod), axis=-1, keepdims=True)
            dlt_ref[:, sl] = jnp.where(lo, s0, s1)
        dwo_ref[0:D_HALF, :] += _dot_tn(attn_t.astype(BF16), dy1b)
        dwo_ref[D_HALF:D, :] += _dot_tn(gm_ref[...], dy1b)

        @pl.when(pl.program_id(0) == NT - 1)
        def _():
            for j in range(N_DEV):
                gwo_ref[_owner_slot(j)] = dwo_ref[LANES * j:LANES * (j + 1), :].astype(BF16)

    half = jax.ShapeDtypeStruct((S, D_HALF), F32)
    return _host_call(
        body, carried, name="ln1_mix_bwd", grid=(NT,),
        out_shape=(jax.ShapeDtypeStruct((S, D), F32), half, half, half,
                   jax.ShapeDtypeStruct((N_DEV, D // N_DEV, D), BF16), jax.ShapeDtypeStruct((8, D), F32)),
        in_specs=[_row_spec(D), _row_spec(D), _row_spec(D), pl.BlockSpec((TM, 1), _rows), _full_spec((1, D)),
                  _row_spec(D_HALF), _row_spec(D_HALF), _full_spec((D, D))],
        out_specs=(_row_spec(D), _row_spec(D_HALF), _row_spec(D_HALF), _row_spec(D_HALF),
                   _full_spec((N_DEV, D // N_DEV, D)), _full_spec((8, D))),
        scratch_shapes=[pltpu.VMEM((D, D), F32)],
        params=_params(("arbitrary",)), args=(dy2, dx1_ffn, xhat1, rstd1, g1, attn, gm, w_o))


def _gmlp_bwd(d_gm, u_pre, z_pre, ln_z_g, ln_z_b, w_s, bs_exp, carried=None):
    def body(dg_ref, u_ref, z_ref, g_ref, b_ref, ws_ref, bs_ref, du_ref, dz_ref, dws_ref, dbs_ref, vec_ref):
        @pl.when(pl.program_id(0) == 0)
        def _():
            dws_ref[...] = jnp.zeros_like(dws_ref)
            dbs_ref[...] = jnp.zeros_like(dbs_ref)
            vec_ref[...] = jnp.zeros_like(vec_ref)

        u_pre_t, z_pre_t, dgm = u_ref[...], z_ref[...], dg_ref[...]
        zhat, rstd = _ln_stats(_gelu(z_pre_t))
        zn = zhat * g_ref[...] + b_ref[...]
        wm = _masked_ws(ws_ref)
        mixed = _spatial_mix(zn, wm, bs_ref[...])
        du_ref[...] = (dgm * mixed * _gelu_grad(u_pre_t)).astype(BF16)
        dmixed = dgm * _gelu(u_pre_t)
        lo = _lane_lo()
        tril = _tril()
        group_of_lane = lax.broadcasted_iota(jnp.int32, (8, D_HALF), 1) // 64
        pick = (group_of_lane == lax.broadcasted_iota(jnp.int32, (8, D_HALF), 0)).astype(F32)
        dzn_rows = []
        for ch in range(TM // CHUNK):
            rows = slice(ch * CHUNK, (ch + 1) * CHUNK)
            dbs_ref[...] += lax.dot_general(pick, dmixed[rows, :], (((1,), (1,)), ((), ())),
                                            precision=lax.Precision.HIGHEST, preferred_element_type=F32)
            slabs = []
            for pr in range(4):
                sl = slice(pr * LANES, (pr + 1) * LANES)
                dm = dmixed[rows, sl]
                zp = zn[rows, sl].astype(BF16)
                dm_lo = jnp.where(lo, dm, 0.0).astype(BF16)
                dm_hi = jnp.where(lo, 0.0, dm).astype(BF16)
                dws_ref[2 * pr] += jnp.where(tril, _dot_nt(dm_lo, zp), 0.0)
                dws_ref[2 * pr + 1] += jnp.where(tril, _dot_nt(dm_hi, zp), 0.0)
                dmb = dm.astype(BF16)
                slabs.append(jnp.where(lo, _dot_tn(wm[2 * pr], dmb), _dot_tn(wm[2 * pr + 1], dmb)))
            dzn_rows.append(jnp.concatenate(slabs, axis=1))
        dzn = jnp.concatenate(dzn_rows, axis=0)
        vec_ref[0:1, :] += _colsum(dzn * zhat)
        vec_ref[1:2, :] += _colsum(dzn)
        dz = _ln_bwd(dzn * g_ref[...], zhat, rstd)
        dz_ref[...] = (dz * _gelu_grad(z_pre_t)).astype(BF16)

    halfb = jax.ShapeDtypeStruct((S, D_HALF), BF16)
    vec = _full_spec((1, D_HALF))
    return _host_call(
        body, carried, name="gmlp_bwd", grid=(NT,),
        out_shape=(halfb, halfb, jax.ShapeDtypeStruct((8, CHUNK, CHUNK), F32),
                   jax.ShapeDtypeStruct((8, CHUNK), F32), jax.ShapeDtypeStruct((8, D_HALF), F32)),
        in_specs=[_row_spec(D_HALF), _row_spec(D_HALF), _row_spec(D_HALF), vec, vec,
                  _full_spec((8, CHUNK, CHUNK)), _full_spec((CHUNK, D_HALF))],
        out_specs=(_row_spec(D_HALF), _row_spec(D_HALF), _full_spec((8, CHUNK, CHUNK)),
                   _full_spec((8, CHUNK)), _full_spec((8, D_HALF))),
        scratch_shapes=[], params=_params(("arbitrary",)), args=(d_gm, u_pre, z_pre, ln_z_g, ln_z_b, w_s, bs_exp))


def _attention_bwd(q, k, v, lse, d_attn, delta, tabs, carried=None):
    def body(q_ref, k_ref, v_ref, l_ref, do_ref, dl_ref, c_ref, sa_ref, sb_ref, dq_ref, dk_ref, dv_ref,
             qb, kb, vb, gb, bias, lsp, dlp, dqp, dk_own, dk_prev, dv_own, dv_prev, dqa, dka, dva):
        lo = _lane_lo()
        zero_pad = jnp.zeros((CHUNK, LANES), BF16)
        for buf in (qb, kb, vb, gb):
            buf[0:CHUNK, :] = zero_pad
        dk_prev[S:S + CHUNK, :] = jnp.zeros((CHUNK, LANES), F32)
        dv_prev[S:S + CHUNK, :] = jnp.zeros((CHUNK, LANES), F32)
        _store_band_bias(bias)
        for d, nb in DILATIONS:
            length = S // d
            for r in range(d):
                dst = slice(CHUNK + r * length, CHUNK + (r + 1) * length)
                src = slice(r * length, (r + 1) * length)
                qb[dst, :] = (_permuted_rows(q_ref, d, r) * 0.125).astype(BF16)
                kb[dst, :] = _permuted_rows(k_ref, d, r).astype(BF16)
                vb[dst, :] = _permuted_rows(v_ref, d, r).astype(BF16)
                gb[dst, :] = _permuted_rows(do_ref, d, r).astype(BF16)
                lsp[src, :] = _permuted_rows(l_ref, d, r)
                dlp[src, :] = _permuted_rows(dl_ref, d, r)

            def block(b, carry, nb=nb):
                base = pl.multiple_of(b * CHUNK, CHUNK)
                own = pl.multiple_of(base + CHUNK, CHUNK)
                add = bias[jnp.where(b % nb == 0, 1, 0)]
                qblk = qb[pl.ds(own, CHUNK), :]
                gblk = gb[pl.ds(own, CHUNK), :]
                kblk = kb[pl.ds(base, 2 * CHUNK), :]
                vblk = vb[pl.ds(base, 2 * CHUNK), :]
                lse_t = lsp[pl.ds(base, CHUNK), :]
                dlt_t = dlp[pl.ds(base, CHUNK), :]
                q2 = jnp.concatenate([jnp.where(lo, qblk, 0), jnp.where(lo, 0, qblk)], axis=0)
                g2 = jnp.concatenate([jnp.where(lo, gblk, 0), jnp.where(lo, 0, gblk)], axis=0)
                lse2 = jnp.concatenate([lse_t[:, 0:1], lse_t[:, 64:65]], axis=0)
                dlt2 = jnp.concatenate([dlt_t[:, 0:1], dlt_t[:, 64:65]], axis=0)
                add2 = jnp.concatenate([add, add], axis=0)
                p = jnp.exp(_dot_nt(q2, kblk) + add2 - lse2)
                ds = (p * (_dot_nt(g2, vblk) - dlt2)).astype(BF16)
                dv_blk = _dot_tn(p.astype(BF16), g2)
                dk_blk = _dot_tn(ds, q2)
                dq2 = _dot(ds, kblk)
                dqp[pl.ds(base, CHUNK), :] = jnp.where(lo, dq2[0:CHUNK, :], dq2[CHUNK:2 * CHUNK, :]) * 0.125
                dk_prev[pl.ds(base, CHUNK), :] = dk_blk[0:CHUNK, :]
                dk_own[pl.ds(own, CHUNK), :] = dk_blk[CHUNK:2 * CHUNK, :]
                dv_prev[pl.ds(base, CHUNK), :] = dv_blk[0:CHUNK, :]
                dv_own[pl.ds(own, CHUNK), :] = dv_blk[CHUNK:2 * CHUNK, :]
                return carry

            lax.fori_loop(0, S // CHUNK, block, 0, unroll=ATTN_UNROLL)
            for r in range(d):
                src = slice(r * length, (r + 1) * length)
                pad = slice(CHUNK + r * length, CHUNK + (r + 1) * length)
                if d == 1:
                    dqa[...] = dqp[...]
                    dka[...] = dk_own[pad, :] + dk_prev[pad, :]
                    dva[...] = dv_own[pad, :] + dv_prev[pad, :]
                else:
                    dst = pl.ds(r, length, stride=d)
                    dqa[dst, :] = dqa[dst, :] + dqp[src, :]
                    dka[dst, :] = dka[dst, :] + (dk_own[pad, :] + dk_prev[pad, :])
                    dva[dst, :] = dva[dst, :] + (dv_own[pad, :] + dv_prev[pad, :])
        for t in range(NT):
            rows = slice(t * TM, (t + 1) * TM)
            c, sa, sb = c_ref[rows, :], sa_ref[rows, :], sb_ref[rows, :]
            dq_ref[rows, :] = _rope_t(dqa[rows, :], c, sa, sb).astype(BF16)
            dk_ref[rows, :] = _rope_t(dka[rows, :], c, sa, sb).astype(BF16)
            dv_ref[rows, :] = dva[rows, :].astype(BF16)

    slab = pl.BlockSpec((S, LANES), lambda h: (0, h), pipeline_mode=pl.Buffered(1))
    tab = pl.BlockSpec((S, LANES), lambda h: (0, 0), pipeline_mode=pl.Buffered(1))
    out_slab = pl.BlockSpec((S, LANES), lambda h: (0, h))
    out = jax.ShapeDtypeStruct((S, D_HALF), BF16)
    padded_b = pltpu.VMEM((CHUNK + S, LANES), BF16)
    padded_f = pltpu.VMEM((CHUNK + S, LANES), F32)
    whole = pltpu.VMEM((S, LANES), F32)
    return _host_call(
        body, carried, name="attention_bwd", grid=(4,), out_shape=(out, out, out),
        in_specs=[slab] * 6 + [tab] * 3, out_specs=(out_slab,) * 3,
        scratch_shapes=[padded_b] * 4 + [pltpu.VMEM((2, CHUNK, 2 * CHUNK), F32)] + [whole] * 3
        + [padded_f] * 4 + [whole] * 3,
        params=_params(("arbitrary",), 60), args=(q, k, v, lse, d_attn, delta, *tabs))


def _proj_in_bwd_w(xb, parts):
    def body(x_ref, p0, p1, p2, p3, p4, gw_ref, dw_ref):
        @pl.when(pl.program_id(0) == 0)
        def _():
            dw_ref[...] = jnp.zeros_like(dw_ref)

        xt = x_ref[...]
        for n, part in enumerate((p0, p1, p2, p3, p4)):
            dw_ref[:, n * D_HALF:(n + 1) * D_HALF] += _dot_tn(xt, part[...])

        @pl.when(pl.program_id(0) == NT - 1)
        def _():
            width = D_IN // N_DEV
            for j in range(N_DEV):
                gw_ref[_owner_slot(j)] = dw_ref[:, width * j:width * (j + 1)].astype(BF16)

    return pl.pallas_call(
        body, name="proj_in_bwd_w", grid=(NT,), out_shape=jax.ShapeDtypeStruct((N_DEV, D, D_IN // N_DEV), BF16),
        in_specs=[_row_spec(D)] + [_row_spec(D_HALF)] * 5, out_specs=_full_spec((N_DEV, D, D_IN // N_DEV)),
        scratch_shapes=[pltpu.VMEM((D, D_IN), F32)],
        compiler_params=_params(("arbitrary",)),
    )(xb, *parts)


def _proj_in_bwd_x(dy1, parts, w_in, carried=None):
    def body(dy_ref, p0, p1, p2, p3, p4, w_ref, gx_ref):
        acc = ALPHA * dy_ref[...]
        for n, part in enumerate((p0, p1, p2, p3, p4)):
            acc += _dot_nt(part[...], w_ref[:, n * D_HALF:(n + 1) * D_HALF])
        gx_ref[...] = acc

    return _host_call(
        body, carried, name="proj_in_bwd_x", grid=(NT,), out_shape=(jax.ShapeDtypeStruct((S, D), F32),),
        in_specs=[_row_spec(D)] + [_row_spec(D_HALF)] * 5 + [_full_spec((D, D_IN))], out_specs=(_row_spec(D),),
        scratch_shapes=[], params=_params(("arbitrary",)), args=(dy1, *parts, w_in))


def _to_natural(blocks, name):
    n, rows, w = blocks.shape
    tile = min(rows, 256)

    def body(i_ref, o_ref):
        o_ref[...] = jnp.concatenate([i_ref[j] for j in range(n)], axis=1)

    return pl.pallas_call(
        body, name=name, grid=(rows // tile,), out_shape=jax.ShapeDtypeStruct((rows, n * w), blocks.dtype),
        in_specs=[pl.BlockSpec((n, tile, w), lambda i: (0, i, 0))],
        out_specs=pl.BlockSpec((tile, n * w), lambda i: (i, 0)), compiler_params=_params(("parallel",)),
    )(blocks)


def _column_blocks(full, name):
    rows, cols = full.shape
    w = cols // N_DEV
    tile = 256

    def body(i_ref, o_ref):
        for j in range(N_DEV):
            o_ref[_owner_slot(j)] = i_ref[:, j * w:(j + 1) * w].astype(BF16)

    return pl.pallas_call(
        body, name=name, grid=(rows // tile,), out_shape=jax.ShapeDtypeStruct((N_DEV, rows, w), BF16),
        in_specs=[pl.BlockSpec((tile, cols), lambda i: (i, 0))],
        out_specs=pl.BlockSpec((N_DEV, tile, w), lambda i: (0, i, 0)), compiler_params=_params(("parallel",)),
    )(full)


def _row_blocks(full):
    rows, cols = full.shape
    r = rows // N_DEV

    def body(i_ref, o_ref):
        o_ref[0] = i_ref[...].astype(BF16)

    return pl.pallas_call(
        body, name="ff_down_grad_blocks", grid=(N_DEV,), out_shape=jax.ShapeDtypeStruct((N_DEV, r, cols), BF16),
        in_specs=[pl.BlockSpec((r, cols), lambda s: ((s % 4) * 2 + s // 4, 0))],
        out_specs=pl.BlockSpec((1, r, cols), lambda s: (s, 0, 0)), compiler_params=_params(("parallel",)),
    )(full)


def _local_step(x, p, pos_col, target, w_in, sm, ex):
    bs_exp = jnp.repeat(sm["b_s"].T, 64, axis=1)
    tabs = _rope_tables(pos_col)
    q, k, v, u_pre, z_pre, gm, xb = _proj_in_fwd(x, w_in, tabs, sm["ln_z_g"], sm["ln_z_b"], sm["w_s"], bs_exp)
    (attn, lse), got = _attention_fwd(q, k, v, ex.gather_first())
    wa = ex.weights_first(got)
    xhat1, rstd1, x1b = _mix_ln1_fwd(attn, gm, wa["w_o"], x, sm["ln1_g"], sm["ln1_b"])
    (a_pre, a, b_lin, hff), got = _ffn_up_fwd(x1b, wa["w_ff_a"], wa["w_ff_b"], wa["conv_w8"], sm["conv_b"],
                                              ex.gather_second())
    wc = ex.weights_second(got)
    xhat2, rstd2 = _ffn_down_ln2_fwd(hff, wc["w_ff_down"], xhat1, sm["ln1_g"], sm["ln1_b"])
    loss, dy2, dy2b, dw_g, dw_p, vec_tail = _tail_fwd_bwd(
        xhat2, rstd2, p, target, wc["w_ple_gate"], wc["w_ple_in"], sm["ln2_g"], sm["ln2_b"],
        sm["b_ple_gate"], sm["ln3_g"], sm["ln3_b"])
    dw_down, dw_a, dw_b, dconv, dx1_ffn = _ffn_bwd(dy2b, x1b, wc["w_ff_down"], wa["w_ff_a"], wa["w_ff_b"],
                                                    a_pre, a, b_lin, hff, wa["conv_w8"])
    (dy1, d_attn, delta, d_gm, dw_o, vec_ln1), _ = _ln1_mix_bwd(
        dy2, dx1_ffn, xhat1, rstd1, sm["ln1_g"], attn, gm, wa["w_o"])
    early = {"w_ff_a": _column_blocks(dw_a, "ff_a_grad_blocks"), "w_ff_b": _column_blocks(dw_b, "ff_b_grad_blocks"),
             "w_ff_down": _row_blocks(dw_down), "w_ple_gate": dw_g, "w_ple_in": dw_p, "w_o": dw_o}
    (du, dz, dws, dbs, vec_z), got = _gmlp_bwd(d_gm, u_pre, z_pre, sm["ln_z_g"], sm["ln_z_b"], sm["w_s"], bs_exp,
                                               ex.to_sibling(early))
    chip_sums = ex.reduce_on_chip(got)
    small = {"tail": vec_tail, "ln1": vec_ln1, "ln_z": vec_z, "conv": dconv, "w_s": dws, "b_s": dbs, "loss": loss}
    (dq, dk, dv), got_early = _attention_bwd(q, k, v, lse, d_attn, delta, tabs,
                                             ex.between_chips(chip_sums, small))
    parts = (dq, dk, dv, du, dz)
    (grad_x,), got_late = _proj_in_bwd_x(dy1, parts, w_in, ex.last(_proj_in_bwd_w(xb, parts)))
    return grad_x, ex.collect(got_early, got_late)


def _mesh_pos():
    return lax.axis_index("x"), lax.axis_index("y"), lax.axis_index("c")


def _cast_shards(shards):
    n = len(shards)

    def body(*refs):
        for a in range(n):
            refs[n + a][...] = refs[a][...].astype(BF16)

    return pl.pallas_call(
        body, name="cast_shards", out_shape=tuple(jax.ShapeDtypeStruct(s.shape, BF16) for s in shards),
        compiler_params=_params(None),
    )(*shards)


class _GatherComm:
    def __init__(self, shards):
        n = len(shards)
        self.inputs = list(shards)
        self.out_shapes = [jax.ShapeDtypeStruct((N_DEV,) + s.shape, s.dtype) for s in shards]
        self.scratch = [pltpu.SemaphoreType.DMA((7 * n,)), pltpu.SemaphoreType.DMA((7 * n,)),
                        pltpu.SemaphoreType.DMA((n,))]

    def phases(self, x_refs, out_refs, sems):
        send_sems, recv_sems, local_sems = sems
        n_arr = len(x_refs)

        def where():
            x, y, c = _mesh_pos()
            return (x, y, c), (x, y, 1 - c), [(1 - x, y), (x, 1 - y), (1 - x, 1 - y)]

        def copy(a, n, block, to, from_shard=False):
            dst = out_refs[a].at[4 * block[0] + 2 * block[1] + block[2]]
            return pltpu.make_async_remote_copy(
                src_ref=x_refs[a] if from_shard else dst, dst_ref=dst, send_sem=send_sems.at[7 * a + n],
                recv_sem=recv_sems.at[7 * a + n], device_id=to, device_id_type=MESH)

        def local(a):
            x, y, c = _mesh_pos()
            return pltpu.make_async_copy(x_refs[a], out_refs[a].at[4 * x + 2 * y + c], local_sems.at[a])

        def start():
            me, sibling, chips = where()
            for a in range(n_arr):
                local(a).start()
                copy(a, 0, me, sibling, from_shard=True).start()
                for n, chip in enumerate(chips):
                    copy(a, 1 + n, me, (*chip, me[2]), from_shard=True).start()

        def forward():
            me, sibling, chips = where()
            for n, chip in enumerate(chips):
                for a in range(n_arr):
                    copy(a, 1 + n, (*chip, me[2]), me).wait_recv()
                    copy(a, 4 + n, (*chip, me[2]), sibling).start()

        def finish():
            me, sibling, chips = where()
            for a in range(n_arr):
                copy(a, 0, sibling, me).wait_recv()
                copy(a, 0, me, sibling, from_shard=True).wait_send()
                for n, chip in enumerate(chips):
                    copy(a, 4 + n, (*chip, 1 - me[2]), me).wait_recv()
                    copy(a, 1 + n, me, (*chip, me[2]), from_shard=True).wait_send()
                    copy(a, 4 + n, (*chip, me[2]), sibling).wait_send()
                local(a).wait()

        return {"start": start, "forward": forward, "finish": finish}


class _SiblingComm:
    def __init__(self, big):
        n = len(big)
        self.inputs = list(big)
        self.out_shapes = [jax.ShapeDtypeStruct(b.shape[1:], b.dtype) for b in big]
        self.scratch = [pltpu.SemaphoreType.DMA((n,)), pltpu.SemaphoreType.DMA((n,))]

    def phases(self, src, dst, sems):
        send_sems, recv_sems = sems

        def copies():
            x, y, c = _mesh_pos()
            return [pltpu.make_async_remote_copy(
                src_ref=src[a].at[1 - c], dst_ref=dst[a], send_sem=send_sems.at[a], recv_sem=recv_sems.at[a],
                device_id=(x, y, 1 - c), device_id_type=MESH) for a in range(len(src))]

        def start():
            for cp in copies():
                cp.start()

        def finish():
            for cp in copies():
                cp.wait()

        return {"start": start, "finish": finish}


class _ChipComm:
    def __init__(self, sums):
        n = len(sums)
        self.inputs = list(sums)
        self.out_shapes = [jax.ShapeDtypeStruct(s.shape, s.dtype) for s in sums]
        self.scratch = [pltpu.SemaphoreType.DMA((3 * n,)), pltpu.SemaphoreType.DMA((3 * n,)),
                        pltpu.SemaphoreType.DMA((n,))]

    def phases(self, src, dst, sems):
        send_sems, recv_sems, local_sems = sems

        def copies():
            x, y, c = _mesh_pos()
            my_chip = 2 * x + y
            out = [pltpu.make_async_copy(src[a].at[my_chip], dst[a].at[my_chip], local_sems.at[a])
                   for a in range(len(src))]
            for n, (px, py) in enumerate([(1 - x, y), (x, 1 - y), (1 - x, 1 - y)]):
                for a in range(len(src)):
                    out.append(pltpu.make_async_remote_copy(
                        src_ref=src[a].at[2 * px + py], dst_ref=dst[a].at[my_chip],
                        send_sem=send_sems.at[3 * a + n], recv_sem=recv_sems.at[3 * a + n],
                        device_id=(px, py, c), device_id_type=MESH))
            return out

        def start():
            for cp in copies():
                cp.start()

        def finish():
            for cp in copies():
                cp.wait()

        return {"start": start, "finish": finish}


class _ScatterComm:
    def __init__(self, blocks, small):
        self.n_big, self.n_small = len(blocks), len(small)
        n = self.n_big + self.n_small
        self.inputs = list(blocks) + list(small)
        self.out_shapes = ([jax.ShapeDtypeStruct(b.shape, b.dtype) for b in blocks]
                           + [jax.ShapeDtypeStruct((N_DEV,) + s.shape, s.dtype) for s in small])
        self.scratch = [pltpu.SemaphoreType.DMA((7 * n,)), pltpu.SemaphoreType.DMA((7 * n,)),
                        pltpu.SemaphoreType.DMA((n,))]

    def phases(self, src, dst, sems):
        send_sems, recv_sems, local_sems = sems
        n_big, n_all = self.n_big, self.n_big + self.n_small

        def source(a, core, chip):
            return src[a].at[core * 4 + chip] if a < n_big else src[a]

        def copies():
            x, y, c = _mesh_pos()
            me = 4 * x + 2 * y + c
            out = [pltpu.make_async_copy(source(a, c, 2 * x + y), dst[a].at[me], local_sems.at[a])
                   for a in range(n_all)]
            for flip in range(1, N_DEV):
                px = 1 - x if flip & 4 else x
                py = 1 - y if flip & 2 else y
                pc = 1 - c if flip & 1 else c
                for a in range(n_all):
                    n = 7 * a + flip - 1
                    out.append(pltpu.make_async_remote_copy(
                        src_ref=source(a, pc, 2 * px + py), dst_ref=dst[a].at[me], send_sem=send_sems.at[n],
                        recv_sem=recv_sems.at[n], device_id=(px, py, pc), device_id_type=MESH))
            return out

        def start():
            for cp in copies():
                cp.start()

        def finish():
            for cp in copies():
                cp.wait()

        return {"start": start, "finish": finish}


class _Both:
    def __init__(self, first, second):
        self.parts = (first, second)
        self.inputs = first.inputs + second.inputs
        self.out_shapes = first.out_shapes + second.out_shapes
        self.scratch = first.scratch + second.scratch

    def phases(self, src, dst, sems):
        a, b = self.parts
        pa = a.phases(src[:len(a.inputs)], dst[:len(a.out_shapes)], sems[:len(a.scratch)])
        pb = b.phases(src[len(a.inputs):], dst[len(a.out_shapes):], sems[len(a.scratch):])

        def both(name):
            def run():
                pa[name]()
                pb[name]()
            return run

        return {name: both(name) for name in pa}


def _host_call(body, carried, *, name, grid, out_shape, in_specs, out_specs, scratch_shapes, params, args):
    if carried is None:
        return pl.pallas_call(body, name=name, grid=grid, out_shape=tuple(out_shape), in_specs=list(in_specs),
                              out_specs=tuple(out_specs), scratch_shapes=list(scratch_shapes),
                              compiler_params=params)(*args), ()
    comm, when = carried
    n_in, n_out, n_scratch = len(in_specs), len(out_shape), len(scratch_shapes)
    k_in, k_out = len(comm.inputs), len(comm.out_shapes)

    def wrapped(*refs):
        bounds = np.cumsum([0, n_in, k_in, n_out, k_out, n_scratch])
        ins, c_in, outs, c_out, scr = (refs[bounds[i]:bounds[i + 1]] for i in range(5))
        phases = comm.phases(c_in, c_out, refs[bounds[5]:])
        for phase, cond in when("before"):
            pl.when(cond)(phases[phase])
        body(*ins, *outs, *scr)
        for phase, cond in when("after"):
            pl.when(cond)(phases[phase])

    anywhere = pl.BlockSpec(memory_space=pl.ANY)
    results = pl.pallas_call(
        wrapped, name=name, grid=grid, out_shape=tuple(out_shape) + tuple(comm.out_shapes),
        in_specs=list(in_specs) + [anywhere] * k_in, out_specs=tuple(out_specs) + (anywhere,) * k_out,
        scratch_shapes=list(scratch_shapes) + comm.scratch, compiler_params=params,
    )(*args, *comm.inputs)
    return results[:n_out], results[n_out:]


class _Exchanges:
    FIRST = ("w_o", "w_ff_a", "w_ff_b")
    SECOND = ("w_ff_down", "w_ple_gate", "w_ple_in")
    EARLY = ("w_ff_a", "w_ff_b", "w_ff_down", "w_ple_gate", "w_ple_in", "w_o")
    LATE = ("w_in",)

    def __init__(self, shards, conv_rows):
        self.shards, self.conv_rows = shards, conv_rows
        self.axis = {name: axis for name, _, axis in BIG}

    def _natural(self, name, blocks):
        n, r, c = blocks.shape
        return blocks.reshape(n * r, c) if self.axis[name] == 0 else _to_natural(blocks, name + "_natural")

    def gather_first(self):
        comm = _GatherComm([self.shards[n] for n in self.FIRST] + [self.conv_rows])

        def when(position):
            step = pl.program_id(0)
            if position == "before":
                return [("start", step == 0), ("forward", step == 3)]
            return [("finish", step == 3)]
        return comm, when

    def weights_first(self, got):
        out = {name: self._natural(name, blocks) for name, blocks in zip(self.FIRST, got)}
        out["conv_w8"] = _to_natural(got[-1], "conv_w_natural")
        return out

    def gather_second(self):
        comm = _GatherComm([self.shards[n] for n in self.SECOND])

        def when(position):
            j, i = pl.program_id(0), pl.program_id(1)
            if position == "before":
                return [("start", (j == 0) & (i == 0)), ("forward", (j == NJ - 3) & (i == 0))]
            return [("finish", (j == NJ - 1) & (i == NT - 1))]
        return comm, when

    def weights_second(self, got):
        return {name: self._natural(name, blocks) for name, blocks in zip(self.SECOND, got)}

    def to_sibling(self, early):
        self.by_core = [early[n].reshape((2, 4) + early[n].shape[1:]) for n in self.EARLY]
        return _SiblingComm(self.by_core), _first_and_last(NT)

    def reduce_on_chip(self, from_sibling):
        core = lax.axis_index("c").astype(jnp.int32).reshape(1)
        return _chip_reduce(self.by_core, from_sibling, core, "chip_reduce")

    def between_chips(self, chip_sums, small):
        self.small_keys = tuple(small)
        return _Both(_ChipComm(chip_sums), _ScatterComm([], [small[k] for k in self.small_keys])), _first_and_last(4)

    def last(self, dw_in):
        by_core = [dw_in.reshape((2, 4) + dw_in.shape[1:])]
        core = lax.axis_index("c").astype(jnp.int32).reshape(1)
        sums = _chip_reduce(by_core, _standalone(_SiblingComm(by_core), "w_in_grad_to_sibling"), core, "w_in_chip_reduce")
        return _ChipComm(sums), _first_and_last(NT)

    def collect(self, got_early, got_late):
        parts = dict(zip(self.EARLY, got_early[:len(self.EARLY)]))
        parts.update(zip(self.LATE, got_late))
        return parts, dict(zip(self.small_keys, got_early[len(self.EARLY):]))


def _first_and_last(n_steps):
    def when(position):
        step = pl.program_id(0)
        return [("start", step == 0)] if position == "before" else [("finish", step == n_steps - 1)]
    return when


def _all_gather(shards, name):
    comm = _GatherComm(shards)

    def body(*refs):
        n = len(shards)
        phases = comm.phases(refs[:n], refs[n:2 * n], refs[2 * n:])
        phases["start"]()
        phases["forward"]()
        phases["finish"]()

    anywhere = pl.BlockSpec(memory_space=pl.ANY)
    return pl.pallas_call(
        body, name=name, out_shape=tuple(comm.out_shapes), in_specs=[anywhere] * len(shards),
        out_specs=(anywhere,) * len(shards), scratch_shapes=comm.scratch,
    )(*shards)


def _standalone(comm, name):
    n_in = len(comm.inputs)

    def body(*refs):
        phases = comm.phases(refs[:n_in], refs[n_in:n_in + len(comm.out_shapes)], refs[n_in + len(comm.out_shapes):])
        phases["start"]()
        phases["finish"]()

    anywhere = pl.BlockSpec(memory_space=pl.ANY)
    return pl.pallas_call(
        body, name=name, out_shape=tuple(comm.out_shapes), in_specs=[anywhere] * n_in,
        out_specs=(anywhere,) * len(comm.out_shapes), scratch_shapes=comm.scratch,
    )(*comm.inputs)


def _chip_reduce(big, from_sibling, core, name):
    n = len(big)

    def body(core_ref, *refs):
        for a in range(n):
            mine, theirs, out = refs[a], refs[n + a], refs[2 * n + a]
            out[0] = (mine[0, 0].astype(F32) + theirs[0].astype(F32)).astype(BF16)

    def block(shape):
        return pl.BlockSpec((1,) + shape, lambda ch, core_ref: (ch, 0, 0))

    grid_spec = pltpu.PrefetchScalarGridSpec(
        num_scalar_prefetch=1, grid=(4,),
        in_specs=[pl.BlockSpec((1, 1) + b.shape[2:], lambda ch, core_ref: (core_ref[0], ch, 0, 0)) for b in big]
        + [block(b.shape[2:]) for b in big],
        out_specs=[block(b.shape[2:]) for b in big])
    return pl.pallas_call(
        body, name=name, grid_spec=grid_spec,
        out_shape=tuple(jax.ShapeDtypeStruct(b.shape[1:], BF16) for b in big),
        compiler_params=_params(("parallel",)),
    )(core, *big, *from_sibling)


def _adamw(g, w, m, v):
    nm = ADAM_B1 * m + (1.0 - ADAM_B1) * g
    nv = ADAM_B2 * v + (1.0 - ADAM_B2) * (g * g)
    m_hat = nm / (1.0 - ADAM_B1 ** ADAM_STEP)
    v_hat = nv / (1.0 - ADAM_B2 ** ADAM_STEP)
    return -ADAM_LR * (m_hat / (jnp.sqrt(v_hat) + ADAM_EPS) + ADAM_WD * w), nm, nv


def _adamw_sharded(parts, w, m, v, name):
    def body(p_ref, w_ref, m_ref, v_ref, g_ref, d_ref, nm_ref, nv_ref):
        g = p_ref[0].astype(F32)
        for s in range(1, parts.shape[0]):
            g = g + p_ref[s].astype(F32)
        delta, nm, nv = _adamw(g, w_ref[0], m_ref[0], v_ref[0])
        g_ref[0] = g
        d_ref[0] = delta
        nm_ref[0] = nm
        nv_ref[0] = nv

    return pl.pallas_call(
        body, name=name, out_shape=(jax.ShapeDtypeStruct(w.shape, F32),) * 4, compiler_params=_params(None),
    )(parts, w, m, v)


REPLICATED = (("ln_z_g", "ln_z", 0), ("ln_z_b", "ln_z", 1), ("w_s", "w_s", None), ("b_s", "b_s", None),
              ("ln1_g", "ln1", 0), ("ln1_b", "ln1", 1), ("conv_w", "conv_mine", None), ("conv_b", "conv", 3),
              ("ln2_g", "tail", 3), ("ln2_b", "tail", 4), ("b_ple_gate", "tail", 0), ("ln3_g", "tail", 1),
              ("ln3_b", "tail", 2))
GATHERED = ("tail", "ln1", "ln_z", "conv", "w_s", "b_s", "loss", "conv_mine")


def _adamw_replicated(gathered, w, m, v):
    n_par = len(REPLICATED)

    def body(*refs):
        srcs = dict(zip(GATHERED, refs[:len(GATHERED)]))
        rest = refs[len(GATHERED):]
        w_refs, m_refs, v_refs = rest[:n_par], rest[n_par:2 * n_par], rest[2 * n_par:3 * n_par]
        outs = rest[3 * n_par:]
        loss_ref = outs[4 * n_par]
        sums = {}
        for key, ref in srcs.items():
            total = ref[0]
            for dev in range(1, N_DEV):
                total = total + ref[dev]
            sums[key] = total
        loss_ref[...] = sums["loss"]
        for n, (name, key, row) in enumerate(REPLICATED):
            if name == "conv_w":
                g = sums[key][0:3, :]
            elif row is None:
                g = sums[key]
            else:
                g = sums[key][row:row + 1, :]
            lead = len(w_refs[n].shape) - g.ndim
            idx = (0,) * lead + (Ellipsis,)
            delta, nm, nv = _adamw(g, w_refs[n][idx], m_refs[n][idx], v_refs[n][idx])
            for kind, val in enumerate((g, delta, nm, nv)):
                outs[kind * n_par + n][idx] = val

    names = [name for name, _, _ in REPLICATED]
    shapes = [jax.ShapeDtypeStruct(w[name].shape, F32) for name in names]
    return pl.pallas_call(
        body, name="adamw_replicated", out_shape=tuple(shapes * 4) + (jax.ShapeDtypeStruct((8, LANES), F32),),
        compiler_params=_params(None),
    )(*[gathered[k] for k in GATHERED], *[w[n] for n in names], *[m[n] for n in names], *[v[n] for n in names])


def kernel(x, p, positions, w_in, ln_z_g, ln_z_b, w_s, b_s, w_o, ln1_g, ln1_b, w_ff_a, w_ff_b, conv_w, conv_b, w_ff_down, ln2_g, ln2_b, w_ple_gate, b_ple_gate, w_ple_in, ln3_g, ln3_b, loss_target, m_w_in, m_ln_z_g, m_ln_z_b, m_w_s, m_b_s, m_w_o, m_ln1_g, m_ln1_b, m_w_ff_a, m_w_ff_b, m_conv_w, m_conv_b, m_w_ff_down, m_ln2_g, m_ln2_b, m_w_ple_gate, m_b_ple_gate, m_w_ple_in, m_ln3_g, m_ln3_b, v_w_in, v_ln_z_g, v_ln_z_b, v_w_s, v_b_s, v_w_o, v_ln1_g, v_ln1_b, v_w_ff_a, v_w_ff_b, v_conv_w, v_conv_b, v_w_ff_down, v_ln2_g, v_ln2_b, v_w_ple_gate, v_b_ple_gate, v_w_ple_in, v_ln3_g, v_ln3_b):
    w = dict(w_in=w_in, ln_z_g=ln_z_g, ln_z_b=ln_z_b, w_s=w_s, b_s=b_s, w_o=w_o, ln1_g=ln1_g, ln1_b=ln1_b,
             w_ff_a=w_ff_a, w_ff_b=w_ff_b, conv_w=conv_w, conv_b=conv_b, w_ff_down=w_ff_down, ln2_g=ln2_g,
             ln2_b=ln2_b, w_ple_gate=w_ple_gate, b_ple_gate=b_ple_gate, w_ple_in=w_ple_in, ln3_g=ln3_g,
             ln3_b=ln3_b)
    m = dict(w_in=m_w_in, ln_z_g=m_ln_z_g, ln_z_b=m_ln_z_b, w_s=m_w_s, b_s=m_b_s, w_o=m_w_o, ln1_g=m_ln1_g,
             ln1_b=m_ln1_b, w_ff_a=m_w_ff_a, w_ff_b=m_w_ff_b, conv_w=m_conv_w, conv_b=m_conv_b,
             w_ff_down=m_w_ff_down, ln2_g=m_ln2_g, ln2_b=m_ln2_b, w_ple_gate=m_w_ple_gate,
             b_ple_gate=m_b_ple_gate, w_ple_in=m_w_ple_in, ln3_g=m_ln3_g, ln3_b=m_ln3_b)
    v = dict(w_in=v_w_in, ln_z_g=v_ln_z_g, ln_z_b=v_ln_z_b, w_s=v_w_s, b_s=v_b_s, w_o=v_w_o, ln1_g=v_ln1_g,
             ln1_b=v_ln1_b, w_ff_a=v_w_ff_a, w_ff_b=v_w_ff_b, conv_w=v_conv_w, conv_b=v_conv_b,
             w_ff_down=v_w_ff_down, ln2_g=v_ln2_g, ln2_b=v_ln2_b, w_ple_gate=v_w_ple_gate,
             b_ple_gate=v_b_ple_gate, w_ple_in=v_w_ple_in, ln3_g=v_ln3_g, ln3_b=v_ln3_b)
    big_names = [name for name, _, _ in BIG]
    small_names = ("ln_z_g", "ln_z_b", "w_s", "b_s", "ln1_g", "ln1_b", "conv_b", "ln2_g", "ln2_b", "b_ple_gate",
                   "ln3_g", "ln3_b")

    shards = dict(zip(big_names, _cast_shards([w[n][0] for n in big_names])))
    conv_rows = jnp.pad(w["conv_w"][0], ((0, 5), (0, 0)))
    w_in_full = _to_natural(_all_gather([shards["w_in"]], "all_gather_w_in")[0], "w_in_natural")
    sm = {n: w[n][0] if w[n].ndim > 2 else w[n] for n in small_names}
    pos_col = positions.reshape(S, 1).astype(F32)
    grad_x, (parts, small_all) = _local_step(x[0], p[0, 0], pos_col, loss_target[0], w_in_full, sm,
                                             _Exchanges(shards, conv_rows))
    me = 4 * lax.axis_index("x") + 2 * lax.axis_index("y") + lax.axis_index("c")
    conv_cols = small_all["conv"].reshape(N_DEV, 8, N_DEV, D_FF // N_DEV)
    small_all["conv_mine"] = lax.dynamic_index_in_dim(conv_cols, me, axis=2, keepdims=False)

    leaves = {}
    for name in big_names:
        leaves[name] = _adamw_sharded(parts[name], w[name], m[name], v[name], "adamw_" + name)
    rep = _adamw_replicated(small_all, w, m, v)
    n_rep = len(REPLICATED)
    for n, (name, _, _) in enumerate(REPLICATED):
        leaves[name] = tuple(rep[kind * n_rep + n] for kind in range(4))
    loss = rep[4 * n_rep][0, 0]
    return (loss, grad_x[None], *[leaves[n][kind] for kind in range(4) for n in WEIGHT_ORDER])
```

```python
import math

import numpy as np
import jax
import jax.numpy as jnp
from jax import lax
from jax.experimental import pallas as pl
from jax.experimental.pallas import tpu as pltpu

F32 = jnp.float32
BF16 = jnp.bfloat16
MESH = pl.DeviceIdType.MESH

N_DEV = 8
S = 4096
D = 1024
D_HALF = 512
D_IN = 2560
D_FF = 2816
D_PLE = 256
CHUNK = 128
DILATIONS = ((1, 32), (4, 8), (16, 2))
ROPE_THETA = 500000.0
LN_EPS = 1e-5
ALPHA = 2.0 ** 0.25
NEG_INF = -1e30
INV_SQRT2 = 1.0 / math.sqrt(2.0)
INV_SQRT_2PI = 1.0 / math.sqrt(2.0 * math.pi)

ADAM_LR, ADAM_B1, ADAM_B2, ADAM_EPS, ADAM_WD, ADAM_STEP = 0.001, 0.9, 0.999, 1e-08, 0.01, 10

TM = 512
NT = S // TM
ATTN_UNROLL = 8
TN = 256
NJ = D_FF // TN
LANES = 128
VMEM_MIB = 1024 * 1024

BIG = (("w_in", (1024, 320), 1), ("w_o", (128, 1024), 0), ("w_ff_a", (1024, 352), 1),
       ("w_ff_b", (1024, 352), 1), ("w_ff_down", (352, 1024), 0), ("w_ple_gate", (128, 1024), 0),
       ("w_ple_in", (256, 128), 1))
WEIGHT_ORDER = ("w_in", "ln_z_g", "ln_z_b", "w_s", "b_s", "w_o", "ln1_g", "ln1_b", "w_ff_a", "w_ff_b",
                "conv_w", "conv_b", "w_ff_down", "ln2_g", "ln2_b", "w_ple_gate", "b_ple_gate",
                "w_ple_in", "ln3_g", "ln3_b")


def _params(semantics=None, vmem_mib=48):
    return pltpu.CompilerParams(dimension_semantics=semantics, vmem_limit_bytes=vmem_mib * VMEM_MIB)


def _dot(a, b):
    return jnp.dot(a, b, preferred_element_type=F32)


def _dot_nt(a, b):
    return lax.dot_general(a, b, (((1,), (1,)), ((), ())), preferred_element_type=F32)


def _dot_tn(a, b):
    return lax.dot_general(a, b, (((0,), (0,)), ((), ())), preferred_element_type=F32)


def _gelu(x):
    return 0.5 * x * (1.0 + lax.erf(x * INV_SQRT2))


def _gelu_grad(x):
    return 0.5 * (1.0 + lax.erf(x * INV_SQRT2)) + x * (jnp.exp(-0.5 * x * x) * INV_SQRT_2PI)


def _ln_stats(y):
    mu = jnp.mean(y, axis=-1, keepdims=True)
    yc = y - mu
    var = jnp.mean(yc * yc, axis=-1, keepdims=True)
    rstd = lax.rsqrt(var + LN_EPS)
    return yc * rstd, rstd


def _ln_bwd(dxhat, xhat, rstd):
    m1 = jnp.mean(dxhat, axis=-1, keepdims=True)
    m2 = jnp.mean(dxhat * xhat, axis=-1, keepdims=True)
    return rstd * (dxhat - m1 - xhat * m2)


def _colsum(x):
    return jnp.sum(x, axis=0, keepdims=True)


def _rows(i):
    return (i, 0)


def _fixed(*_):
    return (0, 0)


def _row_spec(width):
    return pl.BlockSpec((TM, width), _rows)


def _full_spec(shape):
    return pl.BlockSpec(shape, lambda *_: (0,) * len(shape))


def _owner_slot(j):
    return (j % 2) * 4 + j // 2


def _lane_lo():
    return lax.broadcasted_iota(jnp.int32, (CHUNK, LANES), 1) < 64


def _tril():
    r = lax.broadcasted_iota(jnp.int32, (CHUNK, CHUNK), 0)
    c = lax.broadcasted_iota(jnp.int32, (CHUNK, CHUNK), 1)
    return c <= r


def _rope_consts():
    lane = np.arange(LANES) % 64
    j = lane % 8
    inv = np.where(lane < 16, np.float32(ROPE_THETA) ** (-(2.0 * j).astype(np.float32) / np.float32(16.0)), 0.0)
    m_lo = (lane < 8).astype(np.float32)
    m_hi = ((lane >= 8) & (lane < 16)).astype(np.float32)
    return (jnp.asarray(inv, F32).reshape(1, LANES), jnp.asarray(m_lo).reshape(1, LANES),
            jnp.asarray(m_hi).reshape(1, LANES))


def _rope_tables(pos_col, carried=None):
    inv, m_lo, m_hi = _rope_consts()

    def body(pos_ref, inv_ref, lo_ref, hi_ref, c_ref, sa_ref, sb_ref):
        ang = pos_ref[...] * inv_ref[...]
        c = jnp.cos(ang)
        s = jnp.sin(ang)
        lo = lo_ref[...]
        hi = hi_ref[...]
        c_ref[...] = jnp.where(lo + hi > 0.0, c, 1.0)
        sa_ref[...] = s * hi
        sb_ref[...] = -s * lo

    vec = _full_spec((1, LANES))
    out = jax.ShapeDtypeStruct((S, LANES), F32)
    return _host_call(
        body, carried, name="rope_tables", grid=(NT,), out_shape=(out, out, out),
        in_specs=[pl.BlockSpec((TM, 1), _rows), vec, vec, vec],
        out_specs=(_row_spec(LANES),) * 3, scratch_shapes=[], params=_params(("arbitrary",)),
        args=(pos_col, inv, m_lo, m_hi))


def _rope(t, c, sa, sb):
    return t * c + pltpu.roll(t, 8, 1) * sa + pltpu.roll(t, LANES - 8, 1) * sb


def _rope_t(dy, c, sa, sb):
    return dy * c + pltpu.roll(dy * sa, LANES - 8, 1) + pltpu.roll(dy * sb, 8, 1)


def _masked_ws(ws_ref):
    tril = _tril()
    return [jnp.where(tril, ws_ref[g], 0.0).astype(BF16) for g in range(8)]


def _spatial_mix(zn, wm, bs):
    lo = _lane_lo()
    rows = []
    for ch in range(TM // CHUNK):
        slabs = []
        for pr in range(4):
            zp = zn[ch * CHUNK:(ch + 1) * CHUNK, pr * LANES:(pr + 1) * LANES].astype(BF16)
            slabs.append(jnp.where(lo, _dot(wm[2 * pr], zp), _dot(wm[2 * pr + 1], zp)))
        rows.append(jnp.concatenate(slabs, axis=1) + bs)
    return jnp.concatenate(rows, axis=0)


def _proj_in_fwd(x, w_in, tabs, ln_z_g, ln_z_b, w_s, bs_exp):
    def body(x_ref, w_ref, c_ref, sa_ref, sb_ref, g_ref, b_ref, ws_ref, bs_ref,
             q_ref, k_ref, v_ref, u_ref, z_ref, gm_ref, xb_ref):
        xb = x_ref[...].astype(BF16)
        xb_ref[...] = xb
        c, sa, sb = c_ref[...], sa_ref[...], sb_ref[...]
        hq = _dot(xb, w_ref[:, 0:512])
        hk = _dot(xb, w_ref[:, 512:1024])
        for s in range(4):
            sl = slice(s * LANES, (s + 1) * LANES)
            q_ref[:, sl] = _rope(hq[:, sl], c, sa, sb)
            k_ref[:, sl] = _rope(hk[:, sl], c, sa, sb)
        v_ref[...] = _dot(xb, w_ref[:, 1024:1536])
        u_pre = _dot(xb, w_ref[:, 1536:2048])
        z_pre = _dot(xb, w_ref[:, 2048:2560])
        u_ref[...] = u_pre
        z_ref[...] = z_pre
        zhat, _ = _ln_stats(_gelu(z_pre))
        zn = zhat * g_ref[...] + b_ref[...]
        mixed = _spatial_mix(zn, _masked_ws(ws_ref), bs_ref[...])
        gm_ref[...] = (_gelu(u_pre) * mixed).astype(BF16)

    half = jax.ShapeDtypeStruct((S, D_HALF), F32)
    tab = _row_spec(LANES)
    return pl.pallas_call(
        body, name="proj_in_fwd", grid=(NT,),
        out_shape=(half, half, half, half, half, jax.ShapeDtypeStruct((S, D_HALF), BF16),
                   jax.ShapeDtypeStruct((S, D), BF16)),
        in_specs=[_row_spec(D), _full_spec((D, D_IN)), tab, tab, tab, _full_spec((1, D_HALF)),
                  _full_spec((1, D_HALF)), _full_spec((8, CHUNK, CHUNK)), _full_spec((CHUNK, D_HALF))],
        out_specs=(_row_spec(D_HALF),) * 6 + (_row_spec(D),),
        compiler_params=_params(("parallel",)),
    )(x, w_in, *tabs, ln_z_g, ln_z_b, w_s, bs_exp)


def _store_band_bias(bias_ref):
    qi = lax.broadcasted_iota(jnp.int32, (CHUNK, 2 * CHUNK), 0)
    kj = lax.broadcasted_iota(jnp.int32, (CHUNK, 2 * CHUNK), 1)
    band = (kj >= qi) & (kj <= qi + CHUNK)
    bias_ref[0] = jnp.where(band, 0.0, NEG_INF)
    bias_ref[1] = jnp.where(band & (kj >= CHUNK), 0.0, NEG_INF)


def _permuted_rows(ref, d, r):
    return ref[...] if d == 1 else ref[pl.ds(r, S // d, stride=d), :]


def _attention_fwd(q, k, v, carried=None):
    def body(q_ref, k_ref, v_ref, o_ref, lse_ref, qb, kb, v0b, v1b, bias, op, lp, ob0, lb0, ob1, lb1, ob2, lb2):
        lo = _lane_lo()
        lo_f = lo.astype(F32)[0:1, :]
        hi_f = 1.0 - lo_f
        zero_pad = jnp.zeros((CHUNK, LANES), BF16)
        for buf in (qb, kb, v0b, v1b):
            buf[0:CHUNK, :] = zero_pad
        _store_band_bias(bias)
        outs = ((ob0, lb0), (ob1, lb1), (ob2, lb2))
        for (d, nb), (ob, lb) in zip(DILATIONS, outs):
            length = S // d
            for r in range(d):
                dst = slice(CHUNK + r * length, CHUNK + (r + 1) * length)
                qb[dst, :] = (_permuted_rows(q_ref, d, r) * 0.125).astype(BF16)
                kb[dst, :] = _permuted_rows(k_ref, d, r).astype(BF16)
                vs = _permuted_rows(v_ref, d, r)
                v0b[dst, :] = (vs * lo_f + hi_f).astype(BF16)
                v1b[dst, :] = (vs * hi_f + lo_f).astype(BF16)

            def block(b, carry, nb=nb):
                base = pl.multiple_of(b * CHUNK, CHUNK)
                add = bias[jnp.where(b % nb == 0, 1, 0)]
                qblk = qb[pl.ds(pl.multiple_of(base + CHUNK, CHUNK), CHUNK), :]
                kblk = kb[pl.ds(base, 2 * CHUNK), :]
                q2 = jnp.concatenate([jnp.where(lo, qblk, 0), jnp.where(lo, 0, qblk)], axis=0)
                s2 = _dot_nt(q2, kblk) + jnp.concatenate([add, add], axis=0)
                m2 = jnp.max(s2, axis=-1, keepdims=True)
                p2 = jnp.exp(s2 - m2).astype(BF16)
                pv, mx = [], []
                for head, vh in enumerate((v0b, v1b)):
                    rows = slice(head * CHUNK, (head + 1) * CHUNK)
                    pv.append(_dot(p2[rows, :], vh[pl.ds(base, 2 * CHUNK), :]))
                    mx.append(m2[rows, :])
                den = pltpu.roll(jnp.where(lo, pv[1], pv[0]), 64, 1)
                op[pl.ds(base, CHUNK), :] = jnp.where(lo, pv[0], pv[1]) / den
                lp[pl.ds(base, CHUNK), :] = jnp.where(lo, mx[0], mx[1]) + jnp.log(den)
                return carry

            lax.fori_loop(0, S // CHUNK, block, 0, unroll=ATTN_UNROLL)
            for r in range(d):
                src = slice(r * length, (r + 1) * length)
                if d == 1:
                    ob[...] = op[...]
                    lb[...] = lp[...]
                else:
                    ob[pl.ds(r, length, stride=d), :] = op[src, :]
                    lb[pl.ds(r, length, stride=d), :] = lp[src, :]
        for t in range(NT):
            rows = slice(t * TM, (t + 1) * TM)
            l0, l1, l2 = lb0[rows, :], lb1[rows, :], lb2[rows, :]
            mx = jnp.maximum(jnp.maximum(l0, l1), l2)
            e0, e1, e2 = jnp.exp(l0 - mx), jnp.exp(l1 - mx), jnp.exp(l2 - mx)
            den = e0 + e1 + e2
            o_ref[rows, :] = (e0 * ob0[rows, :] + e1 * ob1[rows, :] + e2 * ob2[rows, :]) / den
            lse_ref[rows, :] = mx + jnp.log(den)

    slab = pl.BlockSpec((S, LANES), lambda h: (0, h))
    out = jax.ShapeDtypeStruct((S, D_HALF), F32)
    padded = pltpu.VMEM((CHUNK + S, LANES), BF16)
    whole = pltpu.VMEM((S, LANES), F32)
    return _host_call(
        body, carried, name="attention_fwd", grid=(4,), out_shape=(out, out),
        in_specs=[slab, slab, slab], out_specs=(slab, slab),
        scratch_shapes=[padded] * 4 + [pltpu.VMEM((2, CHUNK, 2 * CHUNK), F32)] + [whole] * 8,
        params=_params(("arbitrary",), 56), args=(q, k, v))


def _mix_ln1_fwd(attn, gm, w_o, x, g1, b1):
    def body(a_ref, gm_ref, w_ref, x_ref, g_ref, b_ref, xhat_ref, rstd_ref, x1b_ref):
        mix = _dot(a_ref[...].astype(BF16), w_ref[0:D_HALF, :]) + _dot(gm_ref[...], w_ref[D_HALF:D, :])
        xhat, rstd = _ln_stats(ALPHA * x_ref[...] + mix)
        xhat_ref[...] = xhat
        rstd_ref[...] = rstd
        x1b_ref[...] = (xhat * g_ref[...] + b_ref[...]).astype(BF16)

    vec = _full_spec((1, D))
    return pl.pallas_call(
        body, name="mix_ln1_fwd", grid=(NT,),
        out_shape=(jax.ShapeDtypeStruct((S, D), F32), jax.ShapeDtypeStruct((S, 1), F32),
                   jax.ShapeDtypeStruct((S, D), BF16)),
        in_specs=[_row_spec(D_HALF), _row_spec(D_HALF), _full_spec((D, D)), _row_spec(D), vec, vec],
        out_specs=(_row_spec(D), pl.BlockSpec((TM, 1), _rows), _row_spec(D)),
        compiler_params=_params(("parallel",)),
    )(attn, gm, w_o, x, g1, b1)


def _ffn_up_fwd(x1b, w_a, w_b, conv_w8, conv_b, carried=None):
    def body(x_ref, wa_ref, wb_ref, cw_ref, cb_ref, ap_ref, a_ref, bl_ref, h_ref, carry):
        @pl.when(pl.program_id(1) == 0)
        def _():
            carry[...] = jnp.zeros_like(carry)

        xb = x_ref[...]
        ap = _dot(xb, wa_ref[...])
        bl = _dot(xb, wb_ref[...])
        row = lax.broadcasted_iota(jnp.int32, (TM, TN), 0)
        c6, c7 = carry[6:7, :], carry[7:8, :]
        m1 = jnp.where(row == 0, c7, pltpu.roll(ap, 1, 0))
        m2 = jnp.where(row == 0, c6, jnp.where(row == 1, c7, pltpu.roll(ap, 2, 0)))
        a = cb_ref[...] + cw_ref[0:1, :] * m2 + cw_ref[1:2, :] * m1 + cw_ref[2:3, :] * ap
        carry[...] = ap[TM - 8:TM, :]
        ap_ref[...] = ap.astype(BF16)
        a_ref[...] = a
        bl_ref[...] = bl.astype(BF16)
        h_ref[...] = (_gelu(a) * bl).astype(BF16)

    tile = pl.BlockSpec((TM, TN), lambda j, i: (i, j))
    wcol = pl.BlockSpec((D, TN), lambda j, i: (0, j))
    ff = jax.ShapeDtypeStruct((S, D_FF), F32)
    ffb = jax.ShapeDtypeStruct((S, D_FF), BF16)
    return _host_call(
        body, carried, name="ffn_up_fwd", grid=(NJ, NT),
        out_shape=(ffb, ff, ffb, ffb),
        in_specs=[pl.BlockSpec((TM, D), lambda j, i: (i, 0)), wcol, wcol,
                  pl.BlockSpec((8, TN), lambda j, i: (0, j)), pl.BlockSpec((1, TN), lambda j, i: (0, j))],
        out_specs=(tile, tile, tile, tile),
        scratch_shapes=[pltpu.VMEM((8, TN), F32)],
        params=_params(("arbitrary", "arbitrary")), args=(x1b, w_a, w_b, conv_w8, conv_b))


def _ffn_down_ln2_fwd(hff, w_down, xhat1, g1, b1):
    def body(h_ref, w_ref, xh_ref, g_ref, b_ref, xhat_ref, rstd_ref):
        x1 = xh_ref[...] * g_ref[...] + b_ref[...]
        xhat, rstd = _ln_stats(ALPHA * x1 + _dot(h_ref[...], w_ref[...]))
        xhat_ref[...] = xhat
        rstd_ref[...] = rstd

    vec = _full_spec((1, D))
    return pl.pallas_call(
        body, name="ffn_down_ln2_fwd", grid=(NT,),
        out_shape=(jax.ShapeDtypeStruct((S, D), F32), jax.ShapeDtypeStruct((S, 1), F32)),
        in_specs=[_row_spec(D_FF), _full_spec((D_FF, D)), _row_spec(D), vec, vec],
        out_specs=(_row_spec(D), pl.BlockSpec((TM, 1), _rows)),
        compiler_params=_params(("parallel",)),
    )(hff, w_down, xhat1, g1, b1)


def _tail_fwd_bwd(xhat2, rstd2, p, target, w_g, w_p, g2, b2, bg, g3, b3):
    def body(xh_ref, rs_ref, p_ref, t_ref, wg_ref, wp_ref, g2_ref, b2_ref, bg_ref, g3_ref, b3_ref,
             loss_ref, dy2_ref, dy2b_ref, gwg_ref, gwp_ref, vec_ref, dwg_ref, dwp_ref):
        @pl.when(pl.program_id(0) == 0)
        def _():
            loss_ref[...] = jnp.zeros_like(loss_ref)
            dwg_ref[...] = jnp.zeros_like(dwg_ref)
            dwp_ref[...] = jnp.zeros_like(dwp_ref)
            vec_ref[...] = jnp.zeros_like(vec_ref)

        xhat2_t = xh_ref[...]
        x2 = xhat2_t * g2_ref[...] + b2_ref[...]
        x2b = x2.astype(BF16)
        pb = p_ref[...].astype(BF16)
        gate = jax.nn.sigmoid(_dot(x2b, wg_ref[...]) + bg_ref[...])
        pin = _dot(pb, wp_ref[...])
        xhat3, rstd3 = _ln_stats(ALPHA * x2 + gate * pin)
        err = xhat3 * g3_ref[...] + b3_ref[...] - t_ref[...]
        loss_ref[...] += jnp.sum(jnp.mean(err * err, axis=-1, keepdims=True), axis=0, keepdims=True) * 0.5
        dout = err * (1.0 / D)
        dy3 = _ln_bwd(dout * g3_ref[...], xhat3, rstd3)
        dgp = dy3 * pin * gate * (1.0 - gate)
        dgpb = dgp.astype(BF16)
        dwg_ref[...] += _dot_tn(x2b, dgpb)
        dwp_ref[...] += _dot_tn(pb, (dy3 * gate).astype(BF16))
        dx2 = ALPHA * dy3 + _dot_nt(dgpb, wg_ref[...])
        dy2 = _ln_bwd(dx2 * g2_ref[...], xhat2_t, rs_ref[...])
        dy2_ref[...] = dy2
        dy2b_ref[...] = dy2.astype(BF16)
        vec_ref[0:1, :] += _colsum(dgp)
        vec_ref[1:2, :] += _colsum(dout * xhat3)
        vec_ref[2:3, :] += _colsum(dout)
        vec_ref[3:4, :] += _colsum(dx2 * xhat2_t)
        vec_ref[4:5, :] += _colsum(dx2)

        @pl.when(pl.program_id(0) == NT - 1)
        def _():
            for j in range(N_DEV):
                gwg_ref[_owner_slot(j)] = dwg_ref[LANES * j:LANES * (j + 1), :].astype(BF16)
                gwp_ref[_owner_slot(j)] = dwp_ref[:, LANES * j:LANES * (j + 1)].astype(BF16)

    vec = _full_spec((1, D))
    return pl.pallas_call(
        body, name="tail_fwd_bwd", grid=(NT,),
        out_shape=(jax.ShapeDtypeStruct((8, LANES), F32), jax.ShapeDtypeStruct((S, D), F32),
                   jax.ShapeDtypeStruct((S, D), BF16), jax.ShapeDtypeStruct((N_DEV, D // N_DEV, D), BF16),
                   jax.ShapeDtypeStruct((N_DEV, D_PLE, D // N_DEV), BF16), jax.ShapeDtypeStruct((8, D), F32)),
        in_specs=[_row_spec(D), pl.BlockSpec((TM, 1), _rows), _row_spec(D_PLE), _row_spec(D),
                  _full_spec((D, D)), _full_spec((D_PLE, D)), vec, vec, vec, vec, vec],
        out_specs=(_full_spec((8, LANES)), _row_spec(D), _row_spec(D), _full_spec((N_DEV, D // N_DEV, D)),
                   _full_spec((N_DEV, D_PLE, D // N_DEV)), _full_spec((8, D))),
        scratch_shapes=[pltpu.VMEM((D, D), F32), pltpu.VMEM((D_PLE, D), F32)],
        compiler_params=_params(("arbitrary",)),
    )(xhat2, rstd2, p, target, w_g, w_p, g2, b2, bg, g3, b3)


def _ffn_bwd(dy2b, x1b, w_down, w_a, w_b, a_pre, a, b_lin, hff, conv_w8):
    def body(dy_ref, x_ref, wd_ref, wa_ref, wb_ref, ap_ref, a_ref, bl_ref, h_ref, cw_ref,
             dwd_ref, dwa_ref, dwb_ref, dcw_ref, dx_hbm, dx_acc, carry, sem):
        j, i = pl.program_id(0), pl.program_id(1)

        @pl.when(i == 0)
        def _():
            carry[...] = jnp.zeros_like(carry)
            dwd_ref[...] = jnp.zeros_like(dwd_ref)
            dwa_ref[...] = jnp.zeros_like(dwa_ref)
            dwb_ref[...] = jnp.zeros_like(dwb_ref)
            dcw_ref[...] = jnp.zeros_like(dcw_ref)

        dyb = dy_ref[...]
        xb = x_ref[...]
        dh = _dot_nt(dyb, wd_ref[...])
        av = a_ref[...]
        dbl = dh * _gelu(av)
        da = dh * bl_ref[...].astype(F32) * _gelu_grad(av)
        row = lax.broadcasted_iota(jnp.int32, (TM, TN), 0)
        c0, c1 = carry[0:1, :], carry[1:2, :]
        p1 = jnp.where(row == TM - 1, c0, pltpu.roll(da, TM - 1, 0))
        p2 = jnp.where(row == TM - 2, c0, jnp.where(row == TM - 1, c1, pltpu.roll(da, TM - 2, 0)))
        carry[...] = da[0:8, :]
        ap = ap_ref[...].astype(F32)
        dcw_ref[3:4, :] += _colsum(da)
        dcw_ref[0:1, :] += _colsum(ap * p2)
        dcw_ref[1:2, :] += _colsum(ap * p1)
        dcw_ref[2:3, :] += _colsum(ap * da)
        dap = (cw_ref[2:3, :] * da + cw_ref[1:2, :] * p1 + cw_ref[0:1, :] * p2).astype(BF16)
        dblb = dbl.astype(BF16)
        dwa_ref[...] += _dot_tn(xb, dap)
        dwb_ref[...] += _dot_tn(xb, dblb)
        dwd_ref[...] += _dot_tn(h_ref[...], dyb)
        dx = _dot_nt(dap, wa_ref[...]) + _dot_nt(dblb, wb_ref[...])
        rows = pl.ds(pl.multiple_of((NT - 1 - i) * TM, TM), TM)

        @pl.when(j == 0)
        def _():
            dx_acc[rows, :] = dx

        @pl.when(j > 0)
        def _():
            dx_acc[rows, :] += dx

        @pl.when((j == NJ - 1) & (i == NT - 1))
        def _():
            cp = pltpu.make_async_copy(dx_acc, dx_hbm, sem)
            cp.start()
            cp.wait()

    rev_rows = lambda j, i: (NT - 1 - i, 0)
    rev_tile = pl.BlockSpec((TM, TN), lambda j, i: (NT - 1 - i, j))
    wcol = pl.BlockSpec((D, TN), lambda j, i: (0, j))
    small = pl.BlockSpec((8, TN), lambda j, i: (0, j))
    return pl.pallas_call(
        body, name="ffn_bwd", grid=(NJ, NT),
        out_shape=(jax.ShapeDtypeStruct((D_FF, D), F32), jax.ShapeDtypeStruct((D, D_FF), F32),
                   jax.ShapeDtypeStruct((D, D_FF), F32), jax.ShapeDtypeStruct((8, D_FF), F32),
                   jax.ShapeDtypeStruct((S, D), F32)),
        in_specs=[pl.BlockSpec((TM, D), rev_rows), pl.BlockSpec((TM, D), rev_rows),
                  pl.BlockSpec((TN, D), lambda j, i: (j, 0)), wcol, wcol, rev_tile, rev_tile, rev_tile,
                  rev_tile, small],
        out_specs=(pl.BlockSpec((TN, D), lambda j, i: (j, 0)), wcol, wcol, small,
                   pl.BlockSpec(memory_space=pl.ANY)),
        scratch_shapes=[pltpu.VMEM((S, D), F32), pltpu.VMEM((8, TN), F32), pltpu.SemaphoreType.DMA],
        compiler_params=_params(("arbitrary", "arbitrary"), 56),
    )(dy2b, x1b, w_down, w_a, w_b, a_pre, a, b_lin, hff, conv_w8)


def _ln1_mix_bwd(dy2, dx1_ffn, xhat1, rstd1, g1, attn, gm, w_o, carried=None):
    def body(dy2_ref, dxf_ref, xh_ref, rs_ref, g_ref, a_ref, gm_ref, w_ref,
             dy1_ref, da_ref, dlt_ref, dgm_ref, gwo_ref, vec_ref, dwo_ref):
        @pl.when(pl.program_id(0) == 0)
        def _():
            dwo_ref[...] = jnp.zeros_like(dwo_ref)
            vec_ref[...] = jnp.zeros_like(vec_ref)

        xhat = xh_ref[...]
        dx1 = ALPHA * dy2_ref[...] + dxf_ref[...]
        vec_ref[0:1, :] += _colsum(dx1 * xhat)
        vec_ref[1:2, :] += _colsum(dx1)
        dy1 = _ln_bwd(dx1 * g_ref[...], xhat, rs_ref[...])
        dy1_ref[...] = dy1
        dy1b = dy1.astype(BF16)
        dmix = _dot_nt(dy1b, w_ref[...])
        attn_t = a_ref[...]
        d_attn = dmix[:, 0:D_HALF]
        da_ref[...] = d_attn
        dgm_ref[...] = dmix[:, D_HALF:D]
        lo = (lax.broadcasted_iota(jnp.int32, (TM, LANES), 1) < 64)
        for s in range(4):
            sl = slice(s * LANES, (s + 1) * LANES)
            prod = d_attn[:, sl] * attn_t[:, sl]
            s0 = jnp.sum(jnp.where(lo, prod, 0.0), axis=-1, keepdims=True)
            s1 = jnp.sum(jnp.where(lo, 0.0, prod), axis=-1, keepdims=True)
            dlt_ref[:, sl] = jnp.where(lo, s0, s1)
        dwo_ref[0:D_HALF, :] += _dot_tn(attn_t.astype(BF16), dy1b)
        dwo_ref[D_HALF:D, :] += _dot_tn(gm_ref[...], dy1b)

        @pl.when(pl.program_id(0) == NT - 1)
        def _():
            for j in range(N_DEV):
                gwo_ref[_owner_slot(j)] = dwo_ref[LANES * j:LANES * (j + 1), :].astype(BF16)

    half = jax.ShapeDtypeStruct((S, D_HALF), F32)
    return _host_call(
        body, carried, name="ln1_mix_bwd", grid=(NT,),
        out_shape=(jax.ShapeDtypeStruct((S, D), F32), half, half, half,
                   jax.ShapeDtypeStruct((N_DEV, D // N_DEV, D), BF16), jax.ShapeDtypeStruct((8, D), F32)),
        in_specs=[_row_spec(D), _row_spec(D), _row_spec(D), pl.BlockSpec((TM, 1), _rows), _full_spec((1, D)),
                  _row_spec(D_HALF), _row_spec(D_HALF), _full_spec((D, D))],
        out_specs=(_row_spec(D), _row_spec(D_HALF), _row_spec(D_HALF), _row_spec(D_HALF),
                   _full_spec((N_DEV, D // N_DEV, D)), _full_spec((8, D))),
        scratch_shapes=[pltpu.VMEM((D, D), F32)],
        params=_params(("arbitrary",)), args=(dy2, dx1_ffn, xhat1, rstd1, g1, attn, gm, w_o))


def _gmlp_bwd(d_gm, u_pre, z_pre, ln_z_g, ln_z_b, w_s, bs_exp, carried=None):
    def body(dg_ref, u_ref, z_ref, g_ref, b_ref, ws_ref, bs_ref, du_ref, dz_ref, dws_ref, dbs_ref, vec_ref):
        @pl.when(pl.program_id(0) == 0)
        def _():
            dws_ref[...] = jnp.zeros_like(dws_ref)
            dbs_ref[...] = jnp.zeros_like(dbs_ref)
            vec_ref[...] = jnp.zeros_like(vec_ref)

        u_pre_t, z_pre_t, dgm = u_ref[...], z_ref[...], dg_ref[...]
        zhat, rstd = _ln_stats(_gelu(z_pre_t))
        zn = zhat * g_ref[...] + b_ref[...]
        wm = _masked_ws(ws_ref)
        mixed = _spatial_mix(zn, wm, bs_ref[...])
        du_ref[...] = (dgm * mixed * _gelu_grad(u_pre_t)).astype(BF16)
        dmixed = dgm * _gelu(u_pre_t)
        lo = _lane_lo()
        tril = _tril()
        group_of_lane = lax.broadcasted_iota(jnp.int32, (8, D_HALF), 1) // 64
        pick = (group_of_lane == lax.broadcasted_iota(jnp.int32, (8, D_HALF), 0)).astype(F32)
        dzn_rows = []
        for ch in range(TM // CHUNK):
            rows = slice(ch * CHUNK, (ch + 1) * CHUNK)
            dbs_ref[...] += lax.dot_general(pick, dmixed[rows, :], (((1,), (1,)), ((), ())),
                                            precision=lax.Precision.HIGHEST, preferred_element_type=F32)
            slabs = []
            for pr in range(4):
                sl = slice(pr * LANES, (pr + 1) * LANES)
                dm = dmixed[rows, sl]
                zp = zn[rows, sl].astype(BF16)
                dm_lo = jnp.where(lo, dm, 0.0).astype(BF16)
                dm_hi = jnp.where(lo, 0.0, dm).astype(BF16)
                dws_ref[2 * pr] += jnp.where(tril, _dot_nt(dm_lo, zp), 0.0)
                dws_ref[2 * pr + 1] += jnp.where(tril, _dot_nt(dm_hi, zp), 0.0)
                dmb = dm.astype(BF16)
                slabs.append(jnp.where(lo, _dot_tn(wm[2 * pr], dmb), _dot_tn(wm[2 * pr + 1], dmb)))
            dzn_rows.append(jnp.concatenate(slabs, axis=1))
        dzn = jnp.concatenate(dzn_rows, axis=0)
        vec_ref[0:1, :] += _colsum(dzn * zhat)
        vec_ref[1:2, :] += _colsum(dzn)
        dz = _ln_bwd(dzn * g_ref[...], zhat, rstd)
        dz_ref[...] = (dz * _gelu_grad(z_pre_t)).astype(BF16)

    halfb = jax.ShapeDtypeStruct((S, D_HALF), BF16)
    vec = _full_spec((1, D_HALF))
    return _host_call(
        body, carried, name="gmlp_bwd", grid=(NT,),
        out_shape=(halfb, halfb, jax.ShapeDtypeStruct((8, CHUNK, CHUNK), F32),
                   jax.ShapeDtypeStruct((8, CHUNK), F32), jax.ShapeDtypeStruct((8, D_HALF), F32)),
        in_specs=[_row_spec(D_HALF), _row_spec(D_HALF), _row_spec(D_HALF), vec, vec,
                  _full_spec((8, CHUNK, CHUNK)), _full_spec((CHUNK, D_HALF))],
        out_specs=(_row_spec(D_HALF), _row_spec(D_HALF), _full_spec((8, CHUNK, CHUNK)),
                   _full_spec((8, CHUNK)), _full_spec((8, D_HALF))),
        scratch_shapes=[], params=_params(("arbitrary",)), args=(d_gm, u_pre, z_pre, ln_z_g, ln_z_b, w_s, bs_exp))


def _attention_bwd(q, k, v, lse, d_attn, delta, tabs, carried=None):
    def body(q_ref, k_ref, v_ref, l_ref, do_ref, dl_ref, c_ref, sa_ref, sb_ref, dq_ref, dk_ref, dv_ref,
             qb, kb, vb, gb, bias, lsp, dlp, dqp, dk_own, dk_prev, dv_own, dv_prev, dqa, dka, dva):
        lo = _lane_lo()
        zero_pad = jnp.zeros((CHUNK, LANES), BF16)
        for buf in (qb, kb, vb, gb):
            buf[0:CHUNK, :] = zero_pad
        dk_prev[S:S + CHUNK, :] = jnp.zeros((CHUNK, LANES), F32)
        dv_prev[S:S + CHUNK, :] = jnp.zeros((CHUNK, LANES), F32)
        _store_band_bias(bias)
        for d, nb in DILATIONS:
            length = S // d
            for r in range(d):
                dst = slice(CHUNK + r * length, CHUNK + (r + 1) * length)
                src = slice(r * length, (r + 1) * length)
                qb[dst, :] = (_permuted_rows(q_ref, d, r) * 0.125).astype(BF16)
                kb[dst, :] = _permuted_rows(k_ref, d, r).astype(BF16)
                vb[dst, :] = _permuted_rows(v_ref, d, r).astype(BF16)
                gb[dst, :] = _permuted_rows(do_ref, d, r).astype(BF16)
                lsp[src, :] = _permuted_rows(l_ref, d, r)
                dlp[src, :] = _permuted_rows(dl_ref, d, r)

            def block(b, carry, nb=nb):
                base = pl.multiple_of(b * CHUNK, CHUNK)
                own = pl.multiple_of(base + CHUNK, CHUNK)
                add = bias[jnp.where(b % nb == 0, 1, 0)]
                qblk = qb[pl.ds(own, CHUNK), :]
                gblk = gb[pl.ds(own, CHUNK), :]
                kblk = kb[pl.ds(base, 2 * CHUNK), :]
                vblk = vb[pl.ds(base, 2 * CHUNK), :]
                lse_t = lsp[pl.ds(base, CHUNK), :]
                dlt_t = dlp[pl.ds(base, CHUNK), :]
                q2 = jnp.concatenate([jnp.where(lo, qblk, 0), jnp.where(lo, 0, qblk)], axis=0)
                g2 = jnp.concatenate([jnp.where(lo, gblk, 0), jnp.where(lo, 0, gblk)], axis=0)
                lse2 = jnp.concatenate([lse_t[:, 0:1], lse_t[:, 64:65]], axis=0)
                dlt2 = jnp.concatenate([dlt_t[:, 0:1], dlt_t[:, 64:65]], axis=0)
                add2 = jnp.concatenate([add, add], axis=0)
                p = jnp.exp(_dot_nt(q2, kblk) + add2 - lse2)
                ds = (p * (_dot_nt(g2, vblk) - dlt2)).astype(BF16)
                dv_blk = _dot_tn(p.astype(BF16), g2)
                dk_blk = _dot_tn(ds, q2)
                dq2 = _dot(ds, kblk)
                dqp[pl.ds(base, CHUNK), :] = jnp.where(lo, dq2[0:CHUNK, :], dq2[CHUNK:2 * CHUNK, :]) * 0.125
                dk_prev[pl.ds(base, CHUNK), :] = dk_blk[0:CHUNK, :]
                dk_own[pl.ds(own, CHUNK), :] = dk_blk[CHUNK:2 * CHUNK, :]
                dv_prev[pl.ds(base, CHUNK), :] = dv_blk[0:CHUNK, :]
                dv_own[pl.ds(own, CHUNK), :] = dv_blk[CHUNK:2 * CHUNK, :]
                return carry

            lax.fori_loop(0, S // CHUNK, block, 0, unroll=ATTN_UNROLL)
            for r in range(d):
                src = slice(r * length, (r + 1) * length)
                pad = slice(CHUNK + r * length, CHUNK + (r + 1) * length)
                if d == 1:
                    dqa[...] = dqp[...]
                    dka[...] = dk_own[pad, :] + dk_prev[pad, :]
                    dva[...] = dv_own[pad, :] + dv_prev[pad, :]
                else:
                    dst = pl.ds(r, length, stride=d)
                    dqa[dst, :] = dqa[dst, :] + dqp[src, :]
                    dka[dst, :] = dka[dst, :] + (dk_own[pad, :] + dk_prev[pad, :])
                    dva[dst, :] = dva[dst, :] + (dv_own[pad, :] + dv_prev[pad, :])
        for t in range(NT):
            rows = slice(t * TM, (t + 1) * TM)
            c, sa, sb = c_ref[rows, :], sa_ref[rows, :], sb_ref[rows, :]
            dq_ref[rows, :] = _rope_t(dqa[rows, :], c, sa, sb).astype(BF16)
            dk_ref[rows, :] = _rope_t(dka[rows, :], c, sa, sb).astype(BF16)
            dv_ref[rows, :] = dva[rows, :].astype(BF16)

    slab = pl.BlockSpec((S, LANES), lambda h: (0, h), pipeline_mode=pl.Buffered(1))
    tab = pl.BlockSpec((S, LANES), lambda h: (0, 0), pipeline_mode=pl.Buffered(1))
    out_slab = pl.BlockSpec((S, LANES), lambda h: (0, h))
    out = jax.ShapeDtypeStruct((S, D_HALF), BF16)
    padded_b = pltpu.VMEM((CHUNK + S, LANES), BF16)
    padded_f = pltpu.VMEM((CHUNK + S, LANES), F32)
    whole = pltpu.VMEM((S, LANES), F32)
    return _host_call(
        body, carried, name="attention_bwd", grid=(4,), out_shape=(out, out, out),
        in_specs=[slab] * 6 + [tab] * 3, out_specs=(out_slab,) * 3,
        scratch_shapes=[padded_b] * 4 + [pltpu.VMEM((2, CHUNK, 2 * CHUNK), F32)] + [whole] * 3
        + [padded_f] * 4 + [whole] * 3,
        params=_params(("arbitrary",), 60), args=(q, k, v, lse, d_attn, delta, *tabs))


def _proj_in_bwd_w(xb, parts):
    def body(x_ref, p0, p1, p2, p3, p4, gw_ref, dw_ref):
        @pl.when(pl.program_id(0) == 0)
        def _():
            dw_ref[...] = jnp.zeros_like(dw_ref)

        xt = x_ref[...]
        for n, part in enumerate((p0, p1, p2, p3, p4)):
            dw_ref[:, n * D_HALF:(n + 1) * D_HALF] += _dot_tn(xt, part[...])

        @pl.when(pl.program_id(0) == NT - 1)
        def _():
            width = D_IN // N_DEV
            for j in range(N_DEV):
                gw_ref[_owner_slot(j)] = dw_ref[:, width * j:width * (j + 1)].astype(BF16)

    return pl.pallas_call(
        body, name="proj_in_bwd_w", grid=(NT,), out_shape=jax.ShapeDtypeStruct((N_DEV, D, D_IN // N_DEV), BF16),
        in_specs=[_row_spec(D)] + [_row_spec(D_HALF)] * 5, out_specs=_full_spec((N_DEV, D, D_IN // N_DEV)),
        scratch_shapes=[pltpu.VMEM((D, D_IN), F32)],
        compiler_params=_params(("arbitrary",)),
    )(xb, *parts)


def _proj_in_bwd_x(dy1, parts, w_in, carried=None):
    def body(dy_ref, p0, p1, p2, p3, p4, w_ref, gx_ref):
        acc = ALPHA * dy_ref[...]
        for n, part in enumerate((p0, p1, p2, p3, p4)):
            acc += _dot_nt(part[...], w_ref[:, n * D_HALF:(n + 1) * D_HALF])
        gx_ref[...] = acc

    return _host_call(
        body, carried, name="proj_in_bwd_x", grid=(NT,), out_shape=(jax.ShapeDtypeStruct((S, D), F32),),
        in_specs=[_row_spec(D)] + [_row_spec(D_HALF)] * 5 + [_full_spec((D, D_IN))], out_specs=(_row_spec(D),),
        scratch_shapes=[], params=_params(("arbitrary",)), args=(dy1, *parts, w_in))


def _to_natural(blocks, name):
    n, rows, w = blocks.shape
    tile = min(rows, 256)

    def body(i_ref, o_ref):
        o_ref[...] = jnp.concatenate([i_ref[j] for j in range(n)], axis=1)

    return pl.pallas_call(
        body, name=name, grid=(rows // tile,), out_shape=jax.ShapeDtypeStruct((rows, n * w), blocks.dtype),
        in_specs=[pl.BlockSpec((n, tile, w), lambda i: (0, i, 0))],
        out_specs=pl.BlockSpec((tile, n * w), lambda i: (i, 0)), compiler_params=_params(("parallel",)),
    )(blocks)


def _column_blocks(full, name):
    rows, cols = full.shape
    w = cols // N_DEV
    tile = 256

    def body(i_ref, o_ref):
        for j in range(N_DEV):
            o_ref[_owner_slot(j)] = i_ref[:, j * w:(j + 1) * w].astype(BF16)

    return pl.pallas_call(
        body, name=name, grid=(rows // tile,), out_shape=jax.ShapeDtypeStruct((N_DEV, rows, w), BF16),
        in_specs=[pl.BlockSpec((tile, cols), lambda i: (i, 0))],
        out_specs=pl.BlockSpec((N_DEV, tile, w), lambda i: (0, i, 0)), compiler_params=_params(("parallel",)),
    )(full)


def _row_blocks(full):
    rows, cols = full.shape
    r = rows // N_DEV

    def body(i_ref, o_ref):
        o_ref[0] = i_ref[...].astype(BF16)

    return pl.pallas_call(
        body, name="ff_down_grad_blocks", grid=(N_DEV,), out_shape=jax.ShapeDtypeStruct((N_DEV, r, cols), BF16),
        in_specs=[pl.BlockSpec((r, cols), lambda s: ((s % 4) * 2 + s // 4, 0))],
        out_specs=pl.BlockSpec((1, r, cols), lambda s: (s, 0, 0)), compiler_params=_params(("parallel",)),
    )(full)


def _local_step(x, p, pos_col, target, sm, ex):
    bs_exp = jnp.repeat(sm["b_s"].T, 64, axis=1)
    tabs, got = _rope_tables(pos_col, ex.gather_input())
    w_in = ex.weight_input(got)
    q, k, v, u_pre, z_pre, gm, xb = _proj_in_fwd(x, w_in, tabs, sm["ln_z_g"], sm["ln_z_b"], sm["w_s"], bs_exp)
    (attn, lse), got = _attention_fwd(q, k, v, ex.gather_first())
    wa = ex.weights_first(got)
    xhat1, rstd1, x1b = _mix_ln1_fwd(attn, gm, wa["w_o"], x, sm["ln1_g"], sm["ln1_b"])
    (a_pre, a, b_lin, hff), got = _ffn_up_fwd(x1b, wa["w_ff_a"], wa["w_ff_b"], wa["conv_w8"], sm["conv_b"],
                                              ex.gather_second())
    wc = ex.weights_second(got)
    xhat2, rstd2 = _ffn_down_ln2_fwd(hff, wc["w_ff_down"], xhat1, sm["ln1_g"], sm["ln1_b"])
    loss, dy2, dy2b, dw_g, dw_p, vec_tail = _tail_fwd_bwd(
        xhat2, rstd2, p, target, wc["w_ple_gate"], wc["w_ple_in"], sm["ln2_g"], sm["ln2_b"],
        sm["b_ple_gate"], sm["ln3_g"], sm["ln3_b"])
    dw_down, dw_a, dw_b, dconv, dx1_ffn = _ffn_bwd(dy2b, x1b, wc["w_ff_down"], wa["w_ff_a"], wa["w_ff_b"],
                                                    a_pre, a, b_lin, hff, wa["conv_w8"])
    (dy1, d_attn, delta, d_gm, dw_o, vec_ln1), _ = _ln1_mix_bwd(
        dy2, dx1_ffn, xhat1, rstd1, sm["ln1_g"], attn, gm, wa["w_o"])
    early = {"w_ff_a": _column_blocks(dw_a, "ff_a_grad_blocks"), "w_ff_b": _column_blocks(dw_b, "ff_b_grad_blocks"),
             "w_ff_down": _row_blocks(dw_down), "w_ple_gate": dw_g, "w_ple_in": dw_p, "w_o": dw_o}
    (du, dz, dws, dbs, vec_z), got = _gmlp_bwd(d_gm, u_pre, z_pre, sm["ln_z_g"], sm["ln_z_b"], sm["w_s"], bs_exp,
                                               ex.to_sibling(early))
    chip_sums = ex.reduce_on_chip(got)
    small = {"tail": vec_tail, "ln1": vec_ln1, "ln_z": vec_z, "conv": dconv, "w_s": dws, "b_s": dbs, "loss": loss}
    (dq, dk, dv), got_early = _attention_bwd(q, k, v, lse, d_attn, delta, tabs,
                                             ex.between_chips(chip_sums, small))
    parts = (dq, dk, dv, du, dz)
    (grad_x,), got_late = _proj_in_bwd_x(dy1, parts, w_in, ex.last(_proj_in_bwd_w(xb, parts)))
    return grad_x, ex.collect(got_early, got_late)


def _mesh_pos():
    return lax.axis_index("x"), lax.axis_index("y"), lax.axis_index("c")


def _cast_shards(shards):
    n = len(shards)

    def body(*refs):
        for a in range(n):
            refs[n + a][...] = refs[a][...].astype(BF16)

    whole = [_full_spec(s.shape) for s in shards]
    return pl.pallas_call(
        body, name="cast_shards", grid=(1,), out_shape=tuple(jax.ShapeDtypeStruct(s.shape, BF16) for s in shards),
        in_specs=whole, out_specs=tuple(whole), compiler_params=_params(("arbitrary",)),
    )(*shards)


class _GatherComm:
    def __init__(self, shards):
        n = len(shards)
        self.inputs = list(shards)
        self.out_shapes = [jax.ShapeDtypeStruct((N_DEV,) + s.shape, s.dtype) for s in shards]
        self.scratch = [pltpu.SemaphoreType.DMA((7 * n,)), pltpu.SemaphoreType.DMA((7 * n,)),
                        pltpu.SemaphoreType.DMA((n,))]

    def phases(self, x_refs, out_refs, sems):
        send_sems, recv_sems, local_sems = sems
        n_arr = len(x_refs)

        def where():
            x, y, c = _mesh_pos()
            return (x, y, c), (x, y, 1 - c), [(1 - x, y), (x, 1 - y), (1 - x, 1 - y)]

        def copy(a, n, block, to, from_shard=False):
            dst = out_refs[a].at[4 * block[0] + 2 * block[1] + block[2]]
            return pltpu.make_async_remote_copy(
                src_ref=x_refs[a] if from_shard else dst, dst_ref=dst, send_sem=send_sems.at[7 * a + n],
                recv_sem=recv_sems.at[7 * a + n], device_id=to, device_id_type=MESH)

        def local(a):
            x, y, c = _mesh_pos()
            return pltpu.make_async_copy(x_refs[a], out_refs[a].at[4 * x + 2 * y + c], local_sems.at[a])

        def start():
            me, sibling, chips = where()
            for a in range(n_arr):
                local(a).start()
                copy(a, 0, me, sibling, from_shard=True).start()
                for n, chip in enumerate(chips):
                    copy(a, 1 + n, me, (*chip, me[2]), from_shard=True).start()

        def forward():
            me, sibling, chips = where()
            for n, chip in enumerate(chips):
                for a in range(n_arr):
                    copy(a, 1 + n, (*chip, me[2]), me).wait_recv()
                    copy(a, 4 + n, (*chip, me[2]), sibling).start()

        def finish():
            me, sibling, chips = where()
            for a in range(n_arr):
                copy(a, 0, sibling, me).wait_recv()
                copy(a, 0, me, sibling, from_shard=True).wait_send()
                for n, chip in enumerate(chips):
                    copy(a, 4 + n, (*chip, 1 - me[2]), me).wait_recv()
                    copy(a, 1 + n, me, (*chip, me[2]), from_shard=True).wait_send()
                    copy(a, 4 + n, (*chip, me[2]), sibling).wait_send()
                local(a).wait()

        return {"start": start, "forward": forward, "finish": finish}


class _SiblingComm:
    def __init__(self, big):
        n = len(big)
        self.inputs = list(big)
        self.out_shapes = [jax.ShapeDtypeStruct(b.shape[1:], b.dtype) for b in big]
        self.scratch = [pltpu.SemaphoreType.DMA((n,)), pltpu.SemaphoreType.DMA((n,))]

    def phases(self, src, dst, sems):
        send_sems, recv_sems = sems

        def copies():
            x, y, c = _mesh_pos()
            return [pltpu.make_async_remote_copy(
                src_ref=src[a].at[1 - c], dst_ref=dst[a], send_sem=send_sems.at[a], recv_sem=recv_sems.at[a],
                device_id=(x, y, 1 - c), device_id_type=MESH) for a in range(len(src))]

        def start():
            for cp in copies():
                cp.start()

        def finish():
            for cp in copies():
                cp.wait()

        return {"start": start, "finish": finish}


class _ChipComm:
    def __init__(self, sums):
        n = len(sums)
        self.inputs = list(sums)
        self.out_shapes = [jax.ShapeDtypeStruct(s.shape, s.dtype) for s in sums]
        self.scratch = [pltpu.SemaphoreType.DMA((3 * n,)), pltpu.SemaphoreType.DMA((3 * n,)),
                        pltpu.SemaphoreType.DMA((n,))]

    def phases(self, src, dst, sems):
        send_sems, recv_sems, local_sems = sems

        def copies():
            x, y, c = _mesh_pos()
            my_chip = 2 * x + y
            out = [pltpu.make_async_copy(src[a].at[my_chip], dst[a].at[my_chip], local_sems.at[a])
                   for a in range(len(src))]
            for n, (px, py) in enumerate([(1 - x, y), (x, 1 - y), (1 - x, 1 - y)]):
                for a in range(len(src)):
                    out.append(pltpu.make_async_remote_copy(
                        src_ref=src[a].at[2 * px + py], dst_ref=dst[a].at[my_chip],
                        send_sem=send_sems.at[3 * a + n], recv_sem=recv_sems.at[3 * a + n],
                        device_id=(px, py, c), device_id_type=MESH))
            return out

        def start():
            for cp in copies():
                cp.start()

        def finish():
            for cp in copies():
                cp.wait()

        return {"start": start, "finish": finish}


class _ScatterComm:
    def __init__(self, blocks, small):
        self.n_big, self.n_small = len(blocks), len(small)
        n = self.n_big + self.n_small
        self.inputs = list(blocks) + list(small)
        self.out_shapes = ([jax.ShapeDtypeStruct(b.shape, b.dtype) for b in blocks]
                           + [jax.ShapeDtypeStruct((N_DEV,) + s.shape, s.dtype) for s in small])
        self.scratch = [pltpu.SemaphoreType.DMA((7 * n,)), pltpu.SemaphoreType.DMA((7 * n,)),
                        pltpu.SemaphoreType.DMA((n,))]

    def phases(self, src, dst, sems):
        send_sems, recv_sems, local_sems = sems
        n_big, n_all = self.n_big, self.n_big + self.n_small

        def source(a, core, chip):
            return src[a].at[core * 4 + chip] if a < n_big else src[a]

        def copies():
            x, y, c = _mesh_pos()
            me = 4 * x + 2 * y + c
            out = [pltpu.make_async_copy(source(a, c, 2 * x + y), dst[a].at[me], local_sems.at[a])
                   for a in range(n_all)]
            for flip in range(1, N_DEV):
                px = 1 - x if flip & 4 else x
                py = 1 - y if flip & 2 else y
                pc = 1 - c if flip & 1 else c
                for a in range(n_all):
                    n = 7 * a + flip - 1
                    out.append(pltpu.make_async_remote_copy(
                        src_ref=source(a, pc, 2 * px + py), dst_ref=dst[a].at[me], send_sem=send_sems.at[n],
                        recv_sem=recv_sems.at[n], device_id=(px, py, pc), device_id_type=MESH))
            return out

        def start():
            for cp in copies():
                cp.start()

        def finish():
            for cp in copies():
                cp.wait()

        return {"start": start, "finish": finish}


class _Both:
    def __init__(self, first, second):
        self.parts = (first, second)
        self.inputs = first.inputs + second.inputs
        self.out_shapes = first.out_shapes + second.out_shapes
        self.scratch = first.scratch + second.scratch

    def phases(self, src, dst, sems):
        a, b = self.parts
        pa = a.phases(src[:len(a.inputs)], dst[:len(a.out_shapes)], sems[:len(a.scratch)])
        pb = b.phases(src[len(a.inputs):], dst[len(a.out_shapes):], sems[len(a.scratch):])

        def both(name):
            def run():
                pa[name]()
                pb[name]()
            return run

        return {name: both(name) for name in pa}


def _host_call(body, carried, *, name, grid, out_shape, in_specs, out_specs, scratch_shapes, params, args):
    if carried is None:
        return pl.pallas_call(body, name=name, grid=grid, out_shape=tuple(out_shape), in_specs=list(in_specs),
                              out_specs=tuple(out_specs), scratch_shapes=list(scratch_shapes),
                              compiler_params=params)(*args), ()
    comm, when = carried
    n_in, n_out, n_scratch = len(in_specs), len(out_shape), len(scratch_shapes)
    k_in, k_out = len(comm.inputs), len(comm.out_shapes)

    def wrapped(*refs):
        bounds = np.cumsum([0, n_in, k_in, n_out, k_out, n_scratch])
        ins, c_in, outs, c_out, scr = (refs[bounds[i]:bounds[i + 1]] for i in range(5))
        phases = comm.phases(c_in, c_out, refs[bounds[5]:])
        for phase, cond in when("before"):
            pl.when(cond)(phases[phase])
        body(*ins, *outs, *scr)
        for phase, cond in when("after"):
            pl.when(cond)(phases[phase])

    anywhere = pl.BlockSpec(memory_space=pl.ANY)
    results = pl.pallas_call(
        wrapped, name=name, grid=grid, out_shape=tuple(out_shape) + tuple(comm.out_shapes),
        in_specs=list(in_specs) + [anywhere] * k_in, out_specs=tuple(out_specs) + (anywhere,) * k_out,
        scratch_shapes=list(scratch_shapes) + comm.scratch, compiler_params=params,
    )(*args, *comm.inputs)
    return results[:n_out], results[n_out:]


class _Exchanges:
    FIRST = ("w_o", "w_ff_a", "w_ff_b")
    SECOND = ("w_ff_down", "w_ple_gate", "w_ple_in")
    EARLY = ("w_ff_a", "w_ff_b", "w_ff_down", "w_ple_gate", "w_ple_in", "w_o")
    LATE = ("w_in",)

    def __init__(self, shards, conv_rows):
        self.shards, self.conv_rows = shards, conv_rows
        self.axis = {name: axis for name, _, axis in BIG}

    def _natural(self, name, blocks):
        n, r, c = blocks.shape
        return blocks.reshape(n * r, c) if self.axis[name] == 0 else _to_natural(blocks, name + "_natural")

    def gather_input(self):
        def when(position):
            step = pl.program_id(0)
            if position == "before":
                return [("start", step == 0)]
            return [("forward", step == NT - 1), ("finish", step == NT - 1)]
        return _GatherComm([self.shards["w_in"]]), when

    def weight_input(self, got):
        return self._natural("w_in", got[0])

    def gather_first(self):
        comm = _GatherComm([self.shards[n] for n in self.FIRST] + [self.conv_rows])

        def when(position):
            step = pl.program_id(0)
            if position == "before":
                return [("start", step == 0), ("forward", step == 3)]
            return [("finish", step == 3)]
        return comm, when

    def weights_first(self, got):
        out = {name: self._natural(name, blocks) for name, blocks in zip(self.FIRST, got)}
        out["conv_w8"] = _to_natural(got[-1], "conv_w_natural")
        return out

    def gather_second(self):
        comm = _GatherComm([self.shards[n] for n in self.SECOND])

        def when(position):
            j, i = pl.program_id(0), pl.program_id(1)
            if position == "before":
                return [("start", (j == 0) & (i == 0)), ("forward", (j == NJ - 3) & (i == 0))]
            return [("finish", (j == NJ - 1) & (i == NT - 1))]
        return comm, when

    def weights_second(self, got):
        return {name: self._natural(name, blocks) for name, blocks in zip(self.SECOND, got)}

    def to_sibling(self, early):
        self.by_core = [early[n].reshape((2, 4) + early[n].shape[1:]) for n in self.EARLY]
        return _SiblingComm(self.by_core), _first_and_last(NT)

    def reduce_on_chip(self, from_sibling):
        core = lax.axis_index("c").astype(jnp.int32).reshape(1)
        return _chip_reduce(self.by_core, from_sibling, core, "chip_reduce")

    def between_chips(self, chip_sums, small):
        self.small_keys = tuple(small)
        return _Both(_ChipComm(chip_sums), _ScatterComm([], [small[k] for k in self.small_keys])), _first_and_last(4)

    def last(self, dw_in):
        by_core = [dw_in.reshape((2, 4) + dw_in.shape[1:])]
        core = lax.axis_index("c").astype(jnp.int32).reshape(1)
        sums = _chip_reduce(by_core, _standalone(_SiblingComm(by_core), "w_in_grad_to_sibling"), core, "w_in_chip_reduce")
        return _ChipComm(sums), _first_and_last(NT)

    def collect(self, got_early, got_late):
        parts = dict(zip(self.EARLY, got_early[:len(self.EARLY)]))
        parts.update(zip(self.LATE, got_late))
        return parts, dict(zip(self.small_keys, got_early[len(self.EARLY):]))


def _first_and_last(n_steps):
    def when(position):
        step = pl.program_id(0)
        return [("start", step == 0)] if position == "before" else [("finish", step == n_steps - 1)]
    return when


def _standalone(comm, name):
    n_in = len(comm.inputs)

    def body(*refs):
        phases = comm.phases(refs[:n_in], refs[n_in:n_in + len(comm.out_shapes)], refs[n_in + len(comm.out_shapes):])
        phases["start"]()
        phases["finish"]()

    anywhere = pl.BlockSpec(memory_space=pl.ANY)
    return pl.pallas_call(
        body, name=name, out_shape=tuple(comm.out_shapes), in_specs=[anywhere] * n_in,
        out_specs=(anywhere,) * len(comm.out_shapes), scratch_shapes=comm.scratch,
    )(*comm.inputs)


def _chip_reduce(big, from_sibling, core, name):
    n = len(big)

    def body(core_ref, *refs):
        for a in range(n):
            mine, theirs, out = refs[a], refs[n + a], refs[2 * n + a]
            out[0] = (mine[0, 0].astype(F32) + theirs[0].astype(F32)).astype(BF16)

    def block(shape):
        return pl.BlockSpec((1,) + shape, lambda ch, core_ref: (ch, 0, 0))

    grid_spec = pltpu.PrefetchScalarGridSpec(
        num_scalar_prefetch=1, grid=(4,),
        in_specs=[pl.BlockSpec((1, 1) + b.shape[2:], lambda ch, core_ref: (core_ref[0], ch, 0, 0)) for b in big]
        + [block(b.shape[2:]) for b in big],
        out_specs=[block(b.shape[2:]) for b in big])
    return pl.pallas_call(
        body, name=name, grid_spec=grid_spec,
        out_shape=tuple(jax.ShapeDtypeStruct(b.shape[1:], BF16) for b in big),
        compiler_params=_params(("parallel",)),
    )(core, *big, *from_sibling)


def _adamw(g, w, m, v):
    nm = ADAM_B1 * m + (1.0 - ADAM_B1) * g
    nv = ADAM_B2 * v + (1.0 - ADAM_B2) * (g * g)
    m_hat = nm / (1.0 - ADAM_B1 ** ADAM_STEP)
    v_hat = nv / (1.0 - ADAM_B2 ** ADAM_STEP)
    return -ADAM_LR * (m_hat / (jnp.sqrt(v_hat) + ADAM_EPS) + ADAM_WD * w), nm, nv


def _adamw_sharded(parts, w, m, v, name):
    def body(p_ref, w_ref, m_ref, v_ref, g_ref, d_ref, nm_ref, nv_ref):
        g = p_ref[0].astype(F32)
        for s in range(1, parts.shape[0]):
            g = g + p_ref[s].astype(F32)
        delta, nm, nv = _adamw(g, w_ref[0], m_ref[0], v_ref[0])
        g_ref[0] = g
        d_ref[0] = delta
        nm_ref[0] = nm
        nv_ref[0] = nv

    n, r, c = parts.shape
    steps = 4 if r % 64 == 0 and r >= 512 else (2 if r % 32 == 0 and r >= 256 else 1)
    tile = pl.BlockSpec((1, r // steps, c), lambda i: (0, i, 0))
    return pl.pallas_call(
        body, name=name, grid=(steps,), out_shape=(jax.ShapeDtypeStruct(w.shape, F32),) * 4,
        in_specs=[pl.BlockSpec((n, r // steps, c), lambda i: (0, i, 0)), tile, tile, tile], out_specs=(tile,) * 4,
        compiler_params=_params(("parallel",)),
    )(parts, w, m, v)


REPLICATED = (("ln_z_g", "ln_z", 0), ("ln_z_b", "ln_z", 1), ("w_s", "w_s", None), ("b_s", "b_s", None),
              ("ln1_g", "ln1", 0), ("ln1_b", "ln1", 1), ("conv_w", "conv_mine", None), ("conv_b", "conv", 3),
              ("ln2_g", "tail", 3), ("ln2_b", "tail", 4), ("b_ple_gate", "tail", 0), ("ln3_g", "tail", 1),
              ("ln3_b", "tail", 2))
GATHERED = ("tail", "ln1", "ln_z", "conv", "w_s", "b_s", "loss", "conv_mine")


def _adamw_replicated(gathered, w, m, v):
    n_par = len(REPLICATED)

    def body(*refs):
        srcs = dict(zip(GATHERED, refs[:len(GATHERED)]))
        rest = refs[len(GATHERED):]
        w_refs, m_refs, v_refs = rest[:n_par], rest[n_par:2 * n_par], rest[2 * n_par:3 * n_par]
        outs = rest[3 * n_par:]
        loss_ref = outs[4 * n_par]
        sums = {}
        for key, ref in srcs.items():
            total = ref[0]
            for dev in range(1, N_DEV):
                total = total + ref[dev]
            sums[key] = total
        loss_ref[...] = sums["loss"]
        for n, (name, key, row) in enumerate(REPLICATED):
            if name == "conv_w":
                g = sums[key][0:3, :]
            elif row is None:
                g = sums[key]
            else:
                g = sums[key][row:row + 1, :]
            lead = len(w_refs[n].shape) - g.ndim
            idx = (0,) * lead + (Ellipsis,)
            delta, nm, nv = _adamw(g, w_refs[n][idx], m_refs[n][idx], v_refs[n][idx])
            for kind, val in enumerate((g, delta, nm, nv)):
                outs[kind * n_par + n][idx] = val

    names = [name for name, _, _ in REPLICATED]
    shapes = [jax.ShapeDtypeStruct(w[name].shape, F32) for name in names]
    args = [gathered[k] for k in GATHERED] + [w[n] for n in names] + [m[n] for n in names] + [v[n] for n in names]
    out_shape = tuple(shapes * 4) + (jax.ShapeDtypeStruct((8, LANES), F32),)
    return pl.pallas_call(
        body, name="adamw_replicated", grid=(1,), out_shape=out_shape,
        in_specs=[_full_spec(a.shape) for a in args], out_specs=tuple(_full_spec(s.shape) for s in out_shape),
        compiler_params=_params(("arbitrary",)),
    )(*args)


def kernel(x, p, positions, w_in, ln_z_g, ln_z_b, w_s, b_s, w_o, ln1_g, ln1_b, w_ff_a, w_ff_b, conv_w, conv_b, w_ff_down, ln2_g, ln2_b, w_ple_gate, b_ple_gate, w_ple_in, ln3_g, ln3_b, loss_target, m_w_in, m_ln_z_g, m_ln_z_b, m_w_s, m_b_s, m_w_o, m_ln1_g, m_ln1_b, m_w_ff_a, m_w_ff_b, m_conv_w, m_conv_b, m_w_ff_down, m_ln2_g, m_ln2_b, m_w_ple_gate, m_b_ple_gate, m_w_ple_in, m_ln3_g, m_ln3_b, v_w_in, v_ln_z_g, v_ln_z_b, v_w_s, v_b_s, v_w_o, v_ln1_g, v_ln1_b, v_w_ff_a, v_w_ff_b, v_conv_w, v_conv_b, v_w_ff_down, v_ln2_g, v_ln2_b, v_w_ple_gate, v_b_ple_gate, v_w_ple_in, v_ln3_g, v_ln3_b):
    w = dict(w_in=w_in, ln_z_g=ln_z_g, ln_z_b=ln_z_b, w_s=w_s, b_s=b_s, w_o=w_o, ln1_g=ln1_g, ln1_b=ln1_b,
             w_ff_a=w_ff_a, w_ff_b=w_ff_b, conv_w=conv_w, conv_b=conv_b, w_ff_down=w_ff_down, ln2_g=ln2_g,
             ln2_b=ln2_b, w_ple_gate=w_ple_gate, b_ple_gate=b_ple_gate, w_ple_in=w_ple_in, ln3_g=ln3_g,
             ln3_b=ln3_b)
    m = dict(w_in=m_w_in, ln_z_g=m_ln_z_g, ln_z_b=m_ln_z_b, w_s=m_w_s, b_s=m_b_s, w_o=m_w_o, ln1_g=m_ln1_g,
             ln1_b=m_ln1_b, w_ff_a=m_w_ff_a, w_ff_b=m_w_ff_b, conv_w=m_conv_w, conv_b=m_conv_b,
             w_ff_down=m_w_ff_down, ln2_g=m_ln2_g, ln2_b=m_ln2_b, w_ple_gate=m_w_ple_gate,
             b_ple_gate=m_b_ple_gate, w_ple_in=m_w_ple_in, ln3_g=m_ln3_g, ln3_b=m_ln3_b)
    v = dict(w_in=v_w_in, ln_z_g=v_ln_z_g, ln_z_b=v_ln_z_b, w_s=v_w_s, b_s=v_b_s, w_o=v_w_o, ln1_g=v_ln1_g,
             ln1_b=v_ln1_b, w_ff_a=v_w_ff_a, w_ff_b=v_w_ff_b, conv_w=v_conv_w, conv_b=v_conv_b,
             w_ff_down=v_w_ff_down, ln2_g=v_ln2_g, ln2_b=v_ln2_b, w_ple_gate=v_w_ple_gate,
             b_ple_gate=v_b_ple_gate, w_ple_in=v_w_ple_in, ln3_g=v_ln3_g, ln3_b=v_ln3_b)
    big_names = [name for name, _, _ in BIG]
    small_names = ("ln_z_g", "ln_z_b", "w_s", "b_s", "ln1_g", "ln1_b", "conv_b", "ln2_g", "ln2_b", "b_ple_gate",
                   "ln3_g", "ln3_b")

    shards = dict(zip(big_names, _cast_shards([w[n][0] for n in big_names])))
    conv_rows = jnp.pad(w["conv_w"][0], ((0, 5), (0, 0)))
    sm = {n: w[n][0] if w[n].ndim > 2 else w[n] for n in small_names}
    pos_col = positions.reshape(S, 1).astype(F32)
    grad_x, (parts, small_all) = _local_step(x[0], p[0, 0], pos_col, loss_target[0], sm,
                                             _Exchanges(shards, conv_rows))
    me = 4 * lax.axis_index("x") + 2 * lax.axis_index("y") + lax.axis_index("c")
    conv_cols = small_all["conv"].reshape(N_DEV, 8, N_DEV, D_FF // N_DEV)
    small_all["conv_mine"] = lax.dynamic_index_in_dim(conv_cols, me, axis=2, keepdims=False)

    leaves = {}
    for name in big_names:
        leaves[name] = _adamw_sharded(parts[name], w[name], m[name], v[name], "adamw_" + name)
    rep = _adamw_replicated(small_all, w, m, v)
    n_rep = len(REPLICATED)
    for n, (name, _, _) in enumerate(REPLICATED):
        leaves[name] = tuple(rep[kind * n_rep + n] for kind in range(4))
    loss = rep[4 * n_rep][0, 0]
    return (loss, grad_x[None], *[leaves[n][kind] for kind in range(4) for n in WEIGHT_ORDER])
```

```python
import math

import numpy as np
import jax
import jax.numpy as jnp
from jax import lax
from jax.experimental import pallas as pl
from jax.experimental.pallas import tpu as pltpu

F32 = jnp.float32
BF16 = jnp.bfloat16
MESH = pl.DeviceIdType.MESH

N_DEV = 8
S = 4096
D = 1024
D_HALF = 512
D_IN = 2560
D_FF = 2816
D_PLE = 256
CHUNK = 128
DILATIONS = ((1, 32), (4, 8), (16, 2))
ROPE_THETA = 500000.0
LN_EPS = 1e-5
ALPHA = 2.0 ** 0.25
NEG_INF = -1e30
INV_SQRT2 = 1.0 / math.sqrt(2.0)
INV_SQRT_2PI = 1.0 / math.sqrt(2.0 * math.pi)

ADAM_LR, ADAM_B1, ADAM_B2, ADAM_EPS, ADAM_WD, ADAM_STEP = 0.001, 0.9, 0.999, 1e-08, 0.01, 10

TM = 512
NT = S // TM
ATTN_UNROLL = 8
TN = 256
NJ = D_FF // TN
LANES = 128
VMEM_MIB = 1024 * 1024

BIG = (("w_in", (320, 1024), "rows_t"), ("w_o", (128, 1024), "rows"), ("w_ff_a", (352, 1024), "rows_t"),
       ("w_ff_b", (352, 1024), "rows_t"), ("w_ff_down", (352, 1024), "rows"), ("w_ple_gate", (128, 1024), "rows"),
       ("w_ple_in", (256, 128), "cols"))
WEIGHT_ORDER = ("w_in", "ln_z_g", "ln_z_b", "w_s", "b_s", "w_o", "ln1_g", "ln1_b", "w_ff_a", "w_ff_b",
                "conv_w", "conv_b", "w_ff_down", "ln2_g", "ln2_b", "w_ple_gate", "b_ple_gate",
                "w_ple_in", "ln3_g", "ln3_b")


def _params(semantics=None, vmem_mib=48):
    return pltpu.CompilerParams(dimension_semantics=semantics, vmem_limit_bytes=vmem_mib * VMEM_MIB)


def _dot(a, b):
    return jnp.dot(a, b, preferred_element_type=F32)


def _dot_nt(a, b):
    return lax.dot_general(a, b, (((1,), (1,)), ((), ())), preferred_element_type=F32)


def _dot_tn(a, b):
    return lax.dot_general(a, b, (((0,), (0,)), ((), ())), preferred_element_type=F32)


def _gelu(x):
    return 0.5 * x * (1.0 + lax.erf(x * INV_SQRT2))


def _gelu_grad(x):
    return 0.5 * (1.0 + lax.erf(x * INV_SQRT2)) + x * (jnp.exp(-0.5 * x * x) * INV_SQRT_2PI)


def _ln_stats(y):
    mu = jnp.mean(y, axis=-1, keepdims=True)
    yc = y - mu
    var = jnp.mean(yc * yc, axis=-1, keepdims=True)
    rstd = lax.rsqrt(var + LN_EPS)
    return yc * rstd, rstd


def _ln_bwd(dxhat, xhat, rstd):
    m1 = jnp.mean(dxhat, axis=-1, keepdims=True)
    m2 = jnp.mean(dxhat * xhat, axis=-1, keepdims=True)
    return rstd * (dxhat - m1 - xhat * m2)


def _colsum(x):
    return jnp.sum(x, axis=0, keepdims=True)


def _rows(i):
    return (i, 0)


def _fixed(*_):
    return (0, 0)


def _row_spec(width):
    return pl.BlockSpec((TM, width), _rows)


def _full_spec(shape):
    return pl.BlockSpec(shape, lambda *_: (0,) * len(shape))


def _owner_slot(j):
    return (j % 2) * 4 + j // 2


def _lane_lo():
    return lax.broadcasted_iota(jnp.int32, (CHUNK, LANES), 1) < 64


def _tril():
    r = lax.broadcasted_iota(jnp.int32, (CHUNK, CHUNK), 0)
    c = lax.broadcasted_iota(jnp.int32, (CHUNK, CHUNK), 1)
    return c <= r


def _rope_consts():
    lane = np.arange(LANES) % 64
    j = lane % 8
    inv = np.where(lane < 16, np.float32(ROPE_THETA) ** (-(2.0 * j).astype(np.float32) / np.float32(16.0)), 0.0)
    m_lo = (lane < 8).astype(np.float32)
    m_hi = ((lane >= 8) & (lane < 16)).astype(np.float32)
    return (jnp.asarray(inv, F32).reshape(1, LANES), jnp.asarray(m_lo).reshape(1, LANES),
            jnp.asarray(m_hi).reshape(1, LANES))


def _rope_tables(pos_col, carried=None):
    inv, m_lo, m_hi = _rope_consts()

    def body(pos_ref, inv_ref, lo_ref, hi_ref, c_ref, sa_ref, sb_ref):
        ang = pos_ref[...] * inv_ref[...]
        c = jnp.cos(ang)
        s = jnp.sin(ang)
        lo = lo_ref[...]
        hi = hi_ref[...]
        c_ref[...] = jnp.where(lo + hi > 0.0, c, 1.0)
        sa_ref[...] = s * hi
        sb_ref[...] = -s * lo

    vec = _full_spec((1, LANES))
    out = jax.ShapeDtypeStruct((S, LANES), F32)
    return _host_call(
        body, carried, name="rope_tables", grid=(NT,), out_shape=(out, out, out),
        in_specs=[pl.BlockSpec((TM, 1), _rows), vec, vec, vec],
        out_specs=(_row_spec(LANES),) * 3, scratch_shapes=[], params=_params(("arbitrary",)),
        args=(pos_col, inv, m_lo, m_hi))


def _rope(t, c, sa, sb):
    return t * c + pltpu.roll(t, 8, 1) * sa + pltpu.roll(t, LANES - 8, 1) * sb


def _rope_t(dy, c, sa, sb):
    return dy * c + pltpu.roll(dy * sa, LANES - 8, 1) + pltpu.roll(dy * sb, 8, 1)


def _masked_ws(ws_ref):
    tril = _tril()
    return [jnp.where(tril, ws_ref[g], 0.0).astype(BF16) for g in range(8)]


def _spatial_mix(zn, wm, bs):
    lo = _lane_lo()
    rows = []
    for ch in range(TM // CHUNK):
        slabs = []
        for pr in range(4):
            zp = zn[ch * CHUNK:(ch + 1) * CHUNK, pr * LANES:(pr + 1) * LANES].astype(BF16)
            slabs.append(jnp.where(lo, _dot(wm[2 * pr], zp), _dot(wm[2 * pr + 1], zp)))
        rows.append(jnp.concatenate(slabs, axis=1) + bs)
    return jnp.concatenate(rows, axis=0)


def _proj_in_fwd(x, w_in, tabs, ln_z_g, ln_z_b, w_s, bs_exp):
    def body(x_ref, w_ref, c_ref, sa_ref, sb_ref, g_ref, b_ref, ws_ref, bs_ref,
             q_ref, k_ref, v_ref, u_ref, z_ref, gm_ref, xb_ref):
        xb = x_ref[...].astype(BF16)
        xb_ref[...] = xb
        c, sa, sb = c_ref[...], sa_ref[...], sb_ref[...]
        hq = _dot_nt(xb, w_ref[0:512, :])
        hk = _dot_nt(xb, w_ref[512:1024, :])
        for s in range(4):
            sl = slice(s * LANES, (s + 1) * LANES)
            q_ref[:, sl] = _rope(hq[:, sl], c, sa, sb)
            k_ref[:, sl] = _rope(hk[:, sl], c, sa, sb)
        v_ref[...] = _dot_nt(xb, w_ref[1024:1536, :])
        u_pre = _dot_nt(xb, w_ref[1536:2048, :])
        z_pre = _dot_nt(xb, w_ref[2048:2560, :])
        u_ref[...] = u_pre
        z_ref[...] = z_pre
        zhat, _ = _ln_stats(_gelu(z_pre))
        zn = zhat * g_ref[...] + b_ref[...]
        mixed = _spatial_mix(zn, _masked_ws(ws_ref), bs_ref[...])
        gm_ref[...] = (_gelu(u_pre) * mixed).astype(BF16)

    half = jax.ShapeDtypeStruct((S, D_HALF), F32)
    tab = _row_spec(LANES)
    return pl.pallas_call(
        body, name="proj_in_fwd", grid=(NT,),
        out_shape=(half, half, half, half, half, jax.ShapeDtypeStruct((S, D_HALF), BF16),
                   jax.ShapeDtypeStruct((S, D), BF16)),
        in_specs=[_row_spec(D), _full_spec((D_IN, D)), tab, tab, tab, _full_spec((1, D_HALF)),
                  _full_spec((1, D_HALF)), _full_spec((8, CHUNK, CHUNK)), _full_spec((CHUNK, D_HALF))],
        out_specs=(_row_spec(D_HALF),) * 6 + (_row_spec(D),),
        compiler_params=_params(("parallel",)),
    )(x, w_in, *tabs, ln_z_g, ln_z_b, w_s, bs_exp)


def _store_band_bias(bias_ref):
    qi = lax.broadcasted_iota(jnp.int32, (CHUNK, 2 * CHUNK), 0)
    kj = lax.broadcasted_iota(jnp.int32, (CHUNK, 2 * CHUNK), 1)
    band = (kj >= qi) & (kj <= qi + CHUNK)
    bias_ref[0] = jnp.where(band, 0.0, NEG_INF)
    bias_ref[1] = jnp.where(band & (kj >= CHUNK), 0.0, NEG_INF)


def _permuted_rows(ref, d, r):
    return ref[...] if d == 1 else ref[pl.ds(r, S // d, stride=d), :]


def _attention_fwd(q, k, v, carried=None):
    def body(q_ref, k_ref, v_ref, o_ref, lse_ref, qb, kb, v0b, v1b, bias, op, lp, ob0, lb0, ob1, lb1, ob2, lb2):
        lo = _lane_lo()
        lo_f = lo.astype(F32)[0:1, :]
        hi_f = 1.0 - lo_f
        zero_pad = jnp.zeros((CHUNK, LANES), BF16)
        for buf in (qb, kb, v0b, v1b):
            buf[0:CHUNK, :] = zero_pad
        _store_band_bias(bias)
        outs = ((ob0, lb0), (ob1, lb1), (ob2, lb2))
        for (d, nb), (ob, lb) in zip(DILATIONS, outs):
            length = S // d
            for r in range(d):
                dst = slice(CHUNK + r * length, CHUNK + (r + 1) * length)
                qb[dst, :] = (_permuted_rows(q_ref, d, r) * 0.125).astype(BF16)
                kb[dst, :] = _permuted_rows(k_ref, d, r).astype(BF16)
                vs = _permuted_rows(v_ref, d, r)
                v0b[dst, :] = (vs * lo_f + hi_f).astype(BF16)
                v1b[dst, :] = (vs * hi_f + lo_f).astype(BF16)

            def block(b, carry, nb=nb):
                base = pl.multiple_of(b * CHUNK, CHUNK)
                add = bias[jnp.where(b % nb == 0, 1, 0)]
                qblk = qb[pl.ds(pl.multiple_of(base + CHUNK, CHUNK), CHUNK), :]
                kblk = kb[pl.ds(base, 2 * CHUNK), :]
                q2 = jnp.concatenate([jnp.where(lo, qblk, 0), jnp.where(lo, 0, qblk)], axis=0)
                s2 = _dot_nt(q2, kblk) + jnp.concatenate([add, add], axis=0)
                m2 = jnp.max(s2, axis=-1, keepdims=True)
                p2 = jnp.exp(s2 - m2).astype(BF16)
                pv, mx = [], []
                for head, vh in enumerate((v0b, v1b)):
                    rows = slice(head * CHUNK, (head + 1) * CHUNK)
                    pv.append(_dot(p2[rows, :], vh[pl.ds(base, 2 * CHUNK), :]))
                    mx.append(m2[rows, :])
                den = pltpu.roll(jnp.where(lo, pv[1], pv[0]), 64, 1)
                op[pl.ds(base, CHUNK), :] = jnp.where(lo, pv[0], pv[1]) / den
                lp[pl.ds(base, CHUNK), :] = jnp.where(lo, mx[0], mx[1]) + jnp.log(den)
                return carry

            lax.fori_loop(0, S // CHUNK, block, 0, unroll=ATTN_UNROLL)
            for r in range(d):
                src = slice(r * length, (r + 1) * length)
                if d == 1:
                    ob[...] = op[...]
                    lb[...] = lp[...]
                else:
                    ob[pl.ds(r, length, stride=d), :] = op[src, :]
                    lb[pl.ds(r, length, stride=d), :] = lp[src, :]
        for t in range(NT):
            rows = slice(t * TM, (t + 1) * TM)
            l0, l1, l2 = lb0[rows, :], lb1[rows, :], lb2[rows, :]
            mx = jnp.maximum(jnp.maximum(l0, l1), l2)
            e0, e1, e2 = jnp.exp(l0 - mx), jnp.exp(l1 - mx), jnp.exp(l2 - mx)
            den = e0 + e1 + e2
            o_ref[rows, :] = (e0 * ob0[rows, :] + e1 * ob1[rows, :] + e2 * ob2[rows, :]) / den
            lse_ref[rows, :] = mx + jnp.log(den)

    slab = pl.BlockSpec((S, LANES), lambda h: (0, h))
    out = jax.ShapeDtypeStruct((S, D_HALF), F32)
    padded = pltpu.VMEM((CHUNK + S, LANES), BF16)
    whole = pltpu.VMEM((S, LANES), F32)
    return _host_call(
        body, carried, name="attention_fwd", grid=(4,), out_shape=(out, out),
        in_specs=[slab, slab, slab], out_specs=(slab, slab),
        scratch_shapes=[padded] * 4 + [pltpu.VMEM((2, CHUNK, 2 * CHUNK), F32)] + [whole] * 8,
        params=_params(("arbitrary",), 56), args=(q, k, v))


def _mix_ln1_fwd(attn, gm, w_o, x, g1, b1):
    def body(a_ref, gm_ref, w_ref, x_ref, g_ref, b_ref, xhat_ref, rstd_ref, x1b_ref):
        mix = _dot(a_ref[...].astype(BF16), w_ref[0:D_HALF, :]) + _dot(gm_ref[...], w_ref[D_HALF:D, :])
        xhat, rstd = _ln_stats(ALPHA * x_ref[...] + mix)
        xhat_ref[...] = xhat
        rstd_ref[...] = rstd
        x1b_ref[...] = (xhat * g_ref[...] + b_ref[...]).astype(BF16)

    vec = _full_spec((1, D))
    return pl.pallas_call(
        body, name="mix_ln1_fwd", grid=(NT,),
        out_shape=(jax.ShapeDtypeStruct((S, D), F32), jax.ShapeDtypeStruct((S, 1), F32),
                   jax.ShapeDtypeStruct((S, D), BF16)),
        in_specs=[_row_spec(D_HALF), _row_spec(D_HALF), _full_spec((D, D)), _row_spec(D), vec, vec],
        out_specs=(_row_spec(D), pl.BlockSpec((TM, 1), _rows), _row_spec(D)),
        compiler_params=_params(("parallel",)),
    )(attn, gm, w_o, x, g1, b1)


def _ffn_up_fwd(x1b, w_a, w_b, conv_w8, conv_b, carried=None):
    def body(x_ref, wa_ref, wb_ref, cw_ref, cb_ref, ap_ref, a_ref, bl_ref, h_ref, carry):
        @pl.when(pl.program_id(1) == 0)
        def _():
            carry[...] = jnp.zeros_like(carry)

        xb = x_ref[...]
        ap = _dot_nt(xb, wa_ref[...])
        bl = _dot_nt(xb, wb_ref[...])
        row = lax.broadcasted_iota(jnp.int32, (TM, TN), 0)
        c6, c7 = carry[6:7, :], carry[7:8, :]
        m1 = jnp.where(row == 0, c7, pltpu.roll(ap, 1, 0))
        m2 = jnp.where(row == 0, c6, jnp.where(row == 1, c7, pltpu.roll(ap, 2, 0)))
        a = cb_ref[...] + cw_ref[0:1, :] * m2 + cw_ref[1:2, :] * m1 + cw_ref[2:3, :] * ap
        carry[...] = ap[TM - 8:TM, :]
        ap_ref[...] = ap.astype(BF16)
        a_ref[...] = a
        bl_ref[...] = bl.astype(BF16)
        h_ref[...] = (_gelu(a) * bl).astype(BF16)

    tile = pl.BlockSpec((TM, TN), lambda j, i: (i, j))
    wcol = pl.BlockSpec((TN, D), lambda j, i: (j, 0))
    ff = jax.ShapeDtypeStruct((S, D_FF), F32)
    ffb = jax.ShapeDtypeStruct((S, D_FF), BF16)
    return _host_call(
        body, carried, name="ffn_up_fwd", grid=(NJ, NT),
        out_shape=(ffb, ff, ffb, ffb),
        in_specs=[pl.BlockSpec((TM, D), lambda j, i: (i, 0)), wcol, wcol,
                  pl.BlockSpec((8, TN), lambda j, i: (0, j)), pl.BlockSpec((1, TN), lambda j, i: (0, j))],
        out_specs=(tile, tile, tile, tile),
        scratch_shapes=[pltpu.VMEM((8, TN), F32)],
        params=_params(("arbitrary", "arbitrary")), args=(x1b, w_a, w_b, conv_w8, conv_b))


def _ffn_down_ln2_fwd(hff, w_down, xhat1, g1, b1):
    def body(h_ref, w_ref, xh_ref, g_ref, b_ref, xhat_ref, rstd_ref):
        x1 = xh_ref[...] * g_ref[...] + b_ref[...]
        xhat, rstd = _ln_stats(ALPHA * x1 + _dot(h_ref[...], w_ref[...]))
        xhat_ref[...] = xhat
        rstd_ref[...] = rstd

    vec = _full_spec((1, D))
    return pl.pallas_call(
        body, name="ffn_down_ln2_fwd", grid=(NT,),
        out_shape=(jax.ShapeDtypeStruct((S, D), F32), jax.ShapeDtypeStruct((S, 1), F32)),
        in_specs=[_row_spec(D_FF), _full_spec((D_FF, D)), _row_spec(D), vec, vec],
        out_specs=(_row_spec(D), pl.BlockSpec((TM, 1), _rows)),
        compiler_params=_params(("parallel",)),
    )(hff, w_down, xhat1, g1, b1)


def _tail_fwd_bwd(xhat2, rstd2, p, target, w_g, w_p, g2, b2, bg, g3, b3):
    def body(xh_ref, rs_ref, p_ref, t_ref, wg_ref, wp_ref, g2_ref, b2_ref, bg_ref, g3_ref, b3_ref,
             loss_ref, dy2_ref, dy2b_ref, gwg_ref, gwp_ref, vec_ref, dwg_ref, dwp_ref):
        @pl.when(pl.program_id(0) == 0)
        def _():
            loss_ref[...] = jnp.zeros_like(loss_ref)
            dwg_ref[...] = jnp.zeros_like(dwg_ref)
            dwp_ref[...] = jnp.zeros_like(dwp_ref)
            vec_ref[...] = jnp.zeros_like(vec_ref)

        xhat2_t = xh_ref[...]
        x2 = xhat2_t * g2_ref[...] + b2_ref[...]
        x2b = x2.astype(BF16)
        pb = p_ref[...].astype(BF16)
        gate = jax.nn.sigmoid(_dot(x2b, wg_ref[...]) + bg_ref[...])
        pin = _dot(pb, wp_ref[...])
        xhat3, rstd3 = _ln_stats(ALPHA * x2 + gate * pin)
        err = xhat3 * g3_ref[...] + b3_ref[...] - t_ref[...]
        loss_ref[...] += jnp.sum(jnp.mean(err * err, axis=-1, keepdims=True), axis=0, keepdims=True) * 0.5
        dout = err * (1.0 / D)
        dy3 = _ln_bwd(dout * g3_ref[...], xhat3, rstd3)
        dgp = dy3 * pin * gate * (1.0 - gate)
        dgpb = dgp.astype(BF16)
        dwg_ref[...] += _dot_tn(x2b, dgpb)
        dwp_ref[...] += _dot_tn(pb, (dy3 * gate).astype(BF16))
        dx2 = ALPHA * dy3 + _dot_nt(dgpb, wg_ref[...])
        dy2 = _ln_bwd(dx2 * g2_ref[...], xhat2_t, rs_ref[...])
        dy2_ref[...] = dy2
        dy2b_ref[...] = dy2.astype(BF16)
        vec_ref[0:1, :] += _colsum(dgp)
        vec_ref[1:2, :] += _colsum(dout * xhat3)
        vec_ref[2:3, :] += _colsum(dout)
        vec_ref[3:4, :] += _colsum(dx2 * xhat2_t)
        vec_ref[4:5, :] += _colsum(dx2)

        @pl.when(pl.program_id(0) == NT - 1)
        def _():
            for j in range(N_DEV):
                gwg_ref[_owner_slot(j)] = dwg_ref[LANES * j:LANES * (j + 1), :].astype(BF16)
                gwp_ref[_owner_slot(j)] = dwp_ref[:, LANES * j:LANES * (j + 1)].astype(BF16)

    vec = _full_spec((1, D))
    return pl.pallas_call(
        body, name="tail_fwd_bwd", grid=(NT,),
        out_shape=(jax.ShapeDtypeStruct((8, LANES), F32), jax.ShapeDtypeStruct((S, D), F32),
                   jax.ShapeDtypeStruct((S, D), BF16), jax.ShapeDtypeStruct((N_DEV, D // N_DEV, D), BF16),
                   jax.ShapeDtypeStruct((N_DEV, D_PLE, D // N_DEV), BF16), jax.ShapeDtypeStruct((8, D), F32)),
        in_specs=[_row_spec(D), pl.BlockSpec((TM, 1), _rows), _row_spec(D_PLE), _row_spec(D),
                  _full_spec((D, D)), _full_spec((D_PLE, D)), vec, vec, vec, vec, vec],
        out_specs=(_full_spec((8, LANES)), _row_spec(D), _row_spec(D), _full_spec((N_DEV, D // N_DEV, D)),
                   _full_spec((N_DEV, D_PLE, D // N_DEV)), _full_spec((8, D))),
        scratch_shapes=[pltpu.VMEM((D, D), F32), pltpu.VMEM((D_PLE, D), F32)],
        compiler_params=_params(("arbitrary",)),
    )(xhat2, rstd2, p, target, w_g, w_p, g2, b2, bg, g3, b3)


def _ffn_bwd(dy2b, x1b, w_down, w_a, w_b, a_pre, a, b_lin, hff, conv_w8):
    def body(dy_ref, x_ref, wd_ref, wa_ref, wb_ref, ap_ref, a_ref, bl_ref, h_ref, cw_ref,
             dwd_ref, dwa_ref, dwb_ref, dcw_ref, dx_hbm, dx_acc, carry, sem):
        j, i = pl.program_id(0), pl.program_id(1)

        @pl.when(i == 0)
        def _():
            carry[...] = jnp.zeros_like(carry)
            dwd_ref[...] = jnp.zeros_like(dwd_ref)
            dwa_ref[...] = jnp.zeros_like(dwa_ref)
            dwb_ref[...] = jnp.zeros_like(dwb_ref)
            dcw_ref[...] = jnp.zeros_like(dcw_ref)

        dyb = dy_ref[...]
        xb = x_ref[...]
        dh = _dot_nt(dyb, wd_ref[...])
        av = a_ref[...]
        dbl = dh * _gelu(av)
        da = dh * bl_ref[...].astype(F32) * _gelu_grad(av)
        row = lax.broadcasted_iota(jnp.int32, (TM, TN), 0)
        c0, c1 = carry[0:1, :], carry[1:2, :]
        p1 = jnp.where(row == TM - 1, c0, pltpu.roll(da, TM - 1, 0))
        p2 = jnp.where(row == TM - 2, c0, jnp.where(row == TM - 1, c1, pltpu.roll(da, TM - 2, 0)))
        carry[...] = da[0:8, :]
        ap = ap_ref[...].astype(F32)
        dcw_ref[3:4, :] += _colsum(da)
        dcw_ref[0:1, :] += _colsum(ap * p2)
        dcw_ref[1:2, :] += _colsum(ap * p1)
        dcw_ref[2:3, :] += _colsum(ap * da)
        dap = (cw_ref[2:3, :] * da + cw_ref[1:2, :] * p1 + cw_ref[0:1, :] * p2).astype(BF16)
        dblb = dbl.astype(BF16)
        dwa_ref[...] += _dot_tn(dap, xb)
        dwb_ref[...] += _dot_tn(dblb, xb)
        dwd_ref[...] += _dot_tn(h_ref[...], dyb)
        dx = _dot(dap, wa_ref[...]) + _dot(dblb, wb_ref[...])
        rows = pl.ds(pl.multiple_of((NT - 1 - i) * TM, TM), TM)

        @pl.when(j == 0)
        def _():
            dx_acc[rows, :] = dx

        @pl.when(j > 0)
        def _():
            dx_acc[rows, :] += dx

        @pl.when((j == NJ - 1) & (i == NT - 1))
        def _():
            cp = pltpu.make_async_copy(dx_acc, dx_hbm, sem)
            cp.start()
            cp.wait()

    rev_rows = lambda j, i: (NT - 1 - i, 0)
    rev_tile = pl.BlockSpec((TM, TN), lambda j, i: (NT - 1 - i, j))
    wcol = pl.BlockSpec((TN, D), lambda j, i: (j, 0))
    small = pl.BlockSpec((8, TN), lambda j, i: (0, j))
    return pl.pallas_call(
        body, name="ffn_bwd", grid=(NJ, NT),
        out_shape=(jax.ShapeDtypeStruct((D_FF, D), F32), jax.ShapeDtypeStruct((D_FF, D), F32),
                   jax.ShapeDtypeStruct((D_FF, D), F32), jax.ShapeDtypeStruct((8, D_FF), F32),
                   jax.ShapeDtypeStruct((S, D), F32)),
        in_specs=[pl.BlockSpec((TM, D), rev_rows), pl.BlockSpec((TM, D), rev_rows),
                  pl.BlockSpec((TN, D), lambda j, i: (j, 0)), wcol, wcol, rev_tile, rev_tile, rev_tile,
                  rev_tile, small],
        out_specs=(pl.BlockSpec((TN, D), lambda j, i: (j, 0)), wcol, wcol, small,
                   pl.BlockSpec(memory_space=pl.ANY)),
        scratch_shapes=[pltpu.VMEM((S, D), F32), pltpu.VMEM((8, TN), F32), pltpu.SemaphoreType.DMA],
        compiler_params=_params(("arbitrary", "arbitrary"), 56),
    )(dy2b, x1b, w_down, w_a, w_b, a_pre, a, b_lin, hff, conv_w8)


def _ln1_mix_bwd(dy2, dx1_ffn, xhat1, rstd1, g1, attn, gm, w_o, carried=None):
    def body(dy2_ref, dxf_ref, xh_ref, rs_ref, g_ref, a_ref, gm_ref, w_ref,
             dy1_ref, da_ref, dlt_ref, dgm_ref, gwo_ref, vec_ref, dwo_ref):
        @pl.when(pl.program_id(0) == 0)
        def _():
            dwo_ref[...] = jnp.zeros_like(dwo_ref)
            vec_ref[...] = jnp.zeros_like(vec_ref)

        xhat = xh_ref[...]
        dx1 = ALPHA * dy2_ref[...] + dxf_ref[...]
        vec_ref[0:1, :] += _colsum(dx1 * xhat)
        vec_ref[1:2, :] += _colsum(dx1)
        dy1 = _ln_bwd(dx1 * g_ref[...], xhat, rs_ref[...])
        dy1_ref[...] = dy1
        dy1b = dy1.astype(BF16)
        dmix = _dot_nt(dy1b, w_ref[...])
        attn_t = a_ref[...]
        d_attn = dmix[:, 0:D_HALF]
        da_ref[...] = d_attn
        dgm_ref[...] = dmix[:, D_HALF:D]
        lo = (lax.broadcasted_iota(jnp.int32, (TM, LANES), 1) < 64)
        for s in range(4):
            sl = slice(s * LANES, (s + 1) * LANES)
            prod = d_attn[:, sl] * attn_t[:, sl]
            s0 = jnp.sum(jnp.where(lo, prod, 0.0), axis=-1, keepdims=True)
            s1 = jnp.sum(jnp.where(lo, 0.0, prod), axis=-1, keepdims=True)
            dlt_ref[:, sl] = jnp.where(lo, s0, s1)
        dwo_ref[0:D_HALF, :] += _dot_tn(attn_t.astype(BF16), dy1b)
        dwo_ref[D_HALF:D, :] += _dot_tn(gm_ref[...], dy1b)

        @pl.when(pl.program_id(0) == NT - 1)
        def _():
            for j in range(N_DEV):
                gwo_ref[_owner_slot(j)] = dwo_ref[LANES * j:LANES * (j + 1), :].astype(BF16)

    half = jax.ShapeDtypeStruct((S, D_HALF), F32)
    return _host_call(
        body, carried, name="ln1_mix_bwd", grid=(NT,),
        out_shape=(jax.ShapeDtypeStruct((S, D), F32), half, half, half,
                   jax.ShapeDtypeStruct((N_DEV, D // N_DEV, D), BF16), jax.ShapeDtypeStruct((8, D), F32)),
        in_specs=[_row_spec(D), _row_spec(D), _row_spec(D), pl.BlockSpec((TM, 1), _rows), _full_spec((1, D)),
                  _row_spec(D_HALF), _row_spec(D_HALF), _full_spec((D, D))],
        out_specs=(_row_spec(D), _row_spec(D_HALF), _row_spec(D_HALF), _row_spec(D_HALF),
                   _full_spec((N_DEV, D // N_DEV, D)), _full_spec((8, D))),
        scratch_shapes=[pltpu.VMEM((D, D), F32)],
        params=_params(("arbitrary",)), args=(dy2, dx1_ffn, xhat1, rstd1, g1, attn, gm, w_o))


def _gmlp_bwd(d_gm, u_pre, z_pre, ln_z_g, ln_z_b, w_s, bs_exp, carried=None):
    def body(dg_ref, u_ref, z_ref, g_ref, b_ref, ws_ref, bs_ref, du_ref, dz_ref, dws_ref, dbs_ref, vec_ref):
        @pl.when(pl.program_id(0) == 0)
        def _():
            dws_ref[...] = jnp.zeros_like(dws_ref)
            dbs_ref[...] = jnp.zeros_like(dbs_ref)
            vec_ref[...] = jnp.zeros_like(vec_ref)

        u_pre_t, z_pre_t, dgm = u_ref[...], z_ref[...], dg_ref[...]
        zhat, rstd = _ln_stats(_gelu(z_pre_t))
        zn = zhat * g_ref[...] + b_ref[...]
        wm = _masked_ws(ws_ref)
        mixed = _spatial_mix(zn, wm, bs_ref[...])
        du_ref[...] = (dgm * mixed * _gelu_grad(u_pre_t)).astype(BF16)
        dmixed = dgm * _gelu(u_pre_t)
        lo = _lane_lo()
        tril = _tril()
        group_of_lane = lax.broadcasted_iota(jnp.int32, (8, D_HALF), 1) // 64
        pick = (group_of_lane == lax.broadcasted_iota(jnp.int32, (8, D_HALF), 0)).astype(F32)
        dzn_rows = []
        for ch in range(TM // CHUNK):
            rows = slice(ch * CHUNK, (ch + 1) * CHUNK)
            dbs_ref[...] += lax.dot_general(pick, dmixed[rows, :], (((1,), (1,)), ((), ())),
                                            precision=lax.Precision.HIGHEST, preferred_element_type=F32)
            slabs = []
            for pr in range(4):
                sl = slice(pr * LANES, (pr + 1) * LANES)
                dm = dmixed[rows, sl]
                zp = zn[rows, sl].astype(BF16)
                dm_lo = jnp.where(lo, dm, 0.0).astype(BF16)
                dm_hi = jnp.where(lo, 0.0, dm).astype(BF16)
                dws_ref[2 * pr] += jnp.where(tril, _dot_nt(dm_lo, zp), 0.0)
                dws_ref[2 * pr + 1] += jnp.where(tril, _dot_nt(dm_hi, zp), 0.0)
                dmb = dm.astype(BF16)
                slabs.append(jnp.where(lo, _dot_tn(wm[2 * pr], dmb), _dot_tn(wm[2 * pr + 1], dmb)))
            dzn_rows.append(jnp.concatenate(slabs, axis=1))
        dzn = jnp.concatenate(dzn_rows, axis=0)
        vec_ref[0:1, :] += _colsum(dzn * zhat)
        vec_ref[1:2, :] += _colsum(dzn)
        dz = _ln_bwd(dzn * g_ref[...], zhat, rstd)
        dz_ref[...] = (dz * _gelu_grad(z_pre_t)).astype(BF16)

    halfb = jax.ShapeDtypeStruct((S, D_HALF), BF16)
    vec = _full_spec((1, D_HALF))
    return _host_call(
        body, carried, name="gmlp_bwd", grid=(NT,),
        out_shape=(halfb, halfb, jax.ShapeDtypeStruct((8, CHUNK, CHUNK), F32),
                   jax.ShapeDtypeStruct((8, CHUNK), F32), jax.ShapeDtypeStruct((8, D_HALF), F32)),
        in_specs=[_row_spec(D_HALF), _row_spec(D_HALF), _row_spec(D_HALF), vec, vec,
                  _full_spec((8, CHUNK, CHUNK)), _full_spec((CHUNK, D_HALF))],
        out_specs=(_row_spec(D_HALF), _row_spec(D_HALF), _full_spec((8, CHUNK, CHUNK)),
                   _full_spec((8, CHUNK)), _full_spec((8, D_HALF))),
        scratch_shapes=[], params=_params(("arbitrary",)), args=(d_gm, u_pre, z_pre, ln_z_g, ln_z_b, w_s, bs_exp))


def _attention_bwd(q, k, v, lse, d_attn, delta, tabs, carried=None):
    def body(q_ref, k_ref, v_ref, l_ref, do_ref, dl_ref, c_ref, sa_ref, sb_ref, dq_ref, dk_ref, dv_ref,
             qb, kb, vb, gb, bias, lsp, dlp, dqp, dk_own, dk_prev, dv_own, dv_prev, dqa, dka, dva):
        lo = _lane_lo()
        zero_pad = jnp.zeros((CHUNK, LANES), BF16)
        for buf in (qb, kb, vb, gb):
            buf[0:CHUNK, :] = zero_pad
        dk_prev[S:S + CHUNK, :] = jnp.zeros((CHUNK, LANES), F32)
        dv_prev[S:S + CHUNK, :] = jnp.zeros((CHUNK, LANES), F32)
        _store_band_bias(bias)
        for d, nb in DILATIONS:
            length = S // d
            for r in range(d):
                dst = slice(CHUNK + r * length, CHUNK + (r + 1) * length)
                src = slice(r * length, (r + 1) * length)
                qb[dst, :] = (_permuted_rows(q_ref, d, r) * 0.125).astype(BF16)
                kb[dst, :] = _permuted_rows(k_ref, d, r).astype(BF16)
                vb[dst, :] = _permuted_rows(v_ref, d, r).astype(BF16)
                gb[dst, :] = _permuted_rows(do_ref, d, r).astype(BF16)
                lsp[src, :] = _permuted_rows(l_ref, d, r)
                dlp[src, :] = _permuted_rows(dl_ref, d, r)

            def block(b, carry, nb=nb):
                base = pl.multiple_of(b * CHUNK, CHUNK)
                own = pl.multiple_of(base + CHUNK, CHUNK)
                add = bias[jnp.where(b % nb == 0, 1, 0)]
                qblk = qb[pl.ds(own, CHUNK), :]
                gblk = gb[pl.ds(own, CHUNK), :]
                kblk = kb[pl.ds(base, 2 * CHUNK), :]
                vblk = vb[pl.ds(base, 2 * CHUNK), :]
                lse_t = lsp[pl.ds(base, CHUNK), :]
                dlt_t = dlp[pl.ds(base, CHUNK), :]
                q2 = jnp.concatenate([jnp.where(lo, qblk, 0), jnp.where(lo, 0, qblk)], axis=0)
                g2 = jnp.concatenate([jnp.where(lo, gblk, 0), jnp.where(lo, 0, gblk)], axis=0)
                lse2 = jnp.concatenate([lse_t[:, 0:1], lse_t[:, 64:65]], axis=0)
                dlt2 = jnp.concatenate([dlt_t[:, 0:1], dlt_t[:, 64:65]], axis=0)
                add2 = jnp.concatenate([add, add], axis=0)
                p = jnp.exp(_dot_nt(q2, kblk) + add2 - lse2)
                ds = (p * (_dot_nt(g2, vblk) - dlt2)).astype(BF16)
                dv_blk = _dot_tn(p.astype(BF16), g2)
                dk_blk = _dot_tn(ds, q2)
                dq2 = _dot(ds, kblk)
                dqp[pl.ds(base, CHUNK), :] = jnp.where(lo, dq2[0:CHUNK, :], dq2[CHUNK:2 * CHUNK, :]) * 0.125
                dk_prev[pl.ds(base, CHUNK), :] = dk_blk[0:CHUNK, :]
                dk_own[pl.ds(own, CHUNK), :] = dk_blk[CHUNK:2 * CHUNK, :]
                dv_prev[pl.ds(base, CHUNK), :] = dv_blk[0:CHUNK, :]
                dv_own[pl.ds(own, CHUNK), :] = dv_blk[CHUNK:2 * CHUNK, :]
                return carry

            lax.fori_loop(0, S // CHUNK, block, 0, unroll=ATTN_UNROLL)
            for r in range(d):
                src = slice(r * length, (r + 1) * length)
                pad = slice(CHUNK + r * length, CHUNK + (r + 1) * length)
                if d == 1:
                    dqa[...] = dqp[...]
                    dka[...] = dk_own[pad, :] + dk_prev[pad, :]
                    dva[...] = dv_own[pad, :] + dv_prev[pad, :]
                else:
                    dst = pl.ds(r, length, stride=d)
                    dqa[dst, :] = dqa[dst, :] + dqp[src, :]
                    dka[dst, :] = dka[dst, :] + (dk_own[pad, :] + dk_prev[pad, :])
                    dva[dst, :] = dva[dst, :] + (dv_own[pad, :] + dv_prev[pad, :])
        for t in range(NT):
            rows = slice(t * TM, (t + 1) * TM)
            c, sa, sb = c_ref[rows, :], sa_ref[rows, :], sb_ref[rows, :]
            dq_ref[rows, :] = _rope_t(dqa[rows, :], c, sa, sb).astype(BF16)
            dk_ref[rows, :] = _rope_t(dka[rows, :], c, sa, sb).astype(BF16)
            dv_ref[rows, :] = dva[rows, :].astype(BF16)

    slab = pl.BlockSpec((S, LANES), lambda h: (0, h), pipeline_mode=pl.Buffered(1))
    tab = pl.BlockSpec((S, LANES), lambda h: (0, 0), pipeline_mode=pl.Buffered(1))
    out_slab = pl.BlockSpec((S, LANES), lambda h: (0, h))
    out = jax.ShapeDtypeStruct((S, D_HALF), BF16)
    padded_b = pltpu.VMEM((CHUNK + S, LANES), BF16)
    padded_f = pltpu.VMEM((CHUNK + S, LANES), F32)
    whole = pltpu.VMEM((S, LANES), F32)
    return _host_call(
        body, carried, name="attention_bwd", grid=(4,), out_shape=(out, out, out),
        in_specs=[slab] * 6 + [tab] * 3, out_specs=(out_slab,) * 3,
        scratch_shapes=[padded_b] * 4 + [pltpu.VMEM((2, CHUNK, 2 * CHUNK), F32)] + [whole] * 3
        + [padded_f] * 4 + [whole] * 3,
        params=_params(("arbitrary",), 60), args=(q, k, v, lse, d_attn, delta, *tabs))


def _proj_in_bwd_w(xb, parts):
    def body(x_ref, p0, p1, p2, p3, p4, gw_ref, dw_ref):
        @pl.when(pl.program_id(0) == 0)
        def _():
            dw_ref[...] = jnp.zeros_like(dw_ref)

        xt = x_ref[...]
        for n, part in enumerate((p0, p1, p2, p3, p4)):
            dw_ref[n * D_HALF:(n + 1) * D_HALF, :] += _dot_tn(part[...], xt)

        @pl.when(pl.program_id(0) == NT - 1)
        def _():
            width = D_IN // N_DEV
            for j in range(N_DEV):
                gw_ref[_owner_slot(j)] = dw_ref[width * j:width * (j + 1), :].astype(BF16)

    return pl.pallas_call(
        body, name="proj_in_bwd_w", grid=(NT,), out_shape=jax.ShapeDtypeStruct((N_DEV, D_IN // N_DEV, D), BF16),
        in_specs=[_row_spec(D)] + [_row_spec(D_HALF)] * 5, out_specs=_full_spec((N_DEV, D_IN // N_DEV, D)),
        scratch_shapes=[pltpu.VMEM((D_IN, D), F32)],
        compiler_params=_params(("arbitrary",)),
    )(xb, *parts)


def _proj_in_bwd_x(dy1, parts, w_in, carried=None):
    def body(dy_ref, p0, p1, p2, p3, p4, w_ref, gx_ref):
        acc = ALPHA * dy_ref[...]
        for n, part in enumerate((p0, p1, p2, p3, p4)):
            acc += _dot(part[...], w_ref[n * D_HALF:(n + 1) * D_HALF, :])
        gx_ref[...] = acc

    return _host_call(
        body, carried, name="proj_in_bwd_x", grid=(NT,), out_shape=(jax.ShapeDtypeStruct((S, D), F32),),
        in_specs=[_row_spec(D)] + [_row_spec(D_HALF)] * 5 + [_full_spec((D_IN, D))], out_specs=(_row_spec(D),),
        scratch_shapes=[], params=_params(("arbitrary",)), args=(dy1, *parts, w_in))


def _to_natural(blocks, name):
    n, rows, w = blocks.shape
    tile = min(rows, 256)

    def body(i_ref, o_ref):
        o_ref[...] = jnp.concatenate([i_ref[j] for j in range(n)], axis=1)

    return pl.pallas_call(
        body, name=name, grid=(rows // tile,), out_shape=jax.ShapeDtypeStruct((rows, n * w), blocks.dtype),
        in_specs=[pl.BlockSpec((n, tile, w), lambda i: (0, i, 0))],
        out_specs=pl.BlockSpec((tile, n * w), lambda i: (i, 0)), compiler_params=_params(("parallel",)),
    )(blocks)


def _row_blocks(full, name):
    rows, cols = full.shape
    r = rows // N_DEV

    def body(i_ref, o_ref):
        o_ref[0] = i_ref[...].astype(BF16)

    return pl.pallas_call(
        body, name=name, grid=(N_DEV,), out_shape=jax.ShapeDtypeStruct((N_DEV, r, cols), BF16),
        in_specs=[pl.BlockSpec((r, cols), lambda s: ((s % 4) * 2 + s // 4, 0))],
        out_specs=pl.BlockSpec((1, r, cols), lambda s: (s, 0, 0)), compiler_params=_params(("parallel",)),
    )(full)


def _local_step(x, p, pos_col, target, sm, ex):
    bs_exp = jnp.repeat(sm["b_s"].T, 64, axis=1)
    tabs, got = _rope_tables(pos_col, ex.gather_input())
    w_in = ex.weight_input(got)
    q, k, v, u_pre, z_pre, gm, xb = _proj_in_fwd(x, w_in, tabs, sm["ln_z_g"], sm["ln_z_b"], sm["w_s"], bs_exp)
    (attn, lse), got = _attention_fwd(q, k, v, ex.gather_first())
    wa = ex.weights_first(got)
    xhat1, rstd1, x1b = _mix_ln1_fwd(attn, gm, wa["w_o"], x, sm["ln1_g"], sm["ln1_b"])
    (a_pre, a, b_lin, hff), got = _ffn_up_fwd(x1b, wa["w_ff_a"], wa["w_ff_b"], wa["conv_w8"], sm["conv_b"],
                                              ex.gather_second())
    wc = ex.weights_second(got)
    xhat2, rstd2 = _ffn_down_ln2_fwd(hff, wc["w_ff_down"], xhat1, sm["ln1_g"], sm["ln1_b"])
    loss, dy2, dy2b, dw_g, dw_p, vec_tail = _tail_fwd_bwd(
        xhat2, rstd2, p, target, wc["w_ple_gate"], wc["w_ple_in"], sm["ln2_g"], sm["ln2_b"],
        sm["b_ple_gate"], sm["ln3_g"], sm["ln3_b"])
    dw_down, dw_a, dw_b, dconv, dx1_ffn = _ffn_bwd(dy2b, x1b, wc["w_ff_down"], wa["w_ff_a"], wa["w_ff_b"],
                                                    a_pre, a, b_lin, hff, wa["conv_w8"])
    (dy1, d_attn, delta, d_gm, dw_o, vec_ln1), _ = _ln1_mix_bwd(
        dy2, dx1_ffn, xhat1, rstd1, sm["ln1_g"], attn, gm, wa["w_o"])
    early = {"w_ff_a": _row_blocks(dw_a, "ff_a_grad_blocks"), "w_ff_b": _row_blocks(dw_b, "ff_b_grad_blocks"),
             "w_ff_down": _row_blocks(dw_down, "ff_down_grad_blocks"), "w_ple_gate": dw_g, "w_ple_in": dw_p,
             "w_o": dw_o}
    (du, dz, dws, dbs, vec_z), got = _gmlp_bwd(d_gm, u_pre, z_pre, sm["ln_z_g"], sm["ln_z_b"], sm["w_s"], bs_exp,
                                               ex.to_sibling(early))
    chip_sums = ex.reduce_on_chip(got)
    small = {"tail": vec_tail, "ln1": vec_ln1, "ln_z": vec_z, "conv": dconv, "w_s": dws, "b_s": dbs, "loss": loss}
    (dq, dk, dv), got_early = _attention_bwd(q, k, v, lse, d_attn, delta, tabs,
                                             ex.between_chips(chip_sums, small))
    parts = (dq, dk, dv, du, dz)
    (grad_x,), got_late = _proj_in_bwd_x(dy1, parts, w_in, ex.last(_proj_in_bwd_w(xb, parts)))
    return grad_x, ex.collect(got_early, got_late)


def _mesh_pos():
    return lax.axis_index("x"), lax.axis_index("y"), lax.axis_index("c")


def _cast_shards(shards):
    n = len(shards)

    def body(*refs):
        for a in range(n):
            refs[n + a][...] = refs[a][...].astype(BF16)

    whole = [_full_spec(s.shape) for s in shards]
    return pl.pallas_call(
        body, name="cast_shards", grid=(1,), out_shape=tuple(jax.ShapeDtypeStruct(s.shape, BF16) for s in shards),
        in_specs=whole, out_specs=tuple(whole), compiler_params=_params(("arbitrary",)),
    )(*shards)


class _GatherComm:
    def __init__(self, shards):
        n = len(shards)
        self.inputs = list(shards)
        self.out_shapes = [jax.ShapeDtypeStruct((N_DEV,) + s.shape, s.dtype) for s in shards]
        self.scratch = [pltpu.SemaphoreType.DMA((7 * n,)), pltpu.SemaphoreType.DMA((7 * n,)),
                        pltpu.SemaphoreType.DMA((n,))]

    def phases(self, x_refs, out_refs, sems):
        send_sems, recv_sems, local_sems = sems
        n_arr = len(x_refs)

        def where():
            x, y, c = _mesh_pos()
            return (x, y, c), (x, y, 1 - c), [(1 - x, y), (x, 1 - y), (1 - x, 1 - y)]

        def copy(a, n, block, to, from_shard=False):
            dst = out_refs[a].at[4 * block[0] + 2 * block[1] + block[2]]
            return pltpu.make_async_remote_copy(
                src_ref=x_refs[a] if from_shard else dst, dst_ref=dst, send_sem=send_sems.at[7 * a + n],
                recv_sem=recv_sems.at[7 * a + n], device_id=to, device_id_type=MESH)

        def local(a):
            x, y, c = _mesh_pos()
            return pltpu.make_async_copy(x_refs[a], out_refs[a].at[4 * x + 2 * y + c], local_sems.at[a])

        def start():
            me, sibling, chips = where()
            for a in range(n_arr):
                local(a).start()
                copy(a, 0, me, sibling, from_shard=True).start()
                for n, chip in enumerate(chips):
                    copy(a, 1 + n, me, (*chip, me[2]), from_shard=True).start()

        def forward():
            me, sibling, chips = where()
            for n, chip in enumerate(chips):
                for a in range(n_arr):
                    copy(a, 1 + n, (*chip, me[2]), me).wait_recv()
                    copy(a, 4 + n, (*chip, me[2]), sibling).start()

        def finish():
            me, sibling, chips = where()
            for a in range(n_arr):
                copy(a, 0, sibling, me).wait_recv()
                copy(a, 0, me, sibling, from_shard=True).wait_send()
                for n, chip in enumerate(chips):
                    copy(a, 4 + n, (*chip, 1 - me[2]), me).wait_recv()
                    copy(a, 1 + n, me, (*chip, me[2]), from_shard=True).wait_send()
                    copy(a, 4 + n, (*chip, me[2]), sibling).wait_send()
                local(a).wait()

        return {"start": start, "forward": forward, "finish": finish}


class _SiblingComm:
    def __init__(self, big):
        n = len(big)
        self.inputs = list(big)
        self.out_shapes = [jax.ShapeDtypeStruct(b.shape[1:], b.dtype) for b in big]
        self.scratch = [pltpu.SemaphoreType.DMA((n,)), pltpu.SemaphoreType.DMA((n,))]

    def phases(self, src, dst, sems):
        send_sems, recv_sems = sems

        def copies():
            x, y, c = _mesh_pos()
            return [pltpu.make_async_remote_copy(
                src_ref=src[a].at[1 - c], dst_ref=dst[a], send_sem=send_sems.at[a], recv_sem=recv_sems.at[a],
                device_id=(x, y, 1 - c), device_id_type=MESH) for a in range(len(src))]

        def start():
            for cp in copies():
                cp.start()

        def finish():
            for cp in copies():
                cp.wait()

        return {"start": start, "finish": finish}


class _ChipComm:
    def __init__(self, sums):
        n = len(sums)
        self.inputs = list(sums)
        self.out_shapes = [jax.ShapeDtypeStruct(s.shape, s.dtype) for s in sums]
        self.scratch = [pltpu.SemaphoreType.DMA((3 * n,)), pltpu.SemaphoreType.DMA((3 * n,)),
                        pltpu.SemaphoreType.DMA((n,))]

    def phases(self, src, dst, sems):
        send_sems, recv_sems, local_sems = sems

        def copies():
            x, y, c = _mesh_pos()
            my_chip = 2 * x + y
            out = [pltpu.make_async_copy(src[a].at[my_chip], dst[a].at[my_chip], local_sems.at[a])
                   for a in range(len(src))]
            for n, (px, py) in enumerate([(1 - x, y), (x, 1 - y), (1 - x, 1 - y)]):
                for a in range(len(src)):
                    out.append(pltpu.make_async_remote_copy(
                        src_ref=src[a].at[2 * px + py], dst_ref=dst[a].at[my_chip],
                        send_sem=send_sems.at[3 * a + n], recv_sem=recv_sems.at[3 * a + n],
                        device_id=(px, py, c), device_id_type=MESH))
            return out

        def start():
            for cp in copies():
                cp.start()

        def finish():
            for cp in copies():
                cp.wait()

        return {"start": start, "finish": finish}


class _ScatterComm:
    def __init__(self, blocks, small):
        self.n_big, self.n_small = len(blocks), len(small)
        n = self.n_big + self.n_small
        self.inputs = list(blocks) + list(small)
        self.out_shapes = ([jax.ShapeDtypeStruct(b.shape, b.dtype) for b in blocks]
                           + [jax.ShapeDtypeStruct((N_DEV,) + s.shape, s.dtype) for s in small])
        self.scratch = [pltpu.SemaphoreType.DMA((7 * n,)), pltpu.SemaphoreType.DMA((7 * n,)),
                        pltpu.SemaphoreType.DMA((n,))]

    def phases(self, src, dst, sems):
        send_sems, recv_sems, local_sems = sems
        n_big, n_all = self.n_big, self.n_big + self.n_small

        def source(a, core, chip):
            return src[a].at[core * 4 + chip] if a < n_big else src[a]

        def copies():
            x, y, c = _mesh_pos()
            me = 4 * x + 2 * y + c
            out = [pltpu.make_async_copy(source(a, c, 2 * x + y), dst[a].at[me], local_sems.at[a])
                   for a in range(n_all)]
            for flip in range(1, N_DEV):
                px = 1 - x if flip & 4 else x
                py = 1 - y if flip & 2 else y
                pc = 1 - c if flip & 1 else c
                for a in range(n_all):
                    n = 7 * a + flip - 1
                    out.append(pltpu.make_async_remote_copy(
                        src_ref=source(a, pc, 2 * px + py), dst_ref=dst[a].at[me], send_sem=send_sems.at[n],
                        recv_sem=recv_sems.at[n], device_id=(px, py, pc), device_id_type=MESH))
            return out

        def start():
            for cp in copies():
                cp.start()

        def finish():
            for cp in copies():
                cp.wait()

        return {"start": start, "finish": finish}


class _Both:
    def __init__(self, first, second):
        self.parts = (first, second)
        self.inputs = first.inputs + second.inputs
        self.out_shapes = first.out_shapes + second.out_shapes
        self.scratch = first.scratch + second.scratch

    def phases(self, src, dst, sems):
        a, b = self.parts
        pa = a.phases(src[:len(a.inputs)], dst[:len(a.out_shapes)], sems[:len(a.scratch)])
        pb = b.phases(src[len(a.inputs):], dst[len(a.out_shapes):], sems[len(a.scratch):])

        def both(name):
            def run():
                pa[name]()
                pb[name]()
            return run

        return {name: both(name) for name in pa}


def _host_call(body, carried, *, name, grid, out_shape, in_specs, out_specs, scratch_shapes, params, args):
    if carried is None:
        return pl.pallas_call(body, name=name, grid=grid, out_shape=tuple(out_shape), in_specs=list(in_specs),
                              out_specs=tuple(out_specs), scratch_shapes=list(scratch_shapes),
                              compiler_params=params)(*args), ()
    comm, when = carried
    n_in, n_out, n_scratch = len(in_specs), len(out_shape), len(scratch_shapes)
    k_in, k_out = len(comm.inputs), len(comm.out_shapes)

    def wrapped(*refs):
        bounds = np.cumsum([0, n_in, k_in, n_out, k_out, n_scratch])
        ins, c_in, outs, c_out, scr = (refs[bounds[i]:bounds[i + 1]] for i in range(5))
        phases = comm.phases(c_in, c_out, refs[bounds[5]:])
        for phase, cond in when("before"):
            pl.when(cond)(phases[phase])
        body(*ins, *outs, *scr)
        for phase, cond in when("after"):
            pl.when(cond)(phases[phase])

    anywhere = pl.BlockSpec(memory_space=pl.ANY)
    results = pl.pallas_call(
        wrapped, name=name, grid=grid, out_shape=tuple(out_shape) + tuple(comm.out_shapes),
        in_specs=list(in_specs) + [anywhere] * k_in, out_specs=tuple(out_specs) + (anywhere,) * k_out,
        scratch_shapes=list(scratch_shapes) + comm.scratch, compiler_params=params,
    )(*args, *comm.inputs)
    return results[:n_out], results[n_out:]


class _Exchanges:
    FIRST = ("w_o", "w_ff_a", "w_ff_b")
    SECOND = ("w_ff_down", "w_ple_gate", "w_ple_in")
    EARLY = ("w_ff_a", "w_ff_b", "w_ff_down", "w_ple_gate", "w_ple_in", "w_o")
    LATE = ("w_in",)

    def __init__(self, shards, conv_rows):
        self.shards, self.conv_rows = shards, conv_rows
        self.mode = {name: mode for name, _, mode in BIG}

    def _natural(self, name, blocks):
        n, r, c = blocks.shape
        return _to_natural(blocks, name + "_natural") if self.mode[name] == "cols" else blocks.reshape(n * r, c)

    def gather_input(self):
        def when(position):
            step = pl.program_id(0)
            if position == "before":
                return [("start", step == 0)]
            return [("forward", step == NT - 1), ("finish", step == NT - 1)]
        return _GatherComm([self.shards["w_in"]]), when

    def weight_input(self, got):
        return self._natural("w_in", got[0])

    def gather_first(self):
        comm = _GatherComm([self.shards[n] for n in self.FIRST] + [self.conv_rows])

        def when(position):
            step = pl.program_id(0)
            if position == "before":
                return [("start", step == 0), ("forward", step == 3)]
            return [("finish", step == 3)]
        return comm, when

    def weights_first(self, got):
        out = {name: self._natural(name, blocks) for name, blocks in zip(self.FIRST, got)}
        out["conv_w8"] = _to_natural(got[-1], "conv_w_natural")
        return out

    def gather_second(self):
        comm = _GatherComm([self.shards[n] for n in self.SECOND])

        def when(position):
            j, i = pl.program_id(0), pl.program_id(1)
            if position == "before":
                return [("start", (j == 0) & (i == 0)), ("forward", (j == NJ - 3) & (i == 0))]
            return [("finish", (j == NJ - 1) & (i == NT - 1))]
        return comm, when

    def weights_second(self, got):
        return {name: self._natural(name, blocks) for name, blocks in zip(self.SECOND, got)}

    def to_sibling(self, early):
        self.by_core = [early[n].reshape((2, 4) + early[n].shape[1:]) for n in self.EARLY]
        return _SiblingComm(self.by_core), _first_and_last(NT)

    def reduce_on_chip(self, from_sibling):
        core = lax.axis_index("c").astype(jnp.int32).reshape(1)
        return _chip_reduce(self.by_core, from_sibling, core, "chip_reduce")

    def between_chips(self, chip_sums, small):
        self.small_keys = tuple(small)
        return _Both(_ChipComm(chip_sums), _ScatterComm([], [small[k] for k in self.small_keys])), _first_and_last(4)

    def last(self, dw_in):
        by_core = [dw_in.reshape((2, 4) + dw_in.shape[1:])]
        core = lax.axis_index("c").astype(jnp.int32).reshape(1)
        sums = _chip_reduce(by_core, _standalone(_SiblingComm(by_core), "w_in_grad_to_sibling"), core, "w_in_chip_reduce")
        return _ChipComm(sums), _first_and_last(NT)

    def collect(self, got_early, got_late):
        parts = dict(zip(self.EARLY, got_early[:len(self.EARLY)]))
        parts.update(zip(self.LATE, got_late))
        return parts, dict(zip(self.small_keys, got_early[len(self.EARLY):]))


def _first_and_last(n_steps):
    def when(position):
        step = pl.program_id(0)
        return [("start", step == 0)] if position == "before" else [("finish", step == n_steps - 1)]
    return when


def _standalone(comm, name):
    n_in = len(comm.inputs)

    def body(*refs):
        phases = comm.phases(refs[:n_in], refs[n_in:n_in + len(comm.out_shapes)], refs[n_in + len(comm.out_shapes):])
        phases["start"]()
        phases["finish"]()

    anywhere = pl.BlockSpec(memory_space=pl.ANY)
    return pl.pallas_call(
        body, name=name, out_shape=tuple(comm.out_shapes), in_specs=[anywhere] * n_in,
        out_specs=(anywhere,) * len(comm.out_shapes), scratch_shapes=comm.scratch,
    )(*comm.inputs)


def _chip_reduce(big, from_sibling, core, name):
    n = len(big)

    def body(core_ref, *refs):
        for a in range(n):
            mine, theirs, out = refs[a], refs[n + a], refs[2 * n + a]
            out[0] = (mine[0, 0].astype(F32) + theirs[0].astype(F32)).astype(BF16)

    def block(shape):
        return pl.BlockSpec((1,) + shape, lambda ch, core_ref: (ch, 0, 0))

    grid_spec = pltpu.PrefetchScalarGridSpec(
        num_scalar_prefetch=1, grid=(4,),
        in_specs=[pl.BlockSpec((1, 1) + b.shape[2:], lambda ch, core_ref: (core_ref[0], ch, 0, 0)) for b in big]
        + [block(b.shape[2:]) for b in big],
        out_specs=[block(b.shape[2:]) for b in big])
    return pl.pallas_call(
        body, name=name, grid_spec=grid_spec,
        out_shape=tuple(jax.ShapeDtypeStruct(b.shape[1:], BF16) for b in big),
        compiler_params=_params(("parallel",)),
    )(core, *big, *from_sibling)


def _adamw(g, w, m, v):
    nm = ADAM_B1 * m + (1.0 - ADAM_B1) * g
    nv = ADAM_B2 * v + (1.0 - ADAM_B2) * (g * g)
    m_hat = nm / (1.0 - ADAM_B1 ** ADAM_STEP)
    v_hat = nv / (1.0 - ADAM_B2 ** ADAM_STEP)
    return -ADAM_LR * (m_hat / (jnp.sqrt(v_hat) + ADAM_EPS) + ADAM_WD * w), nm, nv


def _adamw_sharded(parts, w, m, v, name):
    def body(p_ref, w_ref, m_ref, v_ref, g_ref, d_ref, nm_ref, nv_ref):
        g = p_ref[0].astype(F32)
        for s in range(1, parts.shape[0]):
            g = g + p_ref[s].astype(F32)
        delta, nm, nv = _adamw(g, w_ref[0], m_ref[0], v_ref[0])
        g_ref[0] = g
        d_ref[0] = delta
        nm_ref[0] = nm
        nv_ref[0] = nv

    n, r, c = parts.shape
    steps = 4 if r % 64 == 0 and r >= 512 else (2 if r % 32 == 0 and r >= 256 else 1)
    tile = pl.BlockSpec((1, r // steps, c), lambda i: (0, i, 0))
    return pl.pallas_call(
        body, name=name, grid=(steps,), out_shape=(jax.ShapeDtypeStruct(w.shape, F32),) * 4,
        in_specs=[pl.BlockSpec((n, r // steps, c), lambda i: (0, i, 0)), tile, tile, tile], out_specs=(tile,) * 4,
        compiler_params=_params(("parallel",)),
    )(parts, w, m, v)


REPLICATED = (("ln_z_g", "ln_z", 0), ("ln_z_b", "ln_z", 1), ("w_s", "w_s", None), ("b_s", "b_s", None),
              ("ln1_g", "ln1", 0), ("ln1_b", "ln1", 1), ("conv_w", "conv_mine", None), ("conv_b", "conv", 3),
              ("ln2_g", "tail", 3), ("ln2_b", "tail", 4), ("b_ple_gate", "tail", 0), ("ln3_g", "tail", 1),
              ("ln3_b", "tail", 2))
GATHERED = ("tail", "ln1", "ln_z", "conv", "w_s", "b_s", "loss", "conv_mine")


def _adamw_replicated(gathered, w, m, v):
    n_par = len(REPLICATED)

    def body(*refs):
        srcs = dict(zip(GATHERED, refs[:len(GATHERED)]))
        rest = refs[len(GATHERED):]
        w_refs, m_refs, v_refs = rest[:n_par], rest[n_par:2 * n_par], rest[2 * n_par:3 * n_par]
        outs = rest[3 * n_par:]
        loss_ref = outs[4 * n_par]
        sums = {}
        for key, ref in srcs.items():
            total = ref[0]
            for dev in range(1, N_DEV):
                total = total + ref[dev]
            sums[key] = total
        loss_ref[...] = sums["loss"]
        for n, (name, key, row) in enumerate(REPLICATED):
            if name == "conv_w":
                g = sums[key][0:3, :]
            elif row is None:
                g = sums[key]
            else:
                g = sums[key][row:row + 1, :]
            lead = len(w_refs[n].shape) - g.ndim
            idx = (0,) * lead + (Ellipsis,)
            delta, nm, nv = _adamw(g, w_refs[n][idx], m_refs[n][idx], v_refs[n][idx])
            for kind, val in enumerate((g, delta, nm, nv)):
                outs[kind * n_par + n][idx] = val

    names = [name for name, _, _ in REPLICATED]
    shapes = [jax.ShapeDtypeStruct(w[name].shape, F32) for name in names]
    args = [gathered[k] for k in GATHERED] + [w[n] for n in names] + [m[n] for n in names] + [v[n] for n in names]
    out_shape = tuple(shapes * 4) + (jax.ShapeDtypeStruct((8, LANES), F32),)
    return pl.pallas_call(
        body, name="adamw_replicated", grid=(1,), out_shape=out_shape,
        in_specs=[_full_spec(a.shape) for a in args], out_specs=tuple(_full_spec(s.shape) for s in out_shape),
        compiler_params=_params(("arbitrary",)),
    )(*args)


def kernel(x, p, positions, w_in, ln_z_g, ln_z_b, w_s, b_s, w_o, ln1_g, ln1_b, w_ff_a, w_ff_b, conv_w, conv_b, w_ff_down, ln2_g, ln2_b, w_ple_gate, b_ple_gate, w_ple_in, ln3_g, ln3_b, loss_target, m_w_in, m_ln_z_g, m_ln_z_b, m_w_s, m_b_s, m_w_o, m_ln1_g, m_ln1_b, m_w_ff_a, m_w_ff_b, m_conv_w, m_conv_b, m_w_ff_down, m_ln2_g, m_ln2_b, m_w_ple_gate, m_b_ple_gate, m_w_ple_in, m_ln3_g, m_ln3_b, v_w_in, v_ln_z_g, v_ln_z_b, v_w_s, v_b_s, v_w_o, v_ln1_g, v_ln1_b, v_w_ff_a, v_w_ff_b, v_conv_w, v_conv_b, v_w_ff_down, v_ln2_g, v_ln2_b, v_w_ple_gate, v_b_ple_gate, v_w_ple_in, v_ln3_g, v_ln3_b):
    w = dict(w_in=w_in, ln_z_g=ln_z_g, ln_z_b=ln_z_b, w_s=w_s, b_s=b_s, w_o=w_o, ln1_g=ln1_g, ln1_b=ln1_b,
             w_ff_a=w_ff_a, w_ff_b=w_ff_b, conv_w=conv_w, conv_b=conv_b, w_ff_down=w_ff_down, ln2_g=ln2_g,
             ln2_b=ln2_b, w_ple_gate=w_ple_gate, b_ple_gate=b_ple_gate, w_ple_in=w_ple_in, ln3_g=ln3_g,
             ln3_b=ln3_b)
    m = dict(w_in=m_w_in, ln_z_g=m_ln_z_g, ln_z_b=m_ln_z_b, w_s=m_w_s, b_s=m_b_s, w_o=m_w_o, ln1_g=m_ln1_g,
             ln1_b=m_ln1_b, w_ff_a=m_w_ff_a, w_ff_b=m_w_ff_b, conv_w=m_conv_w, conv_b=m_conv_b,
             w_ff_down=m_w_ff_down, ln2_g=m_ln2_g, ln2_b=m_ln2_b, w_ple_gate=m_w_ple_gate,
             b_ple_gate=m_b_ple_gate, w_ple_in=m_w_ple_in, ln3_g=m_ln3_g, ln3_b=m_ln3_b)
    v = dict(w_in=v_w_in, ln_z_g=v_ln_z_g, ln_z_b=v_ln_z_b, w_s=v_w_s, b_s=v_b_s, w_o=v_w_o, ln1_g=v_ln1_g,
             ln1_b=v_ln1_b, w_ff_a=v_w_ff_a, w_ff_b=v_w_ff_b, conv_w=v_conv_w, conv_b=v_conv_b,
             w_ff_down=v_w_ff_down, ln2_g=v_ln2_g, ln2_b=v_ln2_b, w_ple_gate=v_w_ple_gate,
             b_ple_gate=v_b_ple_gate, w_ple_in=v_w_ple_in, ln3_g=v_ln3_g, ln3_b=v_ln3_b)
    big_names = [name for name, _, _ in BIG]
    small_names = ("ln_z_g", "ln_z_b", "w_s", "b_s", "ln1_g", "ln1_b", "conv_b", "ln2_g", "ln2_b", "b_ple_gate",
                   "ln3_g", "ln3_b")

    transposed = {name for name, _, mode in BIG if mode == "rows_t"}

    def travel(a, name):
        return jnp.swapaxes(a, 1, 2) if name in transposed else a

    shards = dict(zip(big_names, _cast_shards([travel(w[n], n)[0] for n in big_names])))
    conv_rows = jnp.pad(w["conv_w"][0], ((0, 5), (0, 0)))
    sm = {n: w[n][0] if w[n].ndim > 2 else w[n] for n in small_names}
    pos_col = positions.reshape(S, 1).astype(F32)
    grad_x, (parts, small_all) = _local_step(x[0], p[0, 0], pos_col, loss_target[0], sm,
                                             _Exchanges(shards, conv_rows))
    me = 4 * lax.axis_index("x") + 2 * lax.axis_index("y") + lax.axis_index("c")
    conv_cols = small_all["conv"].reshape(N_DEV, 8, N_DEV, D_FF // N_DEV)
    small_all["conv_mine"] = lax.dynamic_index_in_dim(conv_cols, me, axis=2, keepdims=False)

    leaves = {}
    for name in big_names:
        outs = _adamw_sharded(parts[name], travel(w[name], name), travel(m[name], name), travel(v[name], name),
                              "adamw_" + name)
        leaves[name] = tuple(travel(o, name) for o in outs)
    rep = _adamw_replicated(small_all, w, m, v)
    n_rep = len(REPLICATED)
    for n, (name, _, _) in enumerate(REPLICATED):
        leaves[name] = tuple(rep[kind * n_rep + n] for kind in range(4))
    loss = rep[4 * n_rep][0, 0]
    return (loss, grad_x[None], *[leaves[n][kind] for kind in range(4) for n in WEIGHT_ORDER])
```

```python
import math

import numpy as np
import jax
import jax.numpy as jnp
from jax import lax
from jax.experimental import pallas as pl
from jax.experimental.pallas import tpu as pltpu

F32 = jnp.float32
BF16 = jnp.bfloat16
MESH = pl.DeviceIdType.MESH

N_DEV = 8
S = 4096
D = 1024
D_HALF = 512
D_IN = 2560
D_FF = 2816
D_PLE = 256
CHUNK = 128
DILATIONS = ((1, 32), (4, 8), (16, 2))
ROPE_THETA = 500000.0
LN_EPS = 1e-5
ALPHA = 2.0 ** 0.25
NEG_INF = -1e30
INV_SQRT2 = 1.0 / math.sqrt(2.0)
INV_SQRT_2PI = 1.0 / math.sqrt(2.0 * math.pi)

ADAM_LR, ADAM_B1, ADAM_B2, ADAM_EPS, ADAM_WD, ADAM_STEP = 0.001, 0.9, 0.999, 1e-08, 0.01, 10

TM = 512
NT = S // TM
ATTN_UNROLL = 8
TN = 1408
NJ = D_FF // TN
LANES = 128
VMEM_MIB = 1024 * 1024

BIG = (("w_in", (320, 1024), "rows_t"), ("w_o", (128, 1024), "rows"), ("w_ff_a", (352, 1024), "rows_t"),
       ("w_ff_b", (352, 1024), "rows_t"), ("w_ff_down", (352, 1024), "rows"), ("w_ple_gate", (128, 1024), "rows"),
       ("w_ple_in", (256, 128), "cols"))
WEIGHT_ORDER = ("w_in", "ln_z_g", "ln_z_b", "w_s", "b_s", "w_o", "ln1_g", "ln1_b", "w_ff_a", "w_ff_b",
                "conv_w", "conv_b", "w_ff_down", "ln2_g", "ln2_b", "w_ple_gate", "b_ple_gate",
                "w_ple_in", "ln3_g", "ln3_b")


def _params(semantics=None, vmem_mib=48):
    return pltpu.CompilerParams(dimension_semantics=semantics, vmem_limit_bytes=vmem_mib * VMEM_MIB)


def _dot(a, b):
    return jnp.dot(a, b, preferred_element_type=F32)


def _dot_nt(a, b):
    return lax.dot_general(a, b, (((1,), (1,)), ((), ())), preferred_element_type=F32)


def _dot_tn(a, b):
    return lax.dot_general(a, b, (((0,), (0,)), ((), ())), preferred_element_type=F32)


def _gelu(x):
    return 0.5 * x * (1.0 + lax.erf(x * INV_SQRT2))


def _gelu_grad(x):
    return 0.5 * (1.0 + lax.erf(x * INV_SQRT2)) + x * (jnp.exp(-0.5 * x * x) * INV_SQRT_2PI)


def _ln_stats(y):
    mu = jnp.mean(y, axis=-1, keepdims=True)
    yc = y - mu
    var = jnp.mean(yc * yc, axis=-1, keepdims=True)
    rstd = lax.rsqrt(var + LN_EPS)
    return yc * rstd, rstd


def _ln_bwd(dxhat, xhat, rstd):
    m1 = jnp.mean(dxhat, axis=-1, keepdims=True)
    m2 = jnp.mean(dxhat * xhat, axis=-1, keepdims=True)
    return rstd * (dxhat - m1 - xhat * m2)


def _colsum(x):
    return jnp.sum(x, axis=0, keepdims=True)


def _rows(i):
    return (i, 0)


def _fixed(*_):
    return (0, 0)


def _row_spec(width):
    return pl.BlockSpec((TM, width), _rows)


def _full_spec(shape):
    return pl.BlockSpec(shape, lambda *_: (0,) * len(shape))


def _owner_slot(j):
    return (j % 2) * 4 + j // 2


def _lane_lo():
    return lax.broadcasted_iota(jnp.int32, (CHUNK, LANES), 1) < 64


def _tril():
    r = lax.broadcasted_iota(jnp.int32, (CHUNK, CHUNK), 0)
    c = lax.broadcasted_iota(jnp.int32, (CHUNK, CHUNK), 1)
    return c <= r


def _rope_consts():
    lane = np.arange(LANES) % 64
    j = lane % 8
    inv = np.where(lane < 16, np.float32(ROPE_THETA) ** (-(2.0 * j).astype(np.float32) / np.float32(16.0)), 0.0)
    m_lo = (lane < 8).astype(np.float32)
    m_hi = ((lane >= 8) & (lane < 16)).astype(np.float32)
    return (jnp.asarray(inv, F32).reshape(1, LANES), jnp.asarray(m_lo).reshape(1, LANES),
            jnp.asarray(m_hi).reshape(1, LANES))


def _rope_tables(pos_col, carried=None):
    inv, m_lo, m_hi = _rope_consts()

    def body(pos_ref, inv_ref, lo_ref, hi_ref, c_ref, sa_ref, sb_ref):
        ang = pos_ref[...] * inv_ref[...]
        c = jnp.cos(ang)
        s = jnp.sin(ang)
        lo = lo_ref[...]
        hi = hi_ref[...]
        c_ref[...] = jnp.where(lo + hi > 0.0, c, 1.0)
        sa_ref[...] = s * hi
        sb_ref[...] = -s * lo

    vec = _full_spec((1, LANES))
    out = jax.ShapeDtypeStruct((S, LANES), F32)
    return _host_call(
        body, carried, name="rope_tables", grid=(NT,), out_shape=(out, out, out),
        in_specs=[pl.BlockSpec((TM, 1), _rows), vec, vec, vec],
        out_specs=(_row_spec(LANES),) * 3, scratch_shapes=[], params=_params(("arbitrary",)),
        args=(pos_col, inv, m_lo, m_hi))


def _rope(t, c, sa, sb):
    return t * c + pltpu.roll(t, 8, 1) * sa + pltpu.roll(t, LANES - 8, 1) * sb


def _rope_t(dy, c, sa, sb):
    return dy * c + pltpu.roll(dy * sa, LANES - 8, 1) + pltpu.roll(dy * sb, 8, 1)


def _masked_ws(ws_ref):
    tril = _tril()
    return [jnp.where(tril, ws_ref[g], 0.0).astype(BF16) for g in range(8)]


def _spatial_mix(zn, wm, bs):
    lo = _lane_lo()
    rows = []
    for ch in range(TM // CHUNK):
        slabs = []
        for pr in range(4):
            zp = zn[ch * CHUNK:(ch + 1) * CHUNK, pr * LANES:(pr + 1) * LANES].astype(BF16)
            slabs.append(jnp.where(lo, _dot(wm[2 * pr], zp), _dot(wm[2 * pr + 1], zp)))
        rows.append(jnp.concatenate(slabs, axis=1) + bs)
    return jnp.concatenate(rows, axis=0)


def _proj_in_fwd(x, w_in, tabs, ln_z_g, ln_z_b, w_s, bs_exp):
    def body(x_ref, w_ref, c_ref, sa_ref, sb_ref, g_ref, b_ref, ws_ref, bs_ref,
             q_ref, k_ref, v_ref, u_ref, z_ref, gm_ref, xb_ref):
        xb = x_ref[...].astype(BF16)
        xb_ref[...] = xb
        c, sa, sb = c_ref[...], sa_ref[...], sb_ref[...]
        hq = _dot_nt(xb, w_ref[0:512, :])
        hk = _dot_nt(xb, w_ref[512:1024, :])
        for s in range(4):
            sl = slice(s * LANES, (s + 1) * LANES)
            q_ref[:, sl] = _rope(hq[:, sl], c, sa, sb)
            k_ref[:, sl] = _rope(hk[:, sl], c, sa, sb)
        v_ref[...] = _dot_nt(xb, w_ref[1024:1536, :])
        u_pre = _dot_nt(xb, w_ref[1536:2048, :])
        z_pre = _dot_nt(xb, w_ref[2048:2560, :])
        u_ref[...] = u_pre
        z_ref[...] = z_pre
        zhat, _ = _ln_stats(_gelu(z_pre))
        zn = zhat * g_ref[...] + b_ref[...]
        mixed = _spatial_mix(zn, _masked_ws(ws_ref), bs_ref[...])
        gm_ref[...] = (_gelu(u_pre) * mixed).astype(BF16)

    half = jax.ShapeDtypeStruct((S, D_HALF), F32)
    tab = _row_spec(LANES)
    return pl.pallas_call(
        body, name="proj_in_fwd", grid=(NT,),
        out_shape=(half, half, half, half, half, jax.ShapeDtypeStruct((S, D_HALF), BF16),
                   jax.ShapeDtypeStruct((S, D), BF16)),
        in_specs=[_row_spec(D), _full_spec((D_IN, D)), tab, tab, tab, _full_spec((1, D_HALF)),
                  _full_spec((1, D_HALF)), _full_spec((8, CHUNK, CHUNK)), _full_spec((CHUNK, D_HALF))],
        out_specs=(_row_spec(D_HALF),) * 6 + (_row_spec(D),),
        compiler_params=_params(("parallel",)),
    )(x, w_in, *tabs, ln_z_g, ln_z_b, w_s, bs_exp)


def _store_band_bias(bias_ref):
    qi = lax.broadcasted_iota(jnp.int32, (CHUNK, 2 * CHUNK), 0)
    kj = lax.broadcasted_iota(jnp.int32, (CHUNK, 2 * CHUNK), 1)
    band = (kj >= qi) & (kj <= qi + CHUNK)
    bias_ref[0] = jnp.where(band, 0.0, NEG_INF)
    bias_ref[1] = jnp.where(band & (kj >= CHUNK), 0.0, NEG_INF)


def _permuted_rows(ref, d, r):
    return ref[...] if d == 1 else ref[pl.ds(r, S // d, stride=d), :]


def _attention_fwd(q, k, v, carried=None):
    def body(q_ref, k_ref, v_ref, o_ref, lse_ref, qb, kb, v0b, v1b, bias, op, lp, ob0, lb0, ob1, lb1, ob2, lb2):
        lo = _lane_lo()
        lo_f = lo.astype(F32)[0:1, :]
        hi_f = 1.0 - lo_f
        zero_pad = jnp.zeros((CHUNK, LANES), BF16)
        for buf in (qb, kb, v0b, v1b):
            buf[0:CHUNK, :] = zero_pad
        _store_band_bias(bias)
        outs = ((ob0, lb0), (ob1, lb1), (ob2, lb2))
        for (d, nb), (ob, lb) in zip(DILATIONS, outs):
            length = S // d
            for r in range(d):
                dst = slice(CHUNK + r * length, CHUNK + (r + 1) * length)
                qb[dst, :] = (_permuted_rows(q_ref, d, r) * 0.125).astype(BF16)
                kb[dst, :] = _permuted_rows(k_ref, d, r).astype(BF16)
                vs = _permuted_rows(v_ref, d, r)
                v0b[dst, :] = (vs * lo_f + hi_f).astype(BF16)
                v1b[dst, :] = (vs * hi_f + lo_f).astype(BF16)

            def block(b, carry, nb=nb):
                base = pl.multiple_of(b * CHUNK, CHUNK)
                add = bias[jnp.where(b % nb == 0, 1, 0)]
                qblk = qb[pl.ds(pl.multiple_of(base + CHUNK, CHUNK), CHUNK), :]
                kblk = kb[pl.ds(base, 2 * CHUNK), :]
                q2 = jnp.concatenate([jnp.where(lo, qblk, 0), jnp.where(lo, 0, qblk)], axis=0)
                s2 = _dot_nt(q2, kblk) + jnp.concatenate([add, add], axis=0)
                m2 = jnp.max(s2, axis=-1, keepdims=True)
                p2 = jnp.exp(s2 - m2).astype(BF16)
                pv, mx = [], []
                for head, vh in enumerate((v0b, v1b)):
                    rows = slice(head * CHUNK, (head + 1) * CHUNK)
                    pv.append(_dot(p2[rows, :], vh[pl.ds(base, 2 * CHUNK), :]))
                    mx.append(m2[rows, :])
                den = pltpu.roll(jnp.where(lo, pv[1], pv[0]), 64, 1)
                op[pl.ds(base, CHUNK), :] = jnp.where(lo, pv[0], pv[1]) / den
                lp[pl.ds(base, CHUNK), :] = jnp.where(lo, mx[0], mx[1]) + jnp.log(den)
                return carry

            lax.fori_loop(0, S // CHUNK, block, 0, unroll=ATTN_UNROLL)
            for r in range(d):
                src = slice(r * length, (r + 1) * length)
                if d == 1:
                    ob[...] = op[...]
                    lb[...] = lp[...]
                else:
                    ob[pl.ds(r, length, stride=d), :] = op[src, :]
                    lb[pl.ds(r, length, stride=d), :] = lp[src, :]
        for t in range(NT):
            rows = slice(t * TM, (t + 1) * TM)
            l0, l1, l2 = lb0[rows, :], lb1[rows, :], lb2[rows, :]
            mx = jnp.maximum(jnp.maximum(l0, l1), l2)
            e0, e1, e2 = jnp.exp(l0 - mx), jnp.exp(l1 - mx), jnp.exp(l2 - mx)
            den = e0 + e1 + e2
            o_ref[rows, :] = (e0 * ob0[rows, :] + e1 * ob1[rows, :] + e2 * ob2[rows, :]) / den
            lse_ref[rows, :] = mx + jnp.log(den)

    slab = pl.BlockSpec((S, LANES), lambda h: (0, h))
    out = jax.ShapeDtypeStruct((S, D_HALF), F32)
    padded = pltpu.VMEM((CHUNK + S, LANES), BF16)
    whole = pltpu.VMEM((S, LANES), F32)
    return _host_call(
        body, carried, name="attention_fwd", grid=(4,), out_shape=(out, out),
        in_specs=[slab, slab, slab], out_specs=(slab, slab),
        scratch_shapes=[padded] * 4 + [pltpu.VMEM((2, CHUNK, 2 * CHUNK), F32)] + [whole] * 8,
        params=_params(("arbitrary",), 56), args=(q, k, v))


def _mix_ln1_fwd(attn, gm, w_o, x, g1, b1):
    def body(a_ref, gm_ref, w_ref, x_ref, g_ref, b_ref, xhat_ref, rstd_ref, x1b_ref):
        mix = _dot(a_ref[...].astype(BF16), w_ref[0:D_HALF, :]) + _dot(gm_ref[...], w_ref[D_HALF:D, :])
        xhat, rstd = _ln_stats(ALPHA * x_ref[...] + mix)
        xhat_ref[...] = xhat
        rstd_ref[...] = rstd
        x1b_ref[...] = (xhat * g_ref[...] + b_ref[...]).astype(BF16)

    vec = _full_spec((1, D))
    return pl.pallas_call(
        body, name="mix_ln1_fwd", grid=(NT,),
        out_shape=(jax.ShapeDtypeStruct((S, D), F32), jax.ShapeDtypeStruct((S, 1), F32),
                   jax.ShapeDtypeStruct((S, D), BF16)),
        in_specs=[_row_spec(D_HALF), _row_spec(D_HALF), _full_spec((D, D)), _row_spec(D), vec, vec],
        out_specs=(_row_spec(D), pl.BlockSpec((TM, 1), _rows), _row_spec(D)),
        compiler_params=_params(("parallel",)),
    )(attn, gm, w_o, x, g1, b1)


def _ffn_up_fwd(x1b, w_a, w_b, conv_w8, conv_b, carried=None):
    def body(x_ref, wa_ref, wb_ref, cw_ref, cb_ref, ap_ref, a_ref, bl_ref, h_ref, carry):
        @pl.when(pl.program_id(1) == 0)
        def _():
            carry[...] = jnp.zeros_like(carry)

        xb = x_ref[...]
        ap = _dot_nt(xb, wa_ref[...])
        bl = _dot_nt(xb, wb_ref[...])
        row = lax.broadcasted_iota(jnp.int32, (TM, TN), 0)
        c6, c7 = carry[6:7, :], carry[7:8, :]
        m1 = jnp.where(row == 0, c7, pltpu.roll(ap, 1, 0))
        m2 = jnp.where(row == 0, c6, jnp.where(row == 1, c7, pltpu.roll(ap, 2, 0)))
        a = cb_ref[...] + cw_ref[0:1, :] * m2 + cw_ref[1:2, :] * m1 + cw_ref[2:3, :] * ap
        carry[...] = ap[TM - 8:TM, :]
        ap_ref[...] = ap.astype(BF16)
        a_ref[...] = a
        bl_ref[...] = bl.astype(BF16)
        h_ref[...] = (_gelu(a) * bl).astype(BF16)

    tile = pl.BlockSpec((TM, TN), lambda j, i: (i, j))
    wcol = pl.BlockSpec((TN, D), lambda j, i: (j, 0))
    ff = jax.ShapeDtypeStruct((S, D_FF), F32)
    ffb = jax.ShapeDtypeStruct((S, D_FF), BF16)
    return _host_call(
        body, carried, name="ffn_up_fwd", grid=(NJ, NT),
        out_shape=(ffb, ff, ffb, ffb),
        in_specs=[pl.BlockSpec((TM, D), lambda j, i: (i, 0)), wcol, wcol,
                  pl.BlockSpec((8, TN), lambda j, i: (0, j)), pl.BlockSpec((1, TN), lambda j, i: (0, j))],
        out_specs=(tile, tile, tile, tile),
        scratch_shapes=[pltpu.VMEM((8, TN), F32)],
        params=_params(("arbitrary", "arbitrary"), 56), args=(x1b, w_a, w_b, conv_w8, conv_b))


def _ffn_down_ln2_fwd(hff, w_down, xhat1, g1, b1):
    def body(h_ref, w_ref, xh_ref, g_ref, b_ref, xhat_ref, rstd_ref):
        x1 = xh_ref[...] * g_ref[...] + b_ref[...]
        xhat, rstd = _ln_stats(ALPHA * x1 + _dot(h_ref[...], w_ref[...]))
        xhat_ref[...] = xhat
        rstd_ref[...] = rstd

    vec = _full_spec((1, D))
    return pl.pallas_call(
        body, name="ffn_down_ln2_fwd", grid=(NT,),
        out_shape=(jax.ShapeDtypeStruct((S, D), F32), jax.ShapeDtypeStruct((S, 1), F32)),
        in_specs=[_row_spec(D_FF), _full_spec((D_FF, D)), _row_spec(D), vec, vec],
        out_specs=(_row_spec(D), pl.BlockSpec((TM, 1), _rows)),
        compiler_params=_params(("parallel",)),
    )(hff, w_down, xhat1, g1, b1)


def _tail_fwd_bwd(xhat2, rstd2, p, target, w_g, w_p, g2, b2, bg, g3, b3):
    def body(xh_ref, rs_ref, p_ref, t_ref, wg_ref, wp_ref, g2_ref, b2_ref, bg_ref, g3_ref, b3_ref,
             loss_ref, dy2_ref, dy2b_ref, gwg_ref, gwp_ref, vec_ref, dwg_ref, dwp_ref):
        @pl.when(pl.program_id(0) == 0)
        def _():
            loss_ref[...] = jnp.zeros_like(loss_ref)
            dwg_ref[...] = jnp.zeros_like(dwg_ref)
            dwp_ref[...] = jnp.zeros_like(dwp_ref)
            vec_ref[...] = jnp.zeros_like(vec_ref)

        xhat2_t = xh_ref[...]
        x2 = xhat2_t * g2_ref[...] + b2_ref[...]
        x2b = x2.astype(BF16)
        pb = p_ref[...].astype(BF16)
        gate = jax.nn.sigmoid(_dot(x2b, wg_ref[...]) + bg_ref[...])
        pin = _dot(pb, wp_ref[...])
        xhat3, rstd3 = _ln_stats(ALPHA * x2 + gate * pin)
        err = xhat3 * g3_ref[...] + b3_ref[...] - t_ref[...]
        loss_ref[...] += jnp.sum(jnp.mean(err * err, axis=-1, keepdims=True), axis=0, keepdims=True) * 0.5
        dout = err * (1.0 / D)
        dy3 = _ln_bwd(dout * g3_ref[...], xhat3, rstd3)
        dgp = dy3 * pin * gate * (1.0 - gate)
        dgpb = dgp.astype(BF16)
        dwg_ref[...] += _dot_tn(x2b, dgpb)
        dwp_ref[...] += _dot_tn(pb, (dy3 * gate).astype(BF16))
        dx2 = ALPHA * dy3 + _dot_nt(dgpb, wg_ref[...])
        dy2 = _ln_bwd(dx2 * g2_ref[...], xhat2_t, rs_ref[...])
        dy2_ref[...] = dy2
        dy2b_ref[...] = dy2.astype(BF16)
        vec_ref[0:1, :] += _colsum(dgp)
        vec_ref[1:2, :] += _colsum(dout * xhat3)
        vec_ref[2:3, :] += _colsum(dout)
        vec_ref[3:4, :] += _colsum(dx2 * xhat2_t)
        vec_ref[4:5, :] += _colsum(dx2)

        @pl.when(pl.program_id(0) == NT - 1)
        def _():
            for j in range(N_DEV):
                gwg_ref[_owner_slot(j)] = dwg_ref[LANES * j:LANES * (j + 1), :].astype(BF16)
                gwp_ref[_owner_slot(j)] = dwp_ref[:, LANES * j:LANES * (j + 1)].astype(BF16)

    vec = _full_spec((1, D))
    return pl.pallas_call(
        body, name="tail_fwd_bwd", grid=(NT,),
        out_shape=(jax.ShapeDtypeStruct((8, LANES), F32), jax.ShapeDtypeStruct((S, D), F32),
                   jax.ShapeDtypeStruct((S, D), BF16), jax.ShapeDtypeStruct((N_DEV, D // N_DEV, D), BF16),
                   jax.ShapeDtypeStruct((N_DEV, D_PLE, D // N_DEV), BF16), jax.ShapeDtypeStruct((8, D), F32)),
        in_specs=[_row_spec(D), pl.BlockSpec((TM, 1), _rows), _row_spec(D_PLE), _row_spec(D),
                  _full_spec((D, D)), _full_spec((D_PLE, D)), vec, vec, vec, vec, vec],
        out_specs=(_full_spec((8, LANES)), _row_spec(D), _row_spec(D), _full_spec((N_DEV, D // N_DEV, D)),
                   _full_spec((N_DEV, D_PLE, D // N_DEV)), _full_spec((8, D))),
        scratch_shapes=[pltpu.VMEM((D, D), F32), pltpu.VMEM((D_PLE, D), F32)],
        compiler_params=_params(("arbitrary",)),
    )(xhat2, rstd2, p, target, w_g, w_p, g2, b2, bg, g3, b3)


def _ffn_bwd_act(dy2b, w_down, a_pre, a, b_lin, hff, conv_w8):
    def body(dy_ref, wd_ref, ap_ref, a_ref, bl_ref, h_ref, cw_ref, dap_ref, dbl_ref, dwd_ref, dcw_ref, carry):
        @pl.when(pl.program_id(1) == 0)
        def _():
            carry[...] = jnp.zeros_like(carry)
            dwd_ref[...] = jnp.zeros_like(dwd_ref)
            dcw_ref[...] = jnp.zeros_like(dcw_ref)

        dyb = dy_ref[...]
        dh = _dot_nt(dyb, wd_ref[...])
        av = a_ref[...]
        cdf = 0.5 * (1.0 + lax.erf(av * INV_SQRT2))
        dbl_ref[...] = (dh * (av * cdf)).astype(BF16)
        da = dh * bl_ref[...].astype(F32) * (cdf + av * (jnp.exp(-0.5 * av * av) * INV_SQRT_2PI))
        row = lax.broadcasted_iota(jnp.int32, (TM, TN), 0)
        c0, c1 = carry[0:1, :], carry[1:2, :]
        p1 = jnp.where(row == TM - 1, c0, pltpu.roll(da, TM - 1, 0))
        p2 = jnp.where(row == TM - 2, c0, jnp.where(row == TM - 1, c1, pltpu.roll(da, TM - 2, 0)))
        carry[...] = da[0:8, :]
        ap = ap_ref[...].astype(F32)
        dcw_ref[3:4, :] += _colsum(da)
        dcw_ref[0:1, :] += _colsum(ap * p2)
        dcw_ref[1:2, :] += _colsum(ap * p1)
        dcw_ref[2:3, :] += _colsum(ap * da)
        dap_ref[...] = (cw_ref[2:3, :] * da + cw_ref[1:2, :] * p1 + cw_ref[0:1, :] * p2).astype(BF16)
        dwd_ref[...] += _dot_tn(h_ref[...], dyb)

    rev_tile = pl.BlockSpec((TM, TN), lambda j, i: (NT - 1 - i, j))
    wrows = pl.BlockSpec((TN, D), lambda j, i: (j, 0))
    small = pl.BlockSpec((8, TN), lambda j, i: (0, j))
    ffb = jax.ShapeDtypeStruct((S, D_FF), BF16)
    return pl.pallas_call(
        body, name="ffn_bwd_act", grid=(NJ, NT),
        out_shape=(ffb, ffb, jax.ShapeDtypeStruct((D_FF, D), F32), jax.ShapeDtypeStruct((8, D_FF), F32)),
        in_specs=[pl.BlockSpec((TM, D), lambda j, i: (NT - 1 - i, 0)), wrows, rev_tile, rev_tile, rev_tile,
                  rev_tile, small],
        out_specs=(rev_tile, rev_tile, wrows, small),
        scratch_shapes=[pltpu.VMEM((8, TN), F32)],
        compiler_params=_params(("arbitrary", "arbitrary"), 56),
    )(dy2b, w_down, a_pre, a, b_lin, hff, conv_w8)


def _ffn_bwd_w(dap, dbl, x1b):
    def body(dap_ref, dbl_ref, x_ref, dwa_ref, dwb_ref):
        @pl.when(pl.program_id(1) == 0)
        def _():
            dwa_ref[...] = jnp.zeros_like(dwa_ref)
            dwb_ref[...] = jnp.zeros_like(dwb_ref)

        xb = x_ref[...]
        dwa_ref[...] += _dot_tn(dap_ref[...], xb)
        dwb_ref[...] += _dot_tn(dbl_ref[...], xb)

    tile = pl.BlockSpec((TM, TN), lambda j, i: (i, j))
    wrows = pl.BlockSpec((TN, D), lambda j, i: (j, 0))
    full = jax.ShapeDtypeStruct((D_FF, D), F32)
    return pl.pallas_call(
        body, name="ffn_bwd_w", grid=(NJ, NT), out_shape=(full, full),
        in_specs=[tile, tile, pl.BlockSpec((TM, D), lambda j, i: (i, 0))], out_specs=(wrows, wrows),
        compiler_params=_params(("arbitrary", "arbitrary"), 56),
    )(dap, dbl, x1b)


def _ffn_bwd_x(dap, dbl, w_a, w_b):
    def body(dap_ref, dbl_ref, wa_ref, wb_ref, dx_ref):
        dx_ref[...] = _dot(dap_ref[...], wa_ref[...]) + _dot(dbl_ref[...], wb_ref[...])

    return pl.pallas_call(
        body, name="ffn_bwd_x", grid=(NT,), out_shape=jax.ShapeDtypeStruct((S, D), F32),
        in_specs=[_row_spec(D_FF), _row_spec(D_FF), _full_spec((D_FF, D)), _full_spec((D_FF, D))],
        out_specs=_row_spec(D), compiler_params=_params(("parallel",), 56),
    )(dap, dbl, w_a, w_b)


def _ln1_mix_bwd(dy2, dx1_ffn, xhat1, rstd1, g1, attn, gm, w_o, carried=None):
    def body(dy2_ref, dxf_ref, xh_ref, rs_ref, g_ref, a_ref, gm_ref, w_ref,
             dy1_ref, da_ref, dlt_ref, dgm_ref, gwo_ref, vec_ref, dwo_ref):
        @pl.when(pl.program_id(0) == 0)
        def _():
            dwo_ref[...] = jnp.zeros_like(dwo_ref)
            vec_ref[...] = jnp.zeros_like(vec_ref)

        xhat = xh_ref[...]
        dx1 = ALPHA * dy2_ref[...] + dxf_ref[...]
        vec_ref[0:1, :] += _colsum(dx1 * xhat)
        vec_ref[1:2, :] += _colsum(dx1)
        dy1 = _ln_bwd(dx1 * g_ref[...], xhat, rs_ref[...])
        dy1_ref[...] = dy1
        dy1b = dy1.astype(BF16)
        dmix = _dot_nt(dy1b, w_ref[...])
        attn_t = a_ref[...]
        d_attn = dmix[:, 0:D_HALF]
        da_ref[...] = d_attn
        dgm_ref[...] = dmix[:, D_HALF:D]
        lo = (lax.broadcasted_iota(jnp.int32, (TM, LANES), 1) < 64)
        for s in range(4):
            sl = slice(s * LANES, (s + 1) * LANES)
            prod = d_attn[:, sl] * attn_t[:, sl]
            s0 = jnp.sum(jnp.where(lo, prod, 0.0), axis=-1, keepdims=True)
            s1 = jnp.sum(jnp.where(lo, 0.0, prod), axis=-1, keepdims=True)
            dlt_ref[:, sl] = jnp.where(lo, s0, s1)
        dwo_ref[0:D_HALF, :] += _dot_tn(attn_t.astype(BF16), dy1b)
        dwo_ref[D_HALF:D, :] += _dot_tn(gm_ref[...], dy1b)

        @pl.when(pl.program_id(0) == NT - 1)
        def _():
            for j in range(N_DEV):
                gwo_ref[_owner_slot(j)] = dwo_ref[LANES * j:LANES * (j + 1), :].astype(BF16)

    half = jax.ShapeDtypeStruct((S, D_HALF), F32)
    return _host_call(
        body, carried, name="ln1_mix_bwd", grid=(NT,),
        out_shape=(jax.ShapeDtypeStruct((S, D), F32), half, half, half,
                   jax.ShapeDtypeStruct((N_DEV, D // N_DEV, D), BF16), jax.ShapeDtypeStruct((8, D), F32)),
        in_specs=[_row_spec(D), _row_spec(D), _row_spec(D), pl.BlockSpec((TM, 1), _rows), _full_spec((1, D)),
                  _row_spec(D_HALF), _row_spec(D_HALF), _full_spec((D, D))],
        out_specs=(_row_spec(D), _row_spec(D_HALF), _row_spec(D_HALF), _row_spec(D_HALF),
                   _full_spec((N_DEV, D // N_DEV, D)), _full_spec((8, D))),
        scratch_shapes=[pltpu.VMEM((D, D), F32)],
        params=_params(("arbitrary",)), args=(dy2, dx1_ffn, xhat1, rstd1, g1, attn, gm, w_o))


def _gmlp_bwd(d_gm, u_pre, z_pre, ln_z_g, ln_z_b, w_s, bs_exp, carried=None):
    def body(dg_ref, u_ref, z_ref, g_ref, b_ref, ws_ref, bs_ref, du_ref, dz_ref, dws_ref, dbs_ref, vec_ref):
        @pl.when(pl.program_id(0) == 0)
        def _():
            dws_ref[...] = jnp.zeros_like(dws_ref)
            dbs_ref[...] = jnp.zeros_like(dbs_ref)
            vec_ref[...] = jnp.zeros_like(vec_ref)

        u_pre_t, z_pre_t, dgm = u_ref[...], z_ref[...], dg_ref[...]
        zhat, rstd = _ln_stats(_gelu(z_pre_t))
        zn = zhat * g_ref[...] + b_ref[...]
        wm = _masked_ws(ws_ref)
        mixed = _spatial_mix(zn, wm, bs_ref[...])
        du_ref[...] = (dgm * mixed * _gelu_grad(u_pre_t)).astype(BF16)
        dmixed = dgm * _gelu(u_pre_t)
        lo = _lane_lo()
        tril = _tril()
        group_of_lane = lax.broadcasted_iota(jnp.int32, (8, D_HALF), 1) // 64
        pick = (group_of_lane == lax.broadcasted_iota(jnp.int32, (8, D_HALF), 0)).astype(F32)
        dzn_rows = []
        for ch in range(TM // CHUNK):
            rows = slice(ch * CHUNK, (ch + 1) * CHUNK)
            dbs_ref[...] += lax.dot_general(pick, dmixed[rows, :], (((1,), (1,)), ((), ())),
                                            precision=lax.Precision.HIGHEST, preferred_element_type=F32)
            slabs = []
            for pr in range(4):
                sl = slice(pr * LANES, (pr + 1) * LANES)
                dm = dmixed[rows, sl]
                zp = zn[rows, sl].astype(BF16)
                dm_lo = jnp.where(lo, dm, 0.0).astype(BF16)
                dm_hi = jnp.where(lo, 0.0, dm).astype(BF16)
                dws_ref[2 * pr] += jnp.where(tril, _dot_nt(dm_lo, zp), 0.0)
                dws_ref[2 * pr + 1] += jnp.where(tril, _dot_nt(dm_hi, zp), 0.0)
                dmb = dm.astype(BF16)
                slabs.append(jnp.where(lo, _dot_tn(wm[2 * pr], dmb), _dot_tn(wm[2 * pr + 1], dmb)))
            dzn_rows.append(jnp.concatenate(slabs, axis=1))
        dzn = jnp.concatenate(dzn_rows, axis=0)
        vec_ref[0:1, :] += _colsum(dzn * zhat)
        vec_ref[1:2, :] += _colsum(dzn)
        dz = _ln_bwd(dzn * g_ref[...], zhat, rstd)
        dz_ref[...] = (dz * _gelu_grad(z_pre_t)).astype(BF16)

    halfb = jax.ShapeDtypeStruct((S, D_HALF), BF16)
    vec = _full_spec((1, D_HALF))
    return _host_call(
        body, carried, name="gmlp_bwd", grid=(NT,),
        out_shape=(halfb, halfb, jax.ShapeDtypeStruct((8, CHUNK, CHUNK), F32),
                   jax.ShapeDtypeStruct((8, CHUNK), F32), jax.ShapeDtypeStruct((8, D_HALF), F32)),
        in_specs=[_row_spec(D_HALF), _row_spec(D_HALF), _row_spec(D_HALF), vec, vec,
                  _full_spec((8, CHUNK, CHUNK)), _full_spec((CHUNK, D_HALF))],
        out_specs=(_row_spec(D_HALF), _row_spec(D_HALF), _full_spec((8, CHUNK, CHUNK)),
                   _full_spec((8, CHUNK)), _full_spec((8, D_HALF))),
        scratch_shapes=[], params=_params(("arbitrary",)), args=(d_gm, u_pre, z_pre, ln_z_g, ln_z_b, w_s, bs_exp))


def _attention_bwd(q, k, v, lse, d_attn, delta, tabs, carried=None):
    def body(q_ref, k_ref, v_ref, l_ref, do_ref, dl_ref, c_ref, sa_ref, sb_ref, dq_ref, dk_ref, dv_ref,
             qb, kb, vb, gb, bias, lsp, dlp, dqp, dk_own, dk_prev, dv_own, dv_prev, dqa, dka, dva):
        lo = _lane_lo()
        zero_pad = jnp.zeros((CHUNK, LANES), BF16)
        for buf in (qb, kb, vb, gb):
            buf[0:CHUNK, :] = zero_pad
        dk_prev[S:S + CHUNK, :] = jnp.zeros((CHUNK, LANES), F32)
        dv_prev[S:S + CHUNK, :] = jnp.zeros((CHUNK, LANES), F32)
        _store_band_bias(bias)
        for d, nb in DILATIONS:
            length = S // d
            for r in range(d):
                dst = slice(CHUNK + r * length, CHUNK + (r + 1) * length)
                src = slice(r * length, (r + 1) * length)
                qb[dst, :] = (_permuted_rows(q_ref, d, r) * 0.125).astype(BF16)
                kb[dst, :] = _permuted_rows(k_ref, d, r).astype(BF16)
                vb[dst, :] = _permuted_rows(v_ref, d, r).astype(BF16)
                gb[dst, :] = _permuted_rows(do_ref, d, r).astype(BF16)
                lsp[src, :] = _permuted_rows(l_ref, d, r)
                dlp[src, :] = _permuted_rows(dl_ref, d, r)

            def block(b, carry, nb=nb):
                base = pl.multiple_of(b * CHUNK, CHUNK)
                own = pl.multiple_of(base + CHUNK, CHUNK)
                add = bias[jnp.where(b % nb == 0, 1, 0)]
                qblk = qb[pl.ds(own, CHUNK), :]
                gblk = gb[pl.ds(own, CHUNK), :]
                kblk = kb[pl.ds(base, 2 * CHUNK), :]
                vblk = vb[pl.ds(base, 2 * CHUNK), :]
                lse_t = lsp[pl.ds(base, CHUNK), :]
                dlt_t = dlp[pl.ds(base, CHUNK), :]
                q2 = jnp.concatenate([jnp.where(lo, qblk, 0), jnp.where(lo, 0, qblk)], axis=0)
                g2 = jnp.concatenate([jnp.where(lo, gblk, 0), jnp.where(lo, 0, gblk)], axis=0)
                lse2 = jnp.concatenate([lse_t[:, 0:1], lse_t[:, 64:65]], axis=0)
                dlt2 = jnp.concatenate([dlt_t[:, 0:1], dlt_t[:, 64:65]], axis=0)
                add2 = jnp.concatenate([add, add], axis=0)
                p = jnp.exp(_dot_nt(q2, kblk) + add2 - lse2)
                ds = (p * (_dot_nt(g2, vblk) - dlt2)).astype(BF16)
                dv_blk = _dot_tn(p.astype(BF16), g2)
                dk_blk = _dot_tn(ds, q2)
                dq2 = _dot(ds, kblk)
                dqp[pl.ds(base, CHUNK), :] = jnp.where(lo, dq2[0:CHUNK, :], dq2[CHUNK:2 * CHUNK, :]) * 0.125
                dk_prev[pl.ds(base, CHUNK), :] = dk_blk[0:CHUNK, :]
                dk_own[pl.ds(own, CHUNK), :] = dk_blk[CHUNK:2 * CHUNK, :]
                dv_prev[pl.ds(base, CHUNK), :] = dv_blk[0:CHUNK, :]
                dv_own[pl.ds(own, CHUNK), :] = dv_blk[CHUNK:2 * CHUNK, :]
                return carry

            lax.fori_loop(0, S // CHUNK, block, 0, unroll=ATTN_UNROLL)
            for r in range(d):
                src = slice(r * length, (r + 1) * length)
                pad = slice(CHUNK + r * length, CHUNK + (r + 1) * length)
                if d == 1:
                    dqa[...] = dqp[...]
                    dka[...] = dk_own[pad, :] + dk_prev[pad, :]
                    dva[...] = dv_own[pad, :] + dv_prev[pad, :]
                else:
                    dst = pl.ds(r, length, stride=d)
                    dqa[dst, :] = dqa[dst, :] + dqp[src, :]
                    dka[dst, :] = dka[dst, :] + (dk_own[pad, :] + dk_prev[pad, :])
                    dva[dst, :] = dva[dst, :] + (dv_own[pad, :] + dv_prev[pad, :])
        for t in range(NT):
            rows = slice(t * TM, (t + 1) * TM)
            c, sa, sb = c_ref[rows, :], sa_ref[rows, :], sb_ref[rows, :]
            dq_ref[rows, :] = _rope_t(dqa[rows, :], c, sa, sb).astype(BF16)
            dk_ref[rows, :] = _rope_t(dka[rows, :], c, sa, sb).astype(BF16)
            dv_ref[rows, :] = dva[rows, :].astype(BF16)

    slab = pl.BlockSpec((S, LANES), lambda h: (0, h), pipeline_mode=pl.Buffered(1))
    tab = pl.BlockSpec((S, LANES), lambda h: (0, 0), pipeline_mode=pl.Buffered(1))
    out_slab = pl.BlockSpec((S, LANES), lambda h: (0, h))
    out = jax.ShapeDtypeStruct((S, D_HALF), BF16)
    padded_b = pltpu.VMEM((CHUNK + S, LANES), BF16)
    padded_f = pltpu.VMEM((CHUNK + S, LANES), F32)
    whole = pltpu.VMEM((S, LANES), F32)
    return _host_call(
        body, carried, name="attention_bwd", grid=(4,), out_shape=(out, out, out),
        in_specs=[slab] * 6 + [tab] * 3, out_specs=(out_slab,) * 3,
        scratch_shapes=[padded_b] * 4 + [pltpu.VMEM((2, CHUNK, 2 * CHUNK), F32)] + [whole] * 3
        + [padded_f] * 4 + [whole] * 3,
        params=_params(("arbitrary",), 60), args=(q, k, v, lse, d_attn, delta, *tabs))


def _proj_in_bwd_w(xb, parts):
    def body(x_ref, p0, p1, p2, p3, p4, gw_ref, dw_ref):
        @pl.when(pl.program_id(0) == 0)
        def _():
            dw_ref[...] = jnp.zeros_like(dw_ref)

        xt = x_ref[...]
        for n, part in enumerate((p0, p1, p2, p3, p4)):
            dw_ref[n * D_HALF:(n + 1) * D_HALF, :] += _dot_tn(part[...], xt)

        @pl.when(pl.program_id(0) == NT - 1)
        def _():
            width = D_IN // N_DEV
            for j in range(N_DEV):
                gw_ref[_owner_slot(j)] = dw_ref[width * j:width * (j + 1), :].astype(BF16)

    return pl.pallas_call(
        body, name="proj_in_bwd_w", grid=(NT,), out_shape=jax.ShapeDtypeStruct((N_DEV, D_IN // N_DEV, D), BF16),
        in_specs=[_row_spec(D)] + [_row_spec(D_HALF)] * 5, out_specs=_full_spec((N_DEV, D_IN // N_DEV, D)),
        scratch_shapes=[pltpu.VMEM((D_IN, D), F32)],
        compiler_params=_params(("arbitrary",)),
    )(xb, *parts)


def _proj_in_bwd_x(dy1, parts, w_in, carried=None):
    def body(dy_ref, p0, p1, p2, p3, p4, w_ref, gx_ref):
        acc = ALPHA * dy_ref[...]
        for n, part in enumerate((p0, p1, p2, p3, p4)):
            acc += _dot(part[...], w_ref[n * D_HALF:(n + 1) * D_HALF, :])
        gx_ref[...] = acc

    return _host_call(
        body, carried, name="proj_in_bwd_x", grid=(NT,), out_shape=(jax.ShapeDtypeStruct((S, D), F32),),
        in_specs=[_row_spec(D)] + [_row_spec(D_HALF)] * 5 + [_full_spec((D_IN, D))], out_specs=(_row_spec(D),),
        scratch_shapes=[], params=_params(("arbitrary",)), args=(dy1, *parts, w_in))


def _to_natural(blocks, name):
    n, rows, w = blocks.shape
    tile = min(rows, 256)

    def body(i_ref, o_ref):
        o_ref[...] = jnp.concatenate([i_ref[j] for j in range(n)], axis=1)

    return pl.pallas_call(
        body, name=name, grid=(rows // tile,), out_shape=jax.ShapeDtypeStruct((rows, n * w), blocks.dtype),
        in_specs=[pl.BlockSpec((n, tile, w), lambda i: (0, i, 0))],
        out_specs=pl.BlockSpec((tile, n * w), lambda i: (i, 0)), compiler_params=_params(("parallel",)),
    )(blocks)


def _row_blocks(full, name):
    rows, cols = full.shape
    r = rows // N_DEV

    def body(i_ref, o_ref):
        o_ref[0] = i_ref[...].astype(BF16)

    return pl.pallas_call(
        body, name=name, grid=(N_DEV,), out_shape=jax.ShapeDtypeStruct((N_DEV, r, cols), BF16),
        in_specs=[pl.BlockSpec((r, cols), lambda s: ((s % 4) * 2 + s // 4, 0))],
        out_specs=pl.BlockSpec((1, r, cols), lambda s: (s, 0, 0)), compiler_params=_params(("parallel",)),
    )(full)


def _local_step(x, p, pos_col, target, sm, ex):
    bs_exp = jnp.repeat(sm["b_s"].T, 64, axis=1)
    tabs, got = _rope_tables(pos_col, ex.gather_input())
    w_in = ex.weight_input(got)
    q, k, v, u_pre, z_pre, gm, xb = _proj_in_fwd(x, w_in, tabs, sm["ln_z_g"], sm["ln_z_b"], sm["w_s"], bs_exp)
    (attn, lse), got = _attention_fwd(q, k, v, ex.gather_first())
    wa = ex.weights_first(got)
    xhat1, rstd1, x1b = _mix_ln1_fwd(attn, gm, wa["w_o"], x, sm["ln1_g"], sm["ln1_b"])
    (a_pre, a, b_lin, hff), got = _ffn_up_fwd(x1b, wa["w_ff_a"], wa["w_ff_b"], wa["conv_w8"], sm["conv_b"],
                                              ex.gather_second())
    wc = ex.weights_second(got)
    xhat2, rstd2 = _ffn_down_ln2_fwd(hff, wc["w_ff_down"], xhat1, sm["ln1_g"], sm["ln1_b"])
    loss, dy2, dy2b, dw_g, dw_p, vec_tail = _tail_fwd_bwd(
        xhat2, rstd2, p, target, wc["w_ple_gate"], wc["w_ple_in"], sm["ln2_g"], sm["ln2_b"],
        sm["b_ple_gate"], sm["ln3_g"], sm["ln3_b"])
    dap, dbl, dw_down, dconv = _ffn_bwd_act(dy2b, wc["w_ff_down"], a_pre, a, b_lin, hff, wa["conv_w8"])
    dw_a, dw_b = _ffn_bwd_w(dap, dbl, x1b)
    dx1_ffn = _ffn_bwd_x(dap, dbl, wa["w_ff_a"], wa["w_ff_b"])
    (dy1, d_attn, delta, d_gm, dw_o, vec_ln1), _ = _ln1_mix_bwd(
        dy2, dx1_ffn, xhat1, rstd1, sm["ln1_g"], attn, gm, wa["w_o"])
    early = {"w_ff_a": _row_blocks(dw_a, "ff_a_grad_blocks"), "w_ff_b": _row_blocks(dw_b, "ff_b_grad_blocks"),
             "w_ff_down": _row_blocks(dw_down, "ff_down_grad_blocks"), "w_ple_gate": dw_g, "w_ple_in": dw_p,
             "w_o": dw_o}
    (du, dz, dws, dbs, vec_z), got = _gmlp_bwd(d_gm, u_pre, z_pre, sm["ln_z_g"], sm["ln_z_b"], sm["w_s"], bs_exp,
                                               ex.to_sibling(early))
    chip_sums = ex.reduce_on_chip(got)
    small = {"tail": vec_tail, "ln1": vec_ln1, "ln_z": vec_z, "conv": dconv, "w_s": dws, "b_s": dbs, "loss": loss}
    (dq, dk, dv), got_early = _attention_bwd(q, k, v, lse, d_attn, delta, tabs,
                                             ex.between_chips(chip_sums, small))
    parts = (dq, dk, dv, du, dz)
    (grad_x,), got_late = _proj_in_bwd_x(dy1, parts, w_in, ex.last(_proj_in_bwd_w(xb, parts)))
    return grad_x, ex.collect(got_early, got_late)


def _mesh_pos():
    return lax.axis_index("x"), lax.axis_index("y"), lax.axis_index("c")


def _cast_shards(shards):
    n = len(shards)

    def body(*refs):
        for a in range(n):
            refs[n + a][...] = refs[a][...].astype(BF16)

    whole = [_full_spec(s.shape) for s in shards]
    return pl.pallas_call(
        body, name="cast_shards", grid=(1,), out_shape=tuple(jax.ShapeDtypeStruct(s.shape, BF16) for s in shards),
        in_specs=whole, out_specs=tuple(whole), compiler_params=_params(("arbitrary",)),
    )(*shards)


class _GatherComm:
    def __init__(self, shards):
        n = len(shards)
        self.inputs = list(shards)
        self.out_shapes = [jax.ShapeDtypeStruct((N_DEV,) + s.shape, s.dtype) for s in shards]
        self.scratch = [pltpu.SemaphoreType.DMA((7 * n,)), pltpu.SemaphoreType.DMA((7 * n,)),
                        pltpu.SemaphoreType.DMA((n,))]

    def phases(self, x_refs, out_refs, sems):
        send_sems, recv_sems, local_sems = sems
        n_arr = len(x_refs)

        def where():
            x, y, c = _mesh_pos()
            return (x, y, c), (x, y, 1 - c), [(1 - x, y), (x, 1 - y), (1 - x, 1 - y)]

        def copy(a, n, block, to, from_shard=False):
            dst = out_refs[a].at[4 * block[0] + 2 * block[1] + block[2]]
            return pltpu.make_async_remote_copy(
                src_ref=x_refs[a] if from_shard else dst, dst_ref=dst, send_sem=send_sems.at[7 * a + n],
                recv_sem=recv_sems.at[7 * a + n], device_id=to, device_id_type=MESH)

        def local(a):
            x, y, c = _mesh_pos()
            return pltpu.make_async_copy(x_refs[a], out_refs[a].at[4 * x + 2 * y + c], local_sems.at[a])

        def start():
            me, sibling, chips = where()
            for a in range(n_arr):
                local(a).start()
                copy(a, 0, me, sibling, from_shard=True).start()
                for n, chip in enumerate(chips):
                    copy(a, 1 + n, me, (*chip, me[2]), from_shard=True).start()

        def forward():
            me, sibling, chips = where()
            for n, chip in enumerate(chips):
                for a in range(n_arr):
                    copy(a, 1 + n, (*chip, me[2]), me).wait_recv()
                    copy(a, 4 + n, (*chip, me[2]), sibling).start()

        def finish():
            me, sibling, chips = where()
            for a in range(n_arr):
                copy(a, 0, sibling, me).wait_recv()
                copy(a, 0, me, sibling, from_shard=True).wait_send()
                for n, chip in enumerate(chips):
                    copy(a, 4 + n, (*chip, 1 - me[2]), me).wait_recv()
                    copy(a, 1 + n, me, (*chip, me[2]), from_shard=True).wait_send()
                    copy(a, 4 + n, (*chip, me[2]), sibling).wait_send()
                local(a).wait()

        return {"start": start, "forward": forward, "finish": finish}


class _SiblingComm:
    def __init__(self, big):
        n = len(big)
        self.inputs = list(big)
        self.out_shapes = [jax.ShapeDtypeStruct(b.shape[1:], b.dtype) for b in big]
        self.scratch = [pltpu.SemaphoreType.DMA((n,)), pltpu.SemaphoreType.DMA((n,))]

    def phases(self, src, dst, sems):
        send_sems, recv_sems = sems

        def copies():
            x, y, c = _mesh_pos()
            return [pltpu.make_async_remote_copy(
                src_ref=src[a].at[1 - c], dst_ref=dst[a], send_sem=send_sems.at[a], recv_sem=recv_sems.at[a],
                device_id=(x, y, 1 - c), device_id_type=MESH) for a in range(len(src))]

        def start():
            for cp in copies():
                cp.start()

        def finish():
            for cp in copies():
                cp.wait()

        return {"start": start, "finish": finish}


class _ChipComm:
    def __init__(self, sums):
        n = len(sums)
        self.inputs = list(sums)
        self.out_shapes = [jax.ShapeDtypeStruct(s.shape, s.dtype) for s in sums]
        self.scratch = [pltpu.SemaphoreType.DMA((3 * n,)), pltpu.SemaphoreType.DMA((3 * n,)),
                        pltpu.SemaphoreType.DMA((n,))]

    def phases(self, src, dst, sems):
        send_sems, recv_sems, local_sems = sems

        def copies():
            x, y, c = _mesh_pos()
            my_chip = 2 * x + y
            out = [pltpu.make_async_copy(src[a].at[my_chip], dst[a].at[my_chip], local_sems.at[a])
                   for a in range(len(src))]
            for n, (px, py) in enumerate([(1 - x, y), (x, 1 - y), (1 - x, 1 - y)]):
                for a in range(len(src)):
                    out.append(pltpu.make_async_remote_copy(
                        src_ref=src[a].at[2 * px + py], dst_ref=dst[a].at[my_chip],
                        send_sem=send_sems.at[3 * a + n], recv_sem=recv_sems.at[3 * a + n],
                        device_id=(px, py, c), device_id_type=MESH))
            return out

        def start():
            for cp in copies():
                cp.start()

        def finish():
            for cp in copies():
                cp.wait()

        return {"start": start, "finish": finish}


class _ScatterComm:
    def __init__(self, blocks, small):
        self.n_big, self.n_small = len(blocks), len(small)
        n = self.n_big + self.n_small
        self.inputs = list(blocks) + list(small)
        self.out_shapes = ([jax.ShapeDtypeStruct(b.shape, b.dtype) for b in blocks]
                           + [jax.ShapeDtypeStruct((N_DEV,) + s.shape, s.dtype) for s in small])
        self.scratch = [pltpu.SemaphoreType.DMA((7 * n,)), pltpu.SemaphoreType.DMA((7 * n,)),
                        pltpu.SemaphoreType.DMA((n,))]

    def phases(self, src, dst, sems):
        send_sems, recv_sems, local_sems = sems
        n_big, n_all = self.n_big, self.n_big + self.n_small

        def source(a, core, chip):
            return src[a].at[core * 4 + chip] if a < n_big else src[a]

        def copies():
            x, y, c = _mesh_pos()
            me = 4 * x + 2 * y + c
            out = [pltpu.make_async_copy(source(a, c, 2 * x + y), dst[a].at[me], local_sems.at[a])
                   for a in range(n_all)]
            for flip in range(1, N_DEV):
                px = 1 - x if flip & 4 else x
                py = 1 - y if flip & 2 else y
                pc = 1 - c if flip & 1 else c
                for a in range(n_all):
                    n = 7 * a + flip - 1
                    out.append(pltpu.make_async_remote_copy(
                        src_ref=source(a, pc, 2 * px + py), dst_ref=dst[a].at[me], send_sem=send_sems.at[n],
                        recv_sem=recv_sems.at[n], device_id=(px, py, pc), device_id_type=MESH))
            return out

        def start():
            for cp in copies():
                cp.start()

        def finish():
            for cp in copies():
                cp.wait()

        return {"start": start, "finish": finish}


class _Both:
    def __init__(self, first, second):
        self.parts = (first, second)
        self.inputs = first.inputs + second.inputs
        self.out_shapes = first.out_shapes + second.out_shapes
        self.scratch = first.scratch + second.scratch

    def phases(self, src, dst, sems):
        a, b = self.parts
        pa = a.phases(src[:len(a.inputs)], dst[:len(a.out_shapes)], sems[:len(a.scratch)])
        pb = b.phases(src[len(a.inputs):], dst[len(a.out_shapes):], sems[len(a.scratch):])

        def both(name):
            def run():
                pa[name]()
                pb[name]()
            return run

        return {name: both(name) for name in pa}


def _host_call(body, carried, *, name, grid, out_shape, in_specs, out_specs, scratch_shapes, params, args):
    if carried is None:
        return pl.pallas_call(body, name=name, grid=grid, out_shape=tuple(out_shape), in_specs=list(in_specs),
                              out_specs=tuple(out_specs), scratch_shapes=list(scratch_shapes),
                              compiler_params=params)(*args), ()
    comm, when = carried
    n_in, n_out, n_scratch = len(in_specs), len(out_shape), len(scratch_shapes)
    k_in, k_out = len(comm.inputs), len(comm.out_shapes)

    def wrapped(*refs):
        bounds = np.cumsum([0, n_in, k_in, n_out, k_out, n_scratch])
        ins, c_in, outs, c_out, scr = (refs[bounds[i]:bounds[i + 1]] for i in range(5))
        phases = comm.phases(c_in, c_out, refs[bounds[5]:])
        for phase, cond in when("before"):
            pl.when(cond)(phases[phase])
        body(*ins, *outs, *scr)
        for phase, cond in when("after"):
            pl.when(cond)(phases[phase])

    anywhere = pl.BlockSpec(memory_space=pl.ANY)
    results = pl.pallas_call(
        wrapped, name=name, grid=grid, out_shape=tuple(out_shape) + tuple(comm.out_shapes),
        in_specs=list(in_specs) + [anywhere] * k_in, out_specs=tuple(out_specs) + (anywhere,) * k_out,
        scratch_shapes=list(scratch_shapes) + comm.scratch, compiler_params=params,
    )(*args, *comm.inputs)
    return results[:n_out], results[n_out:]


class _Exchanges:
    FIRST = ("w_o", "w_ff_a", "w_ff_b")
    SECOND = ("w_ff_down", "w_ple_gate", "w_ple_in")
    EARLY = ("w_ff_a", "w_ff_b", "w_ff_down", "w_ple_gate", "w_ple_in", "w_o")
    LATE = ("w_in",)

    def __init__(self, shards, conv_rows):
        self.shards, self.conv_rows = shards, conv_rows
        self.mode = {name: mode for name, _, mode in BIG}

    def _natural(self, name, blocks):
        n, r, c = blocks.shape
        return _to_natural(blocks, name + "_natural") if self.mode[name] == "cols" else blocks.reshape(n * r, c)

    def gather_input(self):
        def when(position):
            step = pl.program_id(0)
            if position == "before":
                return [("start", step == 0)]
            return [("forward", step == NT - 1), ("finish", step == NT - 1)]
        return _GatherComm([self.shards["w_in"]]), when

    def weight_input(self, got):
        return self._natural("w_in", got[0])

    def gather_first(self):
        comm = _GatherComm([self.shards[n] for n in self.FIRST] + [self.conv_rows])

        def when(position):
            step = pl.program_id(0)
            if position == "before":
                return [("start", step == 0), ("forward", step == 3)]
            return [("finish", step == 3)]
        return comm, when

    def weights_first(self, got):
        out = {name: self._natural(name, blocks) for name, blocks in zip(self.FIRST, got)}
        out["conv_w8"] = _to_natural(got[-1], "conv_w_natural")
        return out

    def gather_second(self):
        comm = _GatherComm([self.shards[n] for n in self.SECOND])

        def when(position):
            j, i = pl.program_id(0), pl.program_id(1)
            if position == "before":
                return [("start", (j == 0) & (i == 0)), ("forward", (j == NJ - 1) & (i == NT // 2))]
            return [("finish", (j == NJ - 1) & (i == NT - 1))]
        return comm, when

    def weights_second(self, got):
        return {name: self._natural(name, blocks) for name, blocks in zip(self.SECOND, got)}

    def to_sibling(self, early):
        self.by_core = [early[n].reshape((2, 4) + early[n].shape[1:]) for n in self.EARLY]
        return _SiblingComm(self.by_core), _first_and_last(NT)

    def reduce_on_chip(self, from_sibling):
        core = lax.axis_index("c").astype(jnp.int32).reshape(1)
        return _chip_reduce(self.by_core, from_sibling, core, "chip_reduce")

    def between_chips(self, chip_sums, small):
        self.small_keys = tuple(small)
        return _Both(_ChipComm(chip_sums), _ScatterComm([], [small[k] for k in self.small_keys])), _first_and_last(4)

    def last(self, dw_in):
        by_core = [dw_in.reshape((2, 4) + dw_in.shape[1:])]
        core = lax.axis_index("c").astype(jnp.int32).reshape(1)
        sums = _chip_reduce(by_core, _standalone(_SiblingComm(by_core), "w_in_grad_to_sibling"), core, "w_in_chip_reduce")
        return _ChipComm(sums), _first_and_last(NT)

    def collect(self, got_early, got_late):
        parts = dict(zip(self.EARLY, got_early[:len(self.EARLY)]))
        parts.update(zip(self.LATE, got_late))
        return parts, dict(zip(self.small_keys, got_early[len(self.EARLY):]))


def _first_and_last(n_steps):
    def when(position):
        step = pl.program_id(0)
        return [("start", step == 0)] if position == "before" else [("finish", step == n_steps - 1)]
    return when


def _standalone(comm, name):
    n_in = len(comm.inputs)

    def body(*refs):
        phases = comm.phases(refs[:n_in], refs[n_in:n_in + len(comm.out_shapes)], refs[n_in + len(comm.out_shapes):])
        phases["start"]()
        phases["finish"]()

    anywhere = pl.BlockSpec(memory_space=pl.ANY)
    return pl.pallas_call(
        body, name=name, out_shape=tuple(comm.out_shapes), in_specs=[anywhere] * n_in,
        out_specs=(anywhere,) * len(comm.out_shapes), scratch_shapes=comm.scratch,
    )(*comm.inputs)


def _chip_reduce(big, from_sibling, core, name):
    n = len(big)

    def body(core_ref, *refs):
        for a in range(n):
            mine, theirs, out = refs[a], refs[n + a], refs[2 * n + a]
            out[0] = (mine[0, 0].astype(F32) + theirs[0].astype(F32)).astype(BF16)

    def block(shape):
        return pl.BlockSpec((1,) + shape, lambda ch, core_ref: (ch, 0, 0))

    grid_spec = pltpu.PrefetchScalarGridSpec(
        num_scalar_prefetch=1, grid=(4,),
        in_specs=[pl.BlockSpec((1, 1) + b.shape[2:], lambda ch, core_ref: (core_ref[0], ch, 0, 0)) for b in big]
        + [block(b.shape[2:]) for b in big],
        out_specs=[block(b.shape[2:]) for b in big])
    return pl.pallas_call(
        body, name=name, grid_spec=grid_spec,
        out_shape=tuple(jax.ShapeDtypeStruct(b.shape[1:], BF16) for b in big),
        compiler_params=_params(("parallel",)),
    )(core, *big, *from_sibling)


def _adamw(g, w, m, v):
    nm = ADAM_B1 * m + (1.0 - ADAM_B1) * g
    nv = ADAM_B2 * v + (1.0 - ADAM_B2) * (g * g)
    m_hat = nm / (1.0 - ADAM_B1 ** ADAM_STEP)
    v_hat = nv / (1.0 - ADAM_B2 ** ADAM_STEP)
    return -ADAM_LR * (m_hat / (jnp.sqrt(v_hat) + ADAM_EPS) + ADAM_WD * w), nm, nv


def _adamw_sharded(parts, w, m, v, name):
    def body(p_ref, w_ref, m_ref, v_ref, g_ref, d_ref, nm_ref, nv_ref):
        g = p_ref[0].astype(F32)
        for s in range(1, parts.shape[0]):
            g = g + p_ref[s].astype(F32)
        delta, nm, nv = _adamw(g, w_ref[0], m_ref[0], v_ref[0])
        g_ref[0] = g
        d_ref[0] = delta
        nm_ref[0] = nm
        nv_ref[0] = nv

    n, r, c = parts.shape
    steps = 4 if r % 64 == 0 and r >= 512 else (2 if r % 32 == 0 and r >= 256 else 1)
    tile = pl.BlockSpec((1, r // steps, c), lambda i: (0, i, 0))
    return pl.pallas_call(
        body, name=name, grid=(steps,), out_shape=(jax.ShapeDtypeStruct(w.shape, F32),) * 4,
        in_specs=[pl.BlockSpec((n, r // steps, c), lambda i: (0, i, 0)), tile, tile, tile], out_specs=(tile,) * 4,
        compiler_params=_params(("parallel",)),
    )(parts, w, m, v)


REPLICATED = (("ln_z_g", "ln_z", 0), ("ln_z_b", "ln_z", 1), ("w_s", "w_s", None), ("b_s", "b_s", None),
              ("ln1_g", "ln1", 0), ("ln1_b", "ln1", 1), ("conv_w", "conv_mine", None), ("conv_b", "conv", 3),
              ("ln2_g", "tail", 3), ("ln2_b", "tail", 4), ("b_ple_gate", "tail", 0), ("ln3_g", "tail", 1),
              ("ln3_b", "tail", 2))
GATHERED = ("tail", "ln1", "ln_z", "conv", "w_s", "b_s", "loss", "conv_mine")


def _adamw_replicated(gathered, w, m, v):
    n_par = len(REPLICATED)

    def body(*refs):
        srcs = dict(zip(GATHERED, refs[:len(GATHERED)]))
        rest = refs[len(GATHERED):]
        w_refs, m_refs, v_refs = rest[:n_par], rest[n_par:2 * n_par], rest[2 * n_par:3 * n_par]
        outs = rest[3 * n_par:]
        loss_ref = outs[4 * n_par]
        sums = {}
        for key, ref in srcs.items():
            total = ref[0]
            for dev in range(1, N_DEV):
                total = total + ref[dev]
            sums[key] = total
        loss_ref[...] = sums["loss"]
        for n, (name, key, row) in enumerate(REPLICATED):
            if name == "conv_w":
                g = sums[key][0:3, :]
            elif row is None:
                g = sums[key]
            else:
                g = sums[key][row:row + 1, :]
            lead = len(w_refs[n].shape) - g.ndim
            idx = (0,) * lead + (Ellipsis,)
            delta, nm, nv = _adamw(g, w_refs[n][idx], m_refs[n][idx], v_refs[n][idx])
            for kind, val in enumerate((g, delta, nm, nv)):
                outs[kind * n_par + n][idx] = val

    names = [name for name, _, _ in REPLICATED]
    shapes = [jax.ShapeDtypeStruct(w[name].shape, F32) for name in names]
    args = [gathered[k] for k in GATHERED] + [w[n] for n in names] + [m[n] for n in names] + [v[n] for n in names]
    out_shape = tuple(shapes * 4) + (jax.ShapeDtypeStruct((8, LANES), F32),)
    return pl.pallas_call(
        body, name="adamw_replicated", grid=(1,), out_shape=out_shape,
        in_specs=[_full_spec(a.shape) for a in args], out_specs=tuple(_full_spec(s.shape) for s in out_shape),
        compiler_params=_params(("arbitrary",)),
    )(*args)


def kernel(x, p, positions, w_in, ln_z_g, ln_z_b, w_s, b_s, w_o, ln1_g, ln1_b, w_ff_a, w_ff_b, conv_w, conv_b, w_ff_down, ln2_g, ln2_b, w_ple_gate, b_ple_gate, w_ple_in, ln3_g, ln3_b, loss_target, m_w_in, m_ln_z_g, m_ln_z_b, m_w_s, m_b_s, m_w_o, m_ln1_g, m_ln1_b, m_w_ff_a, m_w_ff_b, m_conv_w, m_conv_b, m_w_ff_down, m_ln2_g, m_ln2_b, m_w_ple_gate, m_b_ple_gate, m_w_ple_in, m_ln3_g, m_ln3_b, v_w_in, v_ln_z_g, v_ln_z_b, v_w_s, v_b_s, v_w_o, v_ln1_g, v_ln1_b, v_w_ff_a, v_w_ff_b, v_conv_w, v_conv_b, v_w_ff_down, v_ln2_g, v_ln2_b, v_w_ple_gate, v_b_ple_gate, v_w_ple_in, v_ln3_g, v_ln3_b):
    w = dict(w_in=w_in, ln_z_g=ln_z_g, ln_z_b=ln_z_b, w_s=w_s, b_s=b_s, w_o=w_o, ln1_g=ln1_g, ln1_b=ln1_b,
             w_ff_a=w_ff_a, w_ff_b=w_ff_b, conv_w=conv_w, conv_b=conv_b, w_ff_down=w_ff_down, ln2_g=ln2_g,
             ln2_b=ln2_b, w_ple_gate=w_ple_gate, b_ple_gate=b_ple_gate, w_ple_in=w_ple_in, ln3_g=ln3_g,
             ln3_b=ln3_b)
    m = dict(w_in=m_w_in, ln_z_g=m_ln_z_g, ln_z_b=m_ln_z_b, w_s=m_w_s, b_s=m_b_s, w_o=m_w_o, ln1_g=m_ln1_g,
             ln1_b=m_ln1_b, w_ff_a=m_w_ff_a, w_ff_b=m_w_ff_b, conv_w=m_conv_w, conv_b=m_conv_b,
             w_ff_down=m_w_ff_down, ln2_g=m_ln2_g, ln2_b=m_ln2_b, w_ple_gate=m_w_ple_gate,
             b_ple_gate=m_b_ple_gate, w_ple_in=m_w_ple_in, ln3_g=m_ln3_g, ln3_b=m_ln3_b)
    v = dict(w_in=v_w_in, ln_z_g=v_ln_z_g, ln_z_b=v_ln_z_b, w_s=v_w_s, b_s=v_b_s, w_o=v_w_o, ln1_g=v_ln1_g,
             ln1_b=v_ln1_b, w_ff_a=v_w_ff_a, w_ff_b=v_w_ff_b, conv_w=v_conv_w, conv_b=v_conv_b,
             w_ff_down=v_w_ff_down, ln2_g=v_ln2_g, ln2_b=v_ln2_b, w_ple_gate=v_w_ple_gate,
             b_ple_gate=v_b_ple_gate, w_ple_in=v_w_ple_in, ln3_g=v_ln3_g, ln3_b=v_ln3_b)
    big_names = [name for name, _, _ in BIG]
    small_names = ("ln_z_g", "ln_z_b", "w_s", "b_s", "ln1_g", "ln1_b", "conv_b", "ln2_g", "ln2_b", "b_ple_gate",
                   "ln3_g", "ln3_b")

    transposed = {name for name, _, mode in BIG if mode == "rows_t"}

    def travel(a, name):
        return jnp.swapaxes(a, 1, 2) if name in transposed else a

    shards = dict(zip(big_names, _cast_shards([travel(w[n], n)[0] for n in big_names])))
    conv_rows = jnp.pad(w["conv_w"][0], ((0, 5), (0, 0)))
    sm = {n: w[n][0] if w[n].ndim > 2 else w[n] for n in small_names}
    pos_col = positions.reshape(S, 1).astype(F32)
    grad_x, (parts, small_all) = _local_step(x[0], p[0, 0], pos_col, loss_target[0], sm,
                                             _Exchanges(shards, conv_rows))
    me = 4 * lax.axis_index("x") + 2 * lax.axis_index("y") + lax.axis_index("c")
    conv_cols = small_all["conv"].reshape(N_DEV, 8, N_DEV, D_FF // N_DEV)
    small_all["conv_mine"] = lax.dynamic_index_in_dim(conv_cols, me, axis=2, keepdims=False)

    leaves = {}
    for name in big_names:
        outs = _adamw_sharded(parts[name], travel(w[name], name), travel(m[name], name), travel(v[name], name),
                              "adamw_" + name)
        leaves[name] = tuple(travel(o, name) for o in outs)
    rep = _adamw_replicated(small_all, w, m, v)
    n_rep = len(REPLICATED)
    for n, (name, _, _) in enumerate(REPLICATED):
        leaves[name] = tuple(rep[kind * n_rep + n] for kind in range(4))
    loss = rep[4 * n_rep][0, 0]
    return (loss, grad_x[None], *[leaves[n][kind] for kind in range(4) for n in WEIGHT_ORDER])
```

```python
import math

import numpy as np
import jax
import jax.numpy as jnp
from jax import lax
from jax.experimental import pallas as pl
from jax.experimental.pallas import tpu as pltpu

F32 = jnp.float32
BF16 = jnp.bfloat16
MESH = pl.DeviceIdType.MESH

N_DEV = 8
S = 4096
D = 1024
D_HALF = 512
D_IN = 2560
D_FF = 2816
D_PLE = 256
CHUNK = 128
DILATIONS = ((1, 32), (4, 8), (16, 2))
ROPE_THETA = 500000.0
LN_EPS = 1e-5
ALPHA = 2.0 ** 0.25
NEG_INF = -1e30
INV_SQRT2 = 1.0 / math.sqrt(2.0)
INV_SQRT_2PI = 1.0 / math.sqrt(2.0 * math.pi)

ADAM_LR, ADAM_B1, ADAM_B2, ADAM_EPS, ADAM_WD, ADAM_STEP = 0.001, 0.9, 0.999, 1e-08, 0.01, 10

TM = 512
NT = S // TM
ATTN_UNROLL = 8
TN = 1408
NJ = D_FF // TN
LANES = 128
VMEM_MIB = 1024 * 1024

BIG = (("w_in", (320, 1024), "rows_t"), ("w_o", (128, 1024), "rows"), ("w_ff_a", (352, 1024), "rows_t"),
       ("w_ff_b", (352, 1024), "rows_t"), ("w_ff_down", (352, 1024), "rows"), ("w_ple_gate", (128, 1024), "rows"),
       ("w_ple_in", (256, 128), "cols"))
WEIGHT_ORDER = ("w_in", "ln_z_g", "ln_z_b", "w_s", "b_s", "w_o", "ln1_g", "ln1_b", "w_ff_a", "w_ff_b",
                "conv_w", "conv_b", "w_ff_down", "ln2_g", "ln2_b", "w_ple_gate", "b_ple_gate",
                "w_ple_in", "ln3_g", "ln3_b")


def _params(semantics=None, vmem_mib=48):
    return pltpu.CompilerParams(dimension_semantics=semantics, vmem_limit_bytes=vmem_mib * VMEM_MIB)


def _dot(a, b):
    return jnp.dot(a, b, preferred_element_type=F32)


def _dot_nt(a, b):
    return lax.dot_general(a, b, (((1,), (1,)), ((), ())), preferred_element_type=F32)


def _dot_tn(a, b):
    return lax.dot_general(a, b, (((0,), (0,)), ((), ())), preferred_element_type=F32)


def _gelu(x):
    return 0.5 * x * (1.0 + lax.erf(x * INV_SQRT2))


def _gelu_and_grad(x):
    cdf = 0.5 * (1.0 + lax.erf(x * INV_SQRT2))
    return x * cdf, cdf + x * (jnp.exp(-0.5 * x * x) * INV_SQRT_2PI)


def _ln_stats(y):
    mu = jnp.mean(y, axis=-1, keepdims=True)
    yc = y - mu
    var = jnp.mean(yc * yc, axis=-1, keepdims=True)
    rstd = lax.rsqrt(var + LN_EPS)
    return yc * rstd, rstd


def _ln_bwd(dxhat, xhat, rstd):
    m1 = jnp.mean(dxhat, axis=-1, keepdims=True)
    m2 = jnp.mean(dxhat * xhat, axis=-1, keepdims=True)
    return rstd * (dxhat - m1 - xhat * m2)


def _colsum(x):
    return jnp.sum(x, axis=0, keepdims=True)


def _rows(i):
    return (i, 0)


def _fixed(*_):
    return (0, 0)


def _row_spec(width):
    return pl.BlockSpec((TM, width), _rows)


def _full_spec(shape):
    return pl.BlockSpec(shape, lambda *_: (0,) * len(shape))


def _owner_slot(j):
    return (j % 2) * 4 + j // 2


def _lane_lo():
    return lax.broadcasted_iota(jnp.int32, (CHUNK, LANES), 1) < 64


def _tril():
    r = lax.broadcasted_iota(jnp.int32, (CHUNK, CHUNK), 0)
    c = lax.broadcasted_iota(jnp.int32, (CHUNK, CHUNK), 1)
    return c <= r


def _rope_consts():
    lane = np.arange(LANES) % 64
    j = lane % 8
    inv = np.where(lane < 16, np.float32(ROPE_THETA) ** (-(2.0 * j).astype(np.float32) / np.float32(16.0)), 0.0)
    m_lo = (lane < 8).astype(np.float32)
    m_hi = ((lane >= 8) & (lane < 16)).astype(np.float32)
    return (jnp.asarray(inv, F32).reshape(1, LANES), jnp.asarray(m_lo).reshape(1, LANES),
            jnp.asarray(m_hi).reshape(1, LANES))


def _rope_tables(pos_col, carried=None):
    inv, m_lo, m_hi = _rope_consts()

    def body(pos_ref, inv_ref, lo_ref, hi_ref, c_ref, sa_ref, sb_ref):
        ang = pos_ref[...] * inv_ref[...]
        c = jnp.cos(ang)
        s = jnp.sin(ang)
        lo = lo_ref[...]
        hi = hi_ref[...]
        c_ref[...] = jnp.where(lo + hi > 0.0, c, 1.0)
        sa_ref[...] = s * hi
        sb_ref[...] = -s * lo

    vec = _full_spec((1, LANES))
    out = jax.ShapeDtypeStruct((S, LANES), F32)
    return _host_call(
        body, carried, name="rope_tables", grid=(NT,), out_shape=(out, out, out),
        in_specs=[pl.BlockSpec((TM, 1), _rows), vec, vec, vec],
        out_specs=(_row_spec(LANES),) * 3, scratch_shapes=[], params=_params(("arbitrary",)),
        args=(pos_col, inv, m_lo, m_hi))


def _rope(t, c, sa, sb):
    return t * c + pltpu.roll(t, 8, 1) * sa + pltpu.roll(t, LANES - 8, 1) * sb


def _rope_t(dy, c, sa, sb):
    return dy * c + pltpu.roll(dy * sa, LANES - 8, 1) + pltpu.roll(dy * sb, 8, 1)


def _masked_ws(ws_ref):
    tril = _tril()
    return [jnp.where(tril, ws_ref[g], 0.0).astype(BF16) for g in range(8)]


def _spatial_mix(zn, wm, bs):
    lo = _lane_lo()
    rows = []
    for ch in range(TM // CHUNK):
        slabs = []
        for pr in range(4):
            zp = zn[ch * CHUNK:(ch + 1) * CHUNK, pr * LANES:(pr + 1) * LANES].astype(BF16)
            slabs.append(jnp.where(lo, _dot(wm[2 * pr], zp), _dot(wm[2 * pr + 1], zp)))
        rows.append(jnp.concatenate(slabs, axis=1) + bs)
    return jnp.concatenate(rows, axis=0)


def _proj_in_fwd(x, w_in, tabs, ln_z_g, ln_z_b, w_s, bs_exp):
    def body(x_ref, w_ref, c_ref, sa_ref, sb_ref, g_ref, b_ref, ws_ref, bs_ref,
             q_ref, k_ref, v_ref, u_ref, z_ref, gm_ref, xb_ref):
        xb = x_ref[...].astype(BF16)
        xb_ref[...] = xb
        c, sa, sb = c_ref[...], sa_ref[...], sb_ref[...]
        hq = _dot_nt(xb, w_ref[0:512, :])
        hk = _dot_nt(xb, w_ref[512:1024, :])
        for s in range(4):
            sl = slice(s * LANES, (s + 1) * LANES)
            q_ref[:, sl] = _rope(hq[:, sl], c, sa, sb)
            k_ref[:, sl] = _rope(hk[:, sl], c, sa, sb)
        v_ref[...] = _dot_nt(xb, w_ref[1024:1536, :])
        u_pre = _dot_nt(xb, w_ref[1536:2048, :])
        z_pre = _dot_nt(xb, w_ref[2048:2560, :])
        u_ref[...] = u_pre
        z_ref[...] = z_pre
        zhat, _ = _ln_stats(_gelu(z_pre))
        zn = zhat * g_ref[...] + b_ref[...]
        mixed = _spatial_mix(zn, _masked_ws(ws_ref), bs_ref[...])
        gm_ref[...] = (_gelu(u_pre) * mixed).astype(BF16)

    half = jax.ShapeDtypeStruct((S, D_HALF), F32)
    tab = _row_spec(LANES)
    return pl.pallas_call(
        body, name="proj_in_fwd", grid=(NT,),
        out_shape=(half, half, half, half, half, jax.ShapeDtypeStruct((S, D_HALF), BF16),
                   jax.ShapeDtypeStruct((S, D), BF16)),
        in_specs=[_row_spec(D), _full_spec((D_IN, D)), tab, tab, tab, _full_spec((1, D_HALF)),
                  _full_spec((1, D_HALF)), _full_spec((8, CHUNK, CHUNK)), _full_spec((CHUNK, D_HALF))],
        out_specs=(_row_spec(D_HALF),) * 6 + (_row_spec(D),),
        compiler_params=_params(("parallel",)),
    )(x, w_in, *tabs, ln_z_g, ln_z_b, w_s, bs_exp)


def _store_band_bias(bias_ref):
    qi = lax.broadcasted_iota(jnp.int32, (CHUNK, 2 * CHUNK), 0)
    kj = lax.broadcasted_iota(jnp.int32, (CHUNK, 2 * CHUNK), 1)
    band = (kj >= qi) & (kj <= qi + CHUNK)
    bias_ref[0] = jnp.where(band, 0.0, NEG_INF)
    bias_ref[1] = jnp.where(band & (kj >= CHUNK), 0.0, NEG_INF)


def _permuted_rows(ref, d, r):
    return ref[...] if d == 1 else ref[pl.ds(r, S // d, stride=d), :]


def _attention_fwd(q, k, v, carried=None):
    def body(q_ref, k_ref, v_ref, o_ref, lse_ref, qb, kb, v0b, v1b, bias, op, lp, ob0, lb0, ob1, lb1, ob2, lb2):
        lo = _lane_lo()
        lo_f = lo.astype(F32)[0:1, :]
        hi_f = 1.0 - lo_f
        zero_pad = jnp.zeros((CHUNK, LANES), BF16)
        for buf in (qb, kb, v0b, v1b):
            buf[0:CHUNK, :] = zero_pad
        _store_band_bias(bias)
        outs = ((ob0, lb0), (ob1, lb1), (ob2, lb2))
        for (d, nb), (ob, lb) in zip(DILATIONS, outs):
            length = S // d
            for r in range(d):
                dst = slice(CHUNK + r * length, CHUNK + (r + 1) * length)
                qb[dst, :] = (_permuted_rows(q_ref, d, r) * 0.125).astype(BF16)
                kb[dst, :] = _permuted_rows(k_ref, d, r).astype(BF16)
                vs = _permuted_rows(v_ref, d, r)
                v0b[dst, :] = (vs * lo_f + hi_f).astype(BF16)
                v1b[dst, :] = (vs * hi_f + lo_f).astype(BF16)

            def block(b, carry, nb=nb):
                base = pl.multiple_of(b * CHUNK, CHUNK)
                add = bias[jnp.where(b % nb == 0, 1, 0)]
                qblk = qb[pl.ds(pl.multiple_of(base + CHUNK, CHUNK), CHUNK), :]
                kblk = kb[pl.ds(base, 2 * CHUNK), :]
                q2 = jnp.concatenate([jnp.where(lo, qblk, 0), jnp.where(lo, 0, qblk)], axis=0)
                s2 = _dot_nt(q2, kblk) + jnp.concatenate([add, add], axis=0)
                m2 = jnp.max(s2, axis=-1, keepdims=True)
                p2 = jnp.exp(s2 - m2).astype(BF16)
                pv, mx = [], []
                for head, vh in enumerate((v0b, v1b)):
                    rows = slice(head * CHUNK, (head + 1) * CHUNK)
                    pv.append(_dot(p2[rows, :], vh[pl.ds(base, 2 * CHUNK), :]))
                    mx.append(m2[rows, :])
                den = pltpu.roll(jnp.where(lo, pv[1], pv[0]), 64, 1)
                op[pl.ds(base, CHUNK), :] = jnp.where(lo, pv[0], pv[1]) / den
                lp[pl.ds(base, CHUNK), :] = jnp.where(lo, mx[0], mx[1]) + jnp.log(den)
                return carry

            lax.fori_loop(0, S // CHUNK, block, 0, unroll=ATTN_UNROLL)
            for r in range(d):
                src = slice(r * length, (r + 1) * length)
                if d == 1:
                    ob[...] = op[...]
                    lb[...] = lp[...]
                else:
                    ob[pl.ds(r, length, stride=d), :] = op[src, :]
                    lb[pl.ds(r, length, stride=d), :] = lp[src, :]
        for t in range(NT):
            rows = slice(t * TM, (t + 1) * TM)
            l0, l1, l2 = lb0[rows, :], lb1[rows, :], lb2[rows, :]
            mx = jnp.maximum(jnp.maximum(l0, l1), l2)
            e0, e1, e2 = jnp.exp(l0 - mx), jnp.exp(l1 - mx), jnp.exp(l2 - mx)
            den = e0 + e1 + e2
            o_ref[rows, :] = (e0 * ob0[rows, :] + e1 * ob1[rows, :] + e2 * ob2[rows, :]) / den
            lse_ref[rows, :] = mx + jnp.log(den)

    slab = pl.BlockSpec((S, LANES), lambda h: (0, h))
    out = jax.ShapeDtypeStruct((S, D_HALF), F32)
    padded = pltpu.VMEM((CHUNK + S, LANES), BF16)
    whole = pltpu.VMEM((S, LANES), F32)
    return _host_call(
        body, carried, name="attention_fwd", grid=(4,), out_shape=(out, out),
        in_specs=[slab, slab, slab], out_specs=(slab, slab),
        scratch_shapes=[padded] * 4 + [pltpu.VMEM((2, CHUNK, 2 * CHUNK), F32)] + [whole] * 8,
        params=_params(("arbitrary",), 56), args=(q, k, v))


def _mix_ln1_fwd(attn, gm, w_o, x, g1, b1):
    def body(a_ref, gm_ref, w_ref, x_ref, g_ref, b_ref, xhat_ref, rstd_ref, x1b_ref):
        mix = _dot(a_ref[...].astype(BF16), w_ref[0:D_HALF, :]) + _dot(gm_ref[...], w_ref[D_HALF:D, :])
        xhat, rstd = _ln_stats(ALPHA * x_ref[...] + mix)
        xhat_ref[...] = xhat
        rstd_ref[...] = rstd
        x1b_ref[...] = (xhat * g_ref[...] + b_ref[...]).astype(BF16)

    vec = _full_spec((1, D))
    return pl.pallas_call(
        body, name="mix_ln1_fwd", grid=(NT,),
        out_shape=(jax.ShapeDtypeStruct((S, D), F32), jax.ShapeDtypeStruct((S, 1), F32),
                   jax.ShapeDtypeStruct((S, D), BF16)),
        in_specs=[_row_spec(D_HALF), _row_spec(D_HALF), _full_spec((D, D)), _row_spec(D), vec, vec],
        out_specs=(_row_spec(D), pl.BlockSpec((TM, 1), _rows), _row_spec(D)),
        compiler_params=_params(("parallel",)),
    )(attn, gm, w_o, x, g1, b1)


def _ffn_up_fwd(x1b, w_a, w_b, conv_w8, conv_b, carried=None):
    def body(x_ref, wa_ref, wb_ref, cw_ref, cb_ref, ap_ref, a_ref, bl_ref, h_ref, carry):
        @pl.when(pl.program_id(1) == 0)
        def _():
            carry[...] = jnp.zeros_like(carry)

        xb = x_ref[...]
        ap = _dot_nt(xb, wa_ref[...])
        bl = _dot_nt(xb, wb_ref[...])
        row = lax.broadcasted_iota(jnp.int32, (TM, TN), 0)
        c6, c7 = carry[6:7, :], carry[7:8, :]
        m1 = jnp.where(row == 0, c7, pltpu.roll(ap, 1, 0))
        m2 = jnp.where(row == 0, c6, jnp.where(row == 1, c7, pltpu.roll(ap, 2, 0)))
        a = cb_ref[...] + cw_ref[0:1, :] * m2 + cw_ref[1:2, :] * m1 + cw_ref[2:3, :] * ap
        carry[...] = ap[TM - 8:TM, :]
        ap_ref[...] = ap.astype(BF16)
        a_ref[...] = a
        bl_ref[...] = bl.astype(BF16)
        h_ref[...] = (_gelu(a) * bl).astype(BF16)

    tile = pl.BlockSpec((TM, TN), lambda j, i: (i, j))
    wcol = pl.BlockSpec((TN, D), lambda j, i: (j, 0))
    ff = jax.ShapeDtypeStruct((S, D_FF), F32)
    ffb = jax.ShapeDtypeStruct((S, D_FF), BF16)
    return _host_call(
        body, carried, name="ffn_up_fwd", grid=(NJ, NT),
        out_shape=(ffb, ff, ffb, ffb),
        in_specs=[pl.BlockSpec((TM, D), lambda j, i: (i, 0)), wcol, wcol,
                  pl.BlockSpec((8, TN), lambda j, i: (0, j)), pl.BlockSpec((1, TN), lambda j, i: (0, j))],
        out_specs=(tile, tile, tile, tile),
        scratch_shapes=[pltpu.VMEM((8, TN), F32)],
        params=_params(("arbitrary", "arbitrary"), 56), args=(x1b, w_a, w_b, conv_w8, conv_b))


def _ffn_down_ln2_fwd(hff, w_down, xhat1, g1, b1):
    def body(h_ref, w_ref, xh_ref, g_ref, b_ref, xhat_ref, rstd_ref):
        x1 = xh_ref[...] * g_ref[...] + b_ref[...]
        xhat, rstd = _ln_stats(ALPHA * x1 + _dot(h_ref[...], w_ref[...]))
        xhat_ref[...] = xhat
        rstd_ref[...] = rstd

    vec = _full_spec((1, D))
    return pl.pallas_call(
        body, name="ffn_down_ln2_fwd", grid=(NT,),
        out_shape=(jax.ShapeDtypeStruct((S, D), F32), jax.ShapeDtypeStruct((S, 1), F32)),
        in_specs=[_row_spec(D_FF), _full_spec((D_FF, D)), _row_spec(D), vec, vec],
        out_specs=(_row_spec(D), pl.BlockSpec((TM, 1), _rows)),
        compiler_params=_params(("parallel",)),
    )(hff, w_down, xhat1, g1, b1)


def _tail_fwd_bwd(xhat2, rstd2, p, target, w_g, w_p, g2, b2, bg, g3, b3):
    def body(xh_ref, rs_ref, p_ref, t_ref, wg_ref, wp_ref, g2_ref, b2_ref, bg_ref, g3_ref, b3_ref,
             loss_ref, dy2_ref, dy2b_ref, gwg_ref, gwp_ref, vec_ref, dwg_ref, dwp_ref):
        @pl.when(pl.program_id(0) == 0)
        def _():
            loss_ref[...] = jnp.zeros_like(loss_ref)
            dwg_ref[...] = jnp.zeros_like(dwg_ref)
            dwp_ref[...] = jnp.zeros_like(dwp_ref)
            vec_ref[...] = jnp.zeros_like(vec_ref)

        xhat2_t = xh_ref[...]
        x2 = xhat2_t * g2_ref[...] + b2_ref[...]
        x2b = x2.astype(BF16)
        pb = p_ref[...].astype(BF16)
        gate = jax.nn.sigmoid(_dot(x2b, wg_ref[...]) + bg_ref[...])
        pin = _dot(pb, wp_ref[...])
        xhat3, rstd3 = _ln_stats(ALPHA * x2 + gate * pin)
        err = xhat3 * g3_ref[...] + b3_ref[...] - t_ref[...]
        loss_ref[...] += jnp.sum(jnp.mean(err * err, axis=-1, keepdims=True), axis=0, keepdims=True) * 0.5
        dout = err * (1.0 / D)
        dy3 = _ln_bwd(dout * g3_ref[...], xhat3, rstd3)
        dgp = dy3 * pin * gate * (1.0 - gate)
        dgpb = dgp.astype(BF16)
        dwg_ref[...] += _dot_tn(x2b, dgpb)
        dwp_ref[...] += _dot_tn(pb, (dy3 * gate).astype(BF16))
        dx2 = ALPHA * dy3 + _dot_nt(dgpb, wg_ref[...])
        dy2 = _ln_bwd(dx2 * g2_ref[...], xhat2_t, rs_ref[...])
        dy2_ref[...] = dy2
        dy2b_ref[...] = dy2.astype(BF16)
        vec_ref[0:1, :] += _colsum(dgp)
        vec_ref[1:2, :] += _colsum(dout * xhat3)
        vec_ref[2:3, :] += _colsum(dout)
        vec_ref[3:4, :] += _colsum(dx2 * xhat2_t)
        vec_ref[4:5, :] += _colsum(dx2)

        @pl.when(pl.program_id(0) == NT - 1)
        def _():
            for j in range(N_DEV):
                gwg_ref[_owner_slot(j)] = dwg_ref[LANES * j:LANES * (j + 1), :].astype(BF16)
                gwp_ref[_owner_slot(j)] = dwp_ref[:, LANES * j:LANES * (j + 1)].astype(BF16)

    vec = _full_spec((1, D))
    return pl.pallas_call(
        body, name="tail_fwd_bwd", grid=(NT,),
        out_shape=(jax.ShapeDtypeStruct((8, LANES), F32), jax.ShapeDtypeStruct((S, D), F32),
                   jax.ShapeDtypeStruct((S, D), BF16), jax.ShapeDtypeStruct((N_DEV, D // N_DEV, D), BF16),
                   jax.ShapeDtypeStruct((N_DEV, D_PLE, D // N_DEV), BF16), jax.ShapeDtypeStruct((8, D), F32)),
        in_specs=[_row_spec(D), pl.BlockSpec((TM, 1), _rows), _row_spec(D_PLE), _row_spec(D),
                  _full_spec((D, D)), _full_spec((D_PLE, D)), vec, vec, vec, vec, vec],
        out_specs=(_full_spec((8, LANES)), _row_spec(D), _row_spec(D), _full_spec((N_DEV, D // N_DEV, D)),
                   _full_spec((N_DEV, D_PLE, D // N_DEV)), _full_spec((8, D))),
        scratch_shapes=[pltpu.VMEM((D, D), F32), pltpu.VMEM((D_PLE, D), F32)],
        compiler_params=_params(("arbitrary",)),
    )(xhat2, rstd2, p, target, w_g, w_p, g2, b2, bg, g3, b3)


def _ffn_bwd_act(dy2b, w_down, a_pre, a, b_lin, hff, conv_w8):
    def body(dy_ref, wd_ref, ap_ref, a_ref, bl_ref, h_ref, cw_ref, dap_ref, dbl_ref, dwd_ref, dcw_ref, carry):
        @pl.when(pl.program_id(1) == 0)
        def _():
            carry[...] = jnp.zeros_like(carry)
            dwd_ref[...] = jnp.zeros_like(dwd_ref)
            dcw_ref[...] = jnp.zeros_like(dcw_ref)

        dyb = dy_ref[...]
        dh = _dot_nt(dyb, wd_ref[...])
        av = a_ref[...]
        cdf = 0.5 * (1.0 + lax.erf(av * INV_SQRT2))
        dbl_ref[...] = (dh * (av * cdf)).astype(BF16)
        da = dh * bl_ref[...].astype(F32) * (cdf + av * (jnp.exp(-0.5 * av * av) * INV_SQRT_2PI))
        row = lax.broadcasted_iota(jnp.int32, (TM, TN), 0)
        c0, c1 = carry[0:1, :], carry[1:2, :]
        p1 = jnp.where(row == TM - 1, c0, pltpu.roll(da, TM - 1, 0))
        p2 = jnp.where(row == TM - 2, c0, jnp.where(row == TM - 1, c1, pltpu.roll(da, TM - 2, 0)))
        carry[...] = da[0:8, :]
        ap = ap_ref[...].astype(F32)
        dcw_ref[3:4, :] += _colsum(da)
        dcw_ref[0:1, :] += _colsum(ap * p2)
        dcw_ref[1:2, :] += _colsum(ap * p1)
        dcw_ref[2:3, :] += _colsum(ap * da)
        dap_ref[...] = (cw_ref[2:3, :] * da + cw_ref[1:2, :] * p1 + cw_ref[0:1, :] * p2).astype(BF16)
        dwd_ref[...] += _dot_tn(h_ref[...], dyb)

    rev_tile = pl.BlockSpec((TM, TN), lambda j, i: (NT - 1 - i, j))
    wrows = pl.BlockSpec((TN, D), lambda j, i: (j, 0))
    small = pl.BlockSpec((8, TN), lambda j, i: (0, j))
    ffb = jax.ShapeDtypeStruct((S, D_FF), BF16)
    return pl.pallas_call(
        body, name="ffn_bwd_act", grid=(NJ, NT),
        out_shape=(ffb, ffb, jax.ShapeDtypeStruct((D_FF, D), F32), jax.ShapeDtypeStruct((8, D_FF), F32)),
        in_specs=[pl.BlockSpec((TM, D), lambda j, i: (NT - 1 - i, 0)), wrows, rev_tile, rev_tile, rev_tile,
                  rev_tile, small],
        out_specs=(rev_tile, rev_tile, wrows, small),
        scratch_shapes=[pltpu.VMEM((8, TN), F32)],
        compiler_params=_params(("arbitrary", "arbitrary"), 56),
    )(dy2b, w_down, a_pre, a, b_lin, hff, conv_w8)


def _ffn_bwd_w(dap, dbl, x1b):
    def body(dap_ref, dbl_ref, x_ref, dwa_ref, dwb_ref):
        @pl.when(pl.program_id(1) == 0)
        def _():
            dwa_ref[...] = jnp.zeros_like(dwa_ref)
            dwb_ref[...] = jnp.zeros_like(dwb_ref)

        xb = x_ref[...]
        dwa_ref[...] += _dot_tn(dap_ref[...], xb)
        dwb_ref[...] += _dot_tn(dbl_ref[...], xb)

    tile = pl.BlockSpec((TM, TN), lambda j, i: (i, j))
    wrows = pl.BlockSpec((TN, D), lambda j, i: (j, 0))
    full = jax.ShapeDtypeStruct((D_FF, D), F32)
    return pl.pallas_call(
        body, name="ffn_bwd_w", grid=(NJ, NT), out_shape=(full, full),
        in_specs=[tile, tile, pl.BlockSpec((TM, D), lambda j, i: (i, 0))], out_specs=(wrows, wrows),
        compiler_params=_params(("arbitrary", "arbitrary"), 56),
    )(dap, dbl, x1b)


def _ffn_bwd_x(dap, dbl, w_a, w_b):
    def body(dap_ref, dbl_ref, wa_ref, wb_ref, dx_ref):
        dx_ref[...] = _dot(dap_ref[...], wa_ref[...]) + _dot(dbl_ref[...], wb_ref[...])

    return pl.pallas_call(
        body, name="ffn_bwd_x", grid=(NT,), out_shape=jax.ShapeDtypeStruct((S, D), F32),
        in_specs=[_row_spec(D_FF), _row_spec(D_FF), _full_spec((D_FF, D)), _full_spec((D_FF, D))],
        out_specs=_row_spec(D), compiler_params=_params(("parallel",), 56),
    )(dap, dbl, w_a, w_b)


def _ln1_mix_bwd(dy2, dx1_ffn, xhat1, rstd1, g1, attn, gm, w_o, carried=None):
    def body(dy2_ref, dxf_ref, xh_ref, rs_ref, g_ref, a_ref, gm_ref, w_ref,
             dy1_ref, da_ref, dlt_ref, dgm_ref, gwo_ref, vec_ref, dwo_ref):
        @pl.when(pl.program_id(0) == 0)
        def _():
            dwo_ref[...] = jnp.zeros_like(dwo_ref)
            vec_ref[...] = jnp.zeros_like(vec_ref)

        xhat = xh_ref[...]
        dx1 = ALPHA * dy2_ref[...] + dxf_ref[...]
        vec_ref[0:1, :] += _colsum(dx1 * xhat)
        vec_ref[1:2, :] += _colsum(dx1)
        dy1 = _ln_bwd(dx1 * g_ref[...], xhat, rs_ref[...])
        dy1_ref[...] = dy1
        dy1b = dy1.astype(BF16)
        dmix = _dot_nt(dy1b, w_ref[...])
        attn_t = a_ref[...]
        d_attn = dmix[:, 0:D_HALF]
        da_ref[...] = d_attn
        dgm_ref[...] = dmix[:, D_HALF:D]
        lo = (lax.broadcasted_iota(jnp.int32, (TM, LANES), 1) < 64)
        for s in range(4):
            sl = slice(s * LANES, (s + 1) * LANES)
            prod = d_attn[:, sl] * attn_t[:, sl]
            s0 = jnp.sum(jnp.where(lo, prod, 0.0), axis=-1, keepdims=True)
            s1 = jnp.sum(jnp.where(lo, 0.0, prod), axis=-1, keepdims=True)
            dlt_ref[:, sl] = jnp.where(lo, s0, s1)
        dwo_ref[0:D_HALF, :] += _dot_tn(attn_t.astype(BF16), dy1b)
        dwo_ref[D_HALF:D, :] += _dot_tn(gm_ref[...], dy1b)

        @pl.when(pl.program_id(0) == NT - 1)
        def _():
            for j in range(N_DEV):
                gwo_ref[_owner_slot(j)] = dwo_ref[LANES * j:LANES * (j + 1), :].astype(BF16)

    half = jax.ShapeDtypeStruct((S, D_HALF), F32)
    return _host_call(
        body, carried, name="ln1_mix_bwd", grid=(NT,),
        out_shape=(jax.ShapeDtypeStruct((S, D), F32), half, half, half,
                   jax.ShapeDtypeStruct((N_DEV, D // N_DEV, D), BF16), jax.ShapeDtypeStruct((8, D), F32)),
        in_specs=[_row_spec(D), _row_spec(D), _row_spec(D), pl.BlockSpec((TM, 1), _rows), _full_spec((1, D)),
                  _row_spec(D_HALF), _row_spec(D_HALF), _full_spec((D, D))],
        out_specs=(_row_spec(D), _row_spec(D_HALF), _row_spec(D_HALF), _row_spec(D_HALF),
                   _full_spec((N_DEV, D // N_DEV, D)), _full_spec((8, D))),
        scratch_shapes=[pltpu.VMEM((D, D), F32)],
        params=_params(("arbitrary",)), args=(dy2, dx1_ffn, xhat1, rstd1, g1, attn, gm, w_o))


def _gmlp_bwd(d_gm, u_pre, z_pre, ln_z_g, ln_z_b, w_s, bs_exp, carried=None):
    def body(dg_ref, u_ref, z_ref, g_ref, b_ref, ws_ref, bs_ref, du_ref, dz_ref, dws_ref, dbs_ref, vec_ref):
        @pl.when(pl.program_id(0) == 0)
        def _():
            dws_ref[...] = jnp.zeros_like(dws_ref)
            dbs_ref[...] = jnp.zeros_like(dbs_ref)
            vec_ref[...] = jnp.zeros_like(vec_ref)

        u_pre_t, z_pre_t, dgm = u_ref[...], z_ref[...], dg_ref[...]
        z_act, z_slope = _gelu_and_grad(z_pre_t)
        u_act, u_slope = _gelu_and_grad(u_pre_t)
        zhat, rstd = _ln_stats(z_act)
        zn = zhat * g_ref[...] + b_ref[...]
        wm = _masked_ws(ws_ref)
        mixed = _spatial_mix(zn, wm, bs_ref[...])
        du_ref[...] = (dgm * mixed * u_slope).astype(BF16)
        dmixed = dgm * u_act
        lo = _lane_lo()
        tril = _tril()
        group_of_lane = lax.broadcasted_iota(jnp.int32, (8, D_HALF), 1) // 64
        pick = (group_of_lane == lax.broadcasted_iota(jnp.int32, (8, D_HALF), 0)).astype(F32)
        dzn_rows = []
        for ch in range(TM // CHUNK):
            rows = slice(ch * CHUNK, (ch + 1) * CHUNK)
            dbs_ref[...] += lax.dot_general(pick, dmixed[rows, :], (((1,), (1,)), ((), ())),
                                            precision=lax.Precision.HIGHEST, preferred_element_type=F32)
            slabs = []
            for pr in range(4):
                sl = slice(pr * LANES, (pr + 1) * LANES)
                dm = dmixed[rows, sl]
                zp = zn[rows, sl].astype(BF16)
                dm_lo = jnp.where(lo, dm, 0.0).astype(BF16)
                dm_hi = jnp.where(lo, 0.0, dm).astype(BF16)
                dws_ref[2 * pr] += jnp.where(tril, _dot_nt(dm_lo, zp), 0.0)
                dws_ref[2 * pr + 1] += jnp.where(tril, _dot_nt(dm_hi, zp), 0.0)
                dmb = dm.astype(BF16)
                slabs.append(jnp.where(lo, _dot_tn(wm[2 * pr], dmb), _dot_tn(wm[2 * pr + 1], dmb)))
            dzn_rows.append(jnp.concatenate(slabs, axis=1))
        dzn = jnp.concatenate(dzn_rows, axis=0)
        vec_ref[0:1, :] += _colsum(dzn * zhat)
        vec_ref[1:2, :] += _colsum(dzn)
        dz = _ln_bwd(dzn * g_ref[...], zhat, rstd)
        dz_ref[...] = (dz * z_slope).astype(BF16)

    halfb = jax.ShapeDtypeStruct((S, D_HALF), BF16)
    vec = _full_spec((1, D_HALF))
    return _host_call(
        body, carried, name="gmlp_bwd", grid=(NT,),
        out_shape=(halfb, halfb, jax.ShapeDtypeStruct((8, CHUNK, CHUNK), F32),
                   jax.ShapeDtypeStruct((8, CHUNK), F32), jax.ShapeDtypeStruct((8, D_HALF), F32)),
        in_specs=[_row_spec(D_HALF), _row_spec(D_HALF), _row_spec(D_HALF), vec, vec,
                  _full_spec((8, CHUNK, CHUNK)), _full_spec((CHUNK, D_HALF))],
        out_specs=(_row_spec(D_HALF), _row_spec(D_HALF), _full_spec((8, CHUNK, CHUNK)),
                   _full_spec((8, CHUNK)), _full_spec((8, D_HALF))),
        scratch_shapes=[], params=_params(("arbitrary",)), args=(d_gm, u_pre, z_pre, ln_z_g, ln_z_b, w_s, bs_exp))


def _attention_bwd(q, k, v, lse, d_attn, delta, tabs, carried=None):
    def body(q_ref, k_ref, v_ref, l_ref, do_ref, dl_ref, c_ref, sa_ref, sb_ref, dq_ref, dk_ref, dv_ref,
             qb, kb, vb, gb, bias, lsp, dlp, dqp, dk_own, dk_prev, dv_own, dv_prev, dqa, dka, dva):
        lo = _lane_lo()
        zero_pad = jnp.zeros((CHUNK, LANES), BF16)
        for buf in (qb, kb, vb, gb):
            buf[0:CHUNK, :] = zero_pad
        dk_prev[S:S + CHUNK, :] = jnp.zeros((CHUNK, LANES), F32)
        dv_prev[S:S + CHUNK, :] = jnp.zeros((CHUNK, LANES), F32)
        _store_band_bias(bias)
        for d, nb in DILATIONS:
            length = S // d
            for r in range(d):
                dst = slice(CHUNK + r * length, CHUNK + (r + 1) * length)
                src = slice(r * length, (r + 1) * length)
                qb[dst, :] = (_permuted_rows(q_ref, d, r) * 0.125).astype(BF16)
                kb[dst, :] = _permuted_rows(k_ref, d, r).astype(BF16)
                vb[dst, :] = _permuted_rows(v_ref, d, r).astype(BF16)
                gb[dst, :] = _permuted_rows(do_ref, d, r).astype(BF16)
                lsp[src, :] = _permuted_rows(l_ref, d, r)
                dlp[src, :] = _permuted_rows(dl_ref, d, r)

            def block(b, carry, nb=nb):
                base = pl.multiple_of(b * CHUNK, CHUNK)
                own = pl.multiple_of(base + CHUNK, CHUNK)
                add = bias[jnp.where(b % nb == 0, 1, 0)]
                qblk = qb[pl.ds(own, CHUNK), :]
                gblk = gb[pl.ds(own, CHUNK), :]
                kblk = kb[pl.ds(base, 2 * CHUNK), :]
                vblk = vb[pl.ds(base, 2 * CHUNK), :]
                lse_t = lsp[pl.ds(base, CHUNK), :]
                dlt_t = dlp[pl.ds(base, CHUNK), :]
                q2 = jnp.concatenate([jnp.where(lo, qblk, 0), jnp.where(lo, 0, qblk)], axis=0)
                g2 = jnp.concatenate([jnp.where(lo, gblk, 0), jnp.where(lo, 0, gblk)], axis=0)
                lse2 = jnp.concatenate([lse_t[:, 0:1], lse_t[:, 64:65]], axis=0)
                dlt2 = jnp.concatenate([dlt_t[:, 0:1], dlt_t[:, 64:65]], axis=0)
                add2 = jnp.concatenate([add, add], axis=0)
                p = jnp.exp(_dot_nt(q2, kblk) + add2 - lse2)
                ds = (p * (_dot_nt(g2, vblk) - dlt2)).astype(BF16)
                dv_blk = _dot_tn(p.astype(BF16), g2)
                dk_blk = _dot_tn(ds, q2)
                dq2 = _dot(ds, kblk)
                dqp[pl.ds(base, CHUNK), :] = jnp.where(lo, dq2[0:CHUNK, :], dq2[CHUNK:2 * CHUNK, :]) * 0.125
                dk_prev[pl.ds(base, CHUNK), :] = dk_blk[0:CHUNK, :]
                dk_own[pl.ds(own, CHUNK), :] = dk_blk[CHUNK:2 * CHUNK, :]
                dv_prev[pl.ds(base, CHUNK), :] = dv_blk[0:CHUNK, :]
                dv_own[pl.ds(own, CHUNK), :] = dv_blk[CHUNK:2 * CHUNK, :]
                return carry

            lax.fori_loop(0, S // CHUNK, block, 0, unroll=ATTN_UNROLL)
            for r in range(d):
                src = slice(r * length, (r + 1) * length)
                pad = slice(CHUNK + r * length, CHUNK + (r + 1) * length)
                if d == 1:
                    dqa[...] = dqp[...]
                    dka[...] = dk_own[pad, :] + dk_prev[pad, :]
                    dva[...] = dv_own[pad, :] + dv_prev[pad, :]
                else:
                    dst = pl.ds(r, length, stride=d)
                    dqa[dst, :] = dqa[dst, :] + dqp[src, :]
                    dka[dst, :] = dka[dst, :] + (dk_own[pad, :] + dk_prev[pad, :])
                    dva[dst, :] = dva[dst, :] + (dv_own[pad, :] + dv_prev[pad, :])
        for t in range(NT):
            rows = slice(t * TM, (t + 1) * TM)
            c, sa, sb = c_ref[rows, :], sa_ref[rows, :], sb_ref[rows, :]
            dq_ref[rows, :] = _rope_t(dqa[rows, :], c, sa, sb).astype(BF16)
            dk_ref[rows, :] = _rope_t(dka[rows, :], c, sa, sb).astype(BF16)
            dv_ref[rows, :] = dva[rows, :].astype(BF16)

    slab = pl.BlockSpec((S, LANES), lambda h: (0, h), pipeline_mode=pl.Buffered(1))
    tab = pl.BlockSpec((S, LANES), lambda h: (0, 0), pipeline_mode=pl.Buffered(1))
    out_slab = pl.BlockSpec((S, LANES), lambda h: (0, h))
    out = jax.ShapeDtypeStruct((S, D_HALF), BF16)
    padded_b = pltpu.VMEM((CHUNK + S, LANES), BF16)
    padded_f = pltpu.VMEM((CHUNK + S, LANES), F32)
    whole = pltpu.VMEM((S, LANES), F32)
    return _host_call(
        body, carried, name="attention_bwd", grid=(4,), out_shape=(out, out, out),
        in_specs=[slab] * 6 + [tab] * 3, out_specs=(out_slab,) * 3,
        scratch_shapes=[padded_b] * 4 + [pltpu.VMEM((2, CHUNK, 2 * CHUNK), F32)] + [whole] * 3
        + [padded_f] * 4 + [whole] * 3,
        params=_params(("arbitrary",), 60), args=(q, k, v, lse, d_attn, delta, *tabs))


def _proj_in_bwd_w(xb, parts):
    def body(x_ref, p0, p1, p2, p3, p4, gw_ref, dw_ref):
        @pl.when(pl.program_id(0) == 0)
        def _():
            dw_ref[...] = jnp.zeros_like(dw_ref)

        xt = x_ref[...]
        for n, part in enumerate((p0, p1, p2, p3, p4)):
            dw_ref[n * D_HALF:(n + 1) * D_HALF, :] += _dot_tn(part[...], xt)

        @pl.when(pl.program_id(0) == NT - 1)
        def _():
            width = D_IN // N_DEV
            for j in range(N_DEV):
                gw_ref[_owner_slot(j)] = dw_ref[width * j:width * (j + 1), :].astype(BF16)

    return pl.pallas_call(
        body, name="proj_in_bwd_w", grid=(NT,), out_shape=jax.ShapeDtypeStruct((N_DEV, D_IN // N_DEV, D), BF16),
        in_specs=[_row_spec(D)] + [_row_spec(D_HALF)] * 5, out_specs=_full_spec((N_DEV, D_IN // N_DEV, D)),
        scratch_shapes=[pltpu.VMEM((D_IN, D), F32)],
        compiler_params=_params(("arbitrary",)),
    )(xb, *parts)


def _proj_in_bwd_x(dy1, parts, w_in, carried=None):
    def body(dy_ref, p0, p1, p2, p3, p4, w_ref, gx_ref):
        acc = ALPHA * dy_ref[...]
        for n, part in enumerate((p0, p1, p2, p3, p4)):
            acc += _dot(part[...], w_ref[n * D_HALF:(n + 1) * D_HALF, :])
        gx_ref[...] = acc

    return _host_call(
        body, carried, name="proj_in_bwd_x", grid=(NT,), out_shape=(jax.ShapeDtypeStruct((S, D), F32),),
        in_specs=[_row_spec(D)] + [_row_spec(D_HALF)] * 5 + [_full_spec((D_IN, D))], out_specs=(_row_spec(D),),
        scratch_shapes=[], params=_params(("arbitrary",)), args=(dy1, *parts, w_in))


def _to_natural(blocks, name):
    n, rows, w = blocks.shape
    tile = min(rows, 256)

    def body(i_ref, o_ref):
        o_ref[...] = jnp.concatenate([i_ref[j] for j in range(n)], axis=1)

    return pl.pallas_call(
        body, name=name, grid=(rows // tile,), out_shape=jax.ShapeDtypeStruct((rows, n * w), blocks.dtype),
        in_specs=[pl.BlockSpec((n, tile, w), lambda i: (0, i, 0))],
        out_specs=pl.BlockSpec((tile, n * w), lambda i: (i, 0)), compiler_params=_params(("parallel",)),
    )(blocks)


def _row_blocks(full, name):
    rows, cols = full.shape
    r = rows // N_DEV

    def body(i_ref, o_ref):
        o_ref[0] = i_ref[...].astype(BF16)

    return pl.pallas_call(
        body, name=name, grid=(N_DEV,), out_shape=jax.ShapeDtypeStruct((N_DEV, r, cols), BF16),
        in_specs=[pl.BlockSpec((r, cols), lambda s: ((s % 4) * 2 + s // 4, 0))],
        out_specs=pl.BlockSpec((1, r, cols), lambda s: (s, 0, 0)), compiler_params=_params(("parallel",)),
    )(full)


def _local_step(x, p, pos_col, target, sm, ex):
    bs_exp = jnp.repeat(sm["b_s"].T, 64, axis=1)
    tabs, got = _rope_tables(pos_col, ex.gather_input())
    w_in = ex.weight_input(got)
    q, k, v, u_pre, z_pre, gm, xb = _proj_in_fwd(x, w_in, tabs, sm["ln_z_g"], sm["ln_z_b"], sm["w_s"], bs_exp)
    (attn, lse), got = _attention_fwd(q, k, v, ex.gather_first())
    wa = ex.weights_first(got)
    xhat1, rstd1, x1b = _mix_ln1_fwd(attn, gm, wa["w_o"], x, sm["ln1_g"], sm["ln1_b"])
    (a_pre, a, b_lin, hff), got = _ffn_up_fwd(x1b, wa["w_ff_a"], wa["w_ff_b"], wa["conv_w8"], sm["conv_b"],
                                              ex.gather_second())
    wc = ex.weights_second(got)
    xhat2, rstd2 = _ffn_down_ln2_fwd(hff, wc["w_ff_down"], xhat1, sm["ln1_g"], sm["ln1_b"])
    loss, dy2, dy2b, dw_g, dw_p, vec_tail = _tail_fwd_bwd(
        xhat2, rstd2, p, target, wc["w_ple_gate"], wc["w_ple_in"], sm["ln2_g"], sm["ln2_b"],
        sm["b_ple_gate"], sm["ln3_g"], sm["ln3_b"])
    dap, dbl, dw_down, dconv = _ffn_bwd_act(dy2b, wc["w_ff_down"], a_pre, a, b_lin, hff, wa["conv_w8"])
    dw_a, dw_b = _ffn_bwd_w(dap, dbl, x1b)
    dx1_ffn = _ffn_bwd_x(dap, dbl, wa["w_ff_a"], wa["w_ff_b"])
    (dy1, d_attn, delta, d_gm, dw_o, vec_ln1), _ = _ln1_mix_bwd(
        dy2, dx1_ffn, xhat1, rstd1, sm["ln1_g"], attn, gm, wa["w_o"])
    early = {"w_ff_a": _row_blocks(dw_a, "ff_a_grad_blocks"), "w_ff_b": _row_blocks(dw_b, "ff_b_grad_blocks"),
             "w_ff_down": _row_blocks(dw_down, "ff_down_grad_blocks"), "w_ple_gate": dw_g, "w_ple_in": dw_p,
             "w_o": dw_o}
    (du, dz, dws, dbs, vec_z), got = _gmlp_bwd(d_gm, u_pre, z_pre, sm["ln_z_g"], sm["ln_z_b"], sm["w_s"], bs_exp,
                                               ex.to_sibling(early))
    chip_sums = ex.reduce_on_chip(got)
    small = {"tail": vec_tail, "ln1": vec_ln1, "ln_z": vec_z, "conv": dconv, "w_s": dws, "b_s": dbs, "loss": loss}
    (dq, dk, dv), got_early = _attention_bwd(q, k, v, lse, d_attn, delta, tabs,
                                             ex.between_chips(chip_sums, small))
    parts = (dq, dk, dv, du, dz)
    (grad_x,), got_late = _proj_in_bwd_x(dy1, parts, w_in, ex.last(_proj_in_bwd_w(xb, parts)))
    return grad_x, ex.collect(got_early, got_late)


def _mesh_pos():
    return lax.axis_index("x"), lax.axis_index("y"), lax.axis_index("c")


def _cast_shards(shards):
    n = len(shards)

    def body(*refs):
        for a in range(n):
            refs[n + a][...] = refs[a][...].astype(BF16)

    whole = [_full_spec(s.shape) for s in shards]
    return pl.pallas_call(
        body, name="cast_shards", grid=(1,), out_shape=tuple(jax.ShapeDtypeStruct(s.shape, BF16) for s in shards),
        in_specs=whole, out_specs=tuple(whole), compiler_params=_params(("arbitrary",)),
    )(*shards)


class _GatherComm:
    def __init__(self, shards):
        n = len(shards)
        self.inputs = list(shards)
        self.out_shapes = [jax.ShapeDtypeStruct((N_DEV,) + s.shape, s.dtype) for s in shards]
        self.scratch = [pltpu.SemaphoreType.DMA((7 * n,)), pltpu.SemaphoreType.DMA((7 * n,)),
                        pltpu.SemaphoreType.DMA((n,))]

    def phases(self, x_refs, out_refs, sems):
        send_sems, recv_sems, local_sems = sems
        n_arr = len(x_refs)

        def where():
            x, y, c = _mesh_pos()
            return (x, y, c), (x, y, 1 - c), [(1 - x, y), (x, 1 - y), (1 - x, 1 - y)]

        def copy(a, n, block, to, from_shard=False):
            dst = out_refs[a].at[4 * block[0] + 2 * block[1] + block[2]]
            return pltpu.make_async_remote_copy(
                src_ref=x_refs[a] if from_shard else dst, dst_ref=dst, send_sem=send_sems.at[7 * a + n],
                recv_sem=recv_sems.at[7 * a + n], device_id=to, device_id_type=MESH)

        def local(a):
            x, y, c = _mesh_pos()
            return pltpu.make_async_copy(x_refs[a], out_refs[a].at[4 * x + 2 * y + c], local_sems.at[a])

        def start():
            me, sibling, chips = where()
            for a in range(n_arr):
                local(a).start()
                copy(a, 0, me, sibling, from_shard=True).start()
                for n, chip in enumerate(chips):
                    copy(a, 1 + n, me, (*chip, me[2]), from_shard=True).start()

        def forward():
            me, sibling, chips = where()
            for n, chip in enumerate(chips):
                for a in range(n_arr):
                    copy(a, 1 + n, (*chip, me[2]), me).wait_recv()
                    copy(a, 4 + n, (*chip, me[2]), sibling).start()

        def finish():
            me, sibling, chips = where()
            for a in range(n_arr):
                copy(a, 0, sibling, me).wait_recv()
                copy(a, 0, me, sibling, from_shard=True).wait_send()
                for n, chip in enumerate(chips):
                    copy(a, 4 + n, (*chip, 1 - me[2]), me).wait_recv()
                    copy(a, 1 + n, me, (*chip, me[2]), from_shard=True).wait_send()
                    copy(a, 4 + n, (*chip, me[2]), sibling).wait_send()
                local(a).wait()

        return {"start": start, "forward": forward, "finish": finish}


class _SiblingComm:
    def __init__(self, big):
        n = len(big)
        self.inputs = list(big)
        self.out_shapes = [jax.ShapeDtypeStruct(b.shape[1:], b.dtype) for b in big]
        self.scratch = [pltpu.SemaphoreType.DMA((n,)), pltpu.SemaphoreType.DMA((n,))]

    def phases(self, src, dst, sems):
        send_sems, recv_sems = sems

        def copies():
            x, y, c = _mesh_pos()
            return [pltpu.make_async_remote_copy(
                src_ref=src[a].at[1 - c], dst_ref=dst[a], send_sem=send_sems.at[a], recv_sem=recv_sems.at[a],
                device_id=(x, y, 1 - c), device_id_type=MESH) for a in range(len(src))]

        def start():
            for cp in copies():
                cp.start()

        def finish():
            for cp in copies():
                cp.wait()

        return {"start": start, "finish": finish}


class _ChipComm:
    def __init__(self, sums):
        n = len(sums)
        self.inputs = list(sums)
        self.out_shapes = [jax.ShapeDtypeStruct(s.shape, s.dtype) for s in sums]
        self.scratch = [pltpu.SemaphoreType.DMA((3 * n,)), pltpu.SemaphoreType.DMA((3 * n,)),
                        pltpu.SemaphoreType.DMA((n,))]

    def phases(self, src, dst, sems):
        send_sems, recv_sems, local_sems = sems

        def copies():
            x, y, c = _mesh_pos()
            my_chip = 2 * x + y
            out = [pltpu.make_async_copy(src[a].at[my_chip], dst[a].at[my_chip], local_sems.at[a])
                   for a in range(len(src))]
            for n, (px, py) in enumerate([(1 - x, y), (x, 1 - y), (1 - x, 1 - y)]):
                for a in range(len(src)):
                    out.append(pltpu.make_async_remote_copy(
                        src_ref=src[a].at[2 * px + py], dst_ref=dst[a].at[my_chip],
                        send_sem=send_sems.at[3 * a + n], recv_sem=recv_sems.at[3 * a + n],
                        device_id=(px, py, c), device_id_type=MESH))
            return out

        def start():
            for cp in copies():
                cp.start()

        def finish():
            for cp in copies():
                cp.wait()

        return {"start": start, "finish": finish}


class _ScatterComm:
    def __init__(self, blocks, small):
        self.n_big, self.n_small = len(blocks), len(small)
        n = self.n_big + self.n_small
        self.inputs = list(blocks) + list(small)
        self.out_shapes = ([jax.ShapeDtypeStruct(b.shape, b.dtype) for b in blocks]
                           + [jax.ShapeDtypeStruct((N_DEV,) + s.shape, s.dtype) for s in small])
        self.scratch = [pltpu.SemaphoreType.DMA((7 * n,)), pltpu.SemaphoreType.DMA((7 * n,)),
                        pltpu.SemaphoreType.DMA((n,))]

    def phases(self, src, dst, sems):
        send_sems, recv_sems, local_sems = sems
        n_big, n_all = self.n_big, self.n_big + self.n_small

        def source(a, core, chip):
            return src[a].at[core * 4 + chip] if a < n_big else src[a]

        def copies():
            x, y, c = _mesh_pos()
            me = 4 * x + 2 * y + c
            out = [pltpu.make_async_copy(source(a, c, 2 * x + y), dst[a].at[me], local_sems.at[a])
                   for a in range(n_all)]
            for flip in range(1, N_DEV):
                px = 1 - x if flip & 4 else x
                py = 1 - y if flip & 2 else y
                pc = 1 - c if flip & 1 else c
                for a in range(n_all):
                    n = 7 * a + flip - 1
                    out.append(pltpu.make_async_remote_copy(
                        src_ref=source(a, pc, 2 * px + py), dst_ref=dst[a].at[me], send_sem=send_sems.at[n],
                        recv_sem=recv_sems.at[n], device_id=(px, py, pc), device_id_type=MESH))
            return out

        def start():
            for cp in copies():
                cp.start()

        def finish():
            for cp in copies():
                cp.wait()

        return {"start": start, "finish": finish}


class _Both:
    def __init__(self, first, second):
        self.parts = (first, second)
        self.inputs = first.inputs + second.inputs
        self.out_shapes = first.out_shapes + second.out_shapes
        self.scratch = first.scratch + second.scratch

    def phases(self, src, dst, sems):
        a, b = self.parts
        pa = a.phases(src[:len(a.inputs)], dst[:len(a.out_shapes)], sems[:len(a.scratch)])
        pb = b.phases(src[len(a.inputs):], dst[len(a.out_shapes):], sems[len(a.scratch):])

        def both(name):
            def run():
                for phases in (pa, pb):
                    if name in phases:
                        phases[name]()
            return run

        return {name: both(name) for name in {**pa, **pb}}


def _host_call(body, carried, *, name, grid, out_shape, in_specs, out_specs, scratch_shapes, params, args):
    if carried is None:
        return pl.pallas_call(body, name=name, grid=grid, out_shape=tuple(out_shape), in_specs=list(in_specs),
                              out_specs=tuple(out_specs), scratch_shapes=list(scratch_shapes),
                              compiler_params=params)(*args), ()
    comm, when = carried
    n_in, n_out, n_scratch = len(in_specs), len(out_shape), len(scratch_shapes)
    k_in, k_out = len(comm.inputs), len(comm.out_shapes)

    def wrapped(*refs):
        bounds = np.cumsum([0, n_in, k_in, n_out, k_out, n_scratch])
        ins, c_in, outs, c_out, scr = (refs[bounds[i]:bounds[i + 1]] for i in range(5))
        phases = comm.phases(c_in, c_out, refs[bounds[5]:])
        for phase, cond in when("before"):
            pl.when(cond)(phases[phase])
        body(*ins, *outs, *scr)
        for phase, cond in when("after"):
            pl.when(cond)(phases[phase])

    anywhere = pl.BlockSpec(memory_space=pl.ANY)
    results = pl.pallas_call(
        wrapped, name=name, grid=grid, out_shape=tuple(out_shape) + tuple(comm.out_shapes),
        in_specs=list(in_specs) + [anywhere] * k_in, out_specs=tuple(out_specs) + (anywhere,) * k_out,
        scratch_shapes=list(scratch_shapes) + comm.scratch, compiler_params=params,
    )(*args, *comm.inputs)
    return results[:n_out], results[n_out:]


class _Exchanges:
    FIRST = ("w_o", "w_ff_a", "w_ff_b")
    SECOND = ("w_ff_down", "w_ple_gate", "w_ple_in")
    EARLY = ("w_ff_a", "w_ff_b", "w_ff_down", "w_ple_gate", "w_ple_in", "w_o")
    LATE = ("w_in",)

    def __init__(self, shards, conv_rows):
        self.shards, self.conv_rows = shards, conv_rows
        self.mode = {name: mode for name, _, mode in BIG}

    def _natural(self, name, blocks):
        n, r, c = blocks.shape
        return _to_natural(blocks, name + "_natural") if self.mode[name] == "cols" else blocks.reshape(n * r, c)

    def gather_input(self):
        def when(position):
            step = pl.program_id(0)
            if position == "before":
                return [("start", step == 0)]
            return [("forward", step == NT - 1), ("finish", step == NT - 1)]
        return _GatherComm([self.shards["w_in"]]), when

    def weight_input(self, got):
        return self._natural("w_in", got[0])

    def gather_first(self):
        comm = _GatherComm([self.shards[n] for n in self.FIRST] + [self.conv_rows])

        def when(position):
            step = pl.program_id(0)
            if position == "before":
                return [("start", step == 0), ("forward", step == 3)]
            return [("finish", step == 3)]
        return comm, when

    def weights_first(self, got):
        out = {name: self._natural(name, blocks) for name, blocks in zip(self.FIRST, got)}
        out["conv_w8"] = _to_natural(got[-1], "conv_w_natural")
        return out

    def gather_second(self):
        comm = _GatherComm([self.shards[n] for n in self.SECOND])

        def when(position):
            j, i = pl.program_id(0), pl.program_id(1)
            if position == "before":
                return [("start", (j == 0) & (i == 0)), ("forward", (j == NJ - 1) & (i == NT // 2))]
            return [("finish", (j == NJ - 1) & (i == NT - 1))]
        return comm, when

    def weights_second(self, got):
        return {name: self._natural(name, blocks) for name, blocks in zip(self.SECOND, got)}

    def to_sibling(self, early):
        self.by_core = [early[n].reshape((2, 4) + early[n].shape[1:]) for n in self.EARLY]
        return _SiblingComm(self.by_core), _first_and_last(NT)

    def reduce_on_chip(self, from_sibling):
        core = lax.axis_index("c").astype(jnp.int32).reshape(1)
        return _chip_reduce(self.by_core, from_sibling, core, "chip_reduce")

    def between_chips(self, chip_sums, small):
        self.small_keys = tuple(small)

        def when(position):
            step = pl.program_id(0)
            if position == "before":
                return [("start", step == 0), ("forward", step == 2)]
            return [("finish", step == 3)]
        return _Both(_ChipComm(chip_sums), _GatherComm([small[k] for k in self.small_keys])), when

    def last(self, dw_in):
        by_core = [dw_in.reshape((2, 4) + dw_in.shape[1:])]
        core = lax.axis_index("c").astype(jnp.int32).reshape(1)
        sums = _chip_reduce(by_core, _standalone(_SiblingComm(by_core), "w_in_grad_to_sibling"), core, "w_in_chip_reduce")
        return _ChipComm(sums), _first_and_last(NT)

    def collect(self, got_early, got_late):
        parts = dict(zip(self.EARLY, got_early[:len(self.EARLY)]))
        parts.update(zip(self.LATE, got_late))
        return parts, dict(zip(self.small_keys, got_early[len(self.EARLY):]))


def _first_and_last(n_steps):
    def when(position):
        step = pl.program_id(0)
        return [("start", step == 0)] if position == "before" else [("finish", step == n_steps - 1)]
    return when


def _standalone(comm, name):
    n_in = len(comm.inputs)

    def body(*refs):
        phases = comm.phases(refs[:n_in], refs[n_in:n_in + len(comm.out_shapes)], refs[n_in + len(comm.out_shapes):])
        phases["start"]()
        phases["finish"]()

    anywhere = pl.BlockSpec(memory_space=pl.ANY)
    return pl.pallas_call(
        body, name=name, out_shape=tuple(comm.out_shapes), in_specs=[anywhere] * n_in,
        out_specs=(anywhere,) * len(comm.out_shapes), scratch_shapes=comm.scratch,
    )(*comm.inputs)


def _chip_reduce(big, from_sibling, core, name):
    n = len(big)

    def body(core_ref, *refs):
        for a in range(n):
            mine, theirs, out = refs[a], refs[n + a], refs[2 * n + a]
            out[0] = (mine[0, 0].astype(F32) + theirs[0].astype(F32)).astype(BF16)

    def block(shape):
        return pl.BlockSpec((1,) + shape, lambda ch, core_ref: (ch, 0, 0))

    grid_spec = pltpu.PrefetchScalarGridSpec(
        num_scalar_prefetch=1, grid=(4,),
        in_specs=[pl.BlockSpec((1, 1) + b.shape[2:], lambda ch, core_ref: (core_ref[0], ch, 0, 0)) for b in big]
        + [block(b.shape[2:]) for b in big],
        out_specs=[block(b.shape[2:]) for b in big])
    return pl.pallas_call(
        body, name=name, grid_spec=grid_spec,
        out_shape=tuple(jax.ShapeDtypeStruct(b.shape[1:], BF16) for b in big),
        compiler_params=_params(("parallel",)),
    )(core, *big, *from_sibling)


def _adamw(g, w, m, v):
    nm = ADAM_B1 * m + (1.0 - ADAM_B1) * g
    nv = ADAM_B2 * v + (1.0 - ADAM_B2) * (g * g)
    m_hat = nm / (1.0 - ADAM_B1 ** ADAM_STEP)
    v_hat = nv / (1.0 - ADAM_B2 ** ADAM_STEP)
    return -ADAM_LR * (m_hat / (jnp.sqrt(v_hat) + ADAM_EPS) + ADAM_WD * w), nm, nv


def _adamw_sharded(parts, w, m, v, name):
    def body(p_ref, w_ref, m_ref, v_ref, g_ref, d_ref, nm_ref, nv_ref):
        g = p_ref[0].astype(F32)
        for s in range(1, parts.shape[0]):
            g = g + p_ref[s].astype(F32)
        delta, nm, nv = _adamw(g, w_ref[0], m_ref[0], v_ref[0])
        g_ref[0] = g
        d_ref[0] = delta
        nm_ref[0] = nm
        nv_ref[0] = nv

    n, r, c = parts.shape
    steps = 4 if r % 64 == 0 and r >= 512 else (2 if r % 32 == 0 and r >= 256 else 1)
    tile = pl.BlockSpec((1, r // steps, c), lambda i: (0, i, 0))
    return pl.pallas_call(
        body, name=name, grid=(steps,), out_shape=(jax.ShapeDtypeStruct(w.shape, F32),) * 4,
        in_specs=[pl.BlockSpec((n, r // steps, c), lambda i: (0, i, 0)), tile, tile, tile], out_specs=(tile,) * 4,
        compiler_params=_params(("parallel",)),
    )(parts, w, m, v)


REPLICATED = (("ln_z_g", "ln_z", 0), ("ln_z_b", "ln_z", 1), ("w_s", "w_s", None), ("b_s", "b_s", None),
              ("ln1_g", "ln1", 0), ("ln1_b", "ln1", 1), ("conv_w", "conv_mine", None), ("conv_b", "conv", 3),
              ("ln2_g", "tail", 3), ("ln2_b", "tail", 4), ("b_ple_gate", "tail", 0), ("ln3_g", "tail", 1),
              ("ln3_b", "tail", 2))
GATHERED = ("tail", "ln1", "ln_z", "conv", "w_s", "b_s", "loss", "conv_mine")


def _adamw_replicated(gathered, w, m, v):
    n_par = len(REPLICATED)

    def body(*refs):
        srcs = dict(zip(GATHERED, refs[:len(GATHERED)]))
        rest = refs[len(GATHERED):]
        w_refs, m_refs, v_refs = rest[:n_par], rest[n_par:2 * n_par], rest[2 * n_par:3 * n_par]
        outs = rest[3 * n_par:]
        loss_ref = outs[4 * n_par]
        sums = {}
        for key, ref in srcs.items():
            total = ref[0]
            for dev in range(1, N_DEV):
                total = total + ref[dev]
            sums[key] = total
        loss_ref[...] = sums["loss"]
        for n, (name, key, row) in enumerate(REPLICATED):
            if name == "conv_w":
                g = sums[key][0:3, :]
            elif row is None:
                g = sums[key]
            else:
                g = sums[key][row:row + 1, :]
            lead = len(w_refs[n].shape) - g.ndim
            idx = (0,) * lead + (Ellipsis,)
            delta, nm, nv = _adamw(g, w_refs[n][idx], m_refs[n][idx], v_refs[n][idx])
            for kind, val in enumerate((g, delta, nm, nv)):
                outs[kind * n_par + n][idx] = val

    names = [name for name, _, _ in REPLICATED]
    shapes = [jax.ShapeDtypeStruct(w[name].shape, F32) for name in names]
    args = [gathered[k] for k in GATHERED] + [w[n] for n in names] + [m[n] for n in names] + [v[n] for n in names]
    out_shape = tuple(shapes * 4) + (jax.ShapeDtypeStruct((8, LANES), F32),)
    return pl.pallas_call(
        body, name="adamw_replicated", grid=(1,), out_shape=out_shape,
        in_specs=[_full_spec(a.shape) for a in args], out_specs=tuple(_full_spec(s.shape) for s in out_shape),
        compiler_params=_params(("arbitrary",)),
    )(*args)


def kernel(x, p, positions, w_in, ln_z_g, ln_z_b, w_s, b_s, w_o, ln1_g, ln1_b, w_ff_a, w_ff_b, conv_w, conv_b, w_ff_down, ln2_g, ln2_b, w_ple_gate, b_ple_gate, w_ple_in, ln3_g, ln3_b, loss_target, m_w_in, m_ln_z_g, m_ln_z_b, m_w_s, m_b_s, m_w_o, m_ln1_g, m_ln1_b, m_w_ff_a, m_w_ff_b, m_conv_w, m_conv_b, m_w_ff_down, m_ln2_g, m_ln2_b, m_w_ple_gate, m_b_ple_gate, m_w_ple_in, m_ln3_g, m_ln3_b, v_w_in, v_ln_z_g, v_ln_z_b, v_w_s, v_b_s, v_w_o, v_ln1_g, v_ln1_b, v_w_ff_a, v_w_ff_b, v_conv_w, v_conv_b, v_w_ff_down, v_ln2_g, v_ln2_b, v_w_ple_gate, v_b_ple_gate, v_w_ple_in, v_ln3_g, v_ln3_b):
    w = dict(w_in=w_in, ln_z_g=ln_z_g, ln_z_b=ln_z_b, w_s=w_s, b_s=b_s, w_o=w_o, ln1_g=ln1_g, ln1_b=ln1_b,
             w_ff_a=w_ff_a, w_ff_b=w_ff_b, conv_w=conv_w, conv_b=conv_b, w_ff_down=w_ff_down, ln2_g=ln2_g,
             ln2_b=ln2_b, w_ple_gate=w_ple_gate, b_ple_gate=b_ple_gate, w_ple_in=w_ple_in, ln3_g=ln3_g,
             ln3_b=ln3_b)
    m = dict(w_in=m_w_in, ln_z_g=m_ln_z_g, ln_z_b=m_ln_z_b, w_s=m_w_s, b_s=m_b_s, w_o=m_w_o, ln1_g=m_ln1_g,
             ln1_b=m_ln1_b, w_ff_a=m_w_ff_a, w_ff_b=m_w_ff_b, conv_w=m_conv_w, conv_b=m_conv_b,
             w_ff_down=m_w_ff_down, ln2_g=m_ln2_g, ln2_b=m_ln2_b, w_ple_gate=m_w_ple_gate,
             b_ple_gate=m_b_ple_gate, w_ple_in=m_w_ple_in, ln3_g=m_ln3_g, ln3_b=m_ln3_b)
    v = dict(w_in=v_w_in, ln_z_g=v_ln_z_g, ln_z_b=v_ln_z_b, w_s=v_w_s, b_s=v_b_s, w_o=v_w_o, ln1_g=v_ln1_g,
             ln1_b=v_ln1_b, w_ff_a=v_w_ff_a, w_ff_b=v_w_ff_b, conv_w=v_conv_w, conv_b=v_conv_b,
             w_ff_down=v_w_ff_down, ln2_g=v_ln2_g, ln2_b=v_ln2_b, w_ple_gate=v_w_ple_gate,
             b_ple_gate=v_b_ple_gate, w_ple_in=v_w_ple_in, ln3_g=v_ln3_g, ln3_b=v_ln3_b)
    big_names = [name for name, _, _ in BIG]
    small_names = ("ln_z_g", "ln_z_b", "w_s", "b_s", "ln1_g", "ln1_b", "conv_b", "ln2_g", "ln2_b", "b_ple_gate",
                   "ln3_g", "ln3_b")

    transposed = {name for name, _, mode in BIG if mode == "rows_t"}

    def travel(a, name):
        return jnp.swapaxes(a, 1, 2) if name in transposed else a

    shards = dict(zip(big_names, _cast_shards([travel(w[n], n)[0] for n in big_names])))
    conv_rows = jnp.pad(w["conv_w"][0], ((0, 5), (0, 0)))
    sm = {n: w[n][0] if w[n].ndim > 2 else w[n] for n in small_names}
    pos_col = positions.reshape(S, 1).astype(F32)
    grad_x, (parts, small_all) = _local_step(x[0], p[0, 0], pos_col, loss_target[0], sm,
                                             _Exchanges(shards, conv_rows))
    me = 4 * lax.axis_index("x") + 2 * lax.axis_index("y") + lax.axis_index("c")
    conv_cols = small_all["conv"].reshape(N_DEV, 8, N_DEV, D_FF // N_DEV)
    small_all["conv_mine"] = lax.dynamic_index_in_dim(conv_cols, me, axis=2, keepdims=False)

    leaves = {}
    for name in big_names:
        outs = _adamw_sharded(parts[name], travel(w[name], name), travel(m[name], name), travel(v[name], name),
                              "adamw_" + name)
        leaves[name] = tuple(travel(o, name) for o in outs)
    rep = _adamw_replicated(small_all, w, m, v)
    n_rep = len(REPLICATED)
    for n, (name, _, _) in enumerate(REPLICATED):
        leaves[name] = tuple(rep[kind * n_rep + n] for kind in range(4))
    loss = rep[4 * n_rep][0, 0]
    return (loss, grad_x[None], *[leaves[n][kind] for kind in range(4) for n in WEIGHT_ORDER])
```

```python
import math

import numpy as np
import jax
import jax.numpy as jnp
from jax import lax
from jax.experimental import pallas as pl
from jax.experimental.pallas import tpu as pltpu

F32 = jnp.float32
BF16 = jnp.bfloat16
MESH = pl.DeviceIdType.MESH

N_DEV = 8
S = 4096
D = 1024
D_HALF = 512
D_IN = 2560
D_FF = 2816
D_PLE = 256
CHUNK = 128
DILATIONS = ((1, 32), (4, 8), (16, 2))
ROPE_THETA = 500000.0
LN_EPS = 1e-5
ALPHA = 2.0 ** 0.25
NEG_INF = -1e30
INV_SQRT2 = 1.0 / math.sqrt(2.0)
INV_SQRT_2PI = 1.0 / math.sqrt(2.0 * math.pi)

ADAM_LR, ADAM_B1, ADAM_B2, ADAM_EPS, ADAM_WD, ADAM_STEP = 0.001, 0.9, 0.999, 1e-08, 0.01, 10

TM = 512
NT = S // TM
ATTN_UNROLL = 8
TN = 1408
NJ = D_FF // TN
LANES = 128
VMEM_MIB = 1024 * 1024

BIG = (("w_in", (320, 1024), "rows_t"), ("w_o", (128, 1024), "rows"), ("w_ff_a", (352, 1024), "rows_t"),
       ("w_ff_b", (352, 1024), "rows_t"), ("w_ff_down", (352, 1024), "rows"), ("w_ple_gate", (128, 1024), "rows"),
       ("w_ple_in", (256, 128), "cols"))
WEIGHT_ORDER = ("w_in", "ln_z_g", "ln_z_b", "w_s", "b_s", "w_o", "ln1_g", "ln1_b", "w_ff_a", "w_ff_b",
                "conv_w", "conv_b", "w_ff_down", "ln2_g", "ln2_b", "w_ple_gate", "b_ple_gate",
                "w_ple_in", "ln3_g", "ln3_b")


def _params(semantics=None, vmem_mib=48):
    return pltpu.CompilerParams(dimension_semantics=semantics, vmem_limit_bytes=vmem_mib * VMEM_MIB)


def _dot(a, b):
    return jnp.dot(a, b, preferred_element_type=F32)


def _dot_nt(a, b):
    return lax.dot_general(a, b, (((1,), (1,)), ((), ())), preferred_element_type=F32)


def _dot_tn(a, b):
    return lax.dot_general(a, b, (((0,), (0,)), ((), ())), preferred_element_type=F32)


def _gelu(x):
    return 0.5 * x * (1.0 + lax.erf(x * INV_SQRT2))


def _gelu_and_grad(x):
    cdf = 0.5 * (1.0 + lax.erf(x * INV_SQRT2))
    return x * cdf, cdf + x * (jnp.exp(-0.5 * x * x) * INV_SQRT_2PI)


def _ln_stats(y):
    mu = jnp.mean(y, axis=-1, keepdims=True)
    yc = y - mu
    var = jnp.mean(yc * yc, axis=-1, keepdims=True)
    rstd = lax.rsqrt(var + LN_EPS)
    return yc * rstd, rstd


def _ln_bwd(dxhat, xhat, rstd):
    m1 = jnp.mean(dxhat, axis=-1, keepdims=True)
    m2 = jnp.mean(dxhat * xhat, axis=-1, keepdims=True)
    return rstd * (dxhat - m1 - xhat * m2)


def _colsum(x):
    return jnp.sum(x, axis=0, keepdims=True)


def _rows(i):
    return (i, 0)


def _row_spec(width):
    return pl.BlockSpec((TM, width), _rows)


def _full_spec(shape):
    return pl.BlockSpec(shape, lambda *_: (0,) * len(shape))


def _owner_slot(j):
    return (j % 2) * 4 + j // 2


def _lane_lo():
    return lax.broadcasted_iota(jnp.int32, (CHUNK, LANES), 1) < 64


def _tril():
    r = lax.broadcasted_iota(jnp.int32, (CHUNK, CHUNK), 0)
    c = lax.broadcasted_iota(jnp.int32, (CHUNK, CHUNK), 1)
    return c <= r


def _rope_consts():
    lane = np.arange(LANES) % 64
    j = lane % 8
    inv = np.where(lane < 16, np.float32(ROPE_THETA) ** (-(2.0 * j).astype(np.float32) / np.float32(16.0)), 0.0)
    m_lo = (lane < 8).astype(np.float32)
    m_hi = ((lane >= 8) & (lane < 16)).astype(np.float32)
    return (jnp.asarray(inv, F32).reshape(1, LANES), jnp.asarray(m_lo).reshape(1, LANES),
            jnp.asarray(m_hi).reshape(1, LANES))


def _rope_tables(pos_col, carried=None):
    inv, m_lo, m_hi = _rope_consts()

    def body(pos_ref, inv_ref, lo_ref, hi_ref, c_ref, sa_ref, sb_ref):
        ang = pos_ref[...] * inv_ref[...]
        c = jnp.cos(ang)
        s = jnp.sin(ang)
        lo = lo_ref[...]
        hi = hi_ref[...]
        c_ref[...] = jnp.where(lo + hi > 0.0, c, 1.0)
        sa_ref[...] = s * hi
        sb_ref[...] = -s * lo

    vec = _full_spec((1, LANES))
    out = jax.ShapeDtypeStruct((S, LANES), F32)
    return _host_call(
        body, carried, name="rope_tables", grid=(NT,), out_shape=(out, out, out),
        in_specs=[pl.BlockSpec((TM, 1), _rows), vec, vec, vec],
        out_specs=(_row_spec(LANES),) * 3, scratch_shapes=[], params=_params(("arbitrary",)),
        args=(pos_col, inv, m_lo, m_hi))


def _rope(t, c, sa, sb):
    return t * c + pltpu.roll(t, 8, 1) * sa + pltpu.roll(t, LANES - 8, 1) * sb


def _rope_t(dy, c, sa, sb):
    return dy * c + pltpu.roll(dy * sa, LANES - 8, 1) + pltpu.roll(dy * sb, 8, 1)


def _masked_ws(ws_ref):
    tril = _tril()
    return [jnp.where(tril, ws_ref[g], 0.0).astype(BF16) for g in range(8)]


def _spatial_mix(zn, wm, bs):
    lo = _lane_lo()
    rows = []
    for ch in range(TM // CHUNK):
        slabs = []
        for pr in range(4):
            zp = zn[ch * CHUNK:(ch + 1) * CHUNK, pr * LANES:(pr + 1) * LANES].astype(BF16)
            slabs.append(jnp.where(lo, _dot(wm[2 * pr], zp), _dot(wm[2 * pr + 1], zp)))
        rows.append(jnp.concatenate(slabs, axis=1) + bs)
    return jnp.concatenate(rows, axis=0)


def _proj_in_fwd(x, w_in, tabs, ln_z_g, ln_z_b, w_s, bs_exp):
    def body(x_ref, w_ref, c_ref, sa_ref, sb_ref, g_ref, b_ref, ws_ref, bs_ref,
             q_ref, k_ref, v_ref, u_ref, z_ref, gm_ref, xb_ref):
        xb = x_ref[...].astype(BF16)
        xb_ref[...] = xb
        c, sa, sb = c_ref[...], sa_ref[...], sb_ref[...]
        hq = _dot_nt(xb, w_ref[0:512, :])
        hk = _dot_nt(xb, w_ref[512:1024, :])
        for s in range(4):
            sl = slice(s * LANES, (s + 1) * LANES)
            q_ref[:, sl] = _rope(hq[:, sl], c, sa, sb)
            k_ref[:, sl] = _rope(hk[:, sl], c, sa, sb)
        v_ref[...] = _dot_nt(xb, w_ref[1024:1536, :])
        u_pre = _dot_nt(xb, w_ref[1536:2048, :])
        z_pre = _dot_nt(xb, w_ref[2048:2560, :])
        u_ref[...] = u_pre
        z_ref[...] = z_pre
        zhat, _ = _ln_stats(_gelu(z_pre))
        zn = zhat * g_ref[...] + b_ref[...]
        mixed = _spatial_mix(zn, _masked_ws(ws_ref), bs_ref[...])
        gm_ref[...] = (_gelu(u_pre) * mixed).astype(BF16)

    half = jax.ShapeDtypeStruct((S, D_HALF), F32)
    tab = _row_spec(LANES)
    return pl.pallas_call(
        body, name="proj_in_fwd", grid=(NT,),
        out_shape=(half, half, half, half, half, jax.ShapeDtypeStruct((S, D_HALF), BF16),
                   jax.ShapeDtypeStruct((S, D), BF16)),
        in_specs=[_row_spec(D), _full_spec((D_IN, D)), tab, tab, tab, _full_spec((1, D_HALF)),
                  _full_spec((1, D_HALF)), _full_spec((8, CHUNK, CHUNK)), _full_spec((CHUNK, D_HALF))],
        out_specs=(_row_spec(D_HALF),) * 6 + (_row_spec(D),),
        compiler_params=_params(("parallel",)),
    )(x, w_in, *tabs, ln_z_g, ln_z_b, w_s, bs_exp)


def _store_band_bias(bias_ref):
    qi = lax.broadcasted_iota(jnp.int32, (CHUNK, 2 * CHUNK), 0)
    kj = lax.broadcasted_iota(jnp.int32, (CHUNK, 2 * CHUNK), 1)
    band = (kj >= qi) & (kj <= qi + CHUNK)
    bias_ref[0] = jnp.where(band, 0.0, NEG_INF)
    bias_ref[1] = jnp.where(band & (kj >= CHUNK), 0.0, NEG_INF)


def _permuted_rows(ref, d, r):
    return ref[...] if d == 1 else ref[pl.ds(r, S // d, stride=d), :]


def _attention_fwd(q, k, v, carried=None):
    def body(q_ref, k_ref, v_ref, o_ref, lse_ref, qb, kb, v0b, v1b, bias, op, lp, ob0, lb0, ob1, lb1, ob2, lb2):
        lo = _lane_lo()
        lo_f = lo.astype(F32)[0:1, :]
        hi_f = 1.0 - lo_f
        zero_pad = jnp.zeros((CHUNK, LANES), BF16)
        for buf in (qb, kb, v0b, v1b):
            buf[0:CHUNK, :] = zero_pad
        _store_band_bias(bias)
        outs = ((ob0, lb0), (ob1, lb1), (ob2, lb2))
        for (d, nb), (ob, lb) in zip(DILATIONS, outs):
            length = S // d
            for r in range(d):
                dst = slice(CHUNK + r * length, CHUNK + (r + 1) * length)
                qb[dst, :] = (_permuted_rows(q_ref, d, r) * 0.125).astype(BF16)
                kb[dst, :] = _permuted_rows(k_ref, d, r).astype(BF16)
                vs = _permuted_rows(v_ref, d, r)
                v0b[dst, :] = (vs * lo_f + hi_f).astype(BF16)
                v1b[dst, :] = (vs * hi_f + lo_f).astype(BF16)

            def block(b, carry, nb=nb):
                base = pl.multiple_of(b * CHUNK, CHUNK)
                add = bias[jnp.where(b % nb == 0, 1, 0)]
                qblk = qb[pl.ds(pl.multiple_of(base + CHUNK, CHUNK), CHUNK), :]
                kblk = kb[pl.ds(base, 2 * CHUNK), :]
                q2 = jnp.concatenate([jnp.where(lo, qblk, 0), jnp.where(lo, 0, qblk)], axis=0)
                s2 = _dot_nt(q2, kblk) + jnp.concatenate([add, add], axis=0)
                m2 = jnp.max(s2, axis=-1, keepdims=True)
                p2 = jnp.exp(s2 - m2).astype(BF16)
                pv, mx = [], []
                for head, vh in enumerate((v0b, v1b)):
                    rows = slice(head * CHUNK, (head + 1) * CHUNK)
                    pv.append(_dot(p2[rows, :], vh[pl.ds(base, 2 * CHUNK), :]))
                    mx.append(m2[rows, :])
                den = pltpu.roll(jnp.where(lo, pv[1], pv[0]), 64, 1)
                op[pl.ds(base, CHUNK), :] = jnp.where(lo, pv[0], pv[1]) / den
                lp[pl.ds(base, CHUNK), :] = jnp.where(lo, mx[0], mx[1]) + jnp.log(den)
                return carry

            lax.fori_loop(0, S // CHUNK, block, 0, unroll=ATTN_UNROLL)
            for r in range(d):
                src = slice(r * length, (r + 1) * length)
                if d == 1:
                    ob[...] = op[...]
                    lb[...] = lp[...]
                else:
                    ob[pl.ds(r, length, stride=d), :] = op[src, :]
                    lb[pl.ds(r, length, stride=d), :] = lp[src, :]
        for t in range(NT):
            rows = slice(t * TM, (t + 1) * TM)
            l0, l1, l2 = lb0[rows, :], lb1[rows, :], lb2[rows, :]
            mx = jnp.maximum(jnp.maximum(l0, l1), l2)
            e0, e1, e2 = jnp.exp(l0 - mx), jnp.exp(l1 - mx), jnp.exp(l2 - mx)
            den = e0 + e1 + e2
            o_ref[rows, :] = (e0 * ob0[rows, :] + e1 * ob1[rows, :] + e2 * ob2[rows, :]) / den
            lse_ref[rows, :] = mx + jnp.log(den)

    slab = pl.BlockSpec((S, LANES), lambda h: (0, h))
    out = jax.ShapeDtypeStruct((S, D_HALF), F32)
    padded = pltpu.VMEM((CHUNK + S, LANES), BF16)
    whole = pltpu.VMEM((S, LANES), F32)
    return _host_call(
        body, carried, name="attention_fwd", grid=(4,), out_shape=(out, out),
        in_specs=[slab, slab, slab], out_specs=(slab, slab),
        scratch_shapes=[padded] * 4 + [pltpu.VMEM((2, CHUNK, 2 * CHUNK), F32)] + [whole] * 8,
        params=_params(("arbitrary",), 56), args=(q, k, v))


def _mix_ln1_fwd(attn, gm, w_o, x, g1, b1):
    def body(a_ref, gm_ref, w_ref, x_ref, g_ref, b_ref, xhat_ref, rstd_ref, x1b_ref):
        mix = _dot(a_ref[...].astype(BF16), w_ref[0:D_HALF, :]) + _dot(gm_ref[...], w_ref[D_HALF:D, :])
        xhat, rstd = _ln_stats(ALPHA * x_ref[...] + mix)
        xhat_ref[...] = xhat
        rstd_ref[...] = rstd
        x1b_ref[...] = (xhat * g_ref[...] + b_ref[...]).astype(BF16)

    vec = _full_spec((1, D))
    return pl.pallas_call(
        body, name="mix_ln1_fwd", grid=(NT,),
        out_shape=(jax.ShapeDtypeStruct((S, D), F32), jax.ShapeDtypeStruct((S, 1), F32),
                   jax.ShapeDtypeStruct((S, D), BF16)),
        in_specs=[_row_spec(D_HALF), _row_spec(D_HALF), _full_spec((D, D)), _row_spec(D), vec, vec],
        out_specs=(_row_spec(D), pl.BlockSpec((TM, 1), _rows), _row_spec(D)),
        compiler_params=_params(("parallel",)),
    )(attn, gm, w_o, x, g1, b1)


def _ffn_up_fwd(x1b, w_a, w_b, conv_w8, conv_b, carried=None):
    def body(x_ref, wa_ref, wb_ref, cw_ref, cb_ref, ap_ref, a_ref, bl_ref, h_ref, carry):
        @pl.when(pl.program_id(1) == 0)
        def _():
            carry[...] = jnp.zeros_like(carry)

        xb = x_ref[...]
        ap = _dot_nt(xb, wa_ref[...])
        bl = _dot_nt(xb, wb_ref[...])
        row = lax.broadcasted_iota(jnp.int32, (TM, TN), 0)
        c6, c7 = carry[6:7, :], carry[7:8, :]
        m1 = jnp.where(row == 0, c7, pltpu.roll(ap, 1, 0))
        m2 = jnp.where(row == 0, c6, jnp.where(row == 1, c7, pltpu.roll(ap, 2, 0)))
        a = cb_ref[...] + cw_ref[0:1, :] * m2 + cw_ref[1:2, :] * m1 + cw_ref[2:3, :] * ap
        carry[...] = ap[TM - 8:TM, :]
        ap_ref[...] = ap.astype(BF16)
        a_ref[...] = a
        bl_ref[...] = bl.astype(BF16)
        h_ref[...] = (_gelu(a) * bl).astype(BF16)

    tile = pl.BlockSpec((TM, TN), lambda j, i: (i, j))
    wcol = pl.BlockSpec((TN, D), lambda j, i: (j, 0))
    ff = jax.ShapeDtypeStruct((S, D_FF), F32)
    ffb = jax.ShapeDtypeStruct((S, D_FF), BF16)
    return _host_call(
        body, carried, name="ffn_up_fwd", grid=(NJ, NT),
        out_shape=(ffb, ff, ffb, ffb),
        in_specs=[pl.BlockSpec((TM, D), lambda j, i: (i, 0)), wcol, wcol,
                  pl.BlockSpec((8, TN), lambda j, i: (0, j)), pl.BlockSpec((1, TN), lambda j, i: (0, j))],
        out_specs=(tile, tile, tile, tile),
        scratch_shapes=[pltpu.VMEM((8, TN), F32)],
        params=_params(("arbitrary", "arbitrary"), 56), args=(x1b, w_a, w_b, conv_w8, conv_b))


def _ffn_down_ln2_fwd(hff, w_down, xhat1, g1, b1):
    def body(h_ref, w_ref, xh_ref, g_ref, b_ref, xhat_ref, rstd_ref):
        x1 = xh_ref[...] * g_ref[...] + b_ref[...]
        xhat, rstd = _ln_stats(ALPHA * x1 + _dot(h_ref[...], w_ref[...]))
        xhat_ref[...] = xhat
        rstd_ref[...] = rstd

    vec = _full_spec((1, D))
    return pl.pallas_call(
        body, name="ffn_down_ln2_fwd", grid=(NT,),
        out_shape=(jax.ShapeDtypeStruct((S, D), F32), jax.ShapeDtypeStruct((S, 1), F32)),
        in_specs=[_row_spec(D_FF), _full_spec((D_FF, D)), _row_spec(D), vec, vec],
        out_specs=(_row_spec(D), pl.BlockSpec((TM, 1), _rows)),
        compiler_params=_params(("parallel",)),
    )(hff, w_down, xhat1, g1, b1)


def _tail_fwd_bwd(xhat2, rstd2, p, target, w_g, w_p, g2, b2, bg, g3, b3):
    def body(xh_ref, rs_ref, p_ref, t_ref, wg_ref, wp_ref, g2_ref, b2_ref, bg_ref, g3_ref, b3_ref,
             loss_ref, dy2_ref, dy2b_ref, gwg_ref, gwp_ref, vec_ref, dwg_ref, dwp_ref):
        @pl.when(pl.program_id(0) == 0)
        def _():
            loss_ref[...] = jnp.zeros_like(loss_ref)
            dwg_ref[...] = jnp.zeros_like(dwg_ref)
            dwp_ref[...] = jnp.zeros_like(dwp_ref)
            vec_ref[...] = jnp.zeros_like(vec_ref)

        xhat2_t = xh_ref[...]
        x2 = xhat2_t * g2_ref[...] + b2_ref[...]
        x2b = x2.astype(BF16)
        pb = p_ref[...].astype(BF16)
        gate = jax.nn.sigmoid(_dot(x2b, wg_ref[...]) + bg_ref[...])
        pin = _dot(pb, wp_ref[...])
        xhat3, rstd3 = _ln_stats(ALPHA * x2 + gate * pin)
        err = xhat3 * g3_ref[...] + b3_ref[...] - t_ref[...]
        loss_ref[...] += jnp.sum(jnp.mean(err * err, axis=-1, keepdims=True), axis=0, keepdims=True) * 0.5
        dout = err * (1.0 / D)
        dy3 = _ln_bwd(dout * g3_ref[...], xhat3, rstd3)
        dgp = dy3 * pin * gate * (1.0 - gate)
        dgpb = dgp.astype(BF16)
        dwg_ref[...] += _dot_tn(x2b, dgpb)
        dwp_ref[...] += _dot_tn(pb, (dy3 * gate).astype(BF16))
        dx2 = ALPHA * dy3 + _dot_nt(dgpb, wg_ref[...])
        dy2 = _ln_bwd(dx2 * g2_ref[...], xhat2_t, rs_ref[...])
        dy2_ref[...] = dy2
        dy2b_ref[...] = dy2.astype(BF16)
        vec_ref[0:1, :] += _colsum(dgp)
        vec_ref[1:2, :] += _colsum(dout * xhat3)
        vec_ref[2:3, :] += _colsum(dout)
        vec_ref[3:4, :] += _colsum(dx2 * xhat2_t)
        vec_ref[4:5, :] += _colsum(dx2)

        @pl.when(pl.program_id(0) == NT - 1)
        def _():
            for j in range(N_DEV):
                gwg_ref[_owner_slot(j)] = dwg_ref[LANES * j:LANES * (j + 1), :].astype(BF16)
                gwp_ref[_owner_slot(j)] = dwp_ref[:, LANES * j:LANES * (j + 1)].astype(BF16)

    vec = _full_spec((1, D))
    return pl.pallas_call(
        body, name="tail_fwd_bwd", grid=(NT,),
        out_shape=(jax.ShapeDtypeStruct((8, LANES), F32), jax.ShapeDtypeStruct((S, D), F32),
                   jax.ShapeDtypeStruct((S, D), BF16), jax.ShapeDtypeStruct((N_DEV, D // N_DEV, D), BF16),
                   jax.ShapeDtypeStruct((N_DEV, D_PLE, D // N_DEV), BF16), jax.ShapeDtypeStruct((8, D), F32)),
        in_specs=[_row_spec(D), pl.BlockSpec((TM, 1), _rows), _row_spec(D_PLE), _row_spec(D),
                  _full_spec((D, D)), _full_spec((D_PLE, D)), vec, vec, vec, vec, vec],
        out_specs=(_full_spec((8, LANES)), _row_spec(D), _row_spec(D), _full_spec((N_DEV, D // N_DEV, D)),
                   _full_spec((N_DEV, D_PLE, D // N_DEV)), _full_spec((8, D))),
        scratch_shapes=[pltpu.VMEM((D, D), F32), pltpu.VMEM((D_PLE, D), F32)],
        compiler_params=_params(("arbitrary",)),
    )(xhat2, rstd2, p, target, w_g, w_p, g2, b2, bg, g3, b3)


def _ffn_bwd_act(dy2b, w_down, a_pre, a, b_lin, hff, conv_w8):
    def body(dy_ref, wd_ref, ap_ref, a_ref, bl_ref, h_ref, cw_ref, dap_ref, dbl_ref, dwd_ref, dcw_ref, carry):
        @pl.when(pl.program_id(1) == 0)
        def _():
            carry[...] = jnp.zeros_like(carry)
            dwd_ref[...] = jnp.zeros_like(dwd_ref)
            dcw_ref[...] = jnp.zeros_like(dcw_ref)

        dyb = dy_ref[...]
        dh = _dot_nt(dyb, wd_ref[...])
        av = a_ref[...]
        cdf = 0.5 * (1.0 + lax.erf(av * INV_SQRT2))
        dbl_ref[...] = (dh * (av * cdf)).astype(BF16)
        da = dh * bl_ref[...].astype(F32) * (cdf + av * (jnp.exp(-0.5 * av * av) * INV_SQRT_2PI))
        row = lax.broadcasted_iota(jnp.int32, (TM, TN), 0)
        c0, c1 = carry[0:1, :], carry[1:2, :]
        p1 = jnp.where(row == TM - 1, c0, pltpu.roll(da, TM - 1, 0))
        p2 = jnp.where(row == TM - 2, c0, jnp.where(row == TM - 1, c1, pltpu.roll(da, TM - 2, 0)))
        carry[...] = da[0:8, :]
        ap = ap_ref[...].astype(F32)
        dcw_ref[3:4, :] += _colsum(da)
        dcw_ref[0:1, :] += _colsum(ap * p2)
        dcw_ref[1:2, :] += _colsum(ap * p1)
        dcw_ref[2:3, :] += _colsum(ap * da)
        dap_ref[...] = (cw_ref[2:3, :] * da + cw_ref[1:2, :] * p1 + cw_ref[0:1, :] * p2).astype(BF16)
        dwd_ref[...] += _dot_tn(h_ref[...], dyb)

    rev_tile = pl.BlockSpec((TM, TN), lambda j, i: (NT - 1 - i, j))
    wrows = pl.BlockSpec((TN, D), lambda j, i: (j, 0))
    small = pl.BlockSpec((8, TN), lambda j, i: (0, j))
    ffb = jax.ShapeDtypeStruct((S, D_FF), BF16)
    return pl.pallas_call(
        body, name="ffn_bwd_act", grid=(NJ, NT),
        out_shape=(ffb, ffb, jax.ShapeDtypeStruct((D_FF, D), F32), jax.ShapeDtypeStruct((8, D_FF), F32)),
        in_specs=[pl.BlockSpec((TM, D), lambda j, i: (NT - 1 - i, 0)), wrows, rev_tile, rev_tile, rev_tile,
                  rev_tile, small],
        out_specs=(rev_tile, rev_tile, wrows, small),
        scratch_shapes=[pltpu.VMEM((8, TN), F32)],
        compiler_params=_params(("arbitrary", "arbitrary"), 56),
    )(dy2b, w_down, a_pre, a, b_lin, hff, conv_w8)


def _ffn_bwd_w(dap, dbl, x1b):
    def body(dap_ref, dbl_ref, x_ref, dwa_ref, dwb_ref):
        @pl.when(pl.program_id(1) == 0)
        def _():
            dwa_ref[...] = jnp.zeros_like(dwa_ref)
            dwb_ref[...] = jnp.zeros_like(dwb_ref)

        xb = x_ref[...]
        dwa_ref[...] += _dot_tn(dap_ref[...], xb)
        dwb_ref[...] += _dot_tn(dbl_ref[...], xb)

    tile = pl.BlockSpec((TM, TN), lambda j, i: (i, j))
    wrows = pl.BlockSpec((TN, D), lambda j, i: (j, 0))
    full = jax.ShapeDtypeStruct((D_FF, D), F32)
    return pl.pallas_call(
        body, name="ffn_bwd_w", grid=(NJ, NT), out_shape=(full, full),
        in_specs=[tile, tile, pl.BlockSpec((TM, D), lambda j, i: (i, 0))], out_specs=(wrows, wrows),
        compiler_params=_params(("arbitrary", "arbitrary"), 56),
    )(dap, dbl, x1b)


def _ffn_bwd_x(dap, dbl, w_a, w_b):
    def body(dap_ref, dbl_ref, wa_ref, wb_ref, dx_ref):
        dx_ref[...] = _dot(dap_ref[...], wa_ref[...]) + _dot(dbl_ref[...], wb_ref[...])

    return pl.pallas_call(
        body, name="ffn_bwd_x", grid=(NT,), out_shape=jax.ShapeDtypeStruct((S, D), F32),
        in_specs=[_row_spec(D_FF), _row_spec(D_FF), _full_spec((D_FF, D)), _full_spec((D_FF, D))],
        out_specs=_row_spec(D), compiler_params=_params(("parallel",), 56),
    )(dap, dbl, w_a, w_b)


def _ln1_mix_bwd(dy2, dx1_ffn, xhat1, rstd1, g1, attn, gm, w_o, carried=None):
    def body(dy2_ref, dxf_ref, xh_ref, rs_ref, g_ref, a_ref, gm_ref, w_ref,
             dy1_ref, da_ref, dlt_ref, dgm_ref, gwo_ref, vec_ref, dwo_ref):
        @pl.when(pl.program_id(0) == 0)
        def _():
            dwo_ref[...] = jnp.zeros_like(dwo_ref)
            vec_ref[...] = jnp.zeros_like(vec_ref)

        xhat = xh_ref[...]
        dx1 = ALPHA * dy2_ref[...] + dxf_ref[...]
        vec_ref[0:1, :] += _colsum(dx1 * xhat)
        vec_ref[1:2, :] += _colsum(dx1)
        dy1 = _ln_bwd(dx1 * g_ref[...], xhat, rs_ref[...])
        dy1_ref[...] = dy1
        dy1b = dy1.astype(BF16)
        dmix = _dot_nt(dy1b, w_ref[...])
        attn_t = a_ref[...]
        d_attn = dmix[:, 0:D_HALF]
        da_ref[...] = d_attn
        dgm_ref[...] = dmix[:, D_HALF:D]
        lo = (lax.broadcasted_iota(jnp.int32, (TM, LANES), 1) < 64)
        for s in range(4):
            sl = slice(s * LANES, (s + 1) * LANES)
            prod = d_attn[:, sl] * attn_t[:, sl]
            s0 = jnp.sum(jnp.where(lo, prod, 0.0), axis=-1, keepdims=True)
            s1 = jnp.sum(jnp.where(lo, 0.0, prod), axis=-1, keepdims=True)
            dlt_ref[:, sl] = jnp.where(lo, s0, s1)
        dwo_ref[0:D_HALF, :] += _dot_tn(attn_t.astype(BF16), dy1b)
        dwo_ref[D_HALF:D, :] += _dot_tn(gm_ref[...], dy1b)

        @pl.when(pl.program_id(0) == NT - 1)
        def _():
            for j in range(N_DEV):
                gwo_ref[_owner_slot(j)] = dwo_ref[LANES * j:LANES * (j + 1), :].astype(BF16)

    half = jax.ShapeDtypeStruct((S, D_HALF), F32)
    return _host_call(
        body, carried, name="ln1_mix_bwd", grid=(NT,),
        out_shape=(jax.ShapeDtypeStruct((S, D), F32), half, half, half,
                   jax.ShapeDtypeStruct((N_DEV, D // N_DEV, D), BF16), jax.ShapeDtypeStruct((8, D), F32)),
        in_specs=[_row_spec(D), _row_spec(D), _row_spec(D), pl.BlockSpec((TM, 1), _rows), _full_spec((1, D)),
                  _row_spec(D_HALF), _row_spec(D_HALF), _full_spec((D, D))],
        out_specs=(_row_spec(D), _row_spec(D_HALF), _row_spec(D_HALF), _row_spec(D_HALF),
                   _full_spec((N_DEV, D // N_DEV, D)), _full_spec((8, D))),
        scratch_shapes=[pltpu.VMEM((D, D), F32)],
        params=_params(("arbitrary",)), args=(dy2, dx1_ffn, xhat1, rstd1, g1, attn, gm, w_o))


def _gmlp_bwd(d_gm, u_pre, z_pre, ln_z_g, ln_z_b, w_s, bs_exp, carried=None):
    def body(dg_ref, u_ref, z_ref, g_ref, b_ref, ws_ref, bs_ref, du_ref, dz_ref, dws_ref, dbs_ref, vec_ref):
        @pl.when(pl.program_id(0) == 0)
        def _():
            dws_ref[...] = jnp.zeros_like(dws_ref)
            dbs_ref[...] = jnp.zeros_like(dbs_ref)
            vec_ref[...] = jnp.zeros_like(vec_ref)

        u_pre_t, z_pre_t, dgm = u_ref[...], z_ref[...], dg_ref[...]
        z_act, z_slope = _gelu_and_grad(z_pre_t)
        u_act, u_slope = _gelu_and_grad(u_pre_t)
        zhat, rstd = _ln_stats(z_act)
        zn = zhat * g_ref[...] + b_ref[...]
        wm = _masked_ws(ws_ref)
        mixed = _spatial_mix(zn, wm, bs_ref[...])
        du_ref[...] = (dgm * mixed * u_slope).astype(BF16)
        dmixed = dgm * u_act
        lo = _lane_lo()
        tril = _tril()
        group_of_lane = lax.broadcasted_iota(jnp.int32, (8, D_HALF), 1) // 64
        pick = (group_of_lane == lax.broadcasted_iota(jnp.int32, (8, D_HALF), 0)).astype(F32)
        dzn_rows = []
        for ch in range(TM // CHUNK):
            rows = slice(ch * CHUNK, (ch + 1) * CHUNK)
            dbs_ref[...] += lax.dot_general(pick, dmixed[rows, :], (((1,), (1,)), ((), ())),
                                            precision=lax.Precision.HIGHEST, preferred_element_type=F32)
            slabs = []
            for pr in range(4):
                sl = slice(pr * LANES, (pr + 1) * LANES)
                dm = dmixed[rows, sl]
                zp = zn[rows, sl].astype(BF16)
                dm_lo = jnp.where(lo, dm, 0.0).astype(BF16)
                dm_hi = jnp.where(lo, 0.0, dm).astype(BF16)
                dws_ref[2 * pr] += jnp.where(tril, _dot_nt(dm_lo, zp), 0.0)
                dws_ref[2 * pr + 1] += jnp.where(tril, _dot_nt(dm_hi, zp), 0.0)
                dmb = dm.astype(BF16)
                slabs.append(jnp.where(lo, _dot_tn(wm[2 * pr], dmb), _dot_tn(wm[2 * pr + 1], dmb)))
            dzn_rows.append(jnp.concatenate(slabs, axis=1))
        dzn = jnp.concatenate(dzn_rows, axis=0)
        vec_ref[0:1, :] += _colsum(dzn * zhat)
        vec_ref[1:2, :] += _colsum(dzn)
        dz = _ln_bwd(dzn * g_ref[...], zhat, rstd)
        dz_ref[...] = (dz * z_slope).astype(BF16)

    halfb = jax.ShapeDtypeStruct((S, D_HALF), BF16)
    vec = _full_spec((1, D_HALF))
    return _host_call(
        body, carried, name="gmlp_bwd", grid=(NT,),
        out_shape=(halfb, halfb, jax.ShapeDtypeStruct((8, CHUNK, CHUNK), F32),
                   jax.ShapeDtypeStruct((8, CHUNK), F32), jax.ShapeDtypeStruct((8, D_HALF), F32)),
        in_specs=[_row_spec(D_HALF), _row_spec(D_HALF), _row_spec(D_HALF), vec, vec,
                  _full_spec((8, CHUNK, CHUNK)), _full_spec((CHUNK, D_HALF))],
        out_specs=(_row_spec(D_HALF), _row_spec(D_HALF), _full_spec((8, CHUNK, CHUNK)),
                   _full_spec((8, CHUNK)), _full_spec((8, D_HALF))),
        scratch_shapes=[], params=_params(("arbitrary",)), args=(d_gm, u_pre, z_pre, ln_z_g, ln_z_b, w_s, bs_exp))


def _attention_bwd(q, k, v, lse, d_attn, delta, tabs, carried=None):
    def body(q_ref, k_ref, v_ref, l_ref, do_ref, dl_ref, c_ref, sa_ref, sb_ref, dq_ref, dk_ref, dv_ref,
             qb, kb, vb, gb, bias, lsp, dlp, dqp, dk_own, dk_prev, dv_own, dv_prev, dqa, dka, dva):
        lo = _lane_lo()
        zero_pad = jnp.zeros((CHUNK, LANES), BF16)
        for buf in (qb, kb, vb, gb):
            buf[0:CHUNK, :] = zero_pad
        dk_prev[S:S + CHUNK, :] = jnp.zeros((CHUNK, LANES), F32)
        dv_prev[S:S + CHUNK, :] = jnp.zeros((CHUNK, LANES), F32)
        _store_band_bias(bias)
        for d, nb in DILATIONS:
            length = S // d
            for r in range(d):
                dst = slice(CHUNK + r * length, CHUNK + (r + 1) * length)
                src = slice(r * length, (r + 1) * length)
                qb[dst, :] = (_permuted_rows(q_ref, d, r) * 0.125).astype(BF16)
                kb[dst, :] = _permuted_rows(k_ref, d, r).astype(BF16)
                vb[dst, :] = _permuted_rows(v_ref, d, r).astype(BF16)
                gb[dst, :] = _permuted_rows(do_ref, d, r).astype(BF16)
                lsp[src, :] = _permuted_rows(l_ref, d, r)
                dlp[src, :] = _permuted_rows(dl_ref, d, r)

            def block(b, carry, nb=nb):
                base = pl.multiple_of(b * CHUNK, CHUNK)
                own = pl.multiple_of(base + CHUNK, CHUNK)
                add = bias[jnp.where(b % nb == 0, 1, 0)]
                qblk = qb[pl.ds(own, CHUNK), :]
                gblk = gb[pl.ds(own, CHUNK), :]
                kblk = kb[pl.ds(base, 2 * CHUNK), :]
                vblk = vb[pl.ds(base, 2 * CHUNK), :]
                lse_t = lsp[pl.ds(base, CHUNK), :]
                dlt_t = dlp[pl.ds(base, CHUNK), :]
                q2 = jnp.concatenate([jnp.where(lo, qblk, 0), jnp.where(lo, 0, qblk)], axis=0)
                g2 = jnp.concatenate([jnp.where(lo, gblk, 0), jnp.where(lo, 0, gblk)], axis=0)
                lse2 = jnp.concatenate([lse_t[:, 0:1], lse_t[:, 64:65]], axis=0)
                dlt2 = jnp.concatenate([dlt_t[:, 0:1], dlt_t[:, 64:65]], axis=0)
                add2 = jnp.concatenate([add, add], axis=0)
                p = jnp.exp(_dot_nt(q2, kblk) + add2 - lse2)
                ds = (p * (_dot_nt(g2, vblk) - dlt2)).astype(BF16)
                dv_blk = _dot_tn(p.astype(BF16), g2)
                dk_blk = _dot_tn(ds, q2)
                dq2 = _dot(ds, kblk)
                dqp[pl.ds(base, CHUNK), :] = jnp.where(lo, dq2[0:CHUNK, :], dq2[CHUNK:2 * CHUNK, :]) * 0.125
                dk_prev[pl.ds(base, CHUNK), :] = dk_blk[0:CHUNK, :]
                dk_own[pl.ds(own, CHUNK), :] = dk_blk[CHUNK:2 * CHUNK, :]
                dv_prev[pl.ds(base, CHUNK), :] = dv_blk[0:CHUNK, :]
                dv_own[pl.ds(own, CHUNK), :] = dv_blk[CHUNK:2 * CHUNK, :]
                return carry

            lax.fori_loop(0, S // CHUNK, block, 0, unroll=ATTN_UNROLL)
            for r in range(d):
                src = slice(r * length, (r + 1) * length)
                pad = slice(CHUNK + r * length, CHUNK + (r + 1) * length)
                if d == 1:
                    dqa[...] = dqp[...]
                    dka[...] = dk_own[pad, :] + dk_prev[pad, :]
                    dva[...] = dv_own[pad, :] + dv_prev[pad, :]
                else:
                    dst = pl.ds(r, length, stride=d)
                    dqa[dst, :] = dqa[dst, :] + dqp[src, :]
                    dka[dst, :] = dka[dst, :] + (dk_own[pad, :] + dk_prev[pad, :])
                    dva[dst, :] = dva[dst, :] + (dv_own[pad, :] + dv_prev[pad, :])
        for t in range(NT):
            rows = slice(t * TM, (t + 1) * TM)
            c, sa, sb = c_ref[rows, :], sa_ref[rows, :], sb_ref[rows, :]
            dq_ref[rows, :] = _rope_t(dqa[rows, :], c, sa, sb).astype(BF16)
            dk_ref[rows, :] = _rope_t(dka[rows, :], c, sa, sb).astype(BF16)
            dv_ref[rows, :] = dva[rows, :].astype(BF16)

    slab = pl.BlockSpec((S, LANES), lambda h: (0, h), pipeline_mode=pl.Buffered(1))
    tab = pl.BlockSpec((S, LANES), lambda h: (0, 0), pipeline_mode=pl.Buffered(1))
    out_slab = pl.BlockSpec((S, LANES), lambda h: (0, h))
    out = jax.ShapeDtypeStruct((S, D_HALF), BF16)
    padded_b = pltpu.VMEM((CHUNK + S, LANES), BF16)
    padded_f = pltpu.VMEM((CHUNK + S, LANES), F32)
    whole = pltpu.VMEM((S, LANES), F32)
    return _host_call(
        body, carried, name="attention_bwd", grid=(4,), out_shape=(out, out, out),
        in_specs=[slab] * 6 + [tab] * 3, out_specs=(out_slab,) * 3,
        scratch_shapes=[padded_b] * 4 + [pltpu.VMEM((2, CHUNK, 2 * CHUNK), F32)] + [whole] * 3
        + [padded_f] * 4 + [whole] * 3,
        params=_params(("arbitrary",), 60), args=(q, k, v, lse, d_attn, delta, *tabs))


def _proj_in_bwd_w(xb, parts, carried=None):
    def body(x_ref, p0, p1, p2, p3, p4, gw_ref, dw_ref):
        @pl.when(pl.program_id(0) == 0)
        def _():
            dw_ref[...] = jnp.zeros_like(dw_ref)

        xt = x_ref[...]
        for n, part in enumerate((p0, p1, p2, p3, p4)):
            dw_ref[n * D_HALF:(n + 1) * D_HALF, :] += _dot_tn(part[...], xt)

        @pl.when(pl.program_id(0) == NT - 1)
        def _():
            width = D_IN // N_DEV
            for j in range(N_DEV):
                gw_ref[_owner_slot(j)] = dw_ref[width * j:width * (j + 1), :].astype(BF16)

    return _host_call(
        body, carried, name="proj_in_bwd_w", grid=(NT,),
        out_shape=(jax.ShapeDtypeStruct((N_DEV, D_IN // N_DEV, D), BF16),),
        in_specs=[_row_spec(D)] + [_row_spec(D_HALF)] * 5, out_specs=(_full_spec((N_DEV, D_IN // N_DEV, D)),),
        scratch_shapes=[pltpu.VMEM((D_IN, D), F32)], params=_params(("arbitrary",)), args=(xb, *parts))


def _proj_in_bwd_x(dy1, parts, w_in, carried=None):
    def body(dy_ref, p0, p1, p2, p3, p4, w_ref, gx_ref):
        acc = ALPHA * dy_ref[...]
        for n, part in enumerate((p0, p1, p2, p3, p4)):
            acc += _dot(part[...], w_ref[n * D_HALF:(n + 1) * D_HALF, :])
        gx_ref[...] = acc

    return _host_call(
        body, carried, name="proj_in_bwd_x", grid=(NT,), out_shape=(jax.ShapeDtypeStruct((S, D), F32),),
        in_specs=[_row_spec(D)] + [_row_spec(D_HALF)] * 5 + [_full_spec((D_IN, D))], out_specs=(_row_spec(D),),
        scratch_shapes=[], params=_params(("arbitrary",)), args=(dy1, *parts, w_in))


def _to_natural(blocks, name):
    n, rows, w = blocks.shape
    tile = min(rows, 256)

    def body(i_ref, o_ref):
        o_ref[...] = jnp.concatenate([i_ref[j] for j in range(n)], axis=1)

    return pl.pallas_call(
        body, name=name, grid=(rows // tile,), out_shape=jax.ShapeDtypeStruct((rows, n * w), blocks.dtype),
        in_specs=[pl.BlockSpec((n, tile, w), lambda i: (0, i, 0))],
        out_specs=pl.BlockSpec((tile, n * w), lambda i: (i, 0)), compiler_params=_params(("parallel",)),
    )(blocks)


def _row_blocks(full, name):
    rows, cols = full.shape
    r = rows // N_DEV

    def body(i_ref, o_ref):
        o_ref[0] = i_ref[...].astype(BF16)

    return pl.pallas_call(
        body, name=name, grid=(N_DEV,), out_shape=jax.ShapeDtypeStruct((N_DEV, r, cols), BF16),
        in_specs=[pl.BlockSpec((r, cols), lambda s: ((s % 4) * 2 + s // 4, 0))],
        out_specs=pl.BlockSpec((1, r, cols), lambda s: (s, 0, 0)), compiler_params=_params(("parallel",)),
    )(full)


def _local_step(x, p, pos_col, target, sm, ex):
    bs_exp = jnp.repeat(sm["b_s"].T, 64, axis=1)
    tabs, got = _rope_tables(pos_col, ex.gather_input())
    w_in = ex.weight_input(got)
    q, k, v, u_pre, z_pre, gm, xb = _proj_in_fwd(x, w_in, tabs, sm["ln_z_g"], sm["ln_z_b"], sm["w_s"], bs_exp)
    (attn, lse), got = _attention_fwd(q, k, v, ex.gather_first())
    wa = ex.weights_first(got)
    xhat1, rstd1, x1b = _mix_ln1_fwd(attn, gm, wa["w_o"], x, sm["ln1_g"], sm["ln1_b"])
    (a_pre, a, b_lin, hff), got = _ffn_up_fwd(x1b, wa["w_ff_a"], wa["w_ff_b"], wa["conv_w8"], sm["conv_b"],
                                              ex.gather_second())
    wc = ex.weights_second(got)
    xhat2, rstd2 = _ffn_down_ln2_fwd(hff, wc["w_ff_down"], xhat1, sm["ln1_g"], sm["ln1_b"])
    loss, dy2, dy2b, dw_g, dw_p, vec_tail = _tail_fwd_bwd(
        xhat2, rstd2, p, target, wc["w_ple_gate"], wc["w_ple_in"], sm["ln2_g"], sm["ln2_b"],
        sm["b_ple_gate"], sm["ln3_g"], sm["ln3_b"])
    dap, dbl, dw_down, dconv = _ffn_bwd_act(dy2b, wc["w_ff_down"], a_pre, a, b_lin, hff, wa["conv_w8"])
    dw_a, dw_b = _ffn_bwd_w(dap, dbl, x1b)
    dx1_ffn = _ffn_bwd_x(dap, dbl, wa["w_ff_a"], wa["w_ff_b"])
    (dy1, d_attn, delta, d_gm, dw_o, vec_ln1), _ = _ln1_mix_bwd(
        dy2, dx1_ffn, xhat1, rstd1, sm["ln1_g"], attn, gm, wa["w_o"])
    early = {"w_ff_a": _row_blocks(dw_a, "ff_a_grad_blocks"), "w_ff_b": _row_blocks(dw_b, "ff_b_grad_blocks"),
             "w_ff_down": _row_blocks(dw_down, "ff_down_grad_blocks"), "w_ple_gate": dw_g, "w_ple_in": dw_p,
             "w_o": dw_o}
    (du, dz, dws, dbs, vec_z), got = _gmlp_bwd(d_gm, u_pre, z_pre, sm["ln_z_g"], sm["ln_z_b"], sm["w_s"], bs_exp,
                                               ex.to_sibling(early))
    chip_sums = ex.reduce_on_chip(got)
    small = {"tail": vec_tail, "ln1": vec_ln1, "ln_z": vec_z, "conv": dconv, "w_s": dws, "b_s": dbs, "loss": loss}
    (dq, dk, dv), got_early = _attention_bwd(q, k, v, lse, d_attn, delta, tabs, ex.between_chips(chip_sums))
    parts = (dq, dk, dv, du, dz)
    (dw_in,), got_small = _proj_in_bwd_w(xb, parts, ex.small_to_all(small))
    (grad_x,), got_late = _proj_in_bwd_x(dy1, parts, w_in, ex.last(dw_in))
    return grad_x, ex.collect(got_early, got_late, got_small)


def _mesh_pos():
    return lax.axis_index("x"), lax.axis_index("y"), lax.axis_index("c")


def _cast_shards(shards):
    n = len(shards)

    def body(*refs):
        for a in range(n):
            refs[n + a][...] = refs[a][...].astype(BF16)

    whole = [_full_spec(s.shape) for s in shards]
    return pl.pallas_call(
        body, name="cast_shards", grid=(1,), out_shape=tuple(jax.ShapeDtypeStruct(s.shape, BF16) for s in shards),
        in_specs=whole, out_specs=tuple(whole), compiler_params=_params(("arbitrary",)),
    )(*shards)


class _GatherComm:
    def __init__(self, shards):
        n = len(shards)
        self.inputs = list(shards)
        self.out_shapes = [jax.ShapeDtypeStruct((N_DEV,) + s.shape, s.dtype) for s in shards]
        self.scratch = [pltpu.SemaphoreType.DMA((7 * n,)), pltpu.SemaphoreType.DMA((7 * n,)),
                        pltpu.SemaphoreType.DMA((n,))]

    def phases(self, x_refs, out_refs, sems):
        send_sems, recv_sems, local_sems = sems
        n_arr = len(x_refs)

        def where():
            x, y, c = _mesh_pos()
            return (x, y, c), (x, y, 1 - c), [(1 - x, y), (x, 1 - y), (1 - x, 1 - y)]

        def copy(a, n, block, to, from_shard=False):
            dst = out_refs[a].at[4 * block[0] + 2 * block[1] + block[2]]
            return pltpu.make_async_remote_copy(
                src_ref=x_refs[a] if from_shard else dst, dst_ref=dst, send_sem=send_sems.at[7 * a + n],
                recv_sem=recv_sems.at[7 * a + n], device_id=to, device_id_type=MESH)

        def local(a):
            x, y, c = _mesh_pos()
            return pltpu.make_async_copy(x_refs[a], out_refs[a].at[4 * x + 2 * y + c], local_sems.at[a])

        def start():
            me, sibling, chips = where()
            for a in range(n_arr):
                local(a).start()
                copy(a, 0, me, sibling, from_shard=True).start()
                for n, chip in enumerate(chips):
                    copy(a, 1 + n, me, (*chip, me[2]), from_shard=True).start()

        def forward():
            me, sibling, chips = where()
            for n, chip in enumerate(chips):
                for a in range(n_arr):
                    copy(a, 1 + n, (*chip, me[2]), me).wait_recv()
                    copy(a, 4 + n, (*chip, me[2]), sibling).start()

        def finish():
            me, sibling, chips = where()
            for a in range(n_arr):
                copy(a, 0, sibling, me).wait_recv()
                copy(a, 0, me, sibling, from_shard=True).wait_send()
                for n, chip in enumerate(chips):
                    copy(a, 4 + n, (*chip, 1 - me[2]), me).wait_recv()
                    copy(a, 1 + n, me, (*chip, me[2]), from_shard=True).wait_send()
                    copy(a, 4 + n, (*chip, me[2]), sibling).wait_send()
                local(a).wait()

        return {"start": start, "forward": forward, "finish": finish}


class _SiblingComm:
    def __init__(self, big):
        n = len(big)
        self.inputs = list(big)
        self.out_shapes = [jax.ShapeDtypeStruct(b.shape[1:], b.dtype) for b in big]
        self.scratch = [pltpu.SemaphoreType.DMA((n,)), pltpu.SemaphoreType.DMA((n,))]

    def phases(self, src, dst, sems):
        send_sems, recv_sems = sems

        def copies():
            x, y, c = _mesh_pos()
            return [pltpu.make_async_remote_copy(
                src_ref=src[a].at[1 - c], dst_ref=dst[a], send_sem=send_sems.at[a], recv_sem=recv_sems.at[a],
                device_id=(x, y, 1 - c), device_id_type=MESH) for a in range(len(src))]

        def start():
            for cp in copies():
                cp.start()

        def finish():
            for cp in copies():
                cp.wait()

        return {"start": start, "finish": finish}


class _ChipComm:
    def __init__(self, sums):
        n = len(sums)
        self.inputs = list(sums)
        self.out_shapes = [jax.ShapeDtypeStruct(s.shape, s.dtype) for s in sums]
        self.scratch = [pltpu.SemaphoreType.DMA((3 * n,)), pltpu.SemaphoreType.DMA((3 * n,)),
                        pltpu.SemaphoreType.DMA((n,))]

    def phases(self, src, dst, sems):
        send_sems, recv_sems, local_sems = sems

        def copies():
            x, y, c = _mesh_pos()
            my_chip = 2 * x + y
            out = [pltpu.make_async_copy(src[a].at[my_chip], dst[a].at[my_chip], local_sems.at[a])
                   for a in range(len(src))]
            for n, (px, py) in enumerate([(1 - x, y), (x, 1 - y), (1 - x, 1 - y)]):
                for a in range(len(src)):
                    out.append(pltpu.make_async_remote_copy(
                        src_ref=src[a].at[2 * px + py], dst_ref=dst[a].at[my_chip],
                        send_sem=send_sems.at[3 * a + n], recv_sem=recv_sems.at[3 * a + n],
                        device_id=(px, py, c), device_id_type=MESH))
            return out

        def start():
            for cp in copies():
                cp.start()

        def finish():
            for cp in copies():
                cp.wait()

        return {"start": start, "finish": finish}


class _ScatterComm:
    def __init__(self, blocks, small):
        self.n_big, self.n_small = len(blocks), len(small)
        n = self.n_big + self.n_small
        self.inputs = list(blocks) + list(small)
        self.out_shapes = ([jax.ShapeDtypeStruct(b.shape, b.dtype) for b in blocks]
                           + [jax.ShapeDtypeStruct((N_DEV,) + s.shape, s.dtype) for s in small])
        self.scratch = [pltpu.SemaphoreType.DMA((7 * n,)), pltpu.SemaphoreType.DMA((7 * n,)),
                        pltpu.SemaphoreType.DMA((n,))]

    def phases(self, src, dst, sems):
        send_sems, recv_sems, local_sems = sems
        n_big, n_all = self.n_big, self.n_big + self.n_small

        def source(a, core, chip):
            return src[a].at[core * 4 + chip] if a < n_big else src[a]

        def copies():
            x, y, c = _mesh_pos()
            me = 4 * x + 2 * y + c
            out = [pltpu.make_async_copy(source(a, c, 2 * x + y), dst[a].at[me], local_sems.at[a])
                   for a in range(n_all)]
            for flip in range(1, N_DEV):
                px = 1 - x if flip & 4 else x
                py = 1 - y if flip & 2 else y
                pc = 1 - c if flip & 1 else c
                for a in range(n_all):
                    n = 7 * a + flip - 1
                    out.append(pltpu.make_async_remote_copy(
                        src_ref=source(a, pc, 2 * px + py), dst_ref=dst[a].at[me], send_sem=send_sems.at[n],
                        recv_sem=recv_sems.at[n], device_id=(px, py, pc), device_id_type=MESH))
            return out

        def start():
            for cp in copies():
                cp.start()

        def finish():
            for cp in copies():
                cp.wait()

        return {"start": start, "finish": finish}


def _host_call(body, carried, *, name, grid, out_shape, in_specs, out_specs, scratch_shapes, params, args):
    if carried is None:
        return pl.pallas_call(body, name=name, grid=grid, out_shape=tuple(out_shape), in_specs=list(in_specs),
                              out_specs=tuple(out_specs), scratch_shapes=list(scratch_shapes),
                              compiler_params=params)(*args), ()
    comm, when = carried
    n_in, n_out, n_scratch = len(in_specs), len(out_shape), len(scratch_shapes)
    k_in, k_out = len(comm.inputs), len(comm.out_shapes)

    def wrapped(*refs):
        bounds = np.cumsum([0, n_in, k_in, n_out, k_out, n_scratch])
        ins, c_in, outs, c_out, scr = (refs[bounds[i]:bounds[i + 1]] for i in range(5))
        phases = comm.phases(c_in, c_out, refs[bounds[5]:])
        for phase, cond in when("before"):
            pl.when(cond)(phases[phase])
        body(*ins, *outs, *scr)
        for phase, cond in when("after"):
            pl.when(cond)(phases[phase])

    anywhere = pl.BlockSpec(memory_space=pl.ANY)
    results = pl.pallas_call(
        wrapped, name=name, grid=grid, out_shape=tuple(out_shape) + tuple(comm.out_shapes),
        in_specs=list(in_specs) + [anywhere] * k_in, out_specs=tuple(out_specs) + (anywhere,) * k_out,
        scratch_shapes=list(scratch_shapes) + comm.scratch, compiler_params=params,
    )(*args, *comm.inputs)
    return results[:n_out], results[n_out:]


class _Exchanges:
    FIRST = ("w_o", "w_ff_a", "w_ff_b")
    SECOND = ("w_ff_down", "w_ple_gate", "w_ple_in")
    EARLY = ("w_ff_a", "w_ff_b", "w_ff_down", "w_ple_gate", "w_ple_in", "w_o")
    LATE = ("w_in",)

    def __init__(self, shards, conv_rows):
        self.shards, self.conv_rows = shards, conv_rows
        self.mode = {name: mode for name, _, mode in BIG}

    def _natural(self, name, blocks):
        n, r, c = blocks.shape
        return _to_natural(blocks, name + "_natural") if self.mode[name] == "cols" else blocks.reshape(n * r, c)

    def gather_input(self):
        def when(position):
            step = pl.program_id(0)
            if position == "before":
                return [("start", step == 0)]
            return [("forward", step == NT - 1), ("finish", step == NT - 1)]
        return _GatherComm([self.shards["w_in"]]), when

    def weight_input(self, got):
        return self._natural("w_in", got[0])

    def gather_first(self):
        comm = _GatherComm([self.shards[n] for n in self.FIRST] + [self.conv_rows])

        def when(position):
            step = pl.program_id(0)
            if position == "before":
                return [("start", step == 0), ("forward", step == 3)]
            return [("finish", step == 3)]
        return comm, when

    def weights_first(self, got):
        out = {name: self._natural(name, blocks) for name, blocks in zip(self.FIRST, got)}
        out["conv_w8"] = _to_natural(got[-1], "conv_w_natural")
        return out

    def gather_second(self):
        comm = _GatherComm([self.shards[n] for n in self.SECOND])

        def when(position):
            j, i = pl.program_id(0), pl.program_id(1)
            if position == "before":
                return [("start", (j == 0) & (i == 0)), ("forward", (j == NJ - 1) & (i == NT // 2))]
            return [("finish", (j == NJ - 1) & (i == NT - 1))]
        return comm, when

    def weights_second(self, got):
        return {name: self._natural(name, blocks) for name, blocks in zip(self.SECOND, got)}

    def to_sibling(self, early):
        self.by_core = [early[n].reshape((2, 4) + early[n].shape[1:]) for n in self.EARLY]
        return _SiblingComm(self.by_core), _first_and_last(NT)

    def reduce_on_chip(self, from_sibling):
        core = lax.axis_index("c").astype(jnp.int32).reshape(1)
        return _chip_reduce(self.by_core, from_sibling, core, "chip_reduce")

    def between_chips(self, chip_sums):
        return _ChipComm(chip_sums), _first_and_last(4)

    def small_to_all(self, small):
        self.small_keys = tuple(small)
        return _ScatterComm([], [small[k] for k in self.small_keys]), _first_and_last(NT)

    def last(self, dw_in):
        by_core = [dw_in.reshape((2, 4) + dw_in.shape[1:])]
        core = lax.axis_index("c").astype(jnp.int32).reshape(1)
        sums = _chip_reduce(by_core, _standalone(_SiblingComm(by_core), "w_in_grad_to_sibling"), core, "w_in_chip_reduce")
        return _ChipComm(sums), _first_and_last(NT)

    def collect(self, got_early, got_late, got_small):
        parts = dict(zip(self.EARLY, got_early))
        parts.update(zip(self.LATE, got_late))
        return parts, dict(zip(self.small_keys, got_small))


def _first_and_last(n_steps):
    def when(position):
        step = pl.program_id(0)
        return [("start", step == 0)] if position == "before" else [("finish", step == n_steps - 1)]
    return when


def _standalone(comm, name):
    n_in = len(comm.inputs)

    def body(*refs):
        phases = comm.phases(refs[:n_in], refs[n_in:n_in + len(comm.out_shapes)], refs[n_in + len(comm.out_shapes):])
        phases["start"]()
        phases["finish"]()

    anywhere = pl.BlockSpec(memory_space=pl.ANY)
    return pl.pallas_call(
        body, name=name, out_shape=tuple(comm.out_shapes), in_specs=[anywhere] * n_in,
        out_specs=(anywhere,) * len(comm.out_shapes), scratch_shapes=comm.scratch,
    )(*comm.inputs)


def _chip_reduce(big, from_sibling, core, name):
    n = len(big)

    def body(core_ref, *refs):
        for a in range(n):
            mine, theirs, out = refs[a], refs[n + a], refs[2 * n + a]
            out[0] = (mine[0, 0].astype(F32) + theirs[0].astype(F32)).astype(BF16)

    def block(shape):
        return pl.BlockSpec((1,) + shape, lambda ch, core_ref: (ch, 0, 0))

    grid_spec = pltpu.PrefetchScalarGridSpec(
        num_scalar_prefetch=1, grid=(4,),
        in_specs=[pl.BlockSpec((1, 1) + b.shape[2:], lambda ch, core_ref: (core_ref[0], ch, 0, 0)) for b in big]
        + [block(b.shape[2:]) for b in big],
        out_specs=[block(b.shape[2:]) for b in big])
    return pl.pallas_call(
        body, name=name, grid_spec=grid_spec,
        out_shape=tuple(jax.ShapeDtypeStruct(b.shape[1:], BF16) for b in big),
        compiler_params=_params(("parallel",)),
    )(core, *big, *from_sibling)


def _adamw(g, w, m, v):
    nm = ADAM_B1 * m + (1.0 - ADAM_B1) * g
    nv = ADAM_B2 * v + (1.0 - ADAM_B2) * (g * g)
    m_hat = nm / (1.0 - ADAM_B1 ** ADAM_STEP)
    v_hat = nv / (1.0 - ADAM_B2 ** ADAM_STEP)
    return -ADAM_LR * (m_hat / (jnp.sqrt(v_hat) + ADAM_EPS) + ADAM_WD * w), nm, nv


def _adamw_sharded(parts, w, m, v, name):
    def body(p_ref, w_ref, m_ref, v_ref, g_ref, d_ref, nm_ref, nv_ref):
        g = p_ref[0].astype(F32)
        for s in range(1, parts.shape[0]):
            g = g + p_ref[s].astype(F32)
        delta, nm, nv = _adamw(g, w_ref[0], m_ref[0], v_ref[0])
        g_ref[0] = g
        d_ref[0] = delta
        nm_ref[0] = nm
        nv_ref[0] = nv

    n, r, c = parts.shape
    steps = 4 if r % 64 == 0 and r >= 512 else (2 if r % 32 == 0 and r >= 256 else 1)
    tile = pl.BlockSpec((1, r // steps, c), lambda i: (0, i, 0))
    return pl.pallas_call(
        body, name=name, grid=(steps,), out_shape=(jax.ShapeDtypeStruct(w.shape, F32),) * 4,
        in_specs=[pl.BlockSpec((n, r // steps, c), lambda i: (0, i, 0)), tile, tile, tile], out_specs=(tile,) * 4,
        compiler_params=_params(("parallel",)),
    )(parts, w, m, v)


REPLICATED = (("ln_z_g", "ln_z", 0), ("ln_z_b", "ln_z", 1), ("w_s", "w_s", None), ("b_s", "b_s", None),
              ("ln1_g", "ln1", 0), ("ln1_b", "ln1", 1), ("conv_w", "conv_mine", None), ("conv_b", "conv", 3),
              ("ln2_g", "tail", 3), ("ln2_b", "tail", 4), ("b_ple_gate", "tail", 0), ("ln3_g", "tail", 1),
              ("ln3_b", "tail", 2))
GATHERED = ("tail", "ln1", "ln_z", "conv", "w_s", "b_s", "loss", "conv_mine")


def _adamw_replicated(gathered, w, m, v):
    n_par = len(REPLICATED)

    def body(*refs):
        srcs = dict(zip(GATHERED, refs[:len(GATHERED)]))
        rest = refs[len(GATHERED):]
        w_refs, m_refs, v_refs = rest[:n_par], rest[n_par:2 * n_par], rest[2 * n_par:3 * n_par]
        outs = rest[3 * n_par:]
        loss_ref = outs[4 * n_par]
        sums = {}
        for key, ref in srcs.items():
            total = ref[0]
            for dev in range(1, N_DEV):
                total = total + ref[dev]
            sums[key] = total
        loss_ref[...] = sums["loss"]
        for n, (name, key, row) in enumerate(REPLICATED):
            if name == "conv_w":
                g = sums[key][0:3, :]
            elif row is None:
                g = sums[key]
            else:
                g = sums[key][row:row + 1, :]
            lead = len(w_refs[n].shape) - g.ndim
            idx = (0,) * lead + (Ellipsis,)
            delta, nm, nv = _adamw(g, w_refs[n][idx], m_refs[n][idx], v_refs[n][idx])
            for kind, val in enumerate((g, delta, nm, nv)):
                outs[kind * n_par + n][idx] = val

    names = [name for name, _, _ in REPLICATED]
    shapes = [jax.ShapeDtypeStruct(w[name].shape, F32) for name in names]
    args = [gathered[k] for k in GATHERED] + [w[n] for n in names] + [m[n] for n in names] + [v[n] for n in names]
    out_shape = tuple(shapes * 4) + (jax.ShapeDtypeStruct((8, LANES), F32),)
    return pl.pallas_call(
        body, name="adamw_replicated", grid=(1,), out_shape=out_shape,
        in_specs=[_full_spec(a.shape) for a in args], out_specs=tuple(_full_spec(s.shape) for s in out_shape),
        compiler_params=_params(("arbitrary",)),
    )(*args)


def kernel(x, p, positions, w_in, ln_z_g, ln_z_b, w_s, b_s, w_o, ln1_g, ln1_b, w_ff_a, w_ff_b, conv_w, conv_b, w_ff_down, ln2_g, ln2_b, w_ple_gate, b_ple_gate, w_ple_in, ln3_g, ln3_b, loss_target, m_w_in, m_ln_z_g, m_ln_z_b, m_w_s, m_b_s, m_w_o, m_ln1_g, m_ln1_b, m_w_ff_a, m_w_ff_b, m_conv_w, m_conv_b, m_w_ff_down, m_ln2_g, m_ln2_b, m_w_ple_gate, m_b_ple_gate, m_w_ple_in, m_ln3_g, m_ln3_b, v_w_in, v_ln_z_g, v_ln_z_b, v_w_s, v_b_s, v_w_o, v_ln1_g, v_ln1_b, v_w_ff_a, v_w_ff_b, v_conv_w, v_conv_b, v_w_ff_down, v_ln2_g, v_ln2_b, v_w_ple_gate, v_b_ple_gate, v_w_ple_in, v_ln3_g, v_ln3_b):
    w = dict(w_in=w_in, ln_z_g=ln_z_g, ln_z_b=ln_z_b, w_s=w_s, b_s=b_s, w_o=w_o, ln1_g=ln1_g, ln1_b=ln1_b,
             w_ff_a=w_ff_a, w_ff_b=w_ff_b, conv_w=conv_w, conv_b=conv_b, w_ff_down=w_ff_down, ln2_g=ln2_g,
             ln2_b=ln2_b, w_ple_gate=w_ple_gate, b_ple_gate=b_ple_gate, w_ple_in=w_ple_in, ln3_g=ln3_g,
             ln3_b=ln3_b)
    m = dict(w_in=m_w_in, ln_z_g=m_ln_z_g, ln_z_b=m_ln_z_b, w_s=m_w_s, b_s=m_b_s, w_o=m_w_o, ln1_g=m_ln1_g,
             ln1_b=m_ln1_b, w_ff_a=m_w_ff_a, w_ff_b=m_w_ff_b, conv_w=m_conv_w, conv_b=m_conv_b,
             w_ff_down=m_w_ff_down, ln2_g=m_ln2_g, ln2_b=m_ln2_b, w_ple_gate=m_w_ple_gate,
             b_ple_gate=m_b_ple_gate, w_ple_in=m_w_ple_in, ln3_g=m_ln3_g, ln3_b=m_ln3_b)
    v = dict(w_in=v_w_in, ln_z_g=v_ln_z_g, ln_z_b=v_ln_z_b, w_s=v_w_s, b_s=v_b_s, w_o=v_w_o, ln1_g=v_ln1_g,
             ln1_b=v_ln1_b, w_ff_a=v_w_ff_a, w_ff_b=v_w_ff_b, conv_w=v_conv_w, conv_b=v_conv_b,
             w_ff_down=v_w_ff_down, ln2_g=v_ln2_g, ln2_b=v_ln2_b, w_ple_gate=v_w_ple_gate,
             b_ple_gate=v_b_ple_gate, w_ple_in=v_w_ple_in, ln3_g=v_ln3_g, ln3_b=v_ln3_b)
    big_names = [name for name, _, _ in BIG]
    small_names = ("ln_z_g", "ln_z_b", "w_s", "b_s", "ln1_g", "ln1_b", "conv_b", "ln2_g", "ln2_b", "b_ple_gate",
                   "ln3_g", "ln3_b")

    transposed = {name for name, _, mode in BIG if mode == "rows_t"}

    def travel(a, name):
        return jnp.swapaxes(a, 1, 2) if name in transposed else a

    shards = dict(zip(big_names, _cast_shards([travel(w[n], n)[0] for n in big_names])))
    conv_rows = jnp.pad(w["conv_w"][0], ((0, 5), (0, 0)))
    sm = {n: w[n][0] if w[n].ndim > 2 else w[n] for n in small_names}
    pos_col = positions.reshape(S, 1).astype(F32)
    grad_x, (parts, small_all) = _local_step(x[0], p[0, 0], pos_col, loss_target[0], sm,
                                             _Exchanges(shards, conv_rows))
    me = 4 * lax.axis_index("x") + 2 * lax.axis_index("y") + lax.axis_index("c")
    conv_cols = small_all["conv"].reshape(N_DEV, 8, N_DEV, D_FF // N_DEV)
    small_all["conv_mine"] = lax.dynamic_index_in_dim(conv_cols, me, axis=2, keepdims=False)

    leaves = {}
    for name in big_names:
        outs = _adamw_sharded(parts[name], travel(w[name], name), travel(m[name], name), travel(v[name], name),
                              "adamw_" + name)
        leaves[name] = tuple(travel(o, name) for o in outs)
    rep = _adamw_replicated(small_all, w, m, v)
    n_rep = len(REPLICATED)
    for n, (name, _, _) in enumerate(REPLICATED):
        leaves[name] = tuple(rep[kind * n_rep + n] for kind in range(4))
    loss = rep[4 * n_rep][0, 0]
    return (loss, grad_x[None], *[leaves[n][kind] for kind in range(4) for n in WEIGHT_ORDER])
```

```python
import math

import numpy as np
import jax
import jax.numpy as jnp
from jax import lax
from jax.experimental import pallas as pl
from jax.experimental.pallas import tpu as pltpu

F32 = jnp.float32
BF16 = jnp.bfloat16
MESH = pl.DeviceIdType.MESH

N_DEV = 8
S = 4096
D = 1024
D_HALF = 512
D_IN = 2560
D_FF = 2816
D_PLE = 256
CHUNK = 128
DILATIONS = ((1, 32), (4, 8), (16, 2))
ROPE_THETA = 500000.0
LN_EPS = 1e-5
ALPHA = 2.0 ** 0.25
NEG_INF = -1e30
INV_SQRT2 = 1.0 / math.sqrt(2.0)
INV_SQRT_2PI = 1.0 / math.sqrt(2.0 * math.pi)

ADAM_LR, ADAM_B1, ADAM_B2, ADAM_EPS, ADAM_WD, ADAM_STEP = 0.001, 0.9, 0.999, 1e-08, 0.01, 10

TM = 512
NT = S // TM
ATTN_UNROLL = 8
TN = 1408
NJ = D_FF // TN
LANES = 128
VMEM_MIB = 1024 * 1024

BIG = (("w_in", (320, 1024), "rows_t"), ("w_o", (128, 1024), "rows"), ("w_ff_a", (352, 1024), "rows_t"),
       ("w_ff_b", (352, 1024), "rows_t"), ("w_ff_down", (352, 1024), "rows"), ("w_ple_gate", (128, 1024), "rows"),
       ("w_ple_in", (256, 128), "cols"))
WEIGHT_ORDER = ("w_in", "ln_z_g", "ln_z_b", "w_s", "b_s", "w_o", "ln1_g", "ln1_b", "w_ff_a", "w_ff_b",
                "conv_w", "conv_b", "w_ff_down", "ln2_g", "ln2_b", "w_ple_gate", "b_ple_gate",
                "w_ple_in", "ln3_g", "ln3_b")


def _params(semantics=None, vmem_mib=48):
    return pltpu.CompilerParams(dimension_semantics=semantics, vmem_limit_bytes=vmem_mib * VMEM_MIB)


def _dot(a, b):
    return jnp.dot(a, b, preferred_element_type=F32)


def _dot_nt(a, b):
    return lax.dot_general(a, b, (((1,), (1,)), ((), ())), preferred_element_type=F32)


def _dot_tn(a, b):
    return lax.dot_general(a, b, (((0,), (0,)), ((), ())), preferred_element_type=F32)


def _gelu(x):
    return 0.5 * x * (1.0 + lax.erf(x * INV_SQRT2))


def _gelu_and_grad(x):
    cdf = 0.5 * (1.0 + lax.erf(x * INV_SQRT2))
    return x * cdf, cdf + x * (jnp.exp(-0.5 * x * x) * INV_SQRT_2PI)


def _ln_stats(y):
    mu = jnp.mean(y, axis=-1, keepdims=True)
    yc = y - mu
    var = jnp.mean(yc * yc, axis=-1, keepdims=True)
    rstd = lax.rsqrt(var + LN_EPS)
    return yc * rstd, rstd


def _ln_bwd(dxhat, xhat, rstd):
    m1 = jnp.mean(dxhat, axis=-1, keepdims=True)
    m2 = jnp.mean(dxhat * xhat, axis=-1, keepdims=True)
    return rstd * (dxhat - m1 - xhat * m2)


def _colsum(x):
    return jnp.sum(x, axis=0, keepdims=True)


def _rows(i):
    return (i, 0)


def _row_spec(width):
    return pl.BlockSpec((TM, width), _rows)


def _full_spec(shape):
    return pl.BlockSpec(shape, lambda *_: (0,) * len(shape))


def _owner_slot(j):
    return (j % 2) * 4 + j // 2


def _store_owner_blocks(blocks_ref, acc_ref, tile):
    rows = TN // 4
    for t in range(4):
        blocks_ref[(t % 2) * 4 + 2 * tile + t // 2] = acc_ref[rows * t:rows * (t + 1), :].astype(BF16)


def _lane_lo():
    return lax.broadcasted_iota(jnp.int32, (CHUNK, LANES), 1) < 64


def _tril():
    r = lax.broadcasted_iota(jnp.int32, (CHUNK, CHUNK), 0)
    c = lax.broadcasted_iota(jnp.int32, (CHUNK, CHUNK), 1)
    return c <= r


def _rope_consts():
    lane = np.arange(LANES) % 64
    j = lane % 8
    inv = np.where(lane < 16, np.float32(ROPE_THETA) ** (-(2.0 * j).astype(np.float32) / np.float32(16.0)), 0.0)
    m_lo = (lane < 8).astype(np.float32)
    m_hi = ((lane >= 8) & (lane < 16)).astype(np.float32)
    return (jnp.asarray(inv, F32).reshape(1, LANES), jnp.asarray(m_lo).reshape(1, LANES),
            jnp.asarray(m_hi).reshape(1, LANES))


def _rope_tables(pos_col, carried=None):
    inv, m_lo, m_hi = _rope_consts()

    def body(pos_ref, inv_ref, lo_ref, hi_ref, c_ref, sa_ref, sb_ref):
        ang = pos_ref[...] * inv_ref[...]
        c = jnp.cos(ang)
        s = jnp.sin(ang)
        lo = lo_ref[...]
        hi = hi_ref[...]
        c_ref[...] = jnp.where(lo + hi > 0.0, c, 1.0)
        sa_ref[...] = s * hi
        sb_ref[...] = -s * lo

    vec = _full_spec((1, LANES))
    out = jax.ShapeDtypeStruct((S, LANES), F32)
    return _host_call(
        body, carried, name="rope_tables", grid=(NT,), out_shape=(out, out, out),
        in_specs=[pl.BlockSpec((TM, 1), _rows), vec, vec, vec],
        out_specs=(_row_spec(LANES),) * 3, scratch_shapes=[], params=_params(("arbitrary",)),
        args=(pos_col, inv, m_lo, m_hi))


def _rope(t, c, sa, sb):
    return t * c + pltpu.roll(t, 8, 1) * sa + pltpu.roll(t, LANES - 8, 1) * sb


def _rope_t(dy, c, sa, sb):
    return dy * c + pltpu.roll(dy * sa, LANES - 8, 1) + pltpu.roll(dy * sb, 8, 1)


def _masked_ws(ws_ref):
    tril = _tril()
    return [jnp.where(tril, ws_ref[g], 0.0).astype(BF16) for g in range(8)]


def _spatial_mix(zn, wm, bs):
    lo = _lane_lo()
    rows = []
    for ch in range(TM // CHUNK):
        slabs = []
        for pr in range(4):
            zp = zn[ch * CHUNK:(ch + 1) * CHUNK, pr * LANES:(pr + 1) * LANES].astype(BF16)
            slabs.append(jnp.where(lo, _dot(wm[2 * pr], zp), _dot(wm[2 * pr + 1], zp)))
        rows.append(jnp.concatenate(slabs, axis=1) + bs)
    return jnp.concatenate(rows, axis=0)


def _proj_in_fwd(x, w_in, tabs, ln_z_g, ln_z_b, w_s, bs_exp):
    def body(x_ref, w_ref, c_ref, sa_ref, sb_ref, g_ref, b_ref, ws_ref, bs_ref,
             q_ref, k_ref, v_ref, u_ref, z_ref, gm_ref, xb_ref):
        xb = x_ref[...].astype(BF16)
        xb_ref[...] = xb
        c, sa, sb = c_ref[...], sa_ref[...], sb_ref[...]
        hq = _dot_nt(xb, w_ref[0:512, :])
        hk = _dot_nt(xb, w_ref[512:1024, :])
        for s in range(4):
            sl = slice(s * LANES, (s + 1) * LANES)
            q_ref[:, sl] = _rope(hq[:, sl], c, sa, sb)
            k_ref[:, sl] = _rope(hk[:, sl], c, sa, sb)
        v_ref[...] = _dot_nt(xb, w_ref[1024:1536, :])
        u_pre = _dot_nt(xb, w_ref[1536:2048, :])
        z_pre = _dot_nt(xb, w_ref[2048:2560, :])
        u_ref[...] = u_pre
        z_ref[...] = z_pre
        zhat, _ = _ln_stats(_gelu(z_pre))
        zn = zhat * g_ref[...] + b_ref[...]
        mixed = _spatial_mix(zn, _masked_ws(ws_ref), bs_ref[...])
        gm_ref[...] = (_gelu(u_pre) * mixed).astype(BF16)

    half = jax.ShapeDtypeStruct((S, D_HALF), F32)
    tab = _row_spec(LANES)
    return pl.pallas_call(
        body, name="proj_in_fwd", grid=(NT,),
        out_shape=(half, half, half, half, half, jax.ShapeDtypeStruct((S, D_HALF), BF16),
                   jax.ShapeDtypeStruct((S, D), BF16)),
        in_specs=[_row_spec(D), _full_spec((D_IN, D)), tab, tab, tab, _full_spec((1, D_HALF)),
                  _full_spec((1, D_HALF)), _full_spec((8, CHUNK, CHUNK)), _full_spec((CHUNK, D_HALF))],
        out_specs=(_row_spec(D_HALF),) * 6 + (_row_spec(D),),
        compiler_params=_params(("parallel",)),
    )(x, w_in, *tabs, ln_z_g, ln_z_b, w_s, bs_exp)


def _store_band_bias(bias_ref):
    qi = lax.broadcasted_iota(jnp.int32, (CHUNK, 2 * CHUNK), 0)
    kj = lax.broadcasted_iota(jnp.int32, (CHUNK, 2 * CHUNK), 1)
    band = (kj >= qi) & (kj <= qi + CHUNK)
    bias_ref[0] = jnp.where(band, 0.0, NEG_INF)
    bias_ref[1] = jnp.where(band & (kj >= CHUNK), 0.0, NEG_INF)


def _permuted_rows(ref, d, r):
    return ref[...] if d == 1 else ref[pl.ds(r, S // d, stride=d), :]


def _attention_fwd(q, k, v, carried=None):
    def body(q_ref, k_ref, v_ref, o_ref, lse_ref, qb, kb, v0b, v1b, bias, op, lp, ob0, lb0, ob1, lb1, ob2, lb2):
        lo = _lane_lo()
        lo_f = lo.astype(F32)[0:1, :]
        hi_f = 1.0 - lo_f
        zero_pad = jnp.zeros((CHUNK, LANES), BF16)
        for buf in (qb, kb, v0b, v1b):
            buf[0:CHUNK, :] = zero_pad
        _store_band_bias(bias)
        outs = ((ob0, lb0), (ob1, lb1), (ob2, lb2))
        for (d, nb), (ob, lb) in zip(DILATIONS, outs):
            length = S // d
            for r in range(d):
                dst = slice(CHUNK + r * length, CHUNK + (r + 1) * length)
                qb[dst, :] = (_permuted_rows(q_ref, d, r) * 0.125).astype(BF16)
                kb[dst, :] = _permuted_rows(k_ref, d, r).astype(BF16)
                vs = _permuted_rows(v_ref, d, r)
                v0b[dst, :] = (vs * lo_f + hi_f).astype(BF16)
                v1b[dst, :] = (vs * hi_f + lo_f).astype(BF16)

            def block(b, carry, nb=nb):
                base = pl.multiple_of(b * CHUNK, CHUNK)
                add = bias[jnp.where(b % nb == 0, 1, 0)]
                qblk = qb[pl.ds(pl.multiple_of(base + CHUNK, CHUNK), CHUNK), :]
                kblk = kb[pl.ds(base, 2 * CHUNK), :]
                q2 = jnp.concatenate([jnp.where(lo, qblk, 0), jnp.where(lo, 0, qblk)], axis=0)
                s2 = _dot_nt(q2, kblk) + jnp.concatenate([add, add], axis=0)
                m2 = jnp.max(s2, axis=-1, keepdims=True)
                p2 = jnp.exp(s2 - m2).astype(BF16)
                pv, mx = [], []
                for head, vh in enumerate((v0b, v1b)):
                    rows = slice(head * CHUNK, (head + 1) * CHUNK)
                    pv.append(_dot(p2[rows, :], vh[pl.ds(base, 2 * CHUNK), :]))
                    mx.append(m2[rows, :])
                den = pltpu.roll(jnp.where(lo, pv[1], pv[0]), 64, 1)
                op[pl.ds(base, CHUNK), :] = jnp.where(lo, pv[0], pv[1]) / den
                lp[pl.ds(base, CHUNK), :] = jnp.where(lo, mx[0], mx[1]) + jnp.log(den)
                return carry

            lax.fori_loop(0, S // CHUNK, block, 0, unroll=ATTN_UNROLL)
            for r in range(d):
                src = slice(r * length, (r + 1) * length)
                if d == 1:
                    ob[...] = op[...]
                    lb[...] = lp[...]
                else:
                    ob[pl.ds(r, length, stride=d), :] = op[src, :]
                    lb[pl.ds(r, length, stride=d), :] = lp[src, :]
        for t in range(NT):
            rows = slice(t * TM, (t + 1) * TM)
            l0, l1, l2 = lb0[rows, :], lb1[rows, :], lb2[rows, :]
            mx = jnp.maximum(jnp.maximum(l0, l1), l2)
            e0, e1, e2 = jnp.exp(l0 - mx), jnp.exp(l1 - mx), jnp.exp(l2 - mx)
            den = e0 + e1 + e2
            o_ref[rows, :] = (e0 * ob0[rows, :] + e1 * ob1[rows, :] + e2 * ob2[rows, :]) / den
            lse_ref[rows, :] = mx + jnp.log(den)

    slab = pl.BlockSpec((S, LANES), lambda h: (0, h))
    out = jax.ShapeDtypeStruct((S, D_HALF), F32)
    padded = pltpu.VMEM((CHUNK + S, LANES), BF16)
    whole = pltpu.VMEM((S, LANES), F32)
    return _host_call(
        body, carried, name="attention_fwd", grid=(4,), out_shape=(out, out),
        in_specs=[slab, slab, slab], out_specs=(slab, slab),
        scratch_shapes=[padded] * 4 + [pltpu.VMEM((2, CHUNK, 2 * CHUNK), F32)] + [whole] * 8,
        params=_params(("arbitrary",), 56), args=(q, k, v))


def _mix_ln1_fwd(attn, gm, w_o, x, g1, b1):
    def body(a_ref, gm_ref, w_ref, x_ref, g_ref, b_ref, xhat_ref, rstd_ref, x1b_ref):
        mix = _dot(a_ref[...].astype(BF16), w_ref[0:D_HALF, :]) + _dot(gm_ref[...], w_ref[D_HALF:D, :])
        xhat, rstd = _ln_stats(ALPHA * x_ref[...] + mix)
        xhat_ref[...] = xhat
        rstd_ref[...] = rstd
        x1b_ref[...] = (xhat * g_ref[...] + b_ref[...]).astype(BF16)

    vec = _full_spec((1, D))
    return pl.pallas_call(
        body, name="mix_ln1_fwd", grid=(NT,),
        out_shape=(jax.ShapeDtypeStruct((S, D), F32), jax.ShapeDtypeStruct((S, 1), F32),
                   jax.ShapeDtypeStruct((S, D), BF16)),
        in_specs=[_row_spec(D_HALF), _row_spec(D_HALF), _full_spec((D, D)), _row_spec(D), vec, vec],
        out_specs=(_row_spec(D), pl.BlockSpec((TM, 1), _rows), _row_spec(D)),
        compiler_params=_params(("parallel",)),
    )(attn, gm, w_o, x, g1, b1)


def _ffn_up_fwd(x1b, w_a, w_b, conv_w8, conv_b, carried=None):
    def body(x_ref, wa_ref, wb_ref, cw_ref, cb_ref, ap_ref, a_ref, bl_ref, h_ref, carry):
        @pl.when(pl.program_id(1) == 0)
        def _():
            carry[...] = jnp.zeros_like(carry)

        xb = x_ref[...]
        ap = _dot_nt(xb, wa_ref[...])
        bl = _dot_nt(xb, wb_ref[...])
        row = lax.broadcasted_iota(jnp.int32, (TM, TN), 0)
        c6, c7 = carry[6:7, :], carry[7:8, :]
        m1 = jnp.where(row == 0, c7, pltpu.roll(ap, 1, 0))
        m2 = jnp.where(row == 0, c6, jnp.where(row == 1, c7, pltpu.roll(ap, 2, 0)))
        a = cb_ref[...] + cw_ref[0:1, :] * m2 + cw_ref[1:2, :] * m1 + cw_ref[2:3, :] * ap
        carry[...] = ap[TM - 8:TM, :]
        ap_ref[...] = ap.astype(BF16)
        a_ref[...] = a
        bl_ref[...] = bl.astype(BF16)
        h_ref[...] = (_gelu(a) * bl).astype(BF16)

    tile = pl.BlockSpec((TM, TN), lambda j, i: (i, j))
    wcol = pl.BlockSpec((TN, D), lambda j, i: (j, 0))
    ff = jax.ShapeDtypeStruct((S, D_FF), F32)
    ffb = jax.ShapeDtypeStruct((S, D_FF), BF16)
    return _host_call(
        body, carried, name="ffn_up_fwd", grid=(NJ, NT),
        out_shape=(ffb, ff, ffb, ffb),
        in_specs=[pl.BlockSpec((TM, D), lambda j, i: (i, 0)), wcol, wcol,
                  pl.BlockSpec((8, TN), lambda j, i: (0, j)), pl.BlockSpec((1, TN), lambda j, i: (0, j))],
        out_specs=(tile, tile, tile, tile),
        scratch_shapes=[pltpu.VMEM((8, TN), F32)],
        params=_params(("arbitrary", "arbitrary"), 56), args=(x1b, w_a, w_b, conv_w8, conv_b))


def _ffn_down_ln2_fwd(hff, w_down, xhat1, g1, b1):
    def body(h_ref, w_ref, xh_ref, g_ref, b_ref, xhat_ref, rstd_ref):
        x1 = xh_ref[...] * g_ref[...] + b_ref[...]
        xhat, rstd = _ln_stats(ALPHA * x1 + _dot(h_ref[...], w_ref[...]))
        xhat_ref[...] = xhat
        rstd_ref[...] = rstd

    vec = _full_spec((1, D))
    return pl.pallas_call(
        body, name="ffn_down_ln2_fwd", grid=(NT,),
        out_shape=(jax.ShapeDtypeStruct((S, D), F32), jax.ShapeDtypeStruct((S, 1), F32)),
        in_specs=[_row_spec(D_FF), _full_spec((D_FF, D)), _row_spec(D), vec, vec],
        out_specs=(_row_spec(D), pl.BlockSpec((TM, 1), _rows)),
        compiler_params=_params(("parallel",)),
    )(hff, w_down, xhat1, g1, b1)


def _tail_fwd_bwd(xhat2, rstd2, p, target, w_g, w_p, g2, b2, bg, g3, b3):
    def body(xh_ref, rs_ref, p_ref, t_ref, wg_ref, wp_ref, g2_ref, b2_ref, bg_ref, g3_ref, b3_ref,
             loss_ref, dy2_ref, dy2b_ref, gwg_ref, gwp_ref, vec_ref, dwg_ref, dwp_ref):
        @pl.when(pl.program_id(0) == 0)
        def _():
            loss_ref[...] = jnp.zeros_like(loss_ref)
            dwg_ref[...] = jnp.zeros_like(dwg_ref)
            dwp_ref[...] = jnp.zeros_like(dwp_ref)
            vec_ref[...] = jnp.zeros_like(vec_ref)

        xhat2_t = xh_ref[...]
        x2 = xhat2_t * g2_ref[...] + b2_ref[...]
        x2b = x2.astype(BF16)
        pb = p_ref[...].astype(BF16)
        gate = jax.nn.sigmoid(_dot(x2b, wg_ref[...]) + bg_ref[...])
        pin = _dot(pb, wp_ref[...])
        xhat3, rstd3 = _ln_stats(ALPHA * x2 + gate * pin)
        err = xhat3 * g3_ref[...] + b3_ref[...] - t_ref[...]
        loss_ref[...] += jnp.sum(jnp.mean(err * err, axis=-1, keepdims=True), axis=0, keepdims=True) * 0.5
        dout = err * (1.0 / D)
        dy3 = _ln_bwd(dout * g3_ref[...], xhat3, rstd3)
        dgp = dy3 * pin * gate * (1.0 - gate)
        dgpb = dgp.astype(BF16)
        dwg_ref[...] += _dot_tn(x2b, dgpb)
        dwp_ref[...] += _dot_tn(pb, (dy3 * gate).astype(BF16))
        dx2 = ALPHA * dy3 + _dot_nt(dgpb, wg_ref[...])
        dy2 = _ln_bwd(dx2 * g2_ref[...], xhat2_t, rs_ref[...])
        dy2_ref[...] = dy2
        dy2b_ref[...] = dy2.astype(BF16)
        vec_ref[0:1, :] += _colsum(dgp)
        vec_ref[1:2, :] += _colsum(dout * xhat3)
        vec_ref[2:3, :] += _colsum(dout)
        vec_ref[3:4, :] += _colsum(dx2 * xhat2_t)
        vec_ref[4:5, :] += _colsum(dx2)

        @pl.when(pl.program_id(0) == NT - 1)
        def _():
            for j in range(N_DEV):
                gwg_ref[_owner_slot(j)] = dwg_ref[LANES * j:LANES * (j + 1), :].astype(BF16)
                gwp_ref[_owner_slot(j)] = dwp_ref[:, LANES * j:LANES * (j + 1)].astype(BF16)

    vec = _full_spec((1, D))
    return pl.pallas_call(
        body, name="tail_fwd_bwd", grid=(NT,),
        out_shape=(jax.ShapeDtypeStruct((8, LANES), F32), jax.ShapeDtypeStruct((S, D), F32),
                   jax.ShapeDtypeStruct((S, D), BF16), jax.ShapeDtypeStruct((N_DEV, D // N_DEV, D), BF16),
                   jax.ShapeDtypeStruct((N_DEV, D_PLE, D // N_DEV), BF16), jax.ShapeDtypeStruct((8, D), F32)),
        in_specs=[_row_spec(D), pl.BlockSpec((TM, 1), _rows), _row_spec(D_PLE), _row_spec(D),
                  _full_spec((D, D)), _full_spec((D_PLE, D)), vec, vec, vec, vec, vec],
        out_specs=(_full_spec((8, LANES)), _row_spec(D), _row_spec(D), _full_spec((N_DEV, D // N_DEV, D)),
                   _full_spec((N_DEV, D_PLE, D // N_DEV)), _full_spec((8, D))),
        scratch_shapes=[pltpu.VMEM((D, D), F32), pltpu.VMEM((D_PLE, D), F32)],
        compiler_params=_params(("arbitrary",)),
    )(xhat2, rstd2, p, target, w_g, w_p, g2, b2, bg, g3, b3)


def _ffn_bwd_act(dy2b, w_down, a_pre, a, b_lin, hff, conv_w8):
    def body(dy_ref, wd_ref, ap_ref, a_ref, bl_ref, h_ref, cw_ref, dap_ref, dbl_ref, gwd_ref, dcw_ref, carry,
             dwd_ref):
        @pl.when(pl.program_id(1) == 0)
        def _():
            carry[...] = jnp.zeros_like(carry)
            dwd_ref[...] = jnp.zeros_like(dwd_ref)
            dcw_ref[...] = jnp.zeros_like(dcw_ref)

        dyb = dy_ref[...]
        dh = _dot_nt(dyb, wd_ref[...])
        av = a_ref[...]
        cdf = 0.5 * (1.0 + lax.erf(av * INV_SQRT2))
        dbl_ref[...] = (dh * (av * cdf)).astype(BF16)
        da = dh * bl_ref[...].astype(F32) * (cdf + av * (jnp.exp(-0.5 * av * av) * INV_SQRT_2PI))
        row = lax.broadcasted_iota(jnp.int32, (TM, TN), 0)
        c0, c1 = carry[0:1, :], carry[1:2, :]
        p1 = jnp.where(row == TM - 1, c0, pltpu.roll(da, TM - 1, 0))
        p2 = jnp.where(row == TM - 2, c0, jnp.where(row == TM - 1, c1, pltpu.roll(da, TM - 2, 0)))
        carry[...] = da[0:8, :]
        ap = ap_ref[...].astype(F32)
        dcw_ref[3:4, :] += _colsum(da)
        dcw_ref[0:1, :] += _colsum(ap * p2)
        dcw_ref[1:2, :] += _colsum(ap * p1)
        dcw_ref[2:3, :] += _colsum(ap * da)
        dap_ref[...] = (cw_ref[2:3, :] * da + cw_ref[1:2, :] * p1 + cw_ref[0:1, :] * p2).astype(BF16)
        dwd_ref[...] += _dot_tn(h_ref[...], dyb)

        @pl.when(pl.program_id(1) == NT - 1)
        def _():
            _store_owner_blocks(gwd_ref, dwd_ref, pl.program_id(0))

    rev_tile = pl.BlockSpec((TM, TN), lambda j, i: (NT - 1 - i, j))
    wrows = pl.BlockSpec((TN, D), lambda j, i: (j, 0))
    small = pl.BlockSpec((8, TN), lambda j, i: (0, j))
    ffb = jax.ShapeDtypeStruct((S, D_FF), BF16)
    blocks = (N_DEV, D_FF // N_DEV, D)
    return pl.pallas_call(
        body, name="ffn_bwd_act", grid=(NJ, NT),
        out_shape=(ffb, ffb, jax.ShapeDtypeStruct(blocks, BF16), jax.ShapeDtypeStruct((8, D_FF), F32)),
        in_specs=[pl.BlockSpec((TM, D), lambda j, i: (NT - 1 - i, 0)), wrows, rev_tile, rev_tile, rev_tile,
                  rev_tile, small],
        out_specs=(rev_tile, rev_tile, _full_spec(blocks), small),
        scratch_shapes=[pltpu.VMEM((8, TN), F32), pltpu.VMEM((TN, D), F32)],
        compiler_params=_params(("arbitrary", "arbitrary"), 56),
    )(dy2b, w_down, a_pre, a, b_lin, hff, conv_w8)


def _ffn_bwd_w(dap, dbl, x1b):
    def body(dap_ref, dbl_ref, x_ref, gwa_ref, gwb_ref, dwa_ref, dwb_ref):
        @pl.when(pl.program_id(1) == 0)
        def _():
            dwa_ref[...] = jnp.zeros_like(dwa_ref)
            dwb_ref[...] = jnp.zeros_like(dwb_ref)

        xb = x_ref[...]
        dwa_ref[...] += _dot_tn(dap_ref[...], xb)
        dwb_ref[...] += _dot_tn(dbl_ref[...], xb)

        @pl.when(pl.program_id(1) == NT - 1)
        def _():
            _store_owner_blocks(gwa_ref, dwa_ref, pl.program_id(0))
            _store_owner_blocks(gwb_ref, dwb_ref, pl.program_id(0))

    tile = pl.BlockSpec((TM, TN), lambda j, i: (i, j))
    blocks = (N_DEV, D_FF // N_DEV, D)
    out = jax.ShapeDtypeStruct(blocks, BF16)
    return pl.pallas_call(
        body, name="ffn_bwd_w", grid=(NJ, NT), out_shape=(out, out),
        in_specs=[tile, tile, pl.BlockSpec((TM, D), lambda j, i: (i, 0))],
        out_specs=(_full_spec(blocks), _full_spec(blocks)),
        scratch_shapes=[pltpu.VMEM((TN, D), F32), pltpu.VMEM((TN, D), F32)],
        compiler_params=_params(("arbitrary", "arbitrary"), 56),
    )(dap, dbl, x1b)


def _ffn_bwd_x(dap, dbl, w_a, w_b):
    def body(dap_ref, dbl_ref, wa_ref, wb_ref, dx_ref):
        dx_ref[...] = _dot(dap_ref[...], wa_ref[...]) + _dot(dbl_ref[...], wb_ref[...])

    return pl.pallas_call(
        body, name="ffn_bwd_x", grid=(NT,), out_shape=jax.ShapeDtypeStruct((S, D), F32),
        in_specs=[_row_spec(D_FF), _row_spec(D_FF), _full_spec((D_FF, D)), _full_spec((D_FF, D))],
        out_specs=_row_spec(D), compiler_params=_params(("parallel",), 56),
    )(dap, dbl, w_a, w_b)


def _ln1_mix_bwd(dy2, dx1_ffn, xhat1, rstd1, g1, attn, gm, w_o, carried=None):
    def body(dy2_ref, dxf_ref, xh_ref, rs_ref, g_ref, a_ref, gm_ref, w_ref,
             dy1_ref, da_ref, dlt_ref, dgm_ref, gwo_ref, vec_ref, dwo_ref):
        @pl.when(pl.program_id(0) == 0)
        def _():
            dwo_ref[...] = jnp.zeros_like(dwo_ref)
            vec_ref[...] = jnp.zeros_like(vec_ref)

        xhat = xh_ref[...]
        dx1 = ALPHA * dy2_ref[...] + dxf_ref[...]
        vec_ref[0:1, :] += _colsum(dx1 * xhat)
        vec_ref[1:2, :] += _colsum(dx1)
        dy1 = _ln_bwd(dx1 * g_ref[...], xhat, rs_ref[...])
        dy1_ref[...] = dy1
        dy1b = dy1.astype(BF16)
        dmix = _dot_nt(dy1b, w_ref[...])
        attn_t = a_ref[...]
        d_attn = dmix[:, 0:D_HALF]
        da_ref[...] = d_attn
        dgm_ref[...] = dmix[:, D_HALF:D]
        lo = (lax.broadcasted_iota(jnp.int32, (TM, LANES), 1) < 64)
        for s in range(4):
            sl = slice(s * LANES, (s + 1) * LANES)
            prod = d_attn[:, sl] * attn_t[:, sl]
            s0 = jnp.sum(jnp.where(lo, prod, 0.0), axis=-1, keepdims=True)
            s1 = jnp.sum(jnp.where(lo, 0.0, prod), axis=-1, keepdims=True)
            dlt_ref[:, sl] = jnp.where(lo, s0, s1)
        dwo_ref[0:D_HALF, :] += _dot_tn(attn_t.astype(BF16), dy1b)
        dwo_ref[D_HALF:D, :] += _dot_tn(gm_ref[...], dy1b)

        @pl.when(pl.program_id(0) == NT - 1)
        def _():
            for j in range(N_DEV):
                gwo_ref[_owner_slot(j)] = dwo_ref[LANES * j:LANES * (j + 1), :].astype(BF16)

    half = jax.ShapeDtypeStruct((S, D_HALF), F32)
    return _host_call(
        body, carried, name="ln1_mix_bwd", grid=(NT,),
        out_shape=(jax.ShapeDtypeStruct((S, D), F32), half, half, half,
                   jax.ShapeDtypeStruct((N_DEV, D // N_DEV, D), BF16), jax.ShapeDtypeStruct((8, D), F32)),
        in_specs=[_row_spec(D), _row_spec(D), _row_spec(D), pl.BlockSpec((TM, 1), _rows), _full_spec((1, D)),
                  _row_spec(D_HALF), _row_spec(D_HALF), _full_spec((D, D))],
        out_specs=(_row_spec(D), _row_spec(D_HALF), _row_spec(D_HALF), _row_spec(D_HALF),
                   _full_spec((N_DEV, D // N_DEV, D)), _full_spec((8, D))),
        scratch_shapes=[pltpu.VMEM((D, D), F32)],
        params=_params(("arbitrary",)), args=(dy2, dx1_ffn, xhat1, rstd1, g1, attn, gm, w_o))


def _gmlp_bwd(d_gm, u_pre, z_pre, ln_z_g, ln_z_b, w_s, bs_exp, carried=None):
    def body(dg_ref, u_ref, z_ref, g_ref, b_ref, ws_ref, bs_ref, du_ref, dz_ref, dws_ref, dbs_ref, vec_ref):
        @pl.when(pl.program_id(0) == 0)
        def _():
            dws_ref[...] = jnp.zeros_like(dws_ref)
            dbs_ref[...] = jnp.zeros_like(dbs_ref)
            vec_ref[...] = jnp.zeros_like(vec_ref)

        u_pre_t, z_pre_t, dgm = u_ref[...], z_ref[...], dg_ref[...]
        z_act, z_slope = _gelu_and_grad(z_pre_t)
        u_act, u_slope = _gelu_and_grad(u_pre_t)
        zhat, rstd = _ln_stats(z_act)
        zn = zhat * g_ref[...] + b_ref[...]
        wm = _masked_ws(ws_ref)
        mixed = _spatial_mix(zn, wm, bs_ref[...])
        du_ref[...] = (dgm * mixed * u_slope).astype(BF16)
        dmixed = dgm * u_act
        lo = _lane_lo()
        tril = _tril()
        group_of_lane = lax.broadcasted_iota(jnp.int32, (8, D_HALF), 1) // 64
        pick = (group_of_lane == lax.broadcasted_iota(jnp.int32, (8, D_HALF), 0)).astype(F32)
        dzn_rows = []
        for ch in range(TM // CHUNK):
            rows = slice(ch * CHUNK, (ch + 1) * CHUNK)
            dbs_ref[...] += lax.dot_general(pick, dmixed[rows, :], (((1,), (1,)), ((), ())),
                                            precision=lax.Precision.HIGHEST, preferred_element_type=F32)
            slabs = []
            for pr in range(4):
                sl = slice(pr * LANES, (pr + 1) * LANES)
                dm = dmixed[rows, sl]
                zp = zn[rows, sl].astype(BF16)
                dm_lo = jnp.where(lo, dm, 0.0).astype(BF16)
                dm_hi = jnp.where(lo, 0.0, dm).astype(BF16)
                dws_ref[2 * pr] += jnp.where(tril, _dot_nt(dm_lo, zp), 0.0)
                dws_ref[2 * pr + 1] += jnp.where(tril, _dot_nt(dm_hi, zp), 0.0)
                dmb = dm.astype(BF16)
                slabs.append(jnp.where(lo, _dot_tn(wm[2 * pr], dmb), _dot_tn(wm[2 * pr + 1], dmb)))
            dzn_rows.append(jnp.concatenate(slabs, axis=1))
        dzn = jnp.concatenate(dzn_rows, axis=0)
        vec_ref[0:1, :] += _colsum(dzn * zhat)
        vec_ref[1:2, :] += _colsum(dzn)
        dz = _ln_bwd(dzn * g_ref[...], zhat, rstd)
        dz_ref[...] = (dz * z_slope).astype(BF16)

    halfb = jax.ShapeDtypeStruct((S, D_HALF), BF16)
    vec = _full_spec((1, D_HALF))
    return _host_call(
        body, carried, name="gmlp_bwd", grid=(NT,),
        out_shape=(halfb, halfb, jax.ShapeDtypeStruct((8, CHUNK, CHUNK), F32),
                   jax.ShapeDtypeStruct((8, CHUNK), F32), jax.ShapeDtypeStruct((8, D_HALF), F32)),
        in_specs=[_row_spec(D_HALF), _row_spec(D_HALF), _row_spec(D_HALF), vec, vec,
                  _full_spec((8, CHUNK, CHUNK)), _full_spec((CHUNK, D_HALF))],
        out_specs=(_row_spec(D_HALF), _row_spec(D_HALF), _full_spec((8, CHUNK, CHUNK)),
                   _full_spec((8, CHUNK)), _full_spec((8, D_HALF))),
        scratch_shapes=[], params=_params(("arbitrary",)), args=(d_gm, u_pre, z_pre, ln_z_g, ln_z_b, w_s, bs_exp))


def _attention_bwd(q, k, v, lse, d_attn, delta, tabs, carried=None):
    def body(q_ref, k_ref, v_ref, l_ref, do_ref, dl_ref, c_ref, sa_ref, sb_ref, dq_ref, dk_ref, dv_ref,
             qb, kb, vb, gb, bias, lsp, dlp, dqp, dk_own, dk_prev, dv_own, dv_prev, dqa, dka, dva):
        lo = _lane_lo()
        zero_pad = jnp.zeros((CHUNK, LANES), BF16)
        for buf in (qb, kb, vb, gb):
            buf[0:CHUNK, :] = zero_pad
        dk_prev[S:S + CHUNK, :] = jnp.zeros((CHUNK, LANES), F32)
        dv_prev[S:S + CHUNK, :] = jnp.zeros((CHUNK, LANES), F32)
        _store_band_bias(bias)
        for d, nb in DILATIONS:
            length = S // d
            for r in range(d):
                dst = slice(CHUNK + r * length, CHUNK + (r + 1) * length)
                src = slice(r * length, (r + 1) * length)
                qb[dst, :] = (_permuted_rows(q_ref, d, r) * 0.125).astype(BF16)
                kb[dst, :] = _permuted_rows(k_ref, d, r).astype(BF16)
                vb[dst, :] = _permuted_rows(v_ref, d, r).astype(BF16)
                gb[dst, :] = _permuted_rows(do_ref, d, r).astype(BF16)
                lsp[src, :] = _permuted_rows(l_ref, d, r)
                dlp[src, :] = _permuted_rows(dl_ref, d, r)

            def block(b, carry, nb=nb):
                base = pl.multiple_of(b * CHUNK, CHUNK)
                own = pl.multiple_of(base + CHUNK, CHUNK)
                add = bias[jnp.where(b % nb == 0, 1, 0)]
                qblk = qb[pl.ds(own, CHUNK), :]
                gblk = gb[pl.ds(own, CHUNK), :]
                kblk = kb[pl.ds(base, 2 * CHUNK), :]
                vblk = vb[pl.ds(base, 2 * CHUNK), :]
                lse_t = lsp[pl.ds(base, CHUNK), :]
                dlt_t = dlp[pl.ds(base, CHUNK), :]
                q2 = jnp.concatenate([jnp.where(lo, qblk, 0), jnp.where(lo, 0, qblk)], axis=0)
                g2 = jnp.concatenate([jnp.where(lo, gblk, 0), jnp.where(lo, 0, gblk)], axis=0)
                lse2 = jnp.concatenate([lse_t[:, 0:1], lse_t[:, 64:65]], axis=0)
                dlt2 = jnp.concatenate([dlt_t[:, 0:1], dlt_t[:, 64:65]], axis=0)
                add2 = jnp.concatenate([add, add], axis=0)
                p = jnp.exp(_dot_nt(q2, kblk) + add2 - lse2)
                ds = (p * (_dot_nt(g2, vblk) - dlt2)).astype(BF16)
                dv_blk = _dot_tn(p.astype(BF16), g2)
                dk_blk = _dot_tn(ds, q2)
                dq2 = _dot(ds, kblk)
                dqp[pl.ds(base, CHUNK), :] = jnp.where(lo, dq2[0:CHUNK, :], dq2[CHUNK:2 * CHUNK, :]) * 0.125
                dk_prev[pl.ds(base, CHUNK), :] = dk_blk[0:CHUNK, :]
                dk_own[pl.ds(own, CHUNK), :] = dk_blk[CHUNK:2 * CHUNK, :]
                dv_prev[pl.ds(base, CHUNK), :] = dv_blk[0:CHUNK, :]
                dv_own[pl.ds(own, CHUNK), :] = dv_blk[CHUNK:2 * CHUNK, :]
                return carry

            lax.fori_loop(0, S // CHUNK, block, 0, unroll=ATTN_UNROLL)
            for r in range(d):
                src = slice(r * length, (r + 1) * length)
                pad = slice(CHUNK + r * length, CHUNK + (r + 1) * length)
                if d == 1:
                    dqa[...] = dqp[...]
                    dka[...] = dk_own[pad, :] + dk_prev[pad, :]
                    dva[...] = dv_own[pad, :] + dv_prev[pad, :]
                else:
                    dst = pl.ds(r, length, stride=d)
                    dqa[dst, :] = dqa[dst, :] + dqp[src, :]
                    dka[dst, :] = dka[dst, :] + (dk_own[pad, :] + dk_prev[pad, :])
                    dva[dst, :] = dva[dst, :] + (dv_own[pad, :] + dv_prev[pad, :])
        for t in range(NT):
            rows = slice(t * TM, (t + 1) * TM)
            c, sa, sb = c_ref[rows, :], sa_ref[rows, :], sb_ref[rows, :]
            dq_ref[rows, :] = _rope_t(dqa[rows, :], c, sa, sb).astype(BF16)
            dk_ref[rows, :] = _rope_t(dka[rows, :], c, sa, sb).astype(BF16)
            dv_ref[rows, :] = dva[rows, :].astype(BF16)

    slab = pl.BlockSpec((S, LANES), lambda h: (0, h), pipeline_mode=pl.Buffered(1))
    tab = pl.BlockSpec((S, LANES), lambda h: (0, 0), pipeline_mode=pl.Buffered(1))
    out_slab = pl.BlockSpec((S, LANES), lambda h: (0, h))
    out = jax.ShapeDtypeStruct((S, D_HALF), BF16)
    padded_b = pltpu.VMEM((CHUNK + S, LANES), BF16)
    padded_f = pltpu.VMEM((CHUNK + S, LANES), F32)
    whole = pltpu.VMEM((S, LANES), F32)
    return _host_call(
        body, carried, name="attention_bwd", grid=(4,), out_shape=(out, out, out),
        in_specs=[slab] * 6 + [tab] * 3, out_specs=(out_slab,) * 3,
        scratch_shapes=[padded_b] * 4 + [pltpu.VMEM((2, CHUNK, 2 * CHUNK), F32)] + [whole] * 3
        + [padded_f] * 4 + [whole] * 3,
        params=_params(("arbitrary",), 60), args=(q, k, v, lse, d_attn, delta, *tabs))


def _proj_in_bwd_w(xb, parts, carried=None):
    def body(x_ref, p0, p1, p2, p3, p4, gw_ref, dw_ref):
        @pl.when(pl.program_id(0) == 0)
        def _():
            dw_ref[...] = jnp.zeros_like(dw_ref)

        xt = x_ref[...]
        for n, part in enumerate((p0, p1, p2, p3, p4)):
            dw_ref[n * D_HALF:(n + 1) * D_HALF, :] += _dot_tn(part[...], xt)

        @pl.when(pl.program_id(0) == NT - 1)
        def _():
            width = D_IN // N_DEV
            for j in range(N_DEV):
                gw_ref[_owner_slot(j)] = dw_ref[width * j:width * (j + 1), :].astype(BF16)

    return _host_call(
        body, carried, name="proj_in_bwd_w", grid=(NT,),
        out_shape=(jax.ShapeDtypeStruct((N_DEV, D_IN // N_DEV, D), BF16),),
        in_specs=[_row_spec(D)] + [_row_spec(D_HALF)] * 5, out_specs=(_full_spec((N_DEV, D_IN // N_DEV, D)),),
        scratch_shapes=[pltpu.VMEM((D_IN, D), F32)], params=_params(("arbitrary",)), args=(xb, *parts))


def _proj_in_bwd_x(dy1, parts, w_in, carried=None):
    def body(dy_ref, p0, p1, p2, p3, p4, w_ref, gx_ref):
        acc = ALPHA * dy_ref[...]
        for n, part in enumerate((p0, p1, p2, p3, p4)):
            acc += _dot(part[...], w_ref[n * D_HALF:(n + 1) * D_HALF, :])
        gx_ref[...] = acc

    return _host_call(
        body, carried, name="proj_in_bwd_x", grid=(NT,), out_shape=(jax.ShapeDtypeStruct((S, D), F32),),
        in_specs=[_row_spec(D)] + [_row_spec(D_HALF)] * 5 + [_full_spec((D_IN, D))], out_specs=(_row_spec(D),),
        scratch_shapes=[], params=_params(("arbitrary",)), args=(dy1, *parts, w_in))


def _to_natural(blocks, name):
    n, rows, w = blocks.shape
    tile = min(rows, 256)

    def body(i_ref, o_ref):
        o_ref[...] = jnp.concatenate([i_ref[j] for j in range(n)], axis=1)

    return pl.pallas_call(
        body, name=name, grid=(rows // tile,), out_shape=jax.ShapeDtypeStruct((rows, n * w), blocks.dtype),
        in_specs=[pl.BlockSpec((n, tile, w), lambda i: (0, i, 0))],
        out_specs=pl.BlockSpec((tile, n * w), lambda i: (i, 0)), compiler_params=_params(("parallel",)),
    )(blocks)


def _local_step(x, p, pos_col, target, sm, ex):
    bs_exp = jnp.repeat(sm["b_s"].T, 64, axis=1)
    tabs, got = _rope_tables(pos_col, ex.gather_input())
    w_in = ex.weight_input(got)
    q, k, v, u_pre, z_pre, gm, xb = _proj_in_fwd(x, w_in, tabs, sm["ln_z_g"], sm["ln_z_b"], sm["w_s"], bs_exp)
    (attn, lse), got = _attention_fwd(q, k, v, ex.gather_first())
    wa = ex.weights_first(got)
    xhat1, rstd1, x1b = _mix_ln1_fwd(attn, gm, wa["w_o"], x, sm["ln1_g"], sm["ln1_b"])
    (a_pre, a, b_lin, hff), got = _ffn_up_fwd(x1b, wa["w_ff_a"], wa["w_ff_b"], wa["conv_w8"], sm["conv_b"],
                                              ex.gather_second())
    wc = ex.weights_second(got)
    xhat2, rstd2 = _ffn_down_ln2_fwd(hff, wc["w_ff_down"], xhat1, sm["ln1_g"], sm["ln1_b"])
    loss, dy2, dy2b, dw_g, dw_p, vec_tail = _tail_fwd_bwd(
        xhat2, rstd2, p, target, wc["w_ple_gate"], wc["w_ple_in"], sm["ln2_g"], sm["ln2_b"],
        sm["b_ple_gate"], sm["ln3_g"], sm["ln3_b"])
    dap, dbl, dw_down, dconv = _ffn_bwd_act(dy2b, wc["w_ff_down"], a_pre, a, b_lin, hff, wa["conv_w8"])
    dw_a, dw_b = _ffn_bwd_w(dap, dbl, x1b)
    dx1_ffn = _ffn_bwd_x(dap, dbl, wa["w_ff_a"], wa["w_ff_b"])
    (dy1, d_attn, delta, d_gm, dw_o, vec_ln1), _ = _ln1_mix_bwd(
        dy2, dx1_ffn, xhat1, rstd1, sm["ln1_g"], attn, gm, wa["w_o"])
    early = {"w_ff_a": dw_a, "w_ff_b": dw_b, "w_ff_down": dw_down, "w_ple_gate": dw_g, "w_ple_in": dw_p,
             "w_o": dw_o}
    (du, dz, dws, dbs, vec_z), got = _gmlp_bwd(d_gm, u_pre, z_pre, sm["ln_z_g"], sm["ln_z_b"], sm["w_s"], bs_exp,
                                               ex.to_sibling(early))
    chip_sums = ex.reduce_on_chip(got)
    small = {"tail": vec_tail, "ln1": vec_ln1, "ln_z": vec_z, "conv": dconv, "w_s": dws, "b_s": dbs, "loss": loss}
    (dq, dk, dv), got_early = _attention_bwd(q, k, v, lse, d_attn, delta, tabs,
                                             ex.between_chips(chip_sums, small))
    parts = (dq, dk, dv, du, dz)
    (dw_in,), _ = _proj_in_bwd_w(xb, parts)
    (grad_x,), got_late = _proj_in_bwd_x(dy1, parts, w_in, ex.last(dw_in))
    return grad_x, ex.collect(got_early, got_late)


def _mesh_pos():
    return lax.axis_index("x"), lax.axis_index("y"), lax.axis_index("c")


def _cast_shards(shards):
    n = len(shards)

    def body(*refs):
        for a in range(n):
            refs[n + a][...] = refs[a][...].astype(BF16)

    whole = [_full_spec(s.shape) for s in shards]
    return pl.pallas_call(
        body, name="cast_shards", grid=(1,), out_shape=tuple(jax.ShapeDtypeStruct(s.shape, BF16) for s in shards),
        in_specs=whole, out_specs=tuple(whole), compiler_params=_params(("arbitrary",)),
    )(*shards)


class _GatherComm:
    def __init__(self, shards):
        n = len(shards)
        self.inputs = list(shards)
        self.out_shapes = [jax.ShapeDtypeStruct((N_DEV,) + s.shape, s.dtype) for s in shards]
        self.scratch = [pltpu.SemaphoreType.DMA((7 * n,)), pltpu.SemaphoreType.DMA((7 * n,)),
                        pltpu.SemaphoreType.DMA((n,))]

    def phases(self, x_refs, out_refs, sems):
        send_sems, recv_sems, local_sems = sems
        n_arr = len(x_refs)

        def where():
            x, y, c = _mesh_pos()
            return (x, y, c), (x, y, 1 - c), [(1 - x, y), (x, 1 - y), (1 - x, 1 - y)]

        def copy(a, n, block, to, from_shard=False):
            dst = out_refs[a].at[4 * block[0] + 2 * block[1] + block[2]]
            return pltpu.make_async_remote_copy(
                src_ref=x_refs[a] if from_shard else dst, dst_ref=dst, send_sem=send_sems.at[7 * a + n],
                recv_sem=recv_sems.at[7 * a + n], device_id=to, device_id_type=MESH)

        def local(a):
            x, y, c = _mesh_pos()
            return pltpu.make_async_copy(x_refs[a], out_refs[a].at[4 * x + 2 * y + c], local_sems.at[a])

        def start():
            me, sibling, chips = where()
            for a in range(n_arr):
                local(a).start()
                copy(a, 0, me, sibling, from_shard=True).start()
                for n, chip in enumerate(chips):
                    copy(a, 1 + n, me, (*chip, me[2]), from_shard=True).start()

        def forward():
            me, sibling, chips = where()
            for n, chip in enumerate(chips):
                for a in range(n_arr):
                    copy(a, 1 + n, (*chip, me[2]), me).wait_recv()
                    copy(a, 4 + n, (*chip, me[2]), sibling).start()

        def finish():
            me, sibling, chips = where()
            for a in range(n_arr):
                copy(a, 0, sibling, me).wait_recv()
                copy(a, 0, me, sibling, from_shard=True).wait_send()
                for n, chip in enumerate(chips):
                    copy(a, 4 + n, (*chip, 1 - me[2]), me).wait_recv()
                    copy(a, 1 + n, me, (*chip, me[2]), from_shard=True).wait_send()
                    copy(a, 4 + n, (*chip, me[2]), sibling).wait_send()
                local(a).wait()

        return {"start": start, "forward": forward, "finish": finish}


class _SiblingComm:
    def __init__(self, big):
        n = len(big)
        self.inputs = list(big)
        self.out_shapes = [jax.ShapeDtypeStruct(b.shape[1:], b.dtype) for b in big]
        self.scratch = [pltpu.SemaphoreType.DMA((n,)), pltpu.SemaphoreType.DMA((n,))]

    def phases(self, src, dst, sems):
        send_sems, recv_sems = sems

        def copies():
            x, y, c = _mesh_pos()
            return [pltpu.make_async_remote_copy(
                src_ref=src[a].at[1 - c], dst_ref=dst[a], send_sem=send_sems.at[a], recv_sem=recv_sems.at[a],
                device_id=(x, y, 1 - c), device_id_type=MESH) for a in range(len(src))]

        def start():
            for cp in copies():
                cp.start()

        def finish():
            for cp in copies():
                cp.wait()

        return {"start": start, "finish": finish}


class _ChipComm:
    def __init__(self, sums):
        n = len(sums)
        self.inputs = list(sums)
        self.out_shapes = [jax.ShapeDtypeStruct(s.shape, s.dtype) for s in sums]
        self.scratch = [pltpu.SemaphoreType.DMA((3 * n,)), pltpu.SemaphoreType.DMA((3 * n,)),
                        pltpu.SemaphoreType.DMA((n,))]

    def phases(self, src, dst, sems):
        send_sems, recv_sems, local_sems = sems

        def copies():
            x, y, c = _mesh_pos()
            my_chip = 2 * x + y
            out = [pltpu.make_async_copy(src[a].at[my_chip], dst[a].at[my_chip], local_sems.at[a])
                   for a in range(len(src))]
            for n, (px, py) in enumerate([(1 - x, y), (x, 1 - y), (1 - x, 1 - y)]):
                for a in range(len(src)):
                    out.append(pltpu.make_async_remote_copy(
                        src_ref=src[a].at[2 * px + py], dst_ref=dst[a].at[my_chip],
                        send_sem=send_sems.at[3 * a + n], recv_sem=recv_sems.at[3 * a + n],
                        device_id=(px, py, c), device_id_type=MESH))
            return out

        def start():
            for cp in copies():
                cp.start()

        def finish():
            for cp in copies():
                cp.wait()

        return {"start": start, "finish": finish}


class _ScatterComm:
    def __init__(self, blocks, small):
        self.n_big, self.n_small = len(blocks), len(small)
        n = self.n_big + self.n_small
        self.inputs = list(blocks) + list(small)
        self.out_shapes = ([jax.ShapeDtypeStruct(b.shape, b.dtype) for b in blocks]
                           + [jax.ShapeDtypeStruct((N_DEV,) + s.shape, s.dtype) for s in small])
        self.scratch = [pltpu.SemaphoreType.DMA((7 * n,)), pltpu.SemaphoreType.DMA((7 * n,)),
                        pltpu.SemaphoreType.DMA((n,))]

    def phases(self, src, dst, sems):
        send_sems, recv_sems, local_sems = sems
        n_big, n_all = self.n_big, self.n_big + self.n_small

        def source(a, core, chip):
            return src[a].at[core * 4 + chip] if a < n_big else src[a]

        def copies():
            x, y, c = _mesh_pos()
            me = 4 * x + 2 * y + c
            out = [pltpu.make_async_copy(source(a, c, 2 * x + y), dst[a].at[me], local_sems.at[a])
                   for a in range(n_all)]
            for flip in range(1, N_DEV):
                px = 1 - x if flip & 4 else x
                py = 1 - y if flip & 2 else y
                pc = 1 - c if flip & 1 else c
                for a in range(n_all):
                    n = 7 * a + flip - 1
                    out.append(pltpu.make_async_remote_copy(
                        src_ref=source(a, pc, 2 * px + py), dst_ref=dst[a].at[me], send_sem=send_sems.at[n],
                        recv_sem=recv_sems.at[n], device_id=(px, py, pc), device_id_type=MESH))
            return out

        def start():
            for cp in copies():
                cp.start()

        def finish():
            for cp in copies():
                cp.wait()

        return {"start": start, "finish": finish}


class _Both:
    def __init__(self, first, second):
        self.parts = (first, second)
        self.inputs = first.inputs + second.inputs
        self.out_shapes = first.out_shapes + second.out_shapes
        self.scratch = first.scratch + second.scratch

    def phases(self, src, dst, sems):
        a, b = self.parts
        pa = a.phases(src[:len(a.inputs)], dst[:len(a.out_shapes)], sems[:len(a.scratch)])
        pb = b.phases(src[len(a.inputs):], dst[len(a.out_shapes):], sems[len(a.scratch):])

        def both(name):
            def run():
                pa[name]()
                pb[name]()
            return run

        return {name: both(name) for name in pa}


def _host_call(body, carried, *, name, grid, out_shape, in_specs, out_specs, scratch_shapes, params, args):
    if carried is None:
        return pl.pallas_call(body, name=name, grid=grid, out_shape=tuple(out_shape), in_specs=list(in_specs),
                              out_specs=tuple(out_specs), scratch_shapes=list(scratch_shapes),
                              compiler_params=params)(*args), ()
    comm, when = carried
    n_in, n_out, n_scratch = len(in_specs), len(out_shape), len(scratch_shapes)
    k_in, k_out = len(comm.inputs), len(comm.out_shapes)

    def wrapped(*refs):
        bounds = np.cumsum([0, n_in, k_in, n_out, k_out, n_scratch])
        ins, c_in, outs, c_out, scr = (refs[bounds[i]:bounds[i + 1]] for i in range(5))
        phases = comm.phases(c_in, c_out, refs[bounds[5]:])
        for phase, cond in when("before"):
            pl.when(cond)(phases[phase])
        body(*ins, *outs, *scr)
        for phase, cond in when("after"):
            pl.when(cond)(phases[phase])

    anywhere = pl.BlockSpec(memory_space=pl.ANY)
    results = pl.pallas_call(
        wrapped, name=name, grid=grid, out_shape=tuple(out_shape) + tuple(comm.out_shapes),
        in_specs=list(in_specs) + [anywhere] * k_in, out_specs=tuple(out_specs) + (anywhere,) * k_out,
        scratch_shapes=list(scratch_shapes) + comm.scratch, compiler_params=params,
    )(*args, *comm.inputs)
    return results[:n_out], results[n_out:]


class _Exchanges:
    FIRST = ("w_o", "w_ff_a", "w_ff_b")
    SECOND = ("w_ff_down", "w_ple_gate", "w_ple_in")
    EARLY = ("w_ff_a", "w_ff_b", "w_ff_down", "w_ple_gate", "w_ple_in", "w_o")
    LATE = ("w_in",)

    def __init__(self, shards, conv_rows):
        self.shards, self.conv_rows = shards, conv_rows
        self.mode = {name: mode for name, _, mode in BIG}

    def _natural(self, name, blocks):
        n, r, c = blocks.shape
        return _to_natural(blocks, name + "_natural") if self.mode[name] == "cols" else blocks.reshape(n * r, c)

    def gather_input(self):
        def when(position):
            step = pl.program_id(0)
            if position == "before":
                return [("start", step == 0)]
            return [("forward", step == NT - 1), ("finish", step == NT - 1)]
        return _GatherComm([self.shards["w_in"]]), when

    def weight_input(self, got):
        return self._natural("w_in", got[0])

    def gather_first(self):
        comm = _GatherComm([self.shards[n] for n in self.FIRST] + [self.conv_rows])

        def when(position):
            step = pl.program_id(0)
            if position == "before":
                return [("start", step == 0), ("forward", step == 3)]
            return [("finish", step == 3)]
        return comm, when

    def weights_first(self, got):
        out = {name: self._natural(name, blocks) for name, blocks in zip(self.FIRST, got)}
        out["conv_w8"] = _to_natural(got[-1], "conv_w_natural")
        return out

    def gather_second(self):
        comm = _GatherComm([self.shards[n] for n in self.SECOND])

        def when(position):
            j, i = pl.program_id(0), pl.program_id(1)
            if position == "before":
                return [("start", (j == 0) & (i == 0)), ("forward", (j == NJ - 1) & (i == NT // 2))]
            return [("finish", (j == NJ - 1) & (i == NT - 1))]
        return comm, when

    def weights_second(self, got):
        return {name: self._natural(name, blocks) for name, blocks in zip(self.SECOND, got)}

    def to_sibling(self, early):
        self.by_core = [early[n].reshape((2, 4) + early[n].shape[1:]) for n in self.EARLY]
        return _SiblingComm(self.by_core), _first_and_last(NT)

    def reduce_on_chip(self, from_sibling):
        core = lax.axis_index("c").astype(jnp.int32).reshape(1)
        return _chip_reduce(self.by_core, from_sibling, core, "chip_reduce")

    def between_chips(self, chip_sums, small):
        self.small_keys = tuple(small)
        return _Both(_ChipComm(chip_sums), _ScatterComm([], [small[k] for k in self.small_keys])), _first_and_last(4)

    def last(self, dw_in):
        by_core = [dw_in.reshape((2, 4) + dw_in.shape[1:])]
        core = lax.axis_index("c").astype(jnp.int32).reshape(1)
        sums = _chip_reduce(by_core, _standalone(_SiblingComm(by_core), "w_in_grad_to_sibling"), core, "w_in_chip_reduce")
        return _ChipComm(sums), _first_and_last(NT)

    def collect(self, got_early, got_late):
        parts = dict(zip(self.EARLY, got_early[:len(self.EARLY)]))
        parts.update(zip(self.LATE, got_late))
        return parts, dict(zip(self.small_keys, got_early[len(self.EARLY):]))


def _first_and_last(n_steps):
    def when(position):
        step = pl.program_id(0)
        return [("start", step == 0)] if position == "before" else [("finish", step == n_steps - 1)]
    return when


def _standalone(comm, name):
    n_in = len(comm.inputs)

    def body(*refs):
        phases = comm.phases(refs[:n_in], refs[n_in:n_in + len(comm.out_shapes)], refs[n_in + len(comm.out_shapes):])
        phases["start"]()
        phases["finish"]()

    anywhere = pl.BlockSpec(memory_space=pl.ANY)
    return pl.pallas_call(
        body, name=name, out_shape=tuple(comm.out_shapes), in_specs=[anywhere] * n_in,
        out_specs=(anywhere,) * len(comm.out_shapes), scratch_shapes=comm.scratch,
    )(*comm.inputs)


def _chip_reduce(big, from_sibling, core, name):
    n = len(big)

    def body(core_ref, *refs):
        for a in range(n):
            mine, theirs, out = refs[a], refs[n + a], refs[2 * n + a]
            out[0] = (mine[0, 0].astype(F32) + theirs[0].astype(F32)).astype(BF16)

    def block(shape):
        return pl.BlockSpec((1,) + shape, lambda ch, core_ref: (ch, 0, 0))

    grid_spec = pltpu.PrefetchScalarGridSpec(
        num_scalar_prefetch=1, grid=(4,),
        in_specs=[pl.BlockSpec((1, 1) + b.shape[2:], lambda ch, core_ref: (core_ref[0], ch, 0, 0)) for b in big]
        + [block(b.shape[2:]) for b in big],
        out_specs=[block(b.shape[2:]) for b in big])
    return pl.pallas_call(
        body, name=name, grid_spec=grid_spec,
        out_shape=tuple(jax.ShapeDtypeStruct(b.shape[1:], BF16) for b in big),
        compiler_params=_params(("parallel",)),
    )(core, *big, *from_sibling)


def _adamw(g, w, m, v):
    nm = ADAM_B1 * m + (1.0 - ADAM_B1) * g
    nv = ADAM_B2 * v + (1.0 - ADAM_B2) * (g * g)
    m_hat = nm / (1.0 - ADAM_B1 ** ADAM_STEP)
    v_hat = nv / (1.0 - ADAM_B2 ** ADAM_STEP)
    return -ADAM_LR * (m_hat / (jnp.sqrt(v_hat) + ADAM_EPS) + ADAM_WD * w), nm, nv


def _adamw_sharded(parts, w, m, v, name):
    def body(p_ref, w_ref, m_ref, v_ref, g_ref, d_ref, nm_ref, nv_ref):
        g = p_ref[0].astype(F32)
        for s in range(1, parts.shape[0]):
            g = g + p_ref[s].astype(F32)
        delta, nm, nv = _adamw(g, w_ref[0], m_ref[0], v_ref[0])
        g_ref[0] = g
        d_ref[0] = delta
        nm_ref[0] = nm
        nv_ref[0] = nv

    n, r, c = parts.shape
    steps = 4 if r % 64 == 0 and r >= 512 else (2 if r % 32 == 0 and r >= 256 else 1)
    tile = pl.BlockSpec((1, r // steps, c), lambda i: (0, i, 0))
    return pl.pallas_call(
        body, name=name, grid=(steps,), out_shape=(jax.ShapeDtypeStruct(w.shape, F32),) * 4,
        in_specs=[pl.BlockSpec((n, r // steps, c), lambda i: (0, i, 0)), tile, tile, tile], out_specs=(tile,) * 4,
        compiler_params=_params(("parallel",)),
    )(parts, w, m, v)


REPLICATED = (("ln_z_g", "ln_z", 0), ("ln_z_b", "ln_z", 1), ("w_s", "w_s", None), ("b_s", "b_s", None),
              ("ln1_g", "ln1", 0), ("ln1_b", "ln1", 1), ("conv_w", "conv_mine", None), ("conv_b", "conv", 3),
              ("ln2_g", "tail", 3), ("ln2_b", "tail", 4), ("b_ple_gate", "tail", 0), ("ln3_g", "tail", 1),
              ("ln3_b", "tail", 2))
GATHERED = ("tail", "ln1", "ln_z", "conv", "w_s", "b_s", "loss", "conv_mine")


def _adamw_replicated(gathered, w, m, v):
    n_par = len(REPLICATED)

    def body(*refs):
        srcs = dict(zip(GATHERED, refs[:len(GATHERED)]))
        rest = refs[len(GATHERED):]
        w_refs, m_refs, v_refs = rest[:n_par], rest[n_par:2 * n_par], rest[2 * n_par:3 * n_par]
        outs = rest[3 * n_par:]
        loss_ref = outs[4 * n_par]
        sums = {}
        for key, ref in srcs.items():
            total = ref[0]
            for dev in range(1, N_DEV):
                total = total + ref[dev]
            sums[key] = total
        loss_ref[...] = sums["loss"]
        for n, (name, key, row) in enumerate(REPLICATED):
            if name == "conv_w":
                g = sums[key][0:3, :]
            elif row is None:
                g = sums[key]
            else:
                g = sums[key][row:row + 1, :]
            lead = len(w_refs[n].shape) - g.ndim
            idx = (0,) * lead + (Ellipsis,)
            delta, nm, nv = _adamw(g, w_refs[n][idx], m_refs[n][idx], v_refs[n][idx])
            for kind, val in enumerate((g, delta, nm, nv)):
                outs[kind * n_par + n][idx] = val

    names = [name for name, _, _ in REPLICATED]
    shapes = [jax.ShapeDtypeStruct(w[name].shape, F32) for name in names]
    args = [gathered[k] for k in GATHERED] + [w[n] for n in names] + [m[n] for n in names] + [v[n] for n in names]
    out_shape = tuple(shapes * 4) + (jax.ShapeDtypeStruct((8, LANES), F32),)
    return pl.pallas_call(
        body, name="adamw_replicated", grid=(1,), out_shape=out_shape,
        in_specs=[_full_spec(a.shape) for a in args], out_specs=tuple(_full_spec(s.shape) for s in out_shape),
        compiler_params=_params(("arbitrary",)),
    )(*args)


def kernel(x, p, positions, w_in, ln_z_g, ln_z_b, w_s, b_s, w_o, ln1_g, ln1_b, w_ff_a, w_ff_b, conv_w, conv_b, w_ff_down, ln2_g, ln2_b, w_ple_gate, b_ple_gate, w_ple_in, ln3_g, ln3_b, loss_target, m_w_in, m_ln_z_g, m_ln_z_b, m_w_s, m_b_s, m_w_o, m_ln1_g, m_ln1_b, m_w_ff_a, m_w_ff_b, m_conv_w, m_conv_b, m_w_ff_down, m_ln2_g, m_ln2_b, m_w_ple_gate, m_b_ple_gate, m_w_ple_in, m_ln3_g, m_ln3_b, v_w_in, v_ln_z_g, v_ln_z_b, v_w_s, v_b_s, v_w_o, v_ln1_g, v_ln1_b, v_w_ff_a, v_w_ff_b, v_conv_w, v_conv_b, v_w_ff_down, v_ln2_g, v_ln2_b, v_w_ple_gate, v_b_ple_gate, v_w_ple_in, v_ln3_g, v_ln3_b):
    w = dict(w_in=w_in, ln_z_g=ln_z_g, ln_z_b=ln_z_b, w_s=w_s, b_s=b_s, w_o=w_o, ln1_g=ln1_g, ln1_b=ln1_b,
             w_ff_a=w_ff_a, w_ff_b=w_ff_b, conv_w=conv_w, conv_b=conv_b, w_ff_down=w_ff_down, ln2_g=ln2_g,
             ln2_b=ln2_b, w_ple_gate=w_ple_gate, b_ple_gate=b_ple_gate, w_ple_in=w_ple_in, ln3_g=ln3_g,
             ln3_b=ln3_b)
    m = dict(w_in=m_w_in, ln_z_g=m_ln_z_g, ln_z_b=m_ln_z_b, w_s=m_w_s, b_s=m_b_s, w_o=m_w_o, ln1_g=m_ln1_g,
             ln1_b=m_ln1_b, w_ff_a=m_w_ff_a, w_ff_b=m_w_ff_b, conv_w=m_conv_w, conv_b=m_conv_b,
             w_ff_down=m_w_ff_down, ln2_g=m_ln2_g, ln2_b=m_ln2_b, w_ple_gate=m_w_ple_gate,
             b_ple_gate=m_b_ple_gate, w_ple_in=m_w_ple_in, ln3_g=m_ln3_g, ln3_b=m_ln3_b)
    v = dict(w_in=v_w_in, ln_z_g=v_ln_z_g, ln_z_b=v_ln_z_b, w_s=v_w_s, b_s=v_b_s, w_o=v_w_o, ln1_g=v_ln1_g,
             ln1_b=v_ln1_b, w_ff_a=v_w_ff_a, w_ff_b=v_w_ff_b, conv_w=v_conv_w, conv_b=v_conv_b,
             w_ff_down=v_w_ff_down, ln2_g=v_ln2_g, ln2_b=v_ln2_b, w_ple_gate=v_w_ple_gate,
             b_ple_gate=v_b_ple_gate, w_ple_in=v_w_ple_in, ln3_g=v_ln3_g, ln3_b=v_ln3_b)
    big_names = [name for name, _, _ in BIG]
    small_names = ("ln_z_g", "ln_z_b", "w_s", "b_s", "ln1_g", "ln1_b", "conv_b", "ln2_g", "ln2_b", "b_ple_gate",
                   "ln3_g", "ln3_b")

    transposed = {name for name, _, mode in BIG if mode == "rows_t"}

    def travel(a, name):
        return jnp.swapaxes(a, 1, 2) if name in transposed else a

    shards = dict(zip(big_names, _cast_shards([travel(w[n], n)[0] for n in big_names])))
    conv_rows = jnp.pad(w["conv_w"][0], ((0, 5), (0, 0)))
    sm = {n: w[n][0] if w[n].ndim > 2 else w[n] for n in small_names}
    pos_col = positions.reshape(S, 1).astype(F32)
    grad_x, (parts, small_all) = _local_step(x[0], p[0, 0], pos_col, loss_target[0], sm,
                                             _Exchanges(shards, conv_rows))
    me = 4 * lax.axis_index("x") + 2 * lax.axis_index("y") + lax.axis_index("c")
    conv_cols = small_all["conv"].reshape(N_DEV, 8, N_DEV, D_FF // N_DEV)
    small_all["conv_mine"] = lax.dynamic_index_in_dim(conv_cols, me, axis=2, keepdims=False)

    leaves = {}
    for name in big_names:
        outs = _adamw_sharded(parts[name], travel(w[name], name), travel(m[name], name), travel(v[name], name),
                              "adamw_" + name)
        leaves[name] = tuple(travel(o, name) for o in outs)
    rep = _adamw_replicated(small_all, w, m, v)
    n_rep = len(REPLICATED)
    for n, (name, _, _) in enumerate(REPLICATED):
        leaves[name] = tuple(rep[kind * n_rep + n] for kind in range(4))
    loss = rep[4 * n_rep][0, 0]
    return (loss, grad_x[None], *[leaves[n][kind] for kind in range(4) for n in WEIGHT_ORDER])
```

```python
import math

import numpy as np
import jax
import jax.numpy as jnp
from jax import lax
from jax.experimental import pallas as pl
from jax.experimental.pallas import tpu as pltpu

F32 = jnp.float32
BF16 = jnp.bfloat16
MESH = pl.DeviceIdType.MESH

N_DEV = 8
S = 4096
D = 1024
D_HALF = 512
D_IN = 2560
D_FF = 2816
D_PLE = 256
CHUNK = 128
DILATIONS = ((1, 32), (4, 8), (16, 2))
ROPE_THETA = 500000.0
LN_EPS = 1e-5
ALPHA = 2.0 ** 0.25
NEG_INF = -1e30
INV_SQRT2 = 1.0 / math.sqrt(2.0)
INV_SQRT_2PI = 1.0 / math.sqrt(2.0 * math.pi)

ADAM_LR, ADAM_B1, ADAM_B2, ADAM_EPS, ADAM_WD, ADAM_STEP = 0.001, 0.9, 0.999, 1e-08, 0.01, 10

TM = 512
NT = S // TM
ATTN_UNROLL = 8
TN = 1408
NJ = D_FF // TN
LANES = 128
VMEM_MIB = 1024 * 1024

BIG = (("w_in", (320, 1024), "rows_t"), ("w_o", (128, 1024), "rows"), ("w_ff_a", (352, 1024), "rows_t"),
       ("w_ff_b", (352, 1024), "rows_t"), ("w_ff_down", (352, 1024), "rows"), ("w_ple_gate", (128, 1024), "rows"),
       ("w_ple_in", (256, 128), "cols"))
WEIGHT_ORDER = ("w_in", "ln_z_g", "ln_z_b", "w_s", "b_s", "w_o", "ln1_g", "ln1_b", "w_ff_a", "w_ff_b",
                "conv_w", "conv_b", "w_ff_down", "ln2_g", "ln2_b", "w_ple_gate", "b_ple_gate",
                "w_ple_in", "ln3_g", "ln3_b")


def _params(semantics=None, vmem_mib=48):
    return pltpu.CompilerParams(dimension_semantics=semantics, vmem_limit_bytes=vmem_mib * VMEM_MIB)


def _dot(a, b):
    return jnp.dot(a, b, preferred_element_type=F32)


def _dot_nt(a, b):
    return lax.dot_general(a, b, (((1,), (1,)), ((), ())), preferred_element_type=F32)


def _dot_tn(a, b):
    return lax.dot_general(a, b, (((0,), (0,)), ((), ())), preferred_element_type=F32)


def _gelu(x):
    return 0.5 * x * (1.0 + lax.erf(x * INV_SQRT2))


def _gelu_and_grad(x):
    cdf = 0.5 * (1.0 + lax.erf(x * INV_SQRT2))
    return x * cdf, cdf + x * (jnp.exp(-0.5 * x * x) * INV_SQRT_2PI)


def _ln_stats(y):
    mu = jnp.mean(y, axis=-1, keepdims=True)
    yc = y - mu
    var = jnp.mean(yc * yc, axis=-1, keepdims=True)
    rstd = lax.rsqrt(var + LN_EPS)
    return yc * rstd, rstd


def _ln_bwd(dxhat, xhat, rstd):
    m1 = jnp.mean(dxhat, axis=-1, keepdims=True)
    m2 = jnp.mean(dxhat * xhat, axis=-1, keepdims=True)
    return rstd * (dxhat - m1 - xhat * m2)


def _colsum(x):
    return jnp.sum(x, axis=0, keepdims=True)


def _rows(i):
    return (i, 0)


def _row_spec(width):
    return pl.BlockSpec((TM, width), _rows)


def _full_spec(shape):
    return pl.BlockSpec(shape, lambda *_: (0,) * len(shape))


def _owner_slot(j):
    return (j % 2) * 4 + j // 2


def _store_owner_blocks(blocks_ref, acc_ref, tile):
    rows = TN // 4
    for t in range(4):
        blocks_ref[(t % 2) * 4 + 2 * tile + t // 2] = acc_ref[rows * t:rows * (t + 1), :].astype(BF16)


def _lane_lo():
    return lax.broadcasted_iota(jnp.int32, (CHUNK, LANES), 1) < 64


def _tril():
    r = lax.broadcasted_iota(jnp.int32, (CHUNK, CHUNK), 0)
    c = lax.broadcasted_iota(jnp.int32, (CHUNK, CHUNK), 1)
    return c <= r


def _rope_consts():
    lane = np.arange(LANES) % 64
    j = lane % 8
    inv = np.where(lane < 16, np.float32(ROPE_THETA) ** (-(2.0 * j).astype(np.float32) / np.float32(16.0)), 0.0)
    m_lo = (lane < 8).astype(np.float32)
    m_hi = ((lane >= 8) & (lane < 16)).astype(np.float32)
    return (jnp.asarray(inv, F32).reshape(1, LANES), jnp.asarray(m_lo).reshape(1, LANES),
            jnp.asarray(m_hi).reshape(1, LANES))


def _rope_tables(pos_col, carried=None):
    inv, m_lo, m_hi = _rope_consts()

    def body(pos_ref, inv_ref, lo_ref, hi_ref, c_ref, sa_ref, sb_ref):
        ang = pos_ref[...] * inv_ref[...]
        c = jnp.cos(ang)
        s = jnp.sin(ang)
        lo = lo_ref[...]
        hi = hi_ref[...]
        c_ref[...] = jnp.where(lo + hi > 0.0, c, 1.0)
        sa_ref[...] = s * hi
        sb_ref[...] = -s * lo

    vec = _full_spec((1, LANES))
    out = jax.ShapeDtypeStruct((S, LANES), F32)
    return _host_call(
        body, carried, name="rope_tables", grid=(NT,), out_shape=(out, out, out),
        in_specs=[pl.BlockSpec((TM, 1), _rows), vec, vec, vec],
        out_specs=(_row_spec(LANES),) * 3, scratch_shapes=[], params=_params(("arbitrary",)),
        args=(pos_col, inv, m_lo, m_hi))


def _rope(t, c, sa, sb):
    return t * c + pltpu.roll(t, 8, 1) * sa + pltpu.roll(t, LANES - 8, 1) * sb


def _rope_t(dy, c, sa, sb):
    return dy * c + pltpu.roll(dy * sa, LANES - 8, 1) + pltpu.roll(dy * sb, 8, 1)


def _masked_ws(ws_ref):
    tril = _tril()
    return [jnp.where(tril, ws_ref[g], 0.0).astype(BF16) for g in range(8)]


def _spatial_mix(zn, wm, bs):
    lo = _lane_lo()
    rows = []
    for ch in range(TM // CHUNK):
        slabs = []
        for pr in range(4):
            zp = zn[ch * CHUNK:(ch + 1) * CHUNK, pr * LANES:(pr + 1) * LANES].astype(BF16)
            slabs.append(jnp.where(lo, _dot(wm[2 * pr], zp), _dot(wm[2 * pr + 1], zp)))
        rows.append(jnp.concatenate(slabs, axis=1) + bs)
    return jnp.concatenate(rows, axis=0)


def _proj_in_fwd(x, w_in, tabs, ln_z_g, ln_z_b, w_s, bs_exp):
    def body(x_ref, w_ref, c_ref, sa_ref, sb_ref, g_ref, b_ref, ws_ref, bs_ref,
             q_ref, k_ref, v_ref, u_ref, z_ref, gm_ref, xb_ref):
        xb = x_ref[...].astype(BF16)
        xb_ref[...] = xb
        c, sa, sb = c_ref[...], sa_ref[...], sb_ref[...]
        hq = _dot_nt(xb, w_ref[0:512, :])
        hk = _dot_nt(xb, w_ref[512:1024, :])
        for s in range(4):
            sl = slice(s * LANES, (s + 1) * LANES)
            q_ref[:, sl] = _rope(hq[:, sl], c, sa, sb)
            k_ref[:, sl] = _rope(hk[:, sl], c, sa, sb)
        v_ref[...] = _dot_nt(xb, w_ref[1024:1536, :])
        u_pre = _dot_nt(xb, w_ref[1536:2048, :])
        z_pre = _dot_nt(xb, w_ref[2048:2560, :])
        u_ref[...] = u_pre
        z_ref[...] = z_pre
        zhat, _ = _ln_stats(_gelu(z_pre))
        zn = zhat * g_ref[...] + b_ref[...]
        mixed = _spatial_mix(zn, _masked_ws(ws_ref), bs_ref[...])
        gm_ref[...] = (_gelu(u_pre) * mixed).astype(BF16)

    half = jax.ShapeDtypeStruct((S, D_HALF), F32)
    tab = _row_spec(LANES)
    return pl.pallas_call(
        body, name="proj_in_fwd", grid=(NT,),
        out_shape=(half, half, half, half, half, jax.ShapeDtypeStruct((S, D_HALF), BF16),
                   jax.ShapeDtypeStruct((S, D), BF16)),
        in_specs=[_row_spec(D), _full_spec((D_IN, D)), tab, tab, tab, _full_spec((1, D_HALF)),
                  _full_spec((1, D_HALF)), _full_spec((8, CHUNK, CHUNK)), _full_spec((CHUNK, D_HALF))],
        out_specs=(_row_spec(D_HALF),) * 6 + (_row_spec(D),),
        compiler_params=_params(("parallel",)),
    )(x, w_in, *tabs, ln_z_g, ln_z_b, w_s, bs_exp)


def _store_band_bias(bias_ref):
    qi = lax.broadcasted_iota(jnp.int32, (CHUNK, 2 * CHUNK), 0)
    kj = lax.broadcasted_iota(jnp.int32, (CHUNK, 2 * CHUNK), 1)
    band = (kj >= qi) & (kj <= qi + CHUNK)
    bias_ref[0] = jnp.where(band, 0.0, NEG_INF)
    bias_ref[1] = jnp.where(band & (kj >= CHUNK), 0.0, NEG_INF)


PERM_ROWS = 256


def _for_permuted_chunks(d, fn):
    length = S // d
    per_residue = length // PERM_ROWS

    class Rows:
        def __init__(self, start):
            self.index = (pl.ds(pl.multiple_of(start, PERM_ROWS), PERM_ROWS) if d == 1
                          else pl.ds(start, PERM_ROWS, stride=d))

        def __call__(self, ref):
            return ref[self.index, :]

        def store(self, ref, value):
            ref[self.index, :] = value

    def step(n, carry):
        r, c = n // per_residue, n % per_residue
        fn(Rows(r + d * c * PERM_ROWS), pl.multiple_of(r * length + c * PERM_ROWS, PERM_ROWS))
        return carry

    lax.fori_loop(0, S // PERM_ROWS, step, 0)


def _attention_fwd(q, k, v, carried=None):
    def body(q_ref, k_ref, v_ref, o_ref, lse_ref, qb, kb, v0b, v1b, bias, op, lp, ob0, lb0, ob1, lb1, ob2, lb2):
        lo = _lane_lo()
        lo_f = lo.astype(F32)[0:1, :]
        hi_f = 1.0 - lo_f
        zero_pad = jnp.zeros((CHUNK, LANES), BF16)
        for buf in (qb, kb, v0b, v1b):
            buf[0:CHUNK, :] = zero_pad
        _store_band_bias(bias)
        outs = ((ob0, lb0), (ob1, lb1), (ob2, lb2))
        for (d, nb), (ob, lb) in zip(DILATIONS, outs):
            def build(orig, perm):
                dst = pl.ds(pl.multiple_of(perm + CHUNK, CHUNK), PERM_ROWS)
                qb[dst, :] = (orig(q_ref) * 0.125).astype(BF16)
                kb[dst, :] = orig(k_ref).astype(BF16)
                vs = orig(v_ref)
                v0b[dst, :] = (vs * lo_f + hi_f).astype(BF16)
                v1b[dst, :] = (vs * hi_f + lo_f).astype(BF16)

            _for_permuted_chunks(d, build)

            def block(b, carry, nb=nb):
                base = pl.multiple_of(b * CHUNK, CHUNK)
                add = bias[jnp.where(b % nb == 0, 1, 0)]
                qblk = qb[pl.ds(pl.multiple_of(base + CHUNK, CHUNK), CHUNK), :]
                kblk = kb[pl.ds(base, 2 * CHUNK), :]
                q2 = jnp.concatenate([jnp.where(lo, qblk, 0), jnp.where(lo, 0, qblk)], axis=0)
                s2 = _dot_nt(q2, kblk) + jnp.concatenate([add, add], axis=0)
                m2 = jnp.max(s2, axis=-1, keepdims=True)
                p2 = jnp.exp(s2 - m2).astype(BF16)
                pv, mx = [], []
                for head, vh in enumerate((v0b, v1b)):
                    rows = slice(head * CHUNK, (head + 1) * CHUNK)
                    pv.append(_dot(p2[rows, :], vh[pl.ds(base, 2 * CHUNK), :]))
                    mx.append(m2[rows, :])
                den = pltpu.roll(jnp.where(lo, pv[1], pv[0]), 64, 1)
                op[pl.ds(base, CHUNK), :] = jnp.where(lo, pv[0], pv[1]) / den
                lp[pl.ds(base, CHUNK), :] = jnp.where(lo, mx[0], mx[1]) + jnp.log(den)
                return carry

            lax.fori_loop(0, S // CHUNK, block, 0, unroll=ATTN_UNROLL)

            def restore(orig, perm, ob=ob, lb=lb):
                orig.store(ob, op[pl.ds(perm, PERM_ROWS), :])
                orig.store(lb, lp[pl.ds(perm, PERM_ROWS), :])

            _for_permuted_chunks(d, restore)

        def combine(t, carry):
            rows = pl.ds(pl.multiple_of(t * PERM_ROWS, PERM_ROWS), PERM_ROWS)
            l0, l1, l2 = lb0[rows, :], lb1[rows, :], lb2[rows, :]
            mx = jnp.maximum(jnp.maximum(l0, l1), l2)
            e0, e1, e2 = jnp.exp(l0 - mx), jnp.exp(l1 - mx), jnp.exp(l2 - mx)
            den = e0 + e1 + e2
            o_ref[rows, :] = (e0 * ob0[rows, :] + e1 * ob1[rows, :] + e2 * ob2[rows, :]) / den
            lse_ref[rows, :] = mx + jnp.log(den)
            return carry

        lax.fori_loop(0, S // PERM_ROWS, combine, 0)

    slab = pl.BlockSpec((S, LANES), lambda h: (0, h))
    out = jax.ShapeDtypeStruct((S, D_HALF), F32)
    padded = pltpu.VMEM((CHUNK + S, LANES), BF16)
    whole = pltpu.VMEM((S, LANES), F32)
    return _host_call(
        body, carried, name="attention_fwd", grid=(4,), out_shape=(out, out),
        in_specs=[slab, slab, slab], out_specs=(slab, slab),
        scratch_shapes=[padded] * 4 + [pltpu.VMEM((2, CHUNK, 2 * CHUNK), F32)] + [whole] * 8,
        params=_params(("arbitrary",), 56), args=(q, k, v))


def _mix_ln1_fwd(attn, gm, w_o, x, g1, b1):
    def body(a_ref, gm_ref, w_ref, x_ref, g_ref, b_ref, xhat_ref, rstd_ref, x1b_ref):
        mix = _dot(a_ref[...].astype(BF16), w_ref[0:D_HALF, :]) + _dot(gm_ref[...], w_ref[D_HALF:D, :])
        xhat, rstd = _ln_stats(ALPHA * x_ref[...] + mix)
        xhat_ref[...] = xhat
        rstd_ref[...] = rstd
        x1b_ref[...] = (xhat * g_ref[...] + b_ref[...]).astype(BF16)

    vec = _full_spec((1, D))
    return pl.pallas_call(
        body, name="mix_ln1_fwd", grid=(NT,),
        out_shape=(jax.ShapeDtypeStruct((S, D), F32), jax.ShapeDtypeStruct((S, 1), F32),
                   jax.ShapeDtypeStruct((S, D), BF16)),
        in_specs=[_row_spec(D_HALF), _row_spec(D_HALF), _full_spec((D, D)), _row_spec(D), vec, vec],
        out_specs=(_row_spec(D), pl.BlockSpec((TM, 1), _rows), _row_spec(D)),
        compiler_params=_params(("parallel",)),
    )(attn, gm, w_o, x, g1, b1)


def _ffn_up_fwd(x1b, w_a, w_b, conv_w8, conv_b, carried=None):
    def body(x_ref, wa_ref, wb_ref, cw_ref, cb_ref, ap_ref, a_ref, bl_ref, h_ref, carry):
        @pl.when(pl.program_id(1) == 0)
        def _():
            carry[...] = jnp.zeros_like(carry)

        xb = x_ref[...]
        ap = _dot_nt(xb, wa_ref[...])
        bl = _dot_nt(xb, wb_ref[...])
        row = lax.broadcasted_iota(jnp.int32, (TM, TN), 0)
        c6, c7 = carry[6:7, :], carry[7:8, :]
        m1 = jnp.where(row == 0, c7, pltpu.roll(ap, 1, 0))
        m2 = jnp.where(row == 0, c6, jnp.where(row == 1, c7, pltpu.roll(ap, 2, 0)))
        a = cb_ref[...] + cw_ref[0:1, :] * m2 + cw_ref[1:2, :] * m1 + cw_ref[2:3, :] * ap
        carry[...] = ap[TM - 8:TM, :]
        ap_ref[...] = ap.astype(BF16)
        a_ref[...] = a
        bl_ref[...] = bl.astype(BF16)
        h_ref[...] = (_gelu(a) * bl).astype(BF16)

    tile = pl.BlockSpec((TM, TN), lambda j, i: (i, j))
    wcol = pl.BlockSpec((TN, D), lambda j, i: (j, 0))
    ff = jax.ShapeDtypeStruct((S, D_FF), F32)
    ffb = jax.ShapeDtypeStruct((S, D_FF), BF16)
    return _host_call(
        body, carried, name="ffn_up_fwd", grid=(NJ, NT),
        out_shape=(ffb, ff, ffb, ffb),
        in_specs=[pl.BlockSpec((TM, D), lambda j, i: (i, 0)), wcol, wcol,
                  pl.BlockSpec((8, TN), lambda j, i: (0, j)), pl.BlockSpec((1, TN), lambda j, i: (0, j))],
        out_specs=(tile, tile, tile, tile),
        scratch_shapes=[pltpu.VMEM((8, TN), F32)],
        params=_params(("arbitrary", "arbitrary"), 56), args=(x1b, w_a, w_b, conv_w8, conv_b))


def _ffn_down_ln2_fwd(hff, w_down, xhat1, g1, b1):
    def body(h_ref, w_ref, xh_ref, g_ref, b_ref, xhat_ref, rstd_ref):
        x1 = xh_ref[...] * g_ref[...] + b_ref[...]
        xhat, rstd = _ln_stats(ALPHA * x1 + _dot(h_ref[...], w_ref[...]))
        xhat_ref[...] = xhat
        rstd_ref[...] = rstd

    vec = _full_spec((1, D))
    return pl.pallas_call(
        body, name="ffn_down_ln2_fwd", grid=(NT,),
        out_shape=(jax.ShapeDtypeStruct((S, D), F32), jax.ShapeDtypeStruct((S, 1), F32)),
        in_specs=[_row_spec(D_FF), _full_spec((D_FF, D)), _row_spec(D), vec, vec],
        out_specs=(_row_spec(D), pl.BlockSpec((TM, 1), _rows)),
        compiler_params=_params(("parallel",)),
    )(hff, w_down, xhat1, g1, b1)


def _tail_fwd_bwd(xhat2, rstd2, p, target, w_g, w_p, g2, b2, bg, g3, b3):
    def body(xh_ref, rs_ref, p_ref, t_ref, wg_ref, wp_ref, g2_ref, b2_ref, bg_ref, g3_ref, b3_ref,
             loss_ref, dy2_ref, dy2b_ref, gwg_ref, gwp_ref, vec_ref, dwg_ref, dwp_ref):
        @pl.when(pl.program_id(0) == 0)
        def _():
            loss_ref[...] = jnp.zeros_like(loss_ref)
            dwg_ref[...] = jnp.zeros_like(dwg_ref)
            dwp_ref[...] = jnp.zeros_like(dwp_ref)
            vec_ref[...] = jnp.zeros_like(vec_ref)

        xhat2_t = xh_ref[...]
        x2 = xhat2_t * g2_ref[...] + b2_ref[...]
        x2b = x2.astype(BF16)
        pb = p_ref[...].astype(BF16)
        gate = jax.nn.sigmoid(_dot(x2b, wg_ref[...]) + bg_ref[...])
        pin = _dot(pb, wp_ref[...])
        xhat3, rstd3 = _ln_stats(ALPHA * x2 + gate * pin)
        err = xhat3 * g3_ref[...] + b3_ref[...] - t_ref[...]
        loss_ref[...] += jnp.sum(jnp.mean(err * err, axis=-1, keepdims=True), axis=0, keepdims=True) * 0.5
        dout = err * (1.0 / D)
        dy3 = _ln_bwd(dout * g3_ref[...], xhat3, rstd3)
        dgp = dy3 * pin * gate * (1.0 - gate)
        dgpb = dgp.astype(BF16)
        dwg_ref[...] += _dot_tn(x2b, dgpb)
        dwp_ref[...] += _dot_tn(pb, (dy3 * gate).astype(BF16))
        dx2 = ALPHA * dy3 + _dot_nt(dgpb, wg_ref[...])
        dy2 = _ln_bwd(dx2 * g2_ref[...], xhat2_t, rs_ref[...])
        dy2_ref[...] = dy2
        dy2b_ref[...] = dy2.astype(BF16)
        vec_ref[0:1, :] += _colsum(dgp)
        vec_ref[1:2, :] += _colsum(dout * xhat3)
        vec_ref[2:3, :] += _colsum(dout)
        vec_ref[3:4, :] += _colsum(dx2 * xhat2_t)
        vec_ref[4:5, :] += _colsum(dx2)

        @pl.when(pl.program_id(0) == NT - 1)
        def _():
            for j in range(N_DEV):
                gwg_ref[_owner_slot(j)] = dwg_ref[LANES * j:LANES * (j + 1), :].astype(BF16)
                gwp_ref[_owner_slot(j)] = dwp_ref[:, LANES * j:LANES * (j + 1)].astype(BF16)

    vec = _full_spec((1, D))
    return pl.pallas_call(
        body, name="tail_fwd_bwd", grid=(NT,),
        out_shape=(jax.ShapeDtypeStruct((8, LANES), F32), jax.ShapeDtypeStruct((S, D), F32),
                   jax.ShapeDtypeStruct((S, D), BF16), jax.ShapeDtypeStruct((N_DEV, D // N_DEV, D), BF16),
                   jax.ShapeDtypeStruct((N_DEV, D_PLE, D // N_DEV), BF16), jax.ShapeDtypeStruct((8, D), F32)),
        in_specs=[_row_spec(D), pl.BlockSpec((TM, 1), _rows), _row_spec(D_PLE), _row_spec(D),
                  _full_spec((D, D)), _full_spec((D_PLE, D)), vec, vec, vec, vec, vec],
        out_specs=(_full_spec((8, LANES)), _row_spec(D), _row_spec(D), _full_spec((N_DEV, D // N_DEV, D)),
                   _full_spec((N_DEV, D_PLE, D // N_DEV)), _full_spec((8, D))),
        scratch_shapes=[pltpu.VMEM((D, D), F32), pltpu.VMEM((D_PLE, D), F32)],
        compiler_params=_params(("arbitrary",)),
    )(xhat2, rstd2, p, target, w_g, w_p, g2, b2, bg, g3, b3)


def _ffn_bwd_act(dy2b, w_down, a_pre, a, b_lin, hff, conv_w8):
    def body(dy_ref, wd_ref, ap_ref, a_ref, bl_ref, h_ref, cw_ref, dap_ref, dbl_ref, gwd_ref, dcw_ref, carry,
             dwd_ref):
        @pl.when(pl.program_id(1) == 0)
        def _():
            carry[...] = jnp.zeros_like(carry)
            dwd_ref[...] = jnp.zeros_like(dwd_ref)
            dcw_ref[...] = jnp.zeros_like(dcw_ref)

        dyb = dy_ref[...]
        dh = _dot_nt(dyb, wd_ref[...])
        av = a_ref[...]
        cdf = 0.5 * (1.0 + lax.erf(av * INV_SQRT2))
        dbl_ref[...] = (dh * (av * cdf)).astype(BF16)
        da = dh * bl_ref[...].astype(F32) * (cdf + av * (jnp.exp(-0.5 * av * av) * INV_SQRT_2PI))
        row = lax.broadcasted_iota(jnp.int32, (TM, TN), 0)
        c0, c1 = carry[0:1, :], carry[1:2, :]
        p1 = jnp.where(row == TM - 1, c0, pltpu.roll(da, TM - 1, 0))
        p2 = jnp.where(row == TM - 2, c0, jnp.where(row == TM - 1, c1, pltpu.roll(da, TM - 2, 0)))
        carry[...] = da[0:8, :]
        ap = ap_ref[...].astype(F32)
        dcw_ref[3:4, :] += _colsum(da)
        dcw_ref[0:1, :] += _colsum(ap * p2)
        dcw_ref[1:2, :] += _colsum(ap * p1)
        dcw_ref[2:3, :] += _colsum(ap * da)
        dap_ref[...] = (cw_ref[2:3, :] * da + cw_ref[1:2, :] * p1 + cw_ref[0:1, :] * p2).astype(BF16)
        dwd_ref[...] += _dot_tn(h_ref[...], dyb)

        @pl.when(pl.program_id(1) == NT - 1)
        def _():
            _store_owner_blocks(gwd_ref, dwd_ref, pl.program_id(0))

    rev_tile = pl.BlockSpec((TM, TN), lambda j, i: (NT - 1 - i, j))
    wrows = pl.BlockSpec((TN, D), lambda j, i: (j, 0))
    small = pl.BlockSpec((8, TN), lambda j, i: (0, j))
    ffb = jax.ShapeDtypeStruct((S, D_FF), BF16)
    blocks = (N_DEV, D_FF // N_DEV, D)
    return pl.pallas_call(
        body, name="ffn_bwd_act", grid=(NJ, NT),
        out_shape=(ffb, ffb, jax.ShapeDtypeStruct(blocks, BF16), jax.ShapeDtypeStruct((8, D_FF), F32)),
        in_specs=[pl.BlockSpec((TM, D), lambda j, i: (NT - 1 - i, 0)), wrows, rev_tile, rev_tile, rev_tile,
                  rev_tile, small],
        out_specs=(rev_tile, rev_tile, _full_spec(blocks), small),
        scratch_shapes=[pltpu.VMEM((8, TN), F32), pltpu.VMEM((TN, D), F32)],
        compiler_params=_params(("arbitrary", "arbitrary"), 56),
    )(dy2b, w_down, a_pre, a, b_lin, hff, conv_w8)


def _ffn_bwd_w(dap, dbl, x1b):
    def body(dap_ref, dbl_ref, x_ref, gwa_ref, gwb_ref, dwa_ref, dwb_ref):
        @pl.when(pl.program_id(1) == 0)
        def _():
            dwa_ref[...] = jnp.zeros_like(dwa_ref)
            dwb_ref[...] = jnp.zeros_like(dwb_ref)

        xb = x_ref[...]
        dwa_ref[...] += _dot_tn(dap_ref[...], xb)
        dwb_ref[...] += _dot_tn(dbl_ref[...], xb)

        @pl.when(pl.program_id(1) == NT - 1)
        def _():
            _store_owner_blocks(gwa_ref, dwa_ref, pl.program_id(0))
            _store_owner_blocks(gwb_ref, dwb_ref, pl.program_id(0))

    tile = pl.BlockSpec((TM, TN), lambda j, i: (i, j))
    blocks = (N_DEV, D_FF // N_DEV, D)
    out = jax.ShapeDtypeStruct(blocks, BF16)
    return pl.pallas_call(
        body, name="ffn_bwd_w", grid=(NJ, NT), out_shape=(out, out),
        in_specs=[tile, tile, pl.BlockSpec((TM, D), lambda j, i: (i, 0))],
        out_specs=(_full_spec(blocks), _full_spec(blocks)),
        scratch_shapes=[pltpu.VMEM((TN, D), F32), pltpu.VMEM((TN, D), F32)],
        compiler_params=_params(("arbitrary", "arbitrary"), 56),
    )(dap, dbl, x1b)


def _ffn_bwd_x(dap, dbl, w_a, w_b):
    def body(dap_ref, dbl_ref, wa_ref, wb_ref, dx_ref):
        dx_ref[...] = _dot(dap_ref[...], wa_ref[...]) + _dot(dbl_ref[...], wb_ref[...])

    return pl.pallas_call(
        body, name="ffn_bwd_x", grid=(NT,), out_shape=jax.ShapeDtypeStruct((S, D), F32),
        in_specs=[_row_spec(D_FF), _row_spec(D_FF), _full_spec((D_FF, D)), _full_spec((D_FF, D))],
        out_specs=_row_spec(D), compiler_params=_params(("parallel",), 56),
    )(dap, dbl, w_a, w_b)


def _ln1_mix_bwd(dy2, dx1_ffn, xhat1, rstd1, g1, attn, gm, w_o, carried=None):
    def body(dy2_ref, dxf_ref, xh_ref, rs_ref, g_ref, a_ref, gm_ref, w_ref,
             dy1_ref, da_ref, dlt_ref, dgm_ref, gwo_ref, vec_ref, dwo_ref):
        @pl.when(pl.program_id(0) == 0)
        def _():
            dwo_ref[...] = jnp.zeros_like(dwo_ref)
            vec_ref[...] = jnp.zeros_like(vec_ref)

        xhat = xh_ref[...]
        dx1 = ALPHA * dy2_ref[...] + dxf_ref[...]
        vec_ref[0:1, :] += _colsum(dx1 * xhat)
        vec_ref[1:2, :] += _colsum(dx1)
        dy1 = _ln_bwd(dx1 * g_ref[...], xhat, rs_ref[...])
        dy1_ref[...] = dy1
        dy1b = dy1.astype(BF16)
        dmix = _dot_nt(dy1b, w_ref[...])
        attn_t = a_ref[...]
        d_attn = dmix[:, 0:D_HALF]
        da_ref[...] = d_attn
        dgm_ref[...] = dmix[:, D_HALF:D]
        lo = (lax.broadcasted_iota(jnp.int32, (TM, LANES), 1) < 64)
        for s in range(4):
            sl = slice(s * LANES, (s + 1) * LANES)
            prod = d_attn[:, sl] * attn_t[:, sl]
            s0 = jnp.sum(jnp.where(lo, prod, 0.0), axis=-1, keepdims=True)
            s1 = jnp.sum(jnp.where(lo, 0.0, prod), axis=-1, keepdims=True)
            dlt_ref[:, sl] = jnp.where(lo, s0, s1)
        dwo_ref[0:D_HALF, :] += _dot_tn(attn_t.astype(BF16), dy1b)
        dwo_ref[D_HALF:D, :] += _dot_tn(gm_ref[...], dy1b)

        @pl.when(pl.program_id(0) == NT - 1)
        def _():
            for j in range(N_DEV):
                gwo_ref[_owner_slot(j)] = dwo_ref[LANES * j:LANES * (j + 1), :].astype(BF16)

    half = jax.ShapeDtypeStruct((S, D_HALF), F32)
    return _host_call(
        body, carried, name="ln1_mix_bwd", grid=(NT,),
        out_shape=(jax.ShapeDtypeStruct((S, D), F32), half, half, half,
                   jax.ShapeDtypeStruct((N_DEV, D // N_DEV, D), BF16), jax.ShapeDtypeStruct((8, D), F32)),
        in_specs=[_row_spec(D), _row_spec(D), _row_spec(D), pl.BlockSpec((TM, 1), _rows), _full_spec((1, D)),
                  _row_spec(D_HALF), _row_spec(D_HALF), _full_spec((D, D))],
        out_specs=(_row_spec(D), _row_spec(D_HALF), _row_spec(D_HALF), _row_spec(D_HALF),
                   _full_spec((N_DEV, D // N_DEV, D)), _full_spec((8, D))),
        scratch_shapes=[pltpu.VMEM((D, D), F32)],
        params=_params(("arbitrary",)), args=(dy2, dx1_ffn, xhat1, rstd1, g1, attn, gm, w_o))


def _gmlp_bwd(d_gm, u_pre, z_pre, ln_z_g, ln_z_b, w_s, bs_exp, carried=None):
    def body(dg_ref, u_ref, z_ref, g_ref, b_ref, ws_ref, bs_ref, du_ref, dz_ref, dws_ref, dbs_ref, vec_ref):
        @pl.when(pl.program_id(0) == 0)
        def _():
            dws_ref[...] = jnp.zeros_like(dws_ref)
            dbs_ref[...] = jnp.zeros_like(dbs_ref)
            vec_ref[...] = jnp.zeros_like(vec_ref)

        u_pre_t, z_pre_t, dgm = u_ref[...], z_ref[...], dg_ref[...]
        z_act, z_slope = _gelu_and_grad(z_pre_t)
        u_act, u_slope = _gelu_and_grad(u_pre_t)
        zhat, rstd = _ln_stats(z_act)
        zn = zhat * g_ref[...] + b_ref[...]
        wm = _masked_ws(ws_ref)
        mixed = _spatial_mix(zn, wm, bs_ref[...])
        du_ref[...] = (dgm * mixed * u_slope).astype(BF16)
        dmixed = dgm * u_act
        lo = _lane_lo()
        tril = _tril()
        group_of_lane = lax.broadcasted_iota(jnp.int32, (8, D_HALF), 1) // 64
        pick = (group_of_lane == lax.broadcasted_iota(jnp.int32, (8, D_HALF), 0)).astype(F32)
        dzn_rows = []
        for ch in range(TM // CHUNK):
            rows = slice(ch * CHUNK, (ch + 1) * CHUNK)
            dbs_ref[...] += lax.dot_general(pick, dmixed[rows, :], (((1,), (1,)), ((), ())),
                                            precision=lax.Precision.HIGHEST, preferred_element_type=F32)
            slabs = []
            for pr in range(4):
                sl = slice(pr * LANES, (pr + 1) * LANES)
                dm = dmixed[rows, sl]
                zp = zn[rows, sl].astype(BF16)
                dm_lo = jnp.where(lo, dm, 0.0).astype(BF16)
                dm_hi = jnp.where(lo, 0.0, dm).astype(BF16)
                dws_ref[2 * pr] += jnp.where(tril, _dot_nt(dm_lo, zp), 0.0)
                dws_ref[2 * pr + 1] += jnp.where(tril, _dot_nt(dm_hi, zp), 0.0)
                dmb = dm.astype(BF16)
                slabs.append(jnp.where(lo, _dot_tn(wm[2 * pr], dmb), _dot_tn(wm[2 * pr + 1], dmb)))
            dzn_rows.append(jnp.concatenate(slabs, axis=1))
        dzn = jnp.concatenate(dzn_rows, axis=0)
        vec_ref[0:1, :] += _colsum(dzn * zhat)
        vec_ref[1:2, :] += _colsum(dzn)
        dz = _ln_bwd(dzn * g_ref[...], zhat, rstd)
        dz_ref[...] = (dz * z_slope).astype(BF16)

    halfb = jax.ShapeDtypeStruct((S, D_HALF), BF16)
    vec = _full_spec((1, D_HALF))
    return _host_call(
        body, carried, name="gmlp_bwd", grid=(NT,),
        out_shape=(halfb, halfb, jax.ShapeDtypeStruct((8, CHUNK, CHUNK), F32),
                   jax.ShapeDtypeStruct((8, CHUNK), F32), jax.ShapeDtypeStruct((8, D_HALF), F32)),
        in_specs=[_row_spec(D_HALF), _row_spec(D_HALF), _row_spec(D_HALF), vec, vec,
                  _full_spec((8, CHUNK, CHUNK)), _full_spec((CHUNK, D_HALF))],
        out_specs=(_row_spec(D_HALF), _row_spec(D_HALF), _full_spec((8, CHUNK, CHUNK)),
                   _full_spec((8, CHUNK)), _full_spec((8, D_HALF))),
        scratch_shapes=[], params=_params(("arbitrary",)), args=(d_gm, u_pre, z_pre, ln_z_g, ln_z_b, w_s, bs_exp))


def _attention_bwd(q, k, v, lse, d_attn, delta, tabs, carried=None):
    def body(q_ref, k_ref, v_ref, l_ref, do_ref, dl_ref, c_ref, sa_ref, sb_ref, dq_ref, dk_ref, dv_ref,
             qb, kb, vb, gb, bias, lsp, dlp, dqp, dk_own, dk_prev, dv_own, dv_prev, dqa, dka, dva):
        lo = _lane_lo()
        zero_pad = jnp.zeros((CHUNK, LANES), BF16)
        for buf in (qb, kb, vb, gb):
            buf[0:CHUNK, :] = zero_pad
        dk_prev[S:S + CHUNK, :] = jnp.zeros((CHUNK, LANES), F32)
        dv_prev[S:S + CHUNK, :] = jnp.zeros((CHUNK, LANES), F32)
        _store_band_bias(bias)
        for d, nb in DILATIONS:
            def build(orig, perm):
                dst = pl.ds(pl.multiple_of(perm + CHUNK, CHUNK), PERM_ROWS)
                src = pl.ds(perm, PERM_ROWS)
                qb[dst, :] = (orig(q_ref) * 0.125).astype(BF16)
                kb[dst, :] = orig(k_ref).astype(BF16)
                vb[dst, :] = orig(v_ref).astype(BF16)
                gb[dst, :] = orig(do_ref).astype(BF16)
                lsp[src, :] = orig(l_ref)
                dlp[src, :] = orig(dl_ref)

            _for_permuted_chunks(d, build)

            def block(b, carry, nb=nb):
                base = pl.multiple_of(b * CHUNK, CHUNK)
                own = pl.multiple_of(base + CHUNK, CHUNK)
                add = bias[jnp.where(b % nb == 0, 1, 0)]
                qblk = qb[pl.ds(own, CHUNK), :]
                gblk = gb[pl.ds(own, CHUNK), :]
                kblk = kb[pl.ds(base, 2 * CHUNK), :]
                vblk = vb[pl.ds(base, 2 * CHUNK), :]
                lse_t = lsp[pl.ds(base, CHUNK), :]
                dlt_t = dlp[pl.ds(base, CHUNK), :]
                q2 = jnp.concatenate([jnp.where(lo, qblk, 0), jnp.where(lo, 0, qblk)], axis=0)
                g2 = jnp.concatenate([jnp.where(lo, gblk, 0), jnp.where(lo, 0, gblk)], axis=0)
                lse2 = jnp.concatenate([lse_t[:, 0:1], lse_t[:, 64:65]], axis=0)
                dlt2 = jnp.concatenate([dlt_t[:, 0:1], dlt_t[:, 64:65]], axis=0)
                add2 = jnp.concatenate([add, add], axis=0)
                p = jnp.exp(_dot_nt(q2, kblk) + add2 - lse2)
                ds = (p * (_dot_nt(g2, vblk) - dlt2)).astype(BF16)
                dv_blk = _dot_tn(p.astype(BF16), g2)
                dk_blk = _dot_tn(ds, q2)
                dq2 = _dot(ds, kblk)
                dqp[pl.ds(base, CHUNK), :] = jnp.where(lo, dq2[0:CHUNK, :], dq2[CHUNK:2 * CHUNK, :]) * 0.125
                dk_prev[pl.ds(base, CHUNK), :] = dk_blk[0:CHUNK, :]
                dk_own[pl.ds(own, CHUNK), :] = dk_blk[CHUNK:2 * CHUNK, :]
                dv_prev[pl.ds(base, CHUNK), :] = dv_blk[0:CHUNK, :]
                dv_own[pl.ds(own, CHUNK), :] = dv_blk[CHUNK:2 * CHUNK, :]
                return carry

            lax.fori_loop(0, S // CHUNK, block, 0, unroll=ATTN_UNROLL)

            def restore(orig, perm, first=(d == 1)):
                src = pl.ds(perm, PERM_ROWS)
                pad = pl.ds(pl.multiple_of(perm + CHUNK, CHUNK), PERM_ROWS)
                dq_new, dk_new, dv_new = dqp[src, :], dk_own[pad, :] + dk_prev[pad, :], dv_own[pad, :] + dv_prev[pad, :]
                if first:
                    orig.store(dqa, dq_new)
                    orig.store(dka, dk_new)
                    orig.store(dva, dv_new)
                else:
                    orig.store(dqa, orig(dqa) + dq_new)
                    orig.store(dka, orig(dka) + dk_new)
                    orig.store(dva, orig(dva) + dv_new)

            _for_permuted_chunks(d, restore)

        def finish(t, carry):
            rows = pl.ds(pl.multiple_of(t * PERM_ROWS, PERM_ROWS), PERM_ROWS)
            c, sa, sb = c_ref[rows, :], sa_ref[rows, :], sb_ref[rows, :]
            dq_ref[rows, :] = _rope_t(dqa[rows, :], c, sa, sb).astype(BF16)
            dk_ref[rows, :] = _rope_t(dka[rows, :], c, sa, sb).astype(BF16)
            dv_ref[rows, :] = dva[rows, :].astype(BF16)
            return carry

        lax.fori_loop(0, S // PERM_ROWS, finish, 0)

    slab = pl.BlockSpec((S, LANES), lambda h: (0, h), pipeline_mode=pl.Buffered(1))
    tab = pl.BlockSpec((S, LANES), lambda h: (0, 0), pipeline_mode=pl.Buffered(1))
    out_slab = pl.BlockSpec((S, LANES), lambda h: (0, h))
    out = jax.ShapeDtypeStruct((S, D_HALF), BF16)
    padded_b = pltpu.VMEM((CHUNK + S, LANES), BF16)
    padded_f = pltpu.VMEM((CHUNK + S, LANES), F32)
    whole = pltpu.VMEM((S, LANES), F32)
    return _host_call(
        body, carried, name="attention_bwd", grid=(4,), out_shape=(out, out, out),
        in_specs=[slab] * 6 + [tab] * 3, out_specs=(out_slab,) * 3,
        scratch_shapes=[padded_b] * 4 + [pltpu.VMEM((2, CHUNK, 2 * CHUNK), F32)] + [whole] * 3
        + [padded_f] * 4 + [whole] * 3,
        params=_params(("arbitrary",), 60), args=(q, k, v, lse, d_attn, delta, *tabs))


def _proj_in_bwd_w(xb, parts, carried=None):
    def body(x_ref, p0, p1, p2, p3, p4, gw_ref, dw_ref):
        @pl.when(pl.program_id(0) == 0)
        def _():
            dw_ref[...] = jnp.zeros_like(dw_ref)

        xt = x_ref[...]
        for n, part in enumerate((p0, p1, p2, p3, p4)):
            dw_ref[n * D_HALF:(n + 1) * D_HALF, :] += _dot_tn(part[...], xt)

        @pl.when(pl.program_id(0) == NT - 1)
        def _():
            width = D_IN // N_DEV
            for j in range(N_DEV):
                gw_ref[_owner_slot(j)] = dw_ref[width * j:width * (j + 1), :].astype(BF16)

    return _host_call(
        body, carried, name="proj_in_bwd_w", grid=(NT,),
        out_shape=(jax.ShapeDtypeStruct((N_DEV, D_IN // N_DEV, D), BF16),),
        in_specs=[_row_spec(D)] + [_row_spec(D_HALF)] * 5, out_specs=(_full_spec((N_DEV, D_IN // N_DEV, D)),),
        scratch_shapes=[pltpu.VMEM((D_IN, D), F32)], params=_params(("arbitrary",)), args=(xb, *parts))


def _proj_in_bwd_x(dy1, parts, w_in, carried=None):
    def body(dy_ref, p0, p1, p2, p3, p4, w_ref, gx_ref):
        acc = ALPHA * dy_ref[...]
        for n, part in enumerate((p0, p1, p2, p3, p4)):
            acc += _dot(part[...], w_ref[n * D_HALF:(n + 1) * D_HALF, :])
        gx_ref[...] = acc

    return _host_call(
        body, carried, name="proj_in_bwd_x", grid=(NT,), out_shape=(jax.ShapeDtypeStruct((S, D), F32),),
        in_specs=[_row_spec(D)] + [_row_spec(D_HALF)] * 5 + [_full_spec((D_IN, D))], out_specs=(_row_spec(D),),
        scratch_shapes=[], params=_params(("arbitrary",)), args=(dy1, *parts, w_in))


def _to_natural(blocks, name):
    n, rows, w = blocks.shape
    tile = min(rows, 256)

    def body(i_ref, o_ref):
        o_ref[...] = jnp.concatenate([i_ref[j] for j in range(n)], axis=1)

    return pl.pallas_call(
        body, name=name, grid=(rows // tile,), out_shape=jax.ShapeDtypeStruct((rows, n * w), blocks.dtype),
        in_specs=[pl.BlockSpec((n, tile, w), lambda i: (0, i, 0))],
        out_specs=pl.BlockSpec((tile, n * w), lambda i: (i, 0)), compiler_params=_params(("parallel",)),
    )(blocks)


def _local_step(x, p, pos_col, target, sm, ex):
    bs_exp = jnp.repeat(sm["b_s"].T, 64, axis=1)
    tabs, got = _rope_tables(pos_col, ex.gather_input())
    w_in = ex.weight_input(got)
    q, k, v, u_pre, z_pre, gm, xb = _proj_in_fwd(x, w_in, tabs, sm["ln_z_g"], sm["ln_z_b"], sm["w_s"], bs_exp)
    (attn, lse), got = _attention_fwd(q, k, v, ex.gather_first())
    wa = ex.weights_first(got)
    xhat1, rstd1, x1b = _mix_ln1_fwd(attn, gm, wa["w_o"], x, sm["ln1_g"], sm["ln1_b"])
    (a_pre, a, b_lin, hff), got = _ffn_up_fwd(x1b, wa["w_ff_a"], wa["w_ff_b"], wa["conv_w8"], sm["conv_b"],
                                              ex.gather_second())
    wc = ex.weights_second(got)
    xhat2, rstd2 = _ffn_down_ln2_fwd(hff, wc["w_ff_down"], xhat1, sm["ln1_g"], sm["ln1_b"])
    loss, dy2, dy2b, dw_g, dw_p, vec_tail = _tail_fwd_bwd(
        xhat2, rstd2, p, target, wc["w_ple_gate"], wc["w_ple_in"], sm["ln2_g"], sm["ln2_b"],
        sm["b_ple_gate"], sm["ln3_g"], sm["ln3_b"])
    dap, dbl, dw_down, dconv = _ffn_bwd_act(dy2b, wc["w_ff_down"], a_pre, a, b_lin, hff, wa["conv_w8"])
    dw_a, dw_b = _ffn_bwd_w(dap, dbl, x1b)
    dx1_ffn = _ffn_bwd_x(dap, dbl, wa["w_ff_a"], wa["w_ff_b"])
    (dy1, d_attn, delta, d_gm, dw_o, vec_ln1), _ = _ln1_mix_bwd(
        dy2, dx1_ffn, xhat1, rstd1, sm["ln1_g"], attn, gm, wa["w_o"])
    early = {"w_ff_a": dw_a, "w_ff_b": dw_b, "w_ff_down": dw_down, "w_ple_gate": dw_g, "w_ple_in": dw_p,
             "w_o": dw_o}
    (du, dz, dws, dbs, vec_z), got = _gmlp_bwd(d_gm, u_pre, z_pre, sm["ln_z_g"], sm["ln_z_b"], sm["w_s"], bs_exp,
                                               ex.to_sibling(early))
    chip_sums = ex.reduce_on_chip(got)
    small = {"tail": vec_tail, "ln1": vec_ln1, "ln_z": vec_z, "conv": dconv, "w_s": dws, "b_s": dbs, "loss": loss}
    (dq, dk, dv), got_early = _attention_bwd(q, k, v, lse, d_attn, delta, tabs,
                                             ex.between_chips(chip_sums, small))
    parts = (dq, dk, dv, du, dz)
    (dw_in,), _ = _proj_in_bwd_w(xb, parts)
    (grad_x,), got_late = _proj_in_bwd_x(dy1, parts, w_in, ex.last(dw_in))
    return grad_x, ex.collect(got_early, got_late)


def _mesh_pos():
    return lax.axis_index("x"), lax.axis_index("y"), lax.axis_index("c")


def _cast_shards(shards):
    n = len(shards)

    def body(*refs):
        for a in range(n):
            refs[n + a][...] = refs[a][...].astype(BF16)

    whole = [_full_spec(s.shape) for s in shards]
    return pl.pallas_call(
        body, name="cast_shards", grid=(1,), out_shape=tuple(jax.ShapeDtypeStruct(s.shape, BF16) for s in shards),
        in_specs=whole, out_specs=tuple(whole), compiler_params=_params(("arbitrary",)),
    )(*shards)


class _GatherComm:
    def __init__(self, shards):
        n = len(shards)
        self.inputs = list(shards)
        self.out_shapes = [jax.ShapeDtypeStruct((N_DEV,) + s.shape, s.dtype) for s in shards]
        self.scratch = [pltpu.SemaphoreType.DMA((7 * n,)), pltpu.SemaphoreType.DMA((7 * n,)),
                        pltpu.SemaphoreType.DMA((n,))]

    def phases(self, x_refs, out_refs, sems):
        send_sems, recv_sems, local_sems = sems
        n_arr = len(x_refs)

        def where():
            x, y, c = _mesh_pos()
            return (x, y, c), (x, y, 1 - c), [(1 - x, y), (x, 1 - y), (1 - x, 1 - y)]

        def copy(a, n, block, to, from_shard=False):
            dst = out_refs[a].at[4 * block[0] + 2 * block[1] + block[2]]
            return pltpu.make_async_remote_copy(
                src_ref=x_refs[a] if from_shard else dst, dst_ref=dst, send_sem=send_sems.at[7 * a + n],
                recv_sem=recv_sems.at[7 * a + n], device_id=to, device_id_type=MESH)

        def local(a):
            x, y, c = _mesh_pos()
            return pltpu.make_async_copy(x_refs[a], out_refs[a].at[4 * x + 2 * y + c], local_sems.at[a])

        def start():
            me, sibling, chips = where()
            for a in range(n_arr):
                local(a).start()
                copy(a, 0, me, sibling, from_shard=True).start()
                for n, chip in enumerate(chips):
                    copy(a, 1 + n, me, (*chip, me[2]), from_shard=True).start()

        def forward():
            me, sibling, chips = where()
            for n, chip in enumerate(chips):
                for a in range(n_arr):
                    copy(a, 1 + n, (*chip, me[2]), me).wait_recv()
                    copy(a, 4 + n, (*chip, me[2]), sibling).start()

        def finish():
            me, sibling, chips = where()
            for a in range(n_arr):
                copy(a, 0, sibling, me).wait_recv()
                copy(a, 0, me, sibling, from_shard=True).wait_send()
                for n, chip in enumerate(chips):
                    copy(a, 4 + n, (*chip, 1 - me[2]), me).wait_recv()
                    copy(a, 1 + n, me, (*chip, me[2]), from_shard=True).wait_send()
                    copy(a, 4 + n, (*chip, me[2]), sibling).wait_send()
                local(a).wait()

        return {"start": start, "forward": forward, "finish": finish}


class _SiblingComm:
    def __init__(self, big):
        n = len(big)
        self.inputs = list(big)
        self.out_shapes = [jax.ShapeDtypeStruct(b.shape[1:], b.dtype) for b in big]
        self.scratch = [pltpu.SemaphoreType.DMA((n,)), pltpu.SemaphoreType.DMA((n,))]

    def phases(self, src, dst, sems):
        send_sems, recv_sems = sems

        def copies():
            x, y, c = _mesh_pos()
            return [pltpu.make_async_remote_copy(
                src_ref=src[a].at[1 - c], dst_ref=dst[a], send_sem=send_sems.at[a], recv_sem=recv_sems.at[a],
                device_id=(x, y, 1 - c), device_id_type=MESH) for a in range(len(src))]

        def start():
            for cp in copies():
                cp.start()

        def finish():
            for cp in copies():
                cp.wait()

        return {"start": start, "finish": finish}


class _ChipComm:
    def __init__(self, sums):
        n = len(sums)
        self.inputs = list(sums)
        self.out_shapes = [jax.ShapeDtypeStruct(s.shape, s.dtype) for s in sums]
        self.scratch = [pltpu.SemaphoreType.DMA((3 * n,)), pltpu.SemaphoreType.DMA((3 * n,)),
                        pltpu.SemaphoreType.DMA((n,))]

    def phases(self, src, dst, sems):
        send_sems, recv_sems, local_sems = sems

        def copies():
            x, y, c = _mesh_pos()
            my_chip = 2 * x + y
            out = [pltpu.make_async_copy(src[a].at[my_chip], dst[a].at[my_chip], local_sems.at[a])
                   for a in range(len(src))]
            for n, (px, py) in enumerate([(1 - x, y), (x, 1 - y), (1 - x, 1 - y)]):
                for a in range(len(src)):
                    out.append(pltpu.make_async_remote_copy(
                        src_ref=src[a].at[2 * px + py], dst_ref=dst[a].at[my_chip],
                        send_sem=send_sems.at[3 * a + n], recv_sem=recv_sems.at[3 * a + n],
                        device_id=(px, py, c), device_id_type=MESH))
            return out

        def start():
            for cp in copies():
                cp.start()

        def finish():
            for cp in copies():
                cp.wait()

        return {"start": start, "finish": finish}


class _ScatterComm:
    def __init__(self, blocks, small):
        self.n_big, self.n_small = len(blocks), len(small)
        n = self.n_big + self.n_small
        self.inputs = list(blocks) + list(small)
        self.out_shapes = ([jax.ShapeDtypeStruct(b.shape, b.dtype) for b in blocks]
                           + [jax.ShapeDtypeStruct((N_DEV,) + s.shape, s.dtype) for s in small])
        self.scratch = [pltpu.SemaphoreType.DMA((7 * n,)), pltpu.SemaphoreType.DMA((7 * n,)),
                        pltpu.SemaphoreType.DMA((n,))]

    def phases(self, src, dst, sems):
        send_sems, recv_sems, local_sems = sems
        n_big, n_all = self.n_big, self.n_big + self.n_small

        def source(a, core, chip):
            return src[a].at[core * 4 + chip] if a < n_big else src[a]

        def copies():
            x, y, c = _mesh_pos()
            me = 4 * x + 2 * y + c
            out = [pltpu.make_async_copy(source(a, c, 2 * x + y), dst[a].at[me], local_sems.at[a])
                   for a in range(n_all)]
            for flip in range(1, N_DEV):
                px = 1 - x if flip & 4 else x
                py = 1 - y if flip & 2 else y
                pc = 1 - c if flip & 1 else c
                for a in range(n_all):
                    n = 7 * a + flip - 1
                    out.append(pltpu.make_async_remote_copy(
                        src_ref=source(a, pc, 2 * px + py), dst_ref=dst[a].at[me], send_sem=send_sems.at[n],
                        recv_sem=recv_sems.at[n], device_id=(px, py, pc), device_id_type=MESH))
            return out

        def start():
            for cp in copies():
                cp.start()

        def finish():
            for cp in copies():
                cp.wait()

        return {"start": start, "finish": finish}


class _Both:
    def __init__(self, first, second):
        self.parts = (first, second)
        self.inputs = first.inputs + second.inputs
        self.out_shapes = first.out_shapes + second.out_shapes
        self.scratch = first.scratch + second.scratch

    def phases(self, src, dst, sems):
        a, b = self.parts
        pa = a.phases(src[:len(a.inputs)], dst[:len(a.out_shapes)], sems[:len(a.scratch)])
        pb = b.phases(src[len(a.inputs):], dst[len(a.out_shapes):], sems[len(a.scratch):])

        def both(name):
            def run():
                pa[name]()
                pb[name]()
            return run

        return {name: both(name) for name in pa}


def _host_call(body, carried, *, name, grid, out_shape, in_specs, out_specs, scratch_shapes, params, args):
    if carried is None:
        return pl.pallas_call(body, name=name, grid=grid, out_shape=tuple(out_shape), in_specs=list(in_specs),
                              out_specs=tuple(out_specs), scratch_shapes=list(scratch_shapes),
                              compiler_params=params)(*args), ()
    comm, when = carried
    n_in, n_out, n_scratch = len(in_specs), len(out_shape), len(scratch_shapes)
    k_in, k_out = len(comm.inputs), len(comm.out_shapes)

    def wrapped(*refs):
        bounds = np.cumsum([0, n_in, k_in, n_out, k_out, n_scratch])
        ins, c_in, outs, c_out, scr = (refs[bounds[i]:bounds[i + 1]] for i in range(5))
        phases = comm.phases(c_in, c_out, refs[bounds[5]:])
        for phase, cond in when("before"):
            pl.when(cond)(phases[phase])
        body(*ins, *outs, *scr)
        for phase, cond in when("after"):
            pl.when(cond)(phases[phase])

    anywhere = pl.BlockSpec(memory_space=pl.ANY)
    results = pl.pallas_call(
        wrapped, name=name, grid=grid, out_shape=tuple(out_shape) + tuple(comm.out_shapes),
        in_specs=list(in_specs) + [anywhere] * k_in, out_specs=tuple(out_specs) + (anywhere,) * k_out,
        scratch_shapes=list(scratch_shapes) + comm.scratch, compiler_params=params,
    )(*args, *comm.inputs)
    return results[:n_out], results[n_out:]


class _Exchanges:
    FIRST = ("w_o", "w_ff_a", "w_ff_b")
    SECOND = ("w_ff_down", "w_ple_gate", "w_ple_in")
    EARLY = ("w_ff_a", "w_ff_b", "w_ff_down", "w_ple_gate", "w_ple_in", "w_o")
    LATE = ("w_in",)

    def __init__(self, shards, conv_rows):
        self.shards, self.conv_rows = shards, conv_rows
        self.mode = {name: mode for name, _, mode in BIG}

    def _natural(self, name, blocks):
        n, r, c = blocks.shape
        return _to_natural(blocks, name + "_natural") if self.mode[name] == "cols" else blocks.reshape(n * r, c)

    def gather_input(self):
        def when(position):
            step = pl.program_id(0)
            if position == "before":
                return [("start", step == 0)]
            return [("forward", step == NT - 1), ("finish", step == NT - 1)]
        return _GatherComm([self.shards["w_in"]]), when

    def weight_input(self, got):
        return self._natural("w_in", got[0])

    def gather_first(self):
        comm = _GatherComm([self.shards[n] for n in self.FIRST] + [self.conv_rows])

        def when(position):
            step = pl.program_id(0)
            if position == "before":
                return [("start", step == 0), ("forward", step == 3)]
            return [("finish", step == 3)]
        return comm, when

    def weights_first(self, got):
        out = {name: self._natural(name, blocks) for name, blocks in zip(self.FIRST, got)}
        out["conv_w8"] = _to_natural(got[-1], "conv_w_natural")
        return out

    def gather_second(self):
        comm = _GatherComm([self.shards[n] for n in self.SECOND])

        def when(position):
            j, i = pl.program_id(0), pl.program_id(1)
            if position == "before":
                return [("start", (j == 0) & (i == 0)), ("forward", (j == NJ - 1) & (i == NT // 2))]
            return [("finish", (j == NJ - 1) & (i == NT - 1))]
        return comm, when

    def weights_second(self, got):
        return {name: self._natural(name, blocks) for name, blocks in zip(self.SECOND, got)}

    def to_sibling(self, early):
        self.by_core = [early[n].reshape((2, 4) + early[n].shape[1:]) for n in self.EARLY]
        return _SiblingComm(self.by_core), _first_and_last(NT)

    def reduce_on_chip(self, from_sibling):
        core = lax.axis_index("c").astype(jnp.int32).reshape(1)
        return _chip_reduce(self.by_core, from_sibling, core, "chip_reduce")

    def between_chips(self, chip_sums, small):
        self.small_keys = tuple(small)
        return _Both(_ChipComm(chip_sums), _ScatterComm([], [small[k] for k in self.small_keys])), _first_and_last(4)

    def last(self, dw_in):
        by_core = [dw_in.reshape((2, 4) + dw_in.shape[1:])]
        core = lax.axis_index("c").astype(jnp.int32).reshape(1)
        sums = _chip_reduce(by_core, _standalone(_SiblingComm(by_core), "w_in_grad_to_sibling"), core, "w_in_chip_reduce")
        return _ChipComm(sums), _first_and_last(NT)

    def collect(self, got_early, got_late):
        parts = dict(zip(self.EARLY, got_early[:len(self.EARLY)]))
        parts.update(zip(self.LATE, got_late))
        return parts, dict(zip(self.small_keys, got_early[len(self.EARLY):]))


def _first_and_last(n_steps):
    def when(position):
        step = pl.program_id(0)
        return [("start", step == 0)] if position == "before" else [("finish", step == n_steps - 1)]
    return when


def _standalone(comm, name):
    n_in = len(comm.inputs)

    def body(*refs):
        phases = comm.phases(refs[:n_in], refs[n_in:n_in + len(comm.out_shapes)], refs[n_in + len(comm.out_shapes):])
        phases["start"]()
        phases["finish"]()

    anywhere = pl.BlockSpec(memory_space=pl.ANY)
    return pl.pallas_call(
        body, name=name, out_shape=tuple(comm.out_shapes), in_specs=[anywhere] * n_in,
        out_specs=(anywhere,) * len(comm.out_shapes), scratch_shapes=comm.scratch,
    )(*comm.inputs)


def _chip_reduce(big, from_sibling, core, name):
    n = len(big)

    def body(core_ref, *refs):
        for a in range(n):
            mine, theirs, out = refs[a], refs[n + a], refs[2 * n + a]
            out[0] = (mine[0, 0].astype(F32) + theirs[0].astype(F32)).astype(BF16)

    def block(shape):
        return pl.BlockSpec((1,) + shape, lambda ch, core_ref: (ch, 0, 0))

    grid_spec = pltpu.PrefetchScalarGridSpec(
        num_scalar_prefetch=1, grid=(4,),
        in_specs=[pl.BlockSpec((1, 1) + b.shape[2:], lambda ch, core_ref: (core_ref[0], ch, 0, 0)) for b in big]
        + [block(b.shape[2:]) for b in big],
        out_specs=[block(b.shape[2:]) for b in big])
    return pl.pallas_call(
        body, name=name, grid_spec=grid_spec,
        out_shape=tuple(jax.ShapeDtypeStruct(b.shape[1:], BF16) for b in big),
        compiler_params=_params(("parallel",)),
    )(core, *big, *from_sibling)


def _adamw(g, w, m, v):
    nm = ADAM_B1 * m + (1.0 - ADAM_B1) * g
    nv = ADAM_B2 * v + (1.0 - ADAM_B2) * (g * g)
    m_hat = nm / (1.0 - ADAM_B1 ** ADAM_STEP)
    v_hat = nv / (1.0 - ADAM_B2 ** ADAM_STEP)
    return -ADAM_LR * (m_hat / (jnp.sqrt(v_hat) + ADAM_EPS) + ADAM_WD * w), nm, nv


def _adamw_sharded(parts, w, m, v, name):
    def body(p_ref, w_ref, m_ref, v_ref, g_ref, d_ref, nm_ref, nv_ref):
        g = p_ref[0].astype(F32)
        for s in range(1, parts.shape[0]):
            g = g + p_ref[s].astype(F32)
        delta, nm, nv = _adamw(g, w_ref[0], m_ref[0], v_ref[0])
        g_ref[0] = g
        d_ref[0] = delta
        nm_ref[0] = nm
        nv_ref[0] = nv

    n, r, c = parts.shape
    steps = 4 if r % 64 == 0 and r >= 512 else (2 if r % 32 == 0 and r >= 256 else 1)
    tile = pl.BlockSpec((1, r // steps, c), lambda i: (0, i, 0))
    return pl.pallas_call(
        body, name=name, grid=(steps,), out_shape=(jax.ShapeDtypeStruct(w.shape, F32),) * 4,
        in_specs=[pl.BlockSpec((n, r // steps, c), lambda i: (0, i, 0)), tile, tile, tile], out_specs=(tile,) * 4,
        compiler_params=_params(("parallel",)),
    )(parts, w, m, v)


REPLICATED = (("ln_z_g", "ln_z", 0), ("ln_z_b", "ln_z", 1), ("w_s", "w_s", None), ("b_s", "b_s", None),
              ("ln1_g", "ln1", 0), ("ln1_b", "ln1", 1), ("conv_w", "conv_mine", None), ("conv_b", "conv", 3),
              ("ln2_g", "tail", 3), ("ln2_b", "tail", 4), ("b_ple_gate", "tail", 0), ("ln3_g", "tail", 1),
              ("ln3_b", "tail", 2))
GATHERED = ("tail", "ln1", "ln_z", "conv", "w_s", "b_s", "loss", "conv_mine")


def _adamw_replicated(gathered, w, m, v):
    n_par = len(REPLICATED)

    def body(*refs):
        srcs = dict(zip(GATHERED, refs[:len(GATHERED)]))
        rest = refs[len(GATHERED):]
        w_refs, m_refs, v_refs = rest[:n_par], rest[n_par:2 * n_par], rest[2 * n_par:3 * n_par]
        outs = rest[3 * n_par:]
        loss_ref = outs[4 * n_par]
        sums = {}
        for key, ref in srcs.items():
            total = ref[0]
            for dev in range(1, N_DEV):
                total = total + ref[dev]
            sums[key] = total
        loss_ref[...] = sums["loss"]
        for n, (name, key, row) in enumerate(REPLICATED):
            if name == "conv_w":
                g = sums[key][0:3, :]
            elif row is None:
                g = sums[key]
            else:
                g = sums[key][row:row + 1, :]
            lead = len(w_refs[n].shape) - g.ndim
            idx = (0,) * lead + (Ellipsis,)
            delta, nm, nv = _adamw(g, w_refs[n][idx], m_refs[n][idx], v_refs[n][idx])
            for kind, val in enumerate((g, delta, nm, nv)):
                outs[kind * n_par + n][idx] = val

    names = [name for name, _, _ in REPLICATED]
    shapes = [jax.ShapeDtypeStruct(w[name].shape, F32) for name in names]
    args = [gathered[k] for k in GATHERED] + [w[n] for n in names] + [m[n] for n in names] + [v[n] for n in names]
    out_shape = tuple(shapes * 4) + (jax.ShapeDtypeStruct((8, LANES), F32),)
    return pl.pallas_call(
        body, name="adamw_replicated", grid=(1,), out_shape=out_shape,
        in_specs=[_full_spec(a.shape) for a in args], out_specs=tuple(_full_spec(s.shape) for s in out_shape),
        compiler_params=_params(("arbitrary",)),
    )(*args)


def kernel(x, p, positions, w_in, ln_z_g, ln_z_b, w_s, b_s, w_o, ln1_g, ln1_b, w_ff_a, w_ff_b, conv_w, conv_b, w_ff_down, ln2_g, ln2_b, w_ple_gate, b_ple_gate, w_ple_in, ln3_g, ln3_b, loss_target, m_w_in, m_ln_z_g, m_ln_z_b, m_w_s, m_b_s, m_w_o, m_ln1_g, m_ln1_b, m_w_ff_a, m_w_ff_b, m_conv_w, m_conv_b, m_w_ff_down, m_ln2_g, m_ln2_b, m_w_ple_gate, m_b_ple_gate, m_w_ple_in, m_ln3_g, m_ln3_b, v_w_in, v_ln_z_g, v_ln_z_b, v_w_s, v_b_s, v_w_o, v_ln1_g, v_ln1_b, v_w_ff_a, v_w_ff_b, v_conv_w, v_conv_b, v_w_ff_down, v_ln2_g, v_ln2_b, v_w_ple_gate, v_b_ple_gate, v_w_ple_in, v_ln3_g, v_ln3_b):
    w = dict(w_in=w_in, ln_z_g=ln_z_g, ln_z_b=ln_z_b, w_s=w_s, b_s=b_s, w_o=w_o, ln1_g=ln1_g, ln1_b=ln1_b,
             w_ff_a=w_ff_a, w_ff_b=w_ff_b, conv_w=conv_w, conv_b=conv_b, w_ff_down=w_ff_down, ln2_g=ln2_g,
             ln2_b=ln2_b, w_ple_gate=w_ple_gate, b_ple_gate=b_ple_gate, w_ple_in=w_ple_in, ln3_g=ln3_g,
             ln3_b=ln3_b)
    m = dict(w_in=m_w_in, ln_z_g=m_ln_z_g, ln_z_b=m_ln_z_b, w_s=m_w_s, b_s=m_b_s, w_o=m_w_o, ln1_g=m_ln1_g,
             ln1_b=m_ln1_b, w_ff_a=m_w_ff_a, w_ff_b=m_w_ff_b, conv_w=m_conv_w, conv_b=m_conv_b,
             w_ff_down=m_w_ff_down, ln2_g=m_ln2_g, ln2_b=m_ln2_b, w_ple_gate=m_w_ple_gate,
             b_ple_gate=m_b_ple_gate, w_ple_in=m_w_ple_in, ln3_g=m_ln3_g, ln3_b=m_ln3_b)
    v = dict(w_in=v_w_in, ln_z_g=v_ln_z_g, ln_z_b=v_ln_z_b, w_s=v_w_s, b_s=v_b_s, w_o=v_w_o, ln1_g=v_ln1_g,
             ln1_b=v_ln1_b, w_ff_a=v_w_ff_a, w_ff_b=v_w_ff_b, conv_w=v_conv_w, conv_b=v_conv_b,
             w_ff_down=v_w_ff_down, ln2_g=v_ln2_g, ln2_b=v_ln2_b, w_ple_gate=v_w_ple_gate,
             b_ple_gate=v_b_ple_gate, w_ple_in=v_w_ple_in, ln3_g=v_ln3_g, ln3_b=v_ln3_b)
    big_names = [name for name, _, _ in BIG]
    small_names = ("ln_z_g", "ln_z_b", "w_s", "b_s", "ln1_g", "ln1_b", "conv_b", "ln2_g", "ln2_b", "b_ple_gate",
                   "ln3_g", "ln3_b")

    transposed = {name for name, _, mode in BIG if mode == "rows_t"}

    def travel(a, name):
        return jnp.swapaxes(a, 1, 2) if name in transposed else a

    shards = dict(zip(big_names, _cast_shards([travel(w[n], n)[0] for n in big_names])))
    conv_rows = jnp.pad(w["conv_w"][0], ((0, 5), (0, 0)))
    sm = {n: w[n][0] if w[n].ndim > 2 else w[n] for n in small_names}
    pos_col = positions.reshape(S, 1).astype(F32)
    grad_x, (parts, small_all) = _local_step(x[0], p[0, 0], pos_col, loss_target[0], sm,
                                             _Exchanges(shards, conv_rows))
    me = 4 * lax.axis_index("x") + 2 * lax.axis_index("y") + lax.axis_index("c")
    conv_cols = small_all["conv"].reshape(N_DEV, 8, N_DEV, D_FF // N_DEV)
    small_all["conv_mine"] = lax.dynamic_index_in_dim(conv_cols, me, axis=2, keepdims=False)

    leaves = {}
    for name in big_names:
        outs = _adamw_sharded(parts[name], travel(w[name], name), travel(m[name], name), travel(v[name], name),
                              "adamw_" + name)
        leaves[name] = tuple(travel(o, name) for o in outs)
    rep = _adamw_replicated(small_all, w, m, v)
    n_rep = len(REPLICATED)
    for n, (name, _, _) in enumerate(REPLICATED):
        leaves[name] = tuple(rep[kind * n_rep + n] for kind in range(4))
    loss = rep[4 * n_rep][0, 0]
    return (loss, grad_x[None], *[leaves[n][kind] for kind in range(4) for n in WEIGHT_ORDER])
```

```python
import math

import numpy as np
import jax
import jax.numpy as jnp
from jax import lax
from jax.experimental import pallas as pl
from jax.experimental.pallas import tpu as pltpu

F32 = jnp.float32
BF16 = jnp.bfloat16
MESH = pl.DeviceIdType.MESH

N_DEV = 8
S = 4096
D = 1024
D_HALF = 512
D_IN = 2560
D_FF = 2816
D_PLE = 256
CHUNK = 128
DILATIONS = ((1, 32), (4, 8), (16, 2))
ROPE_THETA = 500000.0
LN_EPS = 1e-5
ALPHA = 2.0 ** 0.25
NEG_INF = -1e30
INV_SQRT2 = 1.0 / math.sqrt(2.0)
INV_SQRT_2PI = 1.0 / math.sqrt(2.0 * math.pi)

ADAM_LR, ADAM_B1, ADAM_B2, ADAM_EPS, ADAM_WD, ADAM_STEP = 0.001, 0.9, 0.999, 1e-08, 0.01, 10

TM = 512
NT = S // TM
ATTN_UNROLL = 8
TN = 1408
NJ = D_FF // TN
LANES = 128
VMEM_MIB = 1024 * 1024

BIG = (("w_in", (320, 1024), "rows_t"), ("w_o", (128, 1024), "rows"), ("w_ff_a", (352, 1024), "rows_t"),
       ("w_ff_b", (352, 1024), "rows_t"), ("w_ff_down", (352, 1024), "rows"), ("w_ple_gate", (128, 1024), "rows"),
       ("w_ple_in", (256, 128), "cols"))
WEIGHT_ORDER = ("w_in", "ln_z_g", "ln_z_b", "w_s", "b_s", "w_o", "ln1_g", "ln1_b", "w_ff_a", "w_ff_b",
                "conv_w", "conv_b", "w_ff_down", "ln2_g", "ln2_b", "w_ple_gate", "b_ple_gate",
                "w_ple_in", "ln3_g", "ln3_b")


def _params(semantics=None, vmem_mib=48):
    return pltpu.CompilerParams(dimension_semantics=semantics, vmem_limit_bytes=vmem_mib * VMEM_MIB)


def _dot(a, b):
    return jnp.dot(a, b, preferred_element_type=F32)


def _dot_nt(a, b):
    return lax.dot_general(a, b, (((1,), (1,)), ((), ())), preferred_element_type=F32)


def _dot_tn(a, b):
    return lax.dot_general(a, b, (((0,), (0,)), ((), ())), preferred_element_type=F32)


def _gelu(x):
    return 0.5 * x * (1.0 + lax.erf(x * INV_SQRT2))


def _gelu_and_grad(x):
    cdf = 0.5 * (1.0 + lax.erf(x * INV_SQRT2))
    return x * cdf, cdf + x * (jnp.exp(-0.5 * x * x) * INV_SQRT_2PI)


def _ln_stats(y):
    mu = jnp.mean(y, axis=-1, keepdims=True)
    yc = y - mu
    var = jnp.mean(yc * yc, axis=-1, keepdims=True)
    rstd = lax.rsqrt(var + LN_EPS)
    return yc * rstd, rstd


def _ln_bwd(dxhat, xhat, rstd):
    m1 = jnp.mean(dxhat, axis=-1, keepdims=True)
    m2 = jnp.mean(dxhat * xhat, axis=-1, keepdims=True)
    return rstd * (dxhat - m1 - xhat * m2)


def _colsum(x):
    return jnp.sum(x, axis=0, keepdims=True)


def _rows(i):
    return (i, 0)


def _row_spec(width):
    return pl.BlockSpec((TM, width), _rows)


def _full_spec(shape):
    return pl.BlockSpec(shape, lambda *_: (0,) * len(shape))


def _owner_slot(j):
    return (j % 2) * 4 + j // 2


def _store_owner_blocks(blocks_ref, acc_ref, tile):
    rows = TN // 4
    for t in range(4):
        blocks_ref[(t % 2) * 4 + 2 * tile + t // 2] = acc_ref[rows * t:rows * (t + 1), :].astype(BF16)


def _lane_lo():
    return lax.broadcasted_iota(jnp.int32, (CHUNK, LANES), 1) < 64


def _tril():
    r = lax.broadcasted_iota(jnp.int32, (CHUNK, CHUNK), 0)
    c = lax.broadcasted_iota(jnp.int32, (CHUNK, CHUNK), 1)
    return c <= r


def _rope_consts():
    lane = np.arange(LANES) % 64
    j = lane % 8
    inv = np.where(lane < 16, np.float32(ROPE_THETA) ** (-(2.0 * j).astype(np.float32) / np.float32(16.0)), 0.0)
    m_lo = (lane < 8).astype(np.float32)
    m_hi = ((lane >= 8) & (lane < 16)).astype(np.float32)
    return (jnp.asarray(inv, F32).reshape(1, LANES), jnp.asarray(m_lo).reshape(1, LANES),
            jnp.asarray(m_hi).reshape(1, LANES))


def _rope_tables(pos_col, carried=None):
    inv, m_lo, m_hi = _rope_consts()

    def body(pos_ref, inv_ref, lo_ref, hi_ref, c_ref, sa_ref, sb_ref):
        ang = pos_ref[...] * inv_ref[...]
        c = jnp.cos(ang)
        s = jnp.sin(ang)
        lo = lo_ref[...]
        hi = hi_ref[...]
        c_ref[...] = jnp.where(lo + hi > 0.0, c, 1.0)
        sa_ref[...] = s * hi
        sb_ref[...] = -s * lo

    vec = _full_spec((1, LANES))
    out = jax.ShapeDtypeStruct((S, LANES), F32)
    return _host_call(
        body, carried, name="rope_tables", grid=(NT,), out_shape=(out, out, out),
        in_specs=[pl.BlockSpec((TM, 1), _rows), vec, vec, vec],
        out_specs=(_row_spec(LANES),) * 3, scratch_shapes=[], params=_params(("arbitrary",)),
        args=(pos_col, inv, m_lo, m_hi))


def _rope(t, c, sa, sb):
    return t * c + pltpu.roll(t, 8, 1) * sa + pltpu.roll(t, LANES - 8, 1) * sb


def _rope_t(dy, c, sa, sb):
    return dy * c + pltpu.roll(dy * sa, LANES - 8, 1) + pltpu.roll(dy * sb, 8, 1)


def _masked_ws(ws_ref):
    tril = _tril()
    return [jnp.where(tril, ws_ref[g], 0.0).astype(BF16) for g in range(8)]


def _spatial_mix(zn, wm, bs):
    lo = _lane_lo()
    rows = []
    for ch in range(TM // CHUNK):
        slabs = []
        for pr in range(4):
            zp = zn[ch * CHUNK:(ch + 1) * CHUNK, pr * LANES:(pr + 1) * LANES].astype(BF16)
            slabs.append(jnp.where(lo, _dot(wm[2 * pr], zp), _dot(wm[2 * pr + 1], zp)))
        rows.append(jnp.concatenate(slabs, axis=1) + bs)
    return jnp.concatenate(rows, axis=0)


def _proj_in_fwd(x, w_in, tabs, ln_z_g, ln_z_b, w_s, bs_exp):
    def body(x_ref, w_ref, c_ref, sa_ref, sb_ref, g_ref, b_ref, ws_ref, bs_ref,
             q_ref, k_ref, v_ref, u_ref, z_ref, gm_ref, xb_ref):
        xb = x_ref[...].astype(BF16)
        xb_ref[...] = xb
        c, sa, sb = c_ref[...], sa_ref[...], sb_ref[...]
        hq = _dot_nt(xb, w_ref[0:512, :])
        hk = _dot_nt(xb, w_ref[512:1024, :])
        for s in range(4):
            sl = slice(s * LANES, (s + 1) * LANES)
            q_ref[:, sl] = _rope(hq[:, sl], c, sa, sb)
            k_ref[:, sl] = _rope(hk[:, sl], c, sa, sb)
        v_ref[...] = _dot_nt(xb, w_ref[1024:1536, :])
        u_pre = _dot_nt(xb, w_ref[1536:2048, :])
        z_pre = _dot_nt(xb, w_ref[2048:2560, :])
        u_ref[...] = u_pre
        z_ref[...] = z_pre
        zhat, _ = _ln_stats(_gelu(z_pre))
        zn = zhat * g_ref[...] + b_ref[...]
        mixed = _spatial_mix(zn, _masked_ws(ws_ref), bs_ref[...])
        gm_ref[...] = (_gelu(u_pre) * mixed).astype(BF16)

    half = jax.ShapeDtypeStruct((S, D_HALF), F32)
    tab = _row_spec(LANES)
    return pl.pallas_call(
        body, name="proj_in_fwd", grid=(NT,),
        out_shape=(half, half, half, half, half, jax.ShapeDtypeStruct((S, D_HALF), BF16),
                   jax.ShapeDtypeStruct((S, D), BF16)),
        in_specs=[_row_spec(D), _full_spec((D_IN, D)), tab, tab, tab, _full_spec((1, D_HALF)),
                  _full_spec((1, D_HALF)), _full_spec((8, CHUNK, CHUNK)), _full_spec((CHUNK, D_HALF))],
        out_specs=(_row_spec(D_HALF),) * 6 + (_row_spec(D),),
        compiler_params=_params(("parallel",)),
    )(x, w_in, *tabs, ln_z_g, ln_z_b, w_s, bs_exp)


def _store_band_bias(bias_ref):
    qi = lax.broadcasted_iota(jnp.int32, (CHUNK, 2 * CHUNK), 0)
    kj = lax.broadcasted_iota(jnp.int32, (CHUNK, 2 * CHUNK), 1)
    band = (kj >= qi) & (kj <= qi + CHUNK)
    bias_ref[0] = jnp.where(band, 0.0, NEG_INF)
    bias_ref[1] = jnp.where(band & (kj >= CHUNK), 0.0, NEG_INF)


PERM_ROWS = 256


def _for_permuted_chunks(d, fn):
    length = S // d
    per_residue = length // PERM_ROWS

    class Rows:
        def __init__(self, start):
            self.index = (pl.ds(pl.multiple_of(start, PERM_ROWS), PERM_ROWS) if d == 1
                          else pl.ds(start, PERM_ROWS, stride=d))

        def __call__(self, ref):
            return ref[self.index, :]

        def store(self, ref, value):
            ref[self.index, :] = value

    def step(n, carry):
        r, c = n // per_residue, n % per_residue
        fn(Rows(r + d * c * PERM_ROWS), pl.multiple_of(r * length + c * PERM_ROWS, PERM_ROWS))
        return carry

    lax.fori_loop(0, S // PERM_ROWS, step, 0)


def _attention_fwd(q, k, v, carried=None):
    def body(q_ref, k_ref, v_ref, o_ref, lse_ref, qb, kb, v0b, v1b, bias, op, lp, ob0, lb0, ob1, lb1, ob2, lb2):
        lo = _lane_lo()
        lo_f = lo.astype(F32)[0:1, :]
        hi_f = 1.0 - lo_f
        zero_pad = jnp.zeros((CHUNK, LANES), BF16)
        for buf in (qb, kb, v0b, v1b):
            buf[0:CHUNK, :] = zero_pad
        _store_band_bias(bias)
        outs = ((ob0, lb0), (ob1, lb1), (ob2, lb2))
        for (d, nb), (ob, lb) in zip(DILATIONS, outs):
            def build(orig, perm):
                dst = pl.ds(pl.multiple_of(perm + CHUNK, CHUNK), PERM_ROWS)
                qb[dst, :] = (orig(q_ref) * 0.125).astype(BF16)
                kb[dst, :] = orig(k_ref).astype(BF16)
                vs = orig(v_ref)
                v0b[dst, :] = (vs * lo_f + hi_f).astype(BF16)
                v1b[dst, :] = (vs * hi_f + lo_f).astype(BF16)

            _for_permuted_chunks(d, build)

            def block(b, carry, nb=nb):
                base = pl.multiple_of(b * CHUNK, CHUNK)
                add = bias[jnp.where(b % nb == 0, 1, 0)]
                qblk = qb[pl.ds(pl.multiple_of(base + CHUNK, CHUNK), CHUNK), :]
                kblk = kb[pl.ds(base, 2 * CHUNK), :]
                q2 = jnp.concatenate([jnp.where(lo, qblk, 0), jnp.where(lo, 0, qblk)], axis=0)
                s2 = _dot_nt(q2, kblk) + jnp.concatenate([add, add], axis=0)
                m2 = jnp.max(s2, axis=-1, keepdims=True)
                p2 = jnp.exp(s2 - m2).astype(BF16)
                pv, mx = [], []
                for head, vh in enumerate((v0b, v1b)):
                    rows = slice(head * CHUNK, (head + 1) * CHUNK)
                    pv.append(_dot(p2[rows, :], vh[pl.ds(base, 2 * CHUNK), :]))
                    mx.append(m2[rows, :])
                den = pltpu.roll(jnp.where(lo, pv[1], pv[0]), 64, 1)
                op[pl.ds(base, CHUNK), :] = jnp.where(lo, pv[0], pv[1]) / den
                lp[pl.ds(base, CHUNK), :] = jnp.where(lo, mx[0], mx[1]) + jnp.log(den)
                return carry

            lax.fori_loop(0, S // CHUNK, block, 0, unroll=ATTN_UNROLL)

            def restore(orig, perm, ob=ob, lb=lb):
                orig.store(ob, op[pl.ds(perm, PERM_ROWS), :])
                orig.store(lb, lp[pl.ds(perm, PERM_ROWS), :])

            _for_permuted_chunks(d, restore)

        def combine(t, carry):
            rows = pl.ds(pl.multiple_of(t * PERM_ROWS, PERM_ROWS), PERM_ROWS)
            l0, l1, l2 = lb0[rows, :], lb1[rows, :], lb2[rows, :]
            mx = jnp.maximum(jnp.maximum(l0, l1), l2)
            e0, e1, e2 = jnp.exp(l0 - mx), jnp.exp(l1 - mx), jnp.exp(l2 - mx)
            den = e0 + e1 + e2
            o_ref[rows, :] = (e0 * ob0[rows, :] + e1 * ob1[rows, :] + e2 * ob2[rows, :]) / den
            lse_ref[rows, :] = mx + jnp.log(den)
            return carry

        lax.fori_loop(0, S // PERM_ROWS, combine, 0)

    slab = pl.BlockSpec((S, LANES), lambda h: (0, h))
    out = jax.ShapeDtypeStruct((S, D_HALF), F32)
    padded = pltpu.VMEM((CHUNK + S, LANES), BF16)
    whole = pltpu.VMEM((S, LANES), F32)
    return _host_call(
        body, carried, name="attention_fwd", grid=(4,), out_shape=(out, out),
        in_specs=[slab, slab, slab], out_specs=(slab, slab),
        scratch_shapes=[padded] * 4 + [pltpu.VMEM((2, CHUNK, 2 * CHUNK), F32)] + [whole] * 8,
        params=_params(("arbitrary",), 56), args=(q, k, v))


def _mix_ln1_fwd(attn, gm, w_o, x, g1, b1):
    def body(a_ref, gm_ref, w_ref, x_ref, g_ref, b_ref, xhat_ref, rstd_ref, x1b_ref):
        mix = _dot(a_ref[...].astype(BF16), w_ref[0:D_HALF, :]) + _dot(gm_ref[...], w_ref[D_HALF:D, :])
        xhat, rstd = _ln_stats(ALPHA * x_ref[...] + mix)
        xhat_ref[...] = xhat
        rstd_ref[...] = rstd
        x1b_ref[...] = (xhat * g_ref[...] + b_ref[...]).astype(BF16)

    vec = _full_spec((1, D))
    return pl.pallas_call(
        body, name="mix_ln1_fwd", grid=(NT,),
        out_shape=(jax.ShapeDtypeStruct((S, D), F32), jax.ShapeDtypeStruct((S, 1), F32),
                   jax.ShapeDtypeStruct((S, D), BF16)),
        in_specs=[_row_spec(D_HALF), _row_spec(D_HALF), _full_spec((D, D)), _row_spec(D), vec, vec],
        out_specs=(_row_spec(D), pl.BlockSpec((TM, 1), _rows), _row_spec(D)),
        compiler_params=_params(("parallel",)),
    )(attn, gm, w_o, x, g1, b1)


def _ffn_up_fwd(x1b, w_a, w_b, conv_w8, conv_b, carried=None):
    def body(x_ref, wa_ref, wb_ref, cw_ref, cb_ref, ap_ref, a_ref, bl_ref, h_ref, carry):
        @pl.when(pl.program_id(1) == 0)
        def _():
            carry[...] = jnp.zeros_like(carry)

        xb = x_ref[...]
        ap = _dot_nt(xb, wa_ref[...])
        bl = _dot_nt(xb, wb_ref[...])
        row = lax.broadcasted_iota(jnp.int32, (TM, TN), 0)
        c6, c7 = carry[6:7, :], carry[7:8, :]
        m1 = jnp.where(row == 0, c7, pltpu.roll(ap, 1, 0))
        m2 = jnp.where(row == 0, c6, jnp.where(row == 1, c7, pltpu.roll(ap, 2, 0)))
        a = cb_ref[...] + cw_ref[0:1, :] * m2 + cw_ref[1:2, :] * m1 + cw_ref[2:3, :] * ap
        carry[...] = ap[TM - 8:TM, :]
        ap_ref[...] = ap.astype(BF16)
        a_ref[...] = a
        bl_ref[...] = bl.astype(BF16)
        h_ref[...] = (_gelu(a) * bl).astype(BF16)

    tile = pl.BlockSpec((TM, TN), lambda j, i: (i, j))
    wcol = pl.BlockSpec((TN, D), lambda j, i: (j, 0))
    ff = jax.ShapeDtypeStruct((S, D_FF), F32)
    ffb = jax.ShapeDtypeStruct((S, D_FF), BF16)
    return _host_call(
        body, carried, name="ffn_up_fwd", grid=(NJ, NT),
        out_shape=(ffb, ff, ffb, ffb),
        in_specs=[pl.BlockSpec((TM, D), lambda j, i: (i, 0)), wcol, wcol,
                  pl.BlockSpec((8, TN), lambda j, i: (0, j)), pl.BlockSpec((1, TN), lambda j, i: (0, j))],
        out_specs=(tile, tile, tile, tile),
        scratch_shapes=[pltpu.VMEM((8, TN), F32)],
        params=_params(("arbitrary", "arbitrary"), 56), args=(x1b, w_a, w_b, conv_w8, conv_b))


def _ffn_down_ln2_fwd(hff, w_down, xhat1, g1, b1):
    def body(h_ref, w_ref, xh_ref, g_ref, b_ref, xhat_ref, rstd_ref):
        x1 = xh_ref[...] * g_ref[...] + b_ref[...]
        xhat, rstd = _ln_stats(ALPHA * x1 + _dot(h_ref[...], w_ref[...]))
        xhat_ref[...] = xhat
        rstd_ref[...] = rstd

    vec = _full_spec((1, D))
    return pl.pallas_call(
        body, name="ffn_down_ln2_fwd", grid=(NT,),
        out_shape=(jax.ShapeDtypeStruct((S, D), F32), jax.ShapeDtypeStruct((S, 1), F32)),
        in_specs=[_row_spec(D_FF), _full_spec((D_FF, D)), _row_spec(D), vec, vec],
        out_specs=(_row_spec(D), pl.BlockSpec((TM, 1), _rows)),
        compiler_params=_params(("parallel",)),
    )(hff, w_down, xhat1, g1, b1)


def _tail_fwd_bwd(xhat2, rstd2, p, target, w_g, w_p, g2, b2, bg, g3, b3):
    def body(xh_ref, rs_ref, p_ref, t_ref, wg_ref, wp_ref, g2_ref, b2_ref, bg_ref, g3_ref, b3_ref,
             loss_ref, dy2_ref, dy2b_ref, gwg_ref, gwp_ref, vec_ref, dwg_ref, dwp_ref):
        @pl.when(pl.program_id(0) == 0)
        def _():
            loss_ref[...] = jnp.zeros_like(loss_ref)
            dwg_ref[...] = jnp.zeros_like(dwg_ref)
            dwp_ref[...] = jnp.zeros_like(dwp_ref)
            vec_ref[...] = jnp.zeros_like(vec_ref)

        xhat2_t = xh_ref[...]
        x2 = xhat2_t * g2_ref[...] + b2_ref[...]
        x2b = x2.astype(BF16)
        pb = p_ref[...].astype(BF16)
        gate = jax.nn.sigmoid(_dot(x2b, wg_ref[...]) + bg_ref[...])
        pin = _dot(pb, wp_ref[...])
        xhat3, rstd3 = _ln_stats(ALPHA * x2 + gate * pin)
        err = xhat3 * g3_ref[...] + b3_ref[...] - t_ref[...]
        loss_ref[...] += jnp.sum(jnp.mean(err * err, axis=-1, keepdims=True), axis=0, keepdims=True) * 0.5
        dout = err * (1.0 / D)
        dy3 = _ln_bwd(dout * g3_ref[...], xhat3, rstd3)
        dgp = dy3 * pin * gate * (1.0 - gate)
        dgpb = dgp.astype(BF16)
        dwg_ref[...] += _dot_tn(x2b, dgpb)
        dwp_ref[...] += _dot_tn(pb, (dy3 * gate).astype(BF16))
        dx2 = ALPHA * dy3 + _dot_nt(dgpb, wg_ref[...])
        dy2 = _ln_bwd(dx2 * g2_ref[...], xhat2_t, rs_ref[...])
        dy2_ref[...] = dy2
        dy2b_ref[...] = dy2.astype(BF16)
        vec_ref[0:1, :] += _colsum(dgp)
        vec_ref[1:2, :] += _colsum(dout * xhat3)
        vec_ref[2:3, :] += _colsum(dout)
        vec_ref[3:4, :] += _colsum(dx2 * xhat2_t)
        vec_ref[4:5, :] += _colsum(dx2)

        @pl.when(pl.program_id(0) == NT - 1)
        def _():
            for j in range(N_DEV):
                gwg_ref[_owner_slot(j)] = dwg_ref[LANES * j:LANES * (j + 1), :].astype(BF16)
                gwp_ref[_owner_slot(j)] = dwp_ref[:, LANES * j:LANES * (j + 1)].astype(BF16)

    vec = _full_spec((1, D))
    return pl.pallas_call(
        body, name="tail_fwd_bwd", grid=(NT,),
        out_shape=(jax.ShapeDtypeStruct((8, LANES), F32), jax.ShapeDtypeStruct((S, D), F32),
                   jax.ShapeDtypeStruct((S, D), BF16), jax.ShapeDtypeStruct((N_DEV, D // N_DEV, D), BF16),
                   jax.ShapeDtypeStruct((N_DEV, D_PLE, D // N_DEV), BF16), jax.ShapeDtypeStruct((8, D), F32)),
        in_specs=[_row_spec(D), pl.BlockSpec((TM, 1), _rows), _row_spec(D_PLE), _row_spec(D),
                  _full_spec((D, D)), _full_spec((D_PLE, D)), vec, vec, vec, vec, vec],
        out_specs=(_full_spec((8, LANES)), _row_spec(D), _row_spec(D), _full_spec((N_DEV, D // N_DEV, D)),
                   _full_spec((N_DEV, D_PLE, D // N_DEV)), _full_spec((8, D))),
        scratch_shapes=[pltpu.VMEM((D, D), F32), pltpu.VMEM((D_PLE, D), F32)],
        compiler_params=_params(("arbitrary",)),
    )(xhat2, rstd2, p, target, w_g, w_p, g2, b2, bg, g3, b3)


def _ffn_bwd_act(dy2b, w_down, a_pre, a, b_lin, hff, conv_w8):
    def body(dy_ref, wd_ref, ap_ref, a_ref, bl_ref, h_ref, cw_ref, dap_ref, dbl_ref, gwd_ref, dcw_ref, carry,
             dwd_ref):
        @pl.when(pl.program_id(1) == 0)
        def _():
            carry[...] = jnp.zeros_like(carry)
            dwd_ref[...] = jnp.zeros_like(dwd_ref)
            dcw_ref[...] = jnp.zeros_like(dcw_ref)

        dyb = dy_ref[...]
        dh = _dot_nt(dyb, wd_ref[...])
        av = a_ref[...]
        cdf = 0.5 * (1.0 + lax.erf(av * INV_SQRT2))
        dbl_ref[...] = (dh * (av * cdf)).astype(BF16)
        da = dh * bl_ref[...].astype(F32) * (cdf + av * (jnp.exp(-0.5 * av * av) * INV_SQRT_2PI))
        row = lax.broadcasted_iota(jnp.int32, (TM, TN), 0)
        c0, c1 = carry[0:1, :], carry[1:2, :]
        p1 = jnp.where(row == TM - 1, c0, pltpu.roll(da, TM - 1, 0))
        p2 = jnp.where(row == TM - 2, c0, jnp.where(row == TM - 1, c1, pltpu.roll(da, TM - 2, 0)))
        carry[...] = da[0:8, :]
        ap = ap_ref[...].astype(F32)
        dcw_ref[3:4, :] += _colsum(da)
        dcw_ref[0:1, :] += _colsum(ap * p2)
        dcw_ref[1:2, :] += _colsum(ap * p1)
        dcw_ref[2:3, :] += _colsum(ap * da)
        dap_ref[...] = (cw_ref[2:3, :] * da + cw_ref[1:2, :] * p1 + cw_ref[0:1, :] * p2).astype(BF16)
        dwd_ref[...] += _dot_tn(h_ref[...], dyb)

        @pl.when(pl.program_id(1) == NT - 1)
        def _():
            _store_owner_blocks(gwd_ref, dwd_ref, pl.program_id(0))

    rev_tile = pl.BlockSpec((TM, TN), lambda j, i: (NT - 1 - i, j))
    wrows = pl.BlockSpec((TN, D), lambda j, i: (j, 0))
    small = pl.BlockSpec((8, TN), lambda j, i: (0, j))
    ffb = jax.ShapeDtypeStruct((S, D_FF), BF16)
    blocks = (N_DEV, D_FF // N_DEV, D)
    return pl.pallas_call(
        body, name="ffn_bwd_act", grid=(NJ, NT),
        out_shape=(ffb, ffb, jax.ShapeDtypeStruct(blocks, BF16), jax.ShapeDtypeStruct((8, D_FF), F32)),
        in_specs=[pl.BlockSpec((TM, D), lambda j, i: (NT - 1 - i, 0)), wrows, rev_tile, rev_tile, rev_tile,
                  rev_tile, small],
        out_specs=(rev_tile, rev_tile, _full_spec(blocks), small),
        scratch_shapes=[pltpu.VMEM((8, TN), F32), pltpu.VMEM((TN, D), F32)],
        compiler_params=_params(("arbitrary", "arbitrary"), 56),
    )(dy2b, w_down, a_pre, a, b_lin, hff, conv_w8)


def _ffn_bwd_w(dap, dbl, x1b):
    def body(dap_ref, dbl_ref, x_ref, gwa_ref, gwb_ref, dwa_ref, dwb_ref):
        @pl.when(pl.program_id(1) == 0)
        def _():
            dwa_ref[...] = jnp.zeros_like(dwa_ref)
            dwb_ref[...] = jnp.zeros_like(dwb_ref)

        xb = x_ref[...]
        dwa_ref[...] += _dot_tn(dap_ref[...], xb)
        dwb_ref[...] += _dot_tn(dbl_ref[...], xb)

        @pl.when(pl.program_id(1) == NT - 1)
        def _():
            _store_owner_blocks(gwa_ref, dwa_ref, pl.program_id(0))
            _store_owner_blocks(gwb_ref, dwb_ref, pl.program_id(0))

    tile = pl.BlockSpec((TM, TN), lambda j, i: (i, j))
    blocks = (N_DEV, D_FF // N_DEV, D)
    out = jax.ShapeDtypeStruct(blocks, BF16)
    return pl.pallas_call(
        body, name="ffn_bwd_w", grid=(NJ, NT), out_shape=(out, out),
        in_specs=[tile, tile, pl.BlockSpec((TM, D), lambda j, i: (i, 0))],
        out_specs=(_full_spec(blocks), _full_spec(blocks)),
        scratch_shapes=[pltpu.VMEM((TN, D), F32), pltpu.VMEM((TN, D), F32)],
        compiler_params=_params(("arbitrary", "arbitrary"), 56),
    )(dap, dbl, x1b)


def _ffn_bwd_x(dap, dbl, w_a, w_b):
    def body(dap_ref, dbl_ref, wa_ref, wb_ref, dx_ref):
        dx_ref[...] = _dot(dap_ref[...], wa_ref[...]) + _dot(dbl_ref[...], wb_ref[...])

    return pl.pallas_call(
        body, name="ffn_bwd_x", grid=(NT,), out_shape=jax.ShapeDtypeStruct((S, D), F32),
        in_specs=[_row_spec(D_FF), _row_spec(D_FF), _full_spec((D_FF, D)), _full_spec((D_FF, D))],
        out_specs=_row_spec(D), compiler_params=_params(("parallel",), 56),
    )(dap, dbl, w_a, w_b)


def _ln1_mix_bwd(dy2, dx1_ffn, xhat1, rstd1, g1, attn, gm, w_o, carried=None):
    def body(dy2_ref, dxf_ref, xh_ref, rs_ref, g_ref, a_ref, gm_ref, w_ref,
             dy1_ref, da_ref, dlt_ref, dgm_ref, gwo_ref, vec_ref, dwo_ref):
        @pl.when(pl.program_id(0) == 0)
        def _():
            dwo_ref[...] = jnp.zeros_like(dwo_ref)
            vec_ref[...] = jnp.zeros_like(vec_ref)

        xhat = xh_ref[...]
        dx1 = ALPHA * dy2_ref[...] + dxf_ref[...]
        vec_ref[0:1, :] += _colsum(dx1 * xhat)
        vec_ref[1:2, :] += _colsum(dx1)
        dy1 = _ln_bwd(dx1 * g_ref[...], xhat, rs_ref[...])
        dy1_ref[...] = dy1
        dy1b = dy1.astype(BF16)
        dmix = _dot_nt(dy1b, w_ref[...])
        attn_t = a_ref[...]
        d_attn = dmix[:, 0:D_HALF]
        da_ref[...] = d_attn
        dgm_ref[...] = dmix[:, D_HALF:D]
        lo = (lax.broadcasted_iota(jnp.int32, (TM, LANES), 1) < 64)
        for s in range(4):
            sl = slice(s * LANES, (s + 1) * LANES)
            prod = d_attn[:, sl] * attn_t[:, sl]
            s0 = jnp.sum(jnp.where(lo, prod, 0.0), axis=-1, keepdims=True)
            s1 = jnp.sum(jnp.where(lo, 0.0, prod), axis=-1, keepdims=True)
            dlt_ref[:, sl] = jnp.where(lo, s0, s1)
        dwo_ref[0:D_HALF, :] += _dot_tn(attn_t.astype(BF16), dy1b)
        dwo_ref[D_HALF:D, :] += _dot_tn(gm_ref[...], dy1b)

        @pl.when(pl.program_id(0) == NT - 1)
        def _():
            for j in range(N_DEV):
                gwo_ref[_owner_slot(j)] = dwo_ref[LANES * j:LANES * (j + 1), :].astype(BF16)

    half = jax.ShapeDtypeStruct((S, D_HALF), F32)
    return _host_call(
        body, carried, name="ln1_mix_bwd", grid=(NT,),
        out_shape=(jax.ShapeDtypeStruct((S, D), F32), half, half, half,
                   jax.ShapeDtypeStruct((N_DEV, D // N_DEV, D), BF16), jax.ShapeDtypeStruct((8, D), F32)),
        in_specs=[_row_spec(D), _row_spec(D), _row_spec(D), pl.BlockSpec((TM, 1), _rows), _full_spec((1, D)),
                  _row_spec(D_HALF), _row_spec(D_HALF), _full_spec((D, D))],
        out_specs=(_row_spec(D), _row_spec(D_HALF), _row_spec(D_HALF), _row_spec(D_HALF),
                   _full_spec((N_DEV, D // N_DEV, D)), _full_spec((8, D))),
        scratch_shapes=[pltpu.VMEM((D, D), F32)],
        params=_params(("arbitrary",)), args=(dy2, dx1_ffn, xhat1, rstd1, g1, attn, gm, w_o))


def _gmlp_bwd(d_gm, u_pre, z_pre, ln_z_g, ln_z_b, w_s, bs_exp, carried=None):
    def body(dg_ref, u_ref, z_ref, g_ref, b_ref, ws_ref, bs_ref, du_ref, dz_ref, dws_ref, dbs_ref, vec_ref):
        @pl.when(pl.program_id(0) == 0)
        def _():
            dws_ref[...] = jnp.zeros_like(dws_ref)
            dbs_ref[...] = jnp.zeros_like(dbs_ref)
            vec_ref[...] = jnp.zeros_like(vec_ref)

        u_pre_t, z_pre_t, dgm = u_ref[...], z_ref[...], dg_ref[...]
        z_act, z_slope = _gelu_and_grad(z_pre_t)
        u_act, u_slope = _gelu_and_grad(u_pre_t)
        zhat, rstd = _ln_stats(z_act)
        zn = zhat * g_ref[...] + b_ref[...]
        wm = _masked_ws(ws_ref)
        mixed = _spatial_mix(zn, wm, bs_ref[...])
        du_ref[...] = (dgm * mixed * u_slope).astype(BF16)
        dmixed = dgm * u_act
        lo = _lane_lo()
        tril = _tril()
        group_of_lane = lax.broadcasted_iota(jnp.int32, (8, D_HALF), 1) // 64
        pick = (group_of_lane == lax.broadcasted_iota(jnp.int32, (8, D_HALF), 0)).astype(F32)
        dzn_rows = []
        for ch in range(TM // CHUNK):
            rows = slice(ch * CHUNK, (ch + 1) * CHUNK)
            dbs_ref[...] += lax.dot_general(pick, dmixed[rows, :], (((1,), (1,)), ((), ())),
                                            precision=lax.Precision.HIGHEST, preferred_element_type=F32)
            slabs = []
            for pr in range(4):
                sl = slice(pr * LANES, (pr + 1) * LANES)
                dm = dmixed[rows, sl]
                zp = zn[rows, sl].astype(BF16)
                dm_lo = jnp.where(lo, dm, 0.0).astype(BF16)
                dm_hi = jnp.where(lo, 0.0, dm).astype(BF16)
                dws_ref[2 * pr] += jnp.where(tril, _dot_nt(dm_lo, zp), 0.0)
                dws_ref[2 * pr + 1] += jnp.where(tril, _dot_nt(dm_hi, zp), 0.0)
                dmb = dm.astype(BF16)
                slabs.append(jnp.where(lo, _dot_tn(wm[2 * pr], dmb), _dot_tn(wm[2 * pr + 1], dmb)))
            dzn_rows.append(jnp.concatenate(slabs, axis=1))
        dzn = jnp.concatenate(dzn_rows, axis=0)
        vec_ref[0:1, :] += _colsum(dzn * zhat)
        vec_ref[1:2, :] += _colsum(dzn)
        dz = _ln_bwd(dzn * g_ref[...], zhat, rstd)
        dz_ref[...] = (dz * z_slope).astype(BF16)

    halfb = jax.ShapeDtypeStruct((S, D_HALF), BF16)
    vec = _full_spec((1, D_HALF))
    return _host_call(
        body, carried, name="gmlp_bwd", grid=(NT,),
        out_shape=(halfb, halfb, jax.ShapeDtypeStruct((8, CHUNK, CHUNK), F32),
                   jax.ShapeDtypeStruct((8, CHUNK), F32), jax.ShapeDtypeStruct((8, D_HALF), F32)),
        in_specs=[_row_spec(D_HALF), _row_spec(D_HALF), _row_spec(D_HALF), vec, vec,
                  _full_spec((8, CHUNK, CHUNK)), _full_spec((CHUNK, D_HALF))],
        out_specs=(_row_spec(D_HALF), _row_spec(D_HALF), _full_spec((8, CHUNK, CHUNK)),
                   _full_spec((8, CHUNK)), _full_spec((8, D_HALF))),
        scratch_shapes=[], params=_params(("arbitrary",)), args=(d_gm, u_pre, z_pre, ln_z_g, ln_z_b, w_s, bs_exp))


def _attention_bwd(q, k, v, lse, d_attn, delta, tabs, carried=None):
    def body(q_ref, k_ref, v_ref, l_ref, do_ref, dl_ref, c_ref, sa_ref, sb_ref, dq_ref, dk_ref, dv_ref,
             qb, kb, vb, gb, bias, lsp, dlp, dqp, dk_own, dk_prev, dv_own, dv_prev, dqa, dka, dva):
        lo = _lane_lo()
        zero_pad = jnp.zeros((CHUNK, LANES), BF16)
        for buf in (qb, kb, vb, gb):
            buf[0:CHUNK, :] = zero_pad
        dk_prev[S:S + CHUNK, :] = jnp.zeros((CHUNK, LANES), F32)
        dv_prev[S:S + CHUNK, :] = jnp.zeros((CHUNK, LANES), F32)
        _store_band_bias(bias)
        for d, nb in DILATIONS:
            def build(orig, perm):
                dst = pl.ds(pl.multiple_of(perm + CHUNK, CHUNK), PERM_ROWS)
                src = pl.ds(perm, PERM_ROWS)
                qb[dst, :] = (orig(q_ref) * 0.125).astype(BF16)
                kb[dst, :] = orig(k_ref).astype(BF16)
                vb[dst, :] = orig(v_ref).astype(BF16)
                gb[dst, :] = orig(do_ref).astype(BF16)
                lsp[src, :] = orig(l_ref)
                dlp[src, :] = orig(dl_ref)

            _for_permuted_chunks(d, build)

            def block(b, carry, nb=nb):
                base = pl.multiple_of(b * CHUNK, CHUNK)
                own = pl.multiple_of(base + CHUNK, CHUNK)
                add = bias[jnp.where(b % nb == 0, 1, 0)]
                qblk = qb[pl.ds(own, CHUNK), :]
                gblk = gb[pl.ds(own, CHUNK), :]
                kblk = kb[pl.ds(base, 2 * CHUNK), :]
                vblk = vb[pl.ds(base, 2 * CHUNK), :]
                lse_t = lsp[pl.ds(base, CHUNK), :]
                dlt_t = dlp[pl.ds(base, CHUNK), :]
                q2 = jnp.concatenate([jnp.where(lo, qblk, 0), jnp.where(lo, 0, qblk)], axis=0)
                g2 = jnp.concatenate([jnp.where(lo, gblk, 0), jnp.where(lo, 0, gblk)], axis=0)
                lse2 = jnp.concatenate([lse_t[:, 0:1], lse_t[:, 64:65]], axis=0)
                dlt2 = jnp.concatenate([dlt_t[:, 0:1], dlt_t[:, 64:65]], axis=0)
                add2 = jnp.concatenate([add, add], axis=0)
                p = jnp.exp(_dot_nt(q2, kblk) + add2 - lse2)
                ds = (p * (_dot_nt(g2, vblk) - dlt2)).astype(BF16)
                dv_blk = _dot_tn(p.astype(BF16), g2)
                dk_blk = _dot_tn(ds, q2)
                dq2 = _dot(ds, kblk)
                dqp[pl.ds(base, CHUNK), :] = jnp.where(lo, dq2[0:CHUNK, :], dq2[CHUNK:2 * CHUNK, :]) * 0.125
                dk_prev[pl.ds(base, CHUNK), :] = dk_blk[0:CHUNK, :]
                dk_own[pl.ds(own, CHUNK), :] = dk_blk[CHUNK:2 * CHUNK, :]
                dv_prev[pl.ds(base, CHUNK), :] = dv_blk[0:CHUNK, :]
                dv_own[pl.ds(own, CHUNK), :] = dv_blk[CHUNK:2 * CHUNK, :]
                return carry

            lax.fori_loop(0, S // CHUNK, block, 0, unroll=ATTN_UNROLL)

            def restore(orig, perm, first=(d == 1)):
                src = pl.ds(perm, PERM_ROWS)
                pad = pl.ds(pl.multiple_of(perm + CHUNK, CHUNK), PERM_ROWS)
                dq_new, dk_new, dv_new = dqp[src, :], dk_own[pad, :] + dk_prev[pad, :], dv_own[pad, :] + dv_prev[pad, :]
                if first:
                    orig.store(dqa, dq_new)
                    orig.store(dka, dk_new)
                    orig.store(dva, dv_new)
                else:
                    orig.store(dqa, orig(dqa) + dq_new)
                    orig.store(dka, orig(dka) + dk_new)
                    orig.store(dva, orig(dva) + dv_new)

            _for_permuted_chunks(d, restore)

        def finish(t, carry):
            rows = pl.ds(pl.multiple_of(t * PERM_ROWS, PERM_ROWS), PERM_ROWS)
            c, sa, sb = c_ref[rows, :], sa_ref[rows, :], sb_ref[rows, :]
            dq_ref[rows, :] = _rope_t(dqa[rows, :], c, sa, sb).astype(BF16)
            dk_ref[rows, :] = _rope_t(dka[rows, :], c, sa, sb).astype(BF16)
            dv_ref[rows, :] = dva[rows, :].astype(BF16)
            return carry

        lax.fori_loop(0, S // PERM_ROWS, finish, 0)

    slab = pl.BlockSpec((S, LANES), lambda h: (0, h), pipeline_mode=pl.Buffered(1))
    tab = pl.BlockSpec((S, LANES), lambda h: (0, 0), pipeline_mode=pl.Buffered(1))
    out_slab = pl.BlockSpec((S, LANES), lambda h: (0, h))
    out = jax.ShapeDtypeStruct((S, D_HALF), BF16)
    padded_b = pltpu.VMEM((CHUNK + S, LANES), BF16)
    padded_f = pltpu.VMEM((CHUNK + S, LANES), F32)
    whole = pltpu.VMEM((S, LANES), F32)
    return _host_call(
        body, carried, name="attention_bwd", grid=(4,), out_shape=(out, out, out),
        in_specs=[out_slab] * 3 + [slab] * 3 + [tab] * 3, out_specs=(out_slab,) * 3,
        scratch_shapes=[padded_b] * 4 + [pltpu.VMEM((2, CHUNK, 2 * CHUNK), F32)] + [whole] * 3
        + [padded_f] * 4 + [whole] * 3,
        params=_params(("arbitrary",), 60), args=(q, k, v, lse, d_attn, delta, *tabs))


def _proj_in_bwd_w(xb, parts, carried=None):
    def body(x_ref, p0, p1, p2, p3, p4, gw_ref, dw_ref):
        @pl.when(pl.program_id(0) == 0)
        def _():
            dw_ref[...] = jnp.zeros_like(dw_ref)

        xt = x_ref[...]
        for n, part in enumerate((p0, p1, p2, p3, p4)):
            dw_ref[n * D_HALF:(n + 1) * D_HALF, :] += _dot_tn(part[...], xt)

        @pl.when(pl.program_id(0) == NT - 1)
        def _():
            width = D_IN // N_DEV
            for j in range(N_DEV):
                gw_ref[_owner_slot(j)] = dw_ref[width * j:width * (j + 1), :].astype(BF16)

    return _host_call(
        body, carried, name="proj_in_bwd_w", grid=(NT,),
        out_shape=(jax.ShapeDtypeStruct((N_DEV, D_IN // N_DEV, D), BF16),),
        in_specs=[_row_spec(D)] + [_row_spec(D_HALF)] * 5, out_specs=(_full_spec((N_DEV, D_IN // N_DEV, D)),),
        scratch_shapes=[pltpu.VMEM((D_IN, D), F32)], params=_params(("arbitrary",)), args=(xb, *parts))


def _proj_in_bwd_x(dy1, parts, w_in, carried=None):
    def body(dy_ref, p0, p1, p2, p3, p4, w_ref, gx_ref):
        acc = ALPHA * dy_ref[...]
        for n, part in enumerate((p0, p1, p2, p3, p4)):
            acc += _dot(part[...], w_ref[n * D_HALF:(n + 1) * D_HALF, :])
        gx_ref[...] = acc

    return _host_call(
        body, carried, name="proj_in_bwd_x", grid=(NT,), out_shape=(jax.ShapeDtypeStruct((S, D), F32),),
        in_specs=[_row_spec(D)] + [_row_spec(D_HALF)] * 5 + [_full_spec((D_IN, D))], out_specs=(_row_spec(D),),
        scratch_shapes=[], params=_params(("arbitrary",)), args=(dy1, *parts, w_in))


def _to_natural(blocks, name):
    n, rows, w = blocks.shape
    tile = min(rows, 256)

    def body(i_ref, o_ref):
        o_ref[...] = jnp.concatenate([i_ref[j] for j in range(n)], axis=1)

    return pl.pallas_call(
        body, name=name, grid=(rows // tile,), out_shape=jax.ShapeDtypeStruct((rows, n * w), blocks.dtype),
        in_specs=[pl.BlockSpec((n, tile, w), lambda i: (0, i, 0))],
        out_specs=pl.BlockSpec((tile, n * w), lambda i: (i, 0)), compiler_params=_params(("parallel",)),
    )(blocks)


def _local_step(x, p, pos_col, target, sm, ex):
    bs_exp = jnp.repeat(sm["b_s"].T, 64, axis=1)
    tabs, got = _rope_tables(pos_col, ex.gather_input())
    w_in = ex.weight_input(got)
    q, k, v, u_pre, z_pre, gm, xb = _proj_in_fwd(x, w_in, tabs, sm["ln_z_g"], sm["ln_z_b"], sm["w_s"], bs_exp)
    (attn, lse), got = _attention_fwd(q, k, v, ex.gather_first())
    wa = ex.weights_first(got)
    xhat1, rstd1, x1b = _mix_ln1_fwd(attn, gm, wa["w_o"], x, sm["ln1_g"], sm["ln1_b"])
    (a_pre, a, b_lin, hff), got = _ffn_up_fwd(x1b, wa["w_ff_a"], wa["w_ff_b"], wa["conv_w8"], sm["conv_b"],
                                              ex.gather_second())
    wc = ex.weights_second(got)
    xhat2, rstd2 = _ffn_down_ln2_fwd(hff, wc["w_ff_down"], xhat1, sm["ln1_g"], sm["ln1_b"])
    loss, dy2, dy2b, dw_g, dw_p, vec_tail = _tail_fwd_bwd(
        xhat2, rstd2, p, target, wc["w_ple_gate"], wc["w_ple_in"], sm["ln2_g"], sm["ln2_b"],
        sm["b_ple_gate"], sm["ln3_g"], sm["ln3_b"])
    dap, dbl, dw_down, dconv = _ffn_bwd_act(dy2b, wc["w_ff_down"], a_pre, a, b_lin, hff, wa["conv_w8"])
    dw_a, dw_b = _ffn_bwd_w(dap, dbl, x1b)
    dx1_ffn = _ffn_bwd_x(dap, dbl, wa["w_ff_a"], wa["w_ff_b"])
    (dy1, d_attn, delta, d_gm, dw_o, vec_ln1), _ = _ln1_mix_bwd(
        dy2, dx1_ffn, xhat1, rstd1, sm["ln1_g"], attn, gm, wa["w_o"])
    early = {"w_ff_a": dw_a, "w_ff_b": dw_b, "w_ff_down": dw_down, "w_ple_gate": dw_g, "w_ple_in": dw_p,
             "w_o": dw_o}
    (du, dz, dws, dbs, vec_z), got = _gmlp_bwd(d_gm, u_pre, z_pre, sm["ln_z_g"], sm["ln_z_b"], sm["w_s"], bs_exp,
                                               ex.to_sibling(early))
    chip_sums = ex.reduce_on_chip(got)
    small = {"tail": vec_tail, "ln1": vec_ln1, "ln_z": vec_z, "conv": dconv, "w_s": dws, "b_s": dbs, "loss": loss}
    (dq, dk, dv), got_early = _attention_bwd(q, k, v, lse, d_attn, delta, tabs,
                                             ex.between_chips(chip_sums, small))
    parts = (dq, dk, dv, du, dz)
    (dw_in,), _ = _proj_in_bwd_w(xb, parts)
    (grad_x,), got_late = _proj_in_bwd_x(dy1, parts, w_in, ex.last(dw_in))
    return grad_x, ex.collect(got_early, got_late)


def _mesh_pos():
    return lax.axis_index("x"), lax.axis_index("y"), lax.axis_index("c")


def _cast_shards(shards):
    n = len(shards)

    def body(*refs):
        for a in range(n):
            refs[n + a][...] = refs[a][...].astype(BF16)

    whole = [_full_spec(s.shape) for s in shards]
    return pl.pallas_call(
        body, name="cast_shards", grid=(1,), out_shape=tuple(jax.ShapeDtypeStruct(s.shape, BF16) for s in shards),
        in_specs=whole, out_specs=tuple(whole), compiler_params=_params(("arbitrary",)),
    )(*shards)


class _GatherComm:
    def __init__(self, shards):
        n = len(shards)
        self.inputs = list(shards)
        self.out_shapes = [jax.ShapeDtypeStruct((N_DEV,) + s.shape, s.dtype) for s in shards]
        self.scratch = [pltpu.SemaphoreType.DMA((7 * n,)), pltpu.SemaphoreType.DMA((7 * n,)),
                        pltpu.SemaphoreType.DMA((n,))]

    def phases(self, x_refs, out_refs, sems):
        send_sems, recv_sems, local_sems = sems
        n_arr = len(x_refs)

        def where():
            x, y, c = _mesh_pos()
            return (x, y, c), (x, y, 1 - c), [(1 - x, y), (x, 1 - y), (1 - x, 1 - y)]

        def copy(a, n, block, to, from_shard=False):
            dst = out_refs[a].at[4 * block[0] + 2 * block[1] + block[2]]
            return pltpu.make_async_remote_copy(
                src_ref=x_refs[a] if from_shard else dst, dst_ref=dst, send_sem=send_sems.at[7 * a + n],
                recv_sem=recv_sems.at[7 * a + n], device_id=to, device_id_type=MESH)

        def local(a):
            x, y, c = _mesh_pos()
            return pltpu.make_async_copy(x_refs[a], out_refs[a].at[4 * x + 2 * y + c], local_sems.at[a])

        def start():
            me, sibling, chips = where()
            for a in range(n_arr):
                local(a).start()
                copy(a, 0, me, sibling, from_shard=True).start()
                for n, chip in enumerate(chips):
                    copy(a, 1 + n, me, (*chip, me[2]), from_shard=True).start()

        def forward():
            me, sibling, chips = where()
            for n, chip in enumerate(chips):
                for a in range(n_arr):
                    copy(a, 1 + n, (*chip, me[2]), me).wait_recv()
                    copy(a, 4 + n, (*chip, me[2]), sibling).start()

        def finish():
            me, sibling, chips = where()
            for a in range(n_arr):
                copy(a, 0, sibling, me).wait_recv()
                copy(a, 0, me, sibling, from_shard=True).wait_send()
                for n, chip in enumerate(chips):
                    copy(a, 4 + n, (*chip, 1 - me[2]), me).wait_recv()
                    copy(a, 1 + n, me, (*chip, me[2]), from_shard=True).wait_send()
                    copy(a, 4 + n, (*chip, me[2]), sibling).wait_send()
                local(a).wait()

        return {"start": start, "forward": forward, "finish": finish}


class _SiblingComm:
    def __init__(self, big):
        n = len(big)
        self.inputs = list(big)
        self.out_shapes = [jax.ShapeDtypeStruct(b.shape[1:], b.dtype) for b in big]
        self.scratch = [pltpu.SemaphoreType.DMA((n,)), pltpu.SemaphoreType.DMA((n,))]

    def phases(self, src, dst, sems):
        send_sems, recv_sems = sems

        def copies():
            x, y, c = _mesh_pos()
            return [pltpu.make_async_remote_copy(
                src_ref=src[a].at[1 - c], dst_ref=dst[a], send_sem=send_sems.at[a], recv_sem=recv_sems.at[a],
                device_id=(x, y, 1 - c), device_id_type=MESH) for a in range(len(src))]

        def start():
            for cp in copies():
                cp.start()

        def finish():
            for cp in copies():
                cp.wait()

        return {"start": start, "finish": finish}


class _ChipComm:
    def __init__(self, sums):
        n = len(sums)
        self.inputs = list(sums)
        self.out_shapes = [jax.ShapeDtypeStruct(s.shape, s.dtype) for s in sums]
        self.scratch = [pltpu.SemaphoreType.DMA((3 * n,)), pltpu.SemaphoreType.DMA((3 * n,)),
                        pltpu.SemaphoreType.DMA((n,))]

    def phases(self, src, dst, sems):
        send_sems, recv_sems, local_sems = sems

        def copies():
            x, y, c = _mesh_pos()
            my_chip = 2 * x + y
            out = [pltpu.make_async_copy(src[a].at[my_chip], dst[a].at[my_chip], local_sems.at[a])
                   for a in range(len(src))]
            for n, (px, py) in enumerate([(1 - x, y), (x, 1 - y), (1 - x, 1 - y)]):
                for a in range(len(src)):
                    out.append(pltpu.make_async_remote_copy(
                        src_ref=src[a].at[2 * px + py], dst_ref=dst[a].at[my_chip],
                        send_sem=send_sems.at[3 * a + n], recv_sem=recv_sems.at[3 * a + n],
                        device_id=(px, py, c), device_id_type=MESH))
            return out

        def start():
            for cp in copies():
                cp.start()

        def finish():
            for cp in copies():
                cp.wait()

        return {"start": start, "finish": finish}


class _ScatterComm:
    def __init__(self, blocks, small):
        self.n_big, self.n_small = len(blocks), len(small)
        n = self.n_big + self.n_small
        self.inputs = list(blocks) + list(small)
        self.out_shapes = ([jax.ShapeDtypeStruct(b.shape, b.dtype) for b in blocks]
                           + [jax.ShapeDtypeStruct((N_DEV,) + s.shape, s.dtype) for s in small])
        self.scratch = [pltpu.SemaphoreType.DMA((7 * n,)), pltpu.SemaphoreType.DMA((7 * n,)),
                        pltpu.SemaphoreType.DMA((n,))]

    def phases(self, src, dst, sems):
        send_sems, recv_sems, local_sems = sems
        n_big, n_all = self.n_big, self.n_big + self.n_small

        def source(a, core, chip):
            return src[a].at[core * 4 + chip] if a < n_big else src[a]

        def copies():
            x, y, c = _mesh_pos()
            me = 4 * x + 2 * y + c
            out = [pltpu.make_async_copy(source(a, c, 2 * x + y), dst[a].at[me], local_sems.at[a])
                   for a in range(n_all)]
            for flip in range(1, N_DEV):
                px = 1 - x if flip & 4 else x
                py = 1 - y if flip & 2 else y
                pc = 1 - c if flip & 1 else c
                for a in range(n_all):
                    n = 7 * a + flip - 1
                    out.append(pltpu.make_async_remote_copy(
                        src_ref=source(a, pc, 2 * px + py), dst_ref=dst[a].at[me], send_sem=send_sems.at[n],
                        recv_sem=recv_sems.at[n], device_id=(px, py, pc), device_id_type=MESH))
            return out

        def start():
            for cp in copies():
                cp.start()

        def finish():
            for cp in copies():
                cp.wait()

        return {"start": start, "finish": finish}


class _Both:
    def __init__(self, first, second):
        self.parts = (first, second)
        self.inputs = first.inputs + second.inputs
        self.out_shapes = first.out_shapes + second.out_shapes
        self.scratch = first.scratch + second.scratch

    def phases(self, src, dst, sems):
        a, b = self.parts
        pa = a.phases(src[:len(a.inputs)], dst[:len(a.out_shapes)], sems[:len(a.scratch)])
        pb = b.phases(src[len(a.inputs):], dst[len(a.out_shapes):], sems[len(a.scratch):])

        def both(name):
            def run():
                pa[name]()
                pb[name]()
            return run

        return {name: both(name) for name in pa}


def _host_call(body, carried, *, name, grid, out_shape, in_specs, out_specs, scratch_shapes, params, args):
    if carried is None:
        return pl.pallas_call(body, name=name, grid=grid, out_shape=tuple(out_shape), in_specs=list(in_specs),
                              out_specs=tuple(out_specs), scratch_shapes=list(scratch_shapes),
                              compiler_params=params)(*args), ()
    comm, when = carried
    n_in, n_out, n_scratch = len(in_specs), len(out_shape), len(scratch_shapes)
    k_in, k_out = len(comm.inputs), len(comm.out_shapes)

    def wrapped(*refs):
        bounds = np.cumsum([0, n_in, k_in, n_out, k_out, n_scratch])
        ins, c_in, outs, c_out, scr = (refs[bounds[i]:bounds[i + 1]] for i in range(5))
        phases = comm.phases(c_in, c_out, refs[bounds[5]:])
        for phase, cond in when("before"):
            pl.when(cond)(phases[phase])
        body(*ins, *outs, *scr)
        for phase, cond in when("after"):
            pl.when(cond)(phases[phase])

    anywhere = pl.BlockSpec(memory_space=pl.ANY)
    results = pl.pallas_call(
        wrapped, name=name, grid=grid, out_shape=tuple(out_shape) + tuple(comm.out_shapes),
        in_specs=list(in_specs) + [anywhere] * k_in, out_specs=tuple(out_specs) + (anywhere,) * k_out,
        scratch_shapes=list(scratch_shapes) + comm.scratch, compiler_params=params,
    )(*args, *comm.inputs)
    return results[:n_out], results[n_out:]


class _Exchanges:
    FIRST = ("w_o", "w_ff_a", "w_ff_b")
    SECOND = ("w_ff_down", "w_ple_gate", "w_ple_in")
    EARLY = ("w_ff_a", "w_ff_b", "w_ff_down", "w_ple_gate", "w_ple_in", "w_o")
    LATE = ("w_in",)

    def __init__(self, shards, conv_rows):
        self.shards, self.conv_rows = shards, conv_rows
        self.mode = {name: mode for name, _, mode in BIG}

    def _natural(self, name, blocks):
        n, r, c = blocks.shape
        return _to_natural(blocks, name + "_natural") if self.mode[name] == "cols" else blocks.reshape(n * r, c)

    def gather_input(self):
        def when(position):
            step = pl.program_id(0)
            if position == "before":
                return [("start", step == 0)]
            return [("forward", step == NT - 1), ("finish", step == NT - 1)]
        return _GatherComm([self.shards["w_in"]]), when

    def weight_input(self, got):
        return self._natural("w_in", got[0])

    def gather_first(self):
        comm = _GatherComm([self.shards[n] for n in self.FIRST] + [self.conv_rows])

        def when(position):
            step = pl.program_id(0)
            if position == "before":
                return [("start", step == 0), ("forward", step == 3)]
            return [("finish", step == 3)]
        return comm, when

    def weights_first(self, got):
        out = {name: self._natural(name, blocks) for name, blocks in zip(self.FIRST, got)}
        out["conv_w8"] = _to_natural(got[-1], "conv_w_natural")
        return out

    def gather_second(self):
        comm = _GatherComm([self.shards[n] for n in self.SECOND])

        def when(position):
            j, i = pl.program_id(0), pl.program_id(1)
            if position == "before":
                return [("start", (j == 0) & (i == 0)), ("forward", (j == NJ - 1) & (i == NT // 2))]
            return [("finish", (j == NJ - 1) & (i == NT - 1))]
        return comm, when

    def weights_second(self, got):
        return {name: self._natural(name, blocks) for name, blocks in zip(self.SECOND, got)}

    def to_sibling(self, early):
        self.by_core = [early[n].reshape((2, 4) + early[n].shape[1:]) for n in self.EARLY]
        return _SiblingComm(self.by_core), _first_and_last(NT)

    def reduce_on_chip(self, from_sibling):
        core = lax.axis_index("c").astype(jnp.int32).reshape(1)
        return _chip_reduce(self.by_core, from_sibling, core, "chip_reduce")

    def between_chips(self, chip_sums, small):
        self.small_keys = tuple(small)
        return _Both(_ChipComm(chip_sums), _ScatterComm([], [small[k] for k in self.small_keys])), _first_and_last(4)

    def last(self, dw_in):
        by_core = [dw_in.reshape((2, 4) + dw_in.shape[1:])]
        core = lax.axis_index("c").astype(jnp.int32).reshape(1)
        sums = _chip_reduce(by_core, _standalone(_SiblingComm(by_core), "w_in_grad_to_sibling"), core, "w_in_chip_reduce")
        return _ChipComm(sums), _first_and_last(NT)

    def collect(self, got_early, got_late):
        parts = dict(zip(self.EARLY, got_early[:len(self.EARLY)]))
        parts.update(zip(self.LATE, got_late))
        return parts, dict(zip(self.small_keys, got_early[len(self.EARLY):]))


def _first_and_last(n_steps):
    def when(position):
        step = pl.program_id(0)
        return [("start", step == 0)] if position == "before" else [("finish", step == n_steps - 1)]
    return when


def _standalone(comm, name):
    n_in = len(comm.inputs)

    def body(*refs):
        phases = comm.phases(refs[:n_in], refs[n_in:n_in + len(comm.out_shapes)], refs[n_in + len(comm.out_shapes):])
        phases["start"]()
        phases["finish"]()

    anywhere = pl.BlockSpec(memory_space=pl.ANY)
    return pl.pallas_call(
        body, name=name, out_shape=tuple(comm.out_shapes), in_specs=[anywhere] * n_in,
        out_specs=(anywhere,) * len(comm.out_shapes), scratch_shapes=comm.scratch,
    )(*comm.inputs)


def _chip_reduce(big, from_sibling, core, name):
    n = len(big)

    def body(core_ref, *refs):
        for a in range(n):
            mine, theirs, out = refs[a], refs[n + a], refs[2 * n + a]
            out[0] = (mine[0, 0].astype(F32) + theirs[0].astype(F32)).astype(BF16)

    def block(shape):
        return pl.BlockSpec((1,) + shape, lambda ch, core_ref: (ch, 0, 0))

    grid_spec = pltpu.PrefetchScalarGridSpec(
        num_scalar_prefetch=1, grid=(4,),
        in_specs=[pl.BlockSpec((1, 1) + b.shape[2:], lambda ch, core_ref: (core_ref[0], ch, 0, 0)) for b in big]
        + [block(b.shape[2:]) for b in big],
        out_specs=[block(b.shape[2:]) for b in big])
    return pl.pallas_call(
        body, name=name, grid_spec=grid_spec,
        out_shape=tuple(jax.ShapeDtypeStruct(b.shape[1:], BF16) for b in big),
        compiler_params=_params(("parallel",)),
    )(core, *big, *from_sibling)


def _adamw(g, w, m, v):
    nm = ADAM_B1 * m + (1.0 - ADAM_B1) * g
    nv = ADAM_B2 * v + (1.0 - ADAM_B2) * (g * g)
    m_hat = nm / (1.0 - ADAM_B1 ** ADAM_STEP)
    v_hat = nv / (1.0 - ADAM_B2 ** ADAM_STEP)
    return -ADAM_LR * (m_hat / (jnp.sqrt(v_hat) + ADAM_EPS) + ADAM_WD * w), nm, nv


def _adamw_sharded(parts, w, m, v, name):
    def body(p_ref, w_ref, m_ref, v_ref, g_ref, d_ref, nm_ref, nv_ref):
        g = p_ref[0].astype(F32)
        for s in range(1, parts.shape[0]):
            g = g + p_ref[s].astype(F32)
        delta, nm, nv = _adamw(g, w_ref[0], m_ref[0], v_ref[0])
        g_ref[0] = g
        d_ref[0] = delta
        nm_ref[0] = nm
        nv_ref[0] = nv

    n, r, c = parts.shape
    steps = 4 if r % 64 == 0 and r >= 512 else (2 if r % 32 == 0 and r >= 256 else 1)
    tile = pl.BlockSpec((1, r // steps, c), lambda i: (0, i, 0))
    return pl.pallas_call(
        body, name=name, grid=(steps,), out_shape=(jax.ShapeDtypeStruct(w.shape, F32),) * 4,
        in_specs=[pl.BlockSpec((n, r // steps, c), lambda i: (0, i, 0)), tile, tile, tile], out_specs=(tile,) * 4,
        compiler_params=_params(("parallel",)),
    )(parts, w, m, v)


REPLICATED = (("ln_z_g", "ln_z", 0), ("ln_z_b", "ln_z", 1), ("w_s", "w_s", None), ("b_s", "b_s", None),
              ("ln1_g", "ln1", 0), ("ln1_b", "ln1", 1), ("conv_w", "conv_mine", None), ("conv_b", "conv", 3),
              ("ln2_g", "tail", 3), ("ln2_b", "tail", 4), ("b_ple_gate", "tail", 0), ("ln3_g", "tail", 1),
              ("ln3_b", "tail", 2))
GATHERED = ("tail", "ln1", "ln_z", "conv", "w_s", "b_s", "loss", "conv_mine")


def _adamw_replicated(gathered, w, m, v):
    n_par = len(REPLICATED)

    def body(*refs):
        srcs = dict(zip(GATHERED, refs[:len(GATHERED)]))
        rest = refs[len(GATHERED):]
        w_refs, m_refs, v_refs = rest[:n_par], rest[n_par:2 * n_par], rest[2 * n_par:3 * n_par]
        outs = rest[3 * n_par:]
        loss_ref = outs[4 * n_par]
        sums = {}
        for key, ref in srcs.items():
            total = ref[0]
            for dev in range(1, N_DEV):
                total = total + ref[dev]
            sums[key] = total
        loss_ref[...] = sums["loss"]
        for n, (name, key, row) in enumerate(REPLICATED):
            if name == "conv_w":
                g = sums[key][0:3, :]
            elif row is None:
                g = sums[key]
            else:
                g = sums[key][row:row + 1, :]
            lead = len(w_refs[n].shape) - g.ndim
            idx = (0,) * lead + (Ellipsis,)
            delta, nm, nv = _adamw(g, w_refs[n][idx], m_refs[n][idx], v_refs[n][idx])
            for kind, val in enumerate((g, delta, nm, nv)):
                outs[kind * n_par + n][idx] = val

    names = [name for name, _, _ in REPLICATED]
    shapes = [jax.ShapeDtypeStruct(w[name].shape, F32) for name in names]
    args = [gathered[k] for k in GATHERED] + [w[n] for n in names] + [m[n] for n in names] + [v[n] for n in names]
    out_shape = tuple(shapes * 4) + (jax.ShapeDtypeStruct((8, LANES), F32),)
    return pl.pallas_call(
        body, name="adamw_replicated", grid=(1,), out_shape=out_shape,
        in_specs=[_full_spec(a.shape) for a in args], out_specs=tuple(_full_spec(s.shape) for s in out_shape),
        compiler_params=_params(("arbitrary",)),
    )(*args)


def kernel(x, p, positions, w_in, ln_z_g, ln_z_b, w_s, b_s, w_o, ln1_g, ln1_b, w_ff_a, w_ff_b, conv_w, conv_b, w_ff_down, ln2_g, ln2_b, w_ple_gate, b_ple_gate, w_ple_in, ln3_g, ln3_b, loss_target, m_w_in, m_ln_z_g, m_ln_z_b, m_w_s, m_b_s, m_w_o, m_ln1_g, m_ln1_b, m_w_ff_a, m_w_ff_b, m_conv_w, m_conv_b, m_w_ff_down, m_ln2_g, m_ln2_b, m_w_ple_gate, m_b_ple_gate, m_w_ple_in, m_ln3_g, m_ln3_b, v_w_in, v_ln_z_g, v_ln_z_b, v_w_s, v_b_s, v_w_o, v_ln1_g, v_ln1_b, v_w_ff_a, v_w_ff_b, v_conv_w, v_conv_b, v_w_ff_down, v_ln2_g, v_ln2_b, v_w_ple_gate, v_b_ple_gate, v_w_ple_in, v_ln3_g, v_ln3_b):
    w = dict(w_in=w_in, ln_z_g=ln_z_g, ln_z_b=ln_z_b, w_s=w_s, b_s=b_s, w_o=w_o, ln1_g=ln1_g, ln1_b=ln1_b,
             w_ff_a=w_ff_a, w_ff_b=w_ff_b, conv_w=conv_w, conv_b=conv_b, w_ff_down=w_ff_down, ln2_g=ln2_g,
             ln2_b=ln2_b, w_ple_gate=w_ple_gate, b_ple_gate=b_ple_gate, w_ple_in=w_ple_in, ln3_g=ln3_g,
             ln3_b=ln3_b)
    m = dict(w_in=m_w_in, ln_z_g=m_ln_z_g, ln_z_b=m_ln_z_b, w_s=m_w_s, b_s=m_b_s, w_o=m_w_o, ln1_g=m_ln1_g,
             ln1_b=m_ln1_b, w_ff_a=m_w_ff_a, w_ff_b=m_w_ff_b, conv_w=m_conv_w, conv_b=m_conv_b,
             w_ff_down=m_w_ff_down, ln2_g=m_ln2_g, ln2_b=m_ln2_b, w_ple_gate=m_w_ple_gate,
             b_ple_gate=m_b_ple_gate, w_ple_in=m_w_ple_in, ln3_g=m_ln3_g, ln3_b=m_ln3_b)
    v = dict(w_in=v_w_in, ln_z_g=v_ln_z_g, ln_z_b=v_ln_z_b, w_s=v_w_s, b_s=v_b_s, w_o=v_w_o, ln1_g=v_ln1_g,
             ln1_b=v_ln1_b, w_ff_a=v_w_ff_a, w_ff_b=v_w_ff_b, conv_w=v_conv_w, conv_b=v_conv_b,
             w_ff_down=v_w_ff_down, ln2_g=v_ln2_g, ln2_b=v_ln2_b, w_ple_gate=v_w_ple_gate,
             b_ple_gate=v_b_ple_gate, w_ple_in=v_w_ple_in, ln3_g=v_ln3_g, ln3_b=v_ln3_b)
    big_names = [name for name, _, _ in BIG]
    small_names = ("ln_z_g", "ln_z_b", "w_s", "b_s", "ln1_g", "ln1_b", "conv_b", "ln2_g", "ln2_b", "b_ple_gate",
                   "ln3_g", "ln3_b")

    transposed = {name for name, _, mode in BIG if mode == "rows_t"}

    def travel(a, name):
        return jnp.swapaxes(a, 1, 2) if name in transposed else a

    shards = dict(zip(big_names, _cast_shards([travel(w[n], n)[0] for n in big_names])))
    conv_rows = jnp.pad(w["conv_w"][0], ((0, 5), (0, 0)))
    sm = {n: w[n][0] if w[n].ndim > 2 else w[n] for n in small_names}
    pos_col = positions.reshape(S, 1).astype(F32)
    grad_x, (parts, small_all) = _local_step(x[0], p[0, 0], pos_col, loss_target[0], sm,
                                             _Exchanges(shards, conv_rows))
    me = 4 * lax.axis_index("x") + 2 * lax.axis_index("y") + lax.axis_index("c")
    conv_cols = small_all["conv"].reshape(N_DEV, 8, N_DEV, D_FF // N_DEV)
    small_all["conv_mine"] = lax.dynamic_index_in_dim(conv_cols, me, axis=2, keepdims=False)

    leaves = {}
    for name in big_names:
        outs = _adamw_sharded(parts[name], travel(w[name], name), travel(m[name], name), travel(v[name], name),
                              "adamw_" + name)
        leaves[name] = tuple(travel(o, name) for o in outs)
    rep = _adamw_replicated(small_all, w, m, v)
    n_rep = len(REPLICATED)
    for n, (name, _, _) in enumerate(REPLICATED):
        leaves[name] = tuple(rep[kind * n_rep + n] for kind in range(4))
    loss = rep[4 * n_rep][0, 0]
    return (loss, grad_x[None], *[leaves[n][kind] for kind in range(4) for n in WEIGHT_ORDER])
```

```python
import math

import numpy as np
import jax
import jax.numpy as jnp
from jax import lax
from jax.experimental import pallas as pl
from jax.experimental.pallas import tpu as pltpu

F32 = jnp.float32
BF16 = jnp.bfloat16
MESH = pl.DeviceIdType.MESH

N_DEV = 8
S = 4096
D = 1024
D_HALF = 512
D_IN = 2560
D_FF = 2816
D_PLE = 256
CHUNK = 128
DILATIONS = ((1, 32), (4, 8), (16, 2))
ROPE_THETA = 500000.0
LN_EPS = 1e-5
ALPHA = 2.0 ** 0.25
NEG_INF = -1e30
INV_SQRT2 = 1.0 / math.sqrt(2.0)
INV_SQRT_2PI = 1.0 / math.sqrt(2.0 * math.pi)

ADAM_LR, ADAM_B1, ADAM_B2, ADAM_EPS, ADAM_WD, ADAM_STEP = 0.001, 0.9, 0.999, 1e-08, 0.01, 10

TM = 512
NT = S // TM
ATTN_UNROLL = 8
TN = 1408
NJ = D_FF // TN
LANES = 128
VMEM_MIB = 1024 * 1024

BIG = (("w_in", (320, 1024), "rows_t"), ("w_o", (128, 1024), "rows"), ("w_ff_a", (352, 1024), "rows_t"),
       ("w_ff_b", (352, 1024), "rows_t"), ("w_ff_down", (352, 1024), "rows"), ("w_ple_gate", (128, 1024), "rows"),
       ("w_ple_in", (256, 128), "cols"))
WEIGHT_ORDER = ("w_in", "ln_z_g", "ln_z_b", "w_s", "b_s", "w_o", "ln1_g", "ln1_b", "w_ff_a", "w_ff_b",
                "conv_w", "conv_b", "w_ff_down", "ln2_g", "ln2_b", "w_ple_gate", "b_ple_gate",
                "w_ple_in", "ln3_g", "ln3_b")


def _params(semantics=None, vmem_mib=48):
    return pltpu.CompilerParams(dimension_semantics=semantics, vmem_limit_bytes=vmem_mib * VMEM_MIB)


def _dot(a, b):
    return jnp.dot(a, b, preferred_element_type=F32)


def _dot_nt(a, b):
    return lax.dot_general(a, b, (((1,), (1,)), ((), ())), preferred_element_type=F32)


def _dot_tn(a, b):
    return lax.dot_general(a, b, (((0,), (0,)), ((), ())), preferred_element_type=F32)


def _gelu(x):
    return 0.5 * x * (1.0 + lax.erf(x * INV_SQRT2))


def _gelu_and_grad(x):
    cdf = 0.5 * (1.0 + lax.erf(x * INV_SQRT2))
    return x * cdf, cdf + x * (jnp.exp(-0.5 * x * x) * INV_SQRT_2PI)


def _ln_stats(y):
    mu = jnp.mean(y, axis=-1, keepdims=True)
    yc = y - mu
    var = jnp.mean(yc * yc, axis=-1, keepdims=True)
    rstd = lax.rsqrt(var + LN_EPS)
    return yc * rstd, rstd


def _ln_bwd(dxhat, xhat, rstd):
    m1 = jnp.mean(dxhat, axis=-1, keepdims=True)
    m2 = jnp.mean(dxhat * xhat, axis=-1, keepdims=True)
    return rstd * (dxhat - m1 - xhat * m2)


def _colsum(x):
    return jnp.sum(x, axis=0, keepdims=True)


def _rows(i):
    return (i, 0)


def _row_spec(width):
    return pl.BlockSpec((TM, width), _rows)


def _full_spec(shape):
    return pl.BlockSpec(shape, lambda *_: (0,) * len(shape))


def _owner_slot(j):
    return (j % 2) * 4 + j // 2


def _store_owner_blocks(blocks_ref, acc_ref, tile):
    rows = TN // 4
    for t in range(4):
        blocks_ref[(t % 2) * 4 + 2 * tile + t // 2] = acc_ref[rows * t:rows * (t + 1), :].astype(BF16)


def _lane_lo():
    return lax.broadcasted_iota(jnp.int32, (CHUNK, LANES), 1) < 64


def _tril():
    r = lax.broadcasted_iota(jnp.int32, (CHUNK, CHUNK), 0)
    c = lax.broadcasted_iota(jnp.int32, (CHUNK, CHUNK), 1)
    return c <= r


def _rope_consts():
    lane = np.arange(LANES) % 64
    j = lane % 8
    inv = np.where(lane < 16, np.float32(ROPE_THETA) ** (-(2.0 * j).astype(np.float32) / np.float32(16.0)), 0.0)
    m_lo = (lane < 8).astype(np.float32)
    m_hi = ((lane >= 8) & (lane < 16)).astype(np.float32)
    return (jnp.asarray(inv, F32).reshape(1, LANES), jnp.asarray(m_lo).reshape(1, LANES),
            jnp.asarray(m_hi).reshape(1, LANES))


def _rope_tables(pos_col, carried=None):
    inv, m_lo, m_hi = _rope_consts()

    def body(pos_ref, inv_ref, lo_ref, hi_ref, c_ref, sa_ref, sb_ref):
        ang = pos_ref[...] * inv_ref[...]
        c = jnp.cos(ang)
        s = jnp.sin(ang)
        lo = lo_ref[...]
        hi = hi_ref[...]
        c_ref[...] = jnp.where(lo + hi > 0.0, c, 1.0)
        sa_ref[...] = s * hi
        sb_ref[...] = -s * lo

    vec = _full_spec((1, LANES))
    out = jax.ShapeDtypeStruct((S, LANES), F32)
    return _host_call(
        body, carried, name="rope_tables", grid=(NT,), out_shape=(out, out, out),
        in_specs=[pl.BlockSpec((TM, 1), _rows), vec, vec, vec],
        out_specs=(_row_spec(LANES),) * 3, scratch_shapes=[], params=_params(("arbitrary",)),
        args=(pos_col, inv, m_lo, m_hi))


def _rope(t, c, sa, sb):
    return t * c + pltpu.roll(t, 8, 1) * sa + pltpu.roll(t, LANES - 8, 1) * sb


def _rope_t(dy, c, sa, sb):
    return dy * c + pltpu.roll(dy * sa, LANES - 8, 1) + pltpu.roll(dy * sb, 8, 1)


def _masked_ws(ws_ref):
    tril = _tril()
    return [jnp.where(tril, ws_ref[g], 0.0).astype(BF16) for g in range(8)]


def _spatial_mix(zn, wm, bs):
    lo = _lane_lo()
    rows = []
    for ch in range(TM // CHUNK):
        slabs = []
        for pr in range(4):
            zp = zn[ch * CHUNK:(ch + 1) * CHUNK, pr * LANES:(pr + 1) * LANES].astype(BF16)
            slabs.append(jnp.where(lo, _dot(wm[2 * pr], zp), _dot(wm[2 * pr + 1], zp)))
        rows.append(jnp.concatenate(slabs, axis=1) + bs)
    return jnp.concatenate(rows, axis=0)


def _proj_in_fwd(x, w_in, tabs, ln_z_g, ln_z_b, w_s, bs_exp):
    def body(x_ref, w_ref, c_ref, sa_ref, sb_ref, g_ref, b_ref, ws_ref, bs_ref,
             q_ref, k_ref, v_ref, u_ref, z_ref, gm_ref, xb_ref):
        xb = x_ref[...].astype(BF16)
        xb_ref[...] = xb
        c, sa, sb = c_ref[...], sa_ref[...], sb_ref[...]
        hq = _dot_nt(xb, w_ref[0:512, :])
        hk = _dot_nt(xb, w_ref[512:1024, :])
        for s in range(4):
            sl = slice(s * LANES, (s + 1) * LANES)
            q_ref[:, sl] = _rope(hq[:, sl], c, sa, sb)
            k_ref[:, sl] = _rope(hk[:, sl], c, sa, sb)
        v_ref[...] = _dot_nt(xb, w_ref[1024:1536, :])
        u_pre = _dot_nt(xb, w_ref[1536:2048, :])
        z_pre = _dot_nt(xb, w_ref[2048:2560, :])
        u_ref[...] = u_pre
        z_ref[...] = z_pre
        zhat, _ = _ln_stats(_gelu(z_pre))
        zn = zhat * g_ref[...] + b_ref[...]
        mixed = _spatial_mix(zn, _masked_ws(ws_ref), bs_ref[...])
        gm_ref[...] = (_gelu(u_pre) * mixed).astype(BF16)

    half = jax.ShapeDtypeStruct((S, D_HALF), F32)
    tab = _row_spec(LANES)
    return pl.pallas_call(
        body, name="proj_in_fwd", grid=(NT,),
        out_shape=(half, half, half, half, half, jax.ShapeDtypeStruct((S, D_HALF), BF16),
                   jax.ShapeDtypeStruct((S, D), BF16)),
        in_specs=[_row_spec(D), _full_spec((D_IN, D)), tab, tab, tab, _full_spec((1, D_HALF)),
                  _full_spec((1, D_HALF)), _full_spec((8, CHUNK, CHUNK)), _full_spec((CHUNK, D_HALF))],
        out_specs=(_row_spec(D_HALF),) * 6 + (_row_spec(D),),
        compiler_params=_params(("parallel",)),
    )(x, w_in, *tabs, ln_z_g, ln_z_b, w_s, bs_exp)


def _store_band_bias(bias_ref):
    qi = lax.broadcasted_iota(jnp.int32, (CHUNK, 2 * CHUNK), 0)
    kj = lax.broadcasted_iota(jnp.int32, (CHUNK, 2 * CHUNK), 1)
    band = (kj >= qi) & (kj <= qi + CHUNK)
    bias_ref[0] = jnp.where(band, 0.0, NEG_INF)
    bias_ref[1] = jnp.where(band & (kj >= CHUNK), 0.0, NEG_INF)


PERM_ROWS = 256


def _for_permuted_chunks(d, fn):
    length = S // d
    per_residue = length // PERM_ROWS

    class Rows:
        def __init__(self, start):
            self.index = (pl.ds(pl.multiple_of(start, PERM_ROWS), PERM_ROWS) if d == 1
                          else pl.ds(start, PERM_ROWS, stride=d))

        def __call__(self, ref):
            return ref[self.index, :]

        def store(self, ref, value):
            ref[self.index, :] = value

    def step(n, carry):
        r, c = n // per_residue, n % per_residue
        fn(Rows(r + d * c * PERM_ROWS), pl.multiple_of(r * length + c * PERM_ROWS, PERM_ROWS))
        return carry

    lax.fori_loop(0, S // PERM_ROWS, step, 0)


def _attention_fwd(q, k, v, carried=None):
    def body(q_ref, k_ref, v_ref, o_ref, lse_ref, qb, kb, v0b, v1b, bias, op, lp, ob0, lb0, ob1, lb1, ob2, lb2):
        lo = _lane_lo()
        lo_f = lo.astype(F32)[0:1, :]
        hi_f = 1.0 - lo_f
        zero_pad = jnp.zeros((CHUNK, LANES), BF16)
        for buf in (qb, kb, v0b, v1b):
            buf[0:CHUNK, :] = zero_pad
        _store_band_bias(bias)
        outs = ((ob0, lb0), (ob1, lb1), (ob2, lb2))
        for (d, nb), (ob, lb) in zip(DILATIONS, outs):
            def build(orig, perm):
                dst = pl.ds(pl.multiple_of(perm + CHUNK, CHUNK), PERM_ROWS)
                qb[dst, :] = (orig(q_ref) * 0.125).astype(BF16)
                kb[dst, :] = orig(k_ref).astype(BF16)
                vs = orig(v_ref)
                v0b[dst, :] = (vs * lo_f + hi_f).astype(BF16)
                v1b[dst, :] = (vs * hi_f + lo_f).astype(BF16)

            _for_permuted_chunks(d, build)

            def block(b, carry, nb=nb):
                base = pl.multiple_of(b * CHUNK, CHUNK)
                add = bias[jnp.where(b % nb == 0, 1, 0)]
                qblk = qb[pl.ds(pl.multiple_of(base + CHUNK, CHUNK), CHUNK), :]
                kblk = kb[pl.ds(base, 2 * CHUNK), :]
                q2 = jnp.concatenate([jnp.where(lo, qblk, 0), jnp.where(lo, 0, qblk)], axis=0)
                s2 = _dot_nt(q2, kblk) + jnp.concatenate([add, add], axis=0)
                m2 = jnp.max(s2, axis=-1, keepdims=True)
                p2 = jnp.exp(s2 - m2).astype(BF16)
                pv, mx = [], []
                for head, vh in enumerate((v0b, v1b)):
                    rows = slice(head * CHUNK, (head + 1) * CHUNK)
                    pv.append(_dot(p2[rows, :], vh[pl.ds(base, 2 * CHUNK), :]))
                    mx.append(m2[rows, :])
                den = pltpu.roll(jnp.where(lo, pv[1], pv[0]), 64, 1)
                op[pl.ds(base, CHUNK), :] = jnp.where(lo, pv[0], pv[1]) / den
                lp[pl.ds(base, CHUNK), :] = jnp.where(lo, mx[0], mx[1]) + jnp.log(den)
                return carry

            lax.fori_loop(0, S // CHUNK, block, 0, unroll=ATTN_UNROLL)

            def restore(orig, perm, ob=ob, lb=lb):
                orig.store(ob, op[pl.ds(perm, PERM_ROWS), :])
                orig.store(lb, lp[pl.ds(perm, PERM_ROWS), :])

            _for_permuted_chunks(d, restore)

        def combine(t, carry):
            rows = pl.ds(pl.multiple_of(t * PERM_ROWS, PERM_ROWS), PERM_ROWS)
            l0, l1, l2 = lb0[rows, :], lb1[rows, :], lb2[rows, :]
            mx = jnp.maximum(jnp.maximum(l0, l1), l2)
            e0, e1, e2 = jnp.exp(l0 - mx), jnp.exp(l1 - mx), jnp.exp(l2 - mx)
            den = e0 + e1 + e2
            o_ref[rows, :] = (e0 * ob0[rows, :] + e1 * ob1[rows, :] + e2 * ob2[rows, :]) / den
            lse_ref[rows, :] = mx + jnp.log(den)
            return carry

        lax.fori_loop(0, S // PERM_ROWS, combine, 0)

    slab = pl.BlockSpec((S, LANES), lambda h: (0, h))
    out = jax.ShapeDtypeStruct((S, D_HALF), F32)
    padded = pltpu.VMEM((CHUNK + S, LANES), BF16)
    whole = pltpu.VMEM((S, LANES), F32)
    return _host_call(
        body, carried, name="attention_fwd", grid=(4,), out_shape=(out, out),
        in_specs=[slab, slab, slab], out_specs=(slab, slab),
        scratch_shapes=[padded] * 4 + [pltpu.VMEM((2, CHUNK, 2 * CHUNK), F32)] + [whole] * 8,
        params=_params(("arbitrary",), 56), args=(q, k, v))


def _mix_ln1_fwd(attn, gm, w_o, x, g1, b1):
    def body(a_ref, gm_ref, w_ref, x_ref, g_ref, b_ref, xhat_ref, rstd_ref, x1b_ref):
        mix = _dot(a_ref[...].astype(BF16), w_ref[0:D_HALF, :]) + _dot(gm_ref[...], w_ref[D_HALF:D, :])
        xhat, rstd = _ln_stats(ALPHA * x_ref[...] + mix)
        xhat_ref[...] = xhat
        rstd_ref[...] = rstd
        x1b_ref[...] = (xhat * g_ref[...] + b_ref[...]).astype(BF16)

    vec = _full_spec((1, D))
    return pl.pallas_call(
        body, name="mix_ln1_fwd", grid=(NT,),
        out_shape=(jax.ShapeDtypeStruct((S, D), F32), jax.ShapeDtypeStruct((S, 1), F32),
                   jax.ShapeDtypeStruct((S, D), BF16)),
        in_specs=[_row_spec(D_HALF), _row_spec(D_HALF), _full_spec((D, D)), _row_spec(D), vec, vec],
        out_specs=(_row_spec(D), pl.BlockSpec((TM, 1), _rows), _row_spec(D)),
        compiler_params=_params(("parallel",)),
    )(attn, gm, w_o, x, g1, b1)


def _ffn_up_fwd(x1b, w_a, w_b, conv_w8, conv_b, carried=None):
    def body(x_ref, wa_ref, wb_ref, cw_ref, cb_ref, ap_ref, a_ref, bl_ref, h_ref, carry):
        @pl.when(pl.program_id(1) == 0)
        def _():
            carry[...] = jnp.zeros_like(carry)

        xb = x_ref[...]
        ap = _dot_nt(xb, wa_ref[...])
        bl = _dot_nt(xb, wb_ref[...])
        row = lax.broadcasted_iota(jnp.int32, (TM, TN), 0)
        c6, c7 = carry[6:7, :], carry[7:8, :]
        m1 = jnp.where(row == 0, c7, pltpu.roll(ap, 1, 0))
        m2 = jnp.where(row == 0, c6, jnp.where(row == 1, c7, pltpu.roll(ap, 2, 0)))
        a = cb_ref[...] + cw_ref[0:1, :] * m2 + cw_ref[1:2, :] * m1 + cw_ref[2:3, :] * ap
        carry[...] = ap[TM - 8:TM, :]
        ap_ref[...] = ap.astype(BF16)
        a_ref[...] = a
        bl_ref[...] = bl.astype(BF16)
        h_ref[...] = (_gelu(a) * bl).astype(BF16)

    tile = pl.BlockSpec((TM, TN), lambda j, i: (i, j))
    wcol = pl.BlockSpec((TN, D), lambda j, i: (j, 0))
    ff = jax.ShapeDtypeStruct((S, D_FF), F32)
    ffb = jax.ShapeDtypeStruct((S, D_FF), BF16)
    return _host_call(
        body, carried, name="ffn_up_fwd", grid=(NJ, NT),
        out_shape=(ffb, ff, ffb, ffb),
        in_specs=[pl.BlockSpec((TM, D), lambda j, i: (i, 0)), wcol, wcol,
                  pl.BlockSpec((8, TN), lambda j, i: (0, j)), pl.BlockSpec((1, TN), lambda j, i: (0, j))],
        out_specs=(tile, tile, tile, tile),
        scratch_shapes=[pltpu.VMEM((8, TN), F32)],
        params=_params(("arbitrary", "arbitrary"), 56), args=(x1b, w_a, w_b, conv_w8, conv_b))


def _ffn_down_ln2_fwd(hff, w_down, xhat1, g1, b1):
    def body(h_ref, w_ref, xh_ref, g_ref, b_ref, xhat_ref, rstd_ref):
        x1 = xh_ref[...] * g_ref[...] + b_ref[...]
        xhat, rstd = _ln_stats(ALPHA * x1 + _dot(h_ref[...], w_ref[...]))
        xhat_ref[...] = xhat
        rstd_ref[...] = rstd

    vec = _full_spec((1, D))
    return pl.pallas_call(
        body, name="ffn_down_ln2_fwd", grid=(NT,),
        out_shape=(jax.ShapeDtypeStruct((S, D), F32), jax.ShapeDtypeStruct((S, 1), F32)),
        in_specs=[_row_spec(D_FF), _full_spec((D_FF, D)), _row_spec(D), vec, vec],
        out_specs=(_row_spec(D), pl.BlockSpec((TM, 1), _rows)),
        compiler_params=_params(("parallel",)),
    )(hff, w_down, xhat1, g1, b1)


def _tail_fwd_bwd(xhat2, rstd2, p, target, w_g, w_p, g2, b2, bg, g3, b3):
    def body(xh_ref, rs_ref, p_ref, t_ref, wg_ref, wp_ref, g2_ref, b2_ref, bg_ref, g3_ref, b3_ref,
             loss_ref, dy2_ref, dy2b_ref, gwg_ref, gwp_ref, vec_ref, dwg_ref, dwp_ref):
        @pl.when(pl.program_id(0) == 0)
        def _():
            loss_ref[...] = jnp.zeros_like(loss_ref)
            dwg_ref[...] = jnp.zeros_like(dwg_ref)
            dwp_ref[...] = jnp.zeros_like(dwp_ref)
            vec_ref[...] = jnp.zeros_like(vec_ref)

        xhat2_t = xh_ref[...]
        x2 = xhat2_t * g2_ref[...] + b2_ref[...]
        x2b = x2.astype(BF16)
        pb = p_ref[...].astype(BF16)
        gate = jax.nn.sigmoid(_dot(x2b, wg_ref[...]) + bg_ref[...])
        pin = _dot(pb, wp_ref[...])
        xhat3, rstd3 = _ln_stats(ALPHA * x2 + gate * pin)
        err = xhat3 * g3_ref[...] + b3_ref[...] - t_ref[...]
        loss_ref[...] += jnp.sum(jnp.mean(err * err, axis=-1, keepdims=True), axis=0, keepdims=True) * 0.5
        dout = err * (1.0 / D)
        dy3 = _ln_bwd(dout * g3_ref[...], xhat3, rstd3)
        dgp = dy3 * pin * gate * (1.0 - gate)
        dgpb = dgp.astype(BF16)
        dwg_ref[...] += _dot_tn(x2b, dgpb)
        dwp_ref[...] += _dot_tn(pb, (dy3 * gate).astype(BF16))
        dx2 = ALPHA * dy3 + _dot_nt(dgpb, wg_ref[...])
        dy2 = _ln_bwd(dx2 * g2_ref[...], xhat2_t, rs_ref[...])
        dy2_ref[...] = dy2
        dy2b_ref[...] = dy2.astype(BF16)
        vec_ref[0:1, :] += _colsum(dgp)
        vec_ref[1:2, :] += _colsum(dout * xhat3)
        vec_ref[2:3, :] += _colsum(dout)
        vec_ref[3:4, :] += _colsum(dx2 * xhat2_t)
        vec_ref[4:5, :] += _colsum(dx2)

        @pl.when(pl.program_id(0) == NT - 1)
        def _():
            for j in range(N_DEV):
                gwg_ref[_owner_slot(j)] = dwg_ref[LANES * j:LANES * (j + 1), :].astype(BF16)
                gwp_ref[_owner_slot(j)] = dwp_ref[:, LANES * j:LANES * (j + 1)].astype(BF16)

    vec = _full_spec((1, D))
    return pl.pallas_call(
        body, name="tail_fwd_bwd", grid=(NT,),
        out_shape=(jax.ShapeDtypeStruct((8, LANES), F32), jax.ShapeDtypeStruct((S, D), F32),
                   jax.ShapeDtypeStruct((S, D), BF16), jax.ShapeDtypeStruct((N_DEV, D // N_DEV, D), BF16),
                   jax.ShapeDtypeStruct((N_DEV, D_PLE, D // N_DEV), BF16), jax.ShapeDtypeStruct((8, D), F32)),
        in_specs=[_row_spec(D), pl.BlockSpec((TM, 1), _rows), _row_spec(D_PLE), _row_spec(D),
                  _full_spec((D, D)), _full_spec((D_PLE, D)), vec, vec, vec, vec, vec],
        out_specs=(_full_spec((8, LANES)), _row_spec(D), _row_spec(D), _full_spec((N_DEV, D // N_DEV, D)),
                   _full_spec((N_DEV, D_PLE, D // N_DEV)), _full_spec((8, D))),
        scratch_shapes=[pltpu.VMEM((D, D), F32), pltpu.VMEM((D_PLE, D), F32)],
        compiler_params=_params(("arbitrary",)),
    )(xhat2, rstd2, p, target, w_g, w_p, g2, b2, bg, g3, b3)


def _ffn_bwd_act(dy2b, w_down, a_pre, a, b_lin, hff, conv_w8):
    def body(dy_ref, wd_ref, ap_ref, a_ref, bl_ref, h_ref, cw_ref, dap_ref, dbl_ref, gwd_ref, dcw_ref, carry,
             dwd_ref):
        @pl.when(pl.program_id(1) == 0)
        def _():
            carry[...] = jnp.zeros_like(carry)
            dwd_ref[...] = jnp.zeros_like(dwd_ref)
            dcw_ref[...] = jnp.zeros_like(dcw_ref)

        dyb = dy_ref[...]
        dh = _dot_nt(dyb, wd_ref[...])
        av = a_ref[...]
        cdf = 0.5 * (1.0 + lax.erf(av * INV_SQRT2))
        dbl_ref[...] = (dh * (av * cdf)).astype(BF16)
        da = dh * bl_ref[...].astype(F32) * (cdf + av * (jnp.exp(-0.5 * av * av) * INV_SQRT_2PI))
        row = lax.broadcasted_iota(jnp.int32, (TM, TN), 0)
        c0, c1 = carry[0:1, :], carry[1:2, :]
        p1 = jnp.where(row == TM - 1, c0, pltpu.roll(da, TM - 1, 0))
        p2 = jnp.where(row == TM - 2, c0, jnp.where(row == TM - 1, c1, pltpu.roll(da, TM - 2, 0)))
        carry[...] = da[0:8, :]
        ap = ap_ref[...].astype(F32)
        dcw_ref[3:4, :] += _colsum(da)
        dcw_ref[0:1, :] += _colsum(ap * p2)
        dcw_ref[1:2, :] += _colsum(ap * p1)
        dcw_ref[2:3, :] += _colsum(ap * da)
        dap_ref[...] = (cw_ref[2:3, :] * da + cw_ref[1:2, :] * p1 + cw_ref[0:1, :] * p2).astype(BF16)
        dwd_ref[...] += _dot_tn(h_ref[...], dyb)

        @pl.when(pl.program_id(1) == NT - 1)
        def _():
            _store_owner_blocks(gwd_ref, dwd_ref, pl.program_id(0))

    rev_tile = pl.BlockSpec((TM, TN), lambda j, i: (NT - 1 - i, j))
    wrows = pl.BlockSpec((TN, D), lambda j, i: (j, 0))
    small = pl.BlockSpec((8, TN), lambda j, i: (0, j))
    ffb = jax.ShapeDtypeStruct((S, D_FF), BF16)
    blocks = (N_DEV, D_FF // N_DEV, D)
    return pl.pallas_call(
        body, name="ffn_bwd_act", grid=(NJ, NT),
        out_shape=(ffb, ffb, jax.ShapeDtypeStruct(blocks, BF16), jax.ShapeDtypeStruct((8, D_FF), F32)),
        in_specs=[pl.BlockSpec((TM, D), lambda j, i: (NT - 1 - i, 0)), wrows, rev_tile, rev_tile, rev_tile,
                  rev_tile, small],
        out_specs=(rev_tile, rev_tile, _full_spec(blocks), small),
        scratch_shapes=[pltpu.VMEM((8, TN), F32), pltpu.VMEM((TN, D), F32)],
        compiler_params=_params(("arbitrary", "arbitrary"), 56),
    )(dy2b, w_down, a_pre, a, b_lin, hff, conv_w8)


def _ffn_bwd_w(dap, dbl, x1b):
    def body(dap_ref, dbl_ref, x_ref, gwa_ref, gwb_ref, dwa_ref, dwb_ref):
        @pl.when(pl.program_id(1) == 0)
        def _():
            dwa_ref[...] = jnp.zeros_like(dwa_ref)
            dwb_ref[...] = jnp.zeros_like(dwb_ref)

        xb = x_ref[...]
        dwa_ref[...] += _dot_tn(dap_ref[...], xb)
        dwb_ref[...] += _dot_tn(dbl_ref[...], xb)

        @pl.when(pl.program_id(1) == NT - 1)
        def _():
            _store_owner_blocks(gwa_ref, dwa_ref, pl.program_id(0))
            _store_owner_blocks(gwb_ref, dwb_ref, pl.program_id(0))

    tile = pl.BlockSpec((TM, TN), lambda j, i: (i, j))
    blocks = (N_DEV, D_FF // N_DEV, D)
    out = jax.ShapeDtypeStruct(blocks, BF16)
    return pl.pallas_call(
        body, name="ffn_bwd_w", grid=(NJ, NT), out_shape=(out, out),
        in_specs=[tile, tile, pl.BlockSpec((TM, D), lambda j, i: (i, 0))],
        out_specs=(_full_spec(blocks), _full_spec(blocks)),
        scratch_shapes=[pltpu.VMEM((TN, D), F32), pltpu.VMEM((TN, D), F32)],
        compiler_params=_params(("arbitrary", "arbitrary"), 56),
    )(dap, dbl, x1b)


def _ln1_mix_bwd(dy2, dap, dbl, w_a, w_b, xhat1, rstd1, g1, attn, gm, w_o, carried=None):
    def body(dy2_ref, dap_ref, dbl_ref, wa_ref, wb_ref, xh_ref, rs_ref, g_ref, a_ref, gm_ref, w_ref,
             dy1_ref, da_ref, dlt_ref, dgm_ref, gwo_ref, vec_ref, dwo_ref):
        @pl.when(pl.program_id(0) == 0)
        def _():
            dwo_ref[...] = jnp.zeros_like(dwo_ref)
            vec_ref[...] = jnp.zeros_like(vec_ref)

        xhat = xh_ref[...]
        dx1 = ALPHA * dy2_ref[...] + _dot(dap_ref[...], wa_ref[...]) + _dot(dbl_ref[...], wb_ref[...])
        vec_ref[0:1, :] += _colsum(dx1 * xhat)
        vec_ref[1:2, :] += _colsum(dx1)
        dy1 = _ln_bwd(dx1 * g_ref[...], xhat, rs_ref[...])
        dy1_ref[...] = dy1
        dy1b = dy1.astype(BF16)
        dmix = _dot_nt(dy1b, w_ref[...])
        attn_t = a_ref[...]
        d_attn = dmix[:, 0:D_HALF]
        da_ref[...] = d_attn
        dgm_ref[...] = dmix[:, D_HALF:D]
        lo = (lax.broadcasted_iota(jnp.int32, (TM, LANES), 1) < 64)
        for s in range(4):
            sl = slice(s * LANES, (s + 1) * LANES)
            prod = d_attn[:, sl] * attn_t[:, sl]
            s0 = jnp.sum(jnp.where(lo, prod, 0.0), axis=-1, keepdims=True)
            s1 = jnp.sum(jnp.where(lo, 0.0, prod), axis=-1, keepdims=True)
            dlt_ref[:, sl] = jnp.where(lo, s0, s1)
        dwo_ref[0:D_HALF, :] += _dot_tn(attn_t.astype(BF16), dy1b)
        dwo_ref[D_HALF:D, :] += _dot_tn(gm_ref[...], dy1b)

        @pl.when(pl.program_id(0) == NT - 1)
        def _():
            for j in range(N_DEV):
                gwo_ref[_owner_slot(j)] = dwo_ref[LANES * j:LANES * (j + 1), :].astype(BF16)

    half = jax.ShapeDtypeStruct((S, D_HALF), F32)
    return _host_call(
        body, carried, name="ln1_mix_bwd", grid=(NT,),
        out_shape=(jax.ShapeDtypeStruct((S, D), F32), half, half, half,
                   jax.ShapeDtypeStruct((N_DEV, D // N_DEV, D), BF16), jax.ShapeDtypeStruct((8, D), F32)),
        in_specs=[_row_spec(D), _row_spec(D_FF), _row_spec(D_FF), _full_spec((D_FF, D)), _full_spec((D_FF, D)),
                  _row_spec(D), pl.BlockSpec((TM, 1), _rows), _full_spec((1, D)),
                  _row_spec(D_HALF), _row_spec(D_HALF), _full_spec((D, D))],
        out_specs=(_row_spec(D), _row_spec(D_HALF), _row_spec(D_HALF), _row_spec(D_HALF),
                   _full_spec((N_DEV, D // N_DEV, D)), _full_spec((8, D))),
        scratch_shapes=[pltpu.VMEM((D, D), F32)],
        params=_params(("arbitrary",), 58),
        args=(dy2, dap, dbl, w_a, w_b, xhat1, rstd1, g1, attn, gm, w_o))


def _gmlp_bwd(d_gm, u_pre, z_pre, ln_z_g, ln_z_b, w_s, bs_exp, carried=None):
    def body(dg_ref, u_ref, z_ref, g_ref, b_ref, ws_ref, bs_ref, du_ref, dz_ref, dws_ref, dbs_ref, vec_ref):
        @pl.when(pl.program_id(0) == 0)
        def _():
            dws_ref[...] = jnp.zeros_like(dws_ref)
            dbs_ref[...] = jnp.zeros_like(dbs_ref)
            vec_ref[...] = jnp.zeros_like(vec_ref)

        u_pre_t, z_pre_t, dgm = u_ref[...], z_ref[...], dg_ref[...]
        z_act, z_slope = _gelu_and_grad(z_pre_t)
        u_act, u_slope = _gelu_and_grad(u_pre_t)
        zhat, rstd = _ln_stats(z_act)
        zn = zhat * g_ref[...] + b_ref[...]
        wm = _masked_ws(ws_ref)
        mixed = _spatial_mix(zn, wm, bs_ref[...])
        du_ref[...] = (dgm * mixed * u_slope).astype(BF16)
        dmixed = dgm * u_act
        lo = _lane_lo()
        tril = _tril()
        group_of_lane = lax.broadcasted_iota(jnp.int32, (8, D_HALF), 1) // 64
        pick = (group_of_lane == lax.broadcasted_iota(jnp.int32, (8, D_HALF), 0)).astype(F32)
        dzn_rows = []
        for ch in range(TM // CHUNK):
            rows = slice(ch * CHUNK, (ch + 1) * CHUNK)
            dbs_ref[...] += lax.dot_general(pick, dmixed[rows, :], (((1,), (1,)), ((), ())),
                                            precision=lax.Precision.HIGHEST, preferred_element_type=F32)
            slabs = []
            for pr in range(4):
                sl = slice(pr * LANES, (pr + 1) * LANES)
                dm = dmixed[rows, sl]
                zp = zn[rows, sl].astype(BF16)
                dm_lo = jnp.where(lo, dm, 0.0).astype(BF16)
                dm_hi = jnp.where(lo, 0.0, dm).astype(BF16)
                dws_ref[2 * pr] += jnp.where(tril, _dot_nt(dm_lo, zp), 0.0)
                dws_ref[2 * pr + 1] += jnp.where(tril, _dot_nt(dm_hi, zp), 0.0)
                dmb = dm.astype(BF16)
                slabs.append(jnp.where(lo, _dot_tn(wm[2 * pr], dmb), _dot_tn(wm[2 * pr + 1], dmb)))
            dzn_rows.append(jnp.concatenate(slabs, axis=1))
        dzn = jnp.concatenate(dzn_rows, axis=0)
        vec_ref[0:1, :] += _colsum(dzn * zhat)
        vec_ref[1:2, :] += _colsum(dzn)
        dz = _ln_bwd(dzn * g_ref[...], zhat, rstd)
        dz_ref[...] = (dz * z_slope).astype(BF16)

    halfb = jax.ShapeDtypeStruct((S, D_HALF), BF16)
    vec = _full_spec((1, D_HALF))
    return _host_call(
        body, carried, name="gmlp_bwd", grid=(NT,),
        out_shape=(halfb, halfb, jax.ShapeDtypeStruct((8, CHUNK, CHUNK), F32),
                   jax.ShapeDtypeStruct((8, CHUNK), F32), jax.ShapeDtypeStruct((8, D_HALF), F32)),
        in_specs=[_row_spec(D_HALF), _row_spec(D_HALF), _row_spec(D_HALF), vec, vec,
                  _full_spec((8, CHUNK, CHUNK)), _full_spec((CHUNK, D_HALF))],
        out_specs=(_row_spec(D_HALF), _row_spec(D_HALF), _full_spec((8, CHUNK, CHUNK)),
                   _full_spec((8, CHUNK)), _full_spec((8, D_HALF))),
        scratch_shapes=[], params=_params(("arbitrary",)), args=(d_gm, u_pre, z_pre, ln_z_g, ln_z_b, w_s, bs_exp))


def _attention_bwd(q, k, v, lse, d_attn, delta, tabs, carried=None):
    def body(q_ref, k_ref, v_ref, l_ref, do_ref, dl_ref, c_ref, sa_ref, sb_ref, dq_ref, dk_ref, dv_ref,
             qb, kb, vb, gb, bias, lsp, dlp, dqp, dk_own, dk_prev, dv_own, dv_prev, dqa, dka, dva):
        lo = _lane_lo()
        zero_pad = jnp.zeros((CHUNK, LANES), BF16)
        for buf in (qb, kb, vb, gb):
            buf[0:CHUNK, :] = zero_pad
        dk_prev[S:S + CHUNK, :] = jnp.zeros((CHUNK, LANES), F32)
        dv_prev[S:S + CHUNK, :] = jnp.zeros((CHUNK, LANES), F32)
        _store_band_bias(bias)
        for d, nb in DILATIONS:
            def build(orig, perm):
                dst = pl.ds(pl.multiple_of(perm + CHUNK, CHUNK), PERM_ROWS)
                src = pl.ds(perm, PERM_ROWS)
                qb[dst, :] = (orig(q_ref) * 0.125).astype(BF16)
                kb[dst, :] = orig(k_ref).astype(BF16)
                vb[dst, :] = orig(v_ref).astype(BF16)
                gb[dst, :] = orig(do_ref).astype(BF16)
                lsp[src, :] = orig(l_ref)
                dlp[src, :] = orig(dl_ref)

            _for_permuted_chunks(d, build)

            def block(b, carry, nb=nb):
                base = pl.multiple_of(b * CHUNK, CHUNK)
                own = pl.multiple_of(base + CHUNK, CHUNK)
                add = bias[jnp.where(b % nb == 0, 1, 0)]
                qblk = qb[pl.ds(own, CHUNK), :]
                gblk = gb[pl.ds(own, CHUNK), :]
                kblk = kb[pl.ds(base, 2 * CHUNK), :]
                vblk = vb[pl.ds(base, 2 * CHUNK), :]
                lse_t = lsp[pl.ds(base, CHUNK), :]
                dlt_t = dlp[pl.ds(base, CHUNK), :]
                q2 = jnp.concatenate([jnp.where(lo, qblk, 0), jnp.where(lo, 0, qblk)], axis=0)
                g2 = jnp.concatenate([jnp.where(lo, gblk, 0), jnp.where(lo, 0, gblk)], axis=0)
                lse2 = jnp.concatenate([lse_t[:, 0:1], lse_t[:, 64:65]], axis=0)
                dlt2 = jnp.concatenate([dlt_t[:, 0:1], dlt_t[:, 64:65]], axis=0)
                add2 = jnp.concatenate([add, add], axis=0)
                p = jnp.exp(_dot_nt(q2, kblk) + add2 - lse2)
                ds = (p * (_dot_nt(g2, vblk) - dlt2)).astype(BF16)
                dv_blk = _dot_tn(p.astype(BF16), g2)
                dk_blk = _dot_tn(ds, q2)
                dq2 = _dot(ds, kblk)
                dqp[pl.ds(base, CHUNK), :] = jnp.where(lo, dq2[0:CHUNK, :], dq2[CHUNK:2 * CHUNK, :]) * 0.125
                dk_prev[pl.ds(base, CHUNK), :] = dk_blk[0:CHUNK, :]
                dk_own[pl.ds(own, CHUNK), :] = dk_blk[CHUNK:2 * CHUNK, :]
                dv_prev[pl.ds(base, CHUNK), :] = dv_blk[0:CHUNK, :]
                dv_own[pl.ds(own, CHUNK), :] = dv_blk[CHUNK:2 * CHUNK, :]
                return carry

            lax.fori_loop(0, S // CHUNK, block, 0, unroll=ATTN_UNROLL)

            def restore(orig, perm, first=(d == 1)):
                src = pl.ds(perm, PERM_ROWS)
                pad = pl.ds(pl.multiple_of(perm + CHUNK, CHUNK), PERM_ROWS)
                dq_new, dk_new, dv_new = dqp[src, :], dk_own[pad, :] + dk_prev[pad, :], dv_own[pad, :] + dv_prev[pad, :]
                if first:
                    orig.store(dqa, dq_new)
                    orig.store(dka, dk_new)
                    orig.store(dva, dv_new)
                else:
                    orig.store(dqa, orig(dqa) + dq_new)
                    orig.store(dka, orig(dka) + dk_new)
                    orig.store(dva, orig(dva) + dv_new)

            _for_permuted_chunks(d, restore)

        def finish(t, carry):
            rows = pl.ds(pl.multiple_of(t * PERM_ROWS, PERM_ROWS), PERM_ROWS)
            c, sa, sb = c_ref[rows, :], sa_ref[rows, :], sb_ref[rows, :]
            dq_ref[rows, :] = _rope_t(dqa[rows, :], c, sa, sb).astype(BF16)
            dk_ref[rows, :] = _rope_t(dka[rows, :], c, sa, sb).astype(BF16)
            dv_ref[rows, :] = dva[rows, :].astype(BF16)
            return carry

        lax.fori_loop(0, S // PERM_ROWS, finish, 0)

    slab = pl.BlockSpec((S, LANES), lambda h: (0, h), pipeline_mode=pl.Buffered(1))
    tab = pl.BlockSpec((S, LANES), lambda h: (0, 0), pipeline_mode=pl.Buffered(1))
    out_slab = pl.BlockSpec((S, LANES), lambda h: (0, h))
    out = jax.ShapeDtypeStruct((S, D_HALF), BF16)
    padded_b = pltpu.VMEM((CHUNK + S, LANES), BF16)
    padded_f = pltpu.VMEM((CHUNK + S, LANES), F32)
    whole = pltpu.VMEM((S, LANES), F32)
    return _host_call(
        body, carried, name="attention_bwd", grid=(4,), out_shape=(out, out, out),
        in_specs=[out_slab] * 3 + [slab] * 3 + [tab] * 3, out_specs=(out_slab,) * 3,
        scratch_shapes=[padded_b] * 4 + [pltpu.VMEM((2, CHUNK, 2 * CHUNK), F32)] + [whole] * 3
        + [padded_f] * 4 + [whole] * 3,
        params=_params(("arbitrary",), 60), args=(q, k, v, lse, d_attn, delta, *tabs))


def _proj_in_bwd_w(xb, parts, carried=None):
    def body(x_ref, p0, p1, p2, p3, p4, gw_ref, dw_ref):
        @pl.when(pl.program_id(0) == 0)
        def _():
            dw_ref[...] = jnp.zeros_like(dw_ref)

        xt = x_ref[...]
        for n, part in enumerate((p0, p1, p2, p3, p4)):
            dw_ref[n * D_HALF:(n + 1) * D_HALF, :] += _dot_tn(part[...], xt)

        @pl.when(pl.program_id(0) == NT - 1)
        def _():
            width = D_IN // N_DEV
            for j in range(N_DEV):
                gw_ref[_owner_slot(j)] = dw_ref[width * j:width * (j + 1), :].astype(BF16)

    return _host_call(
        body, carried, name="proj_in_bwd_w", grid=(NT,),
        out_shape=(jax.ShapeDtypeStruct((N_DEV, D_IN // N_DEV, D), BF16),),
        in_specs=[_row_spec(D)] + [_row_spec(D_HALF)] * 5, out_specs=(_full_spec((N_DEV, D_IN // N_DEV, D)),),
        scratch_shapes=[pltpu.VMEM((D_IN, D), F32)], params=_params(("arbitrary",)), args=(xb, *parts))


def _proj_in_bwd_x(dy1, parts, w_in, carried=None):
    def body(dy_ref, p0, p1, p2, p3, p4, w_ref, gx_ref):
        acc = ALPHA * dy_ref[...]
        for n, part in enumerate((p0, p1, p2, p3, p4)):
            acc += _dot(part[...], w_ref[n * D_HALF:(n + 1) * D_HALF, :])
        gx_ref[...] = acc

    return _host_call(
        body, carried, name="proj_in_bwd_x", grid=(NT,), out_shape=(jax.ShapeDtypeStruct((S, D), F32),),
        in_specs=[_row_spec(D)] + [_row_spec(D_HALF)] * 5 + [_full_spec((D_IN, D))], out_specs=(_row_spec(D),),
        scratch_shapes=[], params=_params(("arbitrary",)), args=(dy1, *parts, w_in))


def _to_natural(blocks, name):
    n, rows, w = blocks.shape
    tile = min(rows, 256)

    def body(i_ref, o_ref):
        o_ref[...] = jnp.concatenate([i_ref[j] for j in range(n)], axis=1)

    return pl.pallas_call(
        body, name=name, grid=(rows // tile,), out_shape=jax.ShapeDtypeStruct((rows, n * w), blocks.dtype),
        in_specs=[pl.BlockSpec((n, tile, w), lambda i: (0, i, 0))],
        out_specs=pl.BlockSpec((tile, n * w), lambda i: (i, 0)), compiler_params=_params(("parallel",)),
    )(blocks)


def _local_step(x, p, pos_col, target, sm, ex):
    bs_exp = jnp.repeat(sm["b_s"].T, 64, axis=1)
    tabs, got = _rope_tables(pos_col, ex.gather_input())
    w_in = ex.weight_input(got)
    q, k, v, u_pre, z_pre, gm, xb = _proj_in_fwd(x, w_in, tabs, sm["ln_z_g"], sm["ln_z_b"], sm["w_s"], bs_exp)
    (attn, lse), got = _attention_fwd(q, k, v, ex.gather_first())
    wa = ex.weights_first(got)
    xhat1, rstd1, x1b = _mix_ln1_fwd(attn, gm, wa["w_o"], x, sm["ln1_g"], sm["ln1_b"])
    (a_pre, a, b_lin, hff), got = _ffn_up_fwd(x1b, wa["w_ff_a"], wa["w_ff_b"], wa["conv_w8"], sm["conv_b"],
                                              ex.gather_second())
    wc = ex.weights_second(got)
    xhat2, rstd2 = _ffn_down_ln2_fwd(hff, wc["w_ff_down"], xhat1, sm["ln1_g"], sm["ln1_b"])
    loss, dy2, dy2b, dw_g, dw_p, vec_tail = _tail_fwd_bwd(
        xhat2, rstd2, p, target, wc["w_ple_gate"], wc["w_ple_in"], sm["ln2_g"], sm["ln2_b"],
        sm["b_ple_gate"], sm["ln3_g"], sm["ln3_b"])
    dap, dbl, dw_down, dconv = _ffn_bwd_act(dy2b, wc["w_ff_down"], a_pre, a, b_lin, hff, wa["conv_w8"])
    dw_a, dw_b = _ffn_bwd_w(dap, dbl, x1b)
    (dy1, d_attn, delta, d_gm, dw_o, vec_ln1), _ = _ln1_mix_bwd(
        dy2, dap, dbl, wa["w_ff_a"], wa["w_ff_b"], xhat1, rstd1, sm["ln1_g"], attn, gm, wa["w_o"])
    early = {"w_ff_a": dw_a, "w_ff_b": dw_b, "w_ff_down": dw_down, "w_ple_gate": dw_g, "w_ple_in": dw_p,
             "w_o": dw_o}
    (du, dz, dws, dbs, vec_z), got = _gmlp_bwd(d_gm, u_pre, z_pre, sm["ln_z_g"], sm["ln_z_b"], sm["w_s"], bs_exp,
                                               ex.to_sibling(early))
    chip_sums = ex.reduce_on_chip(got)
    small = {"tail": vec_tail, "ln1": vec_ln1, "ln_z": vec_z, "conv": dconv, "w_s": dws, "b_s": dbs, "loss": loss}
    (dq, dk, dv), got_early = _attention_bwd(q, k, v, lse, d_attn, delta, tabs,
                                             ex.between_chips(chip_sums, small))
    parts = (dq, dk, dv, du, dz)
    (dw_in,), _ = _proj_in_bwd_w(xb, parts)
    (grad_x,), got_late = _proj_in_bwd_x(dy1, parts, w_in, ex.last(dw_in))
    return grad_x, ex.collect(got_early, got_late)


def _mesh_pos():
    return lax.axis_index("x"), lax.axis_index("y"), lax.axis_index("c")


def _cast_shards(shards):
    n = len(shards)

    def body(*refs):
        for a in range(n):
            refs[n + a][...] = refs[a][...].astype(BF16)

    whole = [_full_spec(s.shape) for s in shards]
    return pl.pallas_call(
        body, name="cast_shards", grid=(1,), out_shape=tuple(jax.ShapeDtypeStruct(s.shape, BF16) for s in shards),
        in_specs=whole, out_specs=tuple(whole), compiler_params=_params(("arbitrary",)),
    )(*shards)


class _GatherComm:
    def __init__(self, shards):
        n = len(shards)
        self.inputs = list(shards)
        self.out_shapes = [jax.ShapeDtypeStruct((N_DEV,) + s.shape, s.dtype) for s in shards]
        self.scratch = [pltpu.SemaphoreType.DMA((7 * n,)), pltpu.SemaphoreType.DMA((7 * n,)),
                        pltpu.SemaphoreType.DMA((n,))]

    def phases(self, x_refs, out_refs, sems):
        send_sems, recv_sems, local_sems = sems
        n_arr = len(x_refs)

        def where():
            x, y, c = _mesh_pos()
            return (x, y, c), (x, y, 1 - c), [(1 - x, y), (x, 1 - y), (1 - x, 1 - y)]

        def copy(a, n, block, to, from_shard=False):
            dst = out_refs[a].at[4 * block[0] + 2 * block[1] + block[2]]
            return pltpu.make_async_remote_copy(
                src_ref=x_refs[a] if from_shard else dst, dst_ref=dst, send_sem=send_sems.at[7 * a + n],
                recv_sem=recv_sems.at[7 * a + n], device_id=to, device_id_type=MESH)

        def local(a):
            x, y, c = _mesh_pos()
            return pltpu.make_async_copy(x_refs[a], out_refs[a].at[4 * x + 2 * y + c], local_sems.at[a])

        def start():
            me, sibling, chips = where()
            for a in range(n_arr):
                local(a).start()
                copy(a, 0, me, sibling, from_shard=True).start()
                for n, chip in enumerate(chips):
                    copy(a, 1 + n, me, (*chip, me[2]), from_shard=True).start()

        def forward():
            me, sibling, chips = where()
            for n, chip in enumerate(chips):
                for a in range(n_arr):
                    copy(a, 1 + n, (*chip, me[2]), me).wait_recv()
                    copy(a, 4 + n, (*chip, me[2]), sibling).start()

        def finish():
            me, sibling, chips = where()
            for a in range(n_arr):
                copy(a, 0, sibling, me).wait_recv()
                copy(a, 0, me, sibling, from_shard=True).wait_send()
                for n, chip in enumerate(chips):
                    copy(a, 4 + n, (*chip, 1 - me[2]), me).wait_recv()
                    copy(a, 1 + n, me, (*chip, me[2]), from_shard=True).wait_send()
                    copy(a, 4 + n, (*chip, me[2]), sibling).wait_send()
                local(a).wait()

        return {"start": start, "forward": forward, "finish": finish}


class _SiblingComm:
    def __init__(self, big):
        n = len(big)
        self.inputs = list(big)
        self.out_shapes = [jax.ShapeDtypeStruct(b.shape[1:], b.dtype) for b in big]
        self.scratch = [pltpu.SemaphoreType.DMA((n,)), pltpu.SemaphoreType.DMA((n,))]

    def phases(self, src, dst, sems):
        send_sems, recv_sems = sems

        def copies():
            x, y, c = _mesh_pos()
            return [pltpu.make_async_remote_copy(
                src_ref=src[a].at[1 - c], dst_ref=dst[a], send_sem=send_sems.at[a], recv_sem=recv_sems.at[a],
                device_id=(x, y, 1 - c), device_id_type=MESH) for a in range(len(src))]

        def start():
            for cp in copies():
                cp.start()

        def finish():
            for cp in copies():
                cp.wait()

        return {"start": start, "finish": finish}


class _ChipComm:
    def __init__(self, sums):
        n = len(sums)
        self.inputs = list(sums)
        self.out_shapes = [jax.ShapeDtypeStruct(s.shape, s.dtype) for s in sums]
        self.scratch = [pltpu.SemaphoreType.DMA((3 * n,)), pltpu.SemaphoreType.DMA((3 * n,)),
                        pltpu.SemaphoreType.DMA((n,))]

    def phases(self, src, dst, sems):
        send_sems, recv_sems, local_sems = sems

        def copies():
            x, y, c = _mesh_pos()
            my_chip = 2 * x + y
            out = [pltpu.make_async_copy(src[a].at[my_chip], dst[a].at[my_chip], local_sems.at[a])
                   for a in range(len(src))]
            for n, (px, py) in enumerate([(1 - x, y), (x, 1 - y), (1 - x, 1 - y)]):
                for a in range(len(src)):
                    out.append(pltpu.make_async_remote_copy(
                        src_ref=src[a].at[2 * px + py], dst_ref=dst[a].at[my_chip],
                        send_sem=send_sems.at[3 * a + n], recv_sem=recv_sems.at[3 * a + n],
                        device_id=(px, py, c), device_id_type=MESH))
            return out

        def start():
            for cp in copies():
                cp.start()

        def finish():
            for cp in copies():
                cp.wait()

        return {"start": start, "finish": finish}


class _ScatterComm:
    def __init__(self, blocks, small):
        self.n_big, self.n_small = len(blocks), len(small)
        n = self.n_big + self.n_small
        self.inputs = list(blocks) + list(small)
        self.out_shapes = ([jax.ShapeDtypeStruct(b.shape, b.dtype) for b in blocks]
                           + [jax.ShapeDtypeStruct((N_DEV,) + s.shape, s.dtype) for s in small])
        self.scratch = [pltpu.SemaphoreType.DMA((7 * n,)), pltpu.SemaphoreType.DMA((7 * n,)),
                        pltpu.SemaphoreType.DMA((n,))]

    def phases(self, src, dst, sems):
        send_sems, recv_sems, local_sems = sems
        n_big, n_all = self.n_big, self.n_big + self.n_small

        def source(a, core, chip):
            return src[a].at[core * 4 + chip] if a < n_big else src[a]

        def copies():
            x, y, c = _mesh_pos()
            me = 4 * x + 2 * y + c
            out = [pltpu.make_async_copy(source(a, c, 2 * x + y), dst[a].at[me], local_sems.at[a])
                   for a in range(n_all)]
            for flip in range(1, N_DEV):
                px = 1 - x if flip & 4 else x
                py = 1 - y if flip & 2 else y
                pc = 1 - c if flip & 1 else c
                for a in range(n_all):
                    n = 7 * a + flip - 1
                    out.append(pltpu.make_async_remote_copy(
                        src_ref=source(a, pc, 2 * px + py), dst_ref=dst[a].at[me], send_sem=send_sems.at[n],
                        recv_sem=recv_sems.at[n], device_id=(px, py, pc), device_id_type=MESH))
            return out

        def start():
            for cp in copies():
                cp.start()

        def finish():
            for cp in copies():
                cp.wait()

        return {"start": start, "finish": finish}


class _Both:
    def __init__(self, first, second):
        self.parts = (first, second)
        self.inputs = first.inputs + second.inputs
        self.out_shapes = first.out_shapes + second.out_shapes
        self.scratch = first.scratch + second.scratch

    def phases(self, src, dst, sems):
        a, b = self.parts
        pa = a.phases(src[:len(a.inputs)], dst[:len(a.out_shapes)], sems[:len(a.scratch)])
        pb = b.phases(src[len(a.inputs):], dst[len(a.out_shapes):], sems[len(a.scratch):])

        def both(name):
            def run():
                pa[name]()
                pb[name]()
            return run

        return {name: both(name) for name in pa}


def _host_call(body, carried, *, name, grid, out_shape, in_specs, out_specs, scratch_shapes, params, args):
    if carried is None:
        return pl.pallas_call(body, name=name, grid=grid, out_shape=tuple(out_shape), in_specs=list(in_specs),
                              out_specs=tuple(out_specs), scratch_shapes=list(scratch_shapes),
                              compiler_params=params)(*args), ()
    comm, when = carried
    n_in, n_out, n_scratch = len(in_specs), len(out_shape), len(scratch_shapes)
    k_in, k_out = len(comm.inputs), len(comm.out_shapes)

    def wrapped(*refs):
        bounds = np.cumsum([0, n_in, k_in, n_out, k_out, n_scratch])
        ins, c_in, outs, c_out, scr = (refs[bounds[i]:bounds[i + 1]] for i in range(5))
        phases = comm.phases(c_in, c_out, refs[bounds[5]:])
        for phase, cond in when("before"):
            pl.when(cond)(phases[phase])
        body(*ins, *outs, *scr)
        for phase, cond in when("after"):
            pl.when(cond)(phases[phase])

    anywhere = pl.BlockSpec(memory_space=pl.ANY)
    results = pl.pallas_call(
        wrapped, name=name, grid=grid, out_shape=tuple(out_shape) + tuple(comm.out_shapes),
        in_specs=list(in_specs) + [anywhere] * k_in, out_specs=tuple(out_specs) + (anywhere,) * k_out,
        scratch_shapes=list(scratch_shapes) + comm.scratch, compiler_params=params,
    )(*args, *comm.inputs)
    return results[:n_out], results[n_out:]


class _Exchanges:
    FIRST = ("w_o", "w_ff_a", "w_ff_b")
    SECOND = ("w_ff_down", "w_ple_gate", "w_ple_in")
    EARLY = ("w_ff_a", "w_ff_b", "w_ff_down", "w_ple_gate", "w_ple_in", "w_o")
    LATE = ("w_in",)

    def __init__(self, shards, conv_rows):
        self.shards, self.conv_rows = shards, conv_rows
        self.mode = {name: mode for name, _, mode in BIG}

    def _natural(self, name, blocks):
        n, r, c = blocks.shape
        return _to_natural(blocks, name + "_natural") if self.mode[name] == "cols" else blocks.reshape(n * r, c)

    def gather_input(self):
        def when(position):
            step = pl.program_id(0)
            if position == "before":
                return [("start", step == 0)]
            return [("forward", step == NT - 1), ("finish", step == NT - 1)]
        return _GatherComm([self.shards["w_in"]]), when

    def weight_input(self, got):
        return self._natural("w_in", got[0])

    def gather_first(self):
        comm = _GatherComm([self.shards[n] for n in self.FIRST] + [self.conv_rows])

        def when(position):
            step = pl.program_id(0)
            if position == "before":
                return [("start", step == 0), ("forward", step == 3)]
            return [("finish", step == 3)]
        return comm, when

    def weights_first(self, got):
        out = {name: self._natural(name, blocks) for name, blocks in zip(self.FIRST, got)}
        out["conv_w8"] = _to_natural(got[-1], "conv_w_natural")
        return out

    def gather_second(self):
        comm = _GatherComm([self.shards[n] for n in self.SECOND])

        def when(position):
            j, i = pl.program_id(0), pl.program_id(1)
            if position == "before":
                return [("start", (j == 0) & (i == 0)), ("forward", (j == NJ - 1) & (i == NT // 2))]
            return [("finish", (j == NJ - 1) & (i == NT - 1))]
        return comm, when

    def weights_second(self, got):
        return {name: self._natural(name, blocks) for name, blocks in zip(self.SECOND, got)}

    def to_sibling(self, early):
        self.by_core = [early[n].reshape((2, 4) + early[n].shape[1:]) for n in self.EARLY]
        return _SiblingComm(self.by_core), _first_and_last(NT)

    def reduce_on_chip(self, from_sibling):
        core = lax.axis_index("c").astype(jnp.int32).reshape(1)
        return _chip_reduce(self.by_core, from_sibling, core, "chip_reduce")

    def between_chips(self, chip_sums, small):
        self.small_keys = tuple(small)
        return _Both(_ChipComm(chip_sums), _ScatterComm([], [small[k] for k in self.small_keys])), _first_and_last(4)

    def last(self, dw_in):
        by_core = [dw_in.reshape((2, 4) + dw_in.shape[1:])]
        core = lax.axis_index("c").astype(jnp.int32).reshape(1)
        sums = _chip_reduce(by_core, _standalone(_SiblingComm(by_core), "w_in_grad_to_sibling"), core, "w_in_chip_reduce")
        return _ChipComm(sums), _first_and_last(NT)

    def collect(self, got_early, got_late):
        parts = dict(zip(self.EARLY, got_early[:len(self.EARLY)]))
        parts.update(zip(self.LATE, got_late))
        return parts, dict(zip(self.small_keys, got_early[len(self.EARLY):]))


def _first_and_last(n_steps):
    def when(position):
        step = pl.program_id(0)
        return [("start", step == 0)] if position == "before" else [("finish", step == n_steps - 1)]
    return when


def _standalone(comm, name):
    n_in = len(comm.inputs)

    def body(*refs):
        phases = comm.phases(refs[:n_in], refs[n_in:n_in + len(comm.out_shapes)], refs[n_in + len(comm.out_shapes):])
        phases["start"]()
        phases["finish"]()

    anywhere = pl.BlockSpec(memory_space=pl.ANY)
    return pl.pallas_call(
        body, name=name, out_shape=tuple(comm.out_shapes), in_specs=[anywhere] * n_in,
        out_specs=(anywhere,) * len(comm.out_shapes), scratch_shapes=comm.scratch,
    )(*comm.inputs)


def _chip_reduce(big, from_sibling, core, name):
    n = len(big)

    def body(core_ref, *refs):
        for a in range(n):
            mine, theirs, out = refs[a], refs[n + a], refs[2 * n + a]
            out[0] = (mine[0, 0].astype(F32) + theirs[0].astype(F32)).astype(BF16)

    def block(shape):
        return pl.BlockSpec((1,) + shape, lambda ch, core_ref: (ch, 0, 0))

    grid_spec = pltpu.PrefetchScalarGridSpec(
        num_scalar_prefetch=1, grid=(4,),
        in_specs=[pl.BlockSpec((1, 1) + b.shape[2:], lambda ch, core_ref: (core_ref[0], ch, 0, 0)) for b in big]
        + [block(b.shape[2:]) for b in big],
        out_specs=[block(b.shape[2:]) for b in big])
    return pl.pallas_call(
        body, name=name, grid_spec=grid_spec,
        out_shape=tuple(jax.ShapeDtypeStruct(b.shape[1:], BF16) for b in big),
        compiler_params=_params(("parallel",)),
    )(core, *big, *from_sibling)


def _adamw(g, w, m, v):
    nm = ADAM_B1 * m + (1.0 - ADAM_B1) * g
    nv = ADAM_B2 * v + (1.0 - ADAM_B2) * (g * g)
    m_hat = nm / (1.0 - ADAM_B1 ** ADAM_STEP)
    v_hat = nv / (1.0 - ADAM_B2 ** ADAM_STEP)
    return -ADAM_LR * (m_hat / (jnp.sqrt(v_hat) + ADAM_EPS) + ADAM_WD * w), nm, nv


def _adamw_sharded(parts, w, m, v, name):
    def body(p_ref, w_ref, m_ref, v_ref, g_ref, d_ref, nm_ref, nv_ref):
        g = p_ref[0].astype(F32)
        for s in range(1, parts.shape[0]):
            g = g + p_ref[s].astype(F32)
        delta, nm, nv = _adamw(g, w_ref[0], m_ref[0], v_ref[0])
        g_ref[0] = g
        d_ref[0] = delta
        nm_ref[0] = nm
        nv_ref[0] = nv

    n, r, c = parts.shape
    steps = 4 if r % 64 == 0 and r >= 512 else (2 if r % 32 == 0 and r >= 256 else 1)
    tile = pl.BlockSpec((1, r // steps, c), lambda i: (0, i, 0))
    return pl.pallas_call(
        body, name=name, grid=(steps,), out_shape=(jax.ShapeDtypeStruct(w.shape, F32),) * 4,
        in_specs=[pl.BlockSpec((n, r // steps, c), lambda i: (0, i, 0)), tile, tile, tile], out_specs=(tile,) * 4,
        compiler_params=_params(("parallel",)),
    )(parts, w, m, v)


REPLICATED = (("ln_z_g", "ln_z", 0), ("ln_z_b", "ln_z", 1), ("w_s", "w_s", None), ("b_s", "b_s", None),
              ("ln1_g", "ln1", 0), ("ln1_b", "ln1", 1), ("conv_w", "conv_mine", None), ("conv_b", "conv", 3),
              ("ln2_g", "tail", 3), ("ln2_b", "tail", 4), ("b_ple_gate", "tail", 0), ("ln3_g", "tail", 1),
              ("ln3_b", "tail", 2))
GATHERED = ("tail", "ln1", "ln_z", "conv", "w_s", "b_s", "loss", "conv_mine")


def _adamw_replicated(gathered, w, m, v):
    n_par = len(REPLICATED)

    def body(*refs):
        srcs = dict(zip(GATHERED, refs[:len(GATHERED)]))
        rest = refs[len(GATHERED):]
        w_refs, m_refs, v_refs = rest[:n_par], rest[n_par:2 * n_par], rest[2 * n_par:3 * n_par]
        outs = rest[3 * n_par:]
        loss_ref = outs[4 * n_par]
        sums = {}
        for key, ref in srcs.items():
            total = ref[0]
            for dev in range(1, N_DEV):
                total = total + ref[dev]
            sums[key] = total
        loss_ref[...] = sums["loss"]
        for n, (name, key, row) in enumerate(REPLICATED):
            if name == "conv_w":
                g = sums[key][0:3, :]
            elif row is None:
                g = sums[key]
            else:
                g = sums[key][row:row + 1, :]
            lead = len(w_refs[n].shape) - g.ndim
            idx = (0,) * lead + (Ellipsis,)
            delta, nm, nv = _adamw(g, w_refs[n][idx], m_refs[n][idx], v_refs[n][idx])
            for kind, val in enumerate((g, delta, nm, nv)):
                outs[kind * n_par + n][idx] = val

    names = [name for name, _, _ in REPLICATED]
    shapes = [jax.ShapeDtypeStruct(w[name].shape, F32) for name in names]
    args = [gathered[k] for k in GATHERED] + [w[n] for n in names] + [m[n] for n in names] + [v[n] for n in names]
    out_shape = tuple(shapes * 4) + (jax.ShapeDtypeStruct((8, LANES), F32),)
    return pl.pallas_call(
        body, name="adamw_replicated", grid=(1,), out_shape=out_shape,
        in_specs=[_full_spec(a.shape) for a in args], out_specs=tuple(_full_spec(s.shape) for s in out_shape),
        compiler_params=_params(("arbitrary",)),
    )(*args)


def kernel(x, p, positions, w_in, ln_z_g, ln_z_b, w_s, b_s, w_o, ln1_g, ln1_b, w_ff_a, w_ff_b, conv_w, conv_b, w_ff_down, ln2_g, ln2_b, w_ple_gate, b_ple_gate, w_ple_in, ln3_g, ln3_b, loss_target, m_w_in, m_ln_z_g, m_ln_z_b, m_w_s, m_b_s, m_w_o, m_ln1_g, m_ln1_b, m_w_ff_a, m_w_ff_b, m_conv_w, m_conv_b, m_w_ff_down, m_ln2_g, m_ln2_b, m_w_ple_gate, m_b_ple_gate, m_w_ple_in, m_ln3_g, m_ln3_b, v_w_in, v_ln_z_g, v_ln_z_b, v_w_s, v_b_s, v_w_o, v_ln1_g, v_ln1_b, v_w_ff_a, v_w_ff_b, v_conv_w, v_conv_b, v_w_ff_down, v_ln2_g, v_ln2_b, v_w_ple_gate, v_b_ple_gate, v_w_ple_in, v_ln3_g, v_ln3_b):
    w = dict(w_in=w_in, ln_z_g=ln_z_g, ln_z_b=ln_z_b, w_s=w_s, b_s=b_s, w_o=w_o, ln1_g=ln1_g, ln1_b=ln1_b,
             w_ff_a=w_ff_a, w_ff_b=w_ff_b, conv_w=conv_w, conv_b=conv_b, w_ff_down=w_ff_down, ln2_g=ln2_g,
             ln2_b=ln2_b, w_ple_gate=w_ple_gate, b_ple_gate=b_ple_gate, w_ple_in=w_ple_in, ln3_g=ln3_g,
             ln3_b=ln3_b)
    m = dict(w_in=m_w_in, ln_z_g=m_ln_z_g, ln_z_b=m_ln_z_b, w_s=m_w_s, b_s=m_b_s, w_o=m_w_o, ln1_g=m_ln1_g,
             ln1_b=m_ln1_b, w_ff_a=m_w_ff_a, w_ff_b=m_w_ff_b, conv_w=m_conv_w, conv_b=m_conv_b,
             w_ff_down=m_w_ff_down, ln2_g=m_ln2_g, ln2_b=m_ln2_b, w_ple_gate=m_w_ple_gate,
             b_ple_gate=m_b_ple_gate, w_ple_in=m_w_ple_in, ln3_g=m_ln3_g, ln3_b=m_ln3_b)
    v = dict(w_in=v_w_in, ln_z_g=v_ln_z_g, ln_z_b=v_ln_z_b, w_s=v_w_s, b_s=v_b_s, w_o=v_w_o, ln1_g=v_ln1_g,
             ln1_b=v_ln1_b, w_ff_a=v_w_ff_a, w_ff_b=v_w_ff_b, conv_w=v_conv_w, conv_b=v_conv_b,
             w_ff_down=v_w_ff_down, ln2_g=v_ln2_g, ln2_b=v_ln2_b, w_ple_gate=v_w_ple_gate,
             b_ple_gate=v_b_ple_gate, w_ple_in=v_w_ple_in, ln3_g=v_ln3_g, ln3_b=v_ln3_b)
    big_names = [name for name, _, _ in BIG]
    small_names = ("ln_z_g", "ln_z_b", "w_s", "b_s", "ln1_g", "ln1_b", "conv_b", "ln2_g", "ln2_b", "b_ple_gate",
                   "ln3_g", "ln3_b")

    transposed = {name for name, _, mode in BIG if mode == "rows_t"}

    def travel(a, name):
        return jnp.swapaxes(a, 1, 2) if name in transposed else a

    shards = dict(zip(big_names, _cast_shards([travel(w[n], n)[0] for n in big_names])))
    conv_rows = jnp.pad(w["conv_w"][0], ((0, 5), (0, 0)))
    sm = {n: w[n][0] if w[n].ndim > 2 else w[n] for n in small_names}
    pos_col = positions.reshape(S, 1).astype(F32)
    grad_x, (parts, small_all) = _local_step(x[0], p[0, 0], pos_col, loss_target[0], sm,
                                             _Exchanges(shards, conv_rows))
    me = 4 * lax.axis_index("x") + 2 * lax.axis_index("y") + lax.axis_index("c")
    conv_cols = small_all["conv"].reshape(N_DEV, 8, N_DEV, D_FF // N_DEV)
    small_all["conv_mine"] = lax.dynamic_index_in_dim(conv_cols, me, axis=2, keepdims=False)

    leaves = {}
    for name in big_names:
        outs = _adamw_sharded(parts[name], travel(w[name], name), travel(m[name], name), travel(v[name], name),
                              "adamw_" + name)
        leaves[name] = tuple(travel(o, name) for o in outs)
    rep = _adamw_replicated(small_all, w, m, v)
    n_rep = len(REPLICATED)
    for n, (name, _, _) in enumerate(REPLICATED):
        leaves[name] = tuple(rep[kind * n_rep + n] for kind in range(4))
    loss = rep[4 * n_rep][0, 0]
    return (loss, grad_x[None], *[leaves[n][kind] for kind in range(4) for n in WEIGHT_ORDER])
```
